```python
import jax, jax.numpy as jnp
from jax import lax
import numpy as np

D_MODEL = 1024
BATCH = 16
SEQ = 2048
DEPTH = 1

HEAD_DIM = 64
N_Q_HEADS = 16
N_KV_HEADS = 2
GQA_GROUP = N_Q_HEADS // N_KV_HEADS
WINDOW = 128
BLOCK = 128
Q_WIDTH = N_Q_HEADS * HEAD_DIM
KV_WIDTH = N_KV_HEADS * HEAD_DIM
CONV_CH = D_MODEL
CONV_WIDTH = 31
N_BRANCH = 2
IN_WIDTH = Q_WIDTH + 2 * KV_WIDTH + 2 * CONV_CH + N_BRANCH * D_MODEL
D_FF = 2816
FFN_RESIDUAL = 0.5
N_MOD = 9
EPS = 1e-6

kernel_name = "conditioned_hybrid_swa_conformer_macaron_layer"


def rmsnorm(x, g):
    xf = x.astype(jnp.float32)
    y = xf * lax.rsqrt(jnp.mean(xf * xf, axis=-1, keepdims=True) + EPS)
    return (y * g.astype(jnp.float32)).astype(x.dtype)


def layernorm(x, g, b):
    xf = x.astype(jnp.float32)
    mu = jnp.mean(xf, axis=-1, keepdims=True)
    var = jnp.mean(jnp.square(xf - mu), axis=-1, keepdims=True)
    y = (xf - mu) * lax.rsqrt(var + EPS)
    return (y * g.astype(jnp.float32) + b.astype(jnp.float32)).astype(x.dtype)


def modulate(h, shift, scale):
    return h * (1 + scale[:, None, :]) + shift[:, None, :]


def swiglu(h, w_gate, w_up, w_down):
    return (jax.nn.silu(h @ w_gate) * (h @ w_up)) @ w_down


def sliding_window_sink_attention(q, k, v, sinks):
    B, S = q.shape[0], q.shape[1]
    nb = S // BLOCK
    qb = q.reshape(B, nb, BLOCK, N_KV_HEADS, GQA_GROUP, HEAD_DIM)

    def band(t):
        tp = jnp.pad(t, ((0, 0), (BLOCK, 0), (0, 0), (0, 0)))
        tb = tp.reshape(B, nb + 1, BLOCK, N_KV_HEADS, HEAD_DIM)
        return jnp.concatenate([tb[:, :-1], tb[:, 1:]], axis=2)

    kb, vb = band(k), band(v)
    scores = jnp.einsum('bnqkgd,bnskd->bnkgqs', qb, kb).astype(jnp.float32) * (HEAD_DIM ** -0.5)
    qi = jnp.arange(BLOCK)[:, None]
    sj = jnp.arange(2 * BLOCK)[None, :]
    rel = qi + BLOCK - sj
    key_pos = jnp.arange(nb)[:, None, None] * BLOCK + sj[None] - BLOCK
    valid = ((rel >= 0) & (rel < WINDOW))[None] & (key_pos >= 0)
    valid = valid[None, :, None, None]
    sink = sinks.astype(jnp.float32).reshape(1, 1, N_KV_HEADS, GQA_GROUP, 1, 1)
    masked = jnp.where(valid, scores, -jnp.inf)
    m = jnp.maximum(jnp.max(masked, axis=-1, keepdims=True), sink)
    p = jnp.where(valid, jnp.exp(masked - m), 0.0)
    denom = jnp.sum(p, axis=-1, keepdims=True) + jnp.exp(sink - m)
    probs = (p / denom).astype(v.dtype)
    out = jnp.einsum('bnkgqs,bnskd->bnqkgd', probs, vb)
    return out.reshape(B, S, Q_WIDTH)


def conformer_conv(u2, w_dw, b_dw, ln_g, ln_b, w_pw):
    a, b = jnp.split(u2, 2, axis=-1)
    u = a * jax.nn.sigmoid(b)
    u = jnp.pad(u, ((0, 0), (CONV_WIDTH - 1, 0), (0, 0)))
    y = lax.conv_general_dilated(
        u, w_dw[:, None, :].astype(u.dtype), window_strides=(1,), padding='VALID',
        dimension_numbers=('NWC', 'WIO', 'NWC'), feature_group_count=CONV_CH)
    y = y + b_dw
    y = jax.nn.silu(layernorm(y, ln_g, ln_b))
    return y @ w_pw


def _fwd_setup_inputs(seed: int = 0) -> dict:
    key = jax.random.key(seed)
    ks = jax.random.split(key, 24)
    f32 = jnp.float32
    L = DEPTH

    def w(k, shape, fan_in):
        return jax.random.normal(k, shape, f32) * (fan_in ** -0.5)

    def gain(k, shape):
        return 1.0 + 0.02 * jax.random.normal(k, shape, f32)

    def small(k, shape):
        return 0.01 * jax.random.normal(k, shape, f32)

    return {
        "x": jax.random.normal(ks[0], (BATCH, SEQ, D_MODEL), f32),
        "c": jax.random.normal(ks[1], (BATCH, D_MODEL), f32),
        "w_ada": w(ks[2], (L, D_MODEL, N_MOD * D_MODEL), D_MODEL),
        "b_ada": small(ks[3], (L, N_MOD * D_MODEL)),
        "norm_ffn1_g": gain(ks[4], (L, D_MODEL)),
        "ffn1_w_gate": w(ks[5], (L, D_MODEL, D_FF), D_MODEL),
        "ffn1_w_up": w(ks[6], (L, D_MODEL, D_FF), D_MODEL),
        "ffn1_w_down": w(ks[7], (L, D_FF, D_MODEL), D_FF),
        "norm_mix_g": gain(ks[8], (L, D_MODEL)),
        "w_in": w(ks[9], (L, D_MODEL, IN_WIDTH), D_MODEL),
        "attn_sinks": 0.5 * jax.random.normal(ks[10], (L, N_Q_HEADS), f32),
        "w_attn_o": w(ks[11], (L, Q_WIDTH, D_MODEL), Q_WIDTH),
        "conv_w_dw": w(ks[12], (L, CONV_WIDTH, CONV_CH), CONV_WIDTH),
        "conv_b_dw": small(ks[13], (L, CONV_CH)),
        "conv_ln_g": gain(ks[14], (L, CONV_CH)),
        "conv_ln_b": small(ks[15], (L, CONV_CH)),
        "w_conv_o": w(ks[16], (L, CONV_CH, D_MODEL), CONV_CH),
        "w_out": w(ks[17], (L, D_MODEL, D_MODEL), D_MODEL),
        "norm_ffn2_g": gain(ks[18], (L, D_MODEL)),
        "ffn2_w_gate": w(ks[19], (L, D_MODEL, D_FF), D_MODEL),
        "ffn2_w_up": w(ks[20], (L, D_MODEL, D_FF), D_MODEL),
        "ffn2_w_down": w(ks[21], (L, D_FF, D_MODEL), D_FF),
        "final_norm_g": gain(ks[22], (D_MODEL,)),
    }


def _fwd_reference(x, c, w_ada, b_ada, norm_ffn1_g, ffn1_w_gate, ffn1_w_up, ffn1_w_down,
              norm_mix_g, w_in, attn_sinks, w_attn_o, conv_w_dw, conv_b_dw, conv_ln_g,
              conv_ln_b, w_conv_o, w_out, norm_ffn2_g, ffn2_w_gate, ffn2_w_up, ffn2_w_down,
              final_norm_g):
    B, S, _ = x.shape
    c_act = jax.nn.silu(c)
    split_idx = np.cumsum([Q_WIDTH, KV_WIDTH, KV_WIDTH, 2 * CONV_CH, D_MODEL]).tolist()
    for l in range(DEPTH):
        mod = (c_act @ w_ada[l] + b_ada[l]).reshape(B, N_MOD, D_MODEL)
        sh1, sc1, g1 = mod[:, 0], mod[:, 1], mod[:, 2]
        sh2, sc2, g2 = mod[:, 3], mod[:, 4], mod[:, 5]
        sh3, sc3, g3 = mod[:, 6], mod[:, 7], mod[:, 8]

        h = modulate(rmsnorm(x, norm_ffn1_g[l]), sh1, sc1)
        x = x + FFN_RESIDUAL * g1[:, None, :] * swiglu(h, ffn1_w_gate[l], ffn1_w_up[l], ffn1_w_down[l])

        h = modulate(rmsnorm(x, norm_mix_g[l]), sh2, sc2)
        proj = h @ w_in[l]
        q, k, v, conv_in, gate_a, gate_c = jnp.split(proj, split_idx, axis=-1)
        q = q.reshape(B, S, N_Q_HEADS, HEAD_DIM)
        k = k.reshape(B, S, N_KV_HEADS, HEAD_DIM)
        v = v.reshape(B, S, N_KV_HEADS, HEAD_DIM)
        y_attn = sliding_window_sink_attention(q, k, v, attn_sinks[l]) @ w_attn_o[l]
        y_conv = conformer_conv(conv_in, conv_w_dw[l], conv_b_dw[l], conv_ln_g[l],
                                conv_ln_b[l], w_conv_o[l])
        merged = jax.nn.sigmoid(gate_a) * y_attn + jax.nn.sigmoid(gate_c) * y_conv
        x = x + g2[:, None, :] * (merged @ w_out[l])

        h = modulate(rmsnorm(x, norm_ffn2_g[l]), sh3, sc3)
        x = x + FFN_RESIDUAL * g3[:, None, :] * swiglu(h, ffn2_w_gate[l], ffn2_w_up[l], ffn2_w_down[l])
    return rmsnorm(x, final_norm_g)


import jax as _jax
import jax.numpy as _jnp

TWIN_FORMAT = 'train_step'
FWD_PARAMS = ['x', 'c', 'w_ada', 'b_ada', 'norm_ffn1_g', 'ffn1_w_gate', 'ffn1_w_up', 'ffn1_w_down', 'norm_mix_g', 'w_in', 'attn_sinks', 'w_attn_o', 'conv_w_dw', 'conv_b_dw', 'conv_ln_g', 'conv_ln_b', 'w_conv_o', 'w_out', 'norm_ffn2_g', 'ffn2_w_gate', 'ffn2_w_up', 'ffn2_w_down', 'final_norm_g']
TWIN_WEIGHTS = ['w_ada', 'b_ada', 'norm_ffn1_g', 'ffn1_w_gate', 'ffn1_w_up', 'ffn1_w_down', 'norm_mix_g', 'w_in', 'attn_sinks', 'w_attn_o', 'conv_w_dw', 'conv_b_dw', 'conv_ln_g', 'conv_ln_b', 'w_conv_o', 'w_out', 'norm_ffn2_g', 'ffn2_w_gate', 'ffn2_w_up', 'ffn2_w_down', 'final_norm_g']
TWIN_DIFF_INPUT = 'x'
TWIN_INPUTS = ['x', 'c', 'w_ada', 'b_ada', 'norm_ffn1_g', 'ffn1_w_gate', 'ffn1_w_up', 'ffn1_w_down', 'norm_mix_g', 'w_in', 'attn_sinks', 'w_attn_o', 'conv_w_dw', 'conv_b_dw', 'conv_ln_g', 'conv_ln_b', 'w_conv_o', 'w_out', 'norm_ffn2_g', 'ffn2_w_gate', 'ffn2_w_up', 'ffn2_w_down', 'final_norm_g', 'loss_target', 'm_w_ada', 'm_b_ada', 'm_norm_ffn1_g', 'm_ffn1_w_gate', 'm_ffn1_w_up', 'm_ffn1_w_down', 'm_norm_mix_g', 'm_w_in', 'm_attn_sinks', 'm_w_attn_o', 'm_conv_w_dw', 'm_conv_b_dw', 'm_conv_ln_g', 'm_conv_ln_b', 'm_w_conv_o', 'm_w_out', 'm_norm_ffn2_g', 'm_ffn2_w_gate', 'm_ffn2_w_up', 'm_ffn2_w_down', 'm_final_norm_g', 'v_w_ada', 'v_b_ada', 'v_norm_ffn1_g', 'v_ffn1_w_gate', 'v_ffn1_w_up', 'v_ffn1_w_down', 'v_norm_mix_g', 'v_w_in', 'v_attn_sinks', 'v_w_attn_o', 'v_conv_w_dw', 'v_conv_b_dw', 'v_conv_ln_g', 'v_conv_ln_b', 'v_w_conv_o', 'v_w_out', 'v_norm_ffn2_g', 'v_ffn2_w_gate', 'v_ffn2_w_up', 'v_ffn2_w_down', 'v_final_norm_g']
TWIN_OUTPUTS = ['loss', 'grad_x', 'grad_w_ada', 'grad_b_ada', 'grad_norm_ffn1_g', 'grad_ffn1_w_gate', 'grad_ffn1_w_up', 'grad_ffn1_w_down', 'grad_norm_mix_g', 'grad_w_in', 'grad_attn_sinks', 'grad_w_attn_o', 'grad_conv_w_dw', 'grad_conv_b_dw', 'grad_conv_ln_g', 'grad_conv_ln_b', 'grad_w_conv_o', 'grad_w_out', 'grad_norm_ffn2_g', 'grad_ffn2_w_gate', 'grad_ffn2_w_up', 'grad_ffn2_w_down', 'grad_final_norm_g', 'delta_w_ada', 'delta_b_ada', 'delta_norm_ffn1_g', 'delta_ffn1_w_gate', 'delta_ffn1_w_up', 'delta_ffn1_w_down', 'delta_norm_mix_g', 'delta_w_in', 'delta_attn_sinks', 'delta_w_attn_o', 'delta_conv_w_dw', 'delta_conv_b_dw', 'delta_conv_ln_g', 'delta_conv_ln_b', 'delta_w_conv_o', 'delta_w_out', 'delta_norm_ffn2_g', 'delta_ffn2_w_gate', 'delta_ffn2_w_up', 'delta_ffn2_w_down', 'delta_final_norm_g', 'new_m_w_ada', 'new_m_b_ada', 'new_m_norm_ffn1_g', 'new_m_ffn1_w_gate', 'new_m_ffn1_w_up', 'new_m_ffn1_w_down', 'new_m_norm_mix_g', 'new_m_w_in', 'new_m_attn_sinks', 'new_m_w_attn_o', 'new_m_conv_w_dw', 'new_m_conv_b_dw', 'new_m_conv_ln_g', 'new_m_conv_ln_b', 'new_m_w_conv_o', 'new_m_w_out', 'new_m_norm_ffn2_g', 'new_m_ffn2_w_gate', 'new_m_ffn2_w_up', 'new_m_ffn2_w_down', 'new_m_final_norm_g', 'new_v_w_ada', 'new_v_b_ada', 'new_v_norm_ffn1_g', 'new_v_ffn1_w_gate', 'new_v_ffn1_w_up', 'new_v_ffn1_w_down', 'new_v_norm_mix_g', 'new_v_w_in', 'new_v_attn_sinks', 'new_v_w_attn_o', 'new_v_conv_w_dw', 'new_v_conv_b_dw', 'new_v_conv_ln_g', 'new_v_conv_ln_b', 'new_v_w_conv_o', 'new_v_w_out', 'new_v_norm_ffn2_g', 'new_v_ffn2_w_gate', 'new_v_ffn2_w_up', 'new_v_ffn2_w_down', 'new_v_final_norm_g']
TWIN_LEAF_KINDS = {'loss': 'loss', 'grad_x': 'grad_x', 'grad_w_ada': 'grad_w', 'grad_b_ada': 'grad_w', 'grad_norm_ffn1_g': 'grad_w', 'grad_ffn1_w_gate': 'grad_w', 'grad_ffn1_w_up': 'grad_w', 'grad_ffn1_w_down': 'grad_w', 'grad_norm_mix_g': 'grad_w', 'grad_w_in': 'grad_w', 'grad_attn_sinks': 'grad_w', 'grad_w_attn_o': 'grad_w', 'grad_conv_w_dw': 'grad_w', 'grad_conv_b_dw': 'grad_w', 'grad_conv_ln_g': 'grad_w', 'grad_conv_ln_b': 'grad_w', 'grad_w_conv_o': 'grad_w', 'grad_w_out': 'grad_w', 'grad_norm_ffn2_g': 'grad_w', 'grad_ffn2_w_gate': 'grad_w', 'grad_ffn2_w_up': 'grad_w', 'grad_ffn2_w_down': 'grad_w', 'grad_final_norm_g': 'grad_w', 'delta_w_ada': 'delta_w', 'delta_b_ada': 'delta_w', 'delta_norm_ffn1_g': 'delta_w', 'delta_ffn1_w_gate': 'delta_w', 'delta_ffn1_w_up': 'delta_w', 'delta_ffn1_w_down': 'delta_w', 'delta_norm_mix_g': 'delta_w', 'delta_w_in': 'delta_w', 'delta_attn_sinks': 'delta_w', 'delta_w_attn_o': 'delta_w', 'delta_conv_w_dw': 'delta_w', 'delta_conv_b_dw': 'delta_w', 'delta_conv_ln_g': 'delta_w', 'delta_conv_ln_b': 'delta_w', 'delta_w_conv_o': 'delta_w', 'delta_w_out': 'delta_w', 'delta_norm_ffn2_g': 'delta_w', 'delta_ffn2_w_gate': 'delta_w', 'delta_ffn2_w_up': 'delta_w', 'delta_ffn2_w_down': 'delta_w', 'delta_final_norm_g': 'delta_w', 'new_m_w_ada': 'new_m', 'new_m_b_ada': 'new_m', 'new_m_norm_ffn1_g': 'new_m', 'new_m_ffn1_w_gate': 'new_m', 'new_m_ffn1_w_up': 'new_m', 'new_m_ffn1_w_down': 'new_m', 'new_m_norm_mix_g': 'new_m', 'new_m_w_in': 'new_m', 'new_m_attn_sinks': 'new_m', 'new_m_w_attn_o': 'new_m', 'new_m_conv_w_dw': 'new_m', 'new_m_conv_b_dw': 'new_m', 'new_m_conv_ln_g': 'new_m', 'new_m_conv_ln_b': 'new_m', 'new_m_w_conv_o': 'new_m', 'new_m_w_out': 'new_m', 'new_m_norm_ffn2_g': 'new_m', 'new_m_ffn2_w_gate': 'new_m', 'new_m_ffn2_w_up': 'new_m', 'new_m_ffn2_w_down': 'new_m', 'new_m_final_norm_g': 'new_m', 'new_v_w_ada': 'new_v', 'new_v_b_ada': 'new_v', 'new_v_norm_ffn1_g': 'new_v', 'new_v_ffn1_w_gate': 'new_v', 'new_v_ffn1_w_up': 'new_v', 'new_v_ffn1_w_down': 'new_v', 'new_v_norm_mix_g': 'new_v', 'new_v_w_in': 'new_v', 'new_v_attn_sinks': 'new_v', 'new_v_w_attn_o': 'new_v', 'new_v_conv_w_dw': 'new_v', 'new_v_conv_b_dw': 'new_v', 'new_v_conv_ln_g': 'new_v', 'new_v_conv_ln_b': 'new_v', 'new_v_w_conv_o': 'new_v', 'new_v_w_out': 'new_v', 'new_v_norm_ffn2_g': 'new_v', 'new_v_ffn2_w_gate': 'new_v', 'new_v_ffn2_w_up': 'new_v', 'new_v_ffn2_w_down': 'new_v', 'new_v_final_norm_g': 'new_v'}


def _forward(args):
    return _fwd_reference(*[args[k] for k in FWD_PARAMS])


def _output_shape():
    out = _jax.eval_shape(lambda: _forward(_fwd_setup_inputs(0)))
    return out.shape, out.dtype

N_MICROBATCH = 1
ADAM_LR = 0.001
ADAM_B1 = 0.9
ADAM_B2 = 0.999
ADAM_EPS = 1e-08
ADAM_WD = 0.01
ADAM_STEP = 10
PER_EXAMPLE_BATCH_AXIS = {'x': 0, 'c': 0, 'loss_target': 0}
SHARED_INPUTS = []
_WEIGHT_DTYPES = {'w_ada': _jnp.float32, 'b_ada': _jnp.float32, 'norm_ffn1_g': _jnp.float32, 'ffn1_w_gate': _jnp.float32, 'ffn1_w_up': _jnp.float32, 'ffn1_w_down': _jnp.float32, 'norm_mix_g': _jnp.float32, 'w_in': _jnp.float32, 'attn_sinks': _jnp.float32, 'w_attn_o': _jnp.float32, 'conv_w_dw': _jnp.float32, 'conv_b_dw': _jnp.float32, 'conv_ln_g': _jnp.float32, 'conv_ln_b': _jnp.float32, 'w_conv_o': _jnp.float32, 'w_out': _jnp.float32, 'norm_ffn2_g': _jnp.float32, 'ffn2_w_gate': _jnp.float32, 'ffn2_w_up': _jnp.float32, 'ffn2_w_down': _jnp.float32, 'final_norm_g': _jnp.float32}
MOMENT_SCALE = {'w_ada': 4.720796e-02, 'b_ada': 7.658744e-02, 'norm_ffn1_g': 7.420195e-02, 'ffn1_w_gate': 3.584700e-02, 'ffn1_w_up': 3.468656e-02, 'ffn1_w_down': 5.763313e-02, 'norm_mix_g': 5.592120e-02, 'w_in': 3.059118e-02, 'attn_sinks': 1.624785e-02, 'w_attn_o': 4.383072e-02, 'conv_w_dw': 3.657552e-02, 'conv_b_dw': 5.173275e-02, 'conv_ln_g': 4.207211e-02, 'conv_ln_b': 4.014304e-02, 'w_conv_o': 3.611147e-02, 'w_out': 5.635231e-02, 'norm_ffn2_g': 7.621518e-02, 'ffn2_w_gate': 3.263481e-02, 'ffn2_w_up': 3.141508e-02, 'ffn2_w_down': 5.282874e-02, 'final_norm_g': 3.220568e+01}


def _to_microbatches(a, axis):
    t = _jnp.moveaxis(a, axis, 0)
    t = t.reshape((N_MICROBATCH, t.shape[0] // N_MICROBATCH) + t.shape[1:])
    return _jnp.moveaxis(t, 1, axis + 1)


def setup_inputs(seed: int = 0) -> dict:
    inp = _fwd_setup_inputs(seed)
    key = _jax.random.fold_in(_jax.random.key(seed), 7919)
    shape, _ = _output_shape()
    out = dict(inp)
    out["loss_target"] = _jax.random.normal(_jax.random.fold_in(key, 0), shape, _jnp.float32)
    for i, name in enumerate(TWIN_WEIGHTS):
        w = inp[name].astype(_jnp.float32)
        if MOMENT_SCALE is None:
            s = _jnp.sqrt(_jnp.mean(_jnp.square(w)) + 1e-30)
        else:
            s = MOMENT_SCALE[name]
        km, kv = _jax.random.split(_jax.random.fold_in(key, i + 1))
        out[name] = w
        out["m_" + name] = s * _jax.random.normal(km, w.shape, _jnp.float32)
        out["v_" + name] = (s * s) * _jax.random.uniform(kv, w.shape, _jnp.float32, 0.5, 1.5)
    if N_MICROBATCH > 1:
        for name, axis in PER_EXAMPLE_BATCH_AXIS.items():
            out[name] = _to_microbatches(out[name], axis)
    return {'x': out['x'], 'c': out['c'], 'w_ada': out['w_ada'], 'b_ada': out['b_ada'], 'norm_ffn1_g': out['norm_ffn1_g'], 'ffn1_w_gate': out['ffn1_w_gate'], 'ffn1_w_up': out['ffn1_w_up'], 'ffn1_w_down': out['ffn1_w_down'], 'norm_mix_g': out['norm_mix_g'], 'w_in': out['w_in'], 'attn_sinks': out['attn_sinks'], 'w_attn_o': out['w_attn_o'], 'conv_w_dw': out['conv_w_dw'], 'conv_b_dw': out['conv_b_dw'], 'conv_ln_g': out['conv_ln_g'], 'conv_ln_b': out['conv_ln_b'], 'w_conv_o': out['w_conv_o'], 'w_out': out['w_out'], 'norm_ffn2_g': out['norm_ffn2_g'], 'ffn2_w_gate': out['ffn2_w_gate'], 'ffn2_w_up': out['ffn2_w_up'], 'ffn2_w_down': out['ffn2_w_down'], 'final_norm_g': out['final_norm_g'], 'loss_target': out['loss_target'], 'm_w_ada': out['m_w_ada'], 'm_b_ada': out['m_b_ada'], 'm_norm_ffn1_g': out['m_norm_ffn1_g'], 'm_ffn1_w_gate': out['m_ffn1_w_gate'], 'm_ffn1_w_up': out['m_ffn1_w_up'], 'm_ffn1_w_down': out['m_ffn1_w_down'], 'm_norm_mix_g': out['m_norm_mix_g'], 'm_w_in': out['m_w_in'], 'm_attn_sinks': out['m_attn_sinks'], 'm_w_attn_o': out['m_w_attn_o'], 'm_conv_w_dw': out['m_conv_w_dw'], 'm_conv_b_dw': out['m_conv_b_dw'], 'm_conv_ln_g': out['m_conv_ln_g'], 'm_conv_ln_b': out['m_conv_ln_b'], 'm_w_conv_o': out['m_w_conv_o'], 'm_w_out': out['m_w_out'], 'm_norm_ffn2_g': out['m_norm_ffn2_g'], 'm_ffn2_w_gate': out['m_ffn2_w_gate'], 'm_ffn2_w_up': out['m_ffn2_w_up'], 'm_ffn2_w_down': out['m_ffn2_w_down'], 'm_final_norm_g': out['m_final_norm_g'], 'v_w_ada': out['v_w_ada'], 'v_b_ada': out['v_b_ada'], 'v_norm_ffn1_g': out['v_norm_ffn1_g'], 'v_ffn1_w_gate': out['v_ffn1_w_gate'], 'v_ffn1_w_up': out['v_ffn1_w_up'], 'v_ffn1_w_down': out['v_ffn1_w_down'], 'v_norm_mix_g': out['v_norm_mix_g'], 'v_w_in': out['v_w_in'], 'v_attn_sinks': out['v_attn_sinks'], 'v_w_attn_o': out['v_w_attn_o'], 'v_conv_w_dw': out['v_conv_w_dw'], 'v_conv_b_dw': out['v_conv_b_dw'], 'v_conv_ln_g': out['v_conv_ln_g'], 'v_conv_ln_b': out['v_conv_ln_b'], 'v_w_conv_o': out['v_w_conv_o'], 'v_w_out': out['v_w_out'], 'v_norm_ffn2_g': out['v_norm_ffn2_g'], 'v_ffn2_w_gate': out['v_ffn2_w_gate'], 'v_ffn2_w_up': out['v_ffn2_w_up'], 'v_ffn2_w_down': out['v_ffn2_w_down'], 'v_final_norm_g': out['v_final_norm_g']}


def _loss(weights, diff, rest, loss_target):
    with _jax.named_scope("forward"):
        args = {**rest, TWIN_DIFF_INPUT: diff, **{k: w.astype(_WEIGHT_DTYPES[k]) for k, w in weights.items()}}
        y = _forward(args)
    with _jax.named_scope("loss_head"):
        err = _jnp.square(y.astype(_jnp.float32) - loss_target)
        return 0.5 * _jnp.sum(_jnp.mean(err, axis=-1)) if err.ndim else 0.5 * err


def _adamw(w, g, m, v):
    m = ADAM_B1 * m + (1.0 - ADAM_B1) * g
    v = ADAM_B2 * v + (1.0 - ADAM_B2) * _jnp.square(g)
    m_hat = m / (1.0 - ADAM_B1 ** ADAM_STEP)
    v_hat = v / (1.0 - ADAM_B2 ** ADAM_STEP)
    delta = -ADAM_LR * (m_hat / (_jnp.sqrt(v_hat) + ADAM_EPS) + ADAM_WD * w)
    return delta, m, v


def reference(x, c, w_ada, b_ada, norm_ffn1_g, ffn1_w_gate, ffn1_w_up, ffn1_w_down, norm_mix_g, w_in, attn_sinks, w_attn_o, conv_w_dw, conv_b_dw, conv_ln_g, conv_ln_b, w_conv_o, w_out, norm_ffn2_g, ffn2_w_gate, ffn2_w_up, ffn2_w_down, final_norm_g, loss_target, m_w_ada, m_b_ada, m_norm_ffn1_g, m_ffn1_w_gate, m_ffn1_w_up, m_ffn1_w_down, m_norm_mix_g, m_w_in, m_attn_sinks, m_w_attn_o, m_conv_w_dw, m_conv_b_dw, m_conv_ln_g, m_conv_ln_b, m_w_conv_o, m_w_out, m_norm_ffn2_g, m_ffn2_w_gate, m_ffn2_w_up, m_ffn2_w_down, m_final_norm_g, v_w_ada, v_b_ada, v_norm_ffn1_g, v_ffn1_w_gate, v_ffn1_w_up, v_ffn1_w_down, v_norm_mix_g, v_w_in, v_attn_sinks, v_w_attn_o, v_conv_w_dw, v_conv_b_dw, v_conv_ln_g, v_conv_ln_b, v_w_conv_o, v_w_out, v_norm_ffn2_g, v_ffn2_w_gate, v_ffn2_w_up, v_ffn2_w_down, v_final_norm_g):
    given = dict(x=x, c=c, w_ada=w_ada, b_ada=b_ada, norm_ffn1_g=norm_ffn1_g, ffn1_w_gate=ffn1_w_gate, ffn1_w_up=ffn1_w_up, ffn1_w_down=ffn1_w_down, norm_mix_g=norm_mix_g, w_in=w_in, attn_sinks=attn_sinks, w_attn_o=w_attn_o, conv_w_dw=conv_w_dw, conv_b_dw=conv_b_dw, conv_ln_g=conv_ln_g, conv_ln_b=conv_ln_b, w_conv_o=w_conv_o, w_out=w_out, norm_ffn2_g=norm_ffn2_g, ffn2_w_gate=ffn2_w_gate, ffn2_w_up=ffn2_w_up, ffn2_w_down=ffn2_w_down, final_norm_g=final_norm_g, loss_target=loss_target, m_w_ada=m_w_ada, m_b_ada=m_b_ada, m_norm_ffn1_g=m_norm_ffn1_g, m_ffn1_w_gate=m_ffn1_w_gate, m_ffn1_w_up=m_ffn1_w_up, m_ffn1_w_down=m_ffn1_w_down, m_norm_mix_g=m_norm_mix_g, m_w_in=m_w_in, m_attn_sinks=m_attn_sinks, m_w_attn_o=m_w_attn_o, m_conv_w_dw=m_conv_w_dw, m_conv_b_dw=m_conv_b_dw, m_conv_ln_g=m_conv_ln_g, m_conv_ln_b=m_conv_ln_b, m_w_conv_o=m_w_conv_o, m_w_out=m_w_out, m_norm_ffn2_g=m_norm_ffn2_g, m_ffn2_w_gate=m_ffn2_w_gate, m_ffn2_w_up=m_ffn2_w_up, m_ffn2_w_down=m_ffn2_w_down, m_final_norm_g=m_final_norm_g, v_w_ada=v_w_ada, v_b_ada=v_b_ada, v_norm_ffn1_g=v_norm_ffn1_g, v_ffn1_w_gate=v_ffn1_w_gate, v_ffn1_w_up=v_ffn1_w_up, v_ffn1_w_down=v_ffn1_w_down, v_norm_mix_g=v_norm_mix_g, v_w_in=v_w_in, v_attn_sinks=v_attn_sinks, v_w_attn_o=v_w_attn_o, v_conv_w_dw=v_conv_w_dw, v_conv_b_dw=v_conv_b_dw, v_conv_ln_g=v_conv_ln_g, v_conv_ln_b=v_conv_ln_b, v_w_conv_o=v_w_conv_o, v_w_out=v_w_out, v_norm_ffn2_g=v_norm_ffn2_g, v_ffn2_w_gate=v_ffn2_w_gate, v_ffn2_w_up=v_ffn2_w_up, v_ffn2_w_down=v_ffn2_w_down, v_final_norm_g=v_final_norm_g)
    weights = {n: given[n] for n in TWIN_WEIGHTS}
    shared = {n: given[n] for n in SHARED_INPUTS}
    per_example = {n: given[n] for n in ['x', 'c']}
    grad_fn = _jax.value_and_grad(_loss, argnums=(0, 1))

    def one_microbatch(ex, loss_target):
        ex = dict(ex)
        diff = ex.pop(TWIN_DIFF_INPUT)
        return grad_fn(weights, diff, {**shared, **ex}, loss_target)

    if N_MICROBATCH == 1:
        loss, (grad_w, grad_x) = one_microbatch(per_example, given["loss_target"])
    else:
        def body(carry, xs):
            loss_sum, grad_sum = carry
            l_k, (gw_k, gx_k) = one_microbatch(xs[0], xs[1])
            with _jax.named_scope("update"):
                return (loss_sum + l_k, _jax.tree.map(_jnp.add, grad_sum, gw_k)), gx_k

        init = (_jnp.zeros((), _jnp.float32), _jax.tree.map(_jnp.zeros_like, weights))
        (loss, grad_w), grad_x = _jax.lax.scan(body, init, (per_example, given["loss_target"]))
    with _jax.named_scope("update"):
        delta_w, new_m, new_v = {}, {}, {}
        for n in TWIN_WEIGHTS:
            delta_w[n], new_m[n], new_v[n] = _adamw(weights[n], grad_w[n], given["m_" + n], given["v_" + n])
    return (loss, grad_x, *[grad_w[n] for n in TWIN_WEIGHTS], *[delta_w[n] for n in TWIN_WEIGHTS],
            *[new_m[n] for n in TWIN_WEIGHTS], *[new_v[n] for n in TWIN_WEIGHTS])
```

```python
import functools

import jax
import jax.numpy as jnp
from jax import lax
from jax.experimental import pallas as pl
from jax.experimental.pallas import tpu as pltpu

F32 = jnp.float32
BF16 = jnp.bfloat16

D_MODEL = 1024
D_FF = 2816
N_CHIP = 4
N_DEV = 8
FF_SHARD = D_FF // N_CHIP
IN_WIDTH = 5376
IN_SHARD = IN_WIDTH // N_CHIP
HEAD_DIM = 64
N_Q_HEADS = 16
N_KV_HEADS = 2
BLOCK = 128
CONV_WIDTH = 31
CONV_PAD = 32
N_MOD = 9
EPS = 1e-6
FFN_RESIDUAL = 0.5
ATTN_SCALE = HEAD_DIM ** -0.5
MASK_VALUE = -1e30

ADAM_LR = 0.001
ADAM_B1 = 0.9
ADAM_B2 = 0.999
ADAM_EPS = 1e-08
ADAM_WD = 0.01
ADAM_STEP = 10

COLB_Q, COLB_CA, COLB_CB, COLB_GA, COLB_GC = 0, 1, 2, 3, 4
COLB_K, COLB_V = 40, 41
PROJ_TILE = 768

VMEM_LIMIT = 56 * 1024 * 1024
MESH = pl.DeviceIdType.MESH
ANY = pl.BlockSpec(memory_space=pl.ANY)
VMEM_SPEC = pl.BlockSpec(memory_space=pltpu.VMEM)
SMEM_SPEC = pl.BlockSpec(memory_space=pltpu.SMEM)


def _params(n_grid):
    return pltpu.CompilerParams(dimension_semantics=("arbitrary",) * n_grid, vmem_limit_bytes=VMEM_LIMIT)


def _tile(n, pref):
    t = min(n, pref)
    while n % t:
        t //= 2
    return t


def _sigmoid(v):
    return 1.0 / (1.0 + jnp.exp(-v))


def _dot_nn(a, b):
    return lax.dot_general(a, b, (((1,), (0,)), ((), ())), preferred_element_type=F32)


def _dot_nt(a, b):
    return lax.dot_general(a, b, (((1,), (1,)), ((), ())), preferred_element_type=F32)


def _dot_tn(a, b):
    return lax.dot_general(a, b, (((0,), (0,)), ((), ())), preferred_element_type=F32)


def _norm_mod(xv, gn, sc, sh):
    r = lax.rsqrt(jnp.mean(xv * xv, axis=-1, keepdims=True) + EPS)
    return ((xv * r) * gn) * (1.0 + sc) + sh


def _accumulate(ref, first, value):
    @pl.when(first)
    def _():
        ref[...] = value

    @pl.when(jnp.logical_not(first))
    def _():
        ref[...] += value


def _norm_mod_bwd(dh, xv, gn, sc, dxo, first_of_batch, first, dx_ref, dsc_ref, dsh_ref, dgn_ref):
    r = lax.rsqrt(jnp.mean(xv * xv, axis=-1, keepdims=True) + EPS)
    xh = xv * r
    _accumulate(dsh_ref, first_of_batch, jnp.sum(dh, axis=0, keepdims=True))
    _accumulate(dsc_ref, first_of_batch, jnp.sum(dh * (xh * gn), axis=0, keepdims=True))
    dn = dh * (1.0 + sc)
    _accumulate(dgn_ref, first, jnp.sum(dn * xh, axis=0, keepdims=True))
    dxh = dn * gn
    dx_ref[...] = dxo + r * (dxh - xh * jnp.mean(dxh * xh, axis=-1, keepdims=True))


def _ffn_fwd(x, gn, sc, sh, gate, wg, wu, wd, seq, name):
    T, D = x.shape
    J, _, Fs = wg.shape
    tm = _tile(seq, 512)
    nb = seq // tm

    def body(x_ref, gn_ref, sc_ref, sh_ref, gate_ref, wg_ref, wu_ref, wd_ref,
             h_ref, a_ref, u_ref, f_ref, xo_ref, hs, acc):
        j = pl.program_id(1)

        @pl.when(j == 0)
        def _():
            hb = _norm_mod(x_ref[...], gn_ref[...], sc_ref[...], sh_ref[...]).astype(BF16)
            hs[...] = hb
            h_ref[...] = hb
            acc[...] = jnp.zeros_like(acc)

        hb = hs[...]
        a = _dot_nn(hb, wg_ref[...])
        u = _dot_nn(hb, wu_ref[...])
        a_ref[...] = a.astype(BF16)
        u_ref[...] = u.astype(BF16)
        s = ((a * _sigmoid(a)) * u).astype(BF16)
        acc[...] += _dot_nn(s, wd_ref[...])

        @pl.when(j == J - 1)
        def _():
            f = acc[...]
            f_ref[...] = f.astype(BF16)
            xo_ref[...] = x_ref[...] + (FFN_RESIDUAL * gate_ref[...]) * f

    row = pl.BlockSpec((tm, D), lambda i, j: (i, 0))
    vec = pl.BlockSpec((1, D), lambda i, j: (0, 0))
    per_b = pl.BlockSpec((None, 1, D), lambda i, j: (i // nb, 0, 0))
    hid = pl.BlockSpec((None, tm, Fs), lambda i, j: (j, i, 0))
    return pl.pallas_call(
        body, name=name, grid=(T // tm, J),
        in_specs=[row, vec, per_b, per_b, per_b,
                  pl.BlockSpec((None, D, Fs), lambda i, j: (j, 0, 0)),
                  pl.BlockSpec((None, D, Fs), lambda i, j: (j, 0, 0)),
                  pl.BlockSpec((None, Fs, D), lambda i, j: (j, 0, 0))],
        out_specs=[row, hid, hid, row, row],
        out_shape=[jax.ShapeDtypeStruct((T, D), BF16), jax.ShapeDtypeStruct((J, T, Fs), BF16),
                   jax.ShapeDtypeStruct((J, T, Fs), BF16), jax.ShapeDtypeStruct((T, D), BF16),
                   jax.ShapeDtypeStruct((T, D), F32)],
        scratch_shapes=[pltpu.VMEM((tm, D), BF16), pltpu.VMEM((tm, D), F32)],
        compiler_params=_params(2),
    )(x, gn, sc, sh, gate, wg, wu, wd)


def _ffn_bwd(dxo, x, f, a, u, gn, sc, gate, wg, wu, wd, seq, name):
    T, D = x.shape
    J, _, Fs = wg.shape
    B = T // seq
    tm = _tile(seq, 512)
    nb = seq // tm

    def body(dxo_ref, x_ref, f_ref, a_ref, u_ref, gn_ref, sc_ref, gate_ref, wg_ref, wu_ref, wd_ref,
             da_ref, du_ref, s_ref, df_ref, dx_ref, dgate_ref, dsc_ref, dsh_ref, dgn_ref, dfs, acc):
        i = pl.program_id(0)
        j = pl.program_id(1)
        first_of_batch = i % nb == 0

        @pl.when(j == 0)
        def _():
            dxo_v = dxo_ref[...]
            dfb = ((FFN_RESIDUAL * gate_ref[...]) * dxo_v).astype(BF16)
            dfs[...] = dfb
            df_ref[...] = dfb
            part = jnp.sum((FFN_RESIDUAL * f_ref[...].astype(F32)) * dxo_v, axis=0, keepdims=True)
            _accumulate(dgate_ref, first_of_batch, part)
            acc[...] = jnp.zeros_like(acc)

        ds = _dot_nt(dfs[...], wd_ref[...])
        av = a_ref[...].astype(F32)
        uv = u_ref[...].astype(F32)
        sig = _sigmoid(av)
        sil = av * sig
        s_ref[...] = (sil * uv).astype(BF16)
        dab = (ds * uv * (sig * (1.0 + av * (1.0 - sig)))).astype(BF16)
        dub = (ds * sil).astype(BF16)
        da_ref[...] = dab
        du_ref[...] = dub
        acc[...] += _dot_nt(dab, wg_ref[...]) + _dot_nt(dub, wu_ref[...])

        @pl.when(j == J - 1)
        def _():
            _norm_mod_bwd(acc[...], x_ref[...], gn_ref[...], sc_ref[...], dxo_ref[...],
                          first_of_batch, i == 0, dx_ref, dsc_ref, dsh_ref, dgn_ref)

    row = pl.BlockSpec((tm, D), lambda i, j: (i, 0))
    vec = pl.BlockSpec((1, D), lambda i, j: (0, 0))
    per_b = pl.BlockSpec((None, 1, D), lambda i, j: (i // nb, 0, 0))
    hid = pl.BlockSpec((None, tm, Fs), lambda i, j: (j, i, 0))
    hid_shape = jax.ShapeDtypeStruct((J, T, Fs), BF16)
    per_b_shape = jax.ShapeDtypeStruct((B, 1, D), F32)
    return pl.pallas_call(
        body, name=name, grid=(T // tm, J),
        in_specs=[row, row, row, hid, hid, vec, per_b, per_b,
                  pl.BlockSpec((None, D, Fs), lambda i, j: (j, 0, 0)),
                  pl.BlockSpec((None, D, Fs), lambda i, j: (j, 0, 0)),
                  pl.BlockSpec((None, Fs, D), lambda i, j: (j, 0, 0))],
        out_specs=[hid, hid, hid, row, row, per_b, per_b, per_b, vec],
        out_shape=[hid_shape, hid_shape, hid_shape, jax.ShapeDtypeStruct((T, D), BF16),
                   jax.ShapeDtypeStruct((T, D), F32), per_b_shape, per_b_shape, per_b_shape,
                   jax.ShapeDtypeStruct((1, D), F32)],
        scratch_shapes=[pltpu.VMEM((tm, D), BF16), pltpu.VMEM((tm, D), F32)],
        compiler_params=_params(2),
    )(dxo, x, f, a, u, gn, sc, gate, wg, wu, wd)


def _wgrad(a, a_spec, b, b_spec, rows, cols, n_tok, name):
    tk = _tile(n_tok, 512)
    nk = n_tok // tk
    half = rows // 2

    def body(a_ref, b_ref, o32_ref, o16_ref, acc):
        k = pl.program_id(1)

        @pl.when(k == 0)
        def _():
            acc[...] = jnp.zeros_like(acc)

        acc[...] += _dot_tn(a_ref[...], b_ref[...])

        @pl.when(k == nk - 1)
        def _():
            for h in range(2):
                v = acc[h * half:(h + 1) * half, :]
                o32_ref[h] = v
                o16_ref[h] = v.astype(BF16)

    out_spec = pl.BlockSpec((2, None, half, cols), lambda j, k: (0, j, 0, 0))
    return pl.pallas_call(
        body, name=name, grid=(N_CHIP, nk),
        in_specs=[a_spec(tk), b_spec(tk)],
        out_specs=[out_spec, out_spec],
        out_shape=[jax.ShapeDtypeStruct((2, N_CHIP, half, cols), F32),
                   jax.ShapeDtypeStruct((2, N_CHIP, half, cols), BF16)],
        scratch_shapes=[pltpu.VMEM((rows, cols), F32)],
        compiler_params=_params(2),
    )(a, b)


def _spec_rows(width):
    return lambda tk: pl.BlockSpec((tk, width), lambda j, k: (k, 0))


def _spec_chip_major(width):
    return lambda tk: pl.BlockSpec((None, tk, width), lambda j, k: (j, k, 0))


def _spec_col_block(width):
    return lambda tk: pl.BlockSpec((tk, width), lambda j, k: (k, j))


def _in_proj(x, gn, sc, sh, w_in, seq):
    T, D = x.shape
    N = w_in.shape[1]
    tm = _tile(seq, 1024)
    nb = seq // tm

    def body(x_ref, gn_ref, sc_ref, sh_ref, w_ref, h_ref, p_ref, hs):
        @pl.when(pl.program_id(1) == 0)
        def _():
            hb = _norm_mod(x_ref[...], gn_ref[...], sc_ref[...], sh_ref[...]).astype(BF16)
            hs[...] = hb
            h_ref[...] = hb

        p_ref[...] = _dot_nn(hs[...], w_ref[...]).astype(BF16)

    row = pl.BlockSpec((tm, D), lambda i, j: (i, 0))
    per_b = pl.BlockSpec((None, 1, D), lambda i, j: (i // nb, 0, 0))
    return pl.pallas_call(
        body, name="mix_in_proj", grid=(T // tm, N // PROJ_TILE),
        in_specs=[row, pl.BlockSpec((1, D), lambda i, j: (0, 0)), per_b, per_b,
                  pl.BlockSpec((D, PROJ_TILE), lambda i, j: (0, j))],
        out_specs=[row, pl.BlockSpec((tm, PROJ_TILE), lambda i, j: (i, j))],
        out_shape=[jax.ShapeDtypeStruct((T, D), BF16), jax.ShapeDtypeStruct((T, N), BF16)],
        scratch_shapes=[pltpu.VMEM((tm, D), BF16)],
        compiler_params=_params(2),
    )(x, gn, sc, sh, w_in)


def _attn_specs(nblk):
    def own(col):
        return lambda b, n: (b * nblk + n, col)

    def prev(col):
        return lambda b, n: (b * nblk + jnp.maximum(n - 1, 0), col)

    kv = (BLOCK, 2 * HEAD_DIM)
    return [pl.BlockSpec((BLOCK, D_MODEL), own(COLB_Q)),
            pl.BlockSpec(kv, prev(COLB_K)), pl.BlockSpec(kv, own(COLB_K)),
            pl.BlockSpec(kv, prev(COLB_V)), pl.BlockSpec(kv, own(COLB_V))]


def _band_operands(prev_ref, own_ref, lo):
    band = jnp.concatenate([prev_ref[...], own_ref[...]], axis=0).astype(F32)
    rolled = pltpu.roll(band, HEAD_DIM, 1)
    zero = jnp.zeros_like(band)
    head0 = (jnp.where(lo, band, zero).astype(BF16), jnp.where(lo, zero, rolled).astype(BF16))
    head1 = (jnp.where(lo, rolled, zero).astype(BF16), jnp.where(lo, zero, band).astype(BF16))
    return head0, head1


def _band_valid(has_prev):
    qi = lax.broadcasted_iota(jnp.int32, (BLOCK, 2 * BLOCK), 0)
    sj = lax.broadcasted_iota(jnp.int32, (BLOCK, 2 * BLOCK), 1)
    rel = qi + BLOCK - sj
    return (rel >= 0) & (rel < BLOCK) & ((sj >= BLOCK) | has_prev)


def _attn_fwd(proj, sinks, batch, seq):
    T = proj.shape[0]
    nblk = seq // BLOCK

    def body(sink_ref, q_ref, kp_ref, ko_ref, vp_ref, vo_ref, o_ref, lse_ref):
        lo = lax.broadcasted_iota(jnp.int32, (1, 2 * HEAD_DIM), 1) < HEAD_DIM
        head_lane = lax.broadcasted_iota(jnp.int32, (1, N_Q_HEADS), 1)
        valid = _band_valid(pl.program_id(1) > 0)
        k_ops = _band_operands(kp_ref, ko_ref, lo)
        v_ops = _band_operands(vp_ref, vo_ref, lo)
        lse_all = jnp.zeros((BLOCK, N_Q_HEADS), F32)
        for pair in range(N_Q_HEADS // 2):
            kvh = pair // (N_Q_HEADS // 2 // N_KV_HEADS)
            q2 = q_ref[:, pair * 2 * HEAD_DIM:(pair + 1) * 2 * HEAD_DIM]
            out = jnp.zeros((BLOCK, 2 * HEAD_DIM), F32)
            for side in range(2):
                head = 2 * pair + side
                sink = sink_ref[0, head]
                s = jnp.where(valid, _dot_nt(q2, k_ops[kvh][side]) * ATTN_SCALE, MASK_VALUE)
                m = jnp.maximum(jnp.max(s, axis=-1, keepdims=True), sink)
                p = jnp.where(valid, jnp.exp(s - m), 0.0)
                den = jnp.sum(p, axis=-1, keepdims=True) + jnp.exp(sink - m)
                out = out + _dot_nn((p / den).astype(BF16), v_ops[kvh][side])
                lse_all = jnp.where(head_lane == head, m + jnp.log(den), lse_all)
            o_ref[:, pair * 2 * HEAD_DIM:(pair + 1) * 2 * HEAD_DIM] = out.astype(BF16)
        lse_ref[...] = lse_all

    return pl.pallas_call(
        body, name="attn_fwd", grid=(batch, nblk),
        in_specs=[SMEM_SPEC] + _attn_specs(nblk),
        out_specs=[pl.BlockSpec((BLOCK, D_MODEL), lambda b, n: (b * nblk + n, 0)),
                   pl.BlockSpec((BLOCK, N_Q_HEADS), lambda b, n: (b * nblk + n, 0))],
        out_shape=[jax.ShapeDtypeStruct((T, D_MODEL), BF16), jax.ShapeDtypeStruct((T, N_Q_HEADS), F32)],
        compiler_params=_params(2),
    )(sinks, proj, proj, proj, proj, proj)


def _conv_u(ca, cb):
    return ca.astype(F32) * _sigmoid(cb.astype(F32))


def _conv_specs(ts, tiles_per_seq):
    per_tile = ts // CONV_PAD

    def tile(col):
        return lambda b, t: (b * tiles_per_seq + t, col)

    def before(col):
        return lambda b, t: (jnp.maximum((b * tiles_per_seq + t) * per_tile - 1, 0), col)

    return [pl.BlockSpec((ts, D_MODEL), tile(COLB_CA)), pl.BlockSpec((ts, D_MODEL), tile(COLB_CB)),
            pl.BlockSpec((CONV_PAD, D_MODEL), before(COLB_CA)), pl.BlockSpec((CONV_PAD, D_MODEL), before(COLB_CB))]


def _fill_upad(upad, ca_ref, cb_ref, cah_ref, cbh_ref, t):
    halo = _conv_u(cah_ref[...], cbh_ref[...])
    upad[0:CONV_PAD, :] = jnp.where(t > 0, halo, jnp.zeros_like(halo))
    upad[CONV_PAD:, :] = _conv_u(ca_ref[...], cb_ref[...])


def _layernorm_stats(y):
    mu = jnp.mean(y, axis=-1, keepdims=True)
    yc = y - mu
    rstd = lax.rsqrt(jnp.mean(yc * yc, axis=-1, keepdims=True) + EPS)
    return yc * rstd, rstd


def _conv_fwd(proj, w_dw, b_dw, ln_g, ln_b, batch, seq):
    T = proj.shape[0]
    ts = _tile(seq, 256)
    nt = seq // ts
    shift = CONV_PAD - (CONV_WIDTH - 1)

    def body(ca_ref, cb_ref, cah_ref, cbh_ref, w_ref, b_ref, g_ref, beta_ref, y_ref, z_ref, upad):
        _fill_upad(upad, ca_ref, cb_ref, cah_ref, cbh_ref, pl.program_id(1))
        y = jnp.zeros((ts, D_MODEL), F32) + b_ref[...]
        for k in range(CONV_WIDTH):
            y = y + w_ref[k:k + 1, :] * upad[shift + k:shift + k + ts, :]
        y_ref[...] = y
        lnh, _ = _layernorm_stats(y)
        ln = lnh * g_ref[...] + beta_ref[...]
        z_ref[...] = (ln * _sigmoid(ln)).astype(BF16)

    vec = pl.BlockSpec((1, D_MODEL), lambda b, t: (0, 0))
    row = pl.BlockSpec((ts, D_MODEL), lambda b, t: (b * nt + t, 0))
    return pl.pallas_call(
        body, name="conv_fwd", grid=(batch, nt),
        in_specs=_conv_specs(ts, nt) + [pl.BlockSpec((CONV_PAD, D_MODEL), lambda b, t: (0, 0)), vec, vec, vec],
        out_specs=[row, row],
        out_shape=[jax.ShapeDtypeStruct((T, D_MODEL), F32), jax.ShapeDtypeStruct((T, D_MODEL), BF16)],
        scratch_shapes=[pltpu.VMEM((ts + CONV_PAD, D_MODEL), F32)],
        compiler_params=_params(2),
    )(proj, proj, proj, proj, w_dw, b_dw, ln_g, ln_b)


def _merge(o, z, proj, w_ao, w_co, w_out, x, gate, seq):
    T, D = x.shape
    tm = _tile(seq, 512)
    nb = seq // tm

    def body(o_ref, z_ref, ga_ref, gc_ref, wao_ref, wco_ref, wout_ref, x_ref, gate_ref,
             ya_ref, yc_ref, mg_ref, mo_ref, xo_ref):
        ya = _dot_nn(o_ref[...], wao_ref[...])
        yc = _dot_nn(z_ref[...], wco_ref[...])
        ya_ref[...] = ya.astype(BF16)
        yc_ref[...] = yc.astype(BF16)
        merged = (_sigmoid(ga_ref[...].astype(F32)) * ya + _sigmoid(gc_ref[...].astype(F32)) * yc).astype(BF16)
        mg_ref[...] = merged
        mo = _dot_nn(merged, wout_ref[...])
        mo_ref[...] = mo.astype(BF16)
        xo_ref[...] = x_ref[...] + gate_ref[...] * mo

    row = pl.BlockSpec((tm, D), lambda i: (i, 0))
    mat = pl.BlockSpec((D, D), lambda i: (0, 0))
    act = jax.ShapeDtypeStruct((T, D), BF16)
    return pl.pallas_call(
        body, name="mix_merge", grid=(T // tm,),
        in_specs=[row, row, pl.BlockSpec((tm, D), lambda i: (i, COLB_GA)), pl.BlockSpec((tm, D), lambda i: (i, COLB_GC)),
                  mat, mat, mat, row, pl.BlockSpec((None, 1, D), lambda i: (i // nb, 0, 0))],
        out_specs=[row, row, row, row, row],
        out_shape=[act, act, act, act, jax.ShapeDtypeStruct((T, D), F32)],
        compiler_params=_params(1),
    )(o, z, proj, proj, w_ao, w_co, w_out, x, gate)


def _final_loss(x, gf, target):
    T, D = x.shape
    tm = _tile(T, 512)

    def body(x_ref, gf_ref, t_ref, dx_ref, lp_ref, dgf_ref):
        first = pl.program_id(0) == 0
        xv = x_ref[...]
        gfv = gf_ref[...]
        r = lax.rsqrt(jnp.mean(xv * xv, axis=-1, keepdims=True) + EPS)
        xh = xv * r
        err = xh * gfv - t_ref[...]
        _accumulate(lp_ref, first, jnp.sum(err * err, axis=0, keepdims=True))
        dy = err * (1.0 / D)
        _accumulate(dgf_ref, first, jnp.sum(dy * xh, axis=0, keepdims=True))
        dxh = dy * gfv
        dx_ref[...] = r * (dxh - xh * jnp.mean(dxh * xh, axis=-1, keepdims=True))

    row = pl.BlockSpec((tm, D), lambda i: (i, 0))
    vec = pl.BlockSpec((1, D), lambda i: (0, 0))
    return pl.pallas_call(
        body, name="final_loss", grid=(T // tm,),
        in_specs=[row, vec, row], out_specs=[row, vec, vec],
        out_shape=[jax.ShapeDtypeStruct((T, D), F32), jax.ShapeDtypeStruct((1, D), F32),
                   jax.ShapeDtypeStruct((1, D), F32)],
        compiler_params=_params(1),
    )(x, gf, target)


def _merge_bwd(dxo, mo, gate, proj, ya, yc, w_out, w_ao, w_co, seq):
    T, D = dxo.shape
    B = T // seq
    tm = _tile(seq, 512)
    nb = seq // tm

    def body(dxo_ref, mo_ref, gate_ref, ga_ref, gc_ref, ya_ref, yc_ref, wout_ref, wao_ref, wco_ref,
             dmo_ref, dya_ref, dyc_ref, dga_ref, dgc_ref, do_ref, dz_ref, dgate_ref):
        dxo_v = dxo_ref[...]
        dmo = (gate_ref[...] * dxo_v).astype(BF16)
        dmo_ref[...] = dmo
        _accumulate(dgate_ref, pl.program_id(0) % nb == 0,
                    jnp.sum(mo_ref[...].astype(F32) * dxo_v, axis=0, keepdims=True))
        dm = _dot_nt(dmo, wout_ref[...])
        sa = _sigmoid(ga_ref[...].astype(F32))
        sc = _sigmoid(gc_ref[...].astype(F32))
        dya = (sa * dm).astype(BF16)
        dyc = (sc * dm).astype(BF16)
        dya_ref[...] = dya
        dyc_ref[...] = dyc
        dga_ref[...] = (dm * ya_ref[...].astype(F32) * (sa * (1.0 - sa))).astype(BF16)
        dgc_ref[...] = (dm * yc_ref[...].astype(F32) * (sc * (1.0 - sc))).astype(BF16)
        do_ref[...] = _dot_nt(dya, wao_ref[...]).astype(BF16)
        dz_ref[...] = _dot_nt(dyc, wco_ref[...]).astype(BF16)

    row = pl.BlockSpec((tm, D), lambda i: (i, 0))
    mat = pl.BlockSpec((D, D), lambda i: (0, 0))
    per_b = pl.BlockSpec((None, 1, D), lambda i: (i // nb, 0, 0))
    act = jax.ShapeDtypeStruct((T, D), BF16)
    return pl.pallas_call(
        body, name="mix_merge_bwd", grid=(T // tm,),
        in_specs=[row, row, per_b, pl.BlockSpec((tm, D), lambda i: (i, COLB_GA)),
                  pl.BlockSpec((tm, D), lambda i: (i, COLB_GC)), row, row, mat, mat, mat],
        out_specs=[row] * 7 + [per_b],
        out_shape=[act] * 7 + [jax.ShapeDtypeStruct((B, 1, D), F32)],
        compiler_params=_params(1),
    )(dxo, mo, gate, proj, proj, ya, yc, w_out, w_ao, w_co)


def _attn_bwd(proj, sinks, o, do, lse, batch, seq):
    T = proj.shape[0]
    nblk = seq // BLOCK
    n_steps = batch * nblk
    pairs_per_kv = N_Q_HEADS // 2 // N_KV_HEADS

    def body(sink_ref, q_ref, kp_ref, ko_ref, vp_ref, vo_ref, o_ref, do_ref, lse_ref,
             dq_ref, dkp_ref, dko_ref, dvp_ref, dvo_ref, dsink_ref):
        lo = lax.broadcasted_iota(jnp.int32, (1, 2 * HEAD_DIM), 1) < HEAD_DIM
        sink_lane = lax.broadcasted_iota(jnp.int32, (1, 2 * HEAD_DIM), 1)
        valid = _band_valid(pl.program_id(1) > 0)
        k_ops = _band_operands(kp_ref, ko_ref, lo)
        v_ops = _band_operands(vp_ref, vo_ref, lo)
        dsink = jnp.zeros((1, 2 * HEAD_DIM), F32)
        dk_heads, dv_heads = [], []
        for kvh in range(N_KV_HEADS):
            dk_acc = jnp.zeros((2 * BLOCK, 2 * HEAD_DIM), F32)
            dv_acc = jnp.zeros((2 * BLOCK, 2 * HEAD_DIM), F32)
            for pp in range(pairs_per_kv):
                pair = kvh * pairs_per_kv + pp
                lanes = slice(pair * 2 * HEAD_DIM, (pair + 1) * 2 * HEAD_DIM)
                q2 = q_ref[:, lanes]
                do2 = do_ref[:, lanes]
                dd = do2.astype(F32) * o_ref[:, lanes].astype(F32)
                dq2 = jnp.zeros((BLOCK, 2 * HEAD_DIM), F32)
                for side in range(2):
                    head = 2 * pair + side
                    mine = lo if side == 0 else jnp.logical_not(lo)
                    sink = sink_ref[0, head]
                    lse_h = lse_ref[:, head:head + 1]
                    delta = jnp.sum(jnp.where(mine, dd, 0.0), axis=-1, keepdims=True)
                    s = _dot_nt(q2, k_ops[kvh][side]) * ATTN_SCALE
                    p = jnp.where(valid, jnp.exp(jnp.where(valid, s, MASK_VALUE) - lse_h), 0.0)
                    dp = _dot_nt(do2, v_ops[kvh][side])
                    ds = (p * (dp - delta) * ATTN_SCALE).astype(BF16)
                    dq2 = dq2 + _dot_nn(ds, k_ops[kvh][side])
                    dk_acc = dk_acc + jnp.where(mine, _dot_tn(ds, q2), 0.0)
                    dv_acc = dv_acc + jnp.where(mine, _dot_tn(p.astype(BF16), do2), 0.0)
                    dsink = dsink + jnp.where(sink_lane == head, -jnp.sum(jnp.exp(sink - lse_h) * delta), 0.0)
                dq_ref[:, lanes] = dq2.astype(BF16)
            dk_heads.append(dk_acc + pltpu.roll(dk_acc, HEAD_DIM, 1))
            dv_heads.append(dv_acc + pltpu.roll(dv_acc, HEAD_DIM, 1))
        dk = jnp.where(lo, dk_heads[0], dk_heads[1])
        dv = jnp.where(lo, dv_heads[0], dv_heads[1])
        dkp_ref[...] = dk[:BLOCK]
        dko_ref[...] = dk[BLOCK:]
        dvp_ref[...] = dv[:BLOCK]
        dvo_ref[...] = dv[BLOCK:]
        dsink_ref[...] = dsink

    def own(b, n):
        return (b * nblk + n, 0)

    row = pl.BlockSpec((BLOCK, D_MODEL), own)
    kv = pl.BlockSpec((BLOCK, 2 * HEAD_DIM), own)
    kv_shape = jax.ShapeDtypeStruct((T, 2 * HEAD_DIM), F32)
    return pl.pallas_call(
        body, name="attn_bwd", grid=(batch, nblk),
        in_specs=[SMEM_SPEC] + _attn_specs(nblk) + [row, row, pl.BlockSpec((BLOCK, N_Q_HEADS), own)],
        out_specs=[row, kv, kv, kv, kv, pl.BlockSpec((None, 1, 2 * HEAD_DIM), lambda b, n: (b * nblk + n, 0, 0))],
        out_shape=[jax.ShapeDtypeStruct((T, D_MODEL), BF16), kv_shape, kv_shape, kv_shape, kv_shape,
                   jax.ShapeDtypeStruct((n_steps, 1, 2 * HEAD_DIM), F32)],
        compiler_params=_params(2),
    )(sinks, proj, proj, proj, proj, proj, o, do, lse)


def _conv_bwd(proj, dz, ydw, w_dw, ln_g, ln_b, batch, seq):
    T = proj.shape[0]
    ts = _tile(seq, 256)
    nt = seq // ts
    per_tile = ts // CONV_PAD
    shift = CONV_PAD - (CONV_WIDTH - 1)

    def body(ca_ref, cb_ref, cah_ref, cbh_ref, dz_ref, dzn_ref, y_ref, yn_ref, w_ref, g_ref, beta_ref,
             dca_ref, dcb_ref, dw_ref, db_ref, dg_ref, dbeta_ref, upad, dypad):
        t = pl.program_id(1)
        first = (pl.program_id(0) == 0) & (t == 0)
        gv = g_ref[...]

        def ln_bwd(dzv, yv):
            lnh, rstd = _layernorm_stats(yv)
            ln = lnh * gv + beta_ref[...]
            sg = _sigmoid(ln)
            dln = dzv.astype(F32) * (sg * (1.0 + ln * (1.0 - sg)))
            dyh = dln * gv
            dy = rstd * (dyh - jnp.mean(dyh, axis=-1, keepdims=True)
                         - lnh * jnp.mean(dyh * lnh, axis=-1, keepdims=True))
            return dy, dln, lnh

        dy, dln, lnh = ln_bwd(dz_ref[...], y_ref[...])
        dy_next, _, _ = ln_bwd(dzn_ref[...], yn_ref[...])
        dypad[0:ts, :] = dy
        dypad[ts:, :] = jnp.where(t < nt - 1, dy_next, jnp.zeros_like(dy_next))
        _fill_upad(upad, ca_ref, cb_ref, cah_ref, cbh_ref, t)

        _accumulate(dg_ref, first, jnp.sum(dln * lnh, axis=0, keepdims=True))
        _accumulate(dbeta_ref, first, jnp.sum(dln, axis=0, keepdims=True))
        _accumulate(db_ref, first, jnp.sum(dy, axis=0, keepdims=True))

        @pl.when(first)
        def _():
            dw_ref[...] = jnp.zeros_like(dw_ref)

        du = jnp.zeros((ts, D_MODEL), F32)
        for k in range(CONV_WIDTH):
            du = du + w_ref[k:k + 1, :] * dypad[CONV_WIDTH - 1 - k:CONV_WIDTH - 1 - k + ts, :]
            dw_ref[k:k + 1, :] += jnp.sum(dy * upad[shift + k:shift + k + ts, :], axis=0, keepdims=True)
        cav = ca_ref[...].astype(F32)
        sb = _sigmoid(cb_ref[...].astype(F32))
        dca_ref[...] = (du * sb).astype(BF16)
        dcb_ref[...] = (du * cav * (sb * (1.0 - sb))).astype(BF16)

    def tile(b, t):
        return (b * nt + t, 0)

    def after(b, t):
        return (jnp.minimum((b * nt + t + 1) * per_tile, T // CONV_PAD - 1), 0)

    row = pl.BlockSpec((ts, D_MODEL), tile)
    halo = pl.BlockSpec((CONV_PAD, D_MODEL), after)
    vec = pl.BlockSpec((1, D_MODEL), lambda b, t: (0, 0))
    wspec = pl.BlockSpec((CONV_PAD, D_MODEL), lambda b, t: (0, 0))
    act = jax.ShapeDtypeStruct((T, D_MODEL), BF16)
    vec_shape = jax.ShapeDtypeStruct((1, D_MODEL), F32)
    return pl.pallas_call(
        body, name="conv_bwd", grid=(batch, nt),
        in_specs=_conv_specs(ts, nt) + [row, halo, row, halo, wspec, vec, vec],
        out_specs=[row, row, wspec, vec, vec, vec],
        out_shape=[act, act, jax.ShapeDtypeStruct((CONV_PAD, D_MODEL), F32), vec_shape, vec_shape, vec_shape],
        scratch_shapes=[pltpu.VMEM((ts + CONV_PAD, D_MODEL), F32), pltpu.VMEM((ts + CONV_PAD, D_MODEL), F32)],
        compiler_params=_params(2),
    )(proj, proj, proj, proj, dz, dz, ydw, ydw, w_dw, ln_g, ln_b)


def _in_proj_bwd(dproj, w_in_g, x, gn, sc, dxo, seq):
    T, D = x.shape
    J, _, W = w_in_g.shape
    B = T // seq
    tm = _tile(seq, 512)
    nb = seq // tm

    def body(dp_ref, w_ref, x_ref, gn_ref, sc_ref, dxo_ref, dx_ref, dsc_ref, dsh_ref, dgn_ref, acc):
        i = pl.program_id(0)
        j = pl.program_id(1)

        @pl.when(j == 0)
        def _():
            acc[...] = jnp.zeros_like(acc)

        acc[...] += _dot_nt(dp_ref[...], w_ref[...])

        @pl.when(j == J - 1)
        def _():
            _norm_mod_bwd(acc[...], x_ref[...], gn_ref[...], sc_ref[...], dxo_ref[...],
                          i % nb == 0, i == 0, dx_ref, dsc_ref, dsh_ref, dgn_ref)

    row = pl.BlockSpec((tm, D), lambda i, j: (i, 0))
    vec = pl.BlockSpec((1, D), lambda i, j: (0, 0))
    per_b = pl.BlockSpec((None, 1, D), lambda i, j: (i // nb, 0, 0))
    per_b_shape = jax.ShapeDtypeStruct((B, 1, D), F32)
    return pl.pallas_call(
        body, name="mix_in_proj_bwd", grid=(T // tm, J),
        in_specs=[pl.BlockSpec((None, tm, W), lambda i, j: (j, i, 0)),
                  pl.BlockSpec((None, D, W), lambda i, j: (j, 0, 0)), row, vec, per_b, row],
        out_specs=[row, per_b, per_b, vec],
        out_shape=[jax.ShapeDtypeStruct((T, D), F32), per_b_shape, per_b_shape, jax.ShapeDtypeStruct((1, D), F32)],
        scratch_shapes=[pltpu.VMEM((tm, D), F32)],
        compiler_params=_params(2),
    )(dproj, w_in_g, x, gn, sc, dxo)


def _ada_fwd(c_all, w_ada, b_cols):
    nbatch, D = c_all.shape
    N = w_ada.shape[1]
    tn = _tile(N, 768)

    def body(c_ref, w_ref, b_ref, o_ref):
        cv = c_ref[...]
        act = (cv * _sigmoid(cv)).astype(BF16)
        o_ref[...] = _dot_nn(act, w_ref[...].astype(BF16)) + b_ref[...]

    return pl.pallas_call(
        body, name="ada_fwd", grid=(N // tn,),
        in_specs=[pl.BlockSpec((nbatch, D), lambda j: (0, 0)), pl.BlockSpec((D, tn), lambda j: (0, j)),
                  pl.BlockSpec((1, tn), lambda j: (0, j))],
        out_specs=pl.BlockSpec((nbatch, tn), lambda j: (0, j)),
        out_shape=jax.ShapeDtypeStruct((nbatch, N), F32),
        compiler_params=_params(1),
    )(c_all, w_ada, b_cols)


def _adamw(w, g, m, v):
    m = ADAM_B1 * m + (1.0 - ADAM_B1) * g
    v = ADAM_B2 * v + (1.0 - ADAM_B2) * (g * g)
    m_hat = m / (1.0 - ADAM_B1 ** ADAM_STEP)
    v_hat = v / (1.0 - ADAM_B2 ** ADAM_STEP)
    delta = -ADAM_LR * (m_hat / (jnp.sqrt(v_hat) + ADAM_EPS) + ADAM_WD * w)
    return delta, m, v


def _adam_call(w, g, m, v, name):
    R, C = w.shape
    tr = R if R % 8 else _tile(R, 256)

    def body(w_ref, g_ref, m_ref, v_ref, d_ref, mo_ref, vo_ref):
        d, mn, vn = _adamw(w_ref[...], g_ref[...], m_ref[...], v_ref[...])
        d_ref[...] = d
        mo_ref[...] = mn
        vo_ref[...] = vn

    blk = pl.BlockSpec((tr, C), lambda i: (i, 0))
    shape = jax.ShapeDtypeStruct((R, C), F32)
    return pl.pallas_call(
        body, name=name, grid=(R // tr,), in_specs=[blk] * 4, out_specs=[blk] * 3, out_shape=[shape] * 3,
        compiler_params=_params(1),
    )(w, g, m, v)


def _ada_adam(c_act_t, dmod_cols, w, m, v):
    R, C = w.shape
    nbatch = c_act_t.shape[1]
    tr = _tile(R, 128)

    def body(ct_ref, dm_ref, w_ref, m_ref, v_ref, g_ref, d_ref, mo_ref, vo_ref):
        cv = ct_ref[...]
        g = _dot_nn((cv * _sigmoid(cv)).astype(BF16), dm_ref[...].astype(BF16))
        g_ref[...] = g
        d, mn, vn = _adamw(w_ref[...], g, m_ref[...], v_ref[...])
        d_ref[...] = d
        mo_ref[...] = mn
        vo_ref[...] = vn

    blk = pl.BlockSpec((tr, C), lambda i: (i, 0))
    shape = jax.ShapeDtypeStruct((R, C), F32)
    return pl.pallas_call(
        body, name="ada_adam", grid=(R // tr,),
        in_specs=[pl.BlockSpec((tr, nbatch), lambda i: (i, 0)), pl.BlockSpec((nbatch, C), lambda i: (0, 0)),
                  blk, blk, blk],
        out_specs=[blk] * 4, out_shape=[shape] * 4,
        compiler_params=_params(1),
    )(c_act_t, dmod_cols, w, m, v)


def _small_adam(gathered, w, m, v, rows_b0, rows_b1, rows_vec):
    _, P, D = gathered.shape
    R = w.shape[0]

    def body(ga_ref, w_ref, m_ref, v_ref, sum_ref, g_ref, d_ref, mo_ref, vo_ref):
        total = ga_ref[0]
        for dev in range(1, N_DEV):
            total = total + ga_ref[dev]
        sum_ref[...] = total
        g_ref[...] = jnp.zeros_like(g_ref)
        g_ref[0:N_MOD, :] = (sum_ref[rows_b0:rows_b0 + N_MOD, :] + sum_ref[rows_b1:rows_b1 + N_MOD, :])
        g_ref[N_MOD:N_MOD + 8, :] = sum_ref[rows_vec:rows_vec + 8, :]
        d, mn, vn = _adamw(w_ref[...], g_ref[...], m_ref[...], v_ref[...])
        d_ref[...] = d
        mo_ref[...] = mn
        vo_ref[...] = vn

    shape = jax.ShapeDtypeStruct((R, D), F32)
    return pl.pallas_call(
        body, name="small_adam",
        in_specs=[VMEM_SPEC] * 4, out_specs=[VMEM_SPEC] * 5,
        out_shape=[jax.ShapeDtypeStruct((P, D), F32), shape, shape, shape, shape],
        compiler_params=pltpu.CompilerParams(vmem_limit_bytes=VMEM_LIMIT),
    )(gathered, w, m, v)


def _position():
    x, y, c = lax.axis_index("x"), lax.axis_index("y"), lax.axis_index("c")
    return x, y, c


CHIP_FLIPS = ((1, 0), (0, 1), (1, 1))


def _flip(v, f):
    return 1 - v if f else v


def _gather8(v, name):
    A, W = v.shape
    flips = [(fx, fy, fc) for fx in (0, 1) for fy in (0, 1) for fc in (0, 1) if (fx, fy, fc) != (0, 0, 0)]

    def body(v_ref, out_ref, send_sems, recv_sems, local_sem):
        x, y, c = _position()
        me = 4 * x + 2 * y + c
        mine = pltpu.make_async_copy(v_ref, out_ref.at[me], local_sem)
        mine.start()

        def copy(k, block, to):
            return pltpu.make_async_remote_copy(src_ref=v_ref, dst_ref=out_ref.at[block], send_sem=send_sems.at[k],
                                                recv_sem=recv_sems.at[k], device_id=to, device_id_type=MESH)

        peers = [(_flip(x, fx), _flip(y, fy), _flip(c, fc)) for fx, fy, fc in flips]
        sends = [copy(k, me, peer) for k, peer in enumerate(peers)]
        for cp in sends:
            cp.start()
        for k, (px, py, pc) in enumerate(peers):
            copy(k, 4 * px + 2 * py + pc, (px, py, pc)).wait_recv()
        for cp in sends:
            cp.wait_send()
        mine.wait()

    return pl.pallas_call(
        body, name=name, in_specs=[VMEM_SPEC], out_specs=VMEM_SPEC,
        out_shape=jax.ShapeDtypeStruct((N_DEV, A, W), v.dtype),
        scratch_shapes=[pltpu.SemaphoreType.DMA((N_DEV - 1,)), pltpu.SemaphoreType.DMA((N_DEV - 1,)),
                        pltpu.SemaphoreType.DMA],
    )(v)


def _mod_exchange(part):
    _, A, W = part.shape

    def body(p_ref, out_ref, send_sems, recv_sems, local_sem):
        x, y, c = _position()
        me = 4 * x + 2 * y + c
        chip = 2 * x + y
        mine = pltpu.make_async_copy(p_ref.at[me], out_ref.at[chip], local_sem)
        mine.start()
        peers = [(_flip(x, fx), _flip(y, fy)) for fx, fy in CHIP_FLIPS]
        sends = []
        for k, (px, py) in enumerate(peers):
            sends.append(pltpu.make_async_remote_copy(
                src_ref=p_ref.at[4 * px + 2 * py + c], dst_ref=out_ref.at[chip], send_sem=send_sems.at[k],
                recv_sem=recv_sems.at[k], device_id=(px, py, c), device_id_type=MESH))
        for cp in sends:
            cp.start()
        for k, (px, py) in enumerate(peers):
            pltpu.make_async_remote_copy(
                src_ref=p_ref.at[me], dst_ref=out_ref.at[2 * px + py], send_sem=send_sems.at[k],
                recv_sem=recv_sems.at[k], device_id=(px, py, c), device_id_type=MESH).wait_recv()
        for cp in sends:
            cp.wait_send()
        mine.wait()

    return pl.pallas_call(
        body, name="mod_exchange", in_specs=[VMEM_SPEC], out_specs=VMEM_SPEC,
        out_shape=jax.ShapeDtypeStruct((N_CHIP, A, W), part.dtype),
        scratch_shapes=[pltpu.SemaphoreType.DMA((3,)), pltpu.SemaphoreType.DMA((3,)), pltpu.SemaphoreType.DMA],
    )(part)


def _weight_gather(shards):
    n = len(shards)

    def body(*refs):
        w_refs, out_refs = refs[:n], refs[n:2 * n]
        send_sems, recv_sems, local_sems = refs[2 * n:]
        x, y, c = _position()
        chip = 2 * x + y
        sibling = (x, y, 1 - c)
        peers = [(_flip(x, fx), _flip(y, fy)) for fx, fy in CHIP_FLIPS]

        def half(ref, i, which):
            rows = shards[i].shape[0] // 2
            return ref.at[pl.ds(which * rows, rows), :]

        local = [pltpu.make_async_copy(w_refs[i], out_refs[i].at[chip], local_sems.at[i]) for i in range(n)]
        for cp in local:
            cp.start()

        def ici(i, k, src_chip_ref, dst_chip, to):
            return pltpu.make_async_remote_copy(
                src_ref=src_chip_ref, dst_ref=half(out_refs[i].at[dst_chip], i, c),
                send_sem=send_sems.at[3 * i + k], recv_sem=recv_sems.at[3 * i + k], device_id=to, device_id_type=MESH)

        def d2d(i, k, src_chip, which):
            place = half(out_refs[i].at[src_chip], i, which)
            return pltpu.make_async_remote_copy(
                src_ref=place, dst_ref=place, send_sem=send_sems.at[3 * n + 3 * i + k],
                recv_sem=recv_sems.at[3 * n + 3 * i + k], device_id=sibling, device_id_type=MESH)

        first = [ici(i, k, half(w_refs[i], i, c), chip, (px, py, c))
                 for i in range(n) for k, (px, py) in enumerate(peers)]
        for cp in first:
            cp.start()
        passed = []
        for i in range(n):
            for k, (px, py) in enumerate(peers):
                ici(i, k, half(w_refs[i], i, c), 2 * px + py, (px, py, c)).wait_recv()
                cp = d2d(i, k, 2 * px + py, c)
                cp.start()
                passed.append(cp)
        for i in range(n):
            for k, (px, py) in enumerate(peers):
                d2d(i, k, 2 * px + py, 1 - c).wait_recv()
        for cp in first + passed:
            cp.wait_send()
        for cp in local:
            cp.wait()

    return pl.pallas_call(
        body, name="weight_gather", in_specs=[ANY] * n, out_specs=[ANY] * n,
        out_shape=[jax.ShapeDtypeStruct((N_CHIP,) + s.shape, s.dtype) for s in shards],
        scratch_shapes=[pltpu.SemaphoreType.DMA((6 * n,)), pltpu.SemaphoreType.DMA((6 * n,)),
                        pltpu.SemaphoreType.DMA((n,))],
    )(*shards)


def _sibling_swap_halves(grads16):
    n = len(grads16)

    def body(*refs):
        g_refs, out_refs = refs[:n], refs[n:2 * n]
        send_sems, recv_sems = refs[2 * n:]
        x, y, c = _position()
        copies = [pltpu.make_async_remote_copy(
            src_ref=g_refs[i].at[1 - c], dst_ref=out_refs[i], send_sem=send_sems.at[i], recv_sem=recv_sems.at[i],
            device_id=(x, y, 1 - c), device_id_type=MESH) for i in range(n)]
        for cp in copies:
            cp.start()
        for cp in copies:
            cp.wait()

    return pl.pallas_call(
        body, name="grad_swap_halves", in_specs=[ANY] * n, out_specs=[ANY] * n,
        out_shape=[jax.ShapeDtypeStruct(g.shape[1:], g.dtype) for g in grads16],
        scratch_shapes=[pltpu.SemaphoreType.DMA((n,)), pltpu.SemaphoreType.DMA((n,))],
    )(*grads16)


def _pair_sum(g32, recv, core, name):
    _, J, r, C = g32.shape

    def body(core_ref, g_ref, r_ref, o_ref):
        o_ref[...] = (g_ref[...] + r_ref[...].astype(F32)).astype(BF16)

    return pl.pallas_call(
        body, name=name,
        grid_spec=pltpu.PrefetchScalarGridSpec(
            num_scalar_prefetch=1, grid=(J,),
            in_specs=[pl.BlockSpec((None, None, r, C), lambda j, core_ref: (core_ref[0], j, 0, 0)),
                      pl.BlockSpec((None, r, C), lambda j, core_ref: (j, 0, 0))],
            out_specs=pl.BlockSpec((None, r, C), lambda j, core_ref: (j, 0, 0))),
        out_shape=jax.ShapeDtypeStruct((J, r, C), BF16),
        compiler_params=_params(1),
    )(core, g32, recv)


def _chip_exchange(pairs):
    n = len(pairs)

    def body(*refs):
        p_refs, out_refs = refs[:n], refs[n:2 * n]
        send_sems, recv_sems = refs[2 * n:]
        x, y, c = _position()
        peers = [(_flip(x, fx), _flip(y, fy)) for fx, fy in CHIP_FLIPS]
        copies = [pltpu.make_async_remote_copy(
            src_ref=p_refs[i].at[2 * px + py], dst_ref=out_refs[i].at[k], send_sem=send_sems.at[3 * i + k],
            recv_sem=recv_sems.at[3 * i + k], device_id=(px, py, c), device_id_type=MESH)
            for i in range(n) for k, (px, py) in enumerate(peers)]
        for cp in copies:
            cp.start()
        for cp in copies:
            cp.wait()

    return pl.pallas_call(
        body, name="grad_chip_exchange", in_specs=[ANY] * n, out_specs=[ANY] * n,
        out_shape=[jax.ShapeDtypeStruct((3,) + p.shape[1:], p.dtype) for p in pairs],
        scratch_shapes=[pltpu.SemaphoreType.DMA((3 * n,)), pltpu.SemaphoreType.DMA((3 * n,))],
    )(*pairs)


def _chip_sum(g32, recv_sib, recv_chips, core_chip, name):
    _, J, r, C = g32.shape

    def body(idx_ref, g_ref, s_ref, o_ref_in, o_ref):
        total = g_ref[...] + s_ref[...].astype(F32)
        for k in range(3):
            total = total + o_ref_in[k].astype(F32)
        o_ref[...] = total

    return pl.pallas_call(
        body, name=name,
        grid_spec=pltpu.PrefetchScalarGridSpec(
            num_scalar_prefetch=1, grid=(1,),
            in_specs=[pl.BlockSpec((None, None, r, C), lambda i, idx: (idx[0], idx[1], 0, 0)),
                      pl.BlockSpec((None, r, C), lambda i, idx: (idx[1], 0, 0)),
                      pl.BlockSpec((3, r, C), lambda i, idx: (0, 0, 0))],
            out_specs=pl.BlockSpec((r, C), lambda i, idx: (0, 0))),
        out_shape=jax.ShapeDtypeStruct((r, C), F32),
        compiler_params=_params(1),
    )(core_chip, g32, recv_sib, recv_chips)


def _sibling_join_halves(halves):
    n = len(halves)

    def body(*refs):
        h_refs, out_refs = refs[:n], refs[n:2 * n]
        send_sems, recv_sems, local_sems = refs[2 * n:]
        x, y, c = _position()
        local = [pltpu.make_async_copy(h_refs[i], out_refs[i].at[c], local_sems.at[i]) for i in range(n)]
        for cp in local:
            cp.start()
        sends = [pltpu.make_async_remote_copy(
            src_ref=h_refs[i], dst_ref=out_refs[i].at[c], send_sem=send_sems.at[i], recv_sem=recv_sems.at[i],
            device_id=(x, y, 1 - c), device_id_type=MESH) for i in range(n)]
        for cp in sends:
            cp.start()
        for i in range(n):
            pltpu.make_async_remote_copy(
                src_ref=h_refs[i], dst_ref=out_refs[i].at[1 - c], send_sem=send_sems.at[i], recv_sem=recv_sems.at[i],
                device_id=(x, y, 1 - c), device_id_type=MESH).wait_recv()
        for cp in sends:
            cp.wait_send()
        for cp in local:
            cp.wait()

    return pl.pallas_call(
        body, name="grad_join_halves", in_specs=[ANY] * n, out_specs=[ANY] * n,
        out_shape=[jax.ShapeDtypeStruct((2,) + h.shape, h.dtype) for h in halves],
        scratch_shapes=[pltpu.SemaphoreType.DMA((n,)), pltpu.SemaphoreType.DMA((n,)), pltpu.SemaphoreType.DMA((n,))],
    )(*halves)


BIG_WEIGHTS = ("ffn1_w_gate", "ffn1_w_up", "ffn1_w_down", "w_in", "w_attn_o", "w_conv_o", "w_out",
               "ffn2_w_gate", "ffn2_w_up", "ffn2_w_down")
VECTORS = ("norm_ffn1_g", "norm_mix_g", "conv_b_dw", "conv_ln_g", "conv_ln_b", "norm_ffn2_g", "final_norm_g")
ROW_DMOD0, ROW_DMOD1, ROW_VEC, ROW_SINK, ROW_CONVW, SMALL_ROWS = 0, 16, 33, 40, 41, 72


def _local_grads(x, target, mod, w, small, seq):
    T, D = x.shape
    B = T // seq
    mods = [mod[:, k][:, None, :] for k in range(N_MOD)]
    sh1, sc1, g1, sh2, sc2, g2, sh3, sc3, g3 = mods

    h1, a1, u1, f1, x1 = _ffn_fwd(x, small["norm_ffn1_g"], sc1, sh1, g1, w["ffn1_w_gate"], w["ffn1_w_up"],
                                  w["ffn1_w_down"], seq, "ffn1_fwd")
    h2, proj = _in_proj(x1, small["norm_mix_g"], sc2, sh2, w["w_in_cols"], seq)
    o, lse = _attn_fwd(proj, small["attn_sinks"], B, seq)
    ydw, z = _conv_fwd(proj, small["conv_w_dw"], small["conv_b_dw"], small["conv_ln_g"], small["conv_ln_b"], B, seq)
    ya, yc, merged, mo, x2 = _merge(o, z, proj, w["w_attn_o"], w["w_conv_o"], w["w_out"], x1, g2, seq)
    h3, a3, u3, f3, x3 = _ffn_fwd(x2, small["norm_ffn2_g"], sc3, sh3, g3, w["ffn2_w_gate"], w["ffn2_w_up"],
                                  w["ffn2_w_down"], seq, "ffn2_fwd")
    dx3, loss_parts, d_final_g = _final_loss(x3, small["final_norm_g"], target)

    grads = {}

    def ffn_backward(prefix, dxo, xin, h, a, u, f, gn, sc, gate):
        da, du, s, df, dx, dgate, dsc, dsh, dgn = _ffn_bwd(
            dxo, xin, f, a, u, gn, sc, gate, w[prefix + "_w_gate"], w[prefix + "_w_up"], w[prefix + "_w_down"],
            seq, prefix + "_bwd")
        grads[prefix + "_w_gate"] = _wgrad(h, _spec_rows(D), da, _spec_chip_major(FF_SHARD), D, FF_SHARD, T,
                                           prefix + "_dw_gate")
        grads[prefix + "_w_up"] = _wgrad(h, _spec_rows(D), du, _spec_chip_major(FF_SHARD), D, FF_SHARD, T,
                                         prefix + "_dw_up")
        grads[prefix + "_w_down"] = _wgrad(s, _spec_chip_major(FF_SHARD), df, _spec_rows(D), FF_SHARD, D, T,
                                           prefix + "_dw_down")
        return dx, dgate, dsc, dsh, dgn

    dx2, dg3, dsc3, dsh3, d_gn3 = ffn_backward("ffn2", dx3, x2, h3, a3, u3, f3, small["norm_ffn2_g"], sc3, g3)

    dmo, dya, dyc, dga, dgc, do, dz, dg2 = _merge_bwd(dx2, mo, g2, proj, ya, yc, w["w_out"], w["w_attn_o"],
                                                      w["w_conv_o"], seq)
    shard = D // N_CHIP
    grads["w_out"] = _wgrad(merged, _spec_col_block(shard), dmo, _spec_rows(D), shard, D, T, "dw_out")
    grads["w_attn_o"] = _wgrad(o, _spec_col_block(shard), dya, _spec_rows(D), shard, D, T, "dw_attn_o")
    grads["w_conv_o"] = _wgrad(z, _spec_col_block(shard), dyc, _spec_rows(D), shard, D, T, "dw_conv_o")
    dq, dkp, dko, dvp, dvo, dsink_steps = _attn_bwd(proj, small["attn_sinks"], o, do, lse, B, seq)
    dca, dcb, d_conv_w, d_conv_b, d_ln_g, d_ln_b = _conv_bwd(proj, dz, ydw, small["conv_w_dw"], small["conv_ln_g"],
                                                              small["conv_ln_b"], B, seq)

    def band_sum(own, prev):
        prev = prev.reshape(B, seq // BLOCK, BLOCK, 2 * HEAD_DIM)
        moved = jnp.concatenate([prev[:, 1:], jnp.zeros_like(prev[:, :1])], axis=1)
        return (own + moved.reshape(T, 2 * HEAD_DIM)).astype(BF16)

    dproj = jnp.concatenate([dq, band_sum(dko, dkp), band_sum(dvo, dvp), dca, dcb, dga, dgc], axis=1)
    dproj = dproj.reshape(T, N_CHIP, IN_SHARD).transpose(1, 0, 2)
    grads["w_in"] = _wgrad(h2, _spec_rows(D), dproj, _spec_chip_major(IN_SHARD), D, IN_SHARD, T, "dw_in")
    dx1, dsc2, dsh2, d_gn2 = _in_proj_bwd(dproj, w["w_in"], x1, small["norm_mix_g"], sc2, dx2, seq)

    dx0, dg1, dsc1, dsh1, d_gn1 = ffn_backward("ffn1", dx1, x, h1, a1, u1, f1, small["norm_ffn1_g"], sc1, g1)

    dmod = jnp.concatenate([dsh1, dsc1, dg1, dsh2, dsc2, dg2, dsh3, dsc3, dg3], axis=1)
    d_sinks = jnp.sum(dsink_steps, axis=0)
    vec_grads = {"norm_ffn1_g": d_gn1, "norm_mix_g": d_gn2, "conv_b_dw": d_conv_b, "conv_ln_g": d_ln_g,
                 "conv_ln_b": d_ln_b, "norm_ffn2_g": d_gn3, "final_norm_g": d_final_g}
    return loss_parts, dx0, grads, dmod, vec_grads, d_sinks, d_conv_w


def kernel(x, c, w_ada, b_ada, norm_ffn1_g, ffn1_w_gate, ffn1_w_up, ffn1_w_down, norm_mix_g, w_in, attn_sinks, w_attn_o, conv_w_dw, conv_b_dw, conv_ln_g, conv_ln_b, w_conv_o, w_out, norm_ffn2_g, ffn2_w_gate, ffn2_w_up, ffn2_w_down, final_norm_g, loss_target, m_w_ada, m_b_ada, m_norm_ffn1_g, m_ffn1_w_gate, m_ffn1_w_up, m_ffn1_w_down, m_norm_mix_g, m_w_in, m_attn_sinks, m_w_attn_o, m_conv_w_dw, m_conv_b_dw, m_conv_ln_g, m_conv_ln_b, m_w_conv_o, m_w_out, m_norm_ffn2_g, m_ffn2_w_gate, m_ffn2_w_up, m_ffn2_w_down, m_final_norm_g, v_w_ada, v_b_ada, v_norm_ffn1_g, v_ffn1_w_gate, v_ffn1_w_up, v_ffn1_w_down, v_norm_mix_g, v_w_in, v_attn_sinks, v_w_attn_o, v_conv_w_dw, v_conv_b_dw, v_conv_ln_g, v_conv_ln_b, v_w_conv_o, v_w_out, v_norm_ffn2_g, v_ffn2_w_gate, v_ffn2_w_up, v_ffn2_w_down, v_final_norm_g):
    args = dict(locals())
    B, seq, D = x.shape
    T = B * seq
    xi, yi, ci = _position()
    chip = 2 * xi + yi
    dev = 4 * xi + 2 * yi + ci
    big = {n: args[n][0] for n in BIG_WEIGHTS}
    final_g = final_norm_g[None, :]
    vec_w = {n: (args[n] if n != "final_norm_g" else final_g) for n in VECTORS}

    conv_cols = D // N_CHIP
    conv_flat = jnp.pad(conv_w_dw[0].reshape(-1), (0, 8 * D - CONV_WIDTH * conv_cols)).reshape(8, D)
    first = _gather8(jnp.concatenate([jnp.pad(c, ((0, 8 - B), (0, 0))), conv_flat], axis=0), "gather_c")
    c_all = first[:, :B].reshape(N_DEV * B, D)
    conv_taps = first[::2, 8:].reshape(N_CHIP, 8 * D)[:, :CONV_WIDTH * conv_cols]
    conv_taps = conv_taps.reshape(N_CHIP, CONV_WIDTH, conv_cols).transpose(1, 0, 2).reshape(CONV_WIDTH, D)
    conv_taps = jnp.pad(conv_taps, ((0, CONV_PAD - CONV_WIDTH), (0, 0)))

    ada_cols = w_ada.shape[2]
    b_cols = lax.dynamic_slice(b_ada, (0, chip * ada_cols), (1, ada_cols))
    mod_part = _ada_fwd(c_all, w_ada[0], b_cols).reshape(N_DEV, B, ada_cols)
    mod = _mod_exchange(mod_part).transpose(1, 0, 2).reshape(B, N_MOD, D)

    gathered = dict(zip(BIG_WEIGHTS, _weight_gather([big[n].astype(BF16) for n in BIG_WEIGHTS])))
    w_in_full = gathered["w_in"].transpose(1, 0, 2).reshape(D, IN_WIDTH)
    q_end, k_end, v_end = D, D + 2 * HEAD_DIM, D + 4 * HEAD_DIM
    gathered["w_in_cols"] = jnp.concatenate(
        [w_in_full[:, :q_end], w_in_full[:, v_end:], w_in_full[:, q_end:k_end], w_in_full[:, k_end:v_end]], axis=1)
    for n in ("w_attn_o", "w_conv_o", "w_out"):
        gathered[n] = gathered[n].reshape(D, D)

    small = dict(vec_w)
    small["attn_sinks"] = attn_sinks
    small["conv_w_dw"] = conv_taps

    loss_parts, dx, grads, dmod, vec_grads, d_sinks, d_conv_w = _local_grads(
        x.reshape(T, D), loss_target.reshape(T, D), mod, gathered, small, seq)

    loss = lax.psum((0.5 / D) * jnp.sum(loss_parts), ("x", "y", "c"))
    grad_x = dx.reshape(B, seq, D)

    core_idx = jnp.reshape(ci, (1,)).astype(jnp.int32)
    core_chip = jnp.stack([ci, chip]).astype(jnp.int32)
    from_sibling = _sibling_swap_halves([grads[n][1] for n in BIG_WEIGHTS])
    pair_sums = [_pair_sum(grads[n][0], r, core_idx, "pair_sum_" + n) for n, r in zip(BIG_WEIGHTS, from_sibling)]
    from_chips = _chip_exchange(pair_sums)
    halves = [_chip_sum(grads[n][0], rs, rc, core_chip, "chip_sum_" + n)
              for n, rs, rc in zip(BIG_WEIGHTS, from_sibling, from_chips)]
    reduced = dict(zip(BIG_WEIGHTS, _sibling_join_halves(halves)))

    out = {}
    for n in BIG_WEIGHTS:
        shape = args[n].shape
        g = reduced[n].reshape(shape[1:])
        d, mn, vn = _adam_call(big[n], g, args["m_" + n][0], args["v_" + n][0], "adam_" + n)
        out[n] = tuple(t.reshape(shape) for t in (g, d, mn, vn))

    block = jnp.zeros((SMALL_ROWS, D), F32)
    block = block.at[ROW_DMOD0:ROW_DMOD0 + N_MOD].set(dmod[0]).at[ROW_DMOD1:ROW_DMOD1 + N_MOD].set(dmod[1])
    block = block.at[ROW_VEC:ROW_VEC + len(VECTORS)].set(jnp.concatenate([vec_grads[n] for n in VECTORS], axis=0))
    block = block.at[ROW_SINK, :2 * HEAD_DIM].set(d_sinks[0])
    block = block.at[ROW_CONVW:ROW_CONVW + CONV_WIDTH].set(d_conv_w[:CONV_WIDTH])
    small_all = _gather8(block, "gather_small_grads")

    def pack_small(prefix):
        rows = [args[prefix + "b_ada"].reshape(N_MOD, D)]
        rows += [args[prefix + n].reshape(1, D) for n in VECTORS]
        rows += [jnp.pad(args[prefix + "attn_sinks"], ((0, 0), (0, D - N_Q_HEADS)))]
        return jnp.pad(jnp.concatenate(rows, axis=0), ((0, 24 - N_MOD - len(VECTORS) - 1), (0, 0)))

    small_sum, sg, sd, sm, sv = _small_adam(small_all, pack_small(""), pack_small("m_"), pack_small("v_"),
                                           ROW_DMOD0, ROW_DMOD1, ROW_VEC)

    def unpack_small(t):
        res = {"b_ada": t[:N_MOD].reshape(1, N_MOD * D)}
        for k, n in enumerate(VECTORS):
            res[n] = t[N_MOD + k].reshape(args[n].shape)
        res["attn_sinks"] = t[N_MOD + len(VECTORS), :N_Q_HEADS].reshape(1, N_Q_HEADS)
        return res

    unpacked = [unpack_small(t) for t in (sg, sd, sm, sv)]
    for n in ("b_ada", "attn_sinks") + VECTORS:
        out[n] = tuple(u[n] for u in unpacked)

    conv_g = lax.dynamic_slice(small_sum, (ROW_CONVW, chip * conv_cols), (CONV_WIDTH, conv_cols))
    d, mn, vn = _adam_call(conv_w_dw[0], conv_g, m_conv_w_dw[0], v_conv_w_dw[0], "adam_conv_w_dw")
    out["conv_w_dw"] = tuple(t[None] for t in (conv_g, d, mn, vn))

    dmod_rows = jnp.stack([small_all[:, ROW_DMOD0:ROW_DMOD0 + N_MOD], small_all[:, ROW_DMOD1:ROW_DMOD1 + N_MOD]], axis=1)
    dmod_all = dmod_rows.reshape(N_DEV * B, N_MOD * D)
    dmod_cols = lax.dynamic_slice(dmod_all, (0, chip * ada_cols), (N_DEV * B, ada_cols))
    out["w_ada"] = tuple(t[None] for t in _ada_adam(c_all.T, dmod_cols, w_ada[0], m_w_ada[0], v_w_ada[0]))

    order = ("w_ada", "b_ada", "norm_ffn1_g", "ffn1_w_gate", "ffn1_w_up", "ffn1_w_down", "norm_mix_g", "w_in",
             "attn_sinks", "w_attn_o", "conv_w_dw", "conv_b_dw", "conv_ln_g", "conv_ln_b", "w_conv_o", "w_out",
             "norm_ffn2_g", "ffn2_w_gate", "ffn2_w_up", "ffn2_w_down", "final_norm_g")
    return (loss, grad_x, *[out[n][0] for n in order], *[out[n][1] for n in order],
            *[out[n][2] for n in order], *[out[n][3] for n in order])
```

```python
import functools

import jax
import jax.numpy as jnp
from jax import lax
from jax.experimental import pallas as pl
from jax.experimental.pallas import tpu as pltpu

F32 = jnp.float32
BF16 = jnp.bfloat16

D_MODEL = 1024
D_FF = 2816
N_CHIP = 4
N_DEV = 8
FF_SHARD = D_FF // N_CHIP
IN_WIDTH = 5376
IN_SHARD = IN_WIDTH // N_CHIP
HEAD_DIM = 64
N_Q_HEADS = 16
N_KV_HEADS = 2
BLOCK = 128
CONV_WIDTH = 31
CONV_PAD = 32
N_MOD = 9
EPS = 1e-6
FFN_RESIDUAL = 0.5
ATTN_SCALE = HEAD_DIM ** -0.5
MASK_VALUE = -1e30

ADAM_LR = 0.001
ADAM_B1 = 0.9
ADAM_B2 = 0.999
ADAM_EPS = 1e-08
ADAM_WD = 0.01
ADAM_STEP = 10

COLB_Q, COLB_CA, COLB_CB, COLB_GA, COLB_GC = 0, 1, 2, 3, 4
COLB_K, COLB_V = 40, 41
PROJ_TILE = 768

VMEM_LIMIT = 56 * 1024 * 1024
MESH = pl.DeviceIdType.MESH
ANY = pl.BlockSpec(memory_space=pl.ANY)
VMEM_SPEC = pl.BlockSpec(memory_space=pltpu.VMEM)
SMEM_SPEC = pl.BlockSpec(memory_space=pltpu.SMEM)


def _params(n_grid):
    return pltpu.CompilerParams(dimension_semantics=("arbitrary",) * n_grid, vmem_limit_bytes=VMEM_LIMIT)


def _tile(n, pref):
    t = min(n, pref)
    while n % t:
        t //= 2
    return t


def _sigmoid(v):
    return 1.0 / (1.0 + jnp.exp(-v))


def _dot_nn(a, b):
    return lax.dot_general(a, b, (((1,), (0,)), ((), ())), preferred_element_type=F32)


def _dot_nt(a, b):
    return lax.dot_general(a, b, (((1,), (1,)), ((), ())), preferred_element_type=F32)


def _dot_tn(a, b):
    return lax.dot_general(a, b, (((0,), (0,)), ((), ())), preferred_element_type=F32)


def _norm_mod(xv, gn, sc, sh):
    r = lax.rsqrt(jnp.mean(xv * xv, axis=-1, keepdims=True) + EPS)
    return ((xv * r) * gn) * (1.0 + sc) + sh


def _accumulate(ref, first, value):
    @pl.when(first)
    def _():
        ref[...] = value

    @pl.when(jnp.logical_not(first))
    def _():
        ref[...] += value


def _norm_mod_bwd(dh, xv, gn, sc, dxo, first_of_batch, first, dx_ref, dsc_ref, dsh_ref, dgn_ref):
    r = lax.rsqrt(jnp.mean(xv * xv, axis=-1, keepdims=True) + EPS)
    xh = xv * r
    _accumulate(dsh_ref, first_of_batch, jnp.sum(dh, axis=0, keepdims=True))
    _accumulate(dsc_ref, first_of_batch, jnp.sum(dh * (xh * gn), axis=0, keepdims=True))
    dn = dh * (1.0 + sc)
    _accumulate(dgn_ref, first, jnp.sum(dn * xh, axis=0, keepdims=True))
    dxh = dn * gn
    dx_ref[...] = dxo + r * (dxh - xh * jnp.mean(dxh * xh, axis=-1, keepdims=True))


CHIP_FLIPS = ((1, 0), (0, 1), (1, 1))


def _position():
    return lax.axis_index("x"), lax.axis_index("y"), lax.axis_index("c")


def _flip(v, f):
    return 1 - v if f else v


class _GatherComm:
    def __init__(self, bufs):
        n = len(bufs)
        self.n = n
        self.operands = list(bufs)
        self.out_shape = [jax.ShapeDtypeStruct(b.shape, b.dtype) for b in bufs]
        self.aliases = {i: i for i in range(n)}
        self.sems = [pltpu.SemaphoreType.DMA((6 * n,)), pltpu.SemaphoreType.DMA((6 * n,))]
        self.rows = [b.shape[1] // 2 for b in bufs]

    def _half(self, ref, i, which):
        return ref.at[pl.ds(which * self.rows[i], self.rows[i]), :]

    def _ici(self, cins, couts, sems, i, k, dst_chip, to):
        x, y, c = _position()
        return pltpu.make_async_remote_copy(
            src_ref=self._half(cins[i].at[2 * x + y], i, c), dst_ref=self._half(couts[i].at[dst_chip], i, c),
            send_sem=sems[0].at[3 * i + k], recv_sem=sems[1].at[3 * i + k], device_id=to, device_id_type=MESH)

    def _d2d(self, couts, sems, i, k, src_chip, which):
        x, y, c = _position()
        place = self._half(couts[i].at[src_chip], i, which)
        return pltpu.make_async_remote_copy(
            src_ref=place, dst_ref=place, send_sem=sems[0].at[3 * self.n + 3 * i + k],
            recv_sem=sems[1].at[3 * self.n + 3 * i + k], device_id=(x, y, 1 - c), device_id_type=MESH)

    def _peers(self):
        x, y, _ = _position()
        return [(_flip(x, fx), _flip(y, fy)) for fx, fy in CHIP_FLIPS]

    def start(self, cins, couts, sems):
        x, y, c = _position()
        for i in range(self.n):
            for k, (px, py) in enumerate(self._peers()):
                self._ici(cins, couts, sems, i, k, 2 * x + y, (px, py, c)).start()

    def finish(self, cins, couts, sems):
        _, _, c = _position()
        peers = self._peers()
        for i in range(self.n):
            for k, (px, py) in enumerate(peers):
                self._ici(cins, couts, sems, i, k, 2 * px + py, (px, py, c)).wait_recv()
                self._d2d(couts, sems, i, k, 2 * px + py, c).start()
        for i in range(self.n):
            for k, (px, py) in enumerate(peers):
                self._d2d(couts, sems, i, k, 2 * px + py, 1 - c).wait_recv()
        for i in range(self.n):
            for k, (px, py) in enumerate(peers):
                self._ici(cins, couts, sems, i, k, 2 * px + py, (px, py, c)).wait_send()
                self._d2d(couts, sems, i, k, 2 * px + py, c).wait_send()


class _ExchangeComm:
    def __init__(self, pairs):
        n = len(pairs)
        self.n = n
        self.operands = list(pairs)
        self.out_shape = [jax.ShapeDtypeStruct((3,) + p.shape[1:], p.dtype) for p in pairs]
        self.aliases = {}
        self.sems = [pltpu.SemaphoreType.DMA((3 * n,)), pltpu.SemaphoreType.DMA((3 * n,))]

    def _copies(self, cins, couts, sems):
        x, y, c = _position()
        peers = [(_flip(x, fx), _flip(y, fy)) for fx, fy in CHIP_FLIPS]
        return [pltpu.make_async_remote_copy(
            src_ref=cins[i].at[2 * px + py], dst_ref=couts[i].at[k], send_sem=sems[0].at[3 * i + k],
            recv_sem=sems[1].at[3 * i + k], device_id=(px, py, c), device_id_type=MESH)
            for i in range(self.n) for k, (px, py) in enumerate(peers)]

    def start(self, cins, couts, sems):
        for cp in self._copies(cins, couts, sems):
            cp.start()

    def finish(self, cins, couts, sems):
        for cp in self._copies(cins, couts, sems):
            cp.wait()


def _call(body, *, name, grid, in_specs, out_specs, out_shape, operands, scratch_shapes=(), comm=None):
    n_grid = len(grid)
    if comm is None:
        return pl.pallas_call(
            body, name=name, grid=grid, in_specs=list(in_specs), out_specs=list(out_specs), out_shape=list(out_shape),
            scratch_shapes=list(scratch_shapes), compiler_params=_params(n_grid))(*operands), ()
    counts = (len(in_specs), len(comm.operands), len(out_specs), len(comm.out_shape), len(scratch_shapes),
              len(comm.sems))

    def fused(*refs):
        parts, pos = [], 0
        for k in counts:
            parts.append(refs[pos:pos + k])
            pos += k
        ins, cins, outs, couts, scr, sems = parts
        first = functools.reduce(jnp.logical_and, [pl.program_id(d) == 0 for d in range(n_grid)])
        last = functools.reduce(jnp.logical_and, [pl.program_id(d) == grid[d] - 1 for d in range(n_grid)])

        @pl.when(first)
        def _():
            comm.start(cins, couts, sems)

        body(*ins, *outs, *scr)

        @pl.when(last)
        def _():
            comm.finish(cins, couts, sems)

    res = pl.pallas_call(
        fused, name=name, grid=grid, in_specs=list(in_specs) + [ANY] * counts[1],
        out_specs=list(out_specs) + [ANY] * counts[3], out_shape=list(out_shape) + list(comm.out_shape),
        scratch_shapes=list(scratch_shapes) + list(comm.sems),
        input_output_aliases={counts[0] + i: counts[2] + j for i, j in comm.aliases.items()},
        compiler_params=_params(n_grid))(*operands, *comm.operands)
    return res[:counts[2]], res[counts[2]:]


def _run_comm(comm, name):
    k_in, k_out = len(comm.operands), len(comm.out_shape)

    def body(*refs):
        cins, couts, sems = refs[:k_in], refs[k_in:k_in + k_out], refs[k_in + k_out:]
        comm.start(cins, couts, sems)
        comm.finish(cins, couts, sems)

    return pl.pallas_call(
        body, name=name, in_specs=[ANY] * k_in, out_specs=[ANY] * k_out, out_shape=list(comm.out_shape),
        scratch_shapes=list(comm.sems), input_output_aliases=dict(comm.aliases))(*comm.operands)


def _ffn_fwd(x, gn, sc, sh, gate, wg, wu, wd, seq, name, comm=None):
    T, D = x.shape
    J, _, Fs = wg.shape
    tm = _tile(seq, 512)
    nb = seq // tm

    def body(x_ref, gn_ref, sc_ref, sh_ref, gate_ref, wg_ref, wu_ref, wd_ref,
             h_ref, a_ref, u_ref, f_ref, xo_ref, hs, acc):
        j = pl.program_id(1)

        @pl.when(j == 0)
        def _():
            hb = _norm_mod(x_ref[...], gn_ref[...], sc_ref[...], sh_ref[...]).astype(BF16)
            hs[...] = hb
            h_ref[...] = hb
            acc[...] = jnp.zeros_like(acc)

        hb = hs[...]
        a = _dot_nn(hb, wg_ref[...])
        u = _dot_nn(hb, wu_ref[...])
        a_ref[...] = a.astype(BF16)
        u_ref[...] = u.astype(BF16)
        s = ((a * _sigmoid(a)) * u).astype(BF16)
        acc[...] += _dot_nn(s, wd_ref[...])

        @pl.when(j == J - 1)
        def _():
            f = acc[...]
            f_ref[...] = f.astype(BF16)
            xo_ref[...] = x_ref[...] + (FFN_RESIDUAL * gate_ref[...]) * f

    row = pl.BlockSpec((tm, D), lambda i, j: (i, 0))
    vec = pl.BlockSpec((1, D), lambda i, j: (0, 0))
    per_b = pl.BlockSpec((None, 1, D), lambda i, j: (i // nb, 0, 0))
    hid = pl.BlockSpec((None, tm, Fs), lambda i, j: (j, i, 0))
    return _call(
        body, name=name, grid=(T // tm, J),
        in_specs=[row, vec, per_b, per_b, per_b,
                  pl.BlockSpec((None, D, Fs), lambda i, j: (j, 0, 0)),
                  pl.BlockSpec((None, D, Fs), lambda i, j: (j, 0, 0)),
                  pl.BlockSpec((None, Fs, D), lambda i, j: (j, 0, 0))],
        out_specs=[row, hid, hid, row, row],
        out_shape=[jax.ShapeDtypeStruct((T, D), BF16), jax.ShapeDtypeStruct((J, T, Fs), BF16),
                   jax.ShapeDtypeStruct((J, T, Fs), BF16), jax.ShapeDtypeStruct((T, D), BF16),
                   jax.ShapeDtypeStruct((T, D), F32)],
        scratch_shapes=[pltpu.VMEM((tm, D), BF16), pltpu.VMEM((tm, D), F32)],
        operands=(x, gn, sc, sh, gate, wg, wu, wd), comm=comm)


def _ffn_bwd(dxo, x, f, a, u, gn, sc, gate, wg, wu, wd, seq, name, comm=None):
    T, D = x.shape
    J, _, Fs = wg.shape
    B = T // seq
    tm = _tile(seq, 512)
    nb = seq // tm

    def body(dxo_ref, x_ref, f_ref, a_ref, u_ref, gn_ref, sc_ref, gate_ref, wg_ref, wu_ref, wd_ref,
             da_ref, du_ref, s_ref, df_ref, dx_ref, dgate_ref, dsc_ref, dsh_ref, dgn_ref, dfs, acc):
        i = pl.program_id(0)
        j = pl.program_id(1)
        first_of_batch = i % nb == 0

        @pl.when(j == 0)
        def _():
            dxo_v = dxo_ref[...]
            dfb = ((FFN_RESIDUAL * gate_ref[...]) * dxo_v).astype(BF16)
            dfs[...] = dfb
            df_ref[...] = dfb
            part = jnp.sum((FFN_RESIDUAL * f_ref[...].astype(F32)) * dxo_v, axis=0, keepdims=True)
            _accumulate(dgate_ref, first_of_batch, part)
            acc[...] = jnp.zeros_like(acc)

        ds = _dot_nt(dfs[...], wd_ref[...])
        av = a_ref[...].astype(F32)
        uv = u_ref[...].astype(F32)
        sig = _sigmoid(av)
        sil = av * sig
        s_ref[...] = (sil * uv).astype(BF16)
        dab = (ds * uv * (sig * (1.0 + av * (1.0 - sig)))).astype(BF16)
        dub = (ds * sil).astype(BF16)
        da_ref[...] = dab
        du_ref[...] = dub
        acc[...] += _dot_nt(dab, wg_ref[...]) + _dot_nt(dub, wu_ref[...])

        @pl.when(j == J - 1)
        def _():
            _norm_mod_bwd(acc[...], x_ref[...], gn_ref[...], sc_ref[...], dxo_ref[...],
                          first_of_batch, i == 0, dx_ref, dsc_ref, dsh_ref, dgn_ref)

    row = pl.BlockSpec((tm, D), lambda i, j: (i, 0))
    vec = pl.BlockSpec((1, D), lambda i, j: (0, 0))
    per_b = pl.BlockSpec((None, 1, D), lambda i, j: (i // nb, 0, 0))
    hid = pl.BlockSpec((None, tm, Fs), lambda i, j: (j, i, 0))
    hid_shape = jax.ShapeDtypeStruct((J, T, Fs), BF16)
    per_b_shape = jax.ShapeDtypeStruct((B, 1, D), F32)
    return _call(
        body, name=name, grid=(T // tm, J),
        in_specs=[row, row, row, hid, hid, vec, per_b, per_b,
                  pl.BlockSpec((None, D, Fs), lambda i, j: (j, 0, 0)),
                  pl.BlockSpec((None, D, Fs), lambda i, j: (j, 0, 0)),
                  pl.BlockSpec((None, Fs, D), lambda i, j: (j, 0, 0))],
        out_specs=[hid, hid, hid, row, row, per_b, per_b, per_b, vec],
        out_shape=[hid_shape, hid_shape, hid_shape, jax.ShapeDtypeStruct((T, D), BF16),
                   jax.ShapeDtypeStruct((T, D), F32), per_b_shape, per_b_shape, per_b_shape,
                   jax.ShapeDtypeStruct((1, D), F32)],
        scratch_shapes=[pltpu.VMEM((tm, D), BF16), pltpu.VMEM((tm, D), F32)],
        operands=(dxo, x, f, a, u, gn, sc, gate, wg, wu, wd), comm=comm)


def _wgrad(a, a_spec, b, b_spec, rows, cols, n_tok, name):
    tk = _tile(n_tok, 512)
    nk = n_tok // tk
    half = rows // 2

    def body(a_ref, b_ref, o32_ref, o16_ref, acc):
        k = pl.program_id(1)

        @pl.when(k == 0)
        def _():
            acc[...] = jnp.zeros_like(acc)

        acc[...] += _dot_tn(a_ref[...], b_ref[...])

        @pl.when(k == nk - 1)
        def _():
            for h in range(2):
                v = acc[h * half:(h + 1) * half, :]
                o32_ref[h] = v
                o16_ref[h] = v.astype(BF16)

    out_spec = pl.BlockSpec((2, None, half, cols), lambda j, k: (0, j, 0, 0))
    return pl.pallas_call(
        body, name=name, grid=(N_CHIP, nk),
        in_specs=[a_spec(tk), b_spec(tk)],
        out_specs=[out_spec, out_spec],
        out_shape=[jax.ShapeDtypeStruct((2, N_CHIP, half, cols), F32),
                   jax.ShapeDtypeStruct((2, N_CHIP, half, cols), BF16)],
        scratch_shapes=[pltpu.VMEM((rows, cols), F32)],
        compiler_params=_params(2),
    )(a, b)


def _spec_rows(width):
    return lambda tk: pl.BlockSpec((tk, width), lambda j, k: (k, 0))


def _spec_chip_major(width):
    return lambda tk: pl.BlockSpec((None, tk, width), lambda j, k: (j, k, 0))


def _spec_col_block(width):
    return lambda tk: pl.BlockSpec((tk, width), lambda j, k: (k, j))


def _in_proj(x, gn, sc, sh, w_in, seq, comm=None):
    T, D = x.shape
    N = w_in.shape[1]
    tm = _tile(seq, 1024)
    nb = seq // tm

    def body(x_ref, gn_ref, sc_ref, sh_ref, w_ref, h_ref, p_ref, hs):
        @pl.when(pl.program_id(1) == 0)
        def _():
            hb = _norm_mod(x_ref[...], gn_ref[...], sc_ref[...], sh_ref[...]).astype(BF16)
            hs[...] = hb
            h_ref[...] = hb

        p_ref[...] = _dot_nn(hs[...], w_ref[...]).astype(BF16)

    row = pl.BlockSpec((tm, D), lambda i, j: (i, 0))
    per_b = pl.BlockSpec((None, 1, D), lambda i, j: (i // nb, 0, 0))
    return _call(
        body, name="mix_in_proj", grid=(T // tm, N // PROJ_TILE),
        in_specs=[row, pl.BlockSpec((1, D), lambda i, j: (0, 0)), per_b, per_b,
                  pl.BlockSpec((D, PROJ_TILE), lambda i, j: (0, j))],
        out_specs=[row, pl.BlockSpec((tm, PROJ_TILE), lambda i, j: (i, j))],
        out_shape=[jax.ShapeDtypeStruct((T, D), BF16), jax.ShapeDtypeStruct((T, N), BF16)],
        scratch_shapes=[pltpu.VMEM((tm, D), BF16)],
        operands=(x, gn, sc, sh, w_in), comm=comm)


def _attn_specs(nblk):
    def own(col):
        return lambda b, n: (b * nblk + n, col)

    def prev(col):
        return lambda b, n: (b * nblk + jnp.maximum(n - 1, 0), col)

    kv = (BLOCK, 2 * HEAD_DIM)
    return [pl.BlockSpec((BLOCK, D_MODEL), own(COLB_Q)),
            pl.BlockSpec(kv, prev(COLB_K)), pl.BlockSpec(kv, own(COLB_K)),
            pl.BlockSpec(kv, prev(COLB_V)), pl.BlockSpec(kv, own(COLB_V))]


def _band_operands(prev_ref, own_ref, lo):
    band = jnp.concatenate([prev_ref[...], own_ref[...]], axis=0).astype(F32)
    rolled = pltpu.roll(band, HEAD_DIM, 1)
    zero = jnp.zeros_like(band)
    head0 = (jnp.where(lo, band, zero).astype(BF16), jnp.where(lo, zero, rolled).astype(BF16))
    head1 = (jnp.where(lo, rolled, zero).astype(BF16), jnp.where(lo, zero, band).astype(BF16))
    return head0, head1


def _band_valid(has_prev):
    qi = lax.broadcasted_iota(jnp.int32, (BLOCK, 2 * BLOCK), 0)
    sj = lax.broadcasted_iota(jnp.int32, (BLOCK, 2 * BLOCK), 1)
    rel = qi + BLOCK - sj
    return (rel >= 0) & (rel < BLOCK) & ((sj >= BLOCK) | has_prev)


def _attn_fwd(proj, sinks, batch, seq, comm=None):
    T = proj.shape[0]
    nblk = seq // BLOCK

    def body(sink_ref, q_ref, kp_ref, ko_ref, vp_ref, vo_ref, o_ref, lse_ref):
        lo = lax.broadcasted_iota(jnp.int32, (1, 2 * HEAD_DIM), 1) < HEAD_DIM
        head_lane = lax.broadcasted_iota(jnp.int32, (1, N_Q_HEADS), 1)
        valid = _band_valid(pl.program_id(1) > 0)
        k_ops = _band_operands(kp_ref, ko_ref, lo)
        v_ops = _band_operands(vp_ref, vo_ref, lo)
        lse_all = jnp.zeros((BLOCK, N_Q_HEADS), F32)
        for pair in range(N_Q_HEADS // 2):
            kvh = pair // (N_Q_HEADS // 2 // N_KV_HEADS)
            q2 = q_ref[:, pair * 2 * HEAD_DIM:(pair + 1) * 2 * HEAD_DIM]
            out = jnp.zeros((BLOCK, 2 * HEAD_DIM), F32)
            for side in range(2):
                head = 2 * pair + side
                sink = sink_ref[0, head]
                s = jnp.where(valid, _dot_nt(q2, k_ops[kvh][side]) * ATTN_SCALE, MASK_VALUE)
                m = jnp.maximum(jnp.max(s, axis=-1, keepdims=True), sink)
                p = jnp.where(valid, jnp.exp(s - m), 0.0)
                den = jnp.sum(p, axis=-1, keepdims=True) + jnp.exp(sink - m)
                out = out + _dot_nn((p / den).astype(BF16), v_ops[kvh][side])
                lse_all = jnp.where(head_lane == head, m + jnp.log(den), lse_all)
            o_ref[:, pair * 2 * HEAD_DIM:(pair + 1) * 2 * HEAD_DIM] = out.astype(BF16)
        lse_ref[...] = lse_all

    return _call(
        body, name="attn_fwd", grid=(batch, nblk),
        in_specs=[SMEM_SPEC] + _attn_specs(nblk),
        out_specs=[pl.BlockSpec((BLOCK, D_MODEL), lambda b, n: (b * nblk + n, 0)),
                   pl.BlockSpec((BLOCK, N_Q_HEADS), lambda b, n: (b * nblk + n, 0))],
        out_shape=[jax.ShapeDtypeStruct((T, D_MODEL), BF16), jax.ShapeDtypeStruct((T, N_Q_HEADS), F32)],
        operands=(sinks, proj, proj, proj, proj, proj), comm=comm)


def _conv_u(ca, cb):
    return ca.astype(F32) * _sigmoid(cb.astype(F32))


def _conv_specs(ts, tiles_per_seq):
    per_tile = ts // CONV_PAD

    def tile(col):
        return lambda b, t: (b * tiles_per_seq + t, col)

    def before(col):
        return lambda b, t: (jnp.maximum((b * tiles_per_seq + t) * per_tile - 1, 0), col)

    return [pl.BlockSpec((ts, D_MODEL), tile(COLB_CA)), pl.BlockSpec((ts, D_MODEL), tile(COLB_CB)),
            pl.BlockSpec((CONV_PAD, D_MODEL), before(COLB_CA)), pl.BlockSpec((CONV_PAD, D_MODEL), before(COLB_CB))]


def _fill_upad(upad, ca_ref, cb_ref, cah_ref, cbh_ref, t):
    halo = _conv_u(cah_ref[...], cbh_ref[...])
    upad[0:CONV_PAD, :] = jnp.where(t > 0, halo, jnp.zeros_like(halo))
    upad[CONV_PAD:, :] = _conv_u(ca_ref[...], cb_ref[...])


def _layernorm_stats(y):
    mu = jnp.mean(y, axis=-1, keepdims=True)
    yc = y - mu
    rstd = lax.rsqrt(jnp.mean(yc * yc, axis=-1, keepdims=True) + EPS)
    return yc * rstd, rstd


def _conv_fwd(proj, w_dw, b_dw, ln_g, ln_b, batch, seq, comm=None):
    T = proj.shape[0]
    ts = _tile(seq, 256)
    nt = seq // ts
    shift = CONV_PAD - (CONV_WIDTH - 1)

    def body(ca_ref, cb_ref, cah_ref, cbh_ref, w_ref, b_ref, g_ref, beta_ref, y_ref, z_ref, upad):
        _fill_upad(upad, ca_ref, cb_ref, cah_ref, cbh_ref, pl.program_id(1))
        y = jnp.zeros((ts, D_MODEL), F32) + b_ref[...]
        for k in range(CONV_WIDTH):
            y = y + w_ref[k:k + 1, :] * upad[shift + k:shift + k + ts, :]
        y_ref[...] = y
        lnh, _ = _layernorm_stats(y)
        ln = lnh * g_ref[...] + beta_ref[...]
        z_ref[...] = (ln * _sigmoid(ln)).astype(BF16)

    vec = pl.BlockSpec((1, D_MODEL), lambda b, t: (0, 0))
    row = pl.BlockSpec((ts, D_MODEL), lambda b, t: (b * nt + t, 0))
    return _call(
        body, name="conv_fwd", grid=(batch, nt),
        in_specs=_conv_specs(ts, nt) + [pl.BlockSpec((CONV_PAD, D_MODEL), lambda b, t: (0, 0)), vec, vec, vec],
        out_specs=[row, row],
        out_shape=[jax.ShapeDtypeStruct((T, D_MODEL), F32), jax.ShapeDtypeStruct((T, D_MODEL), BF16)],
        scratch_shapes=[pltpu.VMEM((ts + CONV_PAD, D_MODEL), F32)],
        operands=(proj, proj, proj, proj, w_dw, b_dw, ln_g, ln_b), comm=comm)


def _merge(o, z, proj, w_ao, w_co, w_out, x, gate, seq):
    T, D = x.shape
    tm = _tile(seq, 512)
    nb = seq // tm

    def body(o_ref, z_ref, ga_ref, gc_ref, wao_ref, wco_ref, wout_ref, x_ref, gate_ref,
             ya_ref, yc_ref, mg_ref, mo_ref, xo_ref):
        ya = _dot_nn(o_ref[...], wao_ref[...])
        yc = _dot_nn(z_ref[...], wco_ref[...])
        ya_ref[...] = ya.astype(BF16)
        yc_ref[...] = yc.astype(BF16)
        merged = (_sigmoid(ga_ref[...].astype(F32)) * ya + _sigmoid(gc_ref[...].astype(F32)) * yc).astype(BF16)
        mg_ref[...] = merged
        mo = _dot_nn(merged, wout_ref[...])
        mo_ref[...] = mo.astype(BF16)
        xo_ref[...] = x_ref[...] + gate_ref[...] * mo

    row = pl.BlockSpec((tm, D), lambda i: (i, 0))
    mat = pl.BlockSpec((D, D), lambda i: (0, 0))
    act = jax.ShapeDtypeStruct((T, D), BF16)
    return pl.pallas_call(
        body, name="mix_merge", grid=(T // tm,),
        in_specs=[row, row, pl.BlockSpec((tm, D), lambda i: (i, COLB_GA)), pl.BlockSpec((tm, D), lambda i: (i, COLB_GC)),
                  mat, mat, mat, row, pl.BlockSpec((None, 1, D), lambda i: (i // nb, 0, 0))],
        out_specs=[row, row, row, row, row],
        out_shape=[act, act, act, act, jax.ShapeDtypeStruct((T, D), F32)],
        compiler_params=_params(1),
    )(o, z, proj, proj, w_ao, w_co, w_out, x, gate)


def _final_loss(x, gf, target):
    T, D = x.shape
    tm = _tile(T, 512)

    def body(x_ref, gf_ref, t_ref, dx_ref, lp_ref, dgf_ref):
        first = pl.program_id(0) == 0
        xv = x_ref[...]
        gfv = gf_ref[...]
        r = lax.rsqrt(jnp.mean(xv * xv, axis=-1, keepdims=True) + EPS)
        xh = xv * r
        err = xh * gfv - t_ref[...]
        _accumulate(lp_ref, first, jnp.sum(err * err, axis=0, keepdims=True))
        dy = err * (1.0 / D)
        _accumulate(dgf_ref, first, jnp.sum(dy * xh, axis=0, keepdims=True))
        dxh = dy * gfv
        dx_ref[...] = r * (dxh - xh * jnp.mean(dxh * xh, axis=-1, keepdims=True))

    row = pl.BlockSpec((tm, D), lambda i: (i, 0))
    vec = pl.BlockSpec((1, D), lambda i: (0, 0))
    return pl.pallas_call(
        body, name="final_loss", grid=(T // tm,),
        in_specs=[row, vec, row], out_specs=[row, vec, vec],
        out_shape=[jax.ShapeDtypeStruct((T, D), F32), jax.ShapeDtypeStruct((1, D), F32),
                   jax.ShapeDtypeStruct((1, D), F32)],
        compiler_params=_params(1),
    )(x, gf, target)


def _merge_bwd(dxo, mo, gate, proj, ya, yc, w_out, w_ao, w_co, seq):
    T, D = dxo.shape
    B = T // seq
    tm = _tile(seq, 512)
    nb = seq // tm

    def body(dxo_ref, mo_ref, gate_ref, ga_ref, gc_ref, ya_ref, yc_ref, wout_ref, wao_ref, wco_ref,
             dmo_ref, dya_ref, dyc_ref, dga_ref, dgc_ref, do_ref, dz_ref, dgate_ref):
        dxo_v = dxo_ref[...]
        dmo = (gate_ref[...] * dxo_v).astype(BF16)
        dmo_ref[...] = dmo
        _accumulate(dgate_ref, pl.program_id(0) % nb == 0,
                    jnp.sum(mo_ref[...].astype(F32) * dxo_v, axis=0, keepdims=True))
        dm = _dot_nt(dmo, wout_ref[...])
        sa = _sigmoid(ga_ref[...].astype(F32))
        sc = _sigmoid(gc_ref[...].astype(F32))
        dya = (sa * dm).astype(BF16)
        dyc = (sc * dm).astype(BF16)
        dya_ref[...] = dya
        dyc_ref[...] = dyc
        dga_ref[...] = (dm * ya_ref[...].astype(F32) * (sa * (1.0 - sa))).astype(BF16)
        dgc_ref[...] = (dm * yc_ref[...].astype(F32) * (sc * (1.0 - sc))).astype(BF16)
        do_ref[...] = _dot_nt(dya, wao_ref[...]).astype(BF16)
        dz_ref[...] = _dot_nt(dyc, wco_ref[...]).astype(BF16)

    row = pl.BlockSpec((tm, D), lambda i: (i, 0))
    mat = pl.BlockSpec((D, D), lambda i: (0, 0))
    per_b = pl.BlockSpec((None, 1, D), lambda i: (i // nb, 0, 0))
    act = jax.ShapeDtypeStruct((T, D), BF16)
    return pl.pallas_call(
        body, name="mix_merge_bwd", grid=(T // tm,),
        in_specs=[row, row, per_b, pl.BlockSpec((tm, D), lambda i: (i, COLB_GA)),
                  pl.BlockSpec((tm, D), lambda i: (i, COLB_GC)), row, row, mat, mat, mat],
        out_specs=[row] * 7 + [per_b],
        out_shape=[act] * 7 + [jax.ShapeDtypeStruct((B, 1, D), F32)],
        compiler_params=_params(1),
    )(dxo, mo, gate, proj, proj, ya, yc, w_out, w_ao, w_co)


def _attn_bwd(proj, sinks, o, do, lse, batch, seq, comm=None):
    T = proj.shape[0]
    nblk = seq // BLOCK
    n_steps = batch * nblk
    pairs_per_kv = N_Q_HEADS // 2 // N_KV_HEADS

    def body(sink_ref, q_ref, kp_ref, ko_ref, vp_ref, vo_ref, o_ref, do_ref, lse_ref,
             dq_ref, dkp_ref, dko_ref, dvp_ref, dvo_ref, dsink_ref):
        lo = lax.broadcasted_iota(jnp.int32, (1, 2 * HEAD_DIM), 1) < HEAD_DIM
        sink_lane = lax.broadcasted_iota(jnp.int32, (1, 2 * HEAD_DIM), 1)
        valid = _band_valid(pl.program_id(1) > 0)
        k_ops = _band_operands(kp_ref, ko_ref, lo)
        v_ops = _band_operands(vp_ref, vo_ref, lo)
        dsink = jnp.zeros((1, 2 * HEAD_DIM), F32)
        dk_heads, dv_heads = [], []
        for kvh in range(N_KV_HEADS):
            dk_acc = jnp.zeros((2 * BLOCK, 2 * HEAD_DIM), F32)
            dv_acc = jnp.zeros((2 * BLOCK, 2 * HEAD_DIM), F32)
            for pp in range(pairs_per_kv):
                pair = kvh * pairs_per_kv + pp
                lanes = slice(pair * 2 * HEAD_DIM, (pair + 1) * 2 * HEAD_DIM)
                q2 = q_ref[:, lanes]
                do2 = do_ref[:, lanes]
                dd = do2.astype(F32) * o_ref[:, lanes].astype(F32)
                dq2 = jnp.zeros((BLOCK, 2 * HEAD_DIM), F32)
                for side in range(2):
                    head = 2 * pair + side
                    mine = lo if side == 0 else jnp.logical_not(lo)
                    sink = sink_ref[0, head]
                    lse_h = lse_ref[:, head:head + 1]
                    delta = jnp.sum(jnp.where(mine, dd, 0.0), axis=-1, keepdims=True)
                    s = _dot_nt(q2, k_ops[kvh][side]) * ATTN_SCALE
                    p = jnp.where(valid, jnp.exp(jnp.where(valid, s, MASK_VALUE) - lse_h), 0.0)
                    dp = _dot_nt(do2, v_ops[kvh][side])
                    ds = (p * (dp - delta) * ATTN_SCALE).astype(BF16)
                    dq2 = dq2 + _dot_nn(ds, k_ops[kvh][side])
                    dk_acc = dk_acc + jnp.where(mine, _dot_tn(ds, q2), 0.0)
                    dv_acc = dv_acc + jnp.where(mine, _dot_tn(p.astype(BF16), do2), 0.0)
                    dsink = dsink + jnp.where(sink_lane == head, -jnp.sum(jnp.exp(sink - lse_h) * delta), 0.0)
                dq_ref[:, lanes] = dq2.astype(BF16)
            dk_heads.append(dk_acc + pltpu.roll(dk_acc, HEAD_DIM, 1))
            dv_heads.append(dv_acc + pltpu.roll(dv_acc, HEAD_DIM, 1))
        dk = jnp.where(lo, dk_heads[0], dk_heads[1])
        dv = jnp.where(lo, dv_heads[0], dv_heads[1])
        dkp_ref[...] = dk[:BLOCK]
        dko_ref[...] = dk[BLOCK:]
        dvp_ref[...] = dv[:BLOCK]
        dvo_ref[...] = dv[BLOCK:]
        dsink_ref[...] = dsink

    def own(b, n):
        return (b * nblk + n, 0)

    row = pl.BlockSpec((BLOCK, D_MODEL), own)
    kv = pl.BlockSpec((BLOCK, 2 * HEAD_DIM), own)
    kv_shape = jax.ShapeDtypeStruct((T, 2 * HEAD_DIM), F32)
    return _call(
        body, name="attn_bwd", grid=(batch, nblk),
        in_specs=[SMEM_SPEC] + _attn_specs(nblk) + [row, row, pl.BlockSpec((BLOCK, N_Q_HEADS), own)],
        out_specs=[row, kv, kv, kv, kv, pl.BlockSpec((None, 1, 2 * HEAD_DIM), lambda b, n: (b * nblk + n, 0, 0))],
        out_shape=[jax.ShapeDtypeStruct((T, D_MODEL), BF16), kv_shape, kv_shape, kv_shape, kv_shape,
                   jax.ShapeDtypeStruct((n_steps, 1, 2 * HEAD_DIM), F32)],
        operands=(sinks, proj, proj, proj, proj, proj, o, do, lse), comm=comm)


def _conv_bwd(proj, dz, ydw, w_dw, ln_g, ln_b, batch, seq):
    T = proj.shape[0]
    ts = _tile(seq, 256)
    nt = seq // ts
    per_tile = ts // CONV_PAD
    shift = CONV_PAD - (CONV_WIDTH - 1)

    def body(ca_ref, cb_ref, cah_ref, cbh_ref, dz_ref, dzn_ref, y_ref, yn_ref, w_ref, g_ref, beta_ref,
             dca_ref, dcb_ref, dw_ref, db_ref, dg_ref, dbeta_ref, upad, dypad):
        t = pl.program_id(1)
        first = (pl.program_id(0) == 0) & (t == 0)
        gv = g_ref[...]

        def ln_bwd(dzv, yv):
            lnh, rstd = _layernorm_stats(yv)
            ln = lnh * gv + beta_ref[...]
            sg = _sigmoid(ln)
            dln = dzv.astype(F32) * (sg * (1.0 + ln * (1.0 - sg)))
            dyh = dln * gv
            dy = rstd * (dyh - jnp.mean(dyh, axis=-1, keepdims=True)
                         - lnh * jnp.mean(dyh * lnh, axis=-1, keepdims=True))
            return dy, dln, lnh

        dy, dln, lnh = ln_bwd(dz_ref[...], y_ref[...])
        dy_next, _, _ = ln_bwd(dzn_ref[...], yn_ref[...])
        dypad[0:ts, :] = dy
        dypad[ts:, :] = jnp.where(t < nt - 1, dy_next, jnp.zeros_like(dy_next))
        _fill_upad(upad, ca_ref, cb_ref, cah_ref, cbh_ref, t)

        _accumulate(dg_ref, first, jnp.sum(dln * lnh, axis=0, keepdims=True))
        _accumulate(dbeta_ref, first, jnp.sum(dln, axis=0, keepdims=True))
        _accumulate(db_ref, first, jnp.sum(dy, axis=0, keepdims=True))

        @pl.when(first)
        def _():
            dw_ref[...] = jnp.zeros_like(dw_ref)

        du = jnp.zeros((ts, D_MODEL), F32)
        for k in range(CONV_WIDTH):
            du = du + w_ref[k:k + 1, :] * dypad[CONV_WIDTH - 1 - k:CONV_WIDTH - 1 - k + ts, :]
            dw_ref[k:k + 1, :] += jnp.sum(dy * upad[shift + k:shift + k + ts, :], axis=0, keepdims=True)
        cav = ca_ref[...].astype(F32)
        sb = _sigmoid(cb_ref[...].astype(F32))
        dca_ref[...] = (du * sb).astype(BF16)
        dcb_ref[...] = (du * cav * (sb * (1.0 - sb))).astype(BF16)

    def tile(b, t):
        return (b * nt + t, 0)

    def after(b, t):
        return (jnp.minimum((b * nt + t + 1) * per_tile, T // CONV_PAD - 1), 0)

    row = pl.BlockSpec((ts, D_MODEL), tile)
    halo = pl.BlockSpec((CONV_PAD, D_MODEL), after)
    vec = pl.BlockSpec((1, D_MODEL), lambda b, t: (0, 0))
    wspec = pl.BlockSpec((CONV_PAD, D_MODEL), lambda b, t: (0, 0))
    act = jax.ShapeDtypeStruct((T, D_MODEL), BF16)
    vec_shape = jax.ShapeDtypeStruct((1, D_MODEL), F32)
    return pl.pallas_call(
        body, name="conv_bwd", grid=(batch, nt),
        in_specs=_conv_specs(ts, nt) + [row, halo, row, halo, wspec, vec, vec],
        out_specs=[row, row, wspec, vec, vec, vec],
        out_shape=[act, act, jax.ShapeDtypeStruct((CONV_PAD, D_MODEL), F32), vec_shape, vec_shape, vec_shape],
        scratch_shapes=[pltpu.VMEM((ts + CONV_PAD, D_MODEL), F32), pltpu.VMEM((ts + CONV_PAD, D_MODEL), F32)],
        compiler_params=_params(2),
    )(proj, proj, proj, proj, dz, dz, ydw, ydw, w_dw, ln_g, ln_b)


def _in_proj_bwd(dproj, w_in_g, x, gn, sc, dxo, seq):
    T, D = x.shape
    J, _, W = w_in_g.shape
    B = T // seq
    tm = _tile(seq, 512)
    nb = seq // tm

    def body(dp_ref, w_ref, x_ref, gn_ref, sc_ref, dxo_ref, dx_ref, dsc_ref, dsh_ref, dgn_ref, acc):
        i = pl.program_id(0)
        j = pl.program_id(1)

        @pl.when(j == 0)
        def _():
            acc[...] = jnp.zeros_like(acc)

        acc[...] += _dot_nt(dp_ref[...], w_ref[...])

        @pl.when(j == J - 1)
        def _():
            _norm_mod_bwd(acc[...], x_ref[...], gn_ref[...], sc_ref[...], dxo_ref[...],
                          i % nb == 0, i == 0, dx_ref, dsc_ref, dsh_ref, dgn_ref)

    row = pl.BlockSpec((tm, D), lambda i, j: (i, 0))
    vec = pl.BlockSpec((1, D), lambda i, j: (0, 0))
    per_b = pl.BlockSpec((None, 1, D), lambda i, j: (i // nb, 0, 0))
    per_b_shape = jax.ShapeDtypeStruct((B, 1, D), F32)
    return pl.pallas_call(
        body, name="mix_in_proj_bwd", grid=(T // tm, J),
        in_specs=[pl.BlockSpec((None, tm, W), lambda i, j: (j, i, 0)),
                  pl.BlockSpec((None, D, W), lambda i, j: (j, 0, 0)), row, vec, per_b, row],
        out_specs=[row, per_b, per_b, vec],
        out_shape=[jax.ShapeDtypeStruct((T, D), F32), per_b_shape, per_b_shape, jax.ShapeDtypeStruct((1, D), F32)],
        scratch_shapes=[pltpu.VMEM((tm, D), F32)],
        compiler_params=_params(2),
    )(dproj, w_in_g, x, gn, sc, dxo)


def _ada_fwd(c_all, w_ada, b_cols):
    nbatch, D = c_all.shape
    N = w_ada.shape[1]
    tn = _tile(N, 768)

    def body(c_ref, w_ref, b_ref, o_ref):
        cv = c_ref[...]
        act = (cv * _sigmoid(cv)).astype(BF16)
        o_ref[...] = _dot_nn(act, w_ref[...].astype(BF16)) + b_ref[...]

    return pl.pallas_call(
        body, name="ada_fwd", grid=(N // tn,),
        in_specs=[pl.BlockSpec((nbatch, D), lambda j: (0, 0)), pl.BlockSpec((D, tn), lambda j: (0, j)),
                  pl.BlockSpec((1, tn), lambda j: (0, j))],
        out_specs=pl.BlockSpec((nbatch, tn), lambda j: (0, j)),
        out_shape=jax.ShapeDtypeStruct((nbatch, N), F32),
        compiler_params=_params(1),
    )(c_all, w_ada, b_cols)


def _adamw(w, g, m, v):
    m = ADAM_B1 * m + (1.0 - ADAM_B1) * g
    v = ADAM_B2 * v + (1.0 - ADAM_B2) * (g * g)
    m_hat = m / (1.0 - ADAM_B1 ** ADAM_STEP)
    v_hat = v / (1.0 - ADAM_B2 ** ADAM_STEP)
    delta = -ADAM_LR * (m_hat / (jnp.sqrt(v_hat) + ADAM_EPS) + ADAM_WD * w)
    return delta, m, v


def _adam_call(w, g, m, v, name):
    R, C = w.shape
    tr = R if R % 8 else _tile(R, 256)

    def body(w_ref, g_ref, m_ref, v_ref, d_ref, mo_ref, vo_ref):
        d, mn, vn = _adamw(w_ref[...], g_ref[...], m_ref[...], v_ref[...])
        d_ref[...] = d
        mo_ref[...] = mn
        vo_ref[...] = vn

    blk = pl.BlockSpec((tr, C), lambda i: (i, 0))
    shape = jax.ShapeDtypeStruct((R, C), F32)
    return pl.pallas_call(
        body, name=name, grid=(R // tr,), in_specs=[blk] * 4, out_specs=[blk] * 3, out_shape=[shape] * 3,
        compiler_params=_params(1),
    )(w, g, m, v)


def _ada_adam(c_act_t, dmod_cols, w, m, v):
    R, C = w.shape
    nbatch = c_act_t.shape[1]
    tr = _tile(R, 128)

    def body(ct_ref, dm_ref, w_ref, m_ref, v_ref, g_ref, d_ref, mo_ref, vo_ref):
        cv = ct_ref[...]
        g = _dot_nn((cv * _sigmoid(cv)).astype(BF16), dm_ref[...].astype(BF16))
        g_ref[...] = g
        d, mn, vn = _adamw(w_ref[...], g, m_ref[...], v_ref[...])
        d_ref[...] = d
        mo_ref[...] = mn
        vo_ref[...] = vn

    blk = pl.BlockSpec((tr, C), lambda i: (i, 0))
    shape = jax.ShapeDtypeStruct((R, C), F32)
    return pl.pallas_call(
        body, name="ada_adam", grid=(R // tr,),
        in_specs=[pl.BlockSpec((tr, nbatch), lambda i: (i, 0)), pl.BlockSpec((nbatch, C), lambda i: (0, 0)),
                  blk, blk, blk],
        out_specs=[blk] * 4, out_shape=[shape] * 4,
        compiler_params=_params(1),
    )(c_act_t, dmod_cols, w, m, v)


def _small_adam(gathered, w, m, v, rows_b0, rows_b1, rows_vec):
    _, P, D = gathered.shape
    R = w.shape[0]

    def body(ga_ref, w_ref, m_ref, v_ref, sum_ref, g_ref, d_ref, mo_ref, vo_ref):
        total = ga_ref[0]
        for dev in range(1, N_DEV):
            total = total + ga_ref[dev]
        sum_ref[...] = total
        g_ref[...] = jnp.zeros_like(g_ref)
        g_ref[0:N_MOD, :] = (sum_ref[rows_b0:rows_b0 + N_MOD, :] + sum_ref[rows_b1:rows_b1 + N_MOD, :])
        g_ref[N_MOD:N_MOD + 8, :] = sum_ref[rows_vec:rows_vec + 8, :]
        d, mn, vn = _adamw(w_ref[...], g_ref[...], m_ref[...], v_ref[...])
        d_ref[...] = d
        mo_ref[...] = mn
        vo_ref[...] = vn

    shape = jax.ShapeDtypeStruct((R, D), F32)
    return pl.pallas_call(
        body, name="small_adam",
        in_specs=[VMEM_SPEC] * 4, out_specs=[VMEM_SPEC] * 5,
        out_shape=[jax.ShapeDtypeStruct((P, D), F32), shape, shape, shape, shape],
        compiler_params=pltpu.CompilerParams(vmem_limit_bytes=VMEM_LIMIT),
    )(gathered, w, m, v)


def _gather8(v, name):
    A, W = v.shape
    flips = [(fx, fy, fc) for fx in (0, 1) for fy in (0, 1) for fc in (0, 1) if (fx, fy, fc) != (0, 0, 0)]

    def body(v_ref, out_ref, send_sems, recv_sems, local_sem):
        x, y, c = _position()
        me = 4 * x + 2 * y + c
        mine = pltpu.make_async_copy(v_ref, out_ref.at[me], local_sem)
        mine.start()

        def copy(k, block, to):
            return pltpu.make_async_remote_copy(src_ref=v_ref, dst_ref=out_ref.at[block], send_sem=send_sems.at[k],
                                                recv_sem=recv_sems.at[k], device_id=to, device_id_type=MESH)

        peers = [(_flip(x, fx), _flip(y, fy), _flip(c, fc)) for fx, fy, fc in flips]
        sends = [copy(k, me, peer) for k, peer in enumerate(peers)]
        for cp in sends:
            cp.start()
        for k, (px, py, pc) in enumerate(peers):
            copy(k, 4 * px + 2 * py + pc, (px, py, pc)).wait_recv()
        for cp in sends:
            cp.wait_send()
        mine.wait()

    return pl.pallas_call(
        body, name=name, in_specs=[VMEM_SPEC], out_specs=VMEM_SPEC,
        out_shape=jax.ShapeDtypeStruct((N_DEV, A, W), v.dtype),
        scratch_shapes=[pltpu.SemaphoreType.DMA((N_DEV - 1,)), pltpu.SemaphoreType.DMA((N_DEV - 1,)),
                        pltpu.SemaphoreType.DMA],
    )(v)


def _mod_exchange(part):
    _, A, W = part.shape

    def body(p_ref, out_ref, send_sems, recv_sems, local_sem):
        x, y, c = _position()
        me = 4 * x + 2 * y + c
        chip = 2 * x + y
        mine = pltpu.make_async_copy(p_ref.at[me], out_ref.at[chip], local_sem)
        mine.start()
        peers = [(_flip(x, fx), _flip(y, fy)) for fx, fy in CHIP_FLIPS]
        sends = []
        for k, (px, py) in enumerate(peers):
            sends.append(pltpu.make_async_remote_copy(
                src_ref=p_ref.at[4 * px + 2 * py + c], dst_ref=out_ref.at[chip], send_sem=send_sems.at[k],
                recv_sem=recv_sems.at[k], device_id=(px, py, c), device_id_type=MESH))
        for cp in sends:
            cp.start()
        for k, (px, py) in enumerate(peers):
            pltpu.make_async_remote_copy(
                src_ref=p_ref.at[me], dst_ref=out_ref.at[2 * px + py], send_sem=send_sems.at[k],
                recv_sem=recv_sems.at[k], device_id=(px, py, c), device_id_type=MESH).wait_recv()
        for cp in sends:
            cp.wait_send()
        mine.wait()

    return pl.pallas_call(
        body, name="mod_exchange", in_specs=[VMEM_SPEC], out_specs=VMEM_SPEC,
        out_shape=jax.ShapeDtypeStruct((N_CHIP, A, W), part.dtype),
        scratch_shapes=[pltpu.SemaphoreType.DMA((3,)), pltpu.SemaphoreType.DMA((3,)), pltpu.SemaphoreType.DMA],
    )(part)


def _cast_slot(w, chip_idx, name):
    R, C = w.shape
    tr = _tile(R, 256)

    def body(chip_ref, w_ref, o_ref):
        o_ref[...] = w_ref[...].astype(BF16)

    return pl.pallas_call(
        body, name=name,
        grid_spec=pltpu.PrefetchScalarGridSpec(
            num_scalar_prefetch=1, grid=(R // tr,),
            in_specs=[pl.BlockSpec((tr, C), lambda i, chip_ref: (i, 0))],
            out_specs=pl.BlockSpec((None, tr, C), lambda i, chip_ref: (chip_ref[0], i, 0))),
        out_shape=jax.ShapeDtypeStruct((N_CHIP, R, C), BF16),
        compiler_params=_params(1),
    )(chip_idx, w)


def _sibling_swap_halves(grads16, name):
    n = len(grads16)

    def body(*refs):
        g_refs, out_refs = refs[:n], refs[n:2 * n]
        send_sems, recv_sems = refs[2 * n:]
        x, y, c = _position()
        copies = [pltpu.make_async_remote_copy(
            src_ref=g_refs[i].at[1 - c], dst_ref=out_refs[i], send_sem=send_sems.at[i], recv_sem=recv_sems.at[i],
            device_id=(x, y, 1 - c), device_id_type=MESH) for i in range(n)]
        for cp in copies:
            cp.start()
        for cp in copies:
            cp.wait()

    return pl.pallas_call(
        body, name=name, in_specs=[ANY] * n, out_specs=[ANY] * n,
        out_shape=[jax.ShapeDtypeStruct(g.shape[1:], g.dtype) for g in grads16],
        scratch_shapes=[pltpu.SemaphoreType.DMA((n,)), pltpu.SemaphoreType.DMA((n,))],
    )(*grads16)


def _pair_sum(g32, recv, core, name):
    _, J, r, C = g32.shape

    def body(core_ref, g_ref, r_ref, o_ref):
        o_ref[...] = (g_ref[...] + r_ref[...].astype(F32)).astype(BF16)

    return pl.pallas_call(
        body, name=name,
        grid_spec=pltpu.PrefetchScalarGridSpec(
            num_scalar_prefetch=1, grid=(J,),
            in_specs=[pl.BlockSpec((None, None, r, C), lambda j, core_ref: (core_ref[0], j, 0, 0)),
                      pl.BlockSpec((None, r, C), lambda j, core_ref: (j, 0, 0))],
            out_specs=pl.BlockSpec((None, r, C), lambda j, core_ref: (j, 0, 0))),
        out_shape=jax.ShapeDtypeStruct((J, r, C), BF16),
        compiler_params=_params(1),
    )(core, g32, recv)


def _chip_sum(g32, recv_sib, recv_chips, core_chip, name):
    _, J, r, C = g32.shape

    def body(idx_ref, g_ref, s_ref, o_ref_in, o_ref):
        total = g_ref[...] + s_ref[...].astype(F32)
        for k in range(3):
            total = total + o_ref_in[k].astype(F32)
        o_ref[...] = total

    return pl.pallas_call(
        body, name=name,
        grid_spec=pltpu.PrefetchScalarGridSpec(
            num_scalar_prefetch=1, grid=(1,),
            in_specs=[pl.BlockSpec((None, None, r, C), lambda i, idx: (idx[0], idx[1], 0, 0)),
                      pl.BlockSpec((None, r, C), lambda i, idx: (idx[1], 0, 0)),
                      pl.BlockSpec((3, r, C), lambda i, idx: (0, 0, 0))],
            out_specs=pl.BlockSpec((None, r, C), lambda i, idx: (idx[0], 0, 0))),
        out_shape=jax.ShapeDtypeStruct((2, r, C), F32),
        compiler_params=_params(1),
    )(core_chip, g32, recv_sib, recv_chips)


def _sibling_join_halves(halves, name):
    n = len(halves)

    def body(*refs):
        h_refs, out_refs = refs[:n], refs[n:2 * n]
        send_sems, recv_sems = refs[2 * n:]
        x, y, c = _position()

        def copy(i, which):
            return pltpu.make_async_remote_copy(
                src_ref=h_refs[i].at[which], dst_ref=out_refs[i].at[which], send_sem=send_sems.at[i],
                recv_sem=recv_sems.at[i], device_id=(x, y, 1 - c), device_id_type=MESH)

        for i in range(n):
            copy(i, c).start()
        for i in range(n):
            copy(i, 1 - c).wait_recv()
        for i in range(n):
            copy(i, c).wait_send()

    return pl.pallas_call(
        body, name=name, in_specs=[ANY] * n, out_specs=[ANY] * n,
        out_shape=[jax.ShapeDtypeStruct(h.shape, h.dtype) for h in halves],
        scratch_shapes=[pltpu.SemaphoreType.DMA((n,)), pltpu.SemaphoreType.DMA((n,))],
        input_output_aliases={i: i for i in range(n)},
    )(*halves)


BIG_WEIGHTS = ("ffn1_w_gate", "ffn1_w_up", "ffn1_w_down", "w_in", "w_attn_o", "w_conv_o", "w_out",
               "ffn2_w_gate", "ffn2_w_up", "ffn2_w_down")
VECTORS = ("norm_ffn1_g", "norm_mix_g", "conv_b_dw", "conv_ln_g", "conv_ln_b", "norm_ffn2_g", "final_norm_g")
ROW_DMOD0, ROW_DMOD1, ROW_VEC, ROW_SINK, ROW_CONVW, SMALL_ROWS = 0, 16, 33, 40, 41, 72


def _reduce_begin(grads, names, core_idx, tag):
    from_sibling = _sibling_swap_halves([grads[n][1] for n in names], "grad_swap_halves_" + tag)
    pair_sums = [_pair_sum(grads[n][0], r, core_idx, "pair_sum_" + n) for n, r in zip(names, from_sibling)]
    return from_sibling, pair_sums


def _reduce_end(grads, names, from_sibling, from_chips, core_chip, tag):
    halves = [_chip_sum(grads[n][0], rs, rc, core_chip, "chip_sum_" + n)
              for n, rs, rc in zip(names, from_sibling, from_chips)]
    return dict(zip(names, _sibling_join_halves(halves, "grad_join_halves_" + tag)))


FFN1_WEIGHTS = ("ffn1_w_gate", "ffn1_w_up", "ffn1_w_down")
FFN2_WEIGHTS = ("ffn2_w_gate", "ffn2_w_up", "ffn2_w_down")
MIX_WEIGHTS = ("w_in", "w_attn_o", "w_conv_o", "w_out")


def _local_grads(x, target, mod, slots, small, seq, core_idx, core_chip):
    T, D = x.shape
    B = T // seq
    mods = [mod[:, k][:, None, :] for k in range(N_MOD)]
    sh1, sc1, g1, sh2, sc2, g2, sh3, sc3, g3 = mods
    w = dict(zip(FFN1_WEIGHTS, _run_comm(_GatherComm([slots[n] for n in FFN1_WEIGHTS]), "gather_ffn1")))

    (h1, a1, u1, f1, x1), (w["w_in"],) = _ffn_fwd(
        x, small["norm_ffn1_g"], sc1, sh1, g1, w["ffn1_w_gate"], w["ffn1_w_up"], w["ffn1_w_down"], seq, "ffn1_fwd",
        comm=_GatherComm([slots["w_in"]]))
    w_in_full = w["w_in"].transpose(1, 0, 2).reshape(D, IN_WIDTH)
    q_end, k_end, v_end = D, D + 2 * HEAD_DIM, D + 4 * HEAD_DIM
    w_in_cols = jnp.concatenate(
        [w_in_full[:, :q_end], w_in_full[:, v_end:], w_in_full[:, q_end:k_end], w_in_full[:, k_end:v_end]], axis=1)
    (h2, proj), outs = _in_proj(x1, small["norm_mix_g"], sc2, sh2, w_in_cols, seq,
                                comm=_GatherComm([slots[n] for n in ("w_attn_o", "w_conv_o", "w_out")]))
    w_ao, w_co, w_o = [t.reshape(D, D) for t in outs]
    (o, lse), (w["ffn2_w_gate"], w["ffn2_w_up"]) = _attn_fwd(
        proj, small["attn_sinks"], B, seq, comm=_GatherComm([slots["ffn2_w_gate"], slots["ffn2_w_up"]]))
    (ydw, z), (w["ffn2_w_down"],) = _conv_fwd(
        proj, small["conv_w_dw"], small["conv_b_dw"], small["conv_ln_g"], small["conv_ln_b"], B, seq,
        comm=_GatherComm([slots["ffn2_w_down"]]))
    ya, yc, merged, mo, x2 = _merge(o, z, proj, w_ao, w_co, w_o, x1, g2, seq)
    (h3, a3, u3, f3, x3), _ = _ffn_fwd(x2, small["norm_ffn2_g"], sc3, sh3, g3, w["ffn2_w_gate"], w["ffn2_w_up"],
                                       w["ffn2_w_down"], seq, "ffn2_fwd")
    dx3, loss_parts, d_final_g = _final_loss(x3, small["final_norm_g"], target)

    grads = {}

    def ffn_backward(prefix, dxo, xin, h, a, u, f, gn, sc, gate, comm=None):
        (da, du, s, df, dx, dgate, dsc, dsh, dgn), comm_out = _ffn_bwd(
            dxo, xin, f, a, u, gn, sc, gate, w[prefix + "_w_gate"], w[prefix + "_w_up"], w[prefix + "_w_down"],
            seq, prefix + "_bwd", comm=comm)
        grads[prefix + "_w_gate"] = _wgrad(h, _spec_rows(D), da, _spec_chip_major(FF_SHARD), D, FF_SHARD, T,
                                           prefix + "_dw_gate")
        grads[prefix + "_w_up"] = _wgrad(h, _spec_rows(D), du, _spec_chip_major(FF_SHARD), D, FF_SHARD, T,
                                         prefix + "_dw_up")
        grads[prefix + "_w_down"] = _wgrad(s, _spec_chip_major(FF_SHARD), df, _spec_rows(D), FF_SHARD, D, T,
                                           prefix + "_dw_down")
        return (dx, dgate, dsc, dsh, dgn), comm_out

    (dx2, dg3, dsc3, dsh3, d_gn3), _ = ffn_backward("ffn2", dx3, x2, h3, a3, u3, f3, small["norm_ffn2_g"], sc3, g3)
    sib2, pairs2 = _reduce_begin(grads, FFN2_WEIGHTS, core_idx, "ffn2")

    dmo, dya, dyc, dga, dgc, do, dz, dg2 = _merge_bwd(dx2, mo, g2, proj, ya, yc, w_o, w_ao, w_co, seq)
    shard = D // N_CHIP
    grads["w_out"] = _wgrad(merged, _spec_col_block(shard), dmo, _spec_rows(D), shard, D, T, "dw_out")
    grads["w_attn_o"] = _wgrad(o, _spec_col_block(shard), dya, _spec_rows(D), shard, D, T, "dw_attn_o")
    grads["w_conv_o"] = _wgrad(z, _spec_col_block(shard), dyc, _spec_rows(D), shard, D, T, "dw_conv_o")
    (dq, dkp, dko, dvp, dvo, dsink_steps), chips2 = _attn_bwd(proj, small["attn_sinks"], o, do, lse, B, seq,
                                                              comm=_ExchangeComm(pairs2))
    reduced = _reduce_end(grads, FFN2_WEIGHTS, sib2, chips2, core_chip, "ffn2")
    dca, dcb, d_conv_w, d_conv_b, d_ln_g, d_ln_b = _conv_bwd(proj, dz, ydw, small["conv_w_dw"], small["conv_ln_g"],
                                                              small["conv_ln_b"], B, seq)

    def band_sum(own, prev):
        prev = prev.reshape(B, seq // BLOCK, BLOCK, 2 * HEAD_DIM)
        moved = jnp.concatenate([prev[:, 1:], jnp.zeros_like(prev[:, :1])], axis=1)
        return (own + moved.reshape(T, 2 * HEAD_DIM)).astype(BF16)

    dproj = jnp.concatenate([dq, band_sum(dko, dkp), band_sum(dvo, dvp), dca, dcb, dga, dgc], axis=1)
    dproj = dproj.reshape(T, N_CHIP, IN_SHARD).transpose(1, 0, 2)
    grads["w_in"] = _wgrad(h2, _spec_rows(D), dproj, _spec_chip_major(IN_SHARD), D, IN_SHARD, T, "dw_in")
    dx1, dsc2, dsh2, d_gn2 = _in_proj_bwd(dproj, w["w_in"], x1, small["norm_mix_g"], sc2, dx2, seq)
    sib_mix, pairs_mix = _reduce_begin(grads, MIX_WEIGHTS, core_idx, "mix")

    (dx0, dg1, dsc1, dsh1, d_gn1), chips_mix = ffn_backward(
        "ffn1", dx1, x, h1, a1, u1, f1, small["norm_ffn1_g"], sc1, g1, comm=_ExchangeComm(pairs_mix))
    reduced.update(_reduce_end(grads, MIX_WEIGHTS, sib_mix, chips_mix, core_chip, "mix"))
    sib1, pairs1 = _reduce_begin(grads, FFN1_WEIGHTS, core_idx, "ffn1")
    chips1 = _run_comm(_ExchangeComm(pairs1), "grad_chip_exchange_ffn1")
    reduced.update(_reduce_end(grads, FFN1_WEIGHTS, sib1, chips1, core_chip, "ffn1"))

    dmod = jnp.concatenate([dsh1, dsc1, dg1, dsh2, dsc2, dg2, dsh3, dsc3, dg3], axis=1)
    d_sinks = jnp.sum(dsink_steps, axis=0)
    vec_grads = {"norm_ffn1_g": d_gn1, "norm_mix_g": d_gn2, "conv_b_dw": d_conv_b, "conv_ln_g": d_ln_g,
                 "conv_ln_b": d_ln_b, "norm_ffn2_g": d_gn3, "final_norm_g": d_final_g}
    return loss_parts, dx0, reduced, dmod, vec_grads, d_sinks, d_conv_w


def kernel(x, c, w_ada, b_ada, norm_ffn1_g, ffn1_w_gate, ffn1_w_up, ffn1_w_down, norm_mix_g, w_in, attn_sinks, w_attn_o, conv_w_dw, conv_b_dw, conv_ln_g, conv_ln_b, w_conv_o, w_out, norm_ffn2_g, ffn2_w_gate, ffn2_w_up, ffn2_w_down, final_norm_g, loss_target, m_w_ada, m_b_ada, m_norm_ffn1_g, m_ffn1_w_gate, m_ffn1_w_up, m_ffn1_w_down, m_norm_mix_g, m_w_in, m_attn_sinks, m_w_attn_o, m_conv_w_dw, m_conv_b_dw, m_conv_ln_g, m_conv_ln_b, m_w_conv_o, m_w_out, m_norm_ffn2_g, m_ffn2_w_gate, m_ffn2_w_up, m_ffn2_w_down, m_final_norm_g, v_w_ada, v_b_ada, v_norm_ffn1_g, v_ffn1_w_gate, v_ffn1_w_up, v_ffn1_w_down, v_norm_mix_g, v_w_in, v_attn_sinks, v_w_attn_o, v_conv_w_dw, v_conv_b_dw, v_conv_ln_g, v_conv_ln_b, v_w_conv_o, v_w_out, v_norm_ffn2_g, v_ffn2_w_gate, v_ffn2_w_up, v_ffn2_w_down, v_final_norm_g):
    args = dict(locals())
    B, seq, D = x.shape
    T = B * seq
    xi, yi, ci = _position()
    chip = 2 * xi + yi
    dev = 4 * xi + 2 * yi + ci
    big = {n: args[n][0] for n in BIG_WEIGHTS}
    final_g = final_norm_g[None, :]
    vec_w = {n: (args[n] if n != "final_norm_g" else final_g) for n in VECTORS}

    conv_cols = D // N_CHIP
    conv_flat = jnp.pad(conv_w_dw[0].reshape(-1), (0, 8 * D - CONV_WIDTH * conv_cols)).reshape(8, D)
    first = _gather8(jnp.concatenate([jnp.pad(c, ((0, 8 - B), (0, 0))), conv_flat], axis=0), "gather_c")
    c_all = first[:, :B].reshape(N_DEV * B, D)
    conv_taps = first[::2, 8:].reshape(N_CHIP, 8 * D)[:, :CONV_WIDTH * conv_cols]
    conv_taps = conv_taps.reshape(N_CHIP, CONV_WIDTH, conv_cols).transpose(1, 0, 2).reshape(CONV_WIDTH, D)
    conv_taps = jnp.pad(conv_taps, ((0, CONV_PAD - CONV_WIDTH), (0, 0)))

    ada_cols = w_ada.shape[2]
    b_cols = lax.dynamic_slice(b_ada, (0, chip * ada_cols), (1, ada_cols))
    mod_part = _ada_fwd(c_all, w_ada[0], b_cols).reshape(N_DEV, B, ada_cols)
    mod = _mod_exchange(mod_part).transpose(1, 0, 2).reshape(B, N_MOD, D)

    core_idx = jnp.reshape(ci, (1,)).astype(jnp.int32)
    chip_idx = jnp.reshape(chip, (1,)).astype(jnp.int32)
    core_chip = jnp.stack([ci, chip]).astype(jnp.int32)
    slots = {n: _cast_slot(big[n], chip_idx, "cast_" + n) for n in BIG_WEIGHTS}

    small = dict(vec_w)
    small["attn_sinks"] = attn_sinks
    small["conv_w_dw"] = conv_taps

    loss_parts, dx, reduced, dmod, vec_grads, d_sinks, d_conv_w = _local_grads(
        x.reshape(T, D), loss_target.reshape(T, D), mod, slots, small, seq, core_idx, core_chip)

    loss = lax.psum((0.5 / D) * jnp.sum(loss_parts), ("x", "y", "c"))
    grad_x = dx.reshape(B, seq, D)

    out = {}
    for n in BIG_WEIGHTS:
        shape = args[n].shape
        g = reduced[n].reshape(shape[1:])
        d, mn, vn = _adam_call(big[n], g, args["m_" + n][0], args["v_" + n][0], "adam_" + n)
        out[n] = tuple(t.reshape(shape) for t in (g, d, mn, vn))

    block = jnp.zeros((SMALL_ROWS, D), F32)
    block = block.at[ROW_DMOD0:ROW_DMOD0 + N_MOD].set(dmod[0]).at[ROW_DMOD1:ROW_DMOD1 + N_MOD].set(dmod[1])
    block = block.at[ROW_VEC:ROW_VEC + len(VECTORS)].set(jnp.concatenate([vec_grads[n] for n in VECTORS], axis=0))
    block = block.at[ROW_SINK, :2 * HEAD_DIM].set(d_sinks[0])
    block = block.at[ROW_CONVW:ROW_CONVW + CONV_WIDTH].set(d_conv_w[:CONV_WIDTH])
    small_all = _gather8(block, "gather_small_grads")

    def pack_small(prefix):
        rows = [args[prefix + "b_ada"].reshape(N_MOD, D)]
        rows += [args[prefix + n].reshape(1, D) for n in VECTORS]
        rows += [jnp.pad(args[prefix + "attn_sinks"], ((0, 0), (0, D - N_Q_HEADS)))]
        return jnp.pad(jnp.concatenate(rows, axis=0), ((0, 24 - N_MOD - len(VECTORS) - 1), (0, 0)))

    small_sum, sg, sd, sm, sv = _small_adam(small_all, pack_small(""), pack_small("m_"), pack_small("v_"),
                                           ROW_DMOD0, ROW_DMOD1, ROW_VEC)

    def unpack_small(t):
        res = {"b_ada": t[:N_MOD].reshape(1, N_MOD * D)}
        for k, n in enumerate(VECTORS):
            res[n] = t[N_MOD + k].reshape(args[n].shape)
        res["attn_sinks"] = t[N_MOD + len(VECTORS), :N_Q_HEADS].reshape(1, N_Q_HEADS)
        return res

    unpacked = [unpack_small(t) for t in (sg, sd, sm, sv)]
    for n in ("b_ada", "attn_sinks") + VECTORS:
        out[n] = tuple(u[n] for u in unpacked)

    conv_g = lax.dynamic_slice(small_sum, (ROW_CONVW, chip * conv_cols), (CONV_WIDTH, conv_cols))
    d, mn, vn = _adam_call(conv_w_dw[0], conv_g, m_conv_w_dw[0], v_conv_w_dw[0], "adam_conv_w_dw")
    out["conv_w_dw"] = tuple(t[None] for t in (conv_g, d, mn, vn))

    dmod_rows = jnp.stack([small_all[:, ROW_DMOD0:ROW_DMOD0 + N_MOD], small_all[:, ROW_DMOD1:ROW_DMOD1 + N_MOD]], axis=1)
    dmod_all = dmod_rows.reshape(N_DEV * B, N_MOD * D)
    dmod_cols = lax.dynamic_slice(dmod_all, (0, chip * ada_cols), (N_DEV * B, ada_cols))
    out["w_ada"] = tuple(t[None] for t in _ada_adam(c_all.T, dmod_cols, w_ada[0], m_w_ada[0], v_w_ada[0]))

    order = ("w_ada", "b_ada", "norm_ffn1_g", "ffn1_w_gate", "ffn1_w_up", "ffn1_w_down", "norm_mix_g", "w_in",
             "attn_sinks", "w_attn_o", "conv_w_dw", "conv_b_dw", "conv_ln_g", "conv_ln_b", "w_conv_o", "w_out",
             "norm_ffn2_g", "ffn2_w_gate", "ffn2_w_up", "ffn2_w_down", "final_norm_g")
    return (loss, grad_x, *[out[n][0] for n in order], *[out[n][1] for n in order],
            *[out[n][2] for n in order], *[out[n][3] for n in order])
```

```python
import functools

import jax
import jax.numpy as jnp
from jax import lax
from jax.experimental import pallas as pl
from jax.experimental.pallas import tpu as pltpu

F32 = jnp.float32
BF16 = jnp.bfloat16

D_MODEL = 1024
D_FF = 2816
N_CHIP = 4
N_DEV = 8
FF_SHARD = D_FF // N_CHIP
IN_WIDTH = 5376
IN_SHARD = IN_WIDTH // N_CHIP
HEAD_DIM = 64
N_Q_HEADS = 16
N_KV_HEADS = 2
BLOCK = 128
CONV_WIDTH = 31
CONV_PAD = 32
N_MOD = 9
EPS = 1e-6
FFN_RESIDUAL = 0.5
ATTN_SCALE = HEAD_DIM ** -0.5
MASK_VALUE = -1e30

ADAM_LR = 0.001
ADAM_B1 = 0.9
ADAM_B2 = 0.999
ADAM_EPS = 1e-08
ADAM_WD = 0.01
ADAM_STEP = 10

COLB_Q, COLB_CA, COLB_CB, COLB_GA, COLB_GC = 0, 1, 2, 3, 4
COLB_K, COLB_V = 40, 41
PROJ_TILE = 768

VMEM_LIMIT = 56 * 1024 * 1024
MESH = pl.DeviceIdType.MESH
ANY = pl.BlockSpec(memory_space=pl.ANY)
VMEM_SPEC = pl.BlockSpec(memory_space=pltpu.VMEM)
SMEM_SPEC = pl.BlockSpec(memory_space=pltpu.SMEM)


def _params(n_grid):
    return pltpu.CompilerParams(dimension_semantics=("arbitrary",) * n_grid, vmem_limit_bytes=VMEM_LIMIT)


def _tile(n, pref):
    t = min(n, pref)
    while n % t:
        t //= 2
    return t


def _sigmoid(v):
    return 1.0 / (1.0 + jnp.exp(-v))


def _dot_nn(a, b):
    return lax.dot_general(a, b, (((1,), (0,)), ((), ())), preferred_element_type=F32)


def _dot_nt(a, b):
    return lax.dot_general(a, b, (((1,), (1,)), ((), ())), preferred_element_type=F32)


def _dot_tn(a, b):
    return lax.dot_general(a, b, (((0,), (0,)), ((), ())), preferred_element_type=F32)


def _norm_mod(xv, gn, sc, sh):
    r = lax.rsqrt(jnp.mean(xv * xv, axis=-1, keepdims=True) + EPS)
    return ((xv * r) * gn) * (1.0 + sc) + sh


def _accumulate(ref, first, value):
    @pl.when(first)
    def _():
        ref[...] = value

    @pl.when(jnp.logical_not(first))
    def _():
        ref[...] += value


def _norm_mod_bwd(dh, xv, gn, sc, dxo, first_of_batch, first, dx_ref, dsc_ref, dsh_ref, dgn_ref):
    r = lax.rsqrt(jnp.mean(xv * xv, axis=-1, keepdims=True) + EPS)
    xh = xv * r
    _accumulate(dsh_ref, first_of_batch, jnp.sum(dh, axis=0, keepdims=True))
    _accumulate(dsc_ref, first_of_batch, jnp.sum(dh * (xh * gn), axis=0, keepdims=True))
    dn = dh * (1.0 + sc)
    _accumulate(dgn_ref, first, jnp.sum(dn * xh, axis=0, keepdims=True))
    dxh = dn * gn
    dx_ref[...] = dxo + r * (dxh - xh * jnp.mean(dxh * xh, axis=-1, keepdims=True))


CHIP_FLIPS = ((1, 0), (0, 1), (1, 1))


def _position():
    return lax.axis_index("x"), lax.axis_index("y"), lax.axis_index("c")


def _flip(v, f):
    return 1 - v if f else v


class _GatherComm:
    def __init__(self, bufs):
        n = len(bufs)
        self.n = n
        self.operands = list(bufs)
        self.out_shape = [jax.ShapeDtypeStruct(b.shape, b.dtype) for b in bufs]
        self.aliases = {i: i for i in range(n)}
        self.sems = [pltpu.SemaphoreType.DMA((6 * n,)), pltpu.SemaphoreType.DMA((6 * n,))]
        self.rows = [b.shape[1] // 2 for b in bufs]

    def _half(self, ref, i, which):
        return ref.at[pl.ds(which * self.rows[i], self.rows[i]), :]

    def _ici(self, cins, couts, sems, i, k, dst_chip, to):
        x, y, c = _position()
        return pltpu.make_async_remote_copy(
            src_ref=self._half(cins[i].at[2 * x + y], i, c), dst_ref=self._half(couts[i].at[dst_chip], i, c),
            send_sem=sems[0].at[3 * i + k], recv_sem=sems[1].at[3 * i + k], device_id=to, device_id_type=MESH)

    def _d2d(self, couts, sems, i, k, src_chip, which):
        x, y, c = _position()
        place = self._half(couts[i].at[src_chip], i, which)
        return pltpu.make_async_remote_copy(
            src_ref=place, dst_ref=place, send_sem=sems[0].at[3 * self.n + 3 * i + k],
            recv_sem=sems[1].at[3 * self.n + 3 * i + k], device_id=(x, y, 1 - c), device_id_type=MESH)

    def _peers(self):
        x, y, _ = _position()
        return [(_flip(x, fx), _flip(y, fy)) for fx, fy in CHIP_FLIPS]

    def start(self, cins, couts, sems):
        x, y, c = _position()
        for i in range(self.n):
            for k, (px, py) in enumerate(self._peers()):
                self._ici(cins, couts, sems, i, k, 2 * x + y, (px, py, c)).start()

    def finish(self, cins, couts, sems):
        _, _, c = _position()
        peers = self._peers()
        for i in range(self.n):
            for k, (px, py) in enumerate(peers):
                self._ici(cins, couts, sems, i, k, 2 * px + py, (px, py, c)).wait_recv()
                self._d2d(couts, sems, i, k, 2 * px + py, c).start()
        for i in range(self.n):
            for k, (px, py) in enumerate(peers):
                self._d2d(couts, sems, i, k, 2 * px + py, 1 - c).wait_recv()
        for i in range(self.n):
            for k, (px, py) in enumerate(peers):
                self._ici(cins, couts, sems, i, k, 2 * px + py, (px, py, c)).wait_send()
                self._d2d(couts, sems, i, k, 2 * px + py, c).wait_send()


class _ExchangeComm:
    def __init__(self, pairs):
        n = len(pairs)
        self.n = n
        self.operands = list(pairs)
        self.out_shape = [jax.ShapeDtypeStruct((3,) + p.shape[1:], p.dtype) for p in pairs]
        self.aliases = {}
        self.sems = [pltpu.SemaphoreType.DMA((3 * n,)), pltpu.SemaphoreType.DMA((3 * n,))]

    def _copies(self, cins, couts, sems):
        x, y, c = _position()
        peers = [(_flip(x, fx), _flip(y, fy)) for fx, fy in CHIP_FLIPS]
        return [pltpu.make_async_remote_copy(
            src_ref=cins[i].at[2 * px + py], dst_ref=couts[i].at[k], send_sem=sems[0].at[3 * i + k],
            recv_sem=sems[1].at[3 * i + k], device_id=(px, py, c), device_id_type=MESH)
            for i in range(self.n) for k, (px, py) in enumerate(peers)]

    def start(self, cins, couts, sems):
        for cp in self._copies(cins, couts, sems):
            cp.start()

    def finish(self, cins, couts, sems):
        for cp in self._copies(cins, couts, sems):
            cp.wait()


def _call(body, *, name, grid, in_specs, out_specs, out_shape, operands, scratch_shapes=(), comm=None):
    n_grid = len(grid)
    if comm is None:
        return pl.pallas_call(
            body, name=name, grid=grid, in_specs=list(in_specs), out_specs=list(out_specs), out_shape=list(out_shape),
            scratch_shapes=list(scratch_shapes), compiler_params=_params(n_grid))(*operands), ()
    counts = (len(in_specs), len(comm.operands), len(out_specs), len(comm.out_shape), len(scratch_shapes),
              len(comm.sems))

    def fused(*refs):
        parts, pos = [], 0
        for k in counts:
            parts.append(refs[pos:pos + k])
            pos += k
        ins, cins, outs, couts, scr, sems = parts
        first = functools.reduce(jnp.logical_and, [pl.program_id(d) == 0 for d in range(n_grid)])
        last = functools.reduce(jnp.logical_and, [pl.program_id(d) == grid[d] - 1 for d in range(n_grid)])

        @pl.when(first)
        def _():
            comm.start(cins, couts, sems)

        body(*ins, *outs, *scr)

        @pl.when(last)
        def _():
            comm.finish(cins, couts, sems)

    res = pl.pallas_call(
        fused, name=name, grid=grid, in_specs=list(in_specs) + [ANY] * counts[1],
        out_specs=list(out_specs) + [ANY] * counts[3], out_shape=list(out_shape) + list(comm.out_shape),
        scratch_shapes=list(scratch_shapes) + list(comm.sems),
        input_output_aliases={counts[0] + i: counts[2] + j for i, j in comm.aliases.items()},
        compiler_params=_params(n_grid))(*operands, *comm.operands)
    return res[:counts[2]], res[counts[2]:]


def _run_comm(comm, name):
    k_in, k_out = len(comm.operands), len(comm.out_shape)

    def body(*refs):
        cins, couts, sems = refs[:k_in], refs[k_in:k_in + k_out], refs[k_in + k_out:]
        comm.start(cins, couts, sems)
        comm.finish(cins, couts, sems)

    return pl.pallas_call(
        body, name=name, in_specs=[ANY] * k_in, out_specs=[ANY] * k_out, out_shape=list(comm.out_shape),
        scratch_shapes=list(comm.sems), input_output_aliases=dict(comm.aliases))(*comm.operands)


def _ffn_fwd(x, gn, sc, sh, gate, wg, wu, wd, seq, name, comm=None):
    T, D = x.shape
    J, _, Fs = wg.shape
    tm = _tile(seq, 512)
    nb = seq // tm

    def body(x_ref, gn_ref, sc_ref, sh_ref, gate_ref, wg_ref, wu_ref, wd_ref,
             h_ref, a_ref, u_ref, f_ref, xo_ref, hs, acc):
        j = pl.program_id(1)

        @pl.when(j == 0)
        def _():
            hb = _norm_mod(x_ref[...], gn_ref[...], sc_ref[...], sh_ref[...]).astype(BF16)
            hs[...] = hb
            h_ref[...] = hb
            acc[...] = jnp.zeros_like(acc)

        hb = hs[...]
        a = _dot_nn(hb, wg_ref[...])
        u = _dot_nn(hb, wu_ref[...])
        a_ref[...] = a.astype(BF16)
        u_ref[...] = u.astype(BF16)
        s = ((a * _sigmoid(a)) * u).astype(BF16)
        acc[...] += _dot_nn(s, wd_ref[...])

        @pl.when(j == J - 1)
        def _():
            f = acc[...]
            f_ref[...] = f.astype(BF16)
            xo_ref[...] = x_ref[...] + (FFN_RESIDUAL * gate_ref[...]) * f

    row = pl.BlockSpec((tm, D), lambda i, j: (i, 0))
    vec = pl.BlockSpec((1, D), lambda i, j: (0, 0))
    per_b = pl.BlockSpec((None, 1, D), lambda i, j: (i // nb, 0, 0))
    hid = pl.BlockSpec((None, tm, Fs), lambda i, j: (j, i, 0))
    return _call(
        body, name=name, grid=(T // tm, J),
        in_specs=[row, vec, per_b, per_b, per_b,
                  pl.BlockSpec((None, D, Fs), lambda i, j: (j, 0, 0)),
                  pl.BlockSpec((None, D, Fs), lambda i, j: (j, 0, 0)),
                  pl.BlockSpec((None, Fs, D), lambda i, j: (j, 0, 0))],
        out_specs=[row, hid, hid, row, row],
        out_shape=[jax.ShapeDtypeStruct((T, D), BF16), jax.ShapeDtypeStruct((J, T, Fs), BF16),
                   jax.ShapeDtypeStruct((J, T, Fs), BF16), jax.ShapeDtypeStruct((T, D), BF16),
                   jax.ShapeDtypeStruct((T, D), F32)],
        scratch_shapes=[pltpu.VMEM((tm, D), BF16), pltpu.VMEM((tm, D), F32)],
        operands=(x, gn, sc, sh, gate, wg, wu, wd), comm=comm)


def _ffn_bwd(dxo, x, f, a, u, gn, sc, gate, wg, wu, wd, seq, name, comm=None):
    T, D = x.shape
    J, _, Fs = wg.shape
    B = T // seq
    tm = _tile(seq, 512)
    nb = seq // tm

    def body(dxo_ref, x_ref, f_ref, a_ref, u_ref, gn_ref, sc_ref, gate_ref, wg_ref, wu_ref, wd_ref,
             da_ref, du_ref, s_ref, df_ref, dx_ref, dgate_ref, dsc_ref, dsh_ref, dgn_ref, dfs, acc):
        i = pl.program_id(0)
        j = pl.program_id(1)
        first_of_batch = i % nb == 0

        @pl.when(j == 0)
        def _():
            dxo_v = dxo_ref[...]
            dfb = ((FFN_RESIDUAL * gate_ref[...]) * dxo_v).astype(BF16)
            dfs[...] = dfb
            df_ref[...] = dfb
            part = jnp.sum((FFN_RESIDUAL * f_ref[...].astype(F32)) * dxo_v, axis=0, keepdims=True)
            _accumulate(dgate_ref, first_of_batch, part)
            acc[...] = jnp.zeros_like(acc)

        ds = _dot_nt(dfs[...], wd_ref[...])
        av = a_ref[...].astype(F32)
        uv = u_ref[...].astype(F32)
        sig = _sigmoid(av)
        sil = av * sig
        s_ref[...] = (sil * uv).astype(BF16)
        dab = (ds * uv * (sig * (1.0 + av * (1.0 - sig)))).astype(BF16)
        dub = (ds * sil).astype(BF16)
        da_ref[...] = dab
        du_ref[...] = dub
        acc[...] += _dot_nt(dab, wg_ref[...]) + _dot_nt(dub, wu_ref[...])

        @pl.when(j == J - 1)
        def _():
            _norm_mod_bwd(acc[...], x_ref[...], gn_ref[...], sc_ref[...], dxo_ref[...],
                          first_of_batch, i == 0, dx_ref, dsc_ref, dsh_ref, dgn_ref)

    row = pl.BlockSpec((tm, D), lambda i, j: (i, 0))
    vec = pl.BlockSpec((1, D), lambda i, j: (0, 0))
    per_b = pl.BlockSpec((None, 1, D), lambda i, j: (i // nb, 0, 0))
    hid = pl.BlockSpec((None, tm, Fs), lambda i, j: (j, i, 0))
    hid_shape = jax.ShapeDtypeStruct((J, T, Fs), BF16)
    per_b_shape = jax.ShapeDtypeStruct((B, 1, D), F32)
    return _call(
        body, name=name, grid=(T // tm, J),
        in_specs=[row, row, row, hid, hid, vec, per_b, per_b,
                  pl.BlockSpec((None, D, Fs), lambda i, j: (j, 0, 0)),
                  pl.BlockSpec((None, D, Fs), lambda i, j: (j, 0, 0)),
                  pl.BlockSpec((None, Fs, D), lambda i, j: (j, 0, 0))],
        out_specs=[hid, hid, hid, row, row, per_b, per_b, per_b, vec],
        out_shape=[hid_shape, hid_shape, hid_shape, jax.ShapeDtypeStruct((T, D), BF16),
                   jax.ShapeDtypeStruct((T, D), F32), per_b_shape, per_b_shape, per_b_shape,
                   jax.ShapeDtypeStruct((1, D), F32)],
        scratch_shapes=[pltpu.VMEM((tm, D), BF16), pltpu.VMEM((tm, D), F32)],
        operands=(dxo, x, f, a, u, gn, sc, gate, wg, wu, wd), comm=comm)


def _wgrad(a, a_spec, b, b_spec, rows, cols, n_tok, name, comm=None):
    tk = _tile(n_tok, 512)
    nk = n_tok // tk
    half = rows // 2

    def body(a_ref, b_ref, o32_ref, o16_ref, acc):
        k = pl.program_id(1)

        @pl.when(k == 0)
        def _():
            acc[...] = jnp.zeros_like(acc)

        acc[...] += _dot_tn(a_ref[...], b_ref[...])

        @pl.when(k == nk - 1)
        def _():
            for h in range(2):
                v = acc[h * half:(h + 1) * half, :]
                o32_ref[h] = v
                o16_ref[h] = v.astype(BF16)

    out_spec = pl.BlockSpec((2, None, half, cols), lambda j, k: (0, j, 0, 0))
    outs, comm_outs = _call(
        body, name=name, grid=(N_CHIP, nk),
        in_specs=[a_spec(tk), b_spec(tk)],
        out_specs=[out_spec, out_spec],
        out_shape=[jax.ShapeDtypeStruct((2, N_CHIP, half, cols), F32),
                   jax.ShapeDtypeStruct((2, N_CHIP, half, cols), BF16)],
        scratch_shapes=[pltpu.VMEM((rows, cols), F32)],
        operands=(a, b), comm=comm)
    return outs if comm is None else (outs, comm_outs)


def _spec_rows(width):
    return lambda tk: pl.BlockSpec((tk, width), lambda j, k: (k, 0))


def _spec_chip_major(width):
    return lambda tk: pl.BlockSpec((None, tk, width), lambda j, k: (j, k, 0))


def _spec_col_block(width):
    return lambda tk: pl.BlockSpec((tk, width), lambda j, k: (k, j))


def _in_proj(x, gn, sc, sh, w_in, seq, comm=None):
    T, D = x.shape
    N = w_in.shape[1]
    tm = _tile(seq, 1024)
    nb = seq // tm

    def body(x_ref, gn_ref, sc_ref, sh_ref, w_ref, h_ref, p_ref, hs):
        @pl.when(pl.program_id(1) == 0)
        def _():
            hb = _norm_mod(x_ref[...], gn_ref[...], sc_ref[...], sh_ref[...]).astype(BF16)
            hs[...] = hb
            h_ref[...] = hb

        p_ref[...] = _dot_nn(hs[...], w_ref[...]).astype(BF16)

    row = pl.BlockSpec((tm, D), lambda i, j: (i, 0))
    per_b = pl.BlockSpec((None, 1, D), lambda i, j: (i // nb, 0, 0))
    return _call(
        body, name="mix_in_proj", grid=(T // tm, N // PROJ_TILE),
        in_specs=[row, pl.BlockSpec((1, D), lambda i, j: (0, 0)), per_b, per_b,
                  pl.BlockSpec((D, PROJ_TILE), lambda i, j: (0, j))],
        out_specs=[row, pl.BlockSpec((tm, PROJ_TILE), lambda i, j: (i, j))],
        out_shape=[jax.ShapeDtypeStruct((T, D), BF16), jax.ShapeDtypeStruct((T, N), BF16)],
        scratch_shapes=[pltpu.VMEM((tm, D), BF16)],
        operands=(x, gn, sc, sh, w_in), comm=comm)


def _attn_specs(nblk):
    def own(col):
        return lambda b, n: (b * nblk + n, col)

    def prev(col):
        return lambda b, n: (b * nblk + jnp.maximum(n - 1, 0), col)

    kv = (BLOCK, 2 * HEAD_DIM)
    return [pl.BlockSpec((BLOCK, D_MODEL), own(COLB_Q)),
            pl.BlockSpec(kv, prev(COLB_K)), pl.BlockSpec(kv, own(COLB_K)),
            pl.BlockSpec(kv, prev(COLB_V)), pl.BlockSpec(kv, own(COLB_V))]


def _band_operands(prev_ref, own_ref, lo):
    band = jnp.concatenate([prev_ref[...], own_ref[...]], axis=0).astype(F32)
    rolled = pltpu.roll(band, HEAD_DIM, 1)
    zero = jnp.zeros_like(band)
    head0 = (jnp.where(lo, band, zero).astype(BF16), jnp.where(lo, zero, rolled).astype(BF16))
    head1 = (jnp.where(lo, rolled, zero).astype(BF16), jnp.where(lo, zero, band).astype(BF16))
    return head0, head1


def _band_valid(has_prev):
    qi = lax.broadcasted_iota(jnp.int32, (BLOCK, 2 * BLOCK), 0)
    sj = lax.broadcasted_iota(jnp.int32, (BLOCK, 2 * BLOCK), 1)
    rel = qi + BLOCK - sj
    return (rel >= 0) & (rel < BLOCK) & ((sj >= BLOCK) | has_prev)


def _attn_fwd(proj, sinks, batch, seq, comm=None):
    T = proj.shape[0]
    nblk = seq // BLOCK

    def body(sink_ref, q_ref, kp_ref, ko_ref, vp_ref, vo_ref, o_ref, lse_ref):
        lo = lax.broadcasted_iota(jnp.int32, (1, 2 * HEAD_DIM), 1) < HEAD_DIM
        head_lane = lax.broadcasted_iota(jnp.int32, (1, N_Q_HEADS), 1)
        valid = _band_valid(pl.program_id(1) > 0)
        k_ops = _band_operands(kp_ref, ko_ref, lo)
        v_ops = _band_operands(vp_ref, vo_ref, lo)
        lse_all = jnp.zeros((BLOCK, N_Q_HEADS), F32)
        for pair in range(N_Q_HEADS // 2):
            kvh = pair // (N_Q_HEADS // 2 // N_KV_HEADS)
            q2 = q_ref[:, pair * 2 * HEAD_DIM:(pair + 1) * 2 * HEAD_DIM]
            out = jnp.zeros((BLOCK, 2 * HEAD_DIM), F32)
            for side in range(2):
                head = 2 * pair + side
                sink = sink_ref[0, head]
                s = jnp.where(valid, _dot_nt(q2, k_ops[kvh][side]) * ATTN_SCALE, MASK_VALUE)
                m = jnp.maximum(jnp.max(s, axis=-1, keepdims=True), sink)
                p = jnp.where(valid, jnp.exp(s - m), 0.0)
                den = jnp.sum(p, axis=-1, keepdims=True) + jnp.exp(sink - m)
                out = out + _dot_nn((p / den).astype(BF16), v_ops[kvh][side])
                lse_all = jnp.where(head_lane == head, m + jnp.log(den), lse_all)
            o_ref[:, pair * 2 * HEAD_DIM:(pair + 1) * 2 * HEAD_DIM] = out.astype(BF16)
        lse_ref[...] = lse_all

    return _call(
        body, name="attn_fwd", grid=(batch, nblk),
        in_specs=[SMEM_SPEC] + _attn_specs(nblk),
        out_specs=[pl.BlockSpec((BLOCK, D_MODEL), lambda b, n: (b * nblk + n, 0)),
                   pl.BlockSpec((BLOCK, N_Q_HEADS), lambda b, n: (b * nblk + n, 0))],
        out_shape=[jax.ShapeDtypeStruct((T, D_MODEL), BF16), jax.ShapeDtypeStruct((T, N_Q_HEADS), F32)],
        operands=(sinks, proj, proj, proj, proj, proj), comm=comm)


def _conv_u(ca, cb):
    return ca.astype(F32) * _sigmoid(cb.astype(F32))


def _conv_specs(ts, tiles_per_seq):
    per_tile = ts // CONV_PAD

    def tile(col):
        return lambda b, t: (b * tiles_per_seq + t, col)

    def before(col):
        return lambda b, t: (jnp.maximum((b * tiles_per_seq + t) * per_tile - 1, 0), col)

    return [pl.BlockSpec((ts, D_MODEL), tile(COLB_CA)), pl.BlockSpec((ts, D_MODEL), tile(COLB_CB)),
            pl.BlockSpec((CONV_PAD, D_MODEL), before(COLB_CA)), pl.BlockSpec((CONV_PAD, D_MODEL), before(COLB_CB))]


SUBLANES = 8


def _fill_upad(upad, ca_ref, cb_ref, cah_ref, cbh_ref, t):
    halo = _conv_u(cah_ref[...], cbh_ref[...])
    upad[0, 0:CONV_PAD, :] = jnp.where(t > 0, halo, jnp.zeros_like(halo))
    upad[0, CONV_PAD:, :] = _conv_u(ca_ref[...], cb_ref[...])


def _fill_shifted(pad):
    rows = pad.shape[1] - SUBLANES
    for b in range(1, SUBLANES):
        pad[b, 0:rows, :] = pad[0, b:b + rows, :]


def _shifted_rows(pad, offset, rows):
    b = offset % SUBLANES
    return pad[b, offset - b:offset - b + rows, :]


def _layernorm_stats(y):
    mu = jnp.mean(y, axis=-1, keepdims=True)
    yc = y - mu
    rstd = lax.rsqrt(jnp.mean(yc * yc, axis=-1, keepdims=True) + EPS)
    return yc * rstd, rstd


def _conv_fwd(proj, w_dw, b_dw, ln_g, ln_b, batch, seq, comm=None):
    T = proj.shape[0]
    ts = _tile(seq, 256)
    nt = seq // ts
    shift = CONV_PAD - (CONV_WIDTH - 1)

    def body(ca_ref, cb_ref, cah_ref, cbh_ref, w_ref, b_ref, g_ref, beta_ref, y_ref, z_ref, upad):
        _fill_upad(upad, ca_ref, cb_ref, cah_ref, cbh_ref, pl.program_id(1))
        _fill_shifted(upad)
        y = jnp.zeros((ts, D_MODEL), F32) + b_ref[...]
        for k in range(CONV_WIDTH):
            y = y + w_ref[k:k + 1, :] * _shifted_rows(upad, shift + k, ts)
        y_ref[...] = y
        lnh, _ = _layernorm_stats(y)
        ln = lnh * g_ref[...] + beta_ref[...]
        z_ref[...] = (ln * _sigmoid(ln)).astype(BF16)

    vec = pl.BlockSpec((1, D_MODEL), lambda b, t: (0, 0))
    row = pl.BlockSpec((ts, D_MODEL), lambda b, t: (b * nt + t, 0))
    return _call(
        body, name="conv_fwd", grid=(batch, nt),
        in_specs=_conv_specs(ts, nt) + [pl.BlockSpec((CONV_PAD, D_MODEL), lambda b, t: (0, 0)), vec, vec, vec],
        out_specs=[row, row],
        out_shape=[jax.ShapeDtypeStruct((T, D_MODEL), F32), jax.ShapeDtypeStruct((T, D_MODEL), BF16)],
        scratch_shapes=[pltpu.VMEM((SUBLANES, ts + CONV_PAD, D_MODEL), F32)],
        operands=(proj, proj, proj, proj, w_dw, b_dw, ln_g, ln_b), comm=comm)


def _merge(o, z, proj, w_ao, w_co, w_out, x, gate, seq):
    T, D = x.shape
    tm = _tile(seq, 512)
    nb = seq // tm

    def body(o_ref, z_ref, ga_ref, gc_ref, wao_ref, wco_ref, wout_ref, x_ref, gate_ref,
             ya_ref, yc_ref, mg_ref, mo_ref, xo_ref):
        ya = _dot_nn(o_ref[...], wao_ref[...])
        yc = _dot_nn(z_ref[...], wco_ref[...])
        ya_ref[...] = ya.astype(BF16)
        yc_ref[...] = yc.astype(BF16)
        merged = (_sigmoid(ga_ref[...].astype(F32)) * ya + _sigmoid(gc_ref[...].astype(F32)) * yc).astype(BF16)
        mg_ref[...] = merged
        mo = _dot_nn(merged, wout_ref[...])
        mo_ref[...] = mo.astype(BF16)
        xo_ref[...] = x_ref[...] + gate_ref[...] * mo

    row = pl.BlockSpec((tm, D), lambda i: (i, 0))
    mat = pl.BlockSpec((D, D), lambda i: (0, 0))
    act = jax.ShapeDtypeStruct((T, D), BF16)
    return pl.pallas_call(
        body, name="mix_merge", grid=(T // tm,),
        in_specs=[row, row, pl.BlockSpec((tm, D), lambda i: (i, COLB_GA)), pl.BlockSpec((tm, D), lambda i: (i, COLB_GC)),
                  mat, mat, mat, row, pl.BlockSpec((None, 1, D), lambda i: (i // nb, 0, 0))],
        out_specs=[row, row, row, row, row],
        out_shape=[act, act, act, act, jax.ShapeDtypeStruct((T, D), F32)],
        compiler_params=_params(1),
    )(o, z, proj, proj, w_ao, w_co, w_out, x, gate)


def _final_loss(x, gf, target):
    T, D = x.shape
    tm = _tile(T, 512)

    def body(x_ref, gf_ref, t_ref, dx_ref, lp_ref, dgf_ref):
        first = pl.program_id(0) == 0
        xv = x_ref[...]
        gfv = gf_ref[...]
        r = lax.rsqrt(jnp.mean(xv * xv, axis=-1, keepdims=True) + EPS)
        xh = xv * r
        err = xh * gfv - t_ref[...]
        _accumulate(lp_ref, first, jnp.sum(err * err, axis=0, keepdims=True))
        dy = err * (1.0 / D)
        _accumulate(dgf_ref, first, jnp.sum(dy * xh, axis=0, keepdims=True))
        dxh = dy * gfv
        dx_ref[...] = r * (dxh - xh * jnp.mean(dxh * xh, axis=-1, keepdims=True))

    row = pl.BlockSpec((tm, D), lambda i: (i, 0))
    vec = pl.BlockSpec((1, D), lambda i: (0, 0))
    return pl.pallas_call(
        body, name="final_loss", grid=(T // tm,),
        in_specs=[row, vec, row], out_specs=[row, vec, vec],
        out_shape=[jax.ShapeDtypeStruct((T, D), F32), jax.ShapeDtypeStruct((1, D), F32),
                   jax.ShapeDtypeStruct((1, D), F32)],
        compiler_params=_params(1),
    )(x, gf, target)


def _merge_bwd(dxo, mo, gate, proj, ya, yc, w_out, w_ao, w_co, seq):
    T, D = dxo.shape
    B = T // seq
    tm = _tile(seq, 512)
    nb = seq // tm

    def body(dxo_ref, mo_ref, gate_ref, ga_ref, gc_ref, ya_ref, yc_ref, wout_ref, wao_ref, wco_ref,
             dmo_ref, dya_ref, dyc_ref, dga_ref, dgc_ref, do_ref, dz_ref, dgate_ref):
        dxo_v = dxo_ref[...]
        dmo = (gate_ref[...] * dxo_v).astype(BF16)
        dmo_ref[...] = dmo
        _accumulate(dgate_ref, pl.program_id(0) % nb == 0,
                    jnp.sum(mo_ref[...].astype(F32) * dxo_v, axis=0, keepdims=True))
        dm = _dot_nt(dmo, wout_ref[...])
        sa = _sigmoid(ga_ref[...].astype(F32))
        sc = _sigmoid(gc_ref[...].astype(F32))
        dya = (sa * dm).astype(BF16)
        dyc = (sc * dm).astype(BF16)
        dya_ref[...] = dya
        dyc_ref[...] = dyc
        dga_ref[...] = (dm * ya_ref[...].astype(F32) * (sa * (1.0 - sa))).astype(BF16)
        dgc_ref[...] = (dm * yc_ref[...].astype(F32) * (sc * (1.0 - sc))).astype(BF16)
        do_ref[...] = _dot_nt(dya, wao_ref[...]).astype(BF16)
        dz_ref[...] = _dot_nt(dyc, wco_ref[...]).astype(BF16)

    row = pl.BlockSpec((tm, D), lambda i: (i, 0))
    mat = pl.BlockSpec((D, D), lambda i: (0, 0))
    per_b = pl.BlockSpec((None, 1, D), lambda i: (i // nb, 0, 0))
    act = jax.ShapeDtypeStruct((T, D), BF16)
    return pl.pallas_call(
        body, name="mix_merge_bwd", grid=(T // tm,),
        in_specs=[row, row, per_b, pl.BlockSpec((tm, D), lambda i: (i, COLB_GA)),
                  pl.BlockSpec((tm, D), lambda i: (i, COLB_GC)), row, row, mat, mat, mat],
        out_specs=[row] * 7 + [per_b],
        out_shape=[act] * 7 + [jax.ShapeDtypeStruct((B, 1, D), F32)],
        compiler_params=_params(1),
    )(dxo, mo, gate, proj, proj, ya, yc, w_out, w_ao, w_co)


def _attn_bwd(proj, sinks, o, do, lse, batch, seq, comm=None):
    T = proj.shape[0]
    nblk = seq // BLOCK
    n_steps = batch * nblk
    pairs_per_kv = N_Q_HEADS // 2 // N_KV_HEADS

    def body(sink_ref, q_ref, kp_ref, ko_ref, vp_ref, vo_ref, o_ref, do_ref, lse_ref,
             dq_ref, dkp_ref, dko_ref, dvp_ref, dvo_ref, dsink_ref):
        lo = lax.broadcasted_iota(jnp.int32, (1, 2 * HEAD_DIM), 1) < HEAD_DIM
        sink_lane = lax.broadcasted_iota(jnp.int32, (1, 2 * HEAD_DIM), 1)
        valid = _band_valid(pl.program_id(1) > 0)
        k_ops = _band_operands(kp_ref, ko_ref, lo)
        v_ops = _band_operands(vp_ref, vo_ref, lo)
        dsink = jnp.zeros((1, 2 * HEAD_DIM), F32)
        dk_heads, dv_heads = [], []
        for kvh in range(N_KV_HEADS):
            dk_acc = jnp.zeros((2 * BLOCK, 2 * HEAD_DIM), F32)
            dv_acc = jnp.zeros((2 * BLOCK, 2 * HEAD_DIM), F32)
            for pp in range(pairs_per_kv):
                pair = kvh * pairs_per_kv + pp
                lanes = slice(pair * 2 * HEAD_DIM, (pair + 1) * 2 * HEAD_DIM)
                q2 = q_ref[:, lanes]
                do2 = do_ref[:, lanes]
                dd = do2.astype(F32) * o_ref[:, lanes].astype(F32)
                dq2 = jnp.zeros((BLOCK, 2 * HEAD_DIM), F32)
                for side in range(2):
                    head = 2 * pair + side
                    mine = lo if side == 0 else jnp.logical_not(lo)
                    sink = sink_ref[0, head]
                    lse_h = lse_ref[:, head:head + 1]
                    delta = jnp.sum(jnp.where(mine, dd, 0.0), axis=-1, keepdims=True)
                    s = _dot_nt(q2, k_ops[kvh][side]) * ATTN_SCALE
                    p = jnp.where(valid, jnp.exp(jnp.where(valid, s, MASK_VALUE) - lse_h), 0.0)
                    dp = _dot_nt(do2, v_ops[kvh][side])
                    ds = (p * (dp - delta) * ATTN_SCALE).astype(BF16)
                    dq2 = dq2 + _dot_nn(ds, k_ops[kvh][side])
                    dk_acc = dk_acc + jnp.where(mine, _dot_tn(ds, q2), 0.0)
                    dv_acc = dv_acc + jnp.where(mine, _dot_tn(p.astype(BF16), do2), 0.0)
                    dsink = dsink + jnp.where(sink_lane == head, -jnp.sum(jnp.exp(sink - lse_h) * delta), 0.0)
                dq_ref[:, lanes] = dq2.astype(BF16)
            dk_heads.append(dk_acc + pltpu.roll(dk_acc, HEAD_DIM, 1))
            dv_heads.append(dv_acc + pltpu.roll(dv_acc, HEAD_DIM, 1))
        dk = jnp.where(lo, dk_heads[0], dk_heads[1])
        dv = jnp.where(lo, dv_heads[0], dv_heads[1])
        dkp_ref[...] = dk[:BLOCK]
        dko_ref[...] = dk[BLOCK:]
        dvp_ref[...] = dv[:BLOCK]
        dvo_ref[...] = dv[BLOCK:]
        dsink_ref[...] = dsink

    def own(b, n):
        return (b * nblk + n, 0)

    row = pl.BlockSpec((BLOCK, D_MODEL), own)
    kv = pl.BlockSpec((BLOCK, 2 * HEAD_DIM), own)
    kv_shape = jax.ShapeDtypeStruct((T, 2 * HEAD_DIM), F32)
    return _call(
        body, name="attn_bwd", grid=(batch, nblk),
        in_specs=[SMEM_SPEC] + _attn_specs(nblk) + [row, row, pl.BlockSpec((BLOCK, N_Q_HEADS), own)],
        out_specs=[row, kv, kv, kv, kv, pl.BlockSpec((None, 1, 2 * HEAD_DIM), lambda b, n: (b * nblk + n, 0, 0))],
        out_shape=[jax.ShapeDtypeStruct((T, D_MODEL), BF16), kv_shape, kv_shape, kv_shape, kv_shape,
                   jax.ShapeDtypeStruct((n_steps, 1, 2 * HEAD_DIM), F32)],
        operands=(sinks, proj, proj, proj, proj, proj, o, do, lse), comm=comm)


def _conv_bwd(proj, dz, ydw, w_dw, ln_g, ln_b, batch, seq):
    T = proj.shape[0]
    ts = _tile(seq, 256)
    nt = seq // ts
    per_tile = ts // CONV_PAD
    shift = CONV_PAD - (CONV_WIDTH - 1)

    def body(ca_ref, cb_ref, cah_ref, cbh_ref, dz_ref, dzn_ref, y_ref, yn_ref, w_ref, g_ref, beta_ref,
             dca_ref, dcb_ref, dw_ref, db_ref, dg_ref, dbeta_ref, upad, dypad):
        t = pl.program_id(1)
        first = (pl.program_id(0) == 0) & (t == 0)
        gv = g_ref[...]

        def ln_bwd(dzv, yv):
            lnh, rstd = _layernorm_stats(yv)
            ln = lnh * gv + beta_ref[...]
            sg = _sigmoid(ln)
            dln = dzv.astype(F32) * (sg * (1.0 + ln * (1.0 - sg)))
            dyh = dln * gv
            dy = rstd * (dyh - jnp.mean(dyh, axis=-1, keepdims=True)
                         - lnh * jnp.mean(dyh * lnh, axis=-1, keepdims=True))
            return dy, dln, lnh

        dy, dln, lnh = ln_bwd(dz_ref[...], y_ref[...])
        dy_next, _, _ = ln_bwd(dzn_ref[...], yn_ref[...])
        dypad[0, 0:ts, :] = dy
        dypad[0, ts:, :] = jnp.where(t < nt - 1, dy_next, jnp.zeros_like(dy_next))
        _fill_shifted(dypad)
        _fill_upad(upad, ca_ref, cb_ref, cah_ref, cbh_ref, t)
        _fill_shifted(upad)

        _accumulate(dg_ref, first, jnp.sum(dln * lnh, axis=0, keepdims=True))
        _accumulate(dbeta_ref, first, jnp.sum(dln, axis=0, keepdims=True))
        _accumulate(db_ref, first, jnp.sum(dy, axis=0, keepdims=True))

        @pl.when(first)
        def _():
            dw_ref[...] = jnp.zeros_like(dw_ref)

        du = jnp.zeros((ts, D_MODEL), F32)
        for k in range(CONV_WIDTH):
            du = du + w_ref[k:k + 1, :] * _shifted_rows(dypad, CONV_WIDTH - 1 - k, ts)
            dw_ref[k:k + 1, :] += jnp.sum(dy * _shifted_rows(upad, shift + k, ts), axis=0, keepdims=True)
        cav = ca_ref[...].astype(F32)
        sb = _sigmoid(cb_ref[...].astype(F32))
        dca_ref[...] = (du * sb).astype(BF16)
        dcb_ref[...] = (du * cav * (sb * (1.0 - sb))).astype(BF16)

    def tile(b, t):
        return (b * nt + t, 0)

    def after(b, t):
        return (jnp.minimum((b * nt + t + 1) * per_tile, T // CONV_PAD - 1), 0)

    row = pl.BlockSpec((ts, D_MODEL), tile)
    halo = pl.BlockSpec((CONV_PAD, D_MODEL), after)
    vec = pl.BlockSpec((1, D_MODEL), lambda b, t: (0, 0))
    wspec = pl.BlockSpec((CONV_PAD, D_MODEL), lambda b, t: (0, 0))
    act = jax.ShapeDtypeStruct((T, D_MODEL), BF16)
    vec_shape = jax.ShapeDtypeStruct((1, D_MODEL), F32)
    return pl.pallas_call(
        body, name="conv_bwd", grid=(batch, nt),
        in_specs=_conv_specs(ts, nt) + [row, halo, row, halo, wspec, vec, vec],
        out_specs=[row, row, wspec, vec, vec, vec],
        out_shape=[act, act, jax.ShapeDtypeStruct((CONV_PAD, D_MODEL), F32), vec_shape, vec_shape, vec_shape],
        scratch_shapes=[pltpu.VMEM((SUBLANES, ts + CONV_PAD, D_MODEL), F32)] * 2,
        compiler_params=_params(2),
    )(proj, proj, proj, proj, dz, dz, ydw, ydw, w_dw, ln_g, ln_b)


def _in_proj_bwd(dproj, w_in_g, x, gn, sc, dxo, seq):
    T, D = x.shape
    J, _, W = w_in_g.shape
    B = T // seq
    tm = _tile(seq, 512)
    nb = seq // tm

    def body(dp_ref, w_ref, x_ref, gn_ref, sc_ref, dxo_ref, dx_ref, dsc_ref, dsh_ref, dgn_ref, acc):
        i = pl.program_id(0)
        j = pl.program_id(1)

        @pl.when(j == 0)
        def _():
            acc[...] = jnp.zeros_like(acc)

        acc[...] += _dot_nt(dp_ref[...], w_ref[...])

        @pl.when(j == J - 1)
        def _():
            _norm_mod_bwd(acc[...], x_ref[...], gn_ref[...], sc_ref[...], dxo_ref[...],
                          i % nb == 0, i == 0, dx_ref, dsc_ref, dsh_ref, dgn_ref)

    row = pl.BlockSpec((tm, D), lambda i, j: (i, 0))
    vec = pl.BlockSpec((1, D), lambda i, j: (0, 0))
    per_b = pl.BlockSpec((None, 1, D), lambda i, j: (i // nb, 0, 0))
    per_b_shape = jax.ShapeDtypeStruct((B, 1, D), F32)
    return pl.pallas_call(
        body, name="mix_in_proj_bwd", grid=(T // tm, J),
        in_specs=[pl.BlockSpec((None, tm, W), lambda i, j: (j, i, 0)),
                  pl.BlockSpec((None, D, W), lambda i, j: (j, 0, 0)), row, vec, per_b, row],
        out_specs=[row, per_b, per_b, vec],
        out_shape=[jax.ShapeDtypeStruct((T, D), F32), per_b_shape, per_b_shape, jax.ShapeDtypeStruct((1, D), F32)],
        scratch_shapes=[pltpu.VMEM((tm, D), F32)],
        compiler_params=_params(2),
    )(dproj, w_in_g, x, gn, sc, dxo)


def _ada_fwd(c_all, w_ada, b_cols):
    nbatch, D = c_all.shape
    N = w_ada.shape[1]
    tn = _tile(N, 768)

    def body(c_ref, w_ref, b_ref, o_ref):
        cv = c_ref[...]
        act = (cv * _sigmoid(cv)).astype(BF16)
        o_ref[...] = _dot_nn(act, w_ref[...].astype(BF16)) + b_ref[...]

    return pl.pallas_call(
        body, name="ada_fwd", grid=(N // tn,),
        in_specs=[pl.BlockSpec((nbatch, D), lambda j: (0, 0)), pl.BlockSpec((D, tn), lambda j: (0, j)),
                  pl.BlockSpec((1, tn), lambda j: (0, j))],
        out_specs=pl.BlockSpec((nbatch, tn), lambda j: (0, j)),
        out_shape=jax.ShapeDtypeStruct((nbatch, N), F32),
        compiler_params=_params(1),
    )(c_all, w_ada, b_cols)


def _adamw(w, g, m, v):
    m = ADAM_B1 * m + (1.0 - ADAM_B1) * g
    v = ADAM_B2 * v + (1.0 - ADAM_B2) * (g * g)
    m_hat = m / (1.0 - ADAM_B1 ** ADAM_STEP)
    v_hat = v / (1.0 - ADAM_B2 ** ADAM_STEP)
    delta = -ADAM_LR * (m_hat / (jnp.sqrt(v_hat) + ADAM_EPS) + ADAM_WD * w)
    return delta, m, v


def _adam_call(w, g, m, v, name):
    R, C = w.shape
    tr = R if R % 8 else _tile(R, 256)

    def body(w_ref, g_ref, m_ref, v_ref, d_ref, mo_ref, vo_ref):
        d, mn, vn = _adamw(w_ref[...], g_ref[...], m_ref[...], v_ref[...])
        d_ref[...] = d
        mo_ref[...] = mn
        vo_ref[...] = vn

    blk = pl.BlockSpec((tr, C), lambda i: (i, 0))
    shape = jax.ShapeDtypeStruct((R, C), F32)
    return pl.pallas_call(
        body, name=name, grid=(R // tr,), in_specs=[blk] * 4, out_specs=[blk] * 3, out_shape=[shape] * 3,
        compiler_params=_params(1),
    )(w, g, m, v)


def _ada_adam(c_act_t, dmod_cols, w, m, v, comm):
    R, C = w.shape
    nbatch = c_act_t.shape[1]
    tr = _tile(R, 128)

    def body(ct_ref, dm_ref, w_ref, m_ref, v_ref, g_ref, d_ref, mo_ref, vo_ref):
        cv = ct_ref[...]
        g = _dot_nn((cv * _sigmoid(cv)).astype(BF16), dm_ref[...].astype(BF16))
        g_ref[...] = g
        d, mn, vn = _adamw(w_ref[...], g, m_ref[...], v_ref[...])
        d_ref[...] = d
        mo_ref[...] = mn
        vo_ref[...] = vn

    blk = pl.BlockSpec((tr, C), lambda i: (i, 0))
    shape = jax.ShapeDtypeStruct((R, C), F32)
    return _call(
        body, name="ada_adam", grid=(R // tr,),
        in_specs=[pl.BlockSpec((tr, nbatch), lambda i: (i, 0)), pl.BlockSpec((nbatch, C), lambda i: (0, 0)),
                  blk, blk, blk],
        out_specs=[blk] * 4, out_shape=[shape] * 4,
        operands=(c_act_t, dmod_cols, w, m, v), comm=comm)


def _small_adam(gathered, w, m, v, rows_b0, rows_b1, rows_vec):
    _, P, D = gathered.shape
    R = w.shape[0]

    def body(ga_ref, w_ref, m_ref, v_ref, sum_ref, g_ref, d_ref, mo_ref, vo_ref):
        total = ga_ref[0]
        for dev in range(1, N_DEV):
            total = total + ga_ref[dev]
        sum_ref[...] = total
        g_ref[...] = jnp.zeros_like(g_ref)
        g_ref[0:N_MOD, :] = (sum_ref[rows_b0:rows_b0 + N_MOD, :] + sum_ref[rows_b1:rows_b1 + N_MOD, :])
        g_ref[N_MOD:N_MOD + 8, :] = sum_ref[rows_vec:rows_vec + 8, :]
        d, mn, vn = _adamw(w_ref[...], g_ref[...], m_ref[...], v_ref[...])
        d_ref[...] = d
        mo_ref[...] = mn
        vo_ref[...] = vn

    shape = jax.ShapeDtypeStruct((R, D), F32)
    return pl.pallas_call(
        body, name="small_adam",
        in_specs=[VMEM_SPEC] * 4, out_specs=[VMEM_SPEC] * 5,
        out_shape=[jax.ShapeDtypeStruct((P, D), F32), shape, shape, shape, shape],
        compiler_params=pltpu.CompilerParams(vmem_limit_bytes=VMEM_LIMIT),
    )(gathered, w, m, v)


def _gather8(v, name):
    A, W = v.shape
    flips = [(fx, fy, fc) for fx in (0, 1) for fy in (0, 1) for fc in (0, 1) if (fx, fy, fc) != (0, 0, 0)]

    def body(v_ref, out_ref, send_sems, recv_sems, local_sem):
        x, y, c = _position()
        me = 4 * x + 2 * y + c
        mine = pltpu.make_async_copy(v_ref, out_ref.at[me], local_sem)
        mine.start()

        def copy(k, block, to):
            return pltpu.make_async_remote_copy(src_ref=v_ref, dst_ref=out_ref.at[block], send_sem=send_sems.at[k],
                                                recv_sem=recv_sems.at[k], device_id=to, device_id_type=MESH)

        peers = [(_flip(x, fx), _flip(y, fy), _flip(c, fc)) for fx, fy, fc in flips]
        sends = [copy(k, me, peer) for k, peer in enumerate(peers)]
        for cp in sends:
            cp.start()
        for k, (px, py, pc) in enumerate(peers):
            copy(k, 4 * px + 2 * py + pc, (px, py, pc)).wait_recv()
        for cp in sends:
            cp.wait_send()
        mine.wait()

    return pl.pallas_call(
        body, name=name, in_specs=[VMEM_SPEC], out_specs=VMEM_SPEC,
        out_shape=jax.ShapeDtypeStruct((N_DEV, A, W), v.dtype),
        scratch_shapes=[pltpu.SemaphoreType.DMA((N_DEV - 1,)), pltpu.SemaphoreType.DMA((N_DEV - 1,)),
                        pltpu.SemaphoreType.DMA],
    )(v)


def _mod_exchange(part):
    _, A, W = part.shape

    def body(p_ref, out_ref, send_sems, recv_sems, local_sem):
        x, y, c = _position()
        me = 4 * x + 2 * y + c
        chip = 2 * x + y
        mine = pltpu.make_async_copy(p_ref.at[me], out_ref.at[chip], local_sem)
        mine.start()
        peers = [(_flip(x, fx), _flip(y, fy)) for fx, fy in CHIP_FLIPS]
        sends = []
        for k, (px, py) in enumerate(peers):
            sends.append(pltpu.make_async_remote_copy(
                src_ref=p_ref.at[4 * px + 2 * py + c], dst_ref=out_ref.at[chip], send_sem=send_sems.at[k],
                recv_sem=recv_sems.at[k], device_id=(px, py, c), device_id_type=MESH))
        for cp in sends:
            cp.start()
        for k, (px, py) in enumerate(peers):
            pltpu.make_async_remote_copy(
                src_ref=p_ref.at[me], dst_ref=out_ref.at[2 * px + py], send_sem=send_sems.at[k],
                recv_sem=recv_sems.at[k], device_id=(px, py, c), device_id_type=MESH).wait_recv()
        for cp in sends:
            cp.wait_send()
        mine.wait()

    return pl.pallas_call(
        body, name="mod_exchange", in_specs=[VMEM_SPEC], out_specs=VMEM_SPEC,
        out_shape=jax.ShapeDtypeStruct((N_CHIP, A, W), part.dtype),
        scratch_shapes=[pltpu.SemaphoreType.DMA((3,)), pltpu.SemaphoreType.DMA((3,)), pltpu.SemaphoreType.DMA],
    )(part)


def _cast_slot(w, chip_idx, name):
    R, C = w.shape
    tr = _tile(R, 256)

    def body(chip_ref, w_ref, o_ref):
        o_ref[...] = w_ref[...].astype(BF16)

    return pl.pallas_call(
        body, name=name,
        grid_spec=pltpu.PrefetchScalarGridSpec(
            num_scalar_prefetch=1, grid=(R // tr,),
            in_specs=[pl.BlockSpec((tr, C), lambda i, chip_ref: (i, 0))],
            out_specs=pl.BlockSpec((None, tr, C), lambda i, chip_ref: (chip_ref[0], i, 0))),
        out_shape=jax.ShapeDtypeStruct((N_CHIP, R, C), BF16),
        compiler_params=_params(1),
    )(chip_idx, w)


def _sibling_swap_halves(grads16, name):
    n = len(grads16)

    def body(*refs):
        g_refs, out_refs = refs[:n], refs[n:2 * n]
        send_sems, recv_sems = refs[2 * n:]
        x, y, c = _position()
        copies = [pltpu.make_async_remote_copy(
            src_ref=g_refs[i].at[1 - c], dst_ref=out_refs[i], send_sem=send_sems.at[i], recv_sem=recv_sems.at[i],
            device_id=(x, y, 1 - c), device_id_type=MESH) for i in range(n)]
        for cp in copies:
            cp.start()
        for cp in copies:
            cp.wait()

    return pl.pallas_call(
        body, name=name, in_specs=[ANY] * n, out_specs=[ANY] * n,
        out_shape=[jax.ShapeDtypeStruct(g.shape[1:], g.dtype) for g in grads16],
        scratch_shapes=[pltpu.SemaphoreType.DMA((n,)), pltpu.SemaphoreType.DMA((n,))],
    )(*grads16)


def _pair_sum(g32, recv, core, name):
    _, J, r, C = g32.shape

    def body(core_ref, g_ref, r_ref, o_ref):
        o_ref[...] = (g_ref[...] + r_ref[...].astype(F32)).astype(BF16)

    return pl.pallas_call(
        body, name=name,
        grid_spec=pltpu.PrefetchScalarGridSpec(
            num_scalar_prefetch=1, grid=(J,),
            in_specs=[pl.BlockSpec((None, None, r, C), lambda j, core_ref: (core_ref[0], j, 0, 0)),
                      pl.BlockSpec((None, r, C), lambda j, core_ref: (j, 0, 0))],
            out_specs=pl.BlockSpec((None, r, C), lambda j, core_ref: (j, 0, 0))),
        out_shape=jax.ShapeDtypeStruct((J, r, C), BF16),
        compiler_params=_params(1),
    )(core, g32, recv)


def _chip_sum(g32, recv_sib, recv_chips, core_chip, name):
    _, J, r, C = g32.shape

    def body(idx_ref, g_ref, s_ref, o_ref_in, o_ref):
        total = g_ref[...] + s_ref[...].astype(F32)
        for k in range(3):
            total = total + o_ref_in[k].astype(F32)
        o_ref[...] = total

    return pl.pallas_call(
        body, name=name,
        grid_spec=pltpu.PrefetchScalarGridSpec(
            num_scalar_prefetch=1, grid=(1,),
            in_specs=[pl.BlockSpec((None, None, r, C), lambda i, idx: (idx[0], idx[1], 0, 0)),
                      pl.BlockSpec((None, r, C), lambda i, idx: (idx[1], 0, 0)),
                      pl.BlockSpec((3, r, C), lambda i, idx: (0, 0, 0))],
            out_specs=pl.BlockSpec((None, r, C), lambda i, idx: (idx[0], 0, 0))),
        out_shape=jax.ShapeDtypeStruct((2, r, C), F32),
        compiler_params=_params(1),
    )(core_chip, g32, recv_sib, recv_chips)


def _sibling_join_halves(halves, name):
    n = len(halves)

    def body(*refs):
        h_refs, out_refs = refs[:n], refs[n:2 * n]
        send_sems, recv_sems = refs[2 * n:]
        x, y, c = _position()

        def copy(i, which):
            return pltpu.make_async_remote_copy(
                src_ref=h_refs[i].at[which], dst_ref=out_refs[i].at[which], send_sem=send_sems.at[i],
                recv_sem=recv_sems.at[i], device_id=(x, y, 1 - c), device_id_type=MESH)

        for i in range(n):
            copy(i, c).start()
        for i in range(n):
            copy(i, 1 - c).wait_recv()
        for i in range(n):
            copy(i, c).wait_send()

    return pl.pallas_call(
        body, name=name, in_specs=[ANY] * n, out_specs=[ANY] * n,
        out_shape=[jax.ShapeDtypeStruct(h.shape, h.dtype) for h in halves],
        scratch_shapes=[pltpu.SemaphoreType.DMA((n,)), pltpu.SemaphoreType.DMA((n,))],
        input_output_aliases={i: i for i in range(n)},
    )(*halves)


BIG_WEIGHTS = ("ffn1_w_gate", "ffn1_w_up", "ffn1_w_down", "w_in", "w_attn_o", "w_conv_o", "w_out",
               "ffn2_w_gate", "ffn2_w_up", "ffn2_w_down")
VECTORS = ("norm_ffn1_g", "norm_mix_g", "conv_b_dw", "conv_ln_g", "conv_ln_b", "norm_ffn2_g", "final_norm_g")
ROW_DMOD0, ROW_DMOD1, ROW_VEC, ROW_SINK, ROW_CONVW, SMALL_ROWS = 0, 16, 33, 40, 41, 72


def _reduce_begin(grads, names, core_idx, tag):
    from_sibling = _sibling_swap_halves([grads[n][1] for n in names], "grad_swap_halves_" + tag)
    pair_sums = [_pair_sum(grads[n][0], r, core_idx, "pair_sum_" + n) for n, r in zip(names, from_sibling)]
    return from_sibling, pair_sums


def _reduce_end(grads, names, from_sibling, from_chips, core_chip, tag):
    halves = [_chip_sum(grads[n][0], rs, rc, core_chip, "chip_sum_" + n)
              for n, rs, rc in zip(names, from_sibling, from_chips)]
    return dict(zip(names, _sibling_join_halves(halves, "grad_join_halves_" + tag)))


FFN1_WEIGHTS = ("ffn1_w_gate", "ffn1_w_up", "ffn1_w_down")
FFN2_WEIGHTS = ("ffn2_w_gate", "ffn2_w_up", "ffn2_w_down")
MIX_WEIGHTS = ("w_in", "w_attn_o", "w_conv_o", "w_out")


def _local_grads(x, target, mod, slots, small, seq, core_idx, core_chip):
    T, D = x.shape
    B = T // seq
    mods = [mod[:, k][:, None, :] for k in range(N_MOD)]
    sh1, sc1, g1, sh2, sc2, g2, sh3, sc3, g3 = mods
    w = dict(zip(FFN1_WEIGHTS, _run_comm(_GatherComm([slots[n] for n in FFN1_WEIGHTS]), "gather_ffn1")))

    (h1, a1, u1, f1, x1), (w["w_in"],) = _ffn_fwd(
        x, small["norm_ffn1_g"], sc1, sh1, g1, w["ffn1_w_gate"], w["ffn1_w_up"], w["ffn1_w_down"], seq, "ffn1_fwd",
        comm=_GatherComm([slots["w_in"]]))
    w_in_full = w["w_in"].transpose(1, 0, 2).reshape(D, IN_WIDTH)
    q_end, k_end, v_end = D, D + 2 * HEAD_DIM, D + 4 * HEAD_DIM
    w_in_cols = jnp.concatenate(
        [w_in_full[:, :q_end], w_in_full[:, v_end:], w_in_full[:, q_end:k_end], w_in_full[:, k_end:v_end]], axis=1)
    (h2, proj), outs = _in_proj(x1, small["norm_mix_g"], sc2, sh2, w_in_cols, seq,
                                comm=_GatherComm([slots[n] for n in ("w_attn_o", "w_conv_o", "w_out")]))
    w_ao, w_co, w_o = [t.reshape(D, D) for t in outs]
    (o, lse), (w["ffn2_w_gate"], w["ffn2_w_up"]) = _attn_fwd(
        proj, small["attn_sinks"], B, seq, comm=_GatherComm([slots["ffn2_w_gate"], slots["ffn2_w_up"]]))
    (ydw, z), (w["ffn2_w_down"],) = _conv_fwd(
        proj, small["conv_w_dw"], small["conv_b_dw"], small["conv_ln_g"], small["conv_ln_b"], B, seq,
        comm=_GatherComm([slots["ffn2_w_down"]]))
    ya, yc, merged, mo, x2 = _merge(o, z, proj, w_ao, w_co, w_o, x1, g2, seq)
    (h3, a3, u3, f3, x3), _ = _ffn_fwd(x2, small["norm_ffn2_g"], sc3, sh3, g3, w["ffn2_w_gate"], w["ffn2_w_up"],
                                       w["ffn2_w_down"], seq, "ffn2_fwd")
    dx3, loss_parts, d_final_g = _final_loss(x3, small["final_norm_g"], target)

    grads = {}

    def ffn_backward(prefix, dxo, xin, h, a, u, f, gn, sc, gate, comm=None):
        (da, du, s, df, dx, dgate, dsc, dsh, dgn), comm_out = _ffn_bwd(
            dxo, xin, f, a, u, gn, sc, gate, w[prefix + "_w_gate"], w[prefix + "_w_up"], w[prefix + "_w_down"],
            seq, prefix + "_bwd", comm=comm)
        grads[prefix + "_w_gate"] = _wgrad(h, _spec_rows(D), da, _spec_chip_major(FF_SHARD), D, FF_SHARD, T,
                                           prefix + "_dw_gate")
        grads[prefix + "_w_up"] = _wgrad(h, _spec_rows(D), du, _spec_chip_major(FF_SHARD), D, FF_SHARD, T,
                                         prefix + "_dw_up")
        def down_grad(comm=None):
            return _wgrad(s, _spec_chip_major(FF_SHARD), df, _spec_rows(D), FF_SHARD, D, T, prefix + "_dw_down",
                          comm=comm)

        return (dx, dgate, dsc, dsh, dgn), comm_out, down_grad

    (dx2, dg3, dsc3, dsh3, d_gn3), _, down_grad = ffn_backward(
        "ffn2", dx3, x2, h3, a3, u3, f3, small["norm_ffn2_g"], sc3, g3)
    grads["ffn2_w_down"] = down_grad()
    sib2, pairs2 = _reduce_begin(grads, FFN2_WEIGHTS, core_idx, "ffn2")

    dmo, dya, dyc, dga, dgc, do, dz, dg2 = _merge_bwd(dx2, mo, g2, proj, ya, yc, w_o, w_ao, w_co, seq)
    shard = D // N_CHIP
    grads["w_out"] = _wgrad(merged, _spec_col_block(shard), dmo, _spec_rows(D), shard, D, T, "dw_out")
    grads["w_attn_o"] = _wgrad(o, _spec_col_block(shard), dya, _spec_rows(D), shard, D, T, "dw_attn_o")
    grads["w_conv_o"] = _wgrad(z, _spec_col_block(shard), dyc, _spec_rows(D), shard, D, T, "dw_conv_o")
    (dq, dkp, dko, dvp, dvo, dsink_steps), chips2 = _attn_bwd(proj, small["attn_sinks"], o, do, lse, B, seq,
                                                              comm=_ExchangeComm(pairs2))
    reduced = _reduce_end(grads, FFN2_WEIGHTS, sib2, chips2, core_chip, "ffn2")
    dca, dcb, d_conv_w, d_conv_b, d_ln_g, d_ln_b = _conv_bwd(proj, dz, ydw, small["conv_w_dw"], small["conv_ln_g"],
                                                              small["conv_ln_b"], B, seq)

    def band_sum(own, prev):
        prev = prev.reshape(B, seq // BLOCK, BLOCK, 2 * HEAD_DIM)
        moved = jnp.concatenate([prev[:, 1:], jnp.zeros_like(prev[:, :1])], axis=1)
        return (own + moved.reshape(T, 2 * HEAD_DIM)).astype(BF16)

    dproj = jnp.concatenate([dq, band_sum(dko, dkp), band_sum(dvo, dvp), dca, dcb, dga, dgc], axis=1)
    dproj = dproj.reshape(T, N_CHIP, IN_SHARD).transpose(1, 0, 2)
    grads["w_in"] = _wgrad(h2, _spec_rows(D), dproj, _spec_chip_major(IN_SHARD), D, IN_SHARD, T, "dw_in")
    dx1, dsc2, dsh2, d_gn2 = _in_proj_bwd(dproj, w["w_in"], x1, small["norm_mix_g"], sc2, dx2, seq)
    sib_mix, pairs_mix = _reduce_begin(grads, MIX_WEIGHTS, core_idx, "mix")

    (dx0, dg1, dsc1, dsh1, d_gn1), chips_mix, down_grad = ffn_backward(
        "ffn1", dx1, x, h1, a1, u1, f1, small["norm_ffn1_g"], sc1, g1, comm=_ExchangeComm(pairs_mix))
    reduced.update(_reduce_end(grads, MIX_WEIGHTS, sib_mix, chips_mix, core_chip, "mix"))
    gate_up = FFN1_WEIGHTS[:2]
    sib_gu, pairs_gu = _reduce_begin(grads, gate_up, core_idx, "ffn1_gate_up")
    grads["ffn1_w_down"], chips_gu = down_grad(comm=_ExchangeComm(pairs_gu))
    reduced.update(_reduce_end(grads, gate_up, sib_gu, chips_gu, core_chip, "ffn1_gate_up"))
    sib_d, pairs_d = _reduce_begin(grads, FFN1_WEIGHTS[2:], core_idx, "ffn1_down")

    def finish_reduction(chips_d):
        reduced.update(_reduce_end(grads, FFN1_WEIGHTS[2:], sib_d, chips_d, core_chip, "ffn1_down"))
        return reduced

    dmod = jnp.concatenate([dsh1, dsc1, dg1, dsh2, dsc2, dg2, dsh3, dsc3, dg3], axis=1)
    d_sinks = jnp.sum(dsink_steps, axis=0)
    vec_grads = {"norm_ffn1_g": d_gn1, "norm_mix_g": d_gn2, "conv_b_dw": d_conv_b, "conv_ln_g": d_ln_g,
                 "conv_ln_b": d_ln_b, "norm_ffn2_g": d_gn3, "final_norm_g": d_final_g}
    return loss_parts, dx0, (_ExchangeComm(pairs_d), finish_reduction), dmod, vec_grads, d_sinks, d_conv_w


def kernel(x, c, w_ada, b_ada, norm_ffn1_g, ffn1_w_gate, ffn1_w_up, ffn1_w_down, norm_mix_g, w_in, attn_sinks, w_attn_o, conv_w_dw, conv_b_dw, conv_ln_g, conv_ln_b, w_conv_o, w_out, norm_ffn2_g, ffn2_w_gate, ffn2_w_up, ffn2_w_down, final_norm_g, loss_target, m_w_ada, m_b_ada, m_norm_ffn1_g, m_ffn1_w_gate, m_ffn1_w_up, m_ffn1_w_down, m_norm_mix_g, m_w_in, m_attn_sinks, m_w_attn_o, m_conv_w_dw, m_conv_b_dw, m_conv_ln_g, m_conv_ln_b, m_w_conv_o, m_w_out, m_norm_ffn2_g, m_ffn2_w_gate, m_ffn2_w_up, m_ffn2_w_down, m_final_norm_g, v_w_ada, v_b_ada, v_norm_ffn1_g, v_ffn1_w_gate, v_ffn1_w_up, v_ffn1_w_down, v_norm_mix_g, v_w_in, v_attn_sinks, v_w_attn_o, v_conv_w_dw, v_conv_b_dw, v_conv_ln_g, v_conv_ln_b, v_w_conv_o, v_w_out, v_norm_ffn2_g, v_ffn2_w_gate, v_ffn2_w_up, v_ffn2_w_down, v_final_norm_g):
    args = dict(locals())
    B, seq, D = x.shape
    T = B * seq
    xi, yi, ci = _position()
    chip = 2 * xi + yi
    dev = 4 * xi + 2 * yi + ci
    big = {n: args[n][0] for n in BIG_WEIGHTS}
    final_g = final_norm_g[None, :]
    vec_w = {n: (args[n] if n != "final_norm_g" else final_g) for n in VECTORS}

    conv_cols = D // N_CHIP
    conv_flat = jnp.pad(conv_w_dw[0].reshape(-1), (0, 8 * D - CONV_WIDTH * conv_cols)).reshape(8, D)
    first = _gather8(jnp.concatenate([jnp.pad(c, ((0, 8 - B), (0, 0))), conv_flat], axis=0), "gather_c")
    c_all = first[:, :B].reshape(N_DEV * B, D)
    conv_taps = first[::2, 8:].reshape(N_CHIP, 8 * D)[:, :CONV_WIDTH * conv_cols]
    conv_taps = conv_taps.reshape(N_CHIP, CONV_WIDTH, conv_cols).transpose(1, 0, 2).reshape(CONV_WIDTH, D)
    conv_taps = jnp.pad(conv_taps, ((0, CONV_PAD - CONV_WIDTH), (0, 0)))

    ada_cols = w_ada.shape[2]
    b_cols = lax.dynamic_slice(b_ada, (0, chip * ada_cols), (1, ada_cols))
    mod_part = _ada_fwd(c_all, w_ada[0], b_cols).reshape(N_DEV, B, ada_cols)
    mod = _mod_exchange(mod_part).transpose(1, 0, 2).reshape(B, N_MOD, D)

    core_idx = jnp.reshape(ci, (1,)).astype(jnp.int32)
    chip_idx = jnp.reshape(chip, (1,)).astype(jnp.int32)
    core_chip = jnp.stack([ci, chip]).astype(jnp.int32)
    slots = {n: _cast_slot(big[n], chip_idx, "cast_" + n) for n in BIG_WEIGHTS}

    small = dict(vec_w)
    small["attn_sinks"] = attn_sinks
    small["conv_w_dw"] = conv_taps

    loss_parts, dx, (last_exchange, finish_reduction), dmod, vec_grads, d_sinks, d_conv_w = _local_grads(
        x.reshape(T, D), loss_target.reshape(T, D), mod, slots, small, seq, core_idx, core_chip)

    loss = lax.psum((0.5 / D) * jnp.sum(loss_parts), ("x", "y", "c"))
    grad_x = dx.reshape(B, seq, D)
    out = {}

    block = jnp.zeros((SMALL_ROWS, D), F32)
    block = block.at[ROW_DMOD0:ROW_DMOD0 + N_MOD].set(dmod[0]).at[ROW_DMOD1:ROW_DMOD1 + N_MOD].set(dmod[1])
    block = block.at[ROW_VEC:ROW_VEC + len(VECTORS)].set(jnp.concatenate([vec_grads[n] for n in VECTORS], axis=0))
    block = block.at[ROW_SINK, :2 * HEAD_DIM].set(d_sinks[0])
    block = block.at[ROW_CONVW:ROW_CONVW + CONV_WIDTH].set(d_conv_w[:CONV_WIDTH])
    small_all = _gather8(block, "gather_small_grads")

    def pack_small(prefix):
        rows = [args[prefix + "b_ada"].reshape(N_MOD, D)]
        rows += [args[prefix + n].reshape(1, D) for n in VECTORS]
        rows += [jnp.pad(args[prefix + "attn_sinks"], ((0, 0), (0, D - N_Q_HEADS)))]
        return jnp.pad(jnp.concatenate(rows, axis=0), ((0, 24 - N_MOD - len(VECTORS) - 1), (0, 0)))

    small_sum, sg, sd, sm, sv = _small_adam(small_all, pack_small(""), pack_small("m_"), pack_small("v_"),
                                           ROW_DMOD0, ROW_DMOD1, ROW_VEC)

    def unpack_small(t):
        res = {"b_ada": t[:N_MOD].reshape(1, N_MOD * D)}
        for k, n in enumerate(VECTORS):
            res[n] = t[N_MOD + k].reshape(args[n].shape)
        res["attn_sinks"] = t[N_MOD + len(VECTORS), :N_Q_HEADS].reshape(1, N_Q_HEADS)
        return res

    unpacked = [unpack_small(t) for t in (sg, sd, sm, sv)]
    for n in ("b_ada", "attn_sinks") + VECTORS:
        out[n] = tuple(u[n] for u in unpacked)

    conv_g = lax.dynamic_slice(small_sum, (ROW_CONVW, chip * conv_cols), (CONV_WIDTH, conv_cols))
    d, mn, vn = _adam_call(conv_w_dw[0], conv_g, m_conv_w_dw[0], v_conv_w_dw[0], "adam_conv_w_dw")
    out["conv_w_dw"] = tuple(t[None] for t in (conv_g, d, mn, vn))

    dmod_rows = jnp.stack([small_all[:, ROW_DMOD0:ROW_DMOD0 + N_MOD], small_all[:, ROW_DMOD1:ROW_DMOD1 + N_MOD]], axis=1)
    dmod_all = dmod_rows.reshape(N_DEV * B, N_MOD * D)
    dmod_cols = lax.dynamic_slice(dmod_all, (0, chip * ada_cols), (N_DEV * B, ada_cols))
    ada_out, chips_last = _ada_adam(c_all.T, dmod_cols, w_ada[0], m_w_ada[0], v_w_ada[0], last_exchange)
    out["w_ada"] = tuple(t[None] for t in ada_out)

    reduced = finish_reduction(chips_last)
    for n in BIG_WEIGHTS:
        shape = args[n].shape
        g = reduced[n].reshape(shape[1:])
        d, mn, vn = _adam_call(big[n], g, args["m_" + n][0], args["v_" + n][0], "adam_" + n)
        out[n] = tuple(t.reshape(shape) for t in (g, d, mn, vn))

    order = ("w_ada", "b_ada", "norm_ffn1_g", "ffn1_w_gate", "ffn1_w_up", "ffn1_w_down", "norm_mix_g", "w_in",
             "attn_sinks", "w_attn_o", "conv_w_dw", "conv_b_dw", "conv_ln_g", "conv_ln_b", "w_conv_o", "w_out",
             "norm_ffn2_g", "ffn2_w_gate", "ffn2_w_up", "ffn2_w_down", "final_norm_g")
    return (loss, grad_x, *[out[n][0] for n in order], *[out[n][1] for n in order],
            *[out[n][2] for n in order], *[out[n][3] for n in order])
```

```python
import functools

import jax
import jax.numpy as jnp
from jax import lax
from jax.experimental import pallas as pl
from jax.experimental.pallas import tpu as pltpu

F32 = jnp.float32
BF16 = jnp.bfloat16

D_MODEL = 1024
D_FF = 2816
N_CHIP = 4
N_DEV = 8
FF_SHARD = D_FF // N_CHIP
IN_WIDTH = 5376
IN_SHARD = IN_WIDTH // N_CHIP
HEAD_DIM = 64
N_Q_HEADS = 16
N_KV_HEADS = 2
BLOCK = 128
CONV_WIDTH = 31
CONV_PAD = 32
N_MOD = 9
EPS = 1e-6
FFN_RESIDUAL = 0.5
ATTN_SCALE = HEAD_DIM ** -0.5
MASK_VALUE = -1e30

ADAM_LR = 0.001
ADAM_B1 = 0.9
ADAM_B2 = 0.999
ADAM_EPS = 1e-08
ADAM_WD = 0.01
ADAM_STEP = 10

COLB_Q, COLB_CA, COLB_CB, COLB_GA, COLB_GC = 0, 1, 2, 3, 4
COLB_K, COLB_V = 40, 41
PROJ_TILE = 768

VMEM_LIMIT = 56 * 1024 * 1024
MESH = pl.DeviceIdType.MESH
ANY = pl.BlockSpec(memory_space=pl.ANY)
VMEM_SPEC = pl.BlockSpec(memory_space=pltpu.VMEM)
SMEM_SPEC = pl.BlockSpec(memory_space=pltpu.SMEM)


def _params(n_grid):
    return pltpu.CompilerParams(dimension_semantics=("arbitrary",) * n_grid, vmem_limit_bytes=VMEM_LIMIT)


def _tile(n, pref):
    t = min(n, pref)
    while n % t:
        t //= 2
    return t


def _row_tile(rows, cap):
    for t in range(min(rows, cap) // 16 * 16, 0, -16):
        if rows % t == 0:
            return t
    return rows


def _sigmoid(v):
    return 1.0 / (1.0 + jnp.exp(-v))


def _dot_nn(a, b):
    return lax.dot_general(a, b, (((1,), (0,)), ((), ())), preferred_element_type=F32)


def _dot_nt(a, b):
    return lax.dot_general(a, b, (((1,), (1,)), ((), ())), preferred_element_type=F32)


def _dot_tn(a, b):
    return lax.dot_general(a, b, (((0,), (0,)), ((), ())), preferred_element_type=F32)


def _norm_mod(xv, gn, sc, sh):
    r = lax.rsqrt(jnp.mean(xv * xv, axis=-1, keepdims=True) + EPS)
    return ((xv * r) * gn) * (1.0 + sc) + sh


def _accumulate(ref, first, value):
    @pl.when(first)
    def _():
        ref[...] = value

    @pl.when(jnp.logical_not(first))
    def _():
        ref[...] += value


def _norm_mod_bwd(dh, xv, gn, sc, dxo, first_of_batch, first, dx_ref, dsc_ref, dsh_ref, dgn_ref):
    r = lax.rsqrt(jnp.mean(xv * xv, axis=-1, keepdims=True) + EPS)
    xh = xv * r
    _accumulate(dsh_ref, first_of_batch, jnp.sum(dh, axis=0, keepdims=True))
    _accumulate(dsc_ref, first_of_batch, jnp.sum(dh * (xh * gn), axis=0, keepdims=True))
    dn = dh * (1.0 + sc)
    _accumulate(dgn_ref, first, jnp.sum(dn * xh, axis=0, keepdims=True))
    dxh = dn * gn
    dx_ref[...] = dxo + r * (dxh - xh * jnp.mean(dxh * xh, axis=-1, keepdims=True))


CHIP_FLIPS = ((1, 0), (0, 1), (1, 1))


def _position():
    return lax.axis_index("x"), lax.axis_index("y"), lax.axis_index("c")


def _flip(v, f):
    return 1 - v if f else v


class _GatherComm:
    def __init__(self, bufs):
        n = len(bufs)
        self.n = n
        self.operands = list(bufs)
        self.out_shape = [jax.ShapeDtypeStruct(b.shape, b.dtype) for b in bufs]
        self.aliases = {i: i for i in range(n)}
        self.sems = [pltpu.SemaphoreType.DMA((6 * n,)), pltpu.SemaphoreType.DMA((6 * n,))]
        self.rows = [b.shape[1] // 2 for b in bufs]

    def _half(self, ref, i, which):
        return ref.at[pl.ds(which * self.rows[i], self.rows[i]), :]

    def _ici(self, cins, couts, sems, i, k, dst_chip, to):
        x, y, c = _position()
        return pltpu.make_async_remote_copy(
            src_ref=self._half(cins[i].at[2 * x + y], i, c), dst_ref=self._half(couts[i].at[dst_chip], i, c),
            send_sem=sems[0].at[3 * i + k], recv_sem=sems[1].at[3 * i + k], device_id=to, device_id_type=MESH)

    def _d2d(self, couts, sems, i, k, src_chip, which):
        x, y, c = _position()
        place = self._half(couts[i].at[src_chip], i, which)
        return pltpu.make_async_remote_copy(
            src_ref=place, dst_ref=place, send_sem=sems[0].at[3 * self.n + 3 * i + k],
            recv_sem=sems[1].at[3 * self.n + 3 * i + k], device_id=(x, y, 1 - c), device_id_type=MESH)

    def _peers(self):
        x, y, _ = _position()
        return [(_flip(x, fx), _flip(y, fy)) for fx, fy in CHIP_FLIPS]

    def start(self, cins, couts, sems):
        x, y, c = _position()
        for i in range(self.n):
            for k, (px, py) in enumerate(self._peers()):
                self._ici(cins, couts, sems, i, k, 2 * x + y, (px, py, c)).start()

    def finish(self, cins, couts, sems):
        _, _, c = _position()
        peers = self._peers()
        for i in range(self.n):
            for k, (px, py) in enumerate(peers):
                self._ici(cins, couts, sems, i, k, 2 * px + py, (px, py, c)).wait_recv()
                self._d2d(couts, sems, i, k, 2 * px + py, c).start()
        for i in range(self.n):
            for k, (px, py) in enumerate(peers):
                self._d2d(couts, sems, i, k, 2 * px + py, 1 - c).wait_recv()
        for i in range(self.n):
            for k, (px, py) in enumerate(peers):
                self._ici(cins, couts, sems, i, k, 2 * px + py, (px, py, c)).wait_send()
                self._d2d(couts, sems, i, k, 2 * px + py, c).wait_send()


class _ExchangeComm:
    def __init__(self, pairs):
        n = len(pairs)
        self.n = n
        self.operands = list(pairs)
        self.out_shape = [jax.ShapeDtypeStruct((3,) + p.shape[1:], p.dtype) for p in pairs]
        self.aliases = {}
        self.sems = [pltpu.SemaphoreType.DMA((3 * n,)), pltpu.SemaphoreType.DMA((3 * n,))]

    def _copies(self, cins, couts, sems):
        x, y, c = _position()
        peers = [(_flip(x, fx), _flip(y, fy)) for fx, fy in CHIP_FLIPS]
        return [pltpu.make_async_remote_copy(
            src_ref=cins[i].at[2 * px + py], dst_ref=couts[i].at[k], send_sem=sems[0].at[3 * i + k],
            recv_sem=sems[1].at[3 * i + k], device_id=(px, py, c), device_id_type=MESH)
            for i in range(self.n) for k, (px, py) in enumerate(peers)]

    def start(self, cins, couts, sems):
        for cp in self._copies(cins, couts, sems):
            cp.start()

    def finish(self, cins, couts, sems):
        for cp in self._copies(cins, couts, sems):
            cp.wait()


def _call(body, *, name, grid, in_specs, out_specs, out_shape, operands, scratch_shapes=(), comm=None):
    n_grid = len(grid)
    if comm is None:
        return pl.pallas_call(
            body, name=name, grid=grid, in_specs=list(in_specs), out_specs=list(out_specs), out_shape=list(out_shape),
            scratch_shapes=list(scratch_shapes), compiler_params=_params(n_grid))(*operands), ()
    counts = (len(in_specs), len(comm.operands), len(out_specs), len(comm.out_shape), len(scratch_shapes),
              len(comm.sems))

    def fused(*refs):
        parts, pos = [], 0
        for k in counts:
            parts.append(refs[pos:pos + k])
            pos += k
        ins, cins, outs, couts, scr, sems = parts
        first = functools.reduce(jnp.logical_and, [pl.program_id(d) == 0 for d in range(n_grid)])
        last = functools.reduce(jnp.logical_and, [pl.program_id(d) == grid[d] - 1 for d in range(n_grid)])

        @pl.when(first)
        def _():
            comm.start(cins, couts, sems)

        body(*ins, *outs, *scr)

        @pl.when(last)
        def _():
            comm.finish(cins, couts, sems)

    res = pl.pallas_call(
        fused, name=name, grid=grid, in_specs=list(in_specs) + [ANY] * counts[1],
        out_specs=list(out_specs) + [ANY] * counts[3], out_shape=list(out_shape) + list(comm.out_shape),
        scratch_shapes=list(scratch_shapes) + list(comm.sems),
        input_output_aliases={counts[0] + i: counts[2] + j for i, j in comm.aliases.items()},
        compiler_params=_params(n_grid))(*operands, *comm.operands)
    return res[:counts[2]], res[counts[2]:]


def _run_comm(comm, name):
    k_in, k_out = len(comm.operands), len(comm.out_shape)

    def body(*refs):
        cins, couts, sems = refs[:k_in], refs[k_in:k_in + k_out], refs[k_in + k_out:]
        comm.start(cins, couts, sems)
        comm.finish(cins, couts, sems)

    return pl.pallas_call(
        body, name=name, in_specs=[ANY] * k_in, out_specs=[ANY] * k_out, out_shape=list(comm.out_shape),
        scratch_shapes=list(comm.sems), input_output_aliases=dict(comm.aliases))(*comm.operands)


def _ffn_fwd(x, gn, sc, sh, gate, wg, wu, wd, seq, name, comm=None):
    T, D = x.shape
    J, Fs, _ = wg.shape
    tm = _tile(seq, 1024)
    nb = seq // tm

    def body(x_ref, gn_ref, sc_ref, sh_ref, gate_ref, wg_ref, wu_ref, wd_ref,
             h_ref, a_ref, u_ref, f_ref, xo_ref, hs, acc):
        j = pl.program_id(1)

        @pl.when(j == 0)
        def _():
            hb = _norm_mod(x_ref[...], gn_ref[...], sc_ref[...], sh_ref[...]).astype(BF16)
            hs[...] = hb
            h_ref[...] = hb
            acc[...] = jnp.zeros_like(acc)

        hb = hs[...]
        a = _dot_nt(hb, wg_ref[...])
        u = _dot_nt(hb, wu_ref[...])
        a_ref[...] = a.astype(BF16)
        u_ref[...] = u.astype(BF16)
        s = ((a * _sigmoid(a)) * u).astype(BF16)
        acc[...] += _dot_nn(s, wd_ref[...])

        @pl.when(j == J - 1)
        def _():
            f = acc[...]
            f_ref[...] = f.astype(BF16)
            xo_ref[...] = x_ref[...] + (FFN_RESIDUAL * gate_ref[...]) * f

    row = pl.BlockSpec((tm, D), lambda i, j: (i, 0))
    vec = pl.BlockSpec((1, D), lambda i, j: (0, 0))
    per_b = pl.BlockSpec((None, 1, D), lambda i, j: (i // nb, 0, 0))
    hid = pl.BlockSpec((None, tm, Fs), lambda i, j: (j, i, 0))
    return _call(
        body, name=name, grid=(T // tm, J),
        in_specs=[row, vec, per_b, per_b, per_b] + [pl.BlockSpec((None, Fs, D), lambda i, j: (j, 0, 0))] * 3,
        out_specs=[row, hid, hid, row, row],
        out_shape=[jax.ShapeDtypeStruct((T, D), BF16), jax.ShapeDtypeStruct((J, T, Fs), BF16),
                   jax.ShapeDtypeStruct((J, T, Fs), BF16), jax.ShapeDtypeStruct((T, D), BF16),
                   jax.ShapeDtypeStruct((T, D), F32)],
        scratch_shapes=[pltpu.VMEM((tm, D), BF16), pltpu.VMEM((tm, D), F32)],
        operands=(x, gn, sc, sh, gate, wg, wu, wd), comm=comm)


def _ffn_bwd(dxo, x, f, a, u, gn, sc, gate, wg, wu, wd, seq, name, comm=None):
    T, D = x.shape
    J, Fs, _ = wg.shape
    B = T // seq
    tm = _tile(seq, 512)
    nb = seq // tm

    def body(dxo_ref, x_ref, f_ref, a_ref, u_ref, gn_ref, sc_ref, gate_ref, wg_ref, wu_ref, wd_ref,
             da_ref, du_ref, s_ref, df_ref, dx_ref, dgate_ref, dsc_ref, dsh_ref, dgn_ref, dfs, acc):
        i = pl.program_id(0)
        j = pl.program_id(1)
        first_of_batch = i % nb == 0

        @pl.when(j == 0)
        def _():
            dxo_v = dxo_ref[...]
            dfb = ((FFN_RESIDUAL * gate_ref[...]) * dxo_v).astype(BF16)
            dfs[...] = dfb
            df_ref[...] = dfb
            part = jnp.sum((FFN_RESIDUAL * f_ref[...].astype(F32)) * dxo_v, axis=0, keepdims=True)
            _accumulate(dgate_ref, first_of_batch, part)
            acc[...] = jnp.zeros_like(acc)

        ds = _dot_nt(dfs[...], wd_ref[...])
        av = a_ref[...].astype(F32)
        uv = u_ref[...].astype(F32)
        sig = _sigmoid(av)
        sil = av * sig
        s_ref[...] = (sil * uv).astype(BF16)
        dab = (ds * uv * (sig * (1.0 + av * (1.0 - sig)))).astype(BF16)
        dub = (ds * sil).astype(BF16)
        da_ref[...] = dab
        du_ref[...] = dub
        acc[...] += _dot_nn(dab, wg_ref[...]) + _dot_nn(dub, wu_ref[...])

        @pl.when(j == J - 1)
        def _():
            _norm_mod_bwd(acc[...], x_ref[...], gn_ref[...], sc_ref[...], dxo_ref[...],
                          first_of_batch, i == 0, dx_ref, dsc_ref, dsh_ref, dgn_ref)

    row = pl.BlockSpec((tm, D), lambda i, j: (i, 0))
    vec = pl.BlockSpec((1, D), lambda i, j: (0, 0))
    per_b = pl.BlockSpec((None, 1, D), lambda i, j: (i // nb, 0, 0))
    hid = pl.BlockSpec((None, tm, Fs), lambda i, j: (j, i, 0))
    hid_shape = jax.ShapeDtypeStruct((J, T, Fs), BF16)
    per_b_shape = jax.ShapeDtypeStruct((B, 1, D), F32)
    return _call(
        body, name=name, grid=(T // tm, J),
        in_specs=[row, row, row, hid, hid, vec, per_b, per_b]
        + [pl.BlockSpec((None, Fs, D), lambda i, j: (j, 0, 0))] * 3,
        out_specs=[hid, hid, hid, row, row, per_b, per_b, per_b, vec],
        out_shape=[hid_shape, hid_shape, hid_shape, jax.ShapeDtypeStruct((T, D), BF16),
                   jax.ShapeDtypeStruct((T, D), F32), per_b_shape, per_b_shape, per_b_shape,
                   jax.ShapeDtypeStruct((1, D), F32)],
        scratch_shapes=[pltpu.VMEM((tm, D), BF16), pltpu.VMEM((tm, D), F32)],
        operands=(dxo, x, f, a, u, gn, sc, gate, wg, wu, wd), comm=comm)


def _wgrad(a, a_spec, b, b_spec, rows, cols, n_tok, name, comm=None):
    tk = _tile(n_tok, 1024)
    nk = n_tok // tk
    half = rows // 2

    def body(a_ref, b_ref, o32_ref, o16_ref, acc):
        k = pl.program_id(1)

        @pl.when(k == 0)
        def _():
            acc[...] = jnp.zeros_like(acc)

        acc[...] += _dot_tn(a_ref[...], b_ref[...])

        @pl.when(k == nk - 1)
        def _():
            for h in range(2):
                v = acc[h * half:(h + 1) * half, :]
                o32_ref[h] = v
                o16_ref[h] = v.astype(BF16)

    out_spec = pl.BlockSpec((2, None, half, cols), lambda j, k: (0, j, 0, 0))
    outs, comm_outs = _call(
        body, name=name, grid=(N_CHIP, nk),
        in_specs=[a_spec(tk), b_spec(tk)],
        out_specs=[out_spec, out_spec],
        out_shape=[jax.ShapeDtypeStruct((2, N_CHIP, half, cols), F32),
                   jax.ShapeDtypeStruct((2, N_CHIP, half, cols), BF16)],
        scratch_shapes=[pltpu.VMEM((rows, cols), F32)],
        operands=(a, b), comm=comm)
    return outs if comm is None else (outs, comm_outs)


def _spec_rows(width):
    return lambda tk: pl.BlockSpec((tk, width), lambda j, k: (k, 0))


def _spec_chip_major(width):
    return lambda tk: pl.BlockSpec((None, tk, width), lambda j, k: (j, k, 0))


def _spec_col_block(width):
    return lambda tk: pl.BlockSpec((tk, width), lambda j, k: (k, j))


def _in_proj(x, gn, sc, sh, w_in, seq, comm=None):
    T, D = x.shape
    N = w_in.shape[0]
    tm = _tile(seq, 1024)
    nb = seq // tm

    def body(x_ref, gn_ref, sc_ref, sh_ref, w_ref, h_ref, p_ref, hs):
        @pl.when(pl.program_id(1) == 0)
        def _():
            hb = _norm_mod(x_ref[...], gn_ref[...], sc_ref[...], sh_ref[...]).astype(BF16)
            hs[...] = hb
            h_ref[...] = hb

        p_ref[...] = _dot_nt(hs[...], w_ref[...]).astype(BF16)

    row = pl.BlockSpec((tm, D), lambda i, j: (i, 0))
    per_b = pl.BlockSpec((None, 1, D), lambda i, j: (i // nb, 0, 0))
    return _call(
        body, name="mix_in_proj", grid=(T // tm, N // PROJ_TILE),
        in_specs=[row, pl.BlockSpec((1, D), lambda i, j: (0, 0)), per_b, per_b,
                  pl.BlockSpec((PROJ_TILE, D), lambda i, j: (j, 0))],
        out_specs=[row, pl.BlockSpec((tm, PROJ_TILE), lambda i, j: (i, j))],
        out_shape=[jax.ShapeDtypeStruct((T, D), BF16), jax.ShapeDtypeStruct((T, N), BF16)],
        scratch_shapes=[pltpu.VMEM((tm, D), BF16)],
        operands=(x, gn, sc, sh, w_in), comm=comm)


def _attn_specs(nblk):
    def own(col):
        return lambda b, n: (b * nblk + n, col)

    def prev(col):
        return lambda b, n: (b * nblk + jnp.maximum(n - 1, 0), col)

    kv = (BLOCK, 2 * HEAD_DIM)
    return [pl.BlockSpec((BLOCK, D_MODEL), own(COLB_Q)),
            pl.BlockSpec(kv, prev(COLB_K)), pl.BlockSpec(kv, own(COLB_K)),
            pl.BlockSpec(kv, prev(COLB_V)), pl.BlockSpec(kv, own(COLB_V))]


def _band_operands(prev_ref, own_ref, lo):
    band = jnp.concatenate([prev_ref[...], own_ref[...]], axis=0).astype(F32)
    rolled = pltpu.roll(band, HEAD_DIM, 1)
    zero = jnp.zeros_like(band)
    head0 = (jnp.where(lo, band, zero).astype(BF16), jnp.where(lo, zero, rolled).astype(BF16))
    head1 = (jnp.where(lo, rolled, zero).astype(BF16), jnp.where(lo, zero, band).astype(BF16))
    return head0, head1


def _band_valid(has_prev):
    qi = lax.broadcasted_iota(jnp.int32, (BLOCK, 2 * BLOCK), 0)
    sj = lax.broadcasted_iota(jnp.int32, (BLOCK, 2 * BLOCK), 1)
    rel = qi + BLOCK - sj
    return (rel >= 0) & (rel < BLOCK) & ((sj >= BLOCK) | has_prev)


def _attn_fwd(proj, sinks, batch, seq, comm=None):
    T = proj.shape[0]
    nblk = seq // BLOCK

    def body(sink_ref, q_ref, kp_ref, ko_ref, vp_ref, vo_ref, o_ref, lse_ref):
        lo = lax.broadcasted_iota(jnp.int32, (1, 2 * HEAD_DIM), 1) < HEAD_DIM
        head_lane = lax.broadcasted_iota(jnp.int32, (1, N_Q_HEADS), 1)
        valid = _band_valid(pl.program_id(1) > 0)
        k_ops = _band_operands(kp_ref, ko_ref, lo)
        v_ops = _band_operands(vp_ref, vo_ref, lo)
        lse_all = jnp.zeros((BLOCK, N_Q_HEADS), F32)
        for pair in range(N_Q_HEADS // 2):
            kvh = pair // (N_Q_HEADS // 2 // N_KV_HEADS)
            q2 = q_ref[:, pair * 2 * HEAD_DIM:(pair + 1) * 2 * HEAD_DIM]
            out = jnp.zeros((BLOCK, 2 * HEAD_DIM), F32)
            for side in range(2):
                head = 2 * pair + side
                sink = sink_ref[0, head]
                s = jnp.where(valid, _dot_nt(q2, k_ops[kvh][side]) * ATTN_SCALE, MASK_VALUE)
                m = jnp.maximum(jnp.max(s, axis=-1, keepdims=True), sink)
                p = jnp.where(valid, jnp.exp(s - m), 0.0)
                den = jnp.sum(p, axis=-1, keepdims=True) + jnp.exp(sink - m)
                out = out + _dot_nn((p / den).astype(BF16), v_ops[kvh][side])
                lse_all = jnp.where(head_lane == head, m + jnp.log(den), lse_all)
            o_ref[:, pair * 2 * HEAD_DIM:(pair + 1) * 2 * HEAD_DIM] = out.astype(BF16)
        lse_ref[...] = lse_all

    return _call(
        body, name="attn_fwd", grid=(batch, nblk),
        in_specs=[SMEM_SPEC] + _attn_specs(nblk),
        out_specs=[pl.BlockSpec((BLOCK, D_MODEL), lambda b, n: (b * nblk + n, 0)),
                   pl.BlockSpec((BLOCK, N_Q_HEADS), lambda b, n: (b * nblk + n, 0))],
        out_shape=[jax.ShapeDtypeStruct((T, D_MODEL), BF16), jax.ShapeDtypeStruct((T, N_Q_HEADS), F32)],
        operands=(sinks, proj, proj, proj, proj, proj), comm=comm)


def _conv_u(ca, cb):
    return ca.astype(F32) * _sigmoid(cb.astype(F32))


def _conv_specs(ts, tiles_per_seq):
    per_tile = ts // CONV_PAD

    def tile(col):
        return lambda b, t: (b * tiles_per_seq + t, col)

    def before(col):
        return lambda b, t: (jnp.maximum((b * tiles_per_seq + t) * per_tile - 1, 0), col)

    return [pl.BlockSpec((ts, D_MODEL), tile(COLB_CA)), pl.BlockSpec((ts, D_MODEL), tile(COLB_CB)),
            pl.BlockSpec((CONV_PAD, D_MODEL), before(COLB_CA)), pl.BlockSpec((CONV_PAD, D_MODEL), before(COLB_CB))]


SUBLANES = 8


def _fill_upad(upad, ca_ref, cb_ref, cah_ref, cbh_ref, t):
    halo = _conv_u(cah_ref[...], cbh_ref[...])
    upad[0, 0:CONV_PAD, :] = jnp.where(t > 0, halo, jnp.zeros_like(halo))
    upad[0, CONV_PAD:, :] = _conv_u(ca_ref[...], cb_ref[...])


def _fill_shifted(pad):
    rows = pad.shape[1] - SUBLANES
    for b in range(1, SUBLANES):
        pad[b, 0:rows, :] = pad[0, b:b + rows, :]


def _shifted_rows(pad, offset, rows):
    b = offset % SUBLANES
    return pad[b, offset - b:offset - b + rows, :]


def _layernorm_stats(y):
    mu = jnp.mean(y, axis=-1, keepdims=True)
    yc = y - mu
    rstd = lax.rsqrt(jnp.mean(yc * yc, axis=-1, keepdims=True) + EPS)
    return yc * rstd, rstd


def _conv_fwd(proj, w_dw, b_dw, ln_g, ln_b, batch, seq, comm=None):
    T = proj.shape[0]
    ts = _tile(seq, 256)
    nt = seq // ts
    shift = CONV_PAD - (CONV_WIDTH - 1)

    def body(ca_ref, cb_ref, cah_ref, cbh_ref, w_ref, b_ref, g_ref, beta_ref, y_ref, z_ref, upad):
        _fill_upad(upad, ca_ref, cb_ref, cah_ref, cbh_ref, pl.program_id(1))
        _fill_shifted(upad)
        y = jnp.zeros((ts, D_MODEL), F32) + b_ref[...]
        for k in range(CONV_WIDTH):
            y = y + w_ref[k:k + 1, :] * _shifted_rows(upad, shift + k, ts)
        y_ref[...] = y
        lnh, _ = _layernorm_stats(y)
        ln = lnh * g_ref[...] + beta_ref[...]
        z_ref[...] = (ln * _sigmoid(ln)).astype(BF16)

    vec = pl.BlockSpec((1, D_MODEL), lambda b, t: (0, 0))
    row = pl.BlockSpec((ts, D_MODEL), lambda b, t: (b * nt + t, 0))
    return _call(
        body, name="conv_fwd", grid=(batch, nt),
        in_specs=_conv_specs(ts, nt) + [pl.BlockSpec((CONV_PAD, D_MODEL), lambda b, t: (0, 0)), vec, vec, vec],
        out_specs=[row, row],
        out_shape=[jax.ShapeDtypeStruct((T, D_MODEL), F32), jax.ShapeDtypeStruct((T, D_MODEL), BF16)],
        scratch_shapes=[pltpu.VMEM((SUBLANES, ts + CONV_PAD, D_MODEL), F32)],
        operands=(proj, proj, proj, proj, w_dw, b_dw, ln_g, ln_b), comm=comm)


def _merge(o, z, proj, w_ao, w_co, w_out, x, gate, seq):
    T, D = x.shape
    tm = _tile(seq, 512)
    nb = seq // tm

    def body(o_ref, z_ref, ga_ref, gc_ref, wao_ref, wco_ref, wout_ref, x_ref, gate_ref,
             ya_ref, yc_ref, mg_ref, mo_ref, xo_ref):
        ya = _dot_nn(o_ref[...], wao_ref[...])
        yc = _dot_nn(z_ref[...], wco_ref[...])
        ya_ref[...] = ya.astype(BF16)
        yc_ref[...] = yc.astype(BF16)
        merged = (_sigmoid(ga_ref[...].astype(F32)) * ya + _sigmoid(gc_ref[...].astype(F32)) * yc).astype(BF16)
        mg_ref[...] = merged
        mo = _dot_nn(merged, wout_ref[...])
        mo_ref[...] = mo.astype(BF16)
        xo_ref[...] = x_ref[...] + gate_ref[...] * mo

    row = pl.BlockSpec((tm, D), lambda i: (i, 0))
    mat = pl.BlockSpec((D, D), lambda i: (0, 0))
    act = jax.ShapeDtypeStruct((T, D), BF16)
    return pl.pallas_call(
        body, name="mix_merge", grid=(T // tm,),
        in_specs=[row, row, pl.BlockSpec((tm, D), lambda i: (i, COLB_GA)), pl.BlockSpec((tm, D), lambda i: (i, COLB_GC)),
                  mat, mat, mat, row, pl.BlockSpec((None, 1, D), lambda i: (i // nb, 0, 0))],
        out_specs=[row, row, row, row, row],
        out_shape=[act, act, act, act, jax.ShapeDtypeStruct((T, D), F32)],
        compiler_params=_params(1),
    )(o, z, proj, proj, w_ao, w_co, w_out, x, gate)


def _final_loss(x, gf, target):
    T, D = x.shape
    tm = _tile(T, 512)

    def body(x_ref, gf_ref, t_ref, dx_ref, lp_ref, dgf_ref):
        first = pl.program_id(0) == 0
        xv = x_ref[...]
        gfv = gf_ref[...]
        r = lax.rsqrt(jnp.mean(xv * xv, axis=-1, keepdims=True) + EPS)
        xh = xv * r
        err = xh * gfv - t_ref[...]
        _accumulate(lp_ref, first, jnp.sum(err * err, axis=0, keepdims=True))
        dy = err * (1.0 / D)
        _accumulate(dgf_ref, first, jnp.sum(dy * xh, axis=0, keepdims=True))
        dxh = dy * gfv
        dx_ref[...] = r * (dxh - xh * jnp.mean(dxh * xh, axis=-1, keepdims=True))

    row = pl.BlockSpec((tm, D), lambda i: (i, 0))
    vec = pl.BlockSpec((1, D), lambda i: (0, 0))
    return pl.pallas_call(
        body, name="final_loss", grid=(T // tm,),
        in_specs=[row, vec, row], out_specs=[row, vec, vec],
        out_shape=[jax.ShapeDtypeStruct((T, D), F32), jax.ShapeDtypeStruct((1, D), F32),
                   jax.ShapeDtypeStruct((1, D), F32)],
        compiler_params=_params(1),
    )(x, gf, target)


def _merge_bwd(dxo, mo, gate, proj, ya, yc, w_out, w_ao, w_co, seq):
    T, D = dxo.shape
    B = T // seq
    tm = _tile(seq, 512)
    nb = seq // tm

    def body(dxo_ref, mo_ref, gate_ref, ga_ref, gc_ref, ya_ref, yc_ref, wout_ref, wao_ref, wco_ref,
             dmo_ref, dya_ref, dyc_ref, dga_ref, dgc_ref, do_ref, dz_ref, dgate_ref):
        dxo_v = dxo_ref[...]
        dmo = (gate_ref[...] * dxo_v).astype(BF16)
        dmo_ref[...] = dmo
        _accumulate(dgate_ref, pl.program_id(0) % nb == 0,
                    jnp.sum(mo_ref[...].astype(F32) * dxo_v, axis=0, keepdims=True))
        dm = _dot_nt(dmo, wout_ref[...])
        sa = _sigmoid(ga_ref[...].astype(F32))
        sc = _sigmoid(gc_ref[...].astype(F32))
        dya = (sa * dm).astype(BF16)
        dyc = (sc * dm).astype(BF16)
        dya_ref[...] = dya
        dyc_ref[...] = dyc
        dga_ref[...] = (dm * ya_ref[...].astype(F32) * (sa * (1.0 - sa))).astype(BF16)
        dgc_ref[...] = (dm * yc_ref[...].astype(F32) * (sc * (1.0 - sc))).astype(BF16)
        do_ref[...] = _dot_nt(dya, wao_ref[...]).astype(BF16)
        dz_ref[...] = _dot_nt(dyc, wco_ref[...]).astype(BF16)

    row = pl.BlockSpec((tm, D), lambda i: (i, 0))
    mat = pl.BlockSpec((D, D), lambda i: (0, 0))
    per_b = pl.BlockSpec((None, 1, D), lambda i: (i // nb, 0, 0))
    act = jax.ShapeDtypeStruct((T, D), BF16)
    return pl.pallas_call(
        body, name="mix_merge_bwd", grid=(T // tm,),
        in_specs=[row, row, per_b, pl.BlockSpec((tm, D), lambda i: (i, COLB_GA)),
                  pl.BlockSpec((tm, D), lambda i: (i, COLB_GC)), row, row, mat, mat, mat],
        out_specs=[row] * 7 + [per_b],
        out_shape=[act] * 7 + [jax.ShapeDtypeStruct((B, 1, D), F32)],
        compiler_params=_params(1),
    )(dxo, mo, gate, proj, proj, ya, yc, w_out, w_ao, w_co)


def _attn_bwd(proj, sinks, o, do, lse, batch, seq, comm=None):
    T = proj.shape[0]
    nblk = seq // BLOCK
    n_steps = batch * nblk
    pairs_per_kv = N_Q_HEADS // 2 // N_KV_HEADS

    def body(sink_ref, q_ref, kp_ref, ko_ref, vp_ref, vo_ref, o_ref, do_ref, lse_ref,
             dq_ref, dkp_ref, dko_ref, dvp_ref, dvo_ref, dsink_ref):
        lo = lax.broadcasted_iota(jnp.int32, (1, 2 * HEAD_DIM), 1) < HEAD_DIM
        sink_lane = lax.broadcasted_iota(jnp.int32, (1, 2 * HEAD_DIM), 1)
        valid = _band_valid(pl.program_id(1) > 0)
        k_ops = _band_operands(kp_ref, ko_ref, lo)
        v_ops = _band_operands(vp_ref, vo_ref, lo)
        dsink = jnp.zeros((1, 2 * HEAD_DIM), F32)
        dk_heads, dv_heads = [], []
        for kvh in range(N_KV_HEADS):
            dk_acc = jnp.zeros((2 * BLOCK, 2 * HEAD_DIM), F32)
            dv_acc = jnp.zeros((2 * BLOCK, 2 * HEAD_DIM), F32)
            for pp in range(pairs_per_kv):
                pair = kvh * pairs_per_kv + pp
                lanes = slice(pair * 2 * HEAD_DIM, (pair + 1) * 2 * HEAD_DIM)
                q2 = q_ref[:, lanes]
                do2 = do_ref[:, lanes]
                dd = do2.astype(F32) * o_ref[:, lanes].astype(F32)
                dq2 = jnp.zeros((BLOCK, 2 * HEAD_DIM), F32)
                for side in range(2):
                    head = 2 * pair + side
                    mine = lo if side == 0 else jnp.logical_not(lo)
                    sink = sink_ref[0, head]
                    lse_h = lse_ref[:, head:head + 1]
                    delta = jnp.sum(jnp.where(mine, dd, 0.0), axis=-1, keepdims=True)
                    s = _dot_nt(q2, k_ops[kvh][side]) * ATTN_SCALE
                    p = jnp.where(valid, jnp.exp(jnp.where(valid, s, MASK_VALUE) - lse_h), 0.0)
                    dp = _dot_nt(do2, v_ops[kvh][side])
                    ds = (p * (dp - delta) * ATTN_SCALE).astype(BF16)
                    dq2 = dq2 + _dot_nn(ds, k_ops[kvh][side])
                    dk_acc = dk_acc + jnp.where(mine, _dot_tn(ds, q2), 0.0)
                    dv_acc = dv_acc + jnp.where(mine, _dot_tn(p.astype(BF16), do2), 0.0)
                    dsink = dsink + jnp.where(sink_lane == head, -jnp.sum(jnp.exp(sink - lse_h) * delta), 0.0)
                dq_ref[:, lanes] = dq2.astype(BF16)
            dk_heads.append(dk_acc + pltpu.roll(dk_acc, HEAD_DIM, 1))
            dv_heads.append(dv_acc + pltpu.roll(dv_acc, HEAD_DIM, 1))
        dk = jnp.where(lo, dk_heads[0], dk_heads[1])
        dv = jnp.where(lo, dv_heads[0], dv_heads[1])
        dkp_ref[...] = dk[:BLOCK]
        dko_ref[...] = dk[BLOCK:]
        dvp_ref[...] = dv[:BLOCK]
        dvo_ref[...] = dv[BLOCK:]
        dsink_ref[...] = dsink

    def own(b, n):
        return (b * nblk + n, 0)

    row = pl.BlockSpec((BLOCK, D_MODEL), own)
    kv = pl.BlockSpec((BLOCK, 2 * HEAD_DIM), own)
    kv_shape = jax.ShapeDtypeStruct((T, 2 * HEAD_DIM), F32)
    return _call(
        body, name="attn_bwd", grid=(batch, nblk),
        in_specs=[SMEM_SPEC] + _attn_specs(nblk) + [row, row, pl.BlockSpec((BLOCK, N_Q_HEADS), own)],
        out_specs=[row, kv, kv, kv, kv, pl.BlockSpec((None, 1, 2 * HEAD_DIM), lambda b, n: (b * nblk + n, 0, 0))],
        out_shape=[jax.ShapeDtypeStruct((T, D_MODEL), BF16), kv_shape, kv_shape, kv_shape, kv_shape,
                   jax.ShapeDtypeStruct((n_steps, 1, 2 * HEAD_DIM), F32)],
        operands=(sinks, proj, proj, proj, proj, proj, o, do, lse), comm=comm)


def _conv_bwd(proj, dz, ydw, w_dw, ln_g, ln_b, batch, seq):
    T = proj.shape[0]
    ts = _tile(seq, 256)
    nt = seq // ts
    per_tile = ts // CONV_PAD
    shift = CONV_PAD - (CONV_WIDTH - 1)

    def body(ca_ref, cb_ref, cah_ref, cbh_ref, dz_ref, dzn_ref, y_ref, yn_ref, w_ref, g_ref, beta_ref,
             dca_ref, dcb_ref, dw_ref, db_ref, dg_ref, dbeta_ref, upad, dypad):
        t = pl.program_id(1)
        first = (pl.program_id(0) == 0) & (t == 0)
        gv = g_ref[...]

        def ln_bwd(dzv, yv):
            lnh, rstd = _layernorm_stats(yv)
            ln = lnh * gv + beta_ref[...]
            sg = _sigmoid(ln)
            dln = dzv.astype(F32) * (sg * (1.0 + ln * (1.0 - sg)))
            dyh = dln * gv
            dy = rstd * (dyh - jnp.mean(dyh, axis=-1, keepdims=True)
                         - lnh * jnp.mean(dyh * lnh, axis=-1, keepdims=True))
            return dy, dln, lnh

        dy, dln, lnh = ln_bwd(dz_ref[...], y_ref[...])
        dy_next, _, _ = ln_bwd(dzn_ref[...], yn_ref[...])
        dypad[0, 0:ts, :] = dy
        dypad[0, ts:, :] = jnp.where(t < nt - 1, dy_next, jnp.zeros_like(dy_next))
        _fill_shifted(dypad)
        _fill_upad(upad, ca_ref, cb_ref, cah_ref, cbh_ref, t)
        _fill_shifted(upad)

        _accumulate(dg_ref, first, jnp.sum(dln * lnh, axis=0, keepdims=True))
        _accumulate(dbeta_ref, first, jnp.sum(dln, axis=0, keepdims=True))
        _accumulate(db_ref, first, jnp.sum(dy, axis=0, keepdims=True))

        @pl.when(first)
        def _():
            dw_ref[...] = jnp.zeros_like(dw_ref)

        du = jnp.zeros((ts, D_MODEL), F32)
        for k in range(CONV_WIDTH):
            du = du + w_ref[k:k + 1, :] * _shifted_rows(dypad, CONV_WIDTH - 1 - k, ts)
            dw_ref[k:k + 1, :] += jnp.sum(dy * _shifted_rows(upad, shift + k, ts), axis=0, keepdims=True)
        cav = ca_ref[...].astype(F32)
        sb = _sigmoid(cb_ref[...].astype(F32))
        dca_ref[...] = (du * sb).astype(BF16)
        dcb_ref[...] = (du * cav * (sb * (1.0 - sb))).astype(BF16)

    def tile(b, t):
        return (b * nt + t, 0)

    def after(b, t):
        return (jnp.minimum((b * nt + t + 1) * per_tile, T // CONV_PAD - 1), 0)

    row = pl.BlockSpec((ts, D_MODEL), tile)
    halo = pl.BlockSpec((CONV_PAD, D_MODEL), after)
    vec = pl.BlockSpec((1, D_MODEL), lambda b, t: (0, 0))
    wspec = pl.BlockSpec((CONV_PAD, D_MODEL), lambda b, t: (0, 0))
    act = jax.ShapeDtypeStruct((T, D_MODEL), BF16)
    vec_shape = jax.ShapeDtypeStruct((1, D_MODEL), F32)
    return pl.pallas_call(
        body, name="conv_bwd", grid=(batch, nt),
        in_specs=_conv_specs(ts, nt) + [row, halo, row, halo, wspec, vec, vec],
        out_specs=[row, row, wspec, vec, vec, vec],
        out_shape=[act, act, jax.ShapeDtypeStruct((CONV_PAD, D_MODEL), F32), vec_shape, vec_shape, vec_shape],
        scratch_shapes=[pltpu.VMEM((SUBLANES, ts + CONV_PAD, D_MODEL), F32)] * 2,
        compiler_params=_params(2),
    )(proj, proj, proj, proj, dz, dz, ydw, ydw, w_dw, ln_g, ln_b)


def _in_proj_bwd(dproj, w_in_g, x, gn, sc, dxo, seq):
    T, D = x.shape
    J, W, _ = w_in_g.shape
    B = T // seq
    tm = _tile(seq, 512)
    nb = seq // tm

    def body(dp_ref, w_ref, x_ref, gn_ref, sc_ref, dxo_ref, dx_ref, dsc_ref, dsh_ref, dgn_ref, acc):
        i = pl.program_id(0)
        j = pl.program_id(1)

        @pl.when(j == 0)
        def _():
            acc[...] = jnp.zeros_like(acc)

        acc[...] += _dot_nn(dp_ref[...], w_ref[...])

        @pl.when(j == J - 1)
        def _():
            _norm_mod_bwd(acc[...], x_ref[...], gn_ref[...], sc_ref[...], dxo_ref[...],
                          i % nb == 0, i == 0, dx_ref, dsc_ref, dsh_ref, dgn_ref)

    row = pl.BlockSpec((tm, D), lambda i, j: (i, 0))
    vec = pl.BlockSpec((1, D), lambda i, j: (0, 0))
    per_b = pl.BlockSpec((None, 1, D), lambda i, j: (i // nb, 0, 0))
    per_b_shape = jax.ShapeDtypeStruct((B, 1, D), F32)
    return pl.pallas_call(
        body, name="mix_in_proj_bwd", grid=(T // tm, J),
        in_specs=[pl.BlockSpec((None, tm, W), lambda i, j: (j, i, 0)),
                  pl.BlockSpec((None, W, D), lambda i, j: (j, 0, 0)), row, vec, per_b, row],
        out_specs=[row, per_b, per_b, vec],
        out_shape=[jax.ShapeDtypeStruct((T, D), F32), per_b_shape, per_b_shape, jax.ShapeDtypeStruct((1, D), F32)],
        scratch_shapes=[pltpu.VMEM((tm, D), F32)],
        compiler_params=_params(2),
    )(dproj, w_in_g, x, gn, sc, dxo)


def _ada_fwd(c_all, w_ada, b_cols):
    nbatch, D = c_all.shape
    N = w_ada.shape[1]
    tn = _tile(N, 768)

    def body(c_ref, w_ref, b_ref, o_ref):
        cv = c_ref[...]
        act = (cv * _sigmoid(cv)).astype(BF16)
        o_ref[...] = _dot_nn(act, w_ref[...].astype(BF16)) + b_ref[...]

    return pl.pallas_call(
        body, name="ada_fwd", grid=(N // tn,),
        in_specs=[pl.BlockSpec((nbatch, D), lambda j: (0, 0)), pl.BlockSpec((D, tn), lambda j: (0, j)),
                  pl.BlockSpec((1, tn), lambda j: (0, j))],
        out_specs=pl.BlockSpec((nbatch, tn), lambda j: (0, j)),
        out_shape=jax.ShapeDtypeStruct((nbatch, N), F32),
        compiler_params=_params(1),
    )(c_all, w_ada, b_cols)


def _adamw(w, g, m, v):
    m = ADAM_B1 * m + (1.0 - ADAM_B1) * g
    v = ADAM_B2 * v + (1.0 - ADAM_B2) * (g * g)
    m_hat = m / (1.0 - ADAM_B1 ** ADAM_STEP)
    v_hat = v / (1.0 - ADAM_B2 ** ADAM_STEP)
    delta = -ADAM_LR * (m_hat / (jnp.sqrt(v_hat) + ADAM_EPS) + ADAM_WD * w)
    return delta, m, v


def _adam_call(w, g, m, v, name):
    R, C = w.shape
    tr = _row_tile(R, 512)

    def body(w_ref, g_ref, m_ref, v_ref, d_ref, mo_ref, vo_ref):
        d, mn, vn = _adamw(w_ref[...], g_ref[...], m_ref[...], v_ref[...])
        d_ref[...] = d
        mo_ref[...] = mn
        vo_ref[...] = vn

    blk = pl.BlockSpec((tr, C), lambda i: (i, 0))
    shape = jax.ShapeDtypeStruct((R, C), F32)
    return pl.pallas_call(
        body, name=name, grid=(R // tr,), in_specs=[blk] * 4, out_specs=[blk] * 3, out_shape=[shape] * 3,
        compiler_params=_params(1),
    )(w, g, m, v)


def _ada_adam(c_act_t, dmod_cols, w, m, v, comm):
    R, C = w.shape
    nbatch = c_act_t.shape[1]
    tr = _tile(R, 128)

    def body(ct_ref, dm_ref, w_ref, m_ref, v_ref, g_ref, d_ref, mo_ref, vo_ref):
        cv = ct_ref[...]
        g = _dot_nn((cv * _sigmoid(cv)).astype(BF16), dm_ref[...].astype(BF16))
        g_ref[...] = g
        d, mn, vn = _adamw(w_ref[...], g, m_ref[...], v_ref[...])
        d_ref[...] = d
        mo_ref[...] = mn
        vo_ref[...] = vn

    blk = pl.BlockSpec((tr, C), lambda i: (i, 0))
    shape = jax.ShapeDtypeStruct((R, C), F32)
    return _call(
        body, name="ada_adam", grid=(R // tr,),
        in_specs=[pl.BlockSpec((tr, nbatch), lambda i: (i, 0)), pl.BlockSpec((nbatch, C), lambda i: (0, 0)),
                  blk, blk, blk],
        out_specs=[blk] * 4, out_shape=[shape] * 4,
        operands=(c_act_t, dmod_cols, w, m, v), comm=comm)


def _small_adam(gathered, w, m, v, rows_b0, rows_b1, rows_vec):
    _, P, D = gathered.shape
    R = w.shape[0]

    def body(ga_ref, w_ref, m_ref, v_ref, sum_ref, g_ref, d_ref, mo_ref, vo_ref):
        total = ga_ref[0]
        for dev in range(1, N_DEV):
            total = total + ga_ref[dev]
        sum_ref[...] = total
        g_ref[...] = jnp.zeros_like(g_ref)
        g_ref[0:N_MOD, :] = (sum_ref[rows_b0:rows_b0 + N_MOD, :] + sum_ref[rows_b1:rows_b1 + N_MOD, :])
        g_ref[N_MOD:N_MOD + 8, :] = sum_ref[rows_vec:rows_vec + 8, :]
        d, mn, vn = _adamw(w_ref[...], g_ref[...], m_ref[...], v_ref[...])
        d_ref[...] = d
        mo_ref[...] = mn
        vo_ref[...] = vn

    shape = jax.ShapeDtypeStruct((R, D), F32)
    return pl.pallas_call(
        body, name="small_adam",
        in_specs=[VMEM_SPEC] * 4, out_specs=[VMEM_SPEC] * 5,
        out_shape=[jax.ShapeDtypeStruct((P, D), F32), shape, shape, shape, shape],
        compiler_params=pltpu.CompilerParams(vmem_limit_bytes=VMEM_LIMIT),
    )(gathered, w, m, v)


def _gather8(v, name):
    A, W = v.shape
    flips = [(fx, fy, fc) for fx in (0, 1) for fy in (0, 1) for fc in (0, 1) if (fx, fy, fc) != (0, 0, 0)]

    def body(v_ref, out_ref, send_sems, recv_sems, local_sem):
        x, y, c = _position()
        me = 4 * x + 2 * y + c
        mine = pltpu.make_async_copy(v_ref, out_ref.at[me], local_sem)
        mine.start()

        def copy(k, block, to):
            return pltpu.make_async_remote_copy(src_ref=v_ref, dst_ref=out_ref.at[block], send_sem=send_sems.at[k],
                                                recv_sem=recv_sems.at[k], device_id=to, device_id_type=MESH)

        peers = [(_flip(x, fx), _flip(y, fy), _flip(c, fc)) for fx, fy, fc in flips]
        sends = [copy(k, me, peer) for k, peer in enumerate(peers)]
        for cp in sends:
            cp.start()
        for k, (px, py, pc) in enumerate(peers):
            copy(k, 4 * px + 2 * py + pc, (px, py, pc)).wait_recv()
        for cp in sends:
            cp.wait_send()
        mine.wait()

    return pl.pallas_call(
        body, name=name, in_specs=[VMEM_SPEC], out_specs=VMEM_SPEC,
        out_shape=jax.ShapeDtypeStruct((N_DEV, A, W), v.dtype),
        scratch_shapes=[pltpu.SemaphoreType.DMA((N_DEV - 1,)), pltpu.SemaphoreType.DMA((N_DEV - 1,)),
                        pltpu.SemaphoreType.DMA],
    )(v)


def _mod_exchange(part):
    _, A, W = part.shape

    def body(p_ref, out_ref, send_sems, recv_sems, local_sem):
        x, y, c = _position()
        me = 4 * x + 2 * y + c
        chip = 2 * x + y
        mine = pltpu.make_async_copy(p_ref.at[me], out_ref.at[chip], local_sem)
        mine.start()
        peers = [(_flip(x, fx), _flip(y, fy)) for fx, fy in CHIP_FLIPS]
        sends = []
        for k, (px, py) in enumerate(peers):
            sends.append(pltpu.make_async_remote_copy(
                src_ref=p_ref.at[4 * px + 2 * py + c], dst_ref=out_ref.at[chip], send_sem=send_sems.at[k],
                recv_sem=recv_sems.at[k], device_id=(px, py, c), device_id_type=MESH))
        for cp in sends:
            cp.start()
        for k, (px, py) in enumerate(peers):
            pltpu.make_async_remote_copy(
                src_ref=p_ref.at[me], dst_ref=out_ref.at[2 * px + py], send_sem=send_sems.at[k],
                recv_sem=recv_sems.at[k], device_id=(px, py, c), device_id_type=MESH).wait_recv()
        for cp in sends:
            cp.wait_send()
        mine.wait()

    return pl.pallas_call(
        body, name="mod_exchange", in_specs=[VMEM_SPEC], out_specs=VMEM_SPEC,
        out_shape=jax.ShapeDtypeStruct((N_CHIP, A, W), part.dtype),
        scratch_shapes=[pltpu.SemaphoreType.DMA((3,)), pltpu.SemaphoreType.DMA((3,)), pltpu.SemaphoreType.DMA],
    )(part)


def _cast_slot(w, chip_idx, name):
    R, C = w.shape
    tr = _row_tile(R, 512)

    def body(chip_ref, w_ref, o_ref):
        o_ref[...] = w_ref[...].astype(BF16)

    return pl.pallas_call(
        body, name=name,
        grid_spec=pltpu.PrefetchScalarGridSpec(
            num_scalar_prefetch=1, grid=(R // tr,),
            in_specs=[pl.BlockSpec((tr, C), lambda i, chip_ref: (i, 0))],
            out_specs=pl.BlockSpec((None, tr, C), lambda i, chip_ref: (chip_ref[0], i, 0))),
        out_shape=jax.ShapeDtypeStruct((N_CHIP, R, C), BF16),
        compiler_params=_params(1),
    )(chip_idx, w)


def _sibling_swap_halves(grads16, name):
    n = len(grads16)

    def body(*refs):
        g_refs, out_refs = refs[:n], refs[n:2 * n]
        send_sems, recv_sems = refs[2 * n:]
        x, y, c = _position()
        copies = [pltpu.make_async_remote_copy(
            src_ref=g_refs[i].at[1 - c], dst_ref=out_refs[i], send_sem=send_sems.at[i], recv_sem=recv_sems.at[i],
            device_id=(x, y, 1 - c), device_id_type=MESH) for i in range(n)]
        for cp in copies:
            cp.start()
        for cp in copies:
            cp.wait()

    return pl.pallas_call(
        body, name=name, in_specs=[ANY] * n, out_specs=[ANY] * n,
        out_shape=[jax.ShapeDtypeStruct(g.shape[1:], g.dtype) for g in grads16],
        scratch_shapes=[pltpu.SemaphoreType.DMA((n,)), pltpu.SemaphoreType.DMA((n,))],
    )(*grads16)


def _pair_sum(g32, recv, core, name):
    _, J, r, C = g32.shape

    def body(core_ref, g_ref, r_ref, o_ref):
        o_ref[...] = (g_ref[...] + r_ref[...].astype(F32)).astype(BF16)

    return pl.pallas_call(
        body, name=name,
        grid_spec=pltpu.PrefetchScalarGridSpec(
            num_scalar_prefetch=1, grid=(J,),
            in_specs=[pl.BlockSpec((None, None, r, C), lambda j, core_ref: (core_ref[0], j, 0, 0)),
                      pl.BlockSpec((None, r, C), lambda j, core_ref: (j, 0, 0))],
            out_specs=pl.BlockSpec((None, r, C), lambda j, core_ref: (j, 0, 0))),
        out_shape=jax.ShapeDtypeStruct((J, r, C), BF16),
        compiler_params=_params(1),
    )(core, g32, recv)


def _chip_sum(g32, recv_sib, recv_chips, core_chip, name):
    _, J, r, C = g32.shape

    def body(idx_ref, g_ref, s_ref, o_ref_in, o_ref):
        total = g_ref[...] + s_ref[...].astype(F32)
        for k in range(3):
            total = total + o_ref_in[k].astype(F32)
        o_ref[...] = total

    return pl.pallas_call(
        body, name=name,
        grid_spec=pltpu.PrefetchScalarGridSpec(
            num_scalar_prefetch=1, grid=(1,),
            in_specs=[pl.BlockSpec((None, None, r, C), lambda i, idx: (idx[0], idx[1], 0, 0)),
                      pl.BlockSpec((None, r, C), lambda i, idx: (idx[1], 0, 0)),
                      pl.BlockSpec((3, r, C), lambda i, idx: (0, 0, 0))],
            out_specs=pl.BlockSpec((None, r, C), lambda i, idx: (idx[0], 0, 0))),
        out_shape=jax.ShapeDtypeStruct((2, r, C), F32),
        compiler_params=_params(1),
    )(core_chip, g32, recv_sib, recv_chips)


def _sibling_join_halves(halves, name):
    n = len(halves)

    def body(*refs):
        h_refs, out_refs = refs[:n], refs[n:2 * n]
        send_sems, recv_sems = refs[2 * n:]
        x, y, c = _position()

        def copy(i, which):
            return pltpu.make_async_remote_copy(
                src_ref=h_refs[i].at[which], dst_ref=out_refs[i].at[which], send_sem=send_sems.at[i],
                recv_sem=recv_sems.at[i], device_id=(x, y, 1 - c), device_id_type=MESH)

        for i in range(n):
            copy(i, c).start()
        for i in range(n):
            copy(i, 1 - c).wait_recv()
        for i in range(n):
            copy(i, c).wait_send()

    return pl.pallas_call(
        body, name=name, in_specs=[ANY] * n, out_specs=[ANY] * n,
        out_shape=[jax.ShapeDtypeStruct(h.shape, h.dtype) for h in halves],
        scratch_shapes=[pltpu.SemaphoreType.DMA((n,)), pltpu.SemaphoreType.DMA((n,))],
        input_output_aliases={i: i for i in range(n)},
    )(*halves)


BIG_WEIGHTS = ("ffn1_w_gate", "ffn1_w_up", "ffn1_w_down", "w_in", "w_attn_o", "w_conv_o", "w_out",
               "ffn2_w_gate", "ffn2_w_up", "ffn2_w_down")
VECTORS = ("norm_ffn1_g", "norm_mix_g", "conv_b_dw", "conv_ln_g", "conv_ln_b", "norm_ffn2_g", "final_norm_g")
ROW_DMOD0, ROW_DMOD1, ROW_VEC, ROW_SINK, ROW_CONVW, SMALL_ROWS = 0, 16, 33, 40, 41, 72


def _reduce_begin(grads, names, core_idx, tag):
    from_sibling = _sibling_swap_halves([grads[n][1] for n in names], "grad_swap_halves_" + tag)
    pair_sums = [_pair_sum(grads[n][0], r, core_idx, "pair_sum_" + n) for n, r in zip(names, from_sibling)]
    return from_sibling, pair_sums


def _reduce_end(grads, names, from_sibling, from_chips, core_chip, tag):
    halves = [_chip_sum(grads[n][0], rs, rc, core_chip, "chip_sum_" + n)
              for n, rs, rc in zip(names, from_sibling, from_chips)]
    return dict(zip(names, _sibling_join_halves(halves, "grad_join_halves_" + tag)))


FFN1_WEIGHTS = ("ffn1_w_gate", "ffn1_w_up", "ffn1_w_down")
FFN2_WEIGHTS = ("ffn2_w_gate", "ffn2_w_up", "ffn2_w_down")
MIX_WEIGHTS = ("w_in", "w_attn_o", "w_conv_o", "w_out")
COL_SHARDED = ("ffn1_w_gate", "ffn1_w_up", "ffn2_w_gate", "ffn2_w_up", "w_in")


def _local_grads(x, target, mod, slots, small, seq, core_idx, core_chip):
    T, D = x.shape
    B = T // seq
    mods = [mod[:, k][:, None, :] for k in range(N_MOD)]
    sh1, sc1, g1, sh2, sc2, g2, sh3, sc3, g3 = mods
    w = dict(zip(FFN1_WEIGHTS, _run_comm(_GatherComm([slots[n] for n in FFN1_WEIGHTS]), "gather_ffn1")))

    (h1, a1, u1, f1, x1), (w["w_in"],) = _ffn_fwd(
        x, small["norm_ffn1_g"], sc1, sh1, g1, w["ffn1_w_gate"], w["ffn1_w_up"], w["ffn1_w_down"], seq, "ffn1_fwd",
        comm=_GatherComm([slots["w_in"]]))
    w_in_full = w["w_in"].reshape(IN_WIDTH, D)
    q_end, v_end = D, D + 4 * HEAD_DIM
    w_in_cols = jnp.concatenate([w_in_full[:q_end], w_in_full[v_end:], w_in_full[q_end:v_end]], axis=0)
    (h2, proj), outs = _in_proj(x1, small["norm_mix_g"], sc2, sh2, w_in_cols, seq,
                                comm=_GatherComm([slots[n] for n in ("w_attn_o", "w_conv_o", "w_out")]))
    w_ao, w_co, w_o = [t.reshape(D, D) for t in outs]
    (o, lse), (w["ffn2_w_gate"], w["ffn2_w_up"]) = _attn_fwd(
        proj, small["attn_sinks"], B, seq, comm=_GatherComm([slots["ffn2_w_gate"], slots["ffn2_w_up"]]))
    (ydw, z), (w["ffn2_w_down"],) = _conv_fwd(
        proj, small["conv_w_dw"], small["conv_b_dw"], small["conv_ln_g"], small["conv_ln_b"], B, seq,
        comm=_GatherComm([slots["ffn2_w_down"]]))
    ya, yc, merged, mo, x2 = _merge(o, z, proj, w_ao, w_co, w_o, x1, g2, seq)
    (h3, a3, u3, f3, x3), _ = _ffn_fwd(x2, small["norm_ffn2_g"], sc3, sh3, g3, w["ffn2_w_gate"], w["ffn2_w_up"],
                                       w["ffn2_w_down"], seq, "ffn2_fwd")
    dx3, loss_parts, d_final_g = _final_loss(x3, small["final_norm_g"], target)

    grads = {}

    def ffn_backward(prefix, dxo, xin, h, a, u, f, gn, sc, gate, comm=None):
        (da, du, s, df, dx, dgate, dsc, dsh, dgn), comm_out = _ffn_bwd(
            dxo, xin, f, a, u, gn, sc, gate, w[prefix + "_w_gate"], w[prefix + "_w_up"], w[prefix + "_w_down"],
            seq, prefix + "_bwd", comm=comm)
        grads[prefix + "_w_gate"] = _wgrad(da, _spec_chip_major(FF_SHARD), h, _spec_rows(D), FF_SHARD, D, T,
                                           prefix + "_dw_gate")
        grads[prefix + "_w_up"] = _wgrad(du, _spec_chip_major(FF_SHARD), h, _spec_rows(D), FF_SHARD, D, T,
                                         prefix + "_dw_up")
        def down_grad(comm=None):
            return _wgrad(s, _spec_chip_major(FF_SHARD), df, _spec_rows(D), FF_SHARD, D, T, prefix + "_dw_down",
                          comm=comm)

        return (dx, dgate, dsc, dsh, dgn), comm_out, down_grad

    (dx2, dg3, dsc3, dsh3, d_gn3), _, down_grad = ffn_backward(
        "ffn2", dx3, x2, h3, a3, u3, f3, small["norm_ffn2_g"], sc3, g3)
    grads["ffn2_w_down"] = down_grad()
    sib2, pairs2 = _reduce_begin(grads, FFN2_WEIGHTS, core_idx, "ffn2")

    dmo, dya, dyc, dga, dgc, do, dz, dg2 = _merge_bwd(dx2, mo, g2, proj, ya, yc, w_o, w_ao, w_co, seq)
    shard = D // N_CHIP
    grads["w_out"] = _wgrad(merged, _spec_col_block(shard), dmo, _spec_rows(D), shard, D, T, "dw_out")
    grads["w_attn_o"] = _wgrad(o, _spec_col_block(shard), dya, _spec_rows(D), shard, D, T, "dw_attn_o")
    grads["w_conv_o"] = _wgrad(z, _spec_col_block(shard), dyc, _spec_rows(D), shard, D, T, "dw_conv_o")
    (dq, dkp, dko, dvp, dvo, dsink_steps), chips2 = _attn_bwd(proj, small["attn_sinks"], o, do, lse, B, seq,
                                                              comm=_ExchangeComm(pairs2))
    reduced = _reduce_end(grads, FFN2_WEIGHTS, sib2, chips2, core_chip, "ffn2")
    dca, dcb, d_conv_w, d_conv_b, d_ln_g, d_ln_b = _conv_bwd(proj, dz, ydw, small["conv_w_dw"], small["conv_ln_g"],
                                                              small["conv_ln_b"], B, seq)

    def band_sum(own, prev):
        prev = prev.reshape(B, seq // BLOCK, BLOCK, 2 * HEAD_DIM)
        moved = jnp.concatenate([prev[:, 1:], jnp.zeros_like(prev[:, :1])], axis=1)
        return (own + moved.reshape(T, 2 * HEAD_DIM)).astype(BF16)

    dproj = jnp.concatenate([dq, band_sum(dko, dkp), band_sum(dvo, dvp), dca, dcb, dga, dgc], axis=1)
    dproj = dproj.reshape(T, N_CHIP, IN_SHARD).transpose(1, 0, 2)
    grads["w_in"] = _wgrad(dproj, _spec_chip_major(IN_SHARD), h2, _spec_rows(D), IN_SHARD, D, T, "dw_in")
    dx1, dsc2, dsh2, d_gn2 = _in_proj_bwd(dproj, w["w_in"], x1, small["norm_mix_g"], sc2, dx2, seq)
    sib_mix, pairs_mix = _reduce_begin(grads, MIX_WEIGHTS, core_idx, "mix")

    (dx0, dg1, dsc1, dsh1, d_gn1), chips_mix, down_grad = ffn_backward(
        "ffn1", dx1, x, h1, a1, u1, f1, small["norm_ffn1_g"], sc1, g1, comm=_ExchangeComm(pairs_mix))
    reduced.update(_reduce_end(grads, MIX_WEIGHTS, sib_mix, chips_mix, core_chip, "mix"))
    gate_up = FFN1_WEIGHTS[:2]
    sib_gu, pairs_gu = _reduce_begin(grads, gate_up, core_idx, "ffn1_gate_up")
    grads["ffn1_w_down"], chips_gu = down_grad(comm=_ExchangeComm(pairs_gu))
    reduced.update(_reduce_end(grads, gate_up, sib_gu, chips_gu, core_chip, "ffn1_gate_up"))
    sib_d, pairs_d = _reduce_begin(grads, FFN1_WEIGHTS[2:], core_idx, "ffn1_down")

    def finish_reduction(chips_d):
        reduced.update(_reduce_end(grads, FFN1_WEIGHTS[2:], sib_d, chips_d, core_chip, "ffn1_down"))
        return reduced

    dmod = jnp.concatenate([dsh1, dsc1, dg1, dsh2, dsc2, dg2, dsh3, dsc3, dg3], axis=1)
    d_sinks = jnp.sum(dsink_steps, axis=0)
    vec_grads = {"norm_ffn1_g": d_gn1, "norm_mix_g": d_gn2, "conv_b_dw": d_conv_b, "conv_ln_g": d_ln_g,
                 "conv_ln_b": d_ln_b, "norm_ffn2_g": d_gn3, "final_norm_g": d_final_g}
    return loss_parts, dx0, (_ExchangeComm(pairs_d), finish_reduction), dmod, vec_grads, d_sinks, d_conv_w


def kernel(x, c, w_ada, b_ada, norm_ffn1_g, ffn1_w_gate, ffn1_w_up, ffn1_w_down, norm_mix_g, w_in, attn_sinks, w_attn_o, conv_w_dw, conv_b_dw, conv_ln_g, conv_ln_b, w_conv_o, w_out, norm_ffn2_g, ffn2_w_gate, ffn2_w_up, ffn2_w_down, final_norm_g, loss_target, m_w_ada, m_b_ada, m_norm_ffn1_g, m_ffn1_w_gate, m_ffn1_w_up, m_ffn1_w_down, m_norm_mix_g, m_w_in, m_attn_sinks, m_w_attn_o, m_conv_w_dw, m_conv_b_dw, m_conv_ln_g, m_conv_ln_b, m_w_conv_o, m_w_out, m_norm_ffn2_g, m_ffn2_w_gate, m_ffn2_w_up, m_ffn2_w_down, m_final_norm_g, v_w_ada, v_b_ada, v_norm_ffn1_g, v_ffn1_w_gate, v_ffn1_w_up, v_ffn1_w_down, v_norm_mix_g, v_w_in, v_attn_sinks, v_w_attn_o, v_conv_w_dw, v_conv_b_dw, v_conv_ln_g, v_conv_ln_b, v_w_conv_o, v_w_out, v_norm_ffn2_g, v_ffn2_w_gate, v_ffn2_w_up, v_ffn2_w_down, v_final_norm_g):
    args = dict(locals())
    B, seq, D = x.shape
    T = B * seq
    xi, yi, ci = _position()
    chip = 2 * xi + yi
    dev = 4 * xi + 2 * yi + ci

    def shard_2d(prefix, name):
        t = args[prefix + name][0]
        return t.T if name in COL_SHARDED else t

    big = {n: shard_2d("", n) for n in BIG_WEIGHTS}
    final_g = final_norm_g[None, :]
    vec_w = {n: (args[n] if n != "final_norm_g" else final_g) for n in VECTORS}

    conv_cols = D // N_CHIP
    conv_flat = jnp.pad(conv_w_dw[0].reshape(-1), (0, 8 * D - CONV_WIDTH * conv_cols)).reshape(8, D)
    first = _gather8(jnp.concatenate([jnp.pad(c, ((0, 8 - B), (0, 0))), conv_flat], axis=0), "gather_c")
    c_all = first[:, :B].reshape(N_DEV * B, D)
    conv_taps = first[::2, 8:].reshape(N_CHIP, 8 * D)[:, :CONV_WIDTH * conv_cols]
    conv_taps = conv_taps.reshape(N_CHIP, CONV_WIDTH, conv_cols).transpose(1, 0, 2).reshape(CONV_WIDTH, D)
    conv_taps = jnp.pad(conv_taps, ((0, CONV_PAD - CONV_WIDTH), (0, 0)))

    ada_cols = w_ada.shape[2]
    b_cols = lax.dynamic_slice(b_ada, (0, chip * ada_cols), (1, ada_cols))
    mod_part = _ada_fwd(c_all, w_ada[0], b_cols).reshape(N_DEV, B, ada_cols)
    mod = _mod_exchange(mod_part).transpose(1, 0, 2).reshape(B, N_MOD, D)

    core_idx = jnp.reshape(ci, (1,)).astype(jnp.int32)
    chip_idx = jnp.reshape(chip, (1,)).astype(jnp.int32)
    core_chip = jnp.stack([ci, chip]).astype(jnp.int32)
    slots = {n: _cast_slot(big[n], chip_idx, "cast_" + n) for n in BIG_WEIGHTS}

    small = dict(vec_w)
    small["attn_sinks"] = attn_sinks
    small["conv_w_dw"] = conv_taps

    loss_parts, dx, (last_exchange, finish_reduction), dmod, vec_grads, d_sinks, d_conv_w = _local_grads(
        x.reshape(T, D), loss_target.reshape(T, D), mod, slots, small, seq, core_idx, core_chip)

    loss = lax.psum((0.5 / D) * jnp.sum(loss_parts), ("x", "y", "c"))
    grad_x = dx.reshape(B, seq, D)
    out = {}

    block = jnp.zeros((SMALL_ROWS, D), F32)
    block = block.at[ROW_DMOD0:ROW_DMOD0 + N_MOD].set(dmod[0]).at[ROW_DMOD1:ROW_DMOD1 + N_MOD].set(dmod[1])
    block = block.at[ROW_VEC:ROW_VEC + len(VECTORS)].set(jnp.concatenate([vec_grads[n] for n in VECTORS], axis=0))
    block = block.at[ROW_SINK, :2 * HEAD_DIM].set(d_sinks[0])
    block = block.at[ROW_CONVW:ROW_CONVW + CONV_WIDTH].set(d_conv_w[:CONV_WIDTH])
    small_all = _gather8(block, "gather_small_grads")

    def pack_small(prefix):
        rows = [args[prefix + "b_ada"].reshape(N_MOD, D)]
        rows += [args[prefix + n].reshape(1, D) for n in VECTORS]
        rows += [jnp.pad(args[prefix + "attn_sinks"], ((0, 0), (0, D - N_Q_HEADS)))]
        return jnp.pad(jnp.concatenate(rows, axis=0), ((0, 24 - N_MOD - len(VECTORS) - 1), (0, 0)))

    small_sum, sg, sd, sm, sv = _small_adam(small_all, pack_small(""), pack_small("m_"), pack_small("v_"),
                                           ROW_DMOD0, ROW_DMOD1, ROW_VEC)

    def unpack_small(t):
        res = {"b_ada": t[:N_MOD].reshape(1, N_MOD * D)}
        for k, n in enumerate(VECTORS):
            res[n] = t[N_MOD + k].reshape(args[n].shape)
        res["attn_sinks"] = t[N_MOD + len(VECTORS), :N_Q_HEADS].reshape(1, N_Q_HEADS)
        return res

    unpacked = [unpack_small(t) for t in (sg, sd, sm, sv)]
    for n in ("b_ada", "attn_sinks") + VECTORS:
        out[n] = tuple(u[n] for u in unpacked)

    conv_g = lax.dynamic_slice(small_sum, (ROW_CONVW, chip * conv_cols), (CONV_WIDTH, conv_cols))
    d, mn, vn = _adam_call(conv_w_dw[0], conv_g, m_conv_w_dw[0], v_conv_w_dw[0], "adam_conv_w_dw")
    out["conv_w_dw"] = tuple(t[None] for t in (conv_g, d, mn, vn))

    dmod_rows = jnp.stack([small_all[:, ROW_DMOD0:ROW_DMOD0 + N_MOD], small_all[:, ROW_DMOD1:ROW_DMOD1 + N_MOD]], axis=1)
    dmod_all = dmod_rows.reshape(N_DEV * B, N_MOD * D)
    dmod_cols = lax.dynamic_slice(dmod_all, (0, chip * ada_cols), (N_DEV * B, ada_cols))
    ada_out, chips_last = _ada_adam(c_all.T, dmod_cols, w_ada[0], m_w_ada[0], v_w_ada[0], last_exchange)
    out["w_ada"] = tuple(t[None] for t in ada_out)

    reduced = finish_reduction(chips_last)
    for n in BIG_WEIGHTS:
        g = reduced[n].reshape(big[n].shape)
        d, mn, vn = _adam_call(big[n], g, shard_2d("m_", n), shard_2d("v_", n), "adam_" + n)
        out[n] = tuple((t.T if n in COL_SHARDED else t)[None] for t in (g, d, mn, vn))

    order = ("w_ada", "b_ada", "norm_ffn1_g", "ffn1_w_gate", "ffn1_w_up", "ffn1_w_down", "norm_mix_g", "w_in",
             "attn_sinks", "w_attn_o", "conv_w_dw", "conv_b_dw", "conv_ln_g", "conv_ln_b", "w_conv_o", "w_out",
             "norm_ffn2_g", "ffn2_w_gate", "ffn2_w_up", "ffn2_w_down", "final_norm_g")
    return (loss, grad_x, *[out[n][0] for n in order], *[out[n][1] for n in order],
            *[out[n][2] for n in order], *[out[n][3] for n in order])
```

```python
import functools

import jax
import jax.numpy as jnp
from jax import lax
from jax.experimental import pallas as pl
from jax.experimental.pallas import tpu as pltpu

F32 = jnp.float32
BF16 = jnp.bfloat16

D_MODEL = 1024
D_FF = 2816
N_CHIP = 4
N_DEV = 8
FF_SHARD = D_FF // N_CHIP
IN_WIDTH = 5376
IN_SHARD = IN_WIDTH // N_CHIP
HEAD_DIM = 64
N_Q_HEADS = 16
N_KV_HEADS = 2
BLOCK = 128
CONV_WIDTH = 31
CONV_PAD = 32
N_MOD = 9
EPS = 1e-6
FFN_RESIDUAL = 0.5
ATTN_SCALE = HEAD_DIM ** -0.5
MASK_VALUE = -1e30

ADAM_LR = 0.001
ADAM_B1 = 0.9
ADAM_B2 = 0.999
ADAM_EPS = 1e-08
ADAM_WD = 0.01
ADAM_STEP = 10

COLB_Q, COLB_CA, COLB_CB, COLB_GA, COLB_GC = 0, 1, 2, 3, 4
COLB_K, COLB_V = 40, 41
PROJ_TILE = 768

VMEM_LIMIT = 56 * 1024 * 1024
MESH = pl.DeviceIdType.MESH
ANY = pl.BlockSpec(memory_space=pl.ANY)
VMEM_SPEC = pl.BlockSpec(memory_space=pltpu.VMEM)
SMEM_SPEC = pl.BlockSpec(memory_space=pltpu.SMEM)


def _params(n_grid):
    return pltpu.CompilerParams(dimension_semantics=("arbitrary",) * n_grid, vmem_limit_bytes=VMEM_LIMIT)


def _tile(n, pref):
    t = min(n, pref)
    while n % t:
        t //= 2
    return t


def _row_tile(rows, cap):
    for t in range(min(rows, cap) // 16 * 16, 0, -16):
        if rows % t == 0:
            return t
    return rows


def _sigmoid(v):
    return 1.0 / (1.0 + jnp.exp(-v))


def _dot_nn(a, b):
    return lax.dot_general(a, b, (((1,), (0,)), ((), ())), preferred_element_type=F32)


def _dot_nt(a, b):
    return lax.dot_general(a, b, (((1,), (1,)), ((), ())), preferred_element_type=F32)


def _dot_tn(a, b):
    return lax.dot_general(a, b, (((0,), (0,)), ((), ())), preferred_element_type=F32)


def _norm_mod(xv, gn, sc, sh):
    r = lax.rsqrt(jnp.mean(xv * xv, axis=-1, keepdims=True) + EPS)
    return ((xv * r) * gn) * (1.0 + sc) + sh


def _accumulate(ref, first, value):
    @pl.when(first)
    def _():
        ref[...] = value

    @pl.when(jnp.logical_not(first))
    def _():
        ref[...] += value


def _norm_mod_bwd(dh, xv, gn, sc, dxo, first_of_batch, first, dx_ref, dsc_ref, dsh_ref, dgn_ref):
    r = lax.rsqrt(jnp.mean(xv * xv, axis=-1, keepdims=True) + EPS)
    xh = xv * r
    _accumulate(dsh_ref, first_of_batch, jnp.sum(dh, axis=0, keepdims=True))
    _accumulate(dsc_ref, first_of_batch, jnp.sum(dh * (xh * gn), axis=0, keepdims=True))
    dn = dh * (1.0 + sc)
    _accumulate(dgn_ref, first, jnp.sum(dn * xh, axis=0, keepdims=True))
    dxh = dn * gn
    dx_ref[...] = dxo + r * (dxh - xh * jnp.mean(dxh * xh, axis=-1, keepdims=True))


CHIP_FLIPS = ((1, 0), (0, 1), (1, 1))


def _position():
    return lax.axis_index("x"), lax.axis_index("y"), lax.axis_index("c")


def _flip(v, f):
    return 1 - v if f else v


class _GatherComm:
    def __init__(self, bufs):
        n = len(bufs)
        self.n = n
        self.operands = list(bufs)
        self.out_shape = [jax.ShapeDtypeStruct(b.shape, b.dtype) for b in bufs]
        self.aliases = {i: i for i in range(n)}
        self.sems = [pltpu.SemaphoreType.DMA((6 * n,)), pltpu.SemaphoreType.DMA((6 * n,))]
        self.rows = [b.shape[1] // 2 for b in bufs]

    def _half(self, ref, i, which):
        return ref.at[pl.ds(which * self.rows[i], self.rows[i]), :]

    def _ici(self, cins, couts, sems, i, k, dst_chip, to):
        x, y, c = _position()
        return pltpu.make_async_remote_copy(
            src_ref=self._half(cins[i].at[2 * x + y], i, c), dst_ref=self._half(couts[i].at[dst_chip], i, c),
            send_sem=sems[0].at[3 * i + k], recv_sem=sems[1].at[3 * i + k], device_id=to, device_id_type=MESH)

    def _d2d(self, couts, sems, i, k, src_chip, which):
        x, y, c = _position()
        place = self._half(couts[i].at[src_chip], i, which)
        return pltpu.make_async_remote_copy(
            src_ref=place, dst_ref=place, send_sem=sems[0].at[3 * self.n + 3 * i + k],
            recv_sem=sems[1].at[3 * self.n + 3 * i + k], device_id=(x, y, 1 - c), device_id_type=MESH)

    def _peers(self):
        x, y, _ = _position()
        return [(_flip(x, fx), _flip(y, fy)) for fx, fy in CHIP_FLIPS]

    def start(self, cins, couts, sems):
        x, y, c = _position()
        for i in range(self.n):
            for k, (px, py) in enumerate(self._peers()):
                self._ici(cins, couts, sems, i, k, 2 * x + y, (px, py, c)).start()

    def finish(self, cins, couts, sems):
        _, _, c = _position()
        peers = self._peers()
        for i in range(self.n):
            for k, (px, py) in enumerate(peers):
                self._ici(cins, couts, sems, i, k, 2 * px + py, (px, py, c)).wait_recv()
                self._d2d(couts, sems, i, k, 2 * px + py, c).start()
        for i in range(self.n):
            for k, (px, py) in enumerate(peers):
                self._d2d(couts, sems, i, k, 2 * px + py, 1 - c).wait_recv()
        for i in range(self.n):
            for k, (px, py) in enumerate(peers):
                self._ici(cins, couts, sems, i, k, 2 * px + py, (px, py, c)).wait_send()
                self._d2d(couts, sems, i, k, 2 * px + py, c).wait_send()


class _ExchangeComm:
    def __init__(self, pairs):
        n = len(pairs)
        self.n = n
        self.operands = list(pairs)
        self.out_shape = [jax.ShapeDtypeStruct((3,) + p.shape[1:], p.dtype) for p in pairs]
        self.aliases = {}
        self.sems = [pltpu.SemaphoreType.DMA((3 * n,)), pltpu.SemaphoreType.DMA((3 * n,))]

    def _copies(self, cins, couts, sems):
        x, y, c = _position()
        peers = [(_flip(x, fx), _flip(y, fy)) for fx, fy in CHIP_FLIPS]
        return [pltpu.make_async_remote_copy(
            src_ref=cins[i].at[2 * px + py], dst_ref=couts[i].at[k], send_sem=sems[0].at[3 * i + k],
            recv_sem=sems[1].at[3 * i + k], device_id=(px, py, c), device_id_type=MESH)
            for i in range(self.n) for k, (px, py) in enumerate(peers)]

    def start(self, cins, couts, sems):
        for cp in self._copies(cins, couts, sems):
            cp.start()

    def finish(self, cins, couts, sems):
        for cp in self._copies(cins, couts, sems):
            cp.wait()


def _call(body, *, name, grid, in_specs, out_specs, out_shape, operands, scratch_shapes=(), comm=None):
    n_grid = len(grid)
    if comm is None:
        return pl.pallas_call(
            body, name=name, grid=grid, in_specs=list(in_specs), out_specs=list(out_specs), out_shape=list(out_shape),
            scratch_shapes=list(scratch_shapes), compiler_params=_params(n_grid))(*operands), ()
    counts = (len(in_specs), len(comm.operands), len(out_specs), len(comm.out_shape), len(scratch_shapes),
              len(comm.sems))

    def fused(*refs):
        parts, pos = [], 0
        for k in counts:
            parts.append(refs[pos:pos + k])
            pos += k
        ins, cins, outs, couts, scr, sems = parts
        first = functools.reduce(jnp.logical_and, [pl.program_id(d) == 0 for d in range(n_grid)])
        last = functools.reduce(jnp.logical_and, [pl.program_id(d) == grid[d] - 1 for d in range(n_grid)])

        @pl.when(first)
        def _():
            comm.start(cins, couts, sems)

        body(*ins, *outs, *scr)

        @pl.when(last)
        def _():
            comm.finish(cins, couts, sems)

    res = pl.pallas_call(
        fused, name=name, grid=grid, in_specs=list(in_specs) + [ANY] * counts[1],
        out_specs=list(out_specs) + [ANY] * counts[3], out_shape=list(out_shape) + list(comm.out_shape),
        scratch_shapes=list(scratch_shapes) + list(comm.sems),
        input_output_aliases={counts[0] + i: counts[2] + j for i, j in comm.aliases.items()},
        compiler_params=_params(n_grid))(*operands, *comm.operands)
    return res[:counts[2]], res[counts[2]:]


def _run_comm(comm, name):
    k_in, k_out = len(comm.operands), len(comm.out_shape)

    def body(*refs):
        cins, couts, sems = refs[:k_in], refs[k_in:k_in + k_out], refs[k_in + k_out:]
        comm.start(cins, couts, sems)
        comm.finish(cins, couts, sems)

    return pl.pallas_call(
        body, name=name, in_specs=[ANY] * k_in, out_specs=[ANY] * k_out, out_shape=list(comm.out_shape),
        scratch_shapes=list(comm.sems), input_output_aliases=dict(comm.aliases))(*comm.operands)


def _ffn_fwd(x, gn, sc, sh, gate, wg, wu, wd, seq, name, comm=None):
    T, D = x.shape
    J, Fs, _ = wg.shape
    tm = _tile(seq, 1024)
    nb = seq // tm

    def body(x_ref, gn_ref, sc_ref, sh_ref, gate_ref, wg_ref, wu_ref, wd_ref,
             h_ref, a_ref, u_ref, f_ref, xo_ref, hs, acc):
        j = pl.program_id(1)

        @pl.when(j == 0)
        def _():
            hb = _norm_mod(x_ref[...], gn_ref[...], sc_ref[...], sh_ref[...]).astype(BF16)
            hs[...] = hb
            h_ref[...] = hb
            acc[...] = jnp.zeros_like(acc)

        hb = hs[...]
        a = _dot_nt(hb, wg_ref[...])
        u = _dot_nt(hb, wu_ref[...])
        a_ref[...] = a.astype(BF16)
        u_ref[...] = u.astype(BF16)
        s = ((a * _sigmoid(a)) * u).astype(BF16)
        acc[...] += _dot_nn(s, wd_ref[...])

        @pl.when(j == J - 1)
        def _():
            f = acc[...]
            f_ref[...] = f.astype(BF16)
            xo_ref[...] = x_ref[...] + (FFN_RESIDUAL * gate_ref[...]) * f

    row = pl.BlockSpec((tm, D), lambda i, j: (i, 0))
    vec = pl.BlockSpec((1, D), lambda i, j: (0, 0))
    per_b = pl.BlockSpec((None, 1, D), lambda i, j: (i // nb, 0, 0))
    hid = pl.BlockSpec((None, tm, Fs), lambda i, j: (j, i, 0))
    return _call(
        body, name=name, grid=(T // tm, J),
        in_specs=[row, vec, per_b, per_b, per_b] + [pl.BlockSpec((None, Fs, D), lambda i, j: (j, 0, 0))] * 3,
        out_specs=[row, hid, hid, row, row],
        out_shape=[jax.ShapeDtypeStruct((T, D), BF16), jax.ShapeDtypeStruct((J, T, Fs), BF16),
                   jax.ShapeDtypeStruct((J, T, Fs), BF16), jax.ShapeDtypeStruct((T, D), BF16),
                   jax.ShapeDtypeStruct((T, D), F32)],
        scratch_shapes=[pltpu.VMEM((tm, D), BF16), pltpu.VMEM((tm, D), F32)],
        operands=(x, gn, sc, sh, gate, wg, wu, wd), comm=comm)


def _ffn_bwd(dxo, x, f, a, u, gn, sc, gate, wg, wu, wd, seq, name, comm=None):
    T, D = x.shape
    J, Fs, _ = wg.shape
    B = T // seq
    tm = _tile(seq, 512)
    nb = seq // tm

    def body(dxo_ref, x_ref, f_ref, a_ref, u_ref, gn_ref, sc_ref, gate_ref, wg_ref, wu_ref, wd_ref,
             da_ref, du_ref, s_ref, df_ref, dx_ref, dgate_ref, dsc_ref, dsh_ref, dgn_ref, dfs, acc):
        i = pl.program_id(0)
        j = pl.program_id(1)
        first_of_batch = i % nb == 0

        @pl.when(j == 0)
        def _():
            dxo_v = dxo_ref[...]
            dfb = ((FFN_RESIDUAL * gate_ref[...]) * dxo_v).astype(BF16)
            dfs[...] = dfb
            df_ref[...] = dfb
            part = jnp.sum((FFN_RESIDUAL * f_ref[...].astype(F32)) * dxo_v, axis=0, keepdims=True)
            _accumulate(dgate_ref, first_of_batch, part)
            acc[...] = jnp.zeros_like(acc)

        ds = _dot_nt(dfs[...], wd_ref[...])
        av = a_ref[...].astype(F32)
        uv = u_ref[...].astype(F32)
        sig = _sigmoid(av)
        sil = av * sig
        s_ref[...] = (sil * uv).astype(BF16)
        dab = (ds * uv * (sig * (1.0 + av * (1.0 - sig)))).astype(BF16)
        dub = (ds * sil).astype(BF16)
        da_ref[...] = dab
        du_ref[...] = dub
        acc[...] += _dot_nn(dab, wg_ref[...]) + _dot_nn(dub, wu_ref[...])

        @pl.when(j == J - 1)
        def _():
            _norm_mod_bwd(acc[...], x_ref[...], gn_ref[...], sc_ref[...], dxo_ref[...],
                          first_of_batch, i == 0, dx_ref, dsc_ref, dsh_ref, dgn_ref)

    row = pl.BlockSpec((tm, D), lambda i, j: (i, 0))
    vec = pl.BlockSpec((1, D), lambda i, j: (0, 0))
    per_b = pl.BlockSpec((None, 1, D), lambda i, j: (i // nb, 0, 0))
    hid = pl.BlockSpec((None, tm, Fs), lambda i, j: (j, i, 0))
    hid_shape = jax.ShapeDtypeStruct((J, T, Fs), BF16)
    per_b_shape = jax.ShapeDtypeStruct((B, 1, D), F32)
    return _call(
        body, name=name, grid=(T // tm, J),
        in_specs=[row, row, row, hid, hid, vec, per_b, per_b]
        + [pl.BlockSpec((None, Fs, D), lambda i, j: (j, 0, 0))] * 3,
        out_specs=[hid, hid, hid, row, row, per_b, per_b, per_b, vec],
        out_shape=[hid_shape, hid_shape, hid_shape, jax.ShapeDtypeStruct((T, D), BF16),
                   jax.ShapeDtypeStruct((T, D), F32), per_b_shape, per_b_shape, per_b_shape,
                   jax.ShapeDtypeStruct((1, D), F32)],
        scratch_shapes=[pltpu.VMEM((tm, D), BF16), pltpu.VMEM((tm, D), F32)],
        operands=(dxo, x, f, a, u, gn, sc, gate, wg, wu, wd), comm=comm)


def _wgrad(a, a_spec, b, b_spec, rows, cols, n_tok, name, comm=None):
    tk = _tile(n_tok, 1024)
    nk = n_tok // tk
    half = rows // 2

    def body(a_ref, b_ref, o32_ref, o16_ref, acc):
        k = pl.program_id(1)

        @pl.when(k == 0)
        def _():
            acc[...] = jnp.zeros_like(acc)

        acc[...] += _dot_tn(a_ref[...], b_ref[...])

        @pl.when(k == nk - 1)
        def _():
            for h in range(2):
                v = acc[h * half:(h + 1) * half, :]
                o32_ref[h] = v
                o16_ref[h] = v.astype(BF16)

    out_spec = pl.BlockSpec((2, None, half, cols), lambda j, k: (0, j, 0, 0))
    outs, comm_outs = _call(
        body, name=name, grid=(N_CHIP, nk),
        in_specs=[a_spec(tk), b_spec(tk)],
        out_specs=[out_spec, out_spec],
        out_shape=[jax.ShapeDtypeStruct((2, N_CHIP, half, cols), F32),
                   jax.ShapeDtypeStruct((2, N_CHIP, half, cols), BF16)],
        scratch_shapes=[pltpu.VMEM((rows, cols), F32)],
        operands=(a, b), comm=comm)
    return outs if comm is None else (outs, comm_outs)


def _spec_rows(width):
    return lambda tk: pl.BlockSpec((tk, width), lambda j, k: (k, 0))


def _spec_chip_major(width):
    return lambda tk: pl.BlockSpec((None, tk, width), lambda j, k: (j, k, 0))


def _spec_col_block(width):
    return lambda tk: pl.BlockSpec((tk, width), lambda j, k: (k, j))


def _in_proj(x, gn, sc, sh, w_in, seq, comm=None):
    T, D = x.shape
    N = w_in.shape[0]
    tm = _tile(seq, 1024)
    nb = seq // tm

    def body(x_ref, gn_ref, sc_ref, sh_ref, w_ref, h_ref, p_ref, hs):
        @pl.when(pl.program_id(1) == 0)
        def _():
            hb = _norm_mod(x_ref[...], gn_ref[...], sc_ref[...], sh_ref[...]).astype(BF16)
            hs[...] = hb
            h_ref[...] = hb

        p_ref[...] = _dot_nt(hs[...], w_ref[...]).astype(BF16)

    row = pl.BlockSpec((tm, D), lambda i, j: (i, 0))
    per_b = pl.BlockSpec((None, 1, D), lambda i, j: (i // nb, 0, 0))
    return _call(
        body, name="mix_in_proj", grid=(T // tm, N // PROJ_TILE),
        in_specs=[row, pl.BlockSpec((1, D), lambda i, j: (0, 0)), per_b, per_b,
                  pl.BlockSpec((PROJ_TILE, D), lambda i, j: (j, 0))],
        out_specs=[row, pl.BlockSpec((tm, PROJ_TILE), lambda i, j: (i, j))],
        out_shape=[jax.ShapeDtypeStruct((T, D), BF16), jax.ShapeDtypeStruct((T, N), BF16)],
        scratch_shapes=[pltpu.VMEM((tm, D), BF16)],
        operands=(x, gn, sc, sh, w_in), comm=comm)


def _attn_specs(nblk):
    def own(col):
        return lambda b, n: (b * nblk + n, col)

    def prev(col):
        return lambda b, n: (b * nblk + jnp.maximum(n - 1, 0), col)

    kv = (BLOCK, 2 * HEAD_DIM)
    return [pl.BlockSpec((BLOCK, D_MODEL), own(COLB_Q)),
            pl.BlockSpec(kv, prev(COLB_K)), pl.BlockSpec(kv, own(COLB_K)),
            pl.BlockSpec(kv, prev(COLB_V)), pl.BlockSpec(kv, own(COLB_V))]


def _band_operands(prev_ref, own_ref, lo):
    band = jnp.concatenate([prev_ref[...], own_ref[...]], axis=0).astype(F32)
    rolled = pltpu.roll(band, HEAD_DIM, 1)
    zero = jnp.zeros_like(band)
    head0 = jnp.concatenate([jnp.where(lo, band, zero), jnp.where(lo, zero, rolled)], axis=0).astype(BF16)
    head1 = jnp.concatenate([jnp.where(lo, rolled, zero), jnp.where(lo, zero, band)], axis=0).astype(BF16)
    return head0, head1


PAIRS_PER_KV = N_Q_HEADS // 2 // N_KV_HEADS
BAND = 2 * BLOCK


def _band_valid(has_prev):
    qi = lax.broadcasted_iota(jnp.int32, (PAIRS_PER_KV * BLOCK, BAND), 0) & (BLOCK - 1)
    sj = lax.broadcasted_iota(jnp.int32, (PAIRS_PER_KV * BLOCK, BAND), 1)
    rel = qi + BLOCK - sj
    return (rel >= 0) & (rel < BLOCK) & ((sj >= BLOCK) | has_prev)


def _pair_lanes(kvh, pp):
    pair = kvh * PAIRS_PER_KV + pp
    return slice(pair * 2 * HEAD_DIM, (pair + 1) * 2 * HEAD_DIM)


def _stack_pairs(ref, kvh):
    return jnp.concatenate([ref[:, _pair_lanes(kvh, pp)] for pp in range(PAIRS_PER_KV)], axis=0)


def _rows_per_pair(columns):
    return jnp.concatenate(columns, axis=0)


def _attn_fwd(proj, sinks, batch, seq, comm=None):
    T = proj.shape[0]
    nblk = seq // BLOCK

    def body(sink_ref, q_ref, kp_ref, ko_ref, vp_ref, vo_ref, o_ref, lse_ref):
        lo = lax.broadcasted_iota(jnp.int32, (1, 2 * HEAD_DIM), 1) < HEAD_DIM
        head_lane = lax.broadcasted_iota(jnp.int32, (1, N_Q_HEADS), 1)
        valid = _band_valid(pl.program_id(1) > 0)
        k_ops = _band_operands(kp_ref, ko_ref, lo)
        v_ops = _band_operands(vp_ref, vo_ref, lo)
        lse_all = jnp.zeros((BLOCK, N_Q_HEADS), F32)
        col = jnp.zeros((BLOCK, 1), F32)
        for kvh in range(N_KV_HEADS):
            s_all = _dot_nt(_stack_pairs(q_ref, kvh), k_ops[kvh]) * ATTN_SCALE
            probs = []
            for side in range(2):
                heads = [2 * (kvh * PAIRS_PER_KV + pp) + side for pp in range(PAIRS_PER_KV)]
                sink = _rows_per_pair([col + sink_ref[0, h] for h in heads])
                s = jnp.where(valid, s_all[:, side * BAND:(side + 1) * BAND], MASK_VALUE)
                m = jnp.maximum(jnp.max(s, axis=-1, keepdims=True), sink)
                p = jnp.where(valid, jnp.exp(s - m), 0.0)
                den = jnp.sum(p, axis=-1, keepdims=True) + jnp.exp(sink - m)
                probs.append((p * (1.0 / den)).astype(BF16))
                lse = m + jnp.log(den)
                for pp, h in enumerate(heads):
                    lse_all = jnp.where(head_lane == h, lse[pp * BLOCK:(pp + 1) * BLOCK], lse_all)
            out = _dot_nn(jnp.concatenate(probs, axis=1), v_ops[kvh])
            for pp in range(PAIRS_PER_KV):
                o_ref[:, _pair_lanes(kvh, pp)] = out[pp * BLOCK:(pp + 1) * BLOCK].astype(BF16)
        lse_ref[...] = lse_all

    return _call(
        body, name="attn_fwd", grid=(batch, nblk),
        in_specs=[SMEM_SPEC] + _attn_specs(nblk),
        out_specs=[pl.BlockSpec((BLOCK, D_MODEL), lambda b, n: (b * nblk + n, 0)),
                   pl.BlockSpec((BLOCK, N_Q_HEADS), lambda b, n: (b * nblk + n, 0))],
        out_shape=[jax.ShapeDtypeStruct((T, D_MODEL), BF16), jax.ShapeDtypeStruct((T, N_Q_HEADS), F32)],
        operands=(sinks, proj, proj, proj, proj, proj), comm=comm)


def _conv_u(ca, cb):
    return ca.astype(F32) * _sigmoid(cb.astype(F32))


def _conv_specs(ts, tiles_per_seq):
    per_tile = ts // CONV_PAD

    def tile(col):
        return lambda b, t: (b * tiles_per_seq + t, col)

    def before(col):
        return lambda b, t: (jnp.maximum((b * tiles_per_seq + t) * per_tile - 1, 0), col)

    return [pl.BlockSpec((ts, D_MODEL), tile(COLB_CA)), pl.BlockSpec((ts, D_MODEL), tile(COLB_CB)),
            pl.BlockSpec((CONV_PAD, D_MODEL), before(COLB_CA)), pl.BlockSpec((CONV_PAD, D_MODEL), before(COLB_CB))]


SUBLANES = 8


def _fill_upad(upad, ca_ref, cb_ref, cah_ref, cbh_ref, t):
    halo = _conv_u(cah_ref[...], cbh_ref[...])
    upad[0, 0:CONV_PAD, :] = jnp.where(t > 0, halo, jnp.zeros_like(halo))
    upad[0, CONV_PAD:, :] = _conv_u(ca_ref[...], cb_ref[...])


def _fill_shifted(pad):
    rows = pad.shape[1] - SUBLANES
    for b in range(1, SUBLANES):
        pad[b, 0:rows, :] = pad[0, b:b + rows, :]


def _shifted_rows(pad, offset, rows):
    b = offset % SUBLANES
    return pad[b, offset - b:offset - b + rows, :]


def _layernorm_stats(y):
    mu = jnp.mean(y, axis=-1, keepdims=True)
    yc = y - mu
    rstd = lax.rsqrt(jnp.mean(yc * yc, axis=-1, keepdims=True) + EPS)
    return yc * rstd, rstd


def _conv_fwd(proj, w_dw, b_dw, ln_g, ln_b, batch, seq, comm=None):
    T = proj.shape[0]
    ts = _tile(seq, 256)
    nt = seq // ts
    shift = CONV_PAD - (CONV_WIDTH - 1)

    def body(ca_ref, cb_ref, cah_ref, cbh_ref, w_ref, b_ref, g_ref, beta_ref, y_ref, z_ref, upad):
        _fill_upad(upad, ca_ref, cb_ref, cah_ref, cbh_ref, pl.program_id(1))
        _fill_shifted(upad)
        y = jnp.zeros((ts, D_MODEL), F32) + b_ref[...]
        for k in range(CONV_WIDTH):
            y = y + w_ref[k:k + 1, :] * _shifted_rows(upad, shift + k, ts)
        y_ref[...] = y
        lnh, _ = _layernorm_stats(y)
        ln = lnh * g_ref[...] + beta_ref[...]
        z_ref[...] = (ln * _sigmoid(ln)).astype(BF16)

    vec = pl.BlockSpec((1, D_MODEL), lambda b, t: (0, 0))
    row = pl.BlockSpec((ts, D_MODEL), lambda b, t: (b * nt + t, 0))
    return _call(
        body, name="conv_fwd", grid=(batch, nt),
        in_specs=_conv_specs(ts, nt) + [pl.BlockSpec((CONV_PAD, D_MODEL), lambda b, t: (0, 0)), vec, vec, vec],
        out_specs=[row, row],
        out_shape=[jax.ShapeDtypeStruct((T, D_MODEL), F32), jax.ShapeDtypeStruct((T, D_MODEL), BF16)],
        scratch_shapes=[pltpu.VMEM((SUBLANES, ts + CONV_PAD, D_MODEL), F32)],
        operands=(proj, proj, proj, proj, w_dw, b_dw, ln_g, ln_b), comm=comm)


def _merge(o, z, proj, w_ao, w_co, w_out, x, gate, seq):
    T, D = x.shape
    tm = _tile(seq, 512)
    nb = seq // tm

    def body(o_ref, z_ref, ga_ref, gc_ref, wao_ref, wco_ref, wout_ref, x_ref, gate_ref,
             ya_ref, yc_ref, mg_ref, mo_ref, xo_ref):
        ya = _dot_nn(o_ref[...], wao_ref[...])
        yc = _dot_nn(z_ref[...], wco_ref[...])
        ya_ref[...] = ya.astype(BF16)
        yc_ref[...] = yc.astype(BF16)
        merged = (_sigmoid(ga_ref[...].astype(F32)) * ya + _sigmoid(gc_ref[...].astype(F32)) * yc).astype(BF16)
        mg_ref[...] = merged
        mo = _dot_nn(merged, wout_ref[...])
        mo_ref[...] = mo.astype(BF16)
        xo_ref[...] = x_ref[...] + gate_ref[...] * mo

    row = pl.BlockSpec((tm, D), lambda i: (i, 0))
    mat = pl.BlockSpec((D, D), lambda i: (0, 0))
    act = jax.ShapeDtypeStruct((T, D), BF16)
    return pl.pallas_call(
        body, name="mix_merge", grid=(T // tm,),
        in_specs=[row, row, pl.BlockSpec((tm, D), lambda i: (i, COLB_GA)), pl.BlockSpec((tm, D), lambda i: (i, COLB_GC)),
                  mat, mat, mat, row, pl.BlockSpec((None, 1, D), lambda i: (i // nb, 0, 0))],
        out_specs=[row, row, row, row, row],
        out_shape=[act, act, act, act, jax.ShapeDtypeStruct((T, D), F32)],
        compiler_params=_params(1),
    )(o, z, proj, proj, w_ao, w_co, w_out, x, gate)


def _final_loss(x, gf, target):
    T, D = x.shape
    tm = _tile(T, 512)

    def body(x_ref, gf_ref, t_ref, dx_ref, lp_ref, dgf_ref):
        first = pl.program_id(0) == 0
        xv = x_ref[...]
        gfv = gf_ref[...]
        r = lax.rsqrt(jnp.mean(xv * xv, axis=-1, keepdims=True) + EPS)
        xh = xv * r
        err = xh * gfv - t_ref[...]
        _accumulate(lp_ref, first, jnp.sum(err * err, axis=0, keepdims=True))
        dy = err * (1.0 / D)
        _accumulate(dgf_ref, first, jnp.sum(dy * xh, axis=0, keepdims=True))
        dxh = dy * gfv
        dx_ref[...] = r * (dxh - xh * jnp.mean(dxh * xh, axis=-1, keepdims=True))

    row = pl.BlockSpec((tm, D), lambda i: (i, 0))
    vec = pl.BlockSpec((1, D), lambda i: (0, 0))
    return pl.pallas_call(
        body, name="final_loss", grid=(T // tm,),
        in_specs=[row, vec, row], out_specs=[row, vec, vec],
        out_shape=[jax.ShapeDtypeStruct((T, D), F32), jax.ShapeDtypeStruct((1, D), F32),
                   jax.ShapeDtypeStruct((1, D), F32)],
        compiler_params=_params(1),
    )(x, gf, target)


def _merge_bwd(dxo, mo, gate, proj, ya, yc, w_out, w_ao, w_co, seq):
    T, D = dxo.shape
    B = T // seq
    tm = _tile(seq, 512)
    nb = seq // tm

    def body(dxo_ref, mo_ref, gate_ref, ga_ref, gc_ref, ya_ref, yc_ref, wout_ref, wao_ref, wco_ref,
             dmo_ref, dya_ref, dyc_ref, dga_ref, dgc_ref, do_ref, dz_ref, dgate_ref):
        dxo_v = dxo_ref[...]
        dmo = (gate_ref[...] * dxo_v).astype(BF16)
        dmo_ref[...] = dmo
        _accumulate(dgate_ref, pl.program_id(0) % nb == 0,
                    jnp.sum(mo_ref[...].astype(F32) * dxo_v, axis=0, keepdims=True))
        dm = _dot_nt(dmo, wout_ref[...])
        sa = _sigmoid(ga_ref[...].astype(F32))
        sc = _sigmoid(gc_ref[...].astype(F32))
        dya = (sa * dm).astype(BF16)
        dyc = (sc * dm).astype(BF16)
        dya_ref[...] = dya
        dyc_ref[...] = dyc
        dga_ref[...] = (dm * ya_ref[...].astype(F32) * (sa * (1.0 - sa))).astype(BF16)
        dgc_ref[...] = (dm * yc_ref[...].astype(F32) * (sc * (1.0 - sc))).astype(BF16)
        do_ref[...] = _dot_nt(dya, wao_ref[...]).astype(BF16)
        dz_ref[...] = _dot_nt(dyc, wco_ref[...]).astype(BF16)

    row = pl.BlockSpec((tm, D), lambda i: (i, 0))
    mat = pl.BlockSpec((D, D), lambda i: (0, 0))
    per_b = pl.BlockSpec((None, 1, D), lambda i: (i // nb, 0, 0))
    act = jax.ShapeDtypeStruct((T, D), BF16)
    return pl.pallas_call(
        body, name="mix_merge_bwd", grid=(T // tm,),
        in_specs=[row, row, per_b, pl.BlockSpec((tm, D), lambda i: (i, COLB_GA)),
                  pl.BlockSpec((tm, D), lambda i: (i, COLB_GC)), row, row, mat, mat, mat],
        out_specs=[row] * 7 + [per_b],
        out_shape=[act] * 7 + [jax.ShapeDtypeStruct((B, 1, D), F32)],
        compiler_params=_params(1),
    )(dxo, mo, gate, proj, proj, ya, yc, w_out, w_ao, w_co)


def _attn_bwd(proj, sinks, o, do, lse, batch, seq, comm=None):
    T = proj.shape[0]
    nblk = seq // BLOCK
    n_steps = batch * nblk

    def body(sink_ref, q_ref, kp_ref, ko_ref, vp_ref, vo_ref, o_ref, do_ref, lse_ref,
             dq_ref, dkp_ref, dko_ref, dvp_ref, dvo_ref, dsink_ref):
        lo = lax.broadcasted_iota(jnp.int32, (1, 2 * HEAD_DIM), 1) < HEAD_DIM
        sink_lane = lax.broadcasted_iota(jnp.int32, (1, 2 * HEAD_DIM), 1)
        valid = _band_valid(pl.program_id(1) > 0)
        k_ops = _band_operands(kp_ref, ko_ref, lo)
        v_ops = _band_operands(vp_ref, vo_ref, lo)
        dsink = jnp.zeros((1, 2 * HEAD_DIM), F32)
        col = jnp.zeros((BLOCK, 1), F32)

        def fold(both):
            return (jnp.where(lo, both[:BAND], 0.0)
                    + pltpu.roll(jnp.where(lo, 0.0, both[BAND:]), HEAD_DIM, 1))

        dk_heads, dv_heads = [], []
        for kvh in range(N_KV_HEADS):
            q4 = _stack_pairs(q_ref, kvh)
            do4 = _stack_pairs(do_ref, kvh)
            dd = do4.astype(F32) * _stack_pairs(o_ref, kvh).astype(F32)
            s_all = _dot_nt(q4, k_ops[kvh]) * ATTN_SCALE
            dp_all = _dot_nt(do4, v_ops[kvh])
            ds_sides, p_sides = [], []
            for side in range(2):
                heads = [2 * (kvh * PAIRS_PER_KV + pp) + side for pp in range(PAIRS_PER_KV)]
                mine = lo if side == 0 else jnp.logical_not(lo)
                cols = slice(side * BAND, (side + 1) * BAND)
                sink = _rows_per_pair([col + sink_ref[0, h] for h in heads])
                lse = _rows_per_pair([lse_ref[:, h:h + 1] for h in heads])
                delta = jnp.sum(jnp.where(mine, dd, 0.0), axis=-1, keepdims=True)
                p = jnp.where(valid, jnp.exp(jnp.where(valid, s_all[:, cols], MASK_VALUE) - lse), 0.0)
                ds_sides.append((p * (dp_all[:, cols] - delta) * ATTN_SCALE).astype(BF16))
                p_sides.append(p.astype(BF16))
                sink_part = jnp.exp(sink - lse) * delta
                for pp, h in enumerate(heads):
                    dsink = dsink + jnp.where(sink_lane == h, -jnp.sum(sink_part[pp * BLOCK:(pp + 1) * BLOCK]), 0.0)
            ds_all = jnp.concatenate(ds_sides, axis=1)
            dq4 = _dot_nn(ds_all, k_ops[kvh])
            for pp in range(PAIRS_PER_KV):
                dq_ref[:, _pair_lanes(kvh, pp)] = dq4[pp * BLOCK:(pp + 1) * BLOCK].astype(BF16)
            dk_heads.append(fold(_dot_tn(ds_all, q4)))
            dv_heads.append(fold(_dot_tn(jnp.concatenate(p_sides, axis=1), do4)))
        dk = dk_heads[0] + pltpu.roll(dk_heads[1], HEAD_DIM, 1)
        dv = dv_heads[0] + pltpu.roll(dv_heads[1], HEAD_DIM, 1)
        dkp_ref[...] = dk[:BLOCK]
        dko_ref[...] = dk[BLOCK:]
        dvp_ref[...] = dv[:BLOCK]
        dvo_ref[...] = dv[BLOCK:]
        dsink_ref[...] = dsink

    def own(b, n):
        return (b * nblk + n, 0)

    row = pl.BlockSpec((BLOCK, D_MODEL), own)
    kv = pl.BlockSpec((BLOCK, 2 * HEAD_DIM), own)
    kv_shape = jax.ShapeDtypeStruct((T, 2 * HEAD_DIM), F32)
    return _call(
        body, name="attn_bwd", grid=(batch, nblk),
        in_specs=[SMEM_SPEC] + _attn_specs(nblk) + [row, row, pl.BlockSpec((BLOCK, N_Q_HEADS), own)],
        out_specs=[row, kv, kv, kv, kv, pl.BlockSpec((None, 1, 2 * HEAD_DIM), lambda b, n: (b * nblk + n, 0, 0))],
        out_shape=[jax.ShapeDtypeStruct((T, D_MODEL), BF16), kv_shape, kv_shape, kv_shape, kv_shape,
                   jax.ShapeDtypeStruct((n_steps, 1, 2 * HEAD_DIM), F32)],
        operands=(sinks, proj, proj, proj, proj, proj, o, do, lse), comm=comm)


def _conv_bwd(proj, dz, ydw, w_dw, ln_g, ln_b, batch, seq):
    T = proj.shape[0]
    ts = _tile(seq, 256)
    nt = seq // ts
    per_tile = ts // CONV_PAD
    shift = CONV_PAD - (CONV_WIDTH - 1)

    def body(ca_ref, cb_ref, cah_ref, cbh_ref, dz_ref, dzn_ref, y_ref, yn_ref, w_ref, g_ref, beta_ref,
             dca_ref, dcb_ref, dw_ref, db_ref, dg_ref, dbeta_ref, upad, dypad):
        t = pl.program_id(1)
        first = (pl.program_id(0) == 0) & (t == 0)
        gv = g_ref[...]

        def ln_bwd(dzv, yv):
            lnh, rstd = _layernorm_stats(yv)
            ln = lnh * gv + beta_ref[...]
            sg = _sigmoid(ln)
            dln = dzv.astype(F32) * (sg * (1.0 + ln * (1.0 - sg)))
            dyh = dln * gv
            dy = rstd * (dyh - jnp.mean(dyh, axis=-1, keepdims=True)
                         - lnh * jnp.mean(dyh * lnh, axis=-1, keepdims=True))
            return dy, dln, lnh

        dy, dln, lnh = ln_bwd(dz_ref[...], y_ref[...])
        dy_next, _, _ = ln_bwd(dzn_ref[...], yn_ref[...])
        dypad[0, 0:ts, :] = dy
        dypad[0, ts:, :] = jnp.where(t < nt - 1, dy_next, jnp.zeros_like(dy_next))
        _fill_shifted(dypad)
        _fill_upad(upad, ca_ref, cb_ref, cah_ref, cbh_ref, t)
        _fill_shifted(upad)

        _accumulate(dg_ref, first, jnp.sum(dln * lnh, axis=0, keepdims=True))
        _accumulate(dbeta_ref, first, jnp.sum(dln, axis=0, keepdims=True))
        _accumulate(db_ref, first, jnp.sum(dy, axis=0, keepdims=True))

        @pl.when(first)
        def _():
            dw_ref[...] = jnp.zeros_like(dw_ref)

        du = jnp.zeros((ts, D_MODEL), F32)
        for k in range(CONV_WIDTH):
            du = du + w_ref[k:k + 1, :] * _shifted_rows(dypad, CONV_WIDTH - 1 - k, ts)
            dw_ref[k:k + 1, :] += jnp.sum(dy * _shifted_rows(upad, shift + k, ts), axis=0, keepdims=True)
        cav = ca_ref[...].astype(F32)
        sb = _sigmoid(cb_ref[...].astype(F32))
        dca_ref[...] = (du * sb).astype(BF16)
        dcb_ref[...] = (du * cav * (sb * (1.0 - sb))).astype(BF16)

    def tile(b, t):
        return (b * nt + t, 0)

    def after(b, t):
        return (jnp.minimum((b * nt + t + 1) * per_tile, T // CONV_PAD - 1), 0)

    row = pl.BlockSpec((ts, D_MODEL), tile)
    halo = pl.BlockSpec((CONV_PAD, D_MODEL), after)
    vec = pl.BlockSpec((1, D_MODEL), lambda b, t: (0, 0))
    wspec = pl.BlockSpec((CONV_PAD, D_MODEL), lambda b, t: (0, 0))
    act = jax.ShapeDtypeStruct((T, D_MODEL), BF16)
    vec_shape = jax.ShapeDtypeStruct((1, D_MODEL), F32)
    return pl.pallas_call(
        body, name="conv_bwd", grid=(batch, nt),
        in_specs=_conv_specs(ts, nt) + [row, halo, row, halo, wspec, vec, vec],
        out_specs=[row, row, wspec, vec, vec, vec],
        out_shape=[act, act, jax.ShapeDtypeStruct((CONV_PAD, D_MODEL), F32), vec_shape, vec_shape, vec_shape],
        scratch_shapes=[pltpu.VMEM((SUBLANES, ts + CONV_PAD, D_MODEL), F32)] * 2,
        compiler_params=_params(2),
    )(proj, proj, proj, proj, dz, dz, ydw, ydw, w_dw, ln_g, ln_b)


def _in_proj_bwd(dproj, w_in_g, x, gn, sc, dxo, seq):
    T, D = x.shape
    J, W, _ = w_in_g.shape
    B = T // seq
    tm = _tile(seq, 512)
    nb = seq // tm

    def body(dp_ref, w_ref, x_ref, gn_ref, sc_ref, dxo_ref, dx_ref, dsc_ref, dsh_ref, dgn_ref, acc):
        i = pl.program_id(0)
        j = pl.program_id(1)

        @pl.when(j == 0)
        def _():
            acc[...] = jnp.zeros_like(acc)

        acc[...] += _dot_nn(dp_ref[...], w_ref[...])

        @pl.when(j == J - 1)
        def _():
            _norm_mod_bwd(acc[...], x_ref[...], gn_ref[...], sc_ref[...], dxo_ref[...],
                          i % nb == 0, i == 0, dx_ref, dsc_ref, dsh_ref, dgn_ref)

    row = pl.BlockSpec((tm, D), lambda i, j: (i, 0))
    vec = pl.BlockSpec((1, D), lambda i, j: (0, 0))
    per_b = pl.BlockSpec((None, 1, D), lambda i, j: (i // nb, 0, 0))
    per_b_shape = jax.ShapeDtypeStruct((B, 1, D), F32)
    return pl.pallas_call(
        body, name="mix_in_proj_bwd", grid=(T // tm, J),
        in_specs=[pl.BlockSpec((None, tm, W), lambda i, j: (j, i, 0)),
                  pl.BlockSpec((None, W, D), lambda i, j: (j, 0, 0)), row, vec, per_b, row],
        out_specs=[row, per_b, per_b, vec],
        out_shape=[jax.ShapeDtypeStruct((T, D), F32), per_b_shape, per_b_shape, jax.ShapeDtypeStruct((1, D), F32)],
        scratch_shapes=[pltpu.VMEM((tm, D), F32)],
        compiler_params=_params(2),
    )(dproj, w_in_g, x, gn, sc, dxo)


def _ada_fwd(c_all, w_ada, b_cols):
    nbatch, D = c_all.shape
    N = w_ada.shape[1]
    tn = _tile(N, 768)

    def body(c_ref, w_ref, b_ref, o_ref):
        cv = c_ref[...]
        act = (cv * _sigmoid(cv)).astype(BF16)
        o_ref[...] = _dot_nn(act, w_ref[...].astype(BF16)) + b_ref[...]

    return pl.pallas_call(
        body, name="ada_fwd", grid=(N // tn,),
        in_specs=[pl.BlockSpec((nbatch, D), lambda j: (0, 0)), pl.BlockSpec((D, tn), lambda j: (0, j)),
                  pl.BlockSpec((1, tn), lambda j: (0, j))],
        out_specs=pl.BlockSpec((nbatch, tn), lambda j: (0, j)),
        out_shape=jax.ShapeDtypeStruct((nbatch, N), F32),
        compiler_params=_params(1),
    )(c_all, w_ada, b_cols)


def _adamw(w, g, m, v):
    m = ADAM_B1 * m + (1.0 - ADAM_B1) * g
    v = ADAM_B2 * v + (1.0 - ADAM_B2) * (g * g)
    m_hat = m / (1.0 - ADAM_B1 ** ADAM_STEP)
    v_hat = v / (1.0 - ADAM_B2 ** ADAM_STEP)
    delta = -ADAM_LR * (m_hat / (jnp.sqrt(v_hat) + ADAM_EPS) + ADAM_WD * w)
    return delta, m, v


def _adam_call(w, g, m, v, name):
    R, C = w.shape
    tr = _row_tile(R, 512)

    def body(w_ref, g_ref, m_ref, v_ref, d_ref, mo_ref, vo_ref):
        d, mn, vn = _adamw(w_ref[...], g_ref[...], m_ref[...], v_ref[...])
        d_ref[...] = d
        mo_ref[...] = mn
        vo_ref[...] = vn

    blk = pl.BlockSpec((tr, C), lambda i: (i, 0))
    shape = jax.ShapeDtypeStruct((R, C), F32)
    return pl.pallas_call(
        body, name=name, grid=(R // tr,), in_specs=[blk] * 4, out_specs=[blk] * 3, out_shape=[shape] * 3,
        compiler_params=_params(1),
    )(w, g, m, v)


def _ada_adam(c_act_t, dmod_cols, w, m, v, comm):
    R, C = w.shape
    nbatch = c_act_t.shape[1]
    tr = _tile(R, 128)

    def body(ct_ref, dm_ref, w_ref, m_ref, v_ref, g_ref, d_ref, mo_ref, vo_ref):
        cv = ct_ref[...]
        g = _dot_nn((cv * _sigmoid(cv)).astype(BF16), dm_ref[...].astype(BF16))
        g_ref[...] = g
        d, mn, vn = _adamw(w_ref[...], g, m_ref[...], v_ref[...])
        d_ref[...] = d
        mo_ref[...] = mn
        vo_ref[...] = vn

    blk = pl.BlockSpec((tr, C), lambda i: (i, 0))
    shape = jax.ShapeDtypeStruct((R, C), F32)
    return _call(
        body, name="ada_adam", grid=(R // tr,),
        in_specs=[pl.BlockSpec((tr, nbatch), lambda i: (i, 0)), pl.BlockSpec((nbatch, C), lambda i: (0, 0)),
                  blk, blk, blk],
        out_specs=[blk] * 4, out_shape=[shape] * 4,
        operands=(c_act_t, dmod_cols, w, m, v), comm=comm)


def _small_adam(gathered, w, m, v, rows_b0, rows_b1, rows_vec):
    _, P, D = gathered.shape
    R = w.shape[0]

    def body(ga_ref, w_ref, m_ref, v_ref, sum_ref, g_ref, d_ref, mo_ref, vo_ref):
        total = ga_ref[0]
        for dev in range(1, N_DEV):
            total = total + ga_ref[dev]
        sum_ref[...] = total
        g_ref[...] = jnp.zeros_like(g_ref)
        g_ref[0:N_MOD, :] = (sum_ref[rows_b0:rows_b0 + N_MOD, :] + sum_ref[rows_b1:rows_b1 + N_MOD, :])
        g_ref[N_MOD:N_MOD + 8, :] = sum_ref[rows_vec:rows_vec + 8, :]
        d, mn, vn = _adamw(w_ref[...], g_ref[...], m_ref[...], v_ref[...])
        d_ref[...] = d
        mo_ref[...] = mn
        vo_ref[...] = vn

    shape = jax.ShapeDtypeStruct((R, D), F32)
    return pl.pallas_call(
        body, name="small_adam",
        in_specs=[VMEM_SPEC] * 4, out_specs=[VMEM_SPEC] * 5,
        out_shape=[jax.ShapeDtypeStruct((P, D), F32), shape, shape, shape, shape],
        compiler_params=pltpu.CompilerParams(vmem_limit_bytes=VMEM_LIMIT),
    )(gathered, w, m, v)


def _gather8(v, name):
    A, W = v.shape
    flips = [(fx, fy, fc) for fx in (0, 1) for fy in (0, 1) for fc in (0, 1) if (fx, fy, fc) != (0, 0, 0)]

    def body(v_ref, out_ref, send_sems, recv_sems, local_sem):
        x, y, c = _position()
        me = 4 * x + 2 * y + c
        mine = pltpu.make_async_copy(v_ref, out_ref.at[me], local_sem)
        mine.start()

        def copy(k, block, to):
            return pltpu.make_async_remote_copy(src_ref=v_ref, dst_ref=out_ref.at[block], send_sem=send_sems.at[k],
                                                recv_sem=recv_sems.at[k], device_id=to, device_id_type=MESH)

        peers = [(_flip(x, fx), _flip(y, fy), _flip(c, fc)) for fx, fy, fc in flips]
        sends = [copy(k, me, peer) for k, peer in enumerate(peers)]
        for cp in sends:
            cp.start()
        for k, (px, py, pc) in enumerate(peers):
            copy(k, 4 * px + 2 * py + pc, (px, py, pc)).wait_recv()
        for cp in sends:
            cp.wait_send()
        mine.wait()

    return pl.pallas_call(
        body, name=name, in_specs=[VMEM_SPEC], out_specs=VMEM_SPEC,
        out_shape=jax.ShapeDtypeStruct((N_DEV, A, W), v.dtype),
        scratch_shapes=[pltpu.SemaphoreType.DMA((N_DEV - 1,)), pltpu.SemaphoreType.DMA((N_DEV - 1,)),
                        pltpu.SemaphoreType.DMA],
    )(v)


def _mod_exchange(part):
    _, A, W = part.shape

    def body(p_ref, out_ref, send_sems, recv_sems, local_sem):
        x, y, c = _position()
        me = 4 * x + 2 * y + c
        chip = 2 * x + y
        mine = pltpu.make_async_copy(p_ref.at[me], out_ref.at[chip], local_sem)
        mine.start()
        peers = [(_flip(x, fx), _flip(y, fy)) for fx, fy in CHIP_FLIPS]
        sends = []
        for k, (px, py) in enumerate(peers):
            sends.append(pltpu.make_async_remote_copy(
                src_ref=p_ref.at[4 * px + 2 * py + c], dst_ref=out_ref.at[chip], send_sem=send_sems.at[k],
                recv_sem=recv_sems.at[k], device_id=(px, py, c), device_id_type=MESH))
        for cp in sends:
            cp.start()
        for k, (px, py) in enumerate(peers):
            pltpu.make_async_remote_copy(
                src_ref=p_ref.at[me], dst_ref=out_ref.at[2 * px + py], send_sem=send_sems.at[k],
                recv_sem=recv_sems.at[k], device_id=(px, py, c), device_id_type=MESH).wait_recv()
        for cp in sends:
            cp.wait_send()
        mine.wait()

    return pl.pallas_call(
        body, name="mod_exchange", in_specs=[VMEM_SPEC], out_specs=VMEM_SPEC,
        out_shape=jax.ShapeDtypeStruct((N_CHIP, A, W), part.dtype),
        scratch_shapes=[pltpu.SemaphoreType.DMA((3,)), pltpu.SemaphoreType.DMA((3,)), pltpu.SemaphoreType.DMA],
    )(part)


def _cast_slot(w, chip_idx, name):
    R, C = w.shape
    tr = _row_tile(R, 512)

    def body(chip_ref, w_ref, o_ref):
        o_ref[...] = w_ref[...].astype(BF16)

    return pl.pallas_call(
        body, name=name,
        grid_spec=pltpu.PrefetchScalarGridSpec(
            num_scalar_prefetch=1, grid=(R // tr,),
            in_specs=[pl.BlockSpec((tr, C), lambda i, chip_ref: (i, 0))],
            out_specs=pl.BlockSpec((None, tr, C), lambda i, chip_ref: (chip_ref[0], i, 0))),
        out_shape=jax.ShapeDtypeStruct((N_CHIP, R, C), BF16),
        compiler_params=_params(1),
    )(chip_idx, w)


def _sibling_swap_halves(grads16, name):
    n = len(grads16)

    def body(*refs):
        g_refs, out_refs = refs[:n], refs[n:2 * n]
        send_sems, recv_sems = refs[2 * n:]
        x, y, c = _position()
        copies = [pltpu.make_async_remote_copy(
            src_ref=g_refs[i].at[1 - c], dst_ref=out_refs[i], send_sem=send_sems.at[i], recv_sem=recv_sems.at[i],
            device_id=(x, y, 1 - c), device_id_type=MESH) for i in range(n)]
        for cp in copies:
            cp.start()
        for cp in copies:
            cp.wait()

    return pl.pallas_call(
        body, name=name, in_specs=[ANY] * n, out_specs=[ANY] * n,
        out_shape=[jax.ShapeDtypeStruct(g.shape[1:], g.dtype) for g in grads16],
        scratch_shapes=[pltpu.SemaphoreType.DMA((n,)), pltpu.SemaphoreType.DMA((n,))],
    )(*grads16)


def _pair_sum(g32, recv, core, name):
    _, J, r, C = g32.shape

    def body(core_ref, g_ref, r_ref, o_ref):
        o_ref[...] = (g_ref[...] + r_ref[...].astype(F32)).astype(BF16)

    return pl.pallas_call(
        body, name=name,
        grid_spec=pltpu.PrefetchScalarGridSpec(
            num_scalar_prefetch=1, grid=(J,),
            in_specs=[pl.BlockSpec((None, None, r, C), lambda j, core_ref: (core_ref[0], j, 0, 0)),
                      pl.BlockSpec((None, r, C), lambda j, core_ref: (j, 0, 0))],
            out_specs=pl.BlockSpec((None, r, C), lambda j, core_ref: (j, 0, 0))),
        out_shape=jax.ShapeDtypeStruct((J, r, C), BF16),
        compiler_params=_params(1),
    )(core, g32, recv)


def _chip_sum(g32, recv_sib, recv_chips, core_chip, name):
    _, J, r, C = g32.shape

    def body(idx_ref, g_ref, s_ref, o_ref_in, o_ref):
        total = g_ref[...] + s_ref[...].astype(F32)
        for k in range(3):
            total = total + o_ref_in[k].astype(F32)
        o_ref[...] = total

    return pl.pallas_call(
        body, name=name,
        grid_spec=pltpu.PrefetchScalarGridSpec(
            num_scalar_prefetch=1, grid=(1,),
            in_specs=[pl.BlockSpec((None, None, r, C), lambda i, idx: (idx[0], idx[1], 0, 0)),
                      pl.BlockSpec((None, r, C), lambda i, idx: (idx[1], 0, 0)),
                      pl.BlockSpec((3, r, C), lambda i, idx: (0, 0, 0))],
            out_specs=pl.BlockSpec((None, r, C), lambda i, idx: (idx[0], 0, 0))),
        out_shape=jax.ShapeDtypeStruct((2, r, C), F32),
        compiler_params=_params(1),
    )(core_chip, g32, recv_sib, recv_chips)


def _sibling_join_halves(halves, name):
    n = len(halves)

    def body(*refs):
        h_refs, out_refs = refs[:n], refs[n:2 * n]
        send_sems, recv_sems = refs[2 * n:]
        x, y, c = _position()

        def copy(i, which):
            return pltpu.make_async_remote_copy(
                src_ref=h_refs[i].at[which], dst_ref=out_refs[i].at[which], send_sem=send_sems.at[i],
                recv_sem=recv_sems.at[i], device_id=(x, y, 1 - c), device_id_type=MESH)

        for i in range(n):
            copy(i, c).start()
        for i in range(n):
            copy(i, 1 - c).wait_recv()
        for i in range(n):
            copy(i, c).wait_send()

    return pl.pallas_call(
        body, name=name, in_specs=[ANY] * n, out_specs=[ANY] * n,
        out_shape=[jax.ShapeDtypeStruct(h.shape, h.dtype) for h in halves],
        scratch_shapes=[pltpu.SemaphoreType.DMA((n,)), pltpu.SemaphoreType.DMA((n,))],
        input_output_aliases={i: i for i in range(n)},
    )(*halves)


BIG_WEIGHTS = ("ffn1_w_gate", "ffn1_w_up", "ffn1_w_down", "w_in", "w_attn_o", "w_conv_o", "w_out",
               "ffn2_w_gate", "ffn2_w_up", "ffn2_w_down")
VECTORS = ("norm_ffn1_g", "norm_mix_g", "conv_b_dw", "conv_ln_g", "conv_ln_b", "norm_ffn2_g", "final_norm_g")
ROW_DMOD0, ROW_DMOD1, ROW_VEC, ROW_SINK, ROW_CONVW, SMALL_ROWS = 0, 16, 33, 40, 41, 72


def _reduce_begin(grads, names, core_idx, tag):
    from_sibling = _sibling_swap_halves([grads[n][1] for n in names], "grad_swap_halves_" + tag)
    pair_sums = [_pair_sum(grads[n][0], r, core_idx, "pair_sum_" + n) for n, r in zip(names, from_sibling)]
    return from_sibling, pair_sums


def _reduce_end(grads, names, from_sibling, from_chips, core_chip, tag):
    halves = [_chip_sum(grads[n][0], rs, rc, core_chip, "chip_sum_" + n)
              for n, rs, rc in zip(names, from_sibling, from_chips)]
    return dict(zip(names, _sibling_join_halves(halves, "grad_join_halves_" + tag)))


FFN1_WEIGHTS = ("ffn1_w_gate", "ffn1_w_up", "ffn1_w_down")
FFN2_WEIGHTS = ("ffn2_w_gate", "ffn2_w_up", "ffn2_w_down")
MIX_WEIGHTS = ("w_in", "w_attn_o", "w_conv_o", "w_out")
COL_SHARDED = ("ffn1_w_gate", "ffn1_w_up", "ffn2_w_gate", "ffn2_w_up", "w_in")


def _local_grads(x, target, mod, slots, small, seq, core_idx, core_chip):
    T, D = x.shape
    B = T // seq
    mods = [mod[:, k][:, None, :] for k in range(N_MOD)]
    sh1, sc1, g1, sh2, sc2, g2, sh3, sc3, g3 = mods
    w = dict(zip(FFN1_WEIGHTS, _run_comm(_GatherComm([slots[n] for n in FFN1_WEIGHTS]), "gather_ffn1")))

    (h1, a1, u1, f1, x1), (w["w_in"],) = _ffn_fwd(
        x, small["norm_ffn1_g"], sc1, sh1, g1, w["ffn1_w_gate"], w["ffn1_w_up"], w["ffn1_w_down"], seq, "ffn1_fwd",
        comm=_GatherComm([slots["w_in"]]))
    w_in_full = w["w_in"].reshape(IN_WIDTH, D)
    q_end, v_end = D, D + 4 * HEAD_DIM
    w_in_cols = jnp.concatenate([w_in_full[:q_end], w_in_full[v_end:], w_in_full[q_end:v_end]], axis=0)
    (h2, proj), outs = _in_proj(x1, small["norm_mix_g"], sc2, sh2, w_in_cols, seq,
                                comm=_GatherComm([slots[n] for n in ("w_attn_o", "w_conv_o", "w_out")]))
    w_ao, w_co, w_o = [t.reshape(D, D) for t in outs]
    (o, lse), (w["ffn2_w_gate"], w["ffn2_w_up"]) = _attn_fwd(
        proj, small["attn_sinks"], B, seq, comm=_GatherComm([slots["ffn2_w_gate"], slots["ffn2_w_up"]]))
    (ydw, z), (w["ffn2_w_down"],) = _conv_fwd(
        proj, small["conv_w_dw"], small["conv_b_dw"], small["conv_ln_g"], small["conv_ln_b"], B, seq,
        comm=_GatherComm([slots["ffn2_w_down"]]))
    ya, yc, merged, mo, x2 = _merge(o, z, proj, w_ao, w_co, w_o, x1, g2, seq)
    (h3, a3, u3, f3, x3), _ = _ffn_fwd(x2, small["norm_ffn2_g"], sc3, sh3, g3, w["ffn2_w_gate"], w["ffn2_w_up"],
                                       w["ffn2_w_down"], seq, "ffn2_fwd")
    dx3, loss_parts, d_final_g = _final_loss(x3, small["final_norm_g"], target)

    grads = {}

    def ffn_backward(prefix, dxo, xin, h, a, u, f, gn, sc, gate, comm=None):
        (da, du, s, df, dx, dgate, dsc, dsh, dgn), comm_out = _ffn_bwd(
            dxo, xin, f, a, u, gn, sc, gate, w[prefix + "_w_gate"], w[prefix + "_w_up"], w[prefix + "_w_down"],
            seq, prefix + "_bwd", comm=comm)
        grads[prefix + "_w_gate"] = _wgrad(da, _spec_chip_major(FF_SHARD), h, _spec_rows(D), FF_SHARD, D, T,
                                           prefix + "_dw_gate")
        grads[prefix + "_w_up"] = _wgrad(du, _spec_chip_major(FF_SHARD), h, _spec_rows(D), FF_SHARD, D, T,
                                         prefix + "_dw_up")
        def down_grad(comm=None):
            return _wgrad(s, _spec_chip_major(FF_SHARD), df, _spec_rows(D), FF_SHARD, D, T, prefix + "_dw_down",
                          comm=comm)

        return (dx, dgate, dsc, dsh, dgn), comm_out, down_grad

    (dx2, dg3, dsc3, dsh3, d_gn3), _, down_grad = ffn_backward(
        "ffn2", dx3, x2, h3, a3, u3, f3, small["norm_ffn2_g"], sc3, g3)
    grads["ffn2_w_down"] = down_grad()
    sib2, pairs2 = _reduce_begin(grads, FFN2_WEIGHTS, core_idx, "ffn2")

    dmo, dya, dyc, dga, dgc, do, dz, dg2 = _merge_bwd(dx2, mo, g2, proj, ya, yc, w_o, w_ao, w_co, seq)
    shard = D // N_CHIP
    grads["w_out"] = _wgrad(merged, _spec_col_block(shard), dmo, _spec_rows(D), shard, D, T, "dw_out")
    grads["w_attn_o"] = _wgrad(o, _spec_col_block(shard), dya, _spec_rows(D), shard, D, T, "dw_attn_o")
    grads["w_conv_o"] = _wgrad(z, _spec_col_block(shard), dyc, _spec_rows(D), shard, D, T, "dw_conv_o")
    (dq, dkp, dko, dvp, dvo, dsink_steps), chips2 = _attn_bwd(proj, small["attn_sinks"], o, do, lse, B, seq,
                                                              comm=_ExchangeComm(pairs2))
    reduced = _reduce_end(grads, FFN2_WEIGHTS, sib2, chips2, core_chip, "ffn2")
    dca, dcb, d_conv_w, d_conv_b, d_ln_g, d_ln_b = _conv_bwd(proj, dz, ydw, small["conv_w_dw"], small["conv_ln_g"],
                                                              small["conv_ln_b"], B, seq)

    def band_sum(own, prev):
        prev = prev.reshape(B, seq // BLOCK, BLOCK, 2 * HEAD_DIM)
        moved = jnp.concatenate([prev[:, 1:], jnp.zeros_like(prev[:, :1])], axis=1)
        return (own + moved.reshape(T, 2 * HEAD_DIM)).astype(BF16)

    dproj = jnp.concatenate([dq, band_sum(dko, dkp), band_sum(dvo, dvp), dca, dcb, dga, dgc], axis=1)
    dproj = dproj.reshape(T, N_CHIP, IN_SHARD).transpose(1, 0, 2)
    grads["w_in"] = _wgrad(dproj, _spec_chip_major(IN_SHARD), h2, _spec_rows(D), IN_SHARD, D, T, "dw_in")
    dx1, dsc2, dsh2, d_gn2 = _in_proj_bwd(dproj, w["w_in"], x1, small["norm_mix_g"], sc2, dx2, seq)
    sib_mix, pairs_mix = _reduce_begin(grads, MIX_WEIGHTS, core_idx, "mix")

    (dx0, dg1, dsc1, dsh1, d_gn1), chips_mix, down_grad = ffn_backward(
        "ffn1", dx1, x, h1, a1, u1, f1, small["norm_ffn1_g"], sc1, g1, comm=_ExchangeComm(pairs_mix))
    reduced.update(_reduce_end(grads, MIX_WEIGHTS, sib_mix, chips_mix, core_chip, "mix"))
    gate_up = FFN1_WEIGHTS[:2]
    sib_gu, pairs_gu = _reduce_begin(grads, gate_up, core_idx, "ffn1_gate_up")
    grads["ffn1_w_down"], chips_gu = down_grad(comm=_ExchangeComm(pairs_gu))
    reduced.update(_reduce_end(grads, gate_up, sib_gu, chips_gu, core_chip, "ffn1_gate_up"))
    sib_d, pairs_d = _reduce_begin(grads, FFN1_WEIGHTS[2:], core_idx, "ffn1_down")

    def finish_reduction(chips_d):
        reduced.update(_reduce_end(grads, FFN1_WEIGHTS[2:], sib_d, chips_d, core_chip, "ffn1_down"))
        return reduced

    dmod = jnp.concatenate([dsh1, dsc1, dg1, dsh2, dsc2, dg2, dsh3, dsc3, dg3], axis=1)
    d_sinks = jnp.sum(dsink_steps, axis=0)
    vec_grads = {"norm_ffn1_g": d_gn1, "norm_mix_g": d_gn2, "conv_b_dw": d_conv_b, "conv_ln_g": d_ln_g,
                 "conv_ln_b": d_ln_b, "norm_ffn2_g": d_gn3, "final_norm_g": d_final_g}
    return loss_parts, dx0, (_ExchangeComm(pairs_d), finish_reduction), dmod, vec_grads, d_sinks, d_conv_w


def kernel(x, c, w_ada, b_ada, norm_ffn1_g, ffn1_w_gate, ffn1_w_up, ffn1_w_down, norm_mix_g, w_in, attn_sinks, w_attn_o, conv_w_dw, conv_b_dw, conv_ln_g, conv_ln_b, w_conv_o, w_out, norm_ffn2_g, ffn2_w_gate, ffn2_w_up, ffn2_w_down, final_norm_g, loss_target, m_w_ada, m_b_ada, m_norm_ffn1_g, m_ffn1_w_gate, m_ffn1_w_up, m_ffn1_w_down, m_norm_mix_g, m_w_in, m_attn_sinks, m_w_attn_o, m_conv_w_dw, m_conv_b_dw, m_conv_ln_g, m_conv_ln_b, m_w_conv_o, m_w_out, m_norm_ffn2_g, m_ffn2_w_gate, m_ffn2_w_up, m_ffn2_w_down, m_final_norm_g, v_w_ada, v_b_ada, v_norm_ffn1_g, v_ffn1_w_gate, v_ffn1_w_up, v_ffn1_w_down, v_norm_mix_g, v_w_in, v_attn_sinks, v_w_attn_o, v_conv_w_dw, v_conv_b_dw, v_conv_ln_g, v_conv_ln_b, v_w_conv_o, v_w_out, v_norm_ffn2_g, v_ffn2_w_gate, v_ffn2_w_up, v_ffn2_w_down, v_final_norm_g):
    args = dict(locals())
    B, seq, D = x.shape
    T = B * seq
    xi, yi, ci = _position()
    chip = 2 * xi + yi
    dev = 4 * xi + 2 * yi + ci

    def shard_2d(prefix, name):
        t = args[prefix + name][0]
        return t.T if name in COL_SHARDED else t

    big = {n: shard_2d("", n) for n in BIG_WEIGHTS}
    final_g = final_norm_g[None, :]
    vec_w = {n: (args[n] if n != "final_norm_g" else final_g) for n in VECTORS}

    conv_cols = D // N_CHIP
    conv_flat = jnp.pad(conv_w_dw[0].reshape(-1), (0, 8 * D - CONV_WIDTH * conv_cols)).reshape(8, D)
    first = _gather8(jnp.concatenate([jnp.pad(c, ((0, 8 - B), (0, 0))), conv_flat], axis=0), "gather_c")
    c_all = first[:, :B].reshape(N_DEV * B, D)
    conv_taps = first[::2, 8:].reshape(N_CHIP, 8 * D)[:, :CONV_WIDTH * conv_cols]
    conv_taps = conv_taps.reshape(N_CHIP, CONV_WIDTH, conv_cols).transpose(1, 0, 2).reshape(CONV_WIDTH, D)
    conv_taps = jnp.pad(conv_taps, ((0, CONV_PAD - CONV_WIDTH), (0, 0)))

    ada_cols = w_ada.shape[2]
    b_cols = lax.dynamic_slice(b_ada, (0, chip * ada_cols), (1, ada_cols))
    mod_part = _ada_fwd(c_all, w_ada[0], b_cols).reshape(N_DEV, B, ada_cols)
    mod = _mod_exchange(mod_part).transpose(1, 0, 2).reshape(B, N_MOD, D)

    core_idx = jnp.reshape(ci, (1,)).astype(jnp.int32)
    chip_idx = jnp.reshape(chip, (1,)).astype(jnp.int32)
    core_chip = jnp.stack([ci, chip]).astype(jnp.int32)
    slots = {n: _cast_slot(big[n], chip_idx, "cast_" + n) for n in BIG_WEIGHTS}

    small = dict(vec_w)
    small["attn_sinks"] = attn_sinks
    small["conv_w_dw"] = conv_taps

    loss_parts, dx, (last_exchange, finish_reduction), dmod, vec_grads, d_sinks, d_conv_w = _local_grads(
        x.reshape(T, D), loss_target.reshape(T, D), mod, slots, small, seq, core_idx, core_chip)

    loss = lax.psum((0.5 / D) * jnp.sum(loss_parts), ("x", "y", "c"))
    grad_x = dx.reshape(B, seq, D)
    out = {}

    block = jnp.zeros((SMALL_ROWS, D), F32)
    block = block.at[ROW_DMOD0:ROW_DMOD0 + N_MOD].set(dmod[0]).at[ROW_DMOD1:ROW_DMOD1 + N_MOD].set(dmod[1])
    block = block.at[ROW_VEC:ROW_VEC + len(VECTORS)].set(jnp.concatenate([vec_grads[n] for n in VECTORS], axis=0))
    block = block.at[ROW_SINK, :2 * HEAD_DIM].set(d_sinks[0])
    block = block.at[ROW_CONVW:ROW_CONVW + CONV_WIDTH].set(d_conv_w[:CONV_WIDTH])
    small_all = _gather8(block, "gather_small_grads")

    def pack_small(prefix):
        rows = [args[prefix + "b_ada"].reshape(N_MOD, D)]
        rows += [args[prefix + n].reshape(1, D) for n in VECTORS]
        rows += [jnp.pad(args[prefix + "attn_sinks"], ((0, 0), (0, D - N_Q_HEADS)))]
        return jnp.pad(jnp.concatenate(rows, axis=0), ((0, 24 - N_MOD - len(VECTORS) - 1), (0, 0)))

    small_sum, sg, sd, sm, sv = _small_adam(small_all, pack_small(""), pack_small("m_"), pack_small("v_"),
                                           ROW_DMOD0, ROW_DMOD1, ROW_VEC)

    def unpack_small(t):
        res = {"b_ada": t[:N_MOD].reshape(1, N_MOD * D)}
        for k, n in enumerate(VECTORS):
            res[n] = t[N_MOD + k].reshape(args[n].shape)
        res["attn_sinks"] = t[N_MOD + len(VECTORS), :N_Q_HEADS].reshape(1, N_Q_HEADS)
        return res

    unpacked = [unpack_small(t) for t in (sg, sd, sm, sv)]
    for n in ("b_ada", "attn_sinks") + VECTORS:
        out[n] = tuple(u[n] for u in unpacked)

    conv_g = lax.dynamic_slice(small_sum, (ROW_CONVW, chip * conv_cols), (CONV_WIDTH, conv_cols))
    d, mn, vn = _adam_call(conv_w_dw[0], conv_g, m_conv_w_dw[0], v_conv_w_dw[0], "adam_conv_w_dw")
    out["conv_w_dw"] = tuple(t[None] for t in (conv_g, d, mn, vn))

    dmod_rows = jnp.stack([small_all[:, ROW_DMOD0:ROW_DMOD0 + N_MOD], small_all[:, ROW_DMOD1:ROW_DMOD1 + N_MOD]], axis=1)
    dmod_all = dmod_rows.reshape(N_DEV * B, N_MOD * D)
    dmod_cols = lax.dynamic_slice(dmod_all, (0, chip * ada_cols), (N_DEV * B, ada_cols))
    ada_out, chips_last = _ada_adam(c_all.T, dmod_cols, w_ada[0], m_w_ada[0], v_w_ada[0], last_exchange)
    out["w_ada"] = tuple(t[None] for t in ada_out)

    reduced = finish_reduction(chips_last)
    for n in BIG_WEIGHTS:
        g = reduced[n].reshape(big[n].shape)
        d, mn, vn = _adam_call(big[n], g, shard_2d("m_", n), shard_2d("v_", n), "adam_" + n)
        out[n] = tuple((t.T if n in COL_SHARDED else t)[None] for t in (g, d, mn, vn))

    order = ("w_ada", "b_ada", "norm_ffn1_g", "ffn1_w_gate", "ffn1_w_up", "ffn1_w_down", "norm_mix_g", "w_in",
             "attn_sinks", "w_attn_o", "conv_w_dw", "conv_b_dw", "conv_ln_g", "conv_ln_b", "w_conv_o", "w_out",
             "norm_ffn2_g", "ffn2_w_gate", "ffn2_w_up", "ffn2_w_down", "final_norm_g")
    return (loss, grad_x, *[out[n][0] for n in order], *[out[n][1] for n in order],
            *[out[n][2] for n in order], *[out[n][3] for n in order])
```

```python
import functools

import jax
import jax.numpy as jnp
from jax import lax
from jax.experimental import pallas as pl
from jax.experimental.pallas import tpu as pltpu

F32 = jnp.float32
BF16 = jnp.bfloat16

D_MODEL = 1024
D_FF = 2816
N_CHIP = 4
N_DEV = 8
FF_SHARD = D_FF // N_CHIP
IN_WIDTH = 5376
IN_SHARD = IN_WIDTH // N_CHIP
HEAD_DIM = 64
N_Q_HEADS = 16
N_KV_HEADS = 2
BLOCK = 128
CONV_WIDTH = 31
CONV_PAD = 32
N_MOD = 9
EPS = 1e-6
FFN_RESIDUAL = 0.5
ATTN_SCALE = HEAD_DIM ** -0.5
MASK_VALUE = -1e30

ADAM_LR = 0.001
ADAM_B1 = 0.9
ADAM_B2 = 0.999
ADAM_EPS = 1e-08
ADAM_WD = 0.01
ADAM_STEP = 10

COLB_Q, COLB_CA, COLB_CB, COLB_GA, COLB_GC = 0, 1, 2, 3, 4
COLB_K, COLB_V = 40, 41
PROJ_TILE = 768

VMEM_LIMIT = 56 * 1024 * 1024
MESH = pl.DeviceIdType.MESH
ANY = pl.BlockSpec(memory_space=pl.ANY)
VMEM_SPEC = pl.BlockSpec(memory_space=pltpu.VMEM)
SMEM_SPEC = pl.BlockSpec(memory_space=pltpu.SMEM)


def _params(n_grid):
    return pltpu.CompilerParams(dimension_semantics=("arbitrary",) * n_grid, vmem_limit_bytes=VMEM_LIMIT)


def _tile(n, pref):
    t = min(n, pref)
    while n % t:
        t //= 2
    return t


def _row_tile(rows, cap):
    for t in range(min(rows, cap) // 16 * 16, 0, -16):
        if rows % t == 0:
            return t
    return rows


def _sigmoid(v):
    return 1.0 / (1.0 + jnp.exp(-v))


def _dot_nn(a, b):
    return lax.dot_general(a, b, (((1,), (0,)), ((), ())), preferred_element_type=F32)


def _dot_nt(a, b):
    return lax.dot_general(a, b, (((1,), (1,)), ((), ())), preferred_element_type=F32)


def _dot_tn(a, b):
    return lax.dot_general(a, b, (((0,), (0,)), ((), ())), preferred_element_type=F32)


ROW_CHUNK = 16


def _for_row_chunks(n_rows, fn):
    for r in range(0, n_rows, ROW_CHUNK):
        fn(slice(r, r + ROW_CHUNK))


def _norm_mod(xv, gn, sc, sh):
    r = lax.rsqrt(jnp.mean(xv * xv, axis=-1, keepdims=True) + EPS)
    return ((xv * r) * gn) * (1.0 + sc) + sh


def _accumulate(ref, first, value):
    @pl.when(first)
    def _():
        ref[...] = value

    @pl.when(jnp.logical_not(first))
    def _():
        ref[...] += value


def _norm_mod_bwd(dh, xv, gn, sc, dxo, first_of_batch, first, dx_ref, dsc_ref, dsh_ref, dgn_ref):
    r = lax.rsqrt(jnp.mean(xv * xv, axis=-1, keepdims=True) + EPS)
    xh = xv * r
    _accumulate(dsh_ref, first_of_batch, jnp.sum(dh, axis=0, keepdims=True))
    _accumulate(dsc_ref, first_of_batch, jnp.sum(dh * (xh * gn), axis=0, keepdims=True))
    dn = dh * (1.0 + sc)
    _accumulate(dgn_ref, first, jnp.sum(dn * xh, axis=0, keepdims=True))
    dxh = dn * gn
    dx_ref[...] = dxo + r * (dxh - xh * jnp.mean(dxh * xh, axis=-1, keepdims=True))


CHIP_FLIPS = ((1, 0), (0, 1), (1, 1))


def _position():
    return lax.axis_index("x"), lax.axis_index("y"), lax.axis_index("c")


def _flip(v, f):
    return 1 - v if f else v


class _GatherComm:
    def __init__(self, bufs):
        n = len(bufs)
        self.n = n
        self.operands = list(bufs)
        self.out_shape = [jax.ShapeDtypeStruct(b.shape, b.dtype) for b in bufs]
        self.aliases = {i: i for i in range(n)}
        self.sems = [pltpu.SemaphoreType.DMA((6 * n,)), pltpu.SemaphoreType.DMA((6 * n,))]
        self.rows = [b.shape[1] // 2 for b in bufs]

    def _half(self, ref, i, which):
        return ref.at[pl.ds(which * self.rows[i], self.rows[i]), :]

    def _ici(self, cins, couts, sems, i, k, dst_chip, to):
        x, y, c = _position()
        return pltpu.make_async_remote_copy(
            src_ref=self._half(cins[i].at[2 * x + y], i, c), dst_ref=self._half(couts[i].at[dst_chip], i, c),
            send_sem=sems[0].at[3 * i + k], recv_sem=sems[1].at[3 * i + k], device_id=to, device_id_type=MESH)

    def _d2d(self, couts, sems, i, k, src_chip, which):
        x, y, c = _position()
        place = self._half(couts[i].at[src_chip], i, which)
        return pltpu.make_async_remote_copy(
            src_ref=place, dst_ref=place, send_sem=sems[0].at[3 * self.n + 3 * i + k],
            recv_sem=sems[1].at[3 * self.n + 3 * i + k], device_id=(x, y, 1 - c), device_id_type=MESH)

    def _peers(self):
        x, y, _ = _position()
        return [(_flip(x, fx), _flip(y, fy)) for fx, fy in CHIP_FLIPS]

    def start(self, cins, couts, sems):
        x, y, c = _position()
        for i in range(self.n):
            for k, (px, py) in enumerate(self._peers()):
                self._ici(cins, couts, sems, i, k, 2 * x + y, (px, py, c)).start()

    def finish(self, cins, couts, sems):
        _, _, c = _position()
        peers = self._peers()
        for i in range(self.n):
            for k, (px, py) in enumerate(peers):
                self._ici(cins, couts, sems, i, k, 2 * px + py, (px, py, c)).wait_recv()
                self._d2d(couts, sems, i, k, 2 * px + py, c).start()
        for i in range(self.n):
            for k, (px, py) in enumerate(peers):
                self._d2d(couts, sems, i, k, 2 * px + py, 1 - c).wait_recv()
        for i in range(self.n):
            for k, (px, py) in enumerate(peers):
                self._ici(cins, couts, sems, i, k, 2 * px + py, (px, py, c)).wait_send()
                self._d2d(couts, sems, i, k, 2 * px + py, c).wait_send()


class _ExchangeComm:
    def __init__(self, pairs):
        n = len(pairs)
        self.n = n
        self.operands = list(pairs)
        self.out_shape = [jax.ShapeDtypeStruct((3,) + p.shape[1:], p.dtype) for p in pairs]
        self.aliases = {}
        self.sems = [pltpu.SemaphoreType.DMA((3 * n,)), pltpu.SemaphoreType.DMA((3 * n,))]

    def _copies(self, cins, couts, sems):
        x, y, c = _position()
        peers = [(_flip(x, fx), _flip(y, fy)) for fx, fy in CHIP_FLIPS]
        return [pltpu.make_async_remote_copy(
            src_ref=cins[i].at[2 * px + py], dst_ref=couts[i].at[k], send_sem=sems[0].at[3 * i + k],
            recv_sem=sems[1].at[3 * i + k], device_id=(px, py, c), device_id_type=MESH)
            for i in range(self.n) for k, (px, py) in enumerate(peers)]

    def start(self, cins, couts, sems):
        for cp in self._copies(cins, couts, sems):
            cp.start()

    def finish(self, cins, couts, sems):
        for cp in self._copies(cins, couts, sems):
            cp.wait()


class _SwapComm:
    def __init__(self, grads16):
        n = len(grads16)
        self.n = n
        self.operands = list(grads16)
        self.out_shape = [jax.ShapeDtypeStruct(g.shape[1:], g.dtype) for g in grads16]
        self.aliases = {}
        self.sems = [pltpu.SemaphoreType.DMA((n,)), pltpu.SemaphoreType.DMA((n,))]

    def _copies(self, cins, couts, sems):
        x, y, c = _position()
        return [pltpu.make_async_remote_copy(
            src_ref=cins[i].at[1 - c], dst_ref=couts[i], send_sem=sems[0].at[i], recv_sem=sems[1].at[i],
            device_id=(x, y, 1 - c), device_id_type=MESH) for i in range(self.n)]

    def start(self, cins, couts, sems):
        for cp in self._copies(cins, couts, sems):
            cp.start()

    def finish(self, cins, couts, sems):
        for cp in self._copies(cins, couts, sems):
            cp.wait()


class _JoinComm:
    def __init__(self, halves):
        n = len(halves)
        self.n = n
        self.operands = list(halves)
        self.out_shape = [jax.ShapeDtypeStruct(h.shape, h.dtype) for h in halves]
        self.aliases = {i: i for i in range(n)}
        self.sems = [pltpu.SemaphoreType.DMA((n,)), pltpu.SemaphoreType.DMA((n,))]

    def _copy(self, cins, couts, sems, i, which):
        x, y, c = _position()
        return pltpu.make_async_remote_copy(
            src_ref=cins[i].at[which], dst_ref=couts[i].at[which], send_sem=sems[0].at[i], recv_sem=sems[1].at[i],
            device_id=(x, y, 1 - c), device_id_type=MESH)

    def start(self, cins, couts, sems):
        _, _, c = _position()
        for i in range(self.n):
            self._copy(cins, couts, sems, i, c).start()

    def finish(self, cins, couts, sems):
        _, _, c = _position()
        for i in range(self.n):
            self._copy(cins, couts, sems, i, 1 - c).wait_recv()
        for i in range(self.n):
            self._copy(cins, couts, sems, i, c).wait_send()


class _CommList:
    def __init__(self, parts):
        self.parts = list(parts)
        self.operands = [t for p in self.parts for t in p.operands]
        self.out_shape = [t for p in self.parts for t in p.out_shape]
        self.sems = [t for p in self.parts for t in p.sems]
        self.aliases = {}
        n_in = n_out = 0
        for p in self.parts:
            self.aliases.update({n_in + i: n_out + j for i, j in p.aliases.items()})
            n_in += len(p.operands)
            n_out += len(p.out_shape)

    def _split(self, cins, couts, sems):
        pos = [0, 0, 0]
        for p in self.parts:
            sizes = (len(p.operands), len(p.out_shape), len(p.sems))
            yield p, tuple(seq[a:a + k] for seq, a, k in zip((cins, couts, sems), pos, sizes))
            pos = [a + k for a, k in zip(pos, sizes)]

    def start(self, cins, couts, sems):
        for p, refs in self._split(cins, couts, sems):
            p.start(*refs)

    def finish(self, cins, couts, sems):
        for p, refs in self._split(cins, couts, sems):
            p.finish(*refs)

    def split_outputs(self, outs):
        res, pos = [], 0
        for p in self.parts:
            res.append(outs[pos:pos + len(p.out_shape)])
            pos += len(p.out_shape)
        return res


def _call(body, *, name, grid, in_specs, out_specs, out_shape, operands, scratch_shapes=(), comm=None):
    n_grid = len(grid)
    if comm is None:
        return pl.pallas_call(
            body, name=name, grid=grid, in_specs=list(in_specs), out_specs=list(out_specs), out_shape=list(out_shape),
            scratch_shapes=list(scratch_shapes), compiler_params=_params(n_grid))(*operands), ()
    counts = (len(in_specs), len(comm.operands), len(out_specs), len(comm.out_shape), len(scratch_shapes),
              len(comm.sems))

    def fused(*refs):
        parts, pos = [], 0
        for k in counts:
            parts.append(refs[pos:pos + k])
            pos += k
        ins, cins, outs, couts, scr, sems = parts
        first = functools.reduce(jnp.logical_and, [pl.program_id(d) == 0 for d in range(n_grid)])
        last = functools.reduce(jnp.logical_and, [pl.program_id(d) == grid[d] - 1 for d in range(n_grid)])

        @pl.when(first)
        def _():
            comm.start(cins, couts, sems)

        body(*ins, *outs, *scr)

        @pl.when(last)
        def _():
            comm.finish(cins, couts, sems)

    res = pl.pallas_call(
        fused, name=name, grid=grid, in_specs=list(in_specs) + [ANY] * counts[1],
        out_specs=list(out_specs) + [ANY] * counts[3], out_shape=list(out_shape) + list(comm.out_shape),
        scratch_shapes=list(scratch_shapes) + list(comm.sems),
        input_output_aliases={counts[0] + i: counts[2] + j for i, j in comm.aliases.items()},
        compiler_params=_params(n_grid))(*operands, *comm.operands)
    return res[:counts[2]], res[counts[2]:]


def _run_comm(comm, name):
    k_in, k_out = len(comm.operands), len(comm.out_shape)

    def body(*refs):
        cins, couts, sems = refs[:k_in], refs[k_in:k_in + k_out], refs[k_in + k_out:]
        comm.start(cins, couts, sems)
        comm.finish(cins, couts, sems)

    return pl.pallas_call(
        body, name=name, in_specs=[ANY] * k_in, out_specs=[ANY] * k_out, out_shape=list(comm.out_shape),
        scratch_shapes=list(comm.sems), input_output_aliases=dict(comm.aliases))(*comm.operands)


def _ffn_fwd(x, gn, sc, sh, gate, wg, wu, wd, seq, name, comm=None):
    T, D = x.shape
    J, Fs, _ = wg.shape
    tm = _tile(seq, 1024)
    nb = seq // tm

    def body(x_ref, gn_ref, sc_ref, sh_ref, gate_ref, wg_ref, wu_ref, wd_ref,
             h_ref, a_ref, u_ref, f_ref, xo_ref, hs, acc, s16):
        j = pl.program_id(1)

        @pl.when(j == 0)
        def _():
            hb = _norm_mod(x_ref[...], gn_ref[...], sc_ref[...], sh_ref[...]).astype(BF16)
            hs[...] = hb
            h_ref[...] = hb
            acc[...] = jnp.zeros_like(acc)

        hb = hs[...]
        a_all = _dot_nt(hb, wg_ref[...])
        u_all = _dot_nt(hb, wu_ref[...])

        def swiglu_rows(rows):
            a = a_all[rows, :]
            u = u_all[rows, :]
            a_ref[rows, :] = a.astype(BF16)
            u_ref[rows, :] = u.astype(BF16)
            s16[rows, :] = ((a * _sigmoid(a)) * u).astype(BF16)

        _for_row_chunks(tm, swiglu_rows)
        acc[...] += _dot_nn(s16[...], wd_ref[...])

        @pl.when(j == J - 1)
        def _():
            f = acc[...]
            f_ref[...] = f.astype(BF16)
            xo_ref[...] = x_ref[...] + (FFN_RESIDUAL * gate_ref[...]) * f

    row = pl.BlockSpec((tm, D), lambda i, j: (i, 0))
    vec = pl.BlockSpec((1, D), lambda i, j: (0, 0))
    per_b = pl.BlockSpec((None, 1, D), lambda i, j: (i // nb, 0, 0))
    hid = pl.BlockSpec((None, tm, Fs), lambda i, j: (j, i, 0))
    return _call(
        body, name=name, grid=(T // tm, J),
        in_specs=[row, vec, per_b, per_b, per_b] + [pl.BlockSpec((None, Fs, D), lambda i, j: (j, 0, 0))] * 3,
        out_specs=[row, hid, hid, row, row],
        out_shape=[jax.ShapeDtypeStruct((T, D), BF16), jax.ShapeDtypeStruct((J, T, Fs), BF16),
                   jax.ShapeDtypeStruct((J, T, Fs), BF16), jax.ShapeDtypeStruct((T, D), BF16),
                   jax.ShapeDtypeStruct((T, D), F32)],
        scratch_shapes=[pltpu.VMEM((tm, D), BF16), pltpu.VMEM((tm, D), F32), pltpu.VMEM((tm, Fs), BF16)],
        operands=(x, gn, sc, sh, gate, wg, wu, wd), comm=comm)


def _ffn_bwd(dxo, x, f, a, u, gn, sc, gate, wg, wu, wd, seq, name, comm=None):
    T, D = x.shape
    J, Fs, _ = wg.shape
    B = T // seq
    tm = _tile(seq, 512)
    nb = seq // tm

    def body(dxo_ref, x_ref, f_ref, a_ref, u_ref, gn_ref, sc_ref, gate_ref, wg_ref, wu_ref, wd_ref,
             da_ref, du_ref, s_ref, df_ref, dx_ref, dgate_ref, dsc_ref, dsh_ref, dgn_ref, dfs, acc):
        i = pl.program_id(0)
        j = pl.program_id(1)
        first_of_batch = i % nb == 0

        @pl.when(j == 0)
        def _():
            dxo_v = dxo_ref[...]
            dfb = ((FFN_RESIDUAL * gate_ref[...]) * dxo_v).astype(BF16)
            dfs[...] = dfb
            df_ref[...] = dfb
            part = jnp.sum((FFN_RESIDUAL * f_ref[...].astype(F32)) * dxo_v, axis=0, keepdims=True)
            _accumulate(dgate_ref, first_of_batch, part)
            acc[...] = jnp.zeros_like(acc)

        ds_all = _dot_nt(dfs[...], wd_ref[...])

        def swiglu_bwd_rows(rows):
            ds = ds_all[rows, :]
            av = a_ref[rows, :].astype(F32)
            uv = u_ref[rows, :].astype(F32)
            sig = _sigmoid(av)
            sil = av * sig
            s_ref[rows, :] = (sil * uv).astype(BF16)
            da_ref[rows, :] = (ds * uv * (sig * (1.0 + av * (1.0 - sig)))).astype(BF16)
            du_ref[rows, :] = (ds * sil).astype(BF16)

        _for_row_chunks(tm, swiglu_bwd_rows)
        acc[...] += _dot_nn(da_ref[...], wg_ref[...]) + _dot_nn(du_ref[...], wu_ref[...])

        @pl.when(j == J - 1)
        def _():
            _norm_mod_bwd(acc[...], x_ref[...], gn_ref[...], sc_ref[...], dxo_ref[...],
                          first_of_batch, i == 0, dx_ref, dsc_ref, dsh_ref, dgn_ref)

    row = pl.BlockSpec((tm, D), lambda i, j: (i, 0))
    vec = pl.BlockSpec((1, D), lambda i, j: (0, 0))
    per_b = pl.BlockSpec((None, 1, D), lambda i, j: (i // nb, 0, 0))
    hid = pl.BlockSpec((None, tm, Fs), lambda i, j: (j, i, 0))
    hid_shape = jax.ShapeDtypeStruct((J, T, Fs), BF16)
    per_b_shape = jax.ShapeDtypeStruct((B, 1, D), F32)
    return _call(
        body, name=name, grid=(T // tm, J),
        in_specs=[row, row, row, hid, hid, vec, per_b, per_b]
        + [pl.BlockSpec((None, Fs, D), lambda i, j: (j, 0, 0))] * 3,
        out_specs=[hid, hid, hid, row, row, per_b, per_b, per_b, vec],
        out_shape=[hid_shape, hid_shape, hid_shape, jax.ShapeDtypeStruct((T, D), BF16),
                   jax.ShapeDtypeStruct((T, D), F32), per_b_shape, per_b_shape, per_b_shape,
                   jax.ShapeDtypeStruct((1, D), F32)],
        scratch_shapes=[pltpu.VMEM((tm, D), BF16), pltpu.VMEM((tm, D), F32)],
        operands=(dxo, x, f, a, u, gn, sc, gate, wg, wu, wd), comm=comm)


def _wgrad(a, a_spec, b, b_spec, rows, cols, n_tok, name, comm=None):
    tk = _tile(n_tok, 1024)
    nk = n_tok // tk
    half = rows // 2

    def body(a_ref, b_ref, o32_ref, o16_ref, acc):
        k = pl.program_id(1)

        @pl.when(k == 0)
        def _():
            acc[...] = jnp.zeros_like(acc)

        acc[...] += _dot_tn(a_ref[...], b_ref[...])

        @pl.when(k == nk - 1)
        def _():
            for h in range(2):
                v = acc[h * half:(h + 1) * half, :]
                o32_ref[h] = v
                o16_ref[h] = v.astype(BF16)

    out_spec = pl.BlockSpec((2, None, half, cols), lambda j, k: (0, j, 0, 0))
    return _call(
        body, name=name, grid=(N_CHIP, nk),
        in_specs=[a_spec(tk), b_spec(tk)],
        out_specs=[out_spec, out_spec],
        out_shape=[jax.ShapeDtypeStruct((2, N_CHIP, half, cols), F32),
                   jax.ShapeDtypeStruct((2, N_CHIP, half, cols), BF16)],
        scratch_shapes=[pltpu.VMEM((rows, cols), F32)],
        operands=(a, b), comm=comm)


def _spec_rows(width):
    return lambda tk: pl.BlockSpec((tk, width), lambda j, k: (k, 0))


def _spec_chip_major(width):
    return lambda tk: pl.BlockSpec((None, tk, width), lambda j, k: (j, k, 0))


def _spec_col_block(width):
    return lambda tk: pl.BlockSpec((tk, width), lambda j, k: (k, j))


def _in_proj(x, gn, sc, sh, w_in, seq, comm=None):
    T, D = x.shape
    N = w_in.shape[0]
    tm = _tile(seq, 1024)
    nb = seq // tm

    def body(x_ref, gn_ref, sc_ref, sh_ref, w_ref, h_ref, p_ref, hs):
        @pl.when(pl.program_id(1) == 0)
        def _():
            hb = _norm_mod(x_ref[...], gn_ref[...], sc_ref[...], sh_ref[...]).astype(BF16)
            hs[...] = hb
            h_ref[...] = hb

        p_ref[...] = _dot_nt(hs[...], w_ref[...]).astype(BF16)

    row = pl.BlockSpec((tm, D), lambda i, j: (i, 0))
    per_b = pl.BlockSpec((None, 1, D), lambda i, j: (i // nb, 0, 0))
    return _call(
        body, name="mix_in_proj", grid=(T // tm, N // PROJ_TILE),
        in_specs=[row, pl.BlockSpec((1, D), lambda i, j: (0, 0)), per_b, per_b,
                  pl.BlockSpec((PROJ_TILE, D), lambda i, j: (j, 0))],
        out_specs=[row, pl.BlockSpec((tm, PROJ_TILE), lambda i, j: (i, j))],
        out_shape=[jax.ShapeDtypeStruct((T, D), BF16), jax.ShapeDtypeStruct((T, N), BF16)],
        scratch_shapes=[pltpu.VMEM((tm, D), BF16)],
        operands=(x, gn, sc, sh, w_in), comm=comm)


def _attn_specs(nblk):
    def own(col):
        return lambda b, n: (b * nblk + n, col)

    def prev(col):
        return lambda b, n: (b * nblk + jnp.maximum(n - 1, 0), col)

    kv = (BLOCK, 2 * HEAD_DIM)
    return [pl.BlockSpec((BLOCK, D_MODEL), own(COLB_Q)),
            pl.BlockSpec(kv, prev(COLB_K)), pl.BlockSpec(kv, own(COLB_K)),
            pl.BlockSpec(kv, prev(COLB_V)), pl.BlockSpec(kv, own(COLB_V))]


def _band_operands(prev_ref, own_ref, lo):
    band = jnp.concatenate([prev_ref[...], own_ref[...]], axis=0).astype(F32)
    rolled = pltpu.roll(band, HEAD_DIM, 1)
    zero = jnp.zeros_like(band)
    head0 = jnp.concatenate([jnp.where(lo, band, zero), jnp.where(lo, zero, rolled)], axis=0).astype(BF16)
    head1 = jnp.concatenate([jnp.where(lo, rolled, zero), jnp.where(lo, zero, band)], axis=0).astype(BF16)
    return head0, head1


PAIRS_PER_KV = N_Q_HEADS // 2 // N_KV_HEADS
BAND = 2 * BLOCK


def _band_valid(has_prev):
    qi = lax.broadcasted_iota(jnp.int32, (PAIRS_PER_KV * BLOCK, BAND), 0) & (BLOCK - 1)
    sj = lax.broadcasted_iota(jnp.int32, (PAIRS_PER_KV * BLOCK, BAND), 1)
    rel = qi + BLOCK - sj
    return (rel >= 0) & (rel < BLOCK) & ((sj >= BLOCK) | has_prev)


def _pair_lanes(kvh, pp):
    pair = kvh * PAIRS_PER_KV + pp
    return slice(pair * 2 * HEAD_DIM, (pair + 1) * 2 * HEAD_DIM)


def _stack_pairs(ref, kvh):
    return jnp.concatenate([ref[:, _pair_lanes(kvh, pp)] for pp in range(PAIRS_PER_KV)], axis=0)


def _rows_per_pair(columns):
    return jnp.concatenate(columns, axis=0)


def _attn_fwd(proj, sinks, batch, seq, comm=None):
    T = proj.shape[0]
    nblk = seq // BLOCK

    def body(sink_ref, q_ref, kp_ref, ko_ref, vp_ref, vo_ref, o_ref, lse_ref):
        lo = lax.broadcasted_iota(jnp.int32, (1, 2 * HEAD_DIM), 1) < HEAD_DIM
        head_lane = lax.broadcasted_iota(jnp.int32, (1, N_Q_HEADS), 1)
        valid = _band_valid(pl.program_id(1) > 0)
        k_ops = _band_operands(kp_ref, ko_ref, lo)
        v_ops = _band_operands(vp_ref, vo_ref, lo)
        lse_all = jnp.zeros((BLOCK, N_Q_HEADS), F32)
        col = jnp.zeros((BLOCK, 1), F32)
        side0_row = lax.broadcasted_iota(jnp.int32, (2 * BAND, 2 * HEAD_DIM), 0) < BAND
        low_lane = lax.broadcasted_iota(jnp.int32, (2 * BAND, 2 * HEAD_DIM), 1) < HEAD_DIM
        side_ones = jnp.where(side0_row == low_lane, 1.0, 0.0).astype(BF16)
        for kvh in range(N_KV_HEADS):
            s_all = _dot_nt(_stack_pairs(q_ref, kvh), k_ops[kvh]) * ATTN_SCALE
            weights, maxes, sink_terms = [], [], []
            for side in range(2):
                heads = [2 * (kvh * PAIRS_PER_KV + pp) + side for pp in range(PAIRS_PER_KV)]
                sink = _rows_per_pair([col + sink_ref[0, h] for h in heads])
                s = jnp.where(valid, s_all[:, side * BAND:(side + 1) * BAND], MASK_VALUE)
                m = jnp.maximum(jnp.max(s, axis=-1, keepdims=True), sink)
                weights.append(jnp.where(valid, jnp.exp(s - m), 0.0).astype(BF16))
                maxes.append(m)
                sink_terms.append(jnp.exp(sink - m))
            p_all = jnp.concatenate(weights, axis=1)
            den = _dot_nn(p_all, side_ones) + jnp.where(lo, sink_terms[0], sink_terms[1])
            out = _dot_nn(p_all, v_ops[kvh]) / den
            for pp in range(PAIRS_PER_KV):
                o_ref[:, _pair_lanes(kvh, pp)] = out[pp * BLOCK:(pp + 1) * BLOCK].astype(BF16)
            for side in range(2):
                lse = maxes[side] + jnp.log(den[:, side * HEAD_DIM:side * HEAD_DIM + 1])
                for pp in range(PAIRS_PER_KV):
                    h = 2 * (kvh * PAIRS_PER_KV + pp) + side
                    lse_all = jnp.where(head_lane == h, lse[pp * BLOCK:(pp + 1) * BLOCK], lse_all)
        lse_ref[...] = lse_all

    return _call(
        body, name="attn_fwd", grid=(batch, nblk),
        in_specs=[SMEM_SPEC] + _attn_specs(nblk),
        out_specs=[pl.BlockSpec((BLOCK, D_MODEL), lambda b, n: (b * nblk + n, 0)),
                   pl.BlockSpec((BLOCK, N_Q_HEADS), lambda b, n: (b * nblk + n, 0))],
        out_shape=[jax.ShapeDtypeStruct((T, D_MODEL), BF16), jax.ShapeDtypeStruct((T, N_Q_HEADS), F32)],
        operands=(sinks, proj, proj, proj, proj, proj), comm=comm)


def _conv_u(ca, cb):
    return ca.astype(F32) * _sigmoid(cb.astype(F32))


def _conv_specs(ts, tiles_per_seq):
    per_tile = ts // CONV_PAD

    def tile(col):
        return lambda b, t: (b * tiles_per_seq + t, col)

    def before(col):
        return lambda b, t: (jnp.maximum((b * tiles_per_seq + t) * per_tile - 1, 0), col)

    return [pl.BlockSpec((ts, D_MODEL), tile(COLB_CA)), pl.BlockSpec((ts, D_MODEL), tile(COLB_CB)),
            pl.BlockSpec((CONV_PAD, D_MODEL), before(COLB_CA)), pl.BlockSpec((CONV_PAD, D_MODEL), before(COLB_CB))]


SUBLANES = 8


def _fill_upad(upad, ca_ref, cb_ref, cah_ref, cbh_ref, t):
    halo = _conv_u(cah_ref[...], cbh_ref[...])
    upad[0, 0:CONV_PAD, :] = jnp.where(t > 0, halo, jnp.zeros_like(halo))
    upad[0, CONV_PAD:, :] = _conv_u(ca_ref[...], cb_ref[...])


def _fill_shifted(pad):
    rows = pad.shape[1] - SUBLANES
    for b in range(1, SUBLANES):
        pad[b, 0:rows, :] = pad[0, b:b + rows, :]


def _shifted_rows(pad, offset, rows):
    b = offset % SUBLANES
    return pad[b, offset - b:offset - b + rows, :]


def _layernorm_stats(y):
    mu = jnp.mean(y, axis=-1, keepdims=True)
    yc = y - mu
    rstd = lax.rsqrt(jnp.mean(yc * yc, axis=-1, keepdims=True) + EPS)
    return yc * rstd, rstd


def _conv_fwd(proj, w_dw, b_dw, ln_g, ln_b, batch, seq, comm=None):
    T = proj.shape[0]
    ts = _tile(seq, 256)
    nt = seq // ts
    shift = CONV_PAD - (CONV_WIDTH - 1)

    def body(ca_ref, cb_ref, cah_ref, cbh_ref, w_ref, b_ref, g_ref, beta_ref, y_ref, z_ref, upad):
        _fill_upad(upad, ca_ref, cb_ref, cah_ref, cbh_ref, pl.program_id(1))
        _fill_shifted(upad)
        y = jnp.zeros((ts, D_MODEL), F32) + b_ref[...]
        for k in range(CONV_WIDTH):
            y = y + w_ref[k:k + 1, :] * _shifted_rows(upad, shift + k, ts)
        y_ref[...] = y
        lnh, _ = _layernorm_stats(y)
        ln = lnh * g_ref[...] + beta_ref[...]
        z_ref[...] = (ln * _sigmoid(ln)).astype(BF16)

    vec = pl.BlockSpec((1, D_MODEL), lambda b, t: (0, 0))
    row = pl.BlockSpec((ts, D_MODEL), lambda b, t: (b * nt + t, 0))
    return _call(
        body, name="conv_fwd", grid=(batch, nt),
        in_specs=_conv_specs(ts, nt) + [pl.BlockSpec((CONV_PAD, D_MODEL), lambda b, t: (0, 0)), vec, vec, vec],
        out_specs=[row, row],
        out_shape=[jax.ShapeDtypeStruct((T, D_MODEL), F32), jax.ShapeDtypeStruct((T, D_MODEL), BF16)],
        scratch_shapes=[pltpu.VMEM((SUBLANES, ts + CONV_PAD, D_MODEL), F32)],
        operands=(proj, proj, proj, proj, w_dw, b_dw, ln_g, ln_b), comm=comm)


def _merge(o, z, proj, w_ao, w_co, w_out, x, gate, seq):
    T, D = x.shape
    tm = _tile(seq, 512)
    nb = seq // tm

    def body(o_ref, z_ref, ga_ref, gc_ref, wao_ref, wco_ref, wout_ref, x_ref, gate_ref,
             ya_ref, yc_ref, mg_ref, mo_ref, xo_ref):
        ya = _dot_nn(o_ref[...], wao_ref[...])
        yc = _dot_nn(z_ref[...], wco_ref[...])
        ya_ref[...] = ya.astype(BF16)
        yc_ref[...] = yc.astype(BF16)
        merged = (_sigmoid(ga_ref[...].astype(F32)) * ya + _sigmoid(gc_ref[...].astype(F32)) * yc).astype(BF16)
        mg_ref[...] = merged
        mo = _dot_nn(merged, wout_ref[...])
        mo_ref[...] = mo.astype(BF16)
        xo_ref[...] = x_ref[...] + gate_ref[...] * mo

    row = pl.BlockSpec((tm, D), lambda i: (i, 0))
    mat = pl.BlockSpec((D, D), lambda i: (0, 0))
    act = jax.ShapeDtypeStruct((T, D), BF16)
    return pl.pallas_call(
        body, name="mix_merge", grid=(T // tm,),
        in_specs=[row, row, pl.BlockSpec((tm, D), lambda i: (i, COLB_GA)), pl.BlockSpec((tm, D), lambda i: (i, COLB_GC)),
                  mat, mat, mat, row, pl.BlockSpec((None, 1, D), lambda i: (i // nb, 0, 0))],
        out_specs=[row, row, row, row, row],
        out_shape=[act, act, act, act, jax.ShapeDtypeStruct((T, D), F32)],
        compiler_params=_params(1),
    )(o, z, proj, proj, w_ao, w_co, w_out, x, gate)


def _final_loss(x, gf, target):
    T, D = x.shape
    tm = _tile(T, 512)

    def body(x_ref, gf_ref, t_ref, dx_ref, lp_ref, dgf_ref):
        first = pl.program_id(0) == 0
        xv = x_ref[...]
        gfv = gf_ref[...]
        r = lax.rsqrt(jnp.mean(xv * xv, axis=-1, keepdims=True) + EPS)
        xh = xv * r
        err = xh * gfv - t_ref[...]
        _accumulate(lp_ref, first, jnp.sum(err * err, axis=0, keepdims=True))
        dy = err * (1.0 / D)
        _accumulate(dgf_ref, first, jnp.sum(dy * xh, axis=0, keepdims=True))
        dxh = dy * gfv
        dx_ref[...] = r * (dxh - xh * jnp.mean(dxh * xh, axis=-1, keepdims=True))

    row = pl.BlockSpec((tm, D), lambda i: (i, 0))
    vec = pl.BlockSpec((1, D), lambda i: (0, 0))
    return pl.pallas_call(
        body, name="final_loss", grid=(T // tm,),
        in_specs=[row, vec, row], out_specs=[row, vec, vec],
        out_shape=[jax.ShapeDtypeStruct((T, D), F32), jax.ShapeDtypeStruct((1, D), F32),
                   jax.ShapeDtypeStruct((1, D), F32)],
        compiler_params=_params(1),
    )(x, gf, target)


def _merge_bwd(dxo, mo, gate, proj, ya, yc, w_out, w_ao, w_co, seq, comm=None):
    T, D = dxo.shape
    B = T // seq
    tm = _tile(seq, 512)
    nb = seq // tm

    def body(dxo_ref, mo_ref, gate_ref, ga_ref, gc_ref, ya_ref, yc_ref, wout_ref, wao_ref, wco_ref,
             dmo_ref, dya_ref, dyc_ref, dga_ref, dgc_ref, do_ref, dz_ref, dgate_ref):
        dxo_v = dxo_ref[...]
        dmo = (gate_ref[...] * dxo_v).astype(BF16)
        dmo_ref[...] = dmo
        _accumulate(dgate_ref, pl.program_id(0) % nb == 0,
                    jnp.sum(mo_ref[...].astype(F32) * dxo_v, axis=0, keepdims=True))
        dm = _dot_nt(dmo, wout_ref[...])
        sa = _sigmoid(ga_ref[...].astype(F32))
        sc = _sigmoid(gc_ref[...].astype(F32))
        dya = (sa * dm).astype(BF16)
        dyc = (sc * dm).astype(BF16)
        dya_ref[...] = dya
        dyc_ref[...] = dyc
        dga_ref[...] = (dm * ya_ref[...].astype(F32) * (sa * (1.0 - sa))).astype(BF16)
        dgc_ref[...] = (dm * yc_ref[...].astype(F32) * (sc * (1.0 - sc))).astype(BF16)
        do_ref[...] = _dot_nt(dya, wao_ref[...]).astype(BF16)
        dz_ref[...] = _dot_nt(dyc, wco_ref[...]).astype(BF16)

    row = pl.BlockSpec((tm, D), lambda i: (i, 0))
    mat = pl.BlockSpec((D, D), lambda i: (0, 0))
    per_b = pl.BlockSpec((None, 1, D), lambda i: (i // nb, 0, 0))
    act = jax.ShapeDtypeStruct((T, D), BF16)
    return _call(
        body, name="mix_merge_bwd", grid=(T // tm,),
        in_specs=[row, row, per_b, pl.BlockSpec((tm, D), lambda i: (i, COLB_GA)),
                  pl.BlockSpec((tm, D), lambda i: (i, COLB_GC)), row, row, mat, mat, mat],
        out_specs=[row] * 7 + [per_b],
        out_shape=[act] * 7 + [jax.ShapeDtypeStruct((B, 1, D), F32)],
        operands=(dxo, mo, gate, proj, proj, ya, yc, w_out, w_ao, w_co), comm=comm)


def _attn_bwd(proj, sinks, o, do, lse, batch, seq, comm=None):
    T = proj.shape[0]
    nblk = seq // BLOCK
    n_steps = batch * nblk

    def body(sink_ref, q_ref, kp_ref, ko_ref, vp_ref, vo_ref, o_ref, do_ref, lse_ref,
             dq_ref, dkp_ref, dko_ref, dvp_ref, dvo_ref, dsink_ref):
        lo = lax.broadcasted_iota(jnp.int32, (1, 2 * HEAD_DIM), 1) < HEAD_DIM
        sink_lane = lax.broadcasted_iota(jnp.int32, (1, 2 * HEAD_DIM), 1)
        valid = _band_valid(pl.program_id(1) > 0)
        k_ops = _band_operands(kp_ref, ko_ref, lo)
        v_ops = _band_operands(vp_ref, vo_ref, lo)
        dsink = jnp.zeros((1, 2 * HEAD_DIM), F32)
        col = jnp.zeros((BLOCK, 1), F32)

        def fold(both):
            return (jnp.where(lo, both[:BAND], 0.0)
                    + pltpu.roll(jnp.where(lo, 0.0, both[BAND:]), HEAD_DIM, 1))

        dk_heads, dv_heads = [], []
        for kvh in range(N_KV_HEADS):
            q4 = _stack_pairs(q_ref, kvh)
            do4 = _stack_pairs(do_ref, kvh)
            dd = do4.astype(F32) * _stack_pairs(o_ref, kvh).astype(F32)
            s_all = _dot_nt(q4, k_ops[kvh]) * ATTN_SCALE
            dp_all = _dot_nt(do4, v_ops[kvh])
            ds_sides, p_sides = [], []
            for side in range(2):
                heads = [2 * (kvh * PAIRS_PER_KV + pp) + side for pp in range(PAIRS_PER_KV)]
                mine = lo if side == 0 else jnp.logical_not(lo)
                cols = slice(side * BAND, (side + 1) * BAND)
                sink = _rows_per_pair([col + sink_ref[0, h] for h in heads])
                lse = _rows_per_pair([lse_ref[:, h:h + 1] for h in heads])
                delta = jnp.sum(jnp.where(mine, dd, 0.0), axis=-1, keepdims=True)
                p = jnp.where(valid, jnp.exp(jnp.where(valid, s_all[:, cols], MASK_VALUE) - lse), 0.0)
                ds_sides.append((p * (dp_all[:, cols] - delta) * ATTN_SCALE).astype(BF16))
                p_sides.append(p.astype(BF16))
                sink_part = jnp.exp(sink - lse) * delta
                for pp, h in enumerate(heads):
                    dsink = dsink + jnp.where(sink_lane == h, -jnp.sum(sink_part[pp * BLOCK:(pp + 1) * BLOCK]), 0.0)
            ds_all = jnp.concatenate(ds_sides, axis=1)
            dq4 = _dot_nn(ds_all, k_ops[kvh])
            for pp in range(PAIRS_PER_KV):
                dq_ref[:, _pair_lanes(kvh, pp)] = dq4[pp * BLOCK:(pp + 1) * BLOCK].astype(BF16)
            dk_heads.append(fold(_dot_tn(ds_all, q4)))
            dv_heads.append(fold(_dot_tn(jnp.concatenate(p_sides, axis=1), do4)))
        dk = dk_heads[0] + pltpu.roll(dk_heads[1], HEAD_DIM, 1)
        dv = dv_heads[0] + pltpu.roll(dv_heads[1], HEAD_DIM, 1)
        dkp_ref[...] = dk[:BLOCK]
        dko_ref[...] = dk[BLOCK:]
        dvp_ref[...] = dv[:BLOCK]
        dvo_ref[...] = dv[BLOCK:]
        dsink_ref[...] = dsink

    def own(b, n):
        return (b * nblk + n, 0)

    row = pl.BlockSpec((BLOCK, D_MODEL), own)
    kv = pl.BlockSpec((BLOCK, 2 * HEAD_DIM), own)
    kv_shape = jax.ShapeDtypeStruct((T, 2 * HEAD_DIM), F32)
    return _call(
        body, name="attn_bwd", grid=(batch, nblk),
        in_specs=[SMEM_SPEC] + _attn_specs(nblk) + [row, row, pl.BlockSpec((BLOCK, N_Q_HEADS), own)],
        out_specs=[row, kv, kv, kv, kv, pl.BlockSpec((None, 1, 2 * HEAD_DIM), lambda b, n: (b * nblk + n, 0, 0))],
        out_shape=[jax.ShapeDtypeStruct((T, D_MODEL), BF16), kv_shape, kv_shape, kv_shape, kv_shape,
                   jax.ShapeDtypeStruct((n_steps, 1, 2 * HEAD_DIM), F32)],
        operands=(sinks, proj, proj, proj, proj, proj, o, do, lse), comm=comm)


def _conv_bwd(proj, dz, ydw, w_dw, ln_g, ln_b, batch, seq, comm=None):
    T = proj.shape[0]
    ts = _tile(seq, 256)
    nt = seq // ts
    per_tile = ts // CONV_PAD
    shift = CONV_PAD - (CONV_WIDTH - 1)

    def body(ca_ref, cb_ref, cah_ref, cbh_ref, dz_ref, dzn_ref, y_ref, yn_ref, w_ref, g_ref, beta_ref,
             dca_ref, dcb_ref, dw_ref, db_ref, dg_ref, dbeta_ref, upad, dypad):
        t = pl.program_id(1)
        first = (pl.program_id(0) == 0) & (t == 0)
        gv = g_ref[...]

        def ln_bwd(dzv, yv):
            lnh, rstd = _layernorm_stats(yv)
            ln = lnh * gv + beta_ref[...]
            sg = _sigmoid(ln)
            dln = dzv.astype(F32) * (sg * (1.0 + ln * (1.0 - sg)))
            dyh = dln * gv
            dy = rstd * (dyh - jnp.mean(dyh, axis=-1, keepdims=True)
                         - lnh * jnp.mean(dyh * lnh, axis=-1, keepdims=True))
            return dy, dln, lnh

        dy, dln, lnh = ln_bwd(dz_ref[...], y_ref[...])
        dy_next, _, _ = ln_bwd(dzn_ref[...], yn_ref[...])
        dypad[0, 0:ts, :] = dy
        dypad[0, ts:, :] = jnp.where(t < nt - 1, dy_next, jnp.zeros_like(dy_next))
        _fill_shifted(dypad)
        _fill_upad(upad, ca_ref, cb_ref, cah_ref, cbh_ref, t)
        _fill_shifted(upad)

        _accumulate(dg_ref, first, jnp.sum(dln * lnh, axis=0, keepdims=True))
        _accumulate(dbeta_ref, first, jnp.sum(dln, axis=0, keepdims=True))
        _accumulate(db_ref, first, jnp.sum(dy, axis=0, keepdims=True))

        @pl.when(first)
        def _():
            dw_ref[...] = jnp.zeros_like(dw_ref)

        du = jnp.zeros((ts, D_MODEL), F32)
        for k in range(CONV_WIDTH):
            du = du + w_ref[k:k + 1, :] * _shifted_rows(dypad, CONV_WIDTH - 1 - k, ts)
            dw_ref[k:k + 1, :] += jnp.sum(dy * _shifted_rows(upad, shift + k, ts), axis=0, keepdims=True)
        cav = ca_ref[...].astype(F32)
        sb = _sigmoid(cb_ref[...].astype(F32))
        dca_ref[...] = (du * sb).astype(BF16)
        dcb_ref[...] = (du * cav * (sb * (1.0 - sb))).astype(BF16)

    def tile(b, t):
        return (b * nt + t, 0)

    def after(b, t):
        return (jnp.minimum((b * nt + t + 1) * per_tile, T // CONV_PAD - 1), 0)

    row = pl.BlockSpec((ts, D_MODEL), tile)
    halo = pl.BlockSpec((CONV_PAD, D_MODEL), after)
    vec = pl.BlockSpec((1, D_MODEL), lambda b, t: (0, 0))
    wspec = pl.BlockSpec((CONV_PAD, D_MODEL), lambda b, t: (0, 0))
    act = jax.ShapeDtypeStruct((T, D_MODEL), BF16)
    vec_shape = jax.ShapeDtypeStruct((1, D_MODEL), F32)
    return _call(
        body, name="conv_bwd", grid=(batch, nt),
        in_specs=_conv_specs(ts, nt) + [row, halo, row, halo, wspec, vec, vec],
        out_specs=[row, row, wspec, vec, vec, vec],
        out_shape=[act, act, jax.ShapeDtypeStruct((CONV_PAD, D_MODEL), F32), vec_shape, vec_shape, vec_shape],
        scratch_shapes=[pltpu.VMEM((SUBLANES, ts + CONV_PAD, D_MODEL), F32)] * 2,
        operands=(proj, proj, proj, proj, dz, dz, ydw, ydw, w_dw, ln_g, ln_b), comm=comm)


def _in_proj_bwd(dproj, w_in_g, x, gn, sc, dxo, seq, comm=None):
    T, D = x.shape
    J, W, _ = w_in_g.shape
    B = T // seq
    tm = _tile(seq, 512)
    nb = seq // tm

    def body(dp_ref, w_ref, x_ref, gn_ref, sc_ref, dxo_ref, dx_ref, dsc_ref, dsh_ref, dgn_ref, acc):
        i = pl.program_id(0)
        j = pl.program_id(1)

        @pl.when(j == 0)
        def _():
            acc[...] = jnp.zeros_like(acc)

        acc[...] += _dot_nn(dp_ref[...], w_ref[...])

        @pl.when(j == J - 1)
        def _():
            _norm_mod_bwd(acc[...], x_ref[...], gn_ref[...], sc_ref[...], dxo_ref[...],
                          i % nb == 0, i == 0, dx_ref, dsc_ref, dsh_ref, dgn_ref)

    row = pl.BlockSpec((tm, D), lambda i, j: (i, 0))
    vec = pl.BlockSpec((1, D), lambda i, j: (0, 0))
    per_b = pl.BlockSpec((None, 1, D), lambda i, j: (i // nb, 0, 0))
    per_b_shape = jax.ShapeDtypeStruct((B, 1, D), F32)
    return _call(
        body, name="mix_in_proj_bwd", grid=(T // tm, J),
        in_specs=[pl.BlockSpec((None, tm, W), lambda i, j: (j, i, 0)),
                  pl.BlockSpec((None, W, D), lambda i, j: (j, 0, 0)), row, vec, per_b, row],
        out_specs=[row, per_b, per_b, vec],
        out_shape=[jax.ShapeDtypeStruct((T, D), F32), per_b_shape, per_b_shape, jax.ShapeDtypeStruct((1, D), F32)],
        scratch_shapes=[pltpu.VMEM((tm, D), F32)],
        operands=(dproj, w_in_g, x, gn, sc, dxo), comm=comm)


def _ada_fwd(c_all, w_ada, b_cols):
    nbatch, D = c_all.shape
    N = w_ada.shape[1]
    tn = _tile(N, 768)

    def body(c_ref, w_ref, b_ref, o_ref):
        cv = c_ref[...]
        act = (cv * _sigmoid(cv)).astype(BF16)
        o_ref[...] = _dot_nn(act, w_ref[...].astype(BF16)) + b_ref[...]

    return pl.pallas_call(
        body, name="ada_fwd", grid=(N // tn,),
        in_specs=[pl.BlockSpec((nbatch, D), lambda j: (0, 0)), pl.BlockSpec((D, tn), lambda j: (0, j)),
                  pl.BlockSpec((1, tn), lambda j: (0, j))],
        out_specs=pl.BlockSpec((nbatch, tn), lambda j: (0, j)),
        out_shape=jax.ShapeDtypeStruct((nbatch, N), F32),
        compiler_params=_params(1),
    )(c_all, w_ada, b_cols)


def _adamw(w, g, m, v):
    m = ADAM_B1 * m + (1.0 - ADAM_B1) * g
    v = ADAM_B2 * v + (1.0 - ADAM_B2) * (g * g)
    m_hat = m / (1.0 - ADAM_B1 ** ADAM_STEP)
    v_hat = v / (1.0 - ADAM_B2 ** ADAM_STEP)
    delta = -ADAM_LR * (m_hat / (jnp.sqrt(v_hat) + ADAM_EPS) + ADAM_WD * w)
    return delta, m, v


def _adam_call(w, g, m, v, name, comm=None):
    R, C = w.shape
    tr = _row_tile(R, 512)

    def body(w_ref, g_ref, m_ref, v_ref, d_ref, mo_ref, vo_ref):
        d, mn, vn = _adamw(w_ref[...], g_ref[...], m_ref[...], v_ref[...])
        d_ref[...] = d
        mo_ref[...] = mn
        vo_ref[...] = vn

    blk = pl.BlockSpec((tr, C), lambda i: (i, 0))
    shape = jax.ShapeDtypeStruct((R, C), F32)
    return _call(body, name=name, grid=(R // tr,), in_specs=[blk] * 4, out_specs=[blk] * 3, out_shape=[shape] * 3,
                 operands=(w, g, m, v), comm=comm)


def _ada_adam(c_act_t, dmod_cols, w, m, v, comm):
    R, C = w.shape
    nbatch = c_act_t.shape[1]
    tr = _tile(R, 128)

    def body(ct_ref, dm_ref, w_ref, m_ref, v_ref, g_ref, d_ref, mo_ref, vo_ref):
        cv = ct_ref[...]
        g = _dot_nn((cv * _sigmoid(cv)).astype(BF16), dm_ref[...].astype(BF16))
        g_ref[...] = g
        d, mn, vn = _adamw(w_ref[...], g, m_ref[...], v_ref[...])
        d_ref[...] = d
        mo_ref[...] = mn
        vo_ref[...] = vn

    blk = pl.BlockSpec((tr, C), lambda i: (i, 0))
    shape = jax.ShapeDtypeStruct((R, C), F32)
    return _call(
        body, name="ada_adam", grid=(R // tr,),
        in_specs=[pl.BlockSpec((tr, nbatch), lambda i: (i, 0)), pl.BlockSpec((nbatch, C), lambda i: (0, 0)),
                  blk, blk, blk],
        out_specs=[blk] * 4, out_shape=[shape] * 4,
        operands=(c_act_t, dmod_cols, w, m, v), comm=comm)


def _small_adam(gathered, w, m, v, rows_b0, rows_b1, rows_vec):
    _, P, D = gathered.shape
    R = w.shape[0]

    def body(ga_ref, w_ref, m_ref, v_ref, sum_ref, g_ref, d_ref, mo_ref, vo_ref):
        total = ga_ref[0]
        for dev in range(1, N_DEV):
            total = total + ga_ref[dev]
        sum_ref[...] = total
        g_ref[...] = jnp.zeros_like(g_ref)
        g_ref[0:N_MOD, :] = (sum_ref[rows_b0:rows_b0 + N_MOD, :] + sum_ref[rows_b1:rows_b1 + N_MOD, :])
        g_ref[N_MOD:N_MOD + 8, :] = sum_ref[rows_vec:rows_vec + 8, :]
        d, mn, vn = _adamw(w_ref[...], g_ref[...], m_ref[...], v_ref[...])
        d_ref[...] = d
        mo_ref[...] = mn
        vo_ref[...] = vn

    shape = jax.ShapeDtypeStruct((R, D), F32)
    return pl.pallas_call(
        body, name="small_adam",
        in_specs=[VMEM_SPEC] * 4, out_specs=[VMEM_SPEC] * 5,
        out_shape=[jax.ShapeDtypeStruct((P, D), F32), shape, shape, shape, shape],
        compiler_params=pltpu.CompilerParams(vmem_limit_bytes=VMEM_LIMIT),
    )(gathered, w, m, v)


def _gather8(v, name):
    A, W = v.shape
    flips = [(fx, fy, fc) for fx in (0, 1) for fy in (0, 1) for fc in (0, 1) if (fx, fy, fc) != (0, 0, 0)]

    def body(v_ref, out_ref, send_sems, recv_sems, local_sem):
        x, y, c = _position()
        me = 4 * x + 2 * y + c
        mine = pltpu.make_async_copy(v_ref, out_ref.at[me], local_sem)
        mine.start()

        def copy(k, block, to):
            return pltpu.make_async_remote_copy(src_ref=v_ref, dst_ref=out_ref.at[block], send_sem=send_sems.at[k],
                                                recv_sem=recv_sems.at[k], device_id=to, device_id_type=MESH)

        peers = [(_flip(x, fx), _flip(y, fy), _flip(c, fc)) for fx, fy, fc in flips]
        sends = [copy(k, me, peer) for k, peer in enumerate(peers)]
        for cp in sends:
            cp.start()
        for k, (px, py, pc) in enumerate(peers):
            copy(k, 4 * px + 2 * py + pc, (px, py, pc)).wait_recv()
        for cp in sends:
            cp.wait_send()
        mine.wait()

    return pl.pallas_call(
        body, name=name, in_specs=[VMEM_SPEC], out_specs=VMEM_SPEC,
        out_shape=jax.ShapeDtypeStruct((N_DEV, A, W), v.dtype),
        scratch_shapes=[pltpu.SemaphoreType.DMA((N_DEV - 1,)), pltpu.SemaphoreType.DMA((N_DEV - 1,)),
                        pltpu.SemaphoreType.DMA],
    )(v)


def _mod_exchange(part):
    _, A, W = part.shape

    def body(p_ref, out_ref, send_sems, recv_sems, local_sem):
        x, y, c = _position()
        me = 4 * x + 2 * y + c
        chip = 2 * x + y
        mine = pltpu.make_async_copy(p_ref.at[me], out_ref.at[chip], local_sem)
        mine.start()
        peers = [(_flip(x, fx), _flip(y, fy)) for fx, fy in CHIP_FLIPS]
        sends = []
        for k, (px, py) in enumerate(peers):
            sends.append(pltpu.make_async_remote_copy(
                src_ref=p_ref.at[4 * px + 2 * py + c], dst_ref=out_ref.at[chip], send_sem=send_sems.at[k],
                recv_sem=recv_sems.at[k], device_id=(px, py, c), device_id_type=MESH))
        for cp in sends:
            cp.start()
        for k, (px, py) in enumerate(peers):
            pltpu.make_async_remote_copy(
                src_ref=p_ref.at[me], dst_ref=out_ref.at[2 * px + py], send_sem=send_sems.at[k],
                recv_sem=recv_sems.at[k], device_id=(px, py, c), device_id_type=MESH).wait_recv()
        for cp in sends:
            cp.wait_send()
        mine.wait()

    return pl.pallas_call(
        body, name="mod_exchange", in_specs=[VMEM_SPEC], out_specs=VMEM_SPEC,
        out_shape=jax.ShapeDtypeStruct((N_CHIP, A, W), part.dtype),
        scratch_shapes=[pltpu.SemaphoreType.DMA((3,)), pltpu.SemaphoreType.DMA((3,)), pltpu.SemaphoreType.DMA],
    )(part)


def _cast_slot(w, chip_idx, name):
    R, C = w.shape
    tr = _row_tile(R, 512)

    def body(chip_ref, w_ref, o_ref):
        o_ref[...] = w_ref[...].astype(BF16)

    return pl.pallas_call(
        body, name=name,
        grid_spec=pltpu.PrefetchScalarGridSpec(
            num_scalar_prefetch=1, grid=(R // tr,),
            in_specs=[pl.BlockSpec((tr, C), lambda i, chip_ref: (i, 0))],
            out_specs=pl.BlockSpec((None, tr, C), lambda i, chip_ref: (chip_ref[0], i, 0))),
        out_shape=jax.ShapeDtypeStruct((N_CHIP, R, C), BF16),
        compiler_params=_params(1),
    )(chip_idx, w)


def _pair_sum(g32, recv, core, name):
    _, J, r, C = g32.shape

    def body(core_ref, g_ref, r_ref, o_ref):
        o_ref[...] = (g_ref[...] + r_ref[...].astype(F32)).astype(BF16)

    return pl.pallas_call(
        body, name=name,
        grid_spec=pltpu.PrefetchScalarGridSpec(
            num_scalar_prefetch=1, grid=(J,),
            in_specs=[pl.BlockSpec((None, None, r, C), lambda j, core_ref: (core_ref[0], j, 0, 0)),
                      pl.BlockSpec((None, r, C), lambda j, core_ref: (j, 0, 0))],
            out_specs=pl.BlockSpec((None, r, C), lambda j, core_ref: (j, 0, 0))),
        out_shape=jax.ShapeDtypeStruct((J, r, C), BF16),
        compiler_params=_params(1),
    )(core, g32, recv)


def _chip_sum(g32, recv_sib, recv_chips, core_chip, name):
    _, J, r, C = g32.shape

    def body(idx_ref, g_ref, s_ref, o_ref_in, o_ref):
        total = g_ref[...] + s_ref[...].astype(F32)
        for k in range(3):
            total = total + o_ref_in[k].astype(F32)
        o_ref[...] = total

    return pl.pallas_call(
        body, name=name,
        grid_spec=pltpu.PrefetchScalarGridSpec(
            num_scalar_prefetch=1, grid=(1,),
            in_specs=[pl.BlockSpec((None, None, r, C), lambda i, idx: (idx[0], idx[1], 0, 0)),
                      pl.BlockSpec((None, r, C), lambda i, idx: (idx[1], 0, 0)),
                      pl.BlockSpec((3, r, C), lambda i, idx: (0, 0, 0))],
            out_specs=pl.BlockSpec((None, r, C), lambda i, idx: (idx[0], 0, 0))),
        out_shape=jax.ShapeDtypeStruct((2, r, C), F32),
        compiler_params=_params(1),
    )(core_chip, g32, recv_sib, recv_chips)


ICI_US_PER_ELEMENT = 4.6e-5


class _Reducer:
    def __init__(self, core_idx, core_chip):
        self.core_idx, self.core_chip = core_idx, core_chip
        self.grads, self.halves, self.reduced = {}, {}, {}
        self.ready_swap, self.ready_exchange, self.ready_join = [], [], []
        self.inflight, self.current = ([], [], []), None
        self.flushes = 0

    def add(self, name, grad_pair):
        self.grads[name] = grad_pair
        self.ready_swap.append(name)

    def comm(self, budget_us):
        swaps, self.ready_swap = self.ready_swap, []
        joins, self.ready_join = self.ready_join, []
        exchanges, waiting = [], []
        for item in self.ready_exchange:
            cost = ICI_US_PER_ELEMENT * 2 * item[2].shape[1] * item[2].shape[2]
            if cost <= budget_us:
                exchanges.append(item)
                budget_us -= cost
            else:
                waiting.append(item)
        self.ready_exchange = waiting
        parts = []
        if swaps:
            parts.append(_SwapComm([self.grads[n][1] for n in swaps]))
        if exchanges:
            parts.append(_ExchangeComm([pair for _, _, pair in exchanges]))
        if joins:
            parts.append(_JoinComm([self.halves[n] for n in joins]))
        self.inflight = (swaps, exchanges, joins)
        self.current = _CommList(parts) if parts else None
        return self.current

    def done(self, comm_outs):
        if self.current is None:
            return
        swaps, exchanges, joins = self.inflight
        outs = iter(self.current.split_outputs(list(comm_outs)))
        if swaps:
            for n, recv in zip(swaps, next(outs)):
                pair = _pair_sum(self.grads[n][0], recv, self.core_idx, "pair_sum_" + n)
                self.ready_exchange.append((n, recv, pair))
        if exchanges:
            for (n, recv, _), chips in zip(exchanges, next(outs)):
                self.halves[n] = _chip_sum(self.grads[n][0], recv, chips, self.core_chip, "chip_sum_" + n)
                self.ready_join.append(n)
        if joins:
            self.reduced.update(zip(joins, next(outs)))
        self.current = None

    def run(self, kernel, budget_us, *args, **kwargs):
        outs, comm_outs = kernel(*args, comm=self.comm(budget_us), **kwargs)
        self.done(comm_outs)
        return outs

    def step(self):
        comm = self.comm(float("inf"))
        self.flushes += 1
        self.done(_run_comm(comm, "grad_reduce_tail_%d" % self.flushes))


BIG_WEIGHTS = ("ffn1_w_gate", "ffn1_w_up", "ffn1_w_down", "w_in", "w_attn_o", "w_conv_o", "w_out",
               "ffn2_w_gate", "ffn2_w_up", "ffn2_w_down")
VECTORS = ("norm_ffn1_g", "norm_mix_g", "conv_b_dw", "conv_ln_g", "conv_ln_b", "norm_ffn2_g", "final_norm_g")
ROW_DMOD0, ROW_DMOD1, ROW_VEC, ROW_SINK, ROW_CONVW, SMALL_ROWS = 0, 16, 33, 40, 41, 72


FFN1_WEIGHTS = ("ffn1_w_gate", "ffn1_w_up", "ffn1_w_down")
FFN2_WEIGHTS = ("ffn2_w_gate", "ffn2_w_up", "ffn2_w_down")
MIX_WEIGHTS = ("w_in", "w_attn_o", "w_conv_o", "w_out")
COL_SHARDED = ("ffn1_w_gate", "ffn1_w_up", "ffn2_w_gate", "ffn2_w_up", "w_in")


def _local_grads(x, target, mod, slots, small, seq, core_idx, core_chip):
    T, D = x.shape
    B = T // seq
    mods = [mod[:, k][:, None, :] for k in range(N_MOD)]
    sh1, sc1, g1, sh2, sc2, g2, sh3, sc3, g3 = mods
    w = dict(zip(FFN1_WEIGHTS, _run_comm(_GatherComm([slots[n] for n in FFN1_WEIGHTS]), "gather_ffn1")))

    (h1, a1, u1, f1, x1), (w["w_in"],) = _ffn_fwd(
        x, small["norm_ffn1_g"], sc1, sh1, g1, w["ffn1_w_gate"], w["ffn1_w_up"], w["ffn1_w_down"], seq, "ffn1_fwd",
        comm=_GatherComm([slots["w_in"]]))
    w_in_full = w["w_in"].reshape(IN_WIDTH, D)
    q_end, v_end = D, D + 4 * HEAD_DIM
    w_in_cols = jnp.concatenate([w_in_full[:q_end], w_in_full[v_end:], w_in_full[q_end:v_end]], axis=0)
    (h2, proj), outs = _in_proj(x1, small["norm_mix_g"], sc2, sh2, w_in_cols, seq,
                                comm=_GatherComm([slots[n] for n in ("w_attn_o", "w_conv_o", "w_out")]))
    w_ao, w_co, w_o = [t.reshape(D, D) for t in outs]
    (o, lse), (w["ffn2_w_gate"], w["ffn2_w_up"]) = _attn_fwd(
        proj, small["attn_sinks"], B, seq, comm=_GatherComm([slots["ffn2_w_gate"], slots["ffn2_w_up"]]))
    (ydw, z), (w["ffn2_w_down"],) = _conv_fwd(
        proj, small["conv_w_dw"], small["conv_b_dw"], small["conv_ln_g"], small["conv_ln_b"], B, seq,
        comm=_GatherComm([slots["ffn2_w_down"]]))
    ya, yc, merged, mo, x2 = _merge(o, z, proj, w_ao, w_co, w_o, x1, g2, seq)
    (h3, a3, u3, f3, x3), _ = _ffn_fwd(x2, small["norm_ffn2_g"], sc3, sh3, g3, w["ffn2_w_gate"], w["ffn2_w_up"],
                                       w["ffn2_w_down"], seq, "ffn2_fwd")
    dx3, loss_parts, d_final_g = _final_loss(x3, small["final_norm_g"], target)

    red = _Reducer(core_idx, core_chip)

    def weight_grad(name, budget_us, a, a_spec, b, b_spec, rows, cols):
        red.add(name, red.run(_wgrad, budget_us, a, a_spec, b, b_spec, rows, cols, T, "dw_" + name))

    def ffn_backward(prefix, dxo, xin, h, a, u, f, gn, sc, gate):
        da, du, s, df, dx, dgate, dsc, dsh, dgn = red.run(
            _ffn_bwd, 170, dxo, xin, f, a, u, gn, sc, gate, w[prefix + "_w_gate"], w[prefix + "_w_up"],
            w[prefix + "_w_down"], seq, prefix + "_bwd")
        weight_grad(prefix + "_w_down", 38, s, _spec_chip_major(FF_SHARD), df, _spec_rows(D), FF_SHARD, D)
        weight_grad(prefix + "_w_gate", 38, da, _spec_chip_major(FF_SHARD), h, _spec_rows(D), FF_SHARD, D)
        weight_grad(prefix + "_w_up", 38, du, _spec_chip_major(FF_SHARD), h, _spec_rows(D), FF_SHARD, D)
        return dx, dgate, dsc, dsh, dgn

    dx2, dg3, dsc3, dsh3, d_gn3 = ffn_backward("ffn2", dx3, x2, h3, a3, u3, f3, small["norm_ffn2_g"], sc3, g3)

    dmo, dya, dyc, dga, dgc, do, dz, dg2 = red.run(_merge_bwd, 45, dx2, mo, g2, proj, ya, yc, w_o, w_ao, w_co, seq)
    shard = D // N_CHIP
    weight_grad("w_out", 24, merged, _spec_col_block(shard), dmo, _spec_rows(D), shard, D)
    weight_grad("w_attn_o", 24, o, _spec_col_block(shard), dya, _spec_rows(D), shard, D)
    weight_grad("w_conv_o", 24, z, _spec_col_block(shard), dyc, _spec_rows(D), shard, D)
    dq, dkp, dko, dvp, dvo, dsink_steps = red.run(_attn_bwd, 100, proj, small["attn_sinks"], o, do, lse, B, seq)
    dca, dcb, d_conv_w, d_conv_b, d_ln_g, d_ln_b = red.run(
        _conv_bwd, 165, proj, dz, ydw, small["conv_w_dw"], small["conv_ln_g"], small["conv_ln_b"], B, seq)

    def band_sum(own, prev):
        prev = prev.reshape(B, seq // BLOCK, BLOCK, 2 * HEAD_DIM)
        moved = jnp.concatenate([prev[:, 1:], jnp.zeros_like(prev[:, :1])], axis=1)
        return (own + moved.reshape(T, 2 * HEAD_DIM)).astype(BF16)

    dproj = jnp.concatenate([dq, band_sum(dko, dkp), band_sum(dvo, dvp), dca, dcb, dga, dgc], axis=1)
    dproj = dproj.reshape(T, N_CHIP, IN_SHARD).transpose(1, 0, 2)
    weight_grad("w_in", 60, dproj, _spec_chip_major(IN_SHARD), h2, _spec_rows(D), IN_SHARD, D)
    dx1, dsc2, dsh2, d_gn2 = red.run(_in_proj_bwd, 90, dproj, w["w_in"], x1, small["norm_mix_g"], sc2, dx2, seq)

    dx0, dg1, dsc1, dsh1, d_gn1 = ffn_backward("ffn1", dx1, x, h1, a1, u1, f1, small["norm_ffn1_g"], sc1, g1)

    dmod = jnp.concatenate([dsh1, dsc1, dg1, dsh2, dsc2, dg2, dsh3, dsc3, dg3], axis=1)
    d_sinks = jnp.sum(dsink_steps, axis=0)
    vec_grads = {"norm_ffn1_g": d_gn1, "norm_mix_g": d_gn2, "conv_b_dw": d_conv_b, "conv_ln_g": d_ln_g,
                 "conv_ln_b": d_ln_b, "norm_ffn2_g": d_gn3, "final_norm_g": d_final_g}
    return loss_parts, dx0, red, dmod, vec_grads, d_sinks, d_conv_w


def kernel(x, c, w_ada, b_ada, norm_ffn1_g, ffn1_w_gate, ffn1_w_up, ffn1_w_down, norm_mix_g, w_in, attn_sinks, w_attn_o, conv_w_dw, conv_b_dw, conv_ln_g, conv_ln_b, w_conv_o, w_out, norm_ffn2_g, ffn2_w_gate, ffn2_w_up, ffn2_w_down, final_norm_g, loss_target, m_w_ada, m_b_ada, m_norm_ffn1_g, m_ffn1_w_gate, m_ffn1_w_up, m_ffn1_w_down, m_norm_mix_g, m_w_in, m_attn_sinks, m_w_attn_o, m_conv_w_dw, m_conv_b_dw, m_conv_ln_g, m_conv_ln_b, m_w_conv_o, m_w_out, m_norm_ffn2_g, m_ffn2_w_gate, m_ffn2_w_up, m_ffn2_w_down, m_final_norm_g, v_w_ada, v_b_ada, v_norm_ffn1_g, v_ffn1_w_gate, v_ffn1_w_up, v_ffn1_w_down, v_norm_mix_g, v_w_in, v_attn_sinks, v_w_attn_o, v_conv_w_dw, v_conv_b_dw, v_conv_ln_g, v_conv_ln_b, v_w_conv_o, v_w_out, v_norm_ffn2_g, v_ffn2_w_gate, v_ffn2_w_up, v_ffn2_w_down, v_final_norm_g):
    args = dict(locals())
    B, seq, D = x.shape
    T = B * seq
    xi, yi, ci = _position()
    chip = 2 * xi + yi
    dev = 4 * xi + 2 * yi + ci

    def shard_2d(prefix, name):
        t = args[prefix + name][0]
        return t.T if name in COL_SHARDED else t

    big = {n: shard_2d("", n) for n in BIG_WEIGHTS}
    final_g = final_norm_g[None, :]
    vec_w = {n: (args[n] if n != "final_norm_g" else final_g) for n in VECTORS}

    conv_cols = D // N_CHIP
    conv_flat = jnp.pad(conv_w_dw[0].reshape(-1), (0, 8 * D - CONV_WIDTH * conv_cols)).reshape(8, D)
    first = _gather8(jnp.concatenate([jnp.pad(c, ((0, 8 - B), (0, 0))), conv_flat], axis=0), "gather_c")
    c_all = first[:, :B].reshape(N_DEV * B, D)
    conv_taps = first[::2, 8:].reshape(N_CHIP, 8 * D)[:, :CONV_WIDTH * conv_cols]
    conv_taps = conv_taps.reshape(N_CHIP, CONV_WIDTH, conv_cols).transpose(1, 0, 2).reshape(CONV_WIDTH, D)
    conv_taps = jnp.pad(conv_taps, ((0, CONV_PAD - CONV_WIDTH), (0, 0)))

    ada_cols = w_ada.shape[2]
    b_cols = lax.dynamic_slice(b_ada, (0, chip * ada_cols), (1, ada_cols))
    mod_part = _ada_fwd(c_all, w_ada[0], b_cols).reshape(N_DEV, B, ada_cols)
    mod = _mod_exchange(mod_part).transpose(1, 0, 2).reshape(B, N_MOD, D)

    core_idx = jnp.reshape(ci, (1,)).astype(jnp.int32)
    chip_idx = jnp.reshape(chip, (1,)).astype(jnp.int32)
    core_chip = jnp.stack([ci, chip]).astype(jnp.int32)
    slots = {n: _cast_slot(big[n], chip_idx, "cast_" + n) for n in BIG_WEIGHTS}

    small = dict(vec_w)
    small["attn_sinks"] = attn_sinks
    small["conv_w_dw"] = conv_taps

    loss_parts, dx, red, dmod, vec_grads, d_sinks, d_conv_w = _local_grads(
        x.reshape(T, D), loss_target.reshape(T, D), mod, slots, small, seq, core_idx, core_chip)

    loss = lax.psum((0.5 / D) * jnp.sum(loss_parts), ("x", "y", "c"))
    grad_x = dx.reshape(B, seq, D)
    out = {}

    block = jnp.zeros((SMALL_ROWS, D), F32)
    block = block.at[ROW_DMOD0:ROW_DMOD0 + N_MOD].set(dmod[0]).at[ROW_DMOD1:ROW_DMOD1 + N_MOD].set(dmod[1])
    block = block.at[ROW_VEC:ROW_VEC + len(VECTORS)].set(jnp.concatenate([vec_grads[n] for n in VECTORS], axis=0))
    block = block.at[ROW_SINK, :2 * HEAD_DIM].set(d_sinks[0])
    block = block.at[ROW_CONVW:ROW_CONVW + CONV_WIDTH].set(d_conv_w[:CONV_WIDTH])
    small_all = _gather8(block, "gather_small_grads")

    def pack_small(prefix):
        rows = [args[prefix + "b_ada"].reshape(N_MOD, D)]
        rows += [args[prefix + n].reshape(1, D) for n in VECTORS]
        rows += [jnp.pad(args[prefix + "attn_sinks"], ((0, 0), (0, D - N_Q_HEADS)))]
        return jnp.pad(jnp.concatenate(rows, axis=0), ((0, 24 - N_MOD - len(VECTORS) - 1), (0, 0)))

    small_sum, sg, sd, sm, sv = _small_adam(small_all, pack_small(""), pack_small("m_"), pack_small("v_"),
                                           ROW_DMOD0, ROW_DMOD1, ROW_VEC)

    def unpack_small(t):
        res = {"b_ada": t[:N_MOD].reshape(1, N_MOD * D)}
        for k, n in enumerate(VECTORS):
            res[n] = t[N_MOD + k].reshape(args[n].shape)
        res["attn_sinks"] = t[N_MOD + len(VECTORS), :N_Q_HEADS].reshape(1, N_Q_HEADS)
        return res

    unpacked = [unpack_small(t) for t in (sg, sd, sm, sv)]
    for n in ("b_ada", "attn_sinks") + VECTORS:
        out[n] = tuple(u[n] for u in unpacked)

    conv_g = lax.dynamic_slice(small_sum, (ROW_CONVW, chip * conv_cols), (CONV_WIDTH, conv_cols))
    d, mn, vn = red.run(_adam_call, 0, conv_w_dw[0], conv_g, m_conv_w_dw[0], v_conv_w_dw[0], "adam_conv_w_dw")
    out["conv_w_dw"] = tuple(t[None] for t in (conv_g, d, mn, vn))

    dmod_rows = jnp.stack([small_all[:, ROW_DMOD0:ROW_DMOD0 + N_MOD], small_all[:, ROW_DMOD1:ROW_DMOD1 + N_MOD]], axis=1)
    dmod_all = dmod_rows.reshape(N_DEV * B, N_MOD * D)
    dmod_cols = lax.dynamic_slice(dmod_all, (0, chip * ada_cols), (N_DEV * B, ada_cols))
    ada_out = red.run(_ada_adam, 35, c_all.T, dmod_cols, w_ada[0], m_w_ada[0], v_w_ada[0])
    out["w_ada"] = tuple(t[None] for t in ada_out)

    for n in FFN2_WEIGHTS + MIX_WEIGHTS[1:] + MIX_WEIGHTS[:1] + FFN1_WEIGHTS:
        while n not in red.reduced:
            red.step()
        g = red.reduced[n].reshape(big[n].shape)
        d, mn, vn = red.run(_adam_call, 0, big[n], g, shard_2d("m_", n), shard_2d("v_", n), "adam_" + n)
        out[n] = tuple((t.T if n in COL_SHARDED else t)[None] for t in (g, d, mn, vn))

    order = ("w_ada", "b_ada", "norm_ffn1_g", "ffn1_w_gate", "ffn1_w_up", "ffn1_w_down", "norm_mix_g", "w_in",
             "attn_sinks", "w_attn_o", "conv_w_dw", "conv_b_dw", "conv_ln_g", "conv_ln_b", "w_conv_o", "w_out",
             "norm_ffn2_g", "ffn2_w_gate", "ffn2_w_up", "ffn2_w_down", "final_norm_g")
    return (loss, grad_x, *[out[n][0] for n in order], *[out[n][1] for n in order],
            *[out[n][2] for n in order], *[out[n][3] for n in order])
```

```python
import functools

import jax
import jax.numpy as jnp
from jax import lax
from jax.experimental import pallas as pl
from jax.experimental.pallas import tpu as pltpu

F32 = jnp.float32
BF16 = jnp.bfloat16

D_MODEL = 1024
D_FF = 2816
N_CHIP = 4
N_DEV = 8
FF_SHARD = D_FF // N_CHIP
IN_WIDTH = 5376
IN_SHARD = IN_WIDTH // N_CHIP
HEAD_DIM = 64
N_Q_HEADS = 16
N_KV_HEADS = 2
BLOCK = 128
CONV_WIDTH = 31
CONV_PAD = 32
N_MOD = 9
EPS = 1e-6
FFN_RESIDUAL = 0.5
ATTN_SCALE = HEAD_DIM ** -0.5
MASK_VALUE = -1e30

ADAM_LR = 0.001
ADAM_B1 = 0.9
ADAM_B2 = 0.999
ADAM_EPS = 1e-08
ADAM_WD = 0.01
ADAM_STEP = 10

COLB_Q, COLB_CA, COLB_CB, COLB_GA, COLB_GC = 0, 1, 2, 3, 4
COLB_K, COLB_V = 40, 41
PROJ_TILE = 768

VMEM_LIMIT = 56 * 1024 * 1024
MESH = pl.DeviceIdType.MESH
ANY = pl.BlockSpec(memory_space=pl.ANY)
VMEM_SPEC = pl.BlockSpec(memory_space=pltpu.VMEM)
SMEM_SPEC = pl.BlockSpec(memory_space=pltpu.SMEM)


def _params(n_grid):
    return pltpu.CompilerParams(dimension_semantics=("arbitrary",) * n_grid, vmem_limit_bytes=VMEM_LIMIT)


def _tile(n, pref):
    t = min(n, pref)
    while n % t:
        t //= 2
    return t


def _row_tile(rows, cap):
    for t in range(min(rows, cap) // 16 * 16, 0, -16):
        if rows % t == 0:
            return t
    return rows


def _sigmoid(v):
    return 1.0 / (1.0 + jnp.exp(-v))


def _dot_nn(a, b):
    return lax.dot_general(a, b, (((1,), (0,)), ((), ())), preferred_element_type=F32)


def _dot_nt(a, b):
    return lax.dot_general(a, b, (((1,), (1,)), ((), ())), preferred_element_type=F32)


def _dot_tn(a, b):
    return lax.dot_general(a, b, (((0,), (0,)), ((), ())), preferred_element_type=F32)


ROW_CHUNK = 16


def _for_row_chunks(n_rows, fn):
    for r in range(0, n_rows, ROW_CHUNK):
        fn(slice(r, r + ROW_CHUNK))


def _norm_mod(xv, gn, sc, sh):
    r = lax.rsqrt(jnp.mean(xv * xv, axis=-1, keepdims=True) + EPS)
    return ((xv * r) * gn) * (1.0 + sc) + sh


def _accumulate(ref, first, value):
    @pl.when(first)
    def _():
        ref[...] = value

    @pl.when(jnp.logical_not(first))
    def _():
        ref[...] += value


def _norm_mod_bwd(dh, xv, gn, sc, dxo, first_of_batch, first, dx_ref, dsc_ref, dsh_ref, dgn_ref):
    r = lax.rsqrt(jnp.mean(xv * xv, axis=-1, keepdims=True) + EPS)
    xh = xv * r
    _accumulate(dsh_ref, first_of_batch, jnp.sum(dh, axis=0, keepdims=True))
    _accumulate(dsc_ref, first_of_batch, jnp.sum(dh * (xh * gn), axis=0, keepdims=True))
    dn = dh * (1.0 + sc)
    _accumulate(dgn_ref, first, jnp.sum(dn * xh, axis=0, keepdims=True))
    dxh = dn * gn
    dx_ref[...] = dxo + r * (dxh - xh * jnp.mean(dxh * xh, axis=-1, keepdims=True))


CHIP_FLIPS = ((1, 0), (0, 1), (1, 1))


def _position():
    return lax.axis_index("x"), lax.axis_index("y"), lax.axis_index("c")


def _flip(v, f):
    return 1 - v if f else v


class _GatherComm:
    def __init__(self, bufs):
        n = len(bufs)
        self.n = n
        self.operands = list(bufs)
        self.out_shape = [jax.ShapeDtypeStruct(b.shape, b.dtype) for b in bufs]
        self.aliases = {i: i for i in range(n)}
        self.sems = [pltpu.SemaphoreType.DMA((6 * n,)), pltpu.SemaphoreType.DMA((6 * n,))]
        self.rows = [b.shape[1] // 2 for b in bufs]

    def _half(self, ref, i, which):
        return ref.at[pl.ds(which * self.rows[i], self.rows[i]), :]

    def _ici(self, cins, couts, sems, i, k, dst_chip, to):
        x, y, c = _position()
        return pltpu.make_async_remote_copy(
            src_ref=self._half(cins[i].at[2 * x + y], i, c), dst_ref=self._half(couts[i].at[dst_chip], i, c),
            send_sem=sems[0].at[3 * i + k], recv_sem=sems[1].at[3 * i + k], device_id=to, device_id_type=MESH)

    def _d2d(self, couts, sems, i, k, src_chip, which):
        x, y, c = _position()
        place = self._half(couts[i].at[src_chip], i, which)
        return pltpu.make_async_remote_copy(
            src_ref=place, dst_ref=place, send_sem=sems[0].at[3 * self.n + 3 * i + k],
            recv_sem=sems[1].at[3 * self.n + 3 * i + k], device_id=(x, y, 1 - c), device_id_type=MESH)

    def _peers(self):
        x, y, _ = _position()
        return [(_flip(x, fx), _flip(y, fy)) for fx, fy in CHIP_FLIPS]

    def start(self, cins, couts, sems):
        x, y, c = _position()
        for i in range(self.n):
            for k, (px, py) in enumerate(self._peers()):
                self._ici(cins, couts, sems, i, k, 2 * x + y, (px, py, c)).start()

    def finish(self, cins, couts, sems):
        _, _, c = _position()
        peers = self._peers()
        for i in range(self.n):
            for k, (px, py) in enumerate(peers):
                self._ici(cins, couts, sems, i, k, 2 * px + py, (px, py, c)).wait_recv()
                self._d2d(couts, sems, i, k, 2 * px + py, c).start()
        for i in range(self.n):
            for k, (px, py) in enumerate(peers):
                self._d2d(couts, sems, i, k, 2 * px + py, 1 - c).wait_recv()
        for i in range(self.n):
            for k, (px, py) in enumerate(peers):
                self._ici(cins, couts, sems, i, k, 2 * px + py, (px, py, c)).wait_send()
                self._d2d(couts, sems, i, k, 2 * px + py, c).wait_send()


class _ExchangeComm:
    def __init__(self, pairs):
        n = len(pairs)
        self.n = n
        self.operands = list(pairs)
        self.out_shape = [jax.ShapeDtypeStruct((3,) + p.shape[1:], p.dtype) for p in pairs]
        self.aliases = {}
        self.sems = [pltpu.SemaphoreType.DMA((3 * n,)), pltpu.SemaphoreType.DMA((3 * n,))]

    def _copies(self, cins, couts, sems):
        x, y, c = _position()
        peers = [(_flip(x, fx), _flip(y, fy)) for fx, fy in CHIP_FLIPS]
        return [pltpu.make_async_remote_copy(
            src_ref=cins[i].at[2 * px + py], dst_ref=couts[i].at[k], send_sem=sems[0].at[3 * i + k],
            recv_sem=sems[1].at[3 * i + k], device_id=(px, py, c), device_id_type=MESH)
            for i in range(self.n) for k, (px, py) in enumerate(peers)]

    def start(self, cins, couts, sems):
        for cp in self._copies(cins, couts, sems):
            cp.start()

    def finish(self, cins, couts, sems):
        for cp in self._copies(cins, couts, sems):
            cp.wait()


class _SwapComm:
    def __init__(self, grads16):
        n = len(grads16)
        self.n = n
        self.operands = list(grads16)
        self.out_shape = [jax.ShapeDtypeStruct(g.shape[1:], g.dtype) for g in grads16]
        self.aliases = {}
        self.sems = [pltpu.SemaphoreType.DMA((n,)), pltpu.SemaphoreType.DMA((n,))]

    def _copies(self, cins, couts, sems):
        x, y, c = _position()
        return [pltpu.make_async_remote_copy(
            src_ref=cins[i].at[1 - c], dst_ref=couts[i], send_sem=sems[0].at[i], recv_sem=sems[1].at[i],
            device_id=(x, y, 1 - c), device_id_type=MESH) for i in range(self.n)]

    def start(self, cins, couts, sems):
        for cp in self._copies(cins, couts, sems):
            cp.start()

    def finish(self, cins, couts, sems):
        for cp in self._copies(cins, couts, sems):
            cp.wait()


class _JoinComm:
    def __init__(self, halves):
        n = len(halves)
        self.n = n
        self.operands = list(halves)
        self.out_shape = [jax.ShapeDtypeStruct(h.shape, h.dtype) for h in halves]
        self.aliases = {i: i for i in range(n)}
        self.sems = [pltpu.SemaphoreType.DMA((n,)), pltpu.SemaphoreType.DMA((n,))]

    def _copy(self, cins, couts, sems, i, which):
        x, y, c = _position()
        return pltpu.make_async_remote_copy(
            src_ref=cins[i].at[which], dst_ref=couts[i].at[which], send_sem=sems[0].at[i], recv_sem=sems[1].at[i],
            device_id=(x, y, 1 - c), device_id_type=MESH)

    def start(self, cins, couts, sems):
        _, _, c = _position()
        for i in range(self.n):
            self._copy(cins, couts, sems, i, c).start()

    def finish(self, cins, couts, sems):
        _, _, c = _position()
        for i in range(self.n):
            self._copy(cins, couts, sems, i, 1 - c).wait_recv()
        for i in range(self.n):
            self._copy(cins, couts, sems, i, c).wait_send()


class _CommList:
    def __init__(self, parts):
        self.parts = list(parts)
        self.operands = [t for p in self.parts for t in p.operands]
        self.out_shape = [t for p in self.parts for t in p.out_shape]
        self.sems = [t for p in self.parts for t in p.sems]
        self.aliases = {}
        n_in = n_out = 0
        for p in self.parts:
            self.aliases.update({n_in + i: n_out + j for i, j in p.aliases.items()})
            n_in += len(p.operands)
            n_out += len(p.out_shape)

    def _split(self, cins, couts, sems):
        pos = [0, 0, 0]
        for p in self.parts:
            sizes = (len(p.operands), len(p.out_shape), len(p.sems))
            yield p, tuple(seq[a:a + k] for seq, a, k in zip((cins, couts, sems), pos, sizes))
            pos = [a + k for a, k in zip(pos, sizes)]

    def start(self, cins, couts, sems):
        for p, refs in self._split(cins, couts, sems):
            p.start(*refs)

    def finish(self, cins, couts, sems):
        for p, refs in self._split(cins, couts, sems):
            p.finish(*refs)

    def split_outputs(self, outs):
        res, pos = [], 0
        for p in self.parts:
            res.append(outs[pos:pos + len(p.out_shape)])
            pos += len(p.out_shape)
        return res


def _call(body, *, name, grid, in_specs, out_specs, out_shape, operands, scratch_shapes=(), comm=None):
    n_grid = len(grid)
    if comm is None:
        return pl.pallas_call(
            body, name=name, grid=grid, in_specs=list(in_specs), out_specs=list(out_specs), out_shape=list(out_shape),
            scratch_shapes=list(scratch_shapes), compiler_params=_params(n_grid))(*operands), ()
    counts = (len(in_specs), len(comm.operands), len(out_specs), len(comm.out_shape), len(scratch_shapes),
              len(comm.sems))

    def fused(*refs):
        parts, pos = [], 0
        for k in counts:
            parts.append(refs[pos:pos + k])
            pos += k
        ins, cins, outs, couts, scr, sems = parts
        first = functools.reduce(jnp.logical_and, [pl.program_id(d) == 0 for d in range(n_grid)])
        last = functools.reduce(jnp.logical_and, [pl.program_id(d) == grid[d] - 1 for d in range(n_grid)])

        @pl.when(first)
        def _():
            comm.start(cins, couts, sems)

        body(*ins, *outs, *scr)

        @pl.when(last)
        def _():
            comm.finish(cins, couts, sems)

    res = pl.pallas_call(
        fused, name=name, grid=grid, in_specs=list(in_specs) + [ANY] * counts[1],
        out_specs=list(out_specs) + [ANY] * counts[3], out_shape=list(out_shape) + list(comm.out_shape),
        scratch_shapes=list(scratch_shapes) + list(comm.sems),
        input_output_aliases={counts[0] + i: counts[2] + j for i, j in comm.aliases.items()},
        compiler_params=_params(n_grid))(*operands, *comm.operands)
    return res[:counts[2]], res[counts[2]:]


def _run_comm(comm, name):
    k_in, k_out = len(comm.operands), len(comm.out_shape)

    def body(*refs):
        cins, couts, sems = refs[:k_in], refs[k_in:k_in + k_out], refs[k_in + k_out:]
        comm.start(cins, couts, sems)
        comm.finish(cins, couts, sems)

    return pl.pallas_call(
        body, name=name, in_specs=[ANY] * k_in, out_specs=[ANY] * k_out, out_shape=list(comm.out_shape),
        scratch_shapes=list(comm.sems), input_output_aliases=dict(comm.aliases))(*comm.operands)


def _ffn_fwd(x, gn, sc, sh, gate, wg, wu, wd, seq, name, comm=None):
    T, D = x.shape
    J, Fs, _ = wg.shape
    tm = _tile(seq, 1024)
    nb = seq // tm

    def body(x_ref, gn_ref, sc_ref, sh_ref, gate_ref, wg_ref, wu_ref, wd_ref,
             h_ref, a_ref, u_ref, f_ref, xo_ref, hs, acc, s16):
        j = pl.program_id(1)

        @pl.when(j == 0)
        def _():
            hb = _norm_mod(x_ref[...], gn_ref[...], sc_ref[...], sh_ref[...]).astype(BF16)
            hs[...] = hb
            h_ref[...] = hb
            acc[...] = jnp.zeros_like(acc)

        hb = hs[...]
        a_all = _dot_nt(hb, wg_ref[...])
        u_all = _dot_nt(hb, wu_ref[...])

        def swiglu_rows(rows):
            a = a_all[rows, :]
            u = u_all[rows, :]
            a_ref[rows, :] = a.astype(BF16)
            u_ref[rows, :] = u.astype(BF16)
            s16[rows, :] = ((a * _sigmoid(a)) * u).astype(BF16)

        _for_row_chunks(tm, swiglu_rows)
        acc[...] += _dot_nn(s16[...], wd_ref[...])

        @pl.when(j == J - 1)
        def _():
            f = acc[...]
            f_ref[...] = f.astype(BF16)
            xo_ref[...] = x_ref[...] + (FFN_RESIDUAL * gate_ref[...]) * f

    row = pl.BlockSpec((tm, D), lambda i, j: (i, 0))
    vec = pl.BlockSpec((1, D), lambda i, j: (0, 0))
    per_b = pl.BlockSpec((None, 1, D), lambda i, j: (i // nb, 0, 0))
    hid = pl.BlockSpec((None, tm, Fs), lambda i, j: (j, i, 0))
    return _call(
        body, name=name, grid=(T // tm, J),
        in_specs=[row, vec, per_b, per_b, per_b] + [pl.BlockSpec((None, Fs, D), lambda i, j: (j, 0, 0))] * 3,
        out_specs=[row, hid, hid, row, row],
        out_shape=[jax.ShapeDtypeStruct((T, D), BF16), jax.ShapeDtypeStruct((J, T, Fs), BF16),
                   jax.ShapeDtypeStruct((J, T, Fs), BF16), jax.ShapeDtypeStruct((T, D), BF16),
                   jax.ShapeDtypeStruct((T, D), F32)],
        scratch_shapes=[pltpu.VMEM((tm, D), BF16), pltpu.VMEM((tm, D), F32), pltpu.VMEM((tm, Fs), BF16)],
        operands=(x, gn, sc, sh, gate, wg, wu, wd), comm=comm)


def _ffn_bwd(dxo, x, f, a, u, gn, sc, gate, wg, wu, wd, seq, name, comm=None):
    T, D = x.shape
    J, Fs, _ = wg.shape
    B = T // seq
    tm = _tile(seq, 512)
    nb = seq // tm

    def body(dxo_ref, x_ref, f_ref, a_ref, u_ref, gn_ref, sc_ref, gate_ref, wg_ref, wu_ref, wd_ref,
             da_ref, du_ref, s_ref, df_ref, dx_ref, dgate_ref, dsc_ref, dsh_ref, dgn_ref, dfs, acc):
        i = pl.program_id(0)
        j = pl.program_id(1)
        first_of_batch = i % nb == 0

        @pl.when(j == 0)
        def _():
            dxo_v = dxo_ref[...]
            dfb = ((FFN_RESIDUAL * gate_ref[...]) * dxo_v).astype(BF16)
            dfs[...] = dfb
            df_ref[...] = dfb
            part = jnp.sum((FFN_RESIDUAL * f_ref[...].astype(F32)) * dxo_v, axis=0, keepdims=True)
            _accumulate(dgate_ref, first_of_batch, part)
            acc[...] = jnp.zeros_like(acc)

        ds_all = _dot_nt(dfs[...], wd_ref[...])

        def swiglu_bwd_rows(rows):
            ds = ds_all[rows, :]
            av = a_ref[rows, :].astype(F32)
            uv = u_ref[rows, :].astype(F32)
            sig = _sigmoid(av)
            sil = av * sig
            s_ref[rows, :] = (sil * uv).astype(BF16)
            da_ref[rows, :] = (ds * uv * (sig * (1.0 + av * (1.0 - sig)))).astype(BF16)
            du_ref[rows, :] = (ds * sil).astype(BF16)

        _for_row_chunks(tm, swiglu_bwd_rows)
        acc[...] += _dot_nn(da_ref[...], wg_ref[...]) + _dot_nn(du_ref[...], wu_ref[...])

        @pl.when(j == J - 1)
        def _():
            _norm_mod_bwd(acc[...], x_ref[...], gn_ref[...], sc_ref[...], dxo_ref[...],
                          first_of_batch, i == 0, dx_ref, dsc_ref, dsh_ref, dgn_ref)

    row = pl.BlockSpec((tm, D), lambda i, j: (i, 0))
    vec = pl.BlockSpec((1, D), lambda i, j: (0, 0))
    per_b = pl.BlockSpec((None, 1, D), lambda i, j: (i // nb, 0, 0))
    hid = pl.BlockSpec((None, tm, Fs), lambda i, j: (j, i, 0))
    hid_shape = jax.ShapeDtypeStruct((J, T, Fs), BF16)
    per_b_shape = jax.ShapeDtypeStruct((B, 1, D), F32)
    return _call(
        body, name=name, grid=(T // tm, J),
        in_specs=[row, row, row, hid, hid, vec, per_b, per_b]
        + [pl.BlockSpec((None, Fs, D), lambda i, j: (j, 0, 0))] * 3,
        out_specs=[hid, hid, hid, row, row, per_b, per_b, per_b, vec],
        out_shape=[hid_shape, hid_shape, hid_shape, jax.ShapeDtypeStruct((T, D), BF16),
                   jax.ShapeDtypeStruct((T, D), F32), per_b_shape, per_b_shape, per_b_shape,
                   jax.ShapeDtypeStruct((1, D), F32)],
        scratch_shapes=[pltpu.VMEM((tm, D), BF16), pltpu.VMEM((tm, D), F32)],
        operands=(dxo, x, f, a, u, gn, sc, gate, wg, wu, wd), comm=comm)


def _wgrad(a, a_spec, b, b_spec, rows, cols, n_tok, name, comm=None):
    tk = _tile(n_tok, 1024)
    nk = n_tok // tk
    half = rows // 2

    def body(a_ref, b_ref, o32_ref, o16_ref, acc):
        k = pl.program_id(1)

        @pl.when(k == 0)
        def _():
            acc[...] = jnp.zeros_like(acc)

        acc[...] += _dot_tn(a_ref[...], b_ref[...])

        @pl.when(k == nk - 1)
        def _():
            for h in range(2):
                v = acc[h * half:(h + 1) * half, :]
                o32_ref[h] = v
                o16_ref[h] = v.astype(BF16)

    out_spec = pl.BlockSpec((2, None, half, cols), lambda j, k: (0, j, 0, 0))
    return _call(
        body, name=name, grid=(N_CHIP, nk),
        in_specs=[a_spec(tk), b_spec(tk)],
        out_specs=[out_spec, out_spec],
        out_shape=[jax.ShapeDtypeStruct((2, N_CHIP, half, cols), F32),
                   jax.ShapeDtypeStruct((2, N_CHIP, half, cols), BF16)],
        scratch_shapes=[pltpu.VMEM((rows, cols), F32)],
        operands=(a, b), comm=comm)


def _spec_rows(width):
    return lambda tk: pl.BlockSpec((tk, width), lambda j, k: (k, 0))


def _spec_chip_major(width):
    return lambda tk: pl.BlockSpec((None, tk, width), lambda j, k: (j, k, 0))


def _spec_col_block(width):
    return lambda tk: pl.BlockSpec((tk, width), lambda j, k: (k, j))


def _in_proj(x, gn, sc, sh, w_in, seq, comm=None):
    T, D = x.shape
    N = w_in.shape[0]
    tm = _tile(seq, 1024)
    nb = seq // tm

    def body(x_ref, gn_ref, sc_ref, sh_ref, w_ref, h_ref, p_ref, hs):
        @pl.when(pl.program_id(1) == 0)
        def _():
            hb = _norm_mod(x_ref[...], gn_ref[...], sc_ref[...], sh_ref[...]).astype(BF16)
            hs[...] = hb
            h_ref[...] = hb

        p_ref[...] = _dot_nt(hs[...], w_ref[...]).astype(BF16)

    row = pl.BlockSpec((tm, D), lambda i, j: (i, 0))
    per_b = pl.BlockSpec((None, 1, D), lambda i, j: (i // nb, 0, 0))
    return _call(
        body, name="mix_in_proj", grid=(T // tm, N // PROJ_TILE),
        in_specs=[row, pl.BlockSpec((1, D), lambda i, j: (0, 0)), per_b, per_b,
                  pl.BlockSpec((PROJ_TILE, D), lambda i, j: (j, 0))],
        out_specs=[row, pl.BlockSpec((tm, PROJ_TILE), lambda i, j: (i, j))],
        out_shape=[jax.ShapeDtypeStruct((T, D), BF16), jax.ShapeDtypeStruct((T, N), BF16)],
        scratch_shapes=[pltpu.VMEM((tm, D), BF16)],
        operands=(x, gn, sc, sh, w_in), comm=comm)


def _attn_specs(nblk):
    def own(col):
        return lambda b, n: (b * nblk + n, col)

    def prev(col):
        return lambda b, n: (b * nblk + jnp.maximum(n - 1, 0), col)

    kv = (BLOCK, 2 * HEAD_DIM)
    return [pl.BlockSpec((BLOCK, D_MODEL), own(COLB_Q)),
            pl.BlockSpec(kv, prev(COLB_K)), pl.BlockSpec(kv, own(COLB_K)),
            pl.BlockSpec(kv, prev(COLB_V)), pl.BlockSpec(kv, own(COLB_V))]


def _band_operands(prev_ref, own_ref, lo):
    band = jnp.concatenate([prev_ref[...], own_ref[...]], axis=0).astype(F32)
    rolled = pltpu.roll(band, HEAD_DIM, 1)
    zero = jnp.zeros_like(band)
    head0 = jnp.concatenate([jnp.where(lo, band, zero), jnp.where(lo, zero, rolled)], axis=0).astype(BF16)
    head1 = jnp.concatenate([jnp.where(lo, rolled, zero), jnp.where(lo, zero, band)], axis=0).astype(BF16)
    return head0, head1


PAIRS_PER_KV = N_Q_HEADS // 2 // N_KV_HEADS
BAND = 2 * BLOCK


def _band_valid(has_prev):
    qi = lax.broadcasted_iota(jnp.int32, (PAIRS_PER_KV * BLOCK, BAND), 0) & (BLOCK - 1)
    sj = lax.broadcasted_iota(jnp.int32, (PAIRS_PER_KV * BLOCK, BAND), 1)
    rel = qi + BLOCK - sj
    return (rel >= 0) & (rel < BLOCK) & ((sj >= BLOCK) | has_prev)


def _pair_lanes(kvh, pp):
    pair = kvh * PAIRS_PER_KV + pp
    return slice(pair * 2 * HEAD_DIM, (pair + 1) * 2 * HEAD_DIM)


def _stack_pairs(ref, kvh):
    return jnp.concatenate([ref[:, _pair_lanes(kvh, pp)] for pp in range(PAIRS_PER_KV)], axis=0)


def _rows_per_pair(columns):
    return jnp.concatenate(columns, axis=0)


def _attn_fwd(proj, sinks, batch, seq, comm=None):
    T = proj.shape[0]
    nblk = seq // BLOCK

    def body(sink_ref, q_ref, kp_ref, ko_ref, vp_ref, vo_ref, o_ref, lse_ref):
        lo = lax.broadcasted_iota(jnp.int32, (1, 2 * HEAD_DIM), 1) < HEAD_DIM
        head_lane = lax.broadcasted_iota(jnp.int32, (1, N_Q_HEADS), 1)
        valid = _band_valid(pl.program_id(1) > 0)
        k_ops = _band_operands(kp_ref, ko_ref, lo)
        v_ops = _band_operands(vp_ref, vo_ref, lo)
        lse_all = jnp.zeros((BLOCK, N_Q_HEADS), F32)
        col = jnp.zeros((BLOCK, 1), F32)
        side0_row = lax.broadcasted_iota(jnp.int32, (2 * BAND, 2 * HEAD_DIM), 0) < BAND
        low_lane = lax.broadcasted_iota(jnp.int32, (2 * BAND, 2 * HEAD_DIM), 1) < HEAD_DIM
        side_ones = jnp.where(side0_row == low_lane, 1.0, 0.0).astype(BF16)
        for kvh in range(N_KV_HEADS):
            s_all = _dot_nt(_stack_pairs(q_ref, kvh), k_ops[kvh]) * ATTN_SCALE
            weights, maxes, sink_terms = [], [], []
            for side in range(2):
                heads = [2 * (kvh * PAIRS_PER_KV + pp) + side for pp in range(PAIRS_PER_KV)]
                sink = _rows_per_pair([col + sink_ref[0, h] for h in heads])
                s = jnp.where(valid, s_all[:, side * BAND:(side + 1) * BAND], MASK_VALUE)
                m = jnp.maximum(jnp.max(s, axis=-1, keepdims=True), sink)
                weights.append(jnp.where(valid, jnp.exp(s - m), 0.0).astype(BF16))
                maxes.append(m)
                sink_terms.append(jnp.exp(sink - m))
            p_all = jnp.concatenate(weights, axis=1)
            den = _dot_nn(p_all, side_ones) + jnp.where(lo, sink_terms[0], sink_terms[1])
            out = _dot_nn(p_all, v_ops[kvh]) / den
            for pp in range(PAIRS_PER_KV):
                o_ref[:, _pair_lanes(kvh, pp)] = out[pp * BLOCK:(pp + 1) * BLOCK].astype(BF16)
            for side in range(2):
                lse = maxes[side] + jnp.log(den[:, side * HEAD_DIM:side * HEAD_DIM + 1])
                for pp in range(PAIRS_PER_KV):
                    h = 2 * (kvh * PAIRS_PER_KV + pp) + side
                    lse_all = jnp.where(head_lane == h, lse[pp * BLOCK:(pp + 1) * BLOCK], lse_all)
        lse_ref[...] = lse_all

    return _call(
        body, name="attn_fwd", grid=(batch, nblk),
        in_specs=[SMEM_SPEC] + _attn_specs(nblk),
        out_specs=[pl.BlockSpec((BLOCK, D_MODEL), lambda b, n: (b * nblk + n, 0)),
                   pl.BlockSpec((BLOCK, N_Q_HEADS), lambda b, n: (b * nblk + n, 0))],
        out_shape=[jax.ShapeDtypeStruct((T, D_MODEL), BF16), jax.ShapeDtypeStruct((T, N_Q_HEADS), F32)],
        operands=(sinks, proj, proj, proj, proj, proj), comm=comm)


def _conv_u(ca, cb):
    return ca.astype(F32) * _sigmoid(cb.astype(F32))


def _conv_specs(ts, tiles_per_seq):
    per_tile = ts // CONV_PAD

    def tile(col):
        return lambda b, t: (b * tiles_per_seq + t, col)

    def before(col):
        return lambda b, t: (jnp.maximum((b * tiles_per_seq + t) * per_tile - 1, 0), col)

    return [pl.BlockSpec((ts, D_MODEL), tile(COLB_CA)), pl.BlockSpec((ts, D_MODEL), tile(COLB_CB)),
            pl.BlockSpec((CONV_PAD, D_MODEL), before(COLB_CA)), pl.BlockSpec((CONV_PAD, D_MODEL), before(COLB_CB))]


SUBLANES = 8


def _fill_upad(upad, ca_ref, cb_ref, cah_ref, cbh_ref, t):
    halo = _conv_u(cah_ref[...], cbh_ref[...])
    upad[0, 0:CONV_PAD, :] = jnp.where(t > 0, halo, jnp.zeros_like(halo))
    upad[0, CONV_PAD:, :] = _conv_u(ca_ref[...], cb_ref[...])


def _fill_shifted(pad):
    rows = pad.shape[1] - SUBLANES
    for b in range(1, SUBLANES):
        pad[b, 0:rows, :] = pad[0, b:b + rows, :]


def _shifted_rows(pad, offset, rows):
    b = offset % SUBLANES
    return pad[b, offset - b:offset - b + rows, :]


def _layernorm_stats(y):
    mu = jnp.mean(y, axis=-1, keepdims=True)
    yc = y - mu
    rstd = lax.rsqrt(jnp.mean(yc * yc, axis=-1, keepdims=True) + EPS)
    return yc * rstd, rstd


def _conv_fwd(proj, w_dw, b_dw, ln_g, ln_b, batch, seq, comm=None):
    T = proj.shape[0]
    ts = _tile(seq, 256)
    nt = seq // ts
    shift = CONV_PAD - (CONV_WIDTH - 1)

    def body(ca_ref, cb_ref, cah_ref, cbh_ref, w_ref, b_ref, g_ref, beta_ref, y_ref, z_ref, upad):
        _fill_upad(upad, ca_ref, cb_ref, cah_ref, cbh_ref, pl.program_id(1))
        _fill_shifted(upad)
        y = jnp.zeros((ts, D_MODEL), F32) + b_ref[...]
        for k in range(CONV_WIDTH):
            y = y + w_ref[k:k + 1, :] * _shifted_rows(upad, shift + k, ts)
        y_ref[...] = y
        lnh, _ = _layernorm_stats(y)
        ln = lnh * g_ref[...] + beta_ref[...]
        z_ref[...] = (ln * _sigmoid(ln)).astype(BF16)

    vec = pl.BlockSpec((1, D_MODEL), lambda b, t: (0, 0))
    row = pl.BlockSpec((ts, D_MODEL), lambda b, t: (b * nt + t, 0))
    return _call(
        body, name="conv_fwd", grid=(batch, nt),
        in_specs=_conv_specs(ts, nt) + [pl.BlockSpec((CONV_PAD, D_MODEL), lambda b, t: (0, 0)), vec, vec, vec],
        out_specs=[row, row],
        out_shape=[jax.ShapeDtypeStruct((T, D_MODEL), F32), jax.ShapeDtypeStruct((T, D_MODEL), BF16)],
        scratch_shapes=[pltpu.VMEM((SUBLANES, ts + CONV_PAD, D_MODEL), F32)],
        operands=(proj, proj, proj, proj, w_dw, b_dw, ln_g, ln_b), comm=comm)


def _merge(o, z, proj, w_ao, w_co, w_out, x, gate, seq):
    T, D = x.shape
    tm = _tile(seq, 512)
    nb = seq // tm

    def body(o_ref, z_ref, ga_ref, gc_ref, wao_ref, wco_ref, wout_ref, x_ref, gate_ref,
             ya_ref, yc_ref, mg_ref, mo_ref, xo_ref):
        ya = _dot_nn(o_ref[...], wao_ref[...])
        yc = _dot_nn(z_ref[...], wco_ref[...])
        ya_ref[...] = ya.astype(BF16)
        yc_ref[...] = yc.astype(BF16)
        merged = (_sigmoid(ga_ref[...].astype(F32)) * ya + _sigmoid(gc_ref[...].astype(F32)) * yc).astype(BF16)
        mg_ref[...] = merged
        mo = _dot_nn(merged, wout_ref[...])
        mo_ref[...] = mo.astype(BF16)
        xo_ref[...] = x_ref[...] + gate_ref[...] * mo

    row = pl.BlockSpec((tm, D), lambda i: (i, 0))
    mat = pl.BlockSpec((D, D), lambda i: (0, 0))
    act = jax.ShapeDtypeStruct((T, D), BF16)
    return pl.pallas_call(
        body, name="mix_merge", grid=(T // tm,),
        in_specs=[row, row, pl.BlockSpec((tm, D), lambda i: (i, COLB_GA)), pl.BlockSpec((tm, D), lambda i: (i, COLB_GC)),
                  mat, mat, mat, row, pl.BlockSpec((None, 1, D), lambda i: (i // nb, 0, 0))],
        out_specs=[row, row, row, row, row],
        out_shape=[act, act, act, act, jax.ShapeDtypeStruct((T, D), F32)],
        compiler_params=_params(1),
    )(o, z, proj, proj, w_ao, w_co, w_out, x, gate)


def _final_loss(x, gf, target):
    T, D = x.shape
    tm = _tile(T, 512)

    def body(x_ref, gf_ref, t_ref, dx_ref, lp_ref, dgf_ref):
        first = pl.program_id(0) == 0
        xv = x_ref[...]
        gfv = gf_ref[...]
        r = lax.rsqrt(jnp.mean(xv * xv, axis=-1, keepdims=True) + EPS)
        xh = xv * r
        err = xh * gfv - t_ref[...]
        _accumulate(lp_ref, first, jnp.sum(err * err, axis=0, keepdims=True))
        dy = err * (1.0 / D)
        _accumulate(dgf_ref, first, jnp.sum(dy * xh, axis=0, keepdims=True))
        dxh = dy * gfv
        dx_ref[...] = r * (dxh - xh * jnp.mean(dxh * xh, axis=-1, keepdims=True))

    row = pl.BlockSpec((tm, D), lambda i: (i, 0))
    vec = pl.BlockSpec((1, D), lambda i: (0, 0))
    return pl.pallas_call(
        body, name="final_loss", grid=(T // tm,),
        in_specs=[row, vec, row], out_specs=[row, vec, vec],
        out_shape=[jax.ShapeDtypeStruct((T, D), F32), jax.ShapeDtypeStruct((1, D), F32),
                   jax.ShapeDtypeStruct((1, D), F32)],
        compiler_params=_params(1),
    )(x, gf, target)


def _merge_bwd(dxo, mo, gate, proj, ya, yc, w_out, w_ao, w_co, seq, comm=None):
    T, D = dxo.shape
    B = T // seq
    tm = _tile(seq, 512)
    nb = seq // tm

    def body(dxo_ref, mo_ref, gate_ref, ga_ref, gc_ref, ya_ref, yc_ref, wout_ref, wao_ref, wco_ref,
             dmo_ref, dya_ref, dyc_ref, dga_ref, dgc_ref, do_ref, dz_ref, dgate_ref):
        dxo_v = dxo_ref[...]
        dmo = (gate_ref[...] * dxo_v).astype(BF16)
        dmo_ref[...] = dmo
        _accumulate(dgate_ref, pl.program_id(0) % nb == 0,
                    jnp.sum(mo_ref[...].astype(F32) * dxo_v, axis=0, keepdims=True))
        dm = _dot_nt(dmo, wout_ref[...])
        sa = _sigmoid(ga_ref[...].astype(F32))
        sc = _sigmoid(gc_ref[...].astype(F32))
        dya = (sa * dm).astype(BF16)
        dyc = (sc * dm).astype(BF16)
        dya_ref[...] = dya
        dyc_ref[...] = dyc
        dga_ref[...] = (dm * ya_ref[...].astype(F32) * (sa * (1.0 - sa))).astype(BF16)
        dgc_ref[...] = (dm * yc_ref[...].astype(F32) * (sc * (1.0 - sc))).astype(BF16)
        do_ref[...] = _dot_nt(dya, wao_ref[...]).astype(BF16)
        dz_ref[...] = _dot_nt(dyc, wco_ref[...]).astype(BF16)

    row = pl.BlockSpec((tm, D), lambda i: (i, 0))
    mat = pl.BlockSpec((D, D), lambda i: (0, 0))
    per_b = pl.BlockSpec((None, 1, D), lambda i: (i // nb, 0, 0))
    act = jax.ShapeDtypeStruct((T, D), BF16)
    return _call(
        body, name="mix_merge_bwd", grid=(T // tm,),
        in_specs=[row, row, per_b, pl.BlockSpec((tm, D), lambda i: (i, COLB_GA)),
                  pl.BlockSpec((tm, D), lambda i: (i, COLB_GC)), row, row, mat, mat, mat],
        out_specs=[row] * 7 + [per_b],
        out_shape=[act] * 7 + [jax.ShapeDtypeStruct((B, 1, D), F32)],
        operands=(dxo, mo, gate, proj, proj, ya, yc, w_out, w_ao, w_co), comm=comm)


def _attn_bwd(proj, sinks, o, do, lse, batch, seq, comm=None):
    T = proj.shape[0]
    nblk = seq // BLOCK
    n_steps = batch * nblk

    def body(sink_ref, q_ref, kp_ref, ko_ref, vp_ref, vo_ref, o_ref, do_ref, lse_ref,
             dq_ref, dkp_ref, dko_ref, dvp_ref, dvo_ref, dsink_ref):
        lo = lax.broadcasted_iota(jnp.int32, (1, 2 * HEAD_DIM), 1) < HEAD_DIM
        sink_lane = lax.broadcasted_iota(jnp.int32, (1, 2 * HEAD_DIM), 1)
        valid = _band_valid(pl.program_id(1) > 0)
        k_ops = _band_operands(kp_ref, ko_ref, lo)
        v_ops = _band_operands(vp_ref, vo_ref, lo)
        dsink = jnp.zeros((1, 2 * HEAD_DIM), F32)
        col = jnp.zeros((BLOCK, 1), F32)

        def fold(both):
            return (jnp.where(lo, both[:BAND], 0.0)
                    + pltpu.roll(jnp.where(lo, 0.0, both[BAND:]), HEAD_DIM, 1))

        dk_heads, dv_heads = [], []
        for kvh in range(N_KV_HEADS):
            q4 = _stack_pairs(q_ref, kvh)
            do4 = _stack_pairs(do_ref, kvh)
            dd = do4.astype(F32) * _stack_pairs(o_ref, kvh).astype(F32)
            s_all = _dot_nt(q4, k_ops[kvh]) * ATTN_SCALE
            dp_all = _dot_nt(do4, v_ops[kvh])
            ds_sides, p_sides = [], []
            for side in range(2):
                heads = [2 * (kvh * PAIRS_PER_KV + pp) + side for pp in range(PAIRS_PER_KV)]
                mine = lo if side == 0 else jnp.logical_not(lo)
                cols = slice(side * BAND, (side + 1) * BAND)
                sink = _rows_per_pair([col + sink_ref[0, h] for h in heads])
                lse = _rows_per_pair([lse_ref[:, h:h + 1] for h in heads])
                delta = jnp.sum(jnp.where(mine, dd, 0.0), axis=-1, keepdims=True)
                p = jnp.where(valid, jnp.exp(jnp.where(valid, s_all[:, cols], MASK_VALUE) - lse), 0.0)
                ds_sides.append((p * (dp_all[:, cols] - delta) * ATTN_SCALE).astype(BF16))
                p_sides.append(p.astype(BF16))
                sink_part = jnp.exp(sink - lse) * delta
                for pp, h in enumerate(heads):
                    dsink = dsink + jnp.where(sink_lane == h, -jnp.sum(sink_part[pp * BLOCK:(pp + 1) * BLOCK]), 0.0)
            ds_all = jnp.concatenate(ds_sides, axis=1)
            dq4 = _dot_nn(ds_all, k_ops[kvh])
            for pp in range(PAIRS_PER_KV):
                dq_ref[:, _pair_lanes(kvh, pp)] = dq4[pp * BLOCK:(pp + 1) * BLOCK].astype(BF16)
            dk_heads.append(fold(_dot_tn(ds_all, q4)))
            dv_heads.append(fold(_dot_tn(jnp.concatenate(p_sides, axis=1), do4)))
        dk = dk_heads[0] + pltpu.roll(dk_heads[1], HEAD_DIM, 1)
        dv = dv_heads[0] + pltpu.roll(dv_heads[1], HEAD_DIM, 1)
        dkp_ref[...] = dk[:BLOCK]
        dko_ref[...] = dk[BLOCK:]
        dvp_ref[...] = dv[:BLOCK]
        dvo_ref[...] = dv[BLOCK:]
        dsink_ref[...] = dsink

    def own(b, n):
        return (b * nblk + n, 0)

    row = pl.BlockSpec((BLOCK, D_MODEL), own)
    kv = pl.BlockSpec((BLOCK, 2 * HEAD_DIM), own)
    kv_shape = jax.ShapeDtypeStruct((T, 2 * HEAD_DIM), F32)
    return _call(
        body, name="attn_bwd", grid=(batch, nblk),
        in_specs=[SMEM_SPEC] + _attn_specs(nblk) + [row, row, pl.BlockSpec((BLOCK, N_Q_HEADS), own)],
        out_specs=[row, kv, kv, kv, kv, pl.BlockSpec((None, 1, 2 * HEAD_DIM), lambda b, n: (b * nblk + n, 0, 0))],
        out_shape=[jax.ShapeDtypeStruct((T, D_MODEL), BF16), kv_shape, kv_shape, kv_shape, kv_shape,
                   jax.ShapeDtypeStruct((n_steps, 1, 2 * HEAD_DIM), F32)],
        operands=(sinks, proj, proj, proj, proj, proj, o, do, lse), comm=comm)


def _conv_bwd(proj, dz, ydw, w_dw, ln_g, ln_b, batch, seq, comm=None):
    T = proj.shape[0]
    ts = _tile(seq, 256)
    nt = seq // ts
    per_tile = ts // CONV_PAD
    shift = CONV_PAD - (CONV_WIDTH - 1)

    def body(ca_ref, cb_ref, cah_ref, cbh_ref, dz_ref, dzn_ref, y_ref, yn_ref, w_ref, g_ref, beta_ref,
             dca_ref, dcb_ref, dw_ref, db_ref, dg_ref, dbeta_ref, upad, dypad):
        t = pl.program_id(1)
        first = (pl.program_id(0) == 0) & (t == 0)
        gv = g_ref[...]

        def ln_bwd(dzv, yv):
            lnh, rstd = _layernorm_stats(yv)
            ln = lnh * gv + beta_ref[...]
            sg = _sigmoid(ln)
            dln = dzv.astype(F32) * (sg * (1.0 + ln * (1.0 - sg)))
            dyh = dln * gv
            dy = rstd * (dyh - jnp.mean(dyh, axis=-1, keepdims=True)
                         - lnh * jnp.mean(dyh * lnh, axis=-1, keepdims=True))
            return dy, dln, lnh

        dy, dln, lnh = ln_bwd(dz_ref[...], y_ref[...])
        dy_next, _, _ = ln_bwd(dzn_ref[...], yn_ref[...])
        dypad[0, 0:ts, :] = dy
        dypad[0, ts:, :] = jnp.where(t < nt - 1, dy_next, jnp.zeros_like(dy_next))
        _fill_shifted(dypad)
        _fill_upad(upad, ca_ref, cb_ref, cah_ref, cbh_ref, t)
        _fill_shifted(upad)

        _accumulate(dg_ref, first, jnp.sum(dln * lnh, axis=0, keepdims=True))
        _accumulate(dbeta_ref, first, jnp.sum(dln, axis=0, keepdims=True))
        _accumulate(db_ref, first, jnp.sum(dy, axis=0, keepdims=True))

        @pl.when(first)
        def _():
            dw_ref[...] = jnp.zeros_like(dw_ref)

        du = jnp.zeros((ts, D_MODEL), F32)
        for k in range(CONV_WIDTH):
            du = du + w_ref[k:k + 1, :] * _shifted_rows(dypad, CONV_WIDTH - 1 - k, ts)
            dw_ref[k:k + 1, :] += jnp.sum(dy * _shifted_rows(upad, shift + k, ts), axis=0, keepdims=True)
        cav = ca_ref[...].astype(F32)
        sb = _sigmoid(cb_ref[...].astype(F32))
        dca_ref[...] = (du * sb).astype(BF16)
        dcb_ref[...] = (du * cav * (sb * (1.0 - sb))).astype(BF16)

    def tile(b, t):
        return (b * nt + t, 0)

    def after(b, t):
        return (jnp.minimum((b * nt + t + 1) * per_tile, T // CONV_PAD - 1), 0)

    row = pl.BlockSpec((ts, D_MODEL), tile)
    halo = pl.BlockSpec((CONV_PAD, D_MODEL), after)
    vec = pl.BlockSpec((1, D_MODEL), lambda b, t: (0, 0))
    wspec = pl.BlockSpec((CONV_PAD, D_MODEL), lambda b, t: (0, 0))
    act = jax.ShapeDtypeStruct((T, D_MODEL), BF16)
    vec_shape = jax.ShapeDtypeStruct((1, D_MODEL), F32)
    return _call(
        body, name="conv_bwd", grid=(batch, nt),
        in_specs=_conv_specs(ts, nt) + [row, halo, row, halo, wspec, vec, vec],
        out_specs=[row, row, wspec, vec, vec, vec],
        out_shape=[act, act, jax.ShapeDtypeStruct((CONV_PAD, D_MODEL), F32), vec_shape, vec_shape, vec_shape],
        scratch_shapes=[pltpu.VMEM((SUBLANES, ts + CONV_PAD, D_MODEL), F32)] * 2,
        operands=(proj, proj, proj, proj, dz, dz, ydw, ydw, w_dw, ln_g, ln_b), comm=comm)


def _in_proj_bwd(dproj, w_in_g, x, gn, sc, dxo, seq, comm=None):
    T, D = x.shape
    J, W, _ = w_in_g.shape
    B = T // seq
    tm = _tile(seq, 512)
    nb = seq // tm

    def body(dp_ref, w_ref, x_ref, gn_ref, sc_ref, dxo_ref, dx_ref, dsc_ref, dsh_ref, dgn_ref, acc):
        i = pl.program_id(0)
        j = pl.program_id(1)

        @pl.when(j == 0)
        def _():
            acc[...] = jnp.zeros_like(acc)

        acc[...] += _dot_nn(dp_ref[...], w_ref[...])

        @pl.when(j == J - 1)
        def _():
            _norm_mod_bwd(acc[...], x_ref[...], gn_ref[...], sc_ref[...], dxo_ref[...],
                          i % nb == 0, i == 0, dx_ref, dsc_ref, dsh_ref, dgn_ref)

    row = pl.BlockSpec((tm, D), lambda i, j: (i, 0))
    vec = pl.BlockSpec((1, D), lambda i, j: (0, 0))
    per_b = pl.BlockSpec((None, 1, D), lambda i, j: (i // nb, 0, 0))
    per_b_shape = jax.ShapeDtypeStruct((B, 1, D), F32)
    return _call(
        body, name="mix_in_proj_bwd", grid=(T // tm, J),
        in_specs=[pl.BlockSpec((None, tm, W), lambda i, j: (j, i, 0)),
                  pl.BlockSpec((None, W, D), lambda i, j: (j, 0, 0)), row, vec, per_b, row],
        out_specs=[row, per_b, per_b, vec],
        out_shape=[jax.ShapeDtypeStruct((T, D), F32), per_b_shape, per_b_shape, jax.ShapeDtypeStruct((1, D), F32)],
        scratch_shapes=[pltpu.VMEM((tm, D), F32)],
        operands=(dproj, w_in_g, x, gn, sc, dxo), comm=comm)


def _ada_fwd(c_all, w_ada, b_cols):
    nbatch, D = c_all.shape
    N = w_ada.shape[1]
    tn = _tile(N, 768)

    def body(c_ref, w_ref, b_ref, o_ref):
        cv = c_ref[...]
        act = (cv * _sigmoid(cv)).astype(BF16)
        o_ref[...] = _dot_nn(act, w_ref[...].astype(BF16)) + b_ref[...]

    return pl.pallas_call(
        body, name="ada_fwd", grid=(N // tn,),
        in_specs=[pl.BlockSpec((nbatch, D), lambda j: (0, 0)), pl.BlockSpec((D, tn), lambda j: (0, j)),
                  pl.BlockSpec((1, tn), lambda j: (0, j))],
        out_specs=pl.BlockSpec((nbatch, tn), lambda j: (0, j)),
        out_shape=jax.ShapeDtypeStruct((nbatch, N), F32),
        compiler_params=_params(1),
    )(c_all, w_ada, b_cols)


def _adamw(w, g, m, v):
    m = ADAM_B1 * m + (1.0 - ADAM_B1) * g
    v = ADAM_B2 * v + (1.0 - ADAM_B2) * (g * g)
    m_hat = m / (1.0 - ADAM_B1 ** ADAM_STEP)
    v_hat = v / (1.0 - ADAM_B2 ** ADAM_STEP)
    delta = -ADAM_LR * (m_hat / (jnp.sqrt(v_hat) + ADAM_EPS) + ADAM_WD * w)
    return delta, m, v


def _adam_call(w, g, m, v, name, comm=None):
    R, C = w.shape
    tr = _row_tile(R, 512)

    def body(w_ref, g_ref, m_ref, v_ref, d_ref, mo_ref, vo_ref):
        d, mn, vn = _adamw(w_ref[...], g_ref[...], m_ref[...], v_ref[...])
        d_ref[...] = d
        mo_ref[...] = mn
        vo_ref[...] = vn

    blk = pl.BlockSpec((tr, C), lambda i: (i, 0))
    shape = jax.ShapeDtypeStruct((R, C), F32)
    return _call(body, name=name, grid=(R // tr,), in_specs=[blk] * 4, out_specs=[blk] * 3, out_shape=[shape] * 3,
                 operands=(w, g, m, v), comm=comm)


def _ada_adam(c_act_t, dmod_cols, w, m, v, comm):
    R, C = w.shape
    nbatch = c_act_t.shape[1]
    tr = _tile(R, 128)

    def body(ct_ref, dm_ref, w_ref, m_ref, v_ref, g_ref, d_ref, mo_ref, vo_ref):
        cv = ct_ref[...]
        g = _dot_nn((cv * _sigmoid(cv)).astype(BF16), dm_ref[...].astype(BF16))
        g_ref[...] = g
        d, mn, vn = _adamw(w_ref[...], g, m_ref[...], v_ref[...])
        d_ref[...] = d
        mo_ref[...] = mn
        vo_ref[...] = vn

    blk = pl.BlockSpec((tr, C), lambda i: (i, 0))
    shape = jax.ShapeDtypeStruct((R, C), F32)
    return _call(
        body, name="ada_adam", grid=(R // tr,),
        in_specs=[pl.BlockSpec((tr, nbatch), lambda i: (i, 0)), pl.BlockSpec((nbatch, C), lambda i: (0, 0)),
                  blk, blk, blk],
        out_specs=[blk] * 4, out_shape=[shape] * 4,
        operands=(c_act_t, dmod_cols, w, m, v), comm=comm)


def _small_adam(gathered, w, m, v, rows_b0, rows_b1, rows_vec):
    _, P, D = gathered.shape
    R = w.shape[0]

    def body(ga_ref, w_ref, m_ref, v_ref, sum_ref, g_ref, d_ref, mo_ref, vo_ref):
        total = ga_ref[0]
        for dev in range(1, N_DEV):
            total = total + ga_ref[dev]
        sum_ref[...] = total
        g_ref[...] = jnp.zeros_like(g_ref)
        g_ref[0:N_MOD, :] = (sum_ref[rows_b0:rows_b0 + N_MOD, :] + sum_ref[rows_b1:rows_b1 + N_MOD, :])
        g_ref[N_MOD:N_MOD + 8, :] = sum_ref[rows_vec:rows_vec + 8, :]
        d, mn, vn = _adamw(w_ref[...], g_ref[...], m_ref[...], v_ref[...])
        d_ref[...] = d
        mo_ref[...] = mn
        vo_ref[...] = vn

    shape = jax.ShapeDtypeStruct((R, D), F32)
    return pl.pallas_call(
        body, name="small_adam",
        in_specs=[VMEM_SPEC] * 4, out_specs=[VMEM_SPEC] * 5,
        out_shape=[jax.ShapeDtypeStruct((P, D), F32), shape, shape, shape, shape],
        compiler_params=pltpu.CompilerParams(vmem_limit_bytes=VMEM_LIMIT),
    )(gathered, w, m, v)


def _gather8(v, name):
    A, W = v.shape
    flips = [(fx, fy, fc) for fx in (0, 1) for fy in (0, 1) for fc in (0, 1) if (fx, fy, fc) != (0, 0, 0)]

    def body(v_ref, out_ref, send_sems, recv_sems, local_sem):
        x, y, c = _position()
        me = 4 * x + 2 * y + c
        mine = pltpu.make_async_copy(v_ref, out_ref.at[me], local_sem)
        mine.start()

        def copy(k, block, to):
            return pltpu.make_async_remote_copy(src_ref=v_ref, dst_ref=out_ref.at[block], send_sem=send_sems.at[k],
                                                recv_sem=recv_sems.at[k], device_id=to, device_id_type=MESH)

        peers = [(_flip(x, fx), _flip(y, fy), _flip(c, fc)) for fx, fy, fc in flips]
        sends = [copy(k, me, peer) for k, peer in enumerate(peers)]
        for cp in sends:
            cp.start()
        for k, (px, py, pc) in enumerate(peers):
            copy(k, 4 * px + 2 * py + pc, (px, py, pc)).wait_recv()
        for cp in sends:
            cp.wait_send()
        mine.wait()

    return pl.pallas_call(
        body, name=name, in_specs=[VMEM_SPEC], out_specs=VMEM_SPEC,
        out_shape=jax.ShapeDtypeStruct((N_DEV, A, W), v.dtype),
        scratch_shapes=[pltpu.SemaphoreType.DMA((N_DEV - 1,)), pltpu.SemaphoreType.DMA((N_DEV - 1,)),
                        pltpu.SemaphoreType.DMA],
    )(v)


def _mod_exchange(part):
    _, A, W = part.shape

    def body(p_ref, out_ref, send_sems, recv_sems, local_sem):
        x, y, c = _position()
        me = 4 * x + 2 * y + c
        chip = 2 * x + y
        mine = pltpu.make_async_copy(p_ref.at[me], out_ref.at[chip], local_sem)
        mine.start()
        peers = [(_flip(x, fx), _flip(y, fy)) for fx, fy in CHIP_FLIPS]
        sends = []
        for k, (px, py) in enumerate(peers):
            sends.append(pltpu.make_async_remote_copy(
                src_ref=p_ref.at[4 * px + 2 * py + c], dst_ref=out_ref.at[chip], send_sem=send_sems.at[k],
                recv_sem=recv_sems.at[k], device_id=(px, py, c), device_id_type=MESH))
        for cp in sends:
            cp.start()
        for k, (px, py) in enumerate(peers):
            pltpu.make_async_remote_copy(
                src_ref=p_ref.at[me], dst_ref=out_ref.at[2 * px + py], send_sem=send_sems.at[k],
                recv_sem=recv_sems.at[k], device_id=(px, py, c), device_id_type=MESH).wait_recv()
        for cp in sends:
            cp.wait_send()
        mine.wait()

    return pl.pallas_call(
        body, name="mod_exchange", in_specs=[VMEM_SPEC], out_specs=VMEM_SPEC,
        out_shape=jax.ShapeDtypeStruct((N_CHIP, A, W), part.dtype),
        scratch_shapes=[pltpu.SemaphoreType.DMA((3,)), pltpu.SemaphoreType.DMA((3,)), pltpu.SemaphoreType.DMA],
    )(part)


def _cast_slot(w, chip_idx, name):
    R, C = w.shape
    tr = _row_tile(R, 512)

    def body(chip_ref, w_ref, o_ref):
        o_ref[...] = w_ref[...].astype(BF16)

    return pl.pallas_call(
        body, name=name,
        grid_spec=pltpu.PrefetchScalarGridSpec(
            num_scalar_prefetch=1, grid=(R // tr,),
            in_specs=[pl.BlockSpec((tr, C), lambda i, chip_ref: (i, 0))],
            out_specs=pl.BlockSpec((None, tr, C), lambda i, chip_ref: (chip_ref[0], i, 0))),
        out_shape=jax.ShapeDtypeStruct((N_CHIP, R, C), BF16),
        compiler_params=_params(1),
    )(chip_idx, w)


def _pair_sum(g32, recv, core, name):
    _, J, r, C = g32.shape

    def body(core_ref, g_ref, r_ref, o_ref):
        o_ref[...] = (g_ref[...] + r_ref[...].astype(F32)).astype(BF16)

    return pl.pallas_call(
        body, name=name,
        grid_spec=pltpu.PrefetchScalarGridSpec(
            num_scalar_prefetch=1, grid=(J,),
            in_specs=[pl.BlockSpec((None, None, r, C), lambda j, core_ref: (core_ref[0], j, 0, 0)),
                      pl.BlockSpec((None, r, C), lambda j, core_ref: (j, 0, 0))],
            out_specs=pl.BlockSpec((None, r, C), lambda j, core_ref: (j, 0, 0))),
        out_shape=jax.ShapeDtypeStruct((J, r, C), BF16),
        compiler_params=_params(1),
    )(core, g32, recv)


def _chip_sum(g32, recv_sib, recv_chips, core_chip, name):
    _, J, r, C = g32.shape

    def body(idx_ref, g_ref, s_ref, o_ref_in, o_ref):
        total = g_ref[...] + s_ref[...].astype(F32)
        for k in range(3):
            total = total + o_ref_in[k].astype(F32)
        o_ref[...] = total

    return pl.pallas_call(
        body, name=name,
        grid_spec=pltpu.PrefetchScalarGridSpec(
            num_scalar_prefetch=1, grid=(1,),
            in_specs=[pl.BlockSpec((None, None, r, C), lambda i, idx: (idx[0], idx[1], 0, 0)),
                      pl.BlockSpec((None, r, C), lambda i, idx: (idx[1], 0, 0)),
                      pl.BlockSpec((3, r, C), lambda i, idx: (0, 0, 0))],
            out_specs=pl.BlockSpec((None, r, C), lambda i, idx: (idx[0], 0, 0))),
        out_shape=jax.ShapeDtypeStruct((2, r, C), F32),
        compiler_params=_params(1),
    )(core_chip, g32, recv_sib, recv_chips)


ICI_US_PER_ELEMENT = 4.6e-5


class _Reducer:
    def __init__(self, core_idx, core_chip):
        self.core_idx, self.core_chip = core_idx, core_chip
        self.grads, self.halves, self.reduced = {}, {}, {}
        self.ready_swap, self.ready_exchange, self.ready_join = [], [], []
        self.inflight, self.current = ([], [], []), None
        self.flushes = 0

    def add(self, name, grad_pair):
        self.grads[name] = grad_pair
        self.ready_swap.append(name)

    def comm(self, budget_us):
        swaps, self.ready_swap = self.ready_swap, []
        joins, self.ready_join = self.ready_join, []
        exchanges, waiting = [], []
        for item in self.ready_exchange:
            cost = ICI_US_PER_ELEMENT * 2 * item[2].shape[1] * item[2].shape[2]
            if cost <= budget_us:
                exchanges.append(item)
                budget_us -= cost
            else:
                waiting.append(item)
        self.ready_exchange = waiting
        parts = []
        if swaps:
            parts.append(_SwapComm([self.grads[n][1] for n in swaps]))
        if exchanges:
            parts.append(_ExchangeComm([pair for _, _, pair in exchanges]))
        if joins:
            parts.append(_JoinComm([self.halves[n] for n in joins]))
        self.inflight = (swaps, exchanges, joins)
        self.current = _CommList(parts) if parts else None
        return self.current

    def done(self, comm_outs):
        if self.current is None:
            return
        swaps, exchanges, joins = self.inflight
        outs = iter(self.current.split_outputs(list(comm_outs)))
        if swaps:
            for n, recv in zip(swaps, next(outs)):
                pair = _pair_sum(self.grads[n][0], recv, self.core_idx, "pair_sum_" + n)
                self.ready_exchange.append((n, recv, pair))
        if exchanges:
            for (n, recv, _), chips in zip(exchanges, next(outs)):
                self.halves[n] = _chip_sum(self.grads[n][0], recv, chips, self.core_chip, "chip_sum_" + n)
                self.ready_join.append(n)
        if joins:
            self.reduced.update(zip(joins, next(outs)))
        self.current = None

    def run(self, kernel, budget_us, *args, **kwargs):
        if budget_us is None:
            return kernel(*args, comm=None, **kwargs)[0]
        outs, comm_outs = kernel(*args, comm=self.comm(budget_us), **kwargs)
        self.done(comm_outs)
        return outs

    def step(self):
        comm = self.comm(float("inf"))
        self.flushes += 1
        self.done(_run_comm(comm, "grad_reduce_tail_%d" % self.flushes))


BIG_WEIGHTS = ("ffn1_w_gate", "ffn1_w_up", "ffn1_w_down", "w_in", "w_attn_o", "w_conv_o", "w_out",
               "ffn2_w_gate", "ffn2_w_up", "ffn2_w_down")
VECTORS = ("norm_ffn1_g", "norm_mix_g", "conv_b_dw", "conv_ln_g", "conv_ln_b", "norm_ffn2_g", "final_norm_g")
ROW_DMOD0, ROW_DMOD1, ROW_VEC, ROW_SINK, ROW_CONVW, SMALL_ROWS = 0, 16, 33, 40, 41, 72


FFN1_WEIGHTS = ("ffn1_w_gate", "ffn1_w_up", "ffn1_w_down")
FFN2_WEIGHTS = ("ffn2_w_gate", "ffn2_w_up", "ffn2_w_down")
MIX_WEIGHTS = ("w_in", "w_attn_o", "w_conv_o", "w_out")
COL_SHARDED = ("ffn1_w_gate", "ffn1_w_up", "ffn2_w_gate", "ffn2_w_up", "w_in")


def _local_grads(x, target, mod, slots, small, seq, core_idx, core_chip):
    T, D = x.shape
    B = T // seq
    mods = [mod[:, k][:, None, :] for k in range(N_MOD)]
    sh1, sc1, g1, sh2, sc2, g2, sh3, sc3, g3 = mods
    w = dict(zip(FFN1_WEIGHTS, _run_comm(_GatherComm([slots[n] for n in FFN1_WEIGHTS]), "gather_ffn1")))

    (h1, a1, u1, f1, x1), (w["w_in"],) = _ffn_fwd(
        x, small["norm_ffn1_g"], sc1, sh1, g1, w["ffn1_w_gate"], w["ffn1_w_up"], w["ffn1_w_down"], seq, "ffn1_fwd",
        comm=_GatherComm([slots["w_in"]]))
    w_in_full = w["w_in"].reshape(IN_WIDTH, D)
    q_end, v_end = D, D + 4 * HEAD_DIM
    w_in_cols = jnp.concatenate([w_in_full[:q_end], w_in_full[v_end:], w_in_full[q_end:v_end]], axis=0)
    (h2, proj), outs = _in_proj(x1, small["norm_mix_g"], sc2, sh2, w_in_cols, seq,
                                comm=_GatherComm([slots[n] for n in ("w_attn_o", "w_conv_o", "w_out")]))
    w_ao, w_co, w_o = [t.reshape(D, D) for t in outs]
    (o, lse), (w["ffn2_w_gate"], w["ffn2_w_up"]) = _attn_fwd(
        proj, small["attn_sinks"], B, seq, comm=_GatherComm([slots["ffn2_w_gate"], slots["ffn2_w_up"]]))
    (ydw, z), (w["ffn2_w_down"],) = _conv_fwd(
        proj, small["conv_w_dw"], small["conv_b_dw"], small["conv_ln_g"], small["conv_ln_b"], B, seq,
        comm=_GatherComm([slots["ffn2_w_down"]]))
    ya, yc, merged, mo, x2 = _merge(o, z, proj, w_ao, w_co, w_o, x1, g2, seq)
    (h3, a3, u3, f3, x3), _ = _ffn_fwd(x2, small["norm_ffn2_g"], sc3, sh3, g3, w["ffn2_w_gate"], w["ffn2_w_up"],
                                       w["ffn2_w_down"], seq, "ffn2_fwd")
    dx3, loss_parts, d_final_g = _final_loss(x3, small["final_norm_g"], target)

    red = _Reducer(core_idx, core_chip)

    def weight_grad(name, budget_us, a, a_spec, b, b_spec, rows, cols):
        red.add(name, red.run(_wgrad, budget_us, a, a_spec, b, b_spec, rows, cols, T, "dw_" + name))

    def ffn_backward(prefix, dw_budget_us, dxo, xin, h, a, u, f, gn, sc, gate):
        da, du, s, df, dx, dgate, dsc, dsh, dgn = red.run(
            _ffn_bwd, 170, dxo, xin, f, a, u, gn, sc, gate, w[prefix + "_w_gate"], w[prefix + "_w_up"],
            w[prefix + "_w_down"], seq, prefix + "_bwd")
        weight_grad(prefix + "_w_down", dw_budget_us, s, _spec_chip_major(FF_SHARD), df, _spec_rows(D), FF_SHARD, D)
        weight_grad(prefix + "_w_gate", dw_budget_us, da, _spec_chip_major(FF_SHARD), h, _spec_rows(D), FF_SHARD, D)
        weight_grad(prefix + "_w_up", dw_budget_us, du, _spec_chip_major(FF_SHARD), h, _spec_rows(D), FF_SHARD, D)
        return dx, dgate, dsc, dsh, dgn

    dx2, dg3, dsc3, dsh3, d_gn3 = ffn_backward("ffn2", None, dx3, x2, h3, a3, u3, f3, small["norm_ffn2_g"], sc3, g3)

    dmo, dya, dyc, dga, dgc, do, dz, dg2 = red.run(_merge_bwd, 45, dx2, mo, g2, proj, ya, yc, w_o, w_ao, w_co, seq)
    shard = D // N_CHIP
    weight_grad("w_out", None, merged, _spec_col_block(shard), dmo, _spec_rows(D), shard, D)
    weight_grad("w_attn_o", None, o, _spec_col_block(shard), dya, _spec_rows(D), shard, D)
    weight_grad("w_conv_o", None, z, _spec_col_block(shard), dyc, _spec_rows(D), shard, D)
    dq, dkp, dko, dvp, dvo, dsink_steps = red.run(_attn_bwd, 100, proj, small["attn_sinks"], o, do, lse, B, seq)
    dca, dcb, d_conv_w, d_conv_b, d_ln_g, d_ln_b = red.run(
        _conv_bwd, 165, proj, dz, ydw, small["conv_w_dw"], small["conv_ln_g"], small["conv_ln_b"], B, seq)

    def band_sum(own, prev):
        prev = prev.reshape(B, seq // BLOCK, BLOCK, 2 * HEAD_DIM)
        moved = jnp.concatenate([prev[:, 1:], jnp.zeros_like(prev[:, :1])], axis=1)
        return (own + moved.reshape(T, 2 * HEAD_DIM)).astype(BF16)

    dproj = jnp.concatenate([dq, band_sum(dko, dkp), band_sum(dvo, dvp), dca, dcb, dga, dgc], axis=1)
    dproj = dproj.reshape(T, N_CHIP, IN_SHARD).transpose(1, 0, 2)
    weight_grad("w_in", 60, dproj, _spec_chip_major(IN_SHARD), h2, _spec_rows(D), IN_SHARD, D)
    dx1, dsc2, dsh2, d_gn2 = red.run(_in_proj_bwd, 90, dproj, w["w_in"], x1, small["norm_mix_g"], sc2, dx2, seq)

    dx0, dg1, dsc1, dsh1, d_gn1 = ffn_backward("ffn1", 38, dx1, x, h1, a1, u1, f1, small["norm_ffn1_g"], sc1, g1)

    dmod = jnp.concatenate([dsh1, dsc1, dg1, dsh2, dsc2, dg2, dsh3, dsc3, dg3], axis=1)
    d_sinks = jnp.sum(dsink_steps, axis=0)
    vec_grads = {"norm_ffn1_g": d_gn1, "norm_mix_g": d_gn2, "conv_b_dw": d_conv_b, "conv_ln_g": d_ln_g,
                 "conv_ln_b": d_ln_b, "norm_ffn2_g": d_gn3, "final_norm_g": d_final_g}
    return loss_parts, dx0, red, dmod, vec_grads, d_sinks, d_conv_w


def kernel(x, c, w_ada, b_ada, norm_ffn1_g, ffn1_w_gate, ffn1_w_up, ffn1_w_down, norm_mix_g, w_in, attn_sinks, w_attn_o, conv_w_dw, conv_b_dw, conv_ln_g, conv_ln_b, w_conv_o, w_out, norm_ffn2_g, ffn2_w_gate, ffn2_w_up, ffn2_w_down, final_norm_g, loss_target, m_w_ada, m_b_ada, m_norm_ffn1_g, m_ffn1_w_gate, m_ffn1_w_up, m_ffn1_w_down, m_norm_mix_g, m_w_in, m_attn_sinks, m_w_attn_o, m_conv_w_dw, m_conv_b_dw, m_conv_ln_g, m_conv_ln_b, m_w_conv_o, m_w_out, m_norm_ffn2_g, m_ffn2_w_gate, m_ffn2_w_up, m_ffn2_w_down, m_final_norm_g, v_w_ada, v_b_ada, v_norm_ffn1_g, v_ffn1_w_gate, v_ffn1_w_up, v_ffn1_w_down, v_norm_mix_g, v_w_in, v_attn_sinks, v_w_attn_o, v_conv_w_dw, v_conv_b_dw, v_conv_ln_g, v_conv_ln_b, v_w_conv_o, v_w_out, v_norm_ffn2_g, v_ffn2_w_gate, v_ffn2_w_up, v_ffn2_w_down, v_final_norm_g):
    args = dict(locals())
    B, seq, D = x.shape
    T = B * seq
    xi, yi, ci = _position()
    chip = 2 * xi + yi
    dev = 4 * xi + 2 * yi + ci

    def shard_2d(prefix, name):
        t = args[prefix + name][0]
        return t.T if name in COL_SHARDED else t

    big = {n: shard_2d("", n) for n in BIG_WEIGHTS}
    final_g = final_norm_g[None, :]
    vec_w = {n: (args[n] if n != "final_norm_g" else final_g) for n in VECTORS}

    conv_cols = D // N_CHIP
    conv_flat = jnp.pad(conv_w_dw[0].reshape(-1), (0, 8 * D - CONV_WIDTH * conv_cols)).reshape(8, D)
    first = _gather8(jnp.concatenate([jnp.pad(c, ((0, 8 - B), (0, 0))), conv_flat], axis=0), "gather_c")
    c_all = first[:, :B].reshape(N_DEV * B, D)
    conv_taps = first[::2, 8:].reshape(N_CHIP, 8 * D)[:, :CONV_WIDTH * conv_cols]
    conv_taps = conv_taps.reshape(N_CHIP, CONV_WIDTH, conv_cols).transpose(1, 0, 2).reshape(CONV_WIDTH, D)
    conv_taps = jnp.pad(conv_taps, ((0, CONV_PAD - CONV_WIDTH), (0, 0)))

    ada_cols = w_ada.shape[2]
    b_cols = lax.dynamic_slice(b_ada, (0, chip * ada_cols), (1, ada_cols))
    mod_part = _ada_fwd(c_all, w_ada[0], b_cols).reshape(N_DEV, B, ada_cols)
    mod = _mod_exchange(mod_part).transpose(1, 0, 2).reshape(B, N_MOD, D)

    core_idx = jnp.reshape(ci, (1,)).astype(jnp.int32)
    chip_idx = jnp.reshape(chip, (1,)).astype(jnp.int32)
    core_chip = jnp.stack([ci, chip]).astype(jnp.int32)
    slots = {n: _cast_slot(big[n], chip_idx, "cast_" + n) for n in BIG_WEIGHTS}

    small = dict(vec_w)
    small["attn_sinks"] = attn_sinks
    small["conv_w_dw"] = conv_taps

    loss_parts, dx, red, dmod, vec_grads, d_sinks, d_conv_w = _local_grads(
        x.reshape(T, D), loss_target.reshape(T, D), mod, slots, small, seq, core_idx, core_chip)

    loss = lax.psum((0.5 / D) * jnp.sum(loss_parts), ("x", "y", "c"))
    grad_x = dx.reshape(B, seq, D)
    out = {}

    block = jnp.zeros((SMALL_ROWS, D), F32)
    block = block.at[ROW_DMOD0:ROW_DMOD0 + N_MOD].set(dmod[0]).at[ROW_DMOD1:ROW_DMOD1 + N_MOD].set(dmod[1])
    block = block.at[ROW_VEC:ROW_VEC + len(VECTORS)].set(jnp.concatenate([vec_grads[n] for n in VECTORS], axis=0))
    block = block.at[ROW_SINK, :2 * HEAD_DIM].set(d_sinks[0])
    block = block.at[ROW_CONVW:ROW_CONVW + CONV_WIDTH].set(d_conv_w[:CONV_WIDTH])
    small_all = _gather8(block, "gather_small_grads")

    def pack_small(prefix):
        rows = [args[prefix + "b_ada"].reshape(N_MOD, D)]
        rows += [args[prefix + n].reshape(1, D) for n in VECTORS]
        rows += [jnp.pad(args[prefix + "attn_sinks"], ((0, 0), (0, D - N_Q_HEADS)))]
        return jnp.pad(jnp.concatenate(rows, axis=0), ((0, 24 - N_MOD - len(VECTORS) - 1), (0, 0)))

    small_sum, sg, sd, sm, sv = _small_adam(small_all, pack_small(""), pack_small("m_"), pack_small("v_"),
                                           ROW_DMOD0, ROW_DMOD1, ROW_VEC)

    def unpack_small(t):
        res = {"b_ada": t[:N_MOD].reshape(1, N_MOD * D)}
        for k, n in enumerate(VECTORS):
            res[n] = t[N_MOD + k].reshape(args[n].shape)
        res["attn_sinks"] = t[N_MOD + len(VECTORS), :N_Q_HEADS].reshape(1, N_Q_HEADS)
        return res

    unpacked = [unpack_small(t) for t in (sg, sd, sm, sv)]
    for n in ("b_ada", "attn_sinks") + VECTORS:
        out[n] = tuple(u[n] for u in unpacked)

    conv_g = lax.dynamic_slice(small_sum, (ROW_CONVW, chip * conv_cols), (CONV_WIDTH, conv_cols))
    d, mn, vn = red.run(_adam_call, None, conv_w_dw[0], conv_g, m_conv_w_dw[0], v_conv_w_dw[0], "adam_conv_w_dw")
    out["conv_w_dw"] = tuple(t[None] for t in (conv_g, d, mn, vn))

    dmod_rows = jnp.stack([small_all[:, ROW_DMOD0:ROW_DMOD0 + N_MOD], small_all[:, ROW_DMOD1:ROW_DMOD1 + N_MOD]], axis=1)
    dmod_all = dmod_rows.reshape(N_DEV * B, N_MOD * D)
    dmod_cols = lax.dynamic_slice(dmod_all, (0, chip * ada_cols), (N_DEV * B, ada_cols))
    ada_out = red.run(_ada_adam, 35, c_all.T, dmod_cols, w_ada[0], m_w_ada[0], v_w_ada[0])
    out["w_ada"] = tuple(t[None] for t in ada_out)

    for n in FFN2_WEIGHTS + MIX_WEIGHTS[1:] + MIX_WEIGHTS[:1] + FFN1_WEIGHTS:
        while n not in red.reduced:
            red.step()
        g = red.reduced[n].reshape(big[n].shape)
        d, mn, vn = red.run(_adam_call, None, big[n], g, shard_2d("m_", n), shard_2d("v_", n), "adam_" + n)
        out[n] = tuple((t.T if n in COL_SHARDED else t)[None] for t in (g, d, mn, vn))

    order = ("w_ada", "b_ada", "norm_ffn1_g", "ffn1_w_gate", "ffn1_w_up", "ffn1_w_down", "norm_mix_g", "w_in",
             "attn_sinks", "w_attn_o", "conv_w_dw", "conv_b_dw", "conv_ln_g", "conv_ln_b", "w_conv_o", "w_out",
             "norm_ffn2_g", "ffn2_w_gate", "ffn2_w_up", "ffn2_w_down", "final_norm_g")
    return (loss, grad_x, *[out[n][0] for n in order], *[out[n][1] for n in order],
            *[out[n][2] for n in order], *[out[n][3] for n in order])
```

```python
import functools

import jax
import jax.numpy as jnp
from jax import lax
from jax.experimental import pallas as pl
from jax.experimental.pallas import tpu as pltpu

F32 = jnp.float32
BF16 = jnp.bfloat16

D_MODEL = 1024
D_FF = 2816
N_CHIP = 4
N_DEV = 8
FF_SHARD = D_FF // N_CHIP
IN_WIDTH = 5376
IN_SHARD = IN_WIDTH // N_CHIP
HEAD_DIM = 64
N_Q_HEADS = 16
N_KV_HEADS = 2
BLOCK = 128
CONV_WIDTH = 31
CONV_PAD = 32
N_MOD = 9
EPS = 1e-6
FFN_RESIDUAL = 0.5
ATTN_SCALE = HEAD_DIM ** -0.5
MASK_VALUE = -1e30

ADAM_LR = 0.001
ADAM_B1 = 0.9
ADAM_B2 = 0.999
ADAM_EPS = 1e-08
ADAM_WD = 0.01
ADAM_STEP = 10

COLB_Q, COLB_CA, COLB_CB, COLB_GA, COLB_GC = 0, 1, 2, 3, 4
COLB_K, COLB_V = 40, 41
PROJ_TILE = 768

VMEM_LIMIT = 56 * 1024 * 1024
MESH = pl.DeviceIdType.MESH
ANY = pl.BlockSpec(memory_space=pl.ANY)
VMEM_SPEC = pl.BlockSpec(memory_space=pltpu.VMEM)
SMEM_SPEC = pl.BlockSpec(memory_space=pltpu.SMEM)


def _params(n_grid):
    return pltpu.CompilerParams(dimension_semantics=("arbitrary",) * n_grid, vmem_limit_bytes=VMEM_LIMIT)


def _tile(n, pref):
    t = min(n, pref)
    while n % t:
        t //= 2
    return t


def _row_tile(rows, cap):
    for t in range(min(rows, cap) // 16 * 16, 0, -16):
        if rows % t == 0:
            return t
    return rows


def _sigmoid(v):
    return 1.0 / (1.0 + jnp.exp(-v))


def _dot_nn(a, b):
    return lax.dot_general(a, b, (((1,), (0,)), ((), ())), preferred_element_type=F32)


def _dot_nt(a, b):
    return lax.dot_general(a, b, (((1,), (1,)), ((), ())), preferred_element_type=F32)


def _dot_tn(a, b):
    return lax.dot_general(a, b, (((0,), (0,)), ((), ())), preferred_element_type=F32)


ROW_CHUNK = 16


def _for_row_chunks(n_rows, fn):
    for r in range(0, n_rows, ROW_CHUNK):
        fn(slice(r, r + ROW_CHUNK))


def _norm_mod(xv, gn, sc, sh):
    r = lax.rsqrt(jnp.mean(xv * xv, axis=-1, keepdims=True) + EPS)
    return ((xv * r) * gn) * (1.0 + sc) + sh


def _accumulate(ref, first, value):
    @pl.when(first)
    def _():
        ref[...] = value

    @pl.when(jnp.logical_not(first))
    def _():
        ref[...] += value


def _norm_mod_bwd(dh, xv, gn, sc, dxo, first_of_batch, first, dx_ref, dsc_ref, dsh_ref, dgn_ref):
    r = lax.rsqrt(jnp.mean(xv * xv, axis=-1, keepdims=True) + EPS)
    xh = xv * r
    _accumulate(dsh_ref, first_of_batch, jnp.sum(dh, axis=0, keepdims=True))
    _accumulate(dsc_ref, first_of_batch, jnp.sum(dh * (xh * gn), axis=0, keepdims=True))
    dn = dh * (1.0 + sc)
    _accumulate(dgn_ref, first, jnp.sum(dn * xh, axis=0, keepdims=True))
    dxh = dn * gn
    dx_ref[...] = dxo + r * (dxh - xh * jnp.mean(dxh * xh, axis=-1, keepdims=True))


CHIP_FLIPS = ((1, 0), (0, 1), (1, 1))


def _position():
    return lax.axis_index("x"), lax.axis_index("y"), lax.axis_index("c")


def _flip(v, f):
    return 1 - v if f else v


class _GatherComm:
    def __init__(self, bufs):
        n = len(bufs)
        self.n = n
        self.operands = list(bufs)
        self.out_shape = [jax.ShapeDtypeStruct(b.shape, b.dtype) for b in bufs]
        self.aliases = {i: i for i in range(n)}
        self.sems = [pltpu.SemaphoreType.DMA((6 * n,)), pltpu.SemaphoreType.DMA((6 * n,))]
        self.rows = [b.shape[1] // 2 for b in bufs]

    def _half(self, ref, i, which):
        return ref.at[pl.ds(which * self.rows[i], self.rows[i]), :]

    def _ici(self, cins, couts, sems, i, k, dst_chip, to):
        x, y, c = _position()
        return pltpu.make_async_remote_copy(
            src_ref=self._half(cins[i].at[2 * x + y], i, c), dst_ref=self._half(couts[i].at[dst_chip], i, c),
            send_sem=sems[0].at[3 * i + k], recv_sem=sems[1].at[3 * i + k], device_id=to, device_id_type=MESH)

    def _d2d(self, couts, sems, i, k, src_chip, which):
        x, y, c = _position()
        place = self._half(couts[i].at[src_chip], i, which)
        return pltpu.make_async_remote_copy(
            src_ref=place, dst_ref=place, send_sem=sems[0].at[3 * self.n + 3 * i + k],
            recv_sem=sems[1].at[3 * self.n + 3 * i + k], device_id=(x, y, 1 - c), device_id_type=MESH)

    def _peers(self):
        x, y, _ = _position()
        return [(_flip(x, fx), _flip(y, fy)) for fx, fy in CHIP_FLIPS]

    def start(self, cins, couts, sems):
        x, y, c = _position()
        for i in range(self.n):
            for k, (px, py) in enumerate(self._peers()):
                self._ici(cins, couts, sems, i, k, 2 * x + y, (px, py, c)).start()

    def finish(self, cins, couts, sems):
        _, _, c = _position()
        peers = self._peers()
        for i in range(self.n):
            for k, (px, py) in enumerate(peers):
                self._ici(cins, couts, sems, i, k, 2 * px + py, (px, py, c)).wait_recv()
                self._d2d(couts, sems, i, k, 2 * px + py, c).start()
        for i in range(self.n):
            for k, (px, py) in enumerate(peers):
                self._d2d(couts, sems, i, k, 2 * px + py, 1 - c).wait_recv()
        for i in range(self.n):
            for k, (px, py) in enumerate(peers):
                self._ici(cins, couts, sems, i, k, 2 * px + py, (px, py, c)).wait_send()
                self._d2d(couts, sems, i, k, 2 * px + py, c).wait_send()


class _ExchangeComm:
    def __init__(self, pairs):
        n = len(pairs)
        self.n = n
        self.operands = list(pairs)
        self.out_shape = [jax.ShapeDtypeStruct((3,) + p.shape[1:], p.dtype) for p in pairs]
        self.aliases = {}
        self.sems = [pltpu.SemaphoreType.DMA((3 * n,)), pltpu.SemaphoreType.DMA((3 * n,))]

    def _copies(self, cins, couts, sems):
        x, y, c = _position()
        peers = [(_flip(x, fx), _flip(y, fy)) for fx, fy in CHIP_FLIPS]
        return [pltpu.make_async_remote_copy(
            src_ref=cins[i].at[2 * px + py], dst_ref=couts[i].at[k], send_sem=sems[0].at[3 * i + k],
            recv_sem=sems[1].at[3 * i + k], device_id=(px, py, c), device_id_type=MESH)
            for i in range(self.n) for k, (px, py) in enumerate(peers)]

    def start(self, cins, couts, sems):
        for cp in self._copies(cins, couts, sems):
            cp.start()

    def finish(self, cins, couts, sems):
        for cp in self._copies(cins, couts, sems):
            cp.wait()


class _SwapComm:
    def __init__(self, grads16):
        n = len(grads16)
        self.n = n
        self.operands = list(grads16)
        self.out_shape = [jax.ShapeDtypeStruct(g.shape[1:], g.dtype) for g in grads16]
        self.aliases = {}
        self.sems = [pltpu.SemaphoreType.DMA((n,)), pltpu.SemaphoreType.DMA((n,))]

    def _copies(self, cins, couts, sems):
        x, y, c = _position()
        return [pltpu.make_async_remote_copy(
            src_ref=cins[i].at[1 - c], dst_ref=couts[i], send_sem=sems[0].at[i], recv_sem=sems[1].at[i],
            device_id=(x, y, 1 - c), device_id_type=MESH) for i in range(self.n)]

    def start(self, cins, couts, sems):
        for cp in self._copies(cins, couts, sems):
            cp.start()

    def finish(self, cins, couts, sems):
        for cp in self._copies(cins, couts, sems):
            cp.wait()


class _JoinComm:
    def __init__(self, halves):
        n = len(halves)
        self.n = n
        self.operands = list(halves)
        self.out_shape = [jax.ShapeDtypeStruct(h.shape, h.dtype) for h in halves]
        self.aliases = {i: i for i in range(n)}
        self.sems = [pltpu.SemaphoreType.DMA((n,)), pltpu.SemaphoreType.DMA((n,))]

    def _copy(self, cins, couts, sems, i, which):
        x, y, c = _position()
        return pltpu.make_async_remote_copy(
            src_ref=cins[i].at[which], dst_ref=couts[i].at[which], send_sem=sems[0].at[i], recv_sem=sems[1].at[i],
            device_id=(x, y, 1 - c), device_id_type=MESH)

    def start(self, cins, couts, sems):
        _, _, c = _position()
        for i in range(self.n):
            self._copy(cins, couts, sems, i, c).start()

    def finish(self, cins, couts, sems):
        _, _, c = _position()
        for i in range(self.n):
            self._copy(cins, couts, sems, i, 1 - c).wait_recv()
        for i in range(self.n):
            self._copy(cins, couts, sems, i, c).wait_send()


class _Gather8Comm:
    def __init__(self, block):
        self.operands = [block]
        self.out_shape = [jax.ShapeDtypeStruct((N_DEV,) + block.shape, block.dtype)]
        self.aliases = {}
        self.sems = [pltpu.SemaphoreType.DMA((N_DEV - 1,)), pltpu.SemaphoreType.DMA((N_DEV - 1,)),
                     pltpu.SemaphoreType.DMA]
        self.flips = [(fx, fy, fc) for fx in (0, 1) for fy in (0, 1) for fc in (0, 1) if (fx, fy, fc) != (0, 0, 0)]

    def _peers(self):
        x, y, c = _position()
        return [(_flip(x, fx), _flip(y, fy), _flip(c, fc)) for fx, fy, fc in self.flips]

    def _copy(self, cins, couts, sems, k, block, to):
        return pltpu.make_async_remote_copy(src_ref=cins[0], dst_ref=couts[0].at[block], send_sem=sems[0].at[k],
                                            recv_sem=sems[1].at[k], device_id=to, device_id_type=MESH)

    def _mine(self, cins, couts, sems):
        x, y, c = _position()
        return pltpu.make_async_copy(cins[0], couts[0].at[4 * x + 2 * y + c], sems[2])

    def start(self, cins, couts, sems):
        x, y, c = _position()
        self._mine(cins, couts, sems).start()
        for k, peer in enumerate(self._peers()):
            self._copy(cins, couts, sems, k, 4 * x + 2 * y + c, peer).start()

    def finish(self, cins, couts, sems):
        for k, (px, py, pc) in enumerate(self._peers()):
            self._copy(cins, couts, sems, k, 4 * px + 2 * py + pc, (px, py, pc)).wait_recv()
        for k, peer in enumerate(self._peers()):
            self._copy(cins, couts, sems, k, 0, peer).wait_send()
        self._mine(cins, couts, sems).wait()


class _CommList:
    def __init__(self, parts):
        self.parts = list(parts)
        self.operands = [t for p in self.parts for t in p.operands]
        self.out_shape = [t for p in self.parts for t in p.out_shape]
        self.sems = [t for p in self.parts for t in p.sems]
        self.aliases = {}
        n_in = n_out = 0
        for p in self.parts:
            self.aliases.update({n_in + i: n_out + j for i, j in p.aliases.items()})
            n_in += len(p.operands)
            n_out += len(p.out_shape)

    def _split(self, cins, couts, sems):
        pos = [0, 0, 0]
        for p in self.parts:
            sizes = (len(p.operands), len(p.out_shape), len(p.sems))
            yield p, tuple(seq[a:a + k] for seq, a, k in zip((cins, couts, sems), pos, sizes))
            pos = [a + k for a, k in zip(pos, sizes)]

    def start(self, cins, couts, sems):
        for p, refs in self._split(cins, couts, sems):
            p.start(*refs)

    def finish(self, cins, couts, sems):
        for p, refs in self._split(cins, couts, sems):
            p.finish(*refs)

    def split_outputs(self, outs):
        res, pos = [], 0
        for p in self.parts:
            res.append(outs[pos:pos + len(p.out_shape)])
            pos += len(p.out_shape)
        return res


def _call(body, *, name, grid, in_specs, out_specs, out_shape, operands, scratch_shapes=(), comm=None):
    n_grid = len(grid)
    if comm is None:
        return pl.pallas_call(
            body, name=name, grid=grid, in_specs=list(in_specs), out_specs=list(out_specs), out_shape=list(out_shape),
            scratch_shapes=list(scratch_shapes), compiler_params=_params(n_grid))(*operands), ()
    counts = (len(in_specs), len(comm.operands), len(out_specs), len(comm.out_shape), len(scratch_shapes),
              len(comm.sems))

    def fused(*refs):
        parts, pos = [], 0
        for k in counts:
            parts.append(refs[pos:pos + k])
            pos += k
        ins, cins, outs, couts, scr, sems = parts
        first = functools.reduce(jnp.logical_and, [pl.program_id(d) == 0 for d in range(n_grid)])
        last = functools.reduce(jnp.logical_and, [pl.program_id(d) == grid[d] - 1 for d in range(n_grid)])

        @pl.when(first)
        def _():
            comm.start(cins, couts, sems)

        body(*ins, *outs, *scr)

        @pl.when(last)
        def _():
            comm.finish(cins, couts, sems)

    res = pl.pallas_call(
        fused, name=name, grid=grid, in_specs=list(in_specs) + [ANY] * counts[1],
        out_specs=list(out_specs) + [ANY] * counts[3], out_shape=list(out_shape) + list(comm.out_shape),
        scratch_shapes=list(scratch_shapes) + list(comm.sems),
        input_output_aliases={counts[0] + i: counts[2] + j for i, j in comm.aliases.items()},
        compiler_params=_params(n_grid))(*operands, *comm.operands)
    return res[:counts[2]], res[counts[2]:]


def _run_comm(comm, name):
    k_in, k_out = len(comm.operands), len(comm.out_shape)

    def body(*refs):
        cins, couts, sems = refs[:k_in], refs[k_in:k_in + k_out], refs[k_in + k_out:]
        comm.start(cins, couts, sems)
        comm.finish(cins, couts, sems)

    return pl.pallas_call(
        body, name=name, in_specs=[ANY] * k_in, out_specs=[ANY] * k_out, out_shape=list(comm.out_shape),
        scratch_shapes=list(comm.sems), input_output_aliases=dict(comm.aliases))(*comm.operands)


def _ffn_fwd(x, gn, sc, sh, gate, wg, wu, wd, seq, name, comm=None):
    T, D = x.shape
    J, Fs, _ = wg.shape
    tm = _tile(seq, 1024)
    nb = seq // tm

    def body(x_ref, gn_ref, sc_ref, sh_ref, gate_ref, wg_ref, wu_ref, wd_ref,
             h_ref, a_ref, u_ref, f_ref, xo_ref, hs, acc, s16):
        j = pl.program_id(1)

        @pl.when(j == 0)
        def _():
            hb = _norm_mod(x_ref[...], gn_ref[...], sc_ref[...], sh_ref[...]).astype(BF16)
            hs[...] = hb
            h_ref[...] = hb
            acc[...] = jnp.zeros_like(acc)

        hb = hs[...]
        a_all = _dot_nt(hb, wg_ref[...])
        u_all = _dot_nt(hb, wu_ref[...])

        def swiglu_rows(rows):
            a = a_all[rows, :]
            u = u_all[rows, :]
            a_ref[rows, :] = a.astype(BF16)
            u_ref[rows, :] = u.astype(BF16)
            s16[rows, :] = ((a * _sigmoid(a)) * u).astype(BF16)

        _for_row_chunks(tm, swiglu_rows)
        acc[...] += _dot_nn(s16[...], wd_ref[...])

        @pl.when(j == J - 1)
        def _():
            f = acc[...]
            f_ref[...] = f.astype(BF16)
            xo_ref[...] = x_ref[...] + (FFN_RESIDUAL * gate_ref[...]) * f

    row = pl.BlockSpec((tm, D), lambda i, j: (i, 0))
    vec = pl.BlockSpec((1, D), lambda i, j: (0, 0))
    per_b = pl.BlockSpec((None, 1, D), lambda i, j: (i // nb, 0, 0))
    hid = pl.BlockSpec((None, tm, Fs), lambda i, j: (j, i, 0))
    return _call(
        body, name=name, grid=(T // tm, J),
        in_specs=[row, vec, per_b, per_b, per_b] + [pl.BlockSpec((None, Fs, D), lambda i, j: (j, 0, 0))] * 3,
        out_specs=[row, hid, hid, row, row],
        out_shape=[jax.ShapeDtypeStruct((T, D), BF16), jax.ShapeDtypeStruct((J, T, Fs), BF16),
                   jax.ShapeDtypeStruct((J, T, Fs), BF16), jax.ShapeDtypeStruct((T, D), BF16),
                   jax.ShapeDtypeStruct((T, D), F32)],
        scratch_shapes=[pltpu.VMEM((tm, D), BF16), pltpu.VMEM((tm, D), F32), pltpu.VMEM((tm, Fs), BF16)],
        operands=(x, gn, sc, sh, gate, wg, wu, wd), comm=comm)


def _ffn_bwd(dxo, x, f, a, u, gn, sc, gate, wg, wu, wd, seq, name, comm=None):
    T, D = x.shape
    J, Fs, _ = wg.shape
    B = T // seq
    tm = _tile(seq, 512)
    nb = seq // tm

    def body(dxo_ref, x_ref, f_ref, a_ref, u_ref, gn_ref, sc_ref, gate_ref, wg_ref, wu_ref, wd_ref,
             da_ref, du_ref, s_ref, df_ref, dx_ref, dgate_ref, dsc_ref, dsh_ref, dgn_ref, dfs, acc):
        i = pl.program_id(0)
        j = pl.program_id(1)
        first_of_batch = i % nb == 0

        @pl.when(j == 0)
        def _():
            dxo_v = dxo_ref[...]
            dfb = ((FFN_RESIDUAL * gate_ref[...]) * dxo_v).astype(BF16)
            dfs[...] = dfb
            df_ref[...] = dfb
            part = jnp.sum((FFN_RESIDUAL * f_ref[...].astype(F32)) * dxo_v, axis=0, keepdims=True)
            _accumulate(dgate_ref, first_of_batch, part)
            acc[...] = jnp.zeros_like(acc)

        ds_all = _dot_nt(dfs[...], wd_ref[...])

        def swiglu_bwd_rows(rows):
            ds = ds_all[rows, :]
            av = a_ref[rows, :].astype(F32)
            uv = u_ref[rows, :].astype(F32)
            sig = _sigmoid(av)
            sil = av * sig
            s_ref[rows, :] = (sil * uv).astype(BF16)
            da_ref[rows, :] = (ds * uv * (sig * (1.0 + av * (1.0 - sig)))).astype(BF16)
            du_ref[rows, :] = (ds * sil).astype(BF16)

        _for_row_chunks(tm, swiglu_bwd_rows)
        acc[...] += _dot_nn(da_ref[...], wg_ref[...]) + _dot_nn(du_ref[...], wu_ref[...])

        @pl.when(j == J - 1)
        def _():
            _norm_mod_bwd(acc[...], x_ref[...], gn_ref[...], sc_ref[...], dxo_ref[...],
                          first_of_batch, i == 0, dx_ref, dsc_ref, dsh_ref, dgn_ref)

    row = pl.BlockSpec((tm, D), lambda i, j: (i, 0))
    vec = pl.BlockSpec((1, D), lambda i, j: (0, 0))
    per_b = pl.BlockSpec((None, 1, D), lambda i, j: (i // nb, 0, 0))
    hid = pl.BlockSpec((None, tm, Fs), lambda i, j: (j, i, 0))
    hid_shape = jax.ShapeDtypeStruct((J, T, Fs), BF16)
    per_b_shape = jax.ShapeDtypeStruct((B, 1, D), F32)
    return _call(
        body, name=name, grid=(T // tm, J),
        in_specs=[row, row, row, hid, hid, vec, per_b, per_b]
        + [pl.BlockSpec((None, Fs, D), lambda i, j: (j, 0, 0))] * 3,
        out_specs=[hid, hid, hid, row, row, per_b, per_b, per_b, vec],
        out_shape=[hid_shape, hid_shape, hid_shape, jax.ShapeDtypeStruct((T, D), BF16),
                   jax.ShapeDtypeStruct((T, D), F32), per_b_shape, per_b_shape, per_b_shape,
                   jax.ShapeDtypeStruct((1, D), F32)],
        scratch_shapes=[pltpu.VMEM((tm, D), BF16), pltpu.VMEM((tm, D), F32)],
        operands=(dxo, x, f, a, u, gn, sc, gate, wg, wu, wd), comm=comm)


def _wgrad(a, a_spec, b, b_spec, rows, cols, n_tok, name, comm=None):
    tk = _tile(n_tok, 1024)
    nk = n_tok // tk
    half = rows // 2

    def body(a_ref, b_ref, o32_ref, o16_ref, acc):
        k = pl.program_id(1)

        @pl.when(k == 0)
        def _():
            acc[...] = jnp.zeros_like(acc)

        acc[...] += _dot_tn(a_ref[...], b_ref[...])

        @pl.when(k == nk - 1)
        def _():
            for h in range(2):
                v = acc[h * half:(h + 1) * half, :]
                o32_ref[h] = v
                o16_ref[h] = v.astype(BF16)

    out_spec = pl.BlockSpec((2, None, half, cols), lambda j, k: (0, j, 0, 0))
    return _call(
        body, name=name, grid=(N_CHIP, nk),
        in_specs=[a_spec(tk), b_spec(tk)],
        out_specs=[out_spec, out_spec],
        out_shape=[jax.ShapeDtypeStruct((2, N_CHIP, half, cols), F32),
                   jax.ShapeDtypeStruct((2, N_CHIP, half, cols), BF16)],
        scratch_shapes=[pltpu.VMEM((rows, cols), F32)],
        operands=(a, b), comm=comm)


def _spec_rows(width):
    return lambda tk: pl.BlockSpec((tk, width), lambda j, k: (k, 0))


def _spec_chip_major(width):
    return lambda tk: pl.BlockSpec((None, tk, width), lambda j, k: (j, k, 0))


def _spec_col_block(width):
    return lambda tk: pl.BlockSpec((tk, width), lambda j, k: (k, j))


def _in_proj(x, gn, sc, sh, w_in, seq, comm=None):
    T, D = x.shape
    N = w_in.shape[0]
    tm = _tile(seq, 1024)
    nb = seq // tm

    def body(x_ref, gn_ref, sc_ref, sh_ref, w_ref, h_ref, p_ref, hs):
        @pl.when(pl.program_id(1) == 0)
        def _():
            hb = _norm_mod(x_ref[...], gn_ref[...], sc_ref[...], sh_ref[...]).astype(BF16)
            hs[...] = hb
            h_ref[...] = hb

        p_ref[...] = _dot_nt(hs[...], w_ref[...]).astype(BF16)

    row = pl.BlockSpec((tm, D), lambda i, j: (i, 0))
    per_b = pl.BlockSpec((None, 1, D), lambda i, j: (i // nb, 0, 0))
    return _call(
        body, name="mix_in_proj", grid=(T // tm, N // PROJ_TILE),
        in_specs=[row, pl.BlockSpec((1, D), lambda i, j: (0, 0)), per_b, per_b,
                  pl.BlockSpec((PROJ_TILE, D), lambda i, j: (j, 0))],
        out_specs=[row, pl.BlockSpec((tm, PROJ_TILE), lambda i, j: (i, j))],
        out_shape=[jax.ShapeDtypeStruct((T, D), BF16), jax.ShapeDtypeStruct((T, N), BF16)],
        scratch_shapes=[pltpu.VMEM((tm, D), BF16)],
        operands=(x, gn, sc, sh, w_in), comm=comm)


def _attn_specs(nblk):
    def own(col):
        return lambda b, n: (b * nblk + n, col)

    def prev(col):
        return lambda b, n: (b * nblk + jnp.maximum(n - 1, 0), col)

    kv = (BLOCK, 2 * HEAD_DIM)
    return [pl.BlockSpec((BLOCK, D_MODEL), own(COLB_Q)),
            pl.BlockSpec(kv, prev(COLB_K)), pl.BlockSpec(kv, own(COLB_K)),
            pl.BlockSpec(kv, prev(COLB_V)), pl.BlockSpec(kv, own(COLB_V))]


def _band_operands(prev_ref, own_ref, lo):
    band = jnp.concatenate([prev_ref[...], own_ref[...]], axis=0).astype(F32)
    rolled = pltpu.roll(band, HEAD_DIM, 1)
    zero = jnp.zeros_like(band)
    head0 = jnp.concatenate([jnp.where(lo, band, zero), jnp.where(lo, zero, rolled)], axis=0).astype(BF16)
    head1 = jnp.concatenate([jnp.where(lo, rolled, zero), jnp.where(lo, zero, band)], axis=0).astype(BF16)
    return head0, head1


PAIRS_PER_KV = N_Q_HEADS // 2 // N_KV_HEADS
BAND = 2 * BLOCK


def _band_valid(has_prev):
    qi = lax.broadcasted_iota(jnp.int32, (PAIRS_PER_KV * BLOCK, BAND), 0) & (BLOCK - 1)
    sj = lax.broadcasted_iota(jnp.int32, (PAIRS_PER_KV * BLOCK, BAND), 1)
    rel = qi + BLOCK - sj
    return (rel >= 0) & (rel < BLOCK) & ((sj >= BLOCK) | has_prev)


def _pair_lanes(kvh, pp):
    pair = kvh * PAIRS_PER_KV + pp
    return slice(pair * 2 * HEAD_DIM, (pair + 1) * 2 * HEAD_DIM)


def _stack_pairs(ref, kvh):
    return jnp.concatenate([ref[:, _pair_lanes(kvh, pp)] for pp in range(PAIRS_PER_KV)], axis=0)


def _rows_per_pair(columns):
    return jnp.concatenate(columns, axis=0)


def _attn_fwd(proj, sinks, batch, seq, comm=None):
    T = proj.shape[0]
    nblk = seq // BLOCK

    def body(sink_ref, q_ref, kp_ref, ko_ref, vp_ref, vo_ref, o_ref, lse_ref):
        lo = lax.broadcasted_iota(jnp.int32, (1, 2 * HEAD_DIM), 1) < HEAD_DIM
        head_lane = lax.broadcasted_iota(jnp.int32, (1, N_Q_HEADS), 1)
        valid = _band_valid(pl.program_id(1) > 0)
        k_ops = _band_operands(kp_ref, ko_ref, lo)
        v_ops = _band_operands(vp_ref, vo_ref, lo)
        lse_all = jnp.zeros((BLOCK, N_Q_HEADS), F32)
        col = jnp.zeros((BLOCK, 1), F32)
        side0_row = lax.broadcasted_iota(jnp.int32, (2 * BAND, 2 * HEAD_DIM), 0) < BAND
        low_lane = lax.broadcasted_iota(jnp.int32, (2 * BAND, 2 * HEAD_DIM), 1) < HEAD_DIM
        side_ones = jnp.where(side0_row == low_lane, 1.0, 0.0).astype(BF16)
        for kvh in range(N_KV_HEADS):
            s_all = _dot_nt(_stack_pairs(q_ref, kvh), k_ops[kvh]) * ATTN_SCALE
            weights, maxes, sink_terms = [], [], []
            for side in range(2):
                heads = [2 * (kvh * PAIRS_PER_KV + pp) + side for pp in range(PAIRS_PER_KV)]
                sink = _rows_per_pair([col + sink_ref[0, h] for h in heads])
                s = jnp.where(valid, s_all[:, side * BAND:(side + 1) * BAND], MASK_VALUE)
                m = jnp.maximum(jnp.max(s, axis=-1, keepdims=True), sink)
                weights.append(jnp.where(valid, jnp.exp(s - m), 0.0).astype(BF16))
                maxes.append(m)
                sink_terms.append(jnp.exp(sink - m))
            p_all = jnp.concatenate(weights, axis=1)
            den = _dot_nn(p_all, side_ones) + jnp.where(lo, sink_terms[0], sink_terms[1])
            out = _dot_nn(p_all, v_ops[kvh]) / den
            for pp in range(PAIRS_PER_KV):
                o_ref[:, _pair_lanes(kvh, pp)] = out[pp * BLOCK:(pp + 1) * BLOCK].astype(BF16)
            for side in range(2):
                lse = maxes[side] + jnp.log(den[:, side * HEAD_DIM:side * HEAD_DIM + 1])
                for pp in range(PAIRS_PER_KV):
                    h = 2 * (kvh * PAIRS_PER_KV + pp) + side
                    lse_all = jnp.where(head_lane == h, lse[pp * BLOCK:(pp + 1) * BLOCK], lse_all)
        lse_ref[...] = lse_all

    return _call(
        body, name="attn_fwd", grid=(batch, nblk),
        in_specs=[SMEM_SPEC] + _attn_specs(nblk),
        out_specs=[pl.BlockSpec((BLOCK, D_MODEL), lambda b, n: (b * nblk + n, 0)),
                   pl.BlockSpec((BLOCK, N_Q_HEADS), lambda b, n: (b * nblk + n, 0))],
        out_shape=[jax.ShapeDtypeStruct((T, D_MODEL), BF16), jax.ShapeDtypeStruct((T, N_Q_HEADS), F32)],
        operands=(sinks, proj, proj, proj, proj, proj), comm=comm)


def _conv_u(ca, cb):
    return ca.astype(F32) * _sigmoid(cb.astype(F32))


def _conv_specs(ts, tiles_per_seq):
    per_tile = ts // CONV_PAD

    def tile(col):
        return lambda b, t: (b * tiles_per_seq + t, col)

    def before(col):
        return lambda b, t: (jnp.maximum((b * tiles_per_seq + t) * per_tile - 1, 0), col)

    return [pl.BlockSpec((ts, D_MODEL), tile(COLB_CA)), pl.BlockSpec((ts, D_MODEL), tile(COLB_CB)),
            pl.BlockSpec((CONV_PAD, D_MODEL), before(COLB_CA)), pl.BlockSpec((CONV_PAD, D_MODEL), before(COLB_CB))]


SUBLANES = 8


def _fill_upad(upad, ca_ref, cb_ref, cah_ref, cbh_ref, t):
    halo = _conv_u(cah_ref[...], cbh_ref[...])
    upad[0, 0:CONV_PAD, :] = jnp.where(t > 0, halo, jnp.zeros_like(halo))
    upad[0, CONV_PAD:, :] = _conv_u(ca_ref[...], cb_ref[...])


def _fill_shifted(pad):
    rows = pad.shape[1] - SUBLANES
    for b in range(1, SUBLANES):
        pad[b, 0:rows, :] = pad[0, b:b + rows, :]


def _shifted_rows(pad, offset, rows):
    b = offset % SUBLANES
    return pad[b, offset - b:offset - b + rows, :]


def _layernorm_stats(y):
    mu = jnp.mean(y, axis=-1, keepdims=True)
    yc = y - mu
    rstd = lax.rsqrt(jnp.mean(yc * yc, axis=-1, keepdims=True) + EPS)
    return yc * rstd, rstd


def _conv_fwd(proj, w_dw, b_dw, ln_g, ln_b, batch, seq, comm=None):
    T = proj.shape[0]
    ts = _tile(seq, 256)
    nt = seq // ts
    shift = CONV_PAD - (CONV_WIDTH - 1)

    def body(ca_ref, cb_ref, cah_ref, cbh_ref, w_ref, b_ref, g_ref, beta_ref, y_ref, z_ref, upad):
        _fill_upad(upad, ca_ref, cb_ref, cah_ref, cbh_ref, pl.program_id(1))
        _fill_shifted(upad)
        y = jnp.zeros((ts, D_MODEL), F32) + b_ref[...]
        for k in range(CONV_WIDTH):
            y = y + w_ref[k:k + 1, :] * _shifted_rows(upad, shift + k, ts)
        y_ref[...] = y
        lnh, _ = _layernorm_stats(y)
        ln = lnh * g_ref[...] + beta_ref[...]
        z_ref[...] = (ln * _sigmoid(ln)).astype(BF16)

    vec = pl.BlockSpec((1, D_MODEL), lambda b, t: (0, 0))
    row = pl.BlockSpec((ts, D_MODEL), lambda b, t: (b * nt + t, 0))
    return _call(
        body, name="conv_fwd", grid=(batch, nt),
        in_specs=_conv_specs(ts, nt) + [pl.BlockSpec((CONV_PAD, D_MODEL), lambda b, t: (0, 0)), vec, vec, vec],
        out_specs=[row, row],
        out_shape=[jax.ShapeDtypeStruct((T, D_MODEL), F32), jax.ShapeDtypeStruct((T, D_MODEL), BF16)],
        scratch_shapes=[pltpu.VMEM((SUBLANES, ts + CONV_PAD, D_MODEL), F32)],
        operands=(proj, proj, proj, proj, w_dw, b_dw, ln_g, ln_b), comm=comm)


def _merge(o, z, proj, w_ao, w_co, w_out, x, gate, seq):
    T, D = x.shape
    tm = _tile(seq, 512)
    nb = seq // tm

    def body(o_ref, z_ref, ga_ref, gc_ref, wao_ref, wco_ref, wout_ref, x_ref, gate_ref,
             ya_ref, yc_ref, mg_ref, mo_ref, xo_ref):
        ya = _dot_nn(o_ref[...], wao_ref[...])
        yc = _dot_nn(z_ref[...], wco_ref[...])
        ya_ref[...] = ya.astype(BF16)
        yc_ref[...] = yc.astype(BF16)
        merged = (_sigmoid(ga_ref[...].astype(F32)) * ya + _sigmoid(gc_ref[...].astype(F32)) * yc).astype(BF16)
        mg_ref[...] = merged
        mo = _dot_nn(merged, wout_ref[...])
        mo_ref[...] = mo.astype(BF16)
        xo_ref[...] = x_ref[...] + gate_ref[...] * mo

    row = pl.BlockSpec((tm, D), lambda i: (i, 0))
    mat = pl.BlockSpec((D, D), lambda i: (0, 0))
    act = jax.ShapeDtypeStruct((T, D), BF16)
    return pl.pallas_call(
        body, name="mix_merge", grid=(T // tm,),
        in_specs=[row, row, pl.BlockSpec((tm, D), lambda i: (i, COLB_GA)), pl.BlockSpec((tm, D), lambda i: (i, COLB_GC)),
                  mat, mat, mat, row, pl.BlockSpec((None, 1, D), lambda i: (i // nb, 0, 0))],
        out_specs=[row, row, row, row, row],
        out_shape=[act, act, act, act, jax.ShapeDtypeStruct((T, D), F32)],
        compiler_params=_params(1),
    )(o, z, proj, proj, w_ao, w_co, w_out, x, gate)


def _final_loss(x, gf, target):
    T, D = x.shape
    tm = _tile(T, 512)

    def body(x_ref, gf_ref, t_ref, dx_ref, lp_ref, dgf_ref):
        first = pl.program_id(0) == 0
        xv = x_ref[...]
        gfv = gf_ref[...]
        r = lax.rsqrt(jnp.mean(xv * xv, axis=-1, keepdims=True) + EPS)
        xh = xv * r
        err = xh * gfv - t_ref[...]
        _accumulate(lp_ref, first, jnp.sum(err * err, axis=0, keepdims=True))
        dy = err * (1.0 / D)
        _accumulate(dgf_ref, first, jnp.sum(dy * xh, axis=0, keepdims=True))
        dxh = dy * gfv
        dx_ref[...] = r * (dxh - xh * jnp.mean(dxh * xh, axis=-1, keepdims=True))

    row = pl.BlockSpec((tm, D), lambda i: (i, 0))
    vec = pl.BlockSpec((1, D), lambda i: (0, 0))
    return pl.pallas_call(
        body, name="final_loss", grid=(T // tm,),
        in_specs=[row, vec, row], out_specs=[row, vec, vec],
        out_shape=[jax.ShapeDtypeStruct((T, D), F32), jax.ShapeDtypeStruct((1, D), F32),
                   jax.ShapeDtypeStruct((1, D), F32)],
        compiler_params=_params(1),
    )(x, gf, target)


def _merge_bwd(dxo, mo, gate, proj, ya, yc, w_out, w_ao, w_co, seq, comm=None):
    T, D = dxo.shape
    B = T // seq
    tm = _tile(seq, 512)
    nb = seq // tm

    def body(dxo_ref, mo_ref, gate_ref, ga_ref, gc_ref, ya_ref, yc_ref, wout_ref, wao_ref, wco_ref,
             dmo_ref, dya_ref, dyc_ref, dga_ref, dgc_ref, do_ref, dz_ref, dgate_ref):
        dxo_v = dxo_ref[...]
        dmo = (gate_ref[...] * dxo_v).astype(BF16)
        dmo_ref[...] = dmo
        _accumulate(dgate_ref, pl.program_id(0) % nb == 0,
                    jnp.sum(mo_ref[...].astype(F32) * dxo_v, axis=0, keepdims=True))
        dm = _dot_nt(dmo, wout_ref[...])
        sa = _sigmoid(ga_ref[...].astype(F32))
        sc = _sigmoid(gc_ref[...].astype(F32))
        dya = (sa * dm).astype(BF16)
        dyc = (sc * dm).astype(BF16)
        dya_ref[...] = dya
        dyc_ref[...] = dyc
        dga_ref[...] = (dm * ya_ref[...].astype(F32) * (sa * (1.0 - sa))).astype(BF16)
        dgc_ref[...] = (dm * yc_ref[...].astype(F32) * (sc * (1.0 - sc))).astype(BF16)
        do_ref[...] = _dot_nt(dya, wao_ref[...]).astype(BF16)
        dz_ref[...] = _dot_nt(dyc, wco_ref[...]).astype(BF16)

    row = pl.BlockSpec((tm, D), lambda i: (i, 0))
    mat = pl.BlockSpec((D, D), lambda i: (0, 0))
    per_b = pl.BlockSpec((None, 1, D), lambda i: (i // nb, 0, 0))
    act = jax.ShapeDtypeStruct((T, D), BF16)
    return _call(
        body, name="mix_merge_bwd", grid=(T // tm,),
        in_specs=[row, row, per_b, pl.BlockSpec((tm, D), lambda i: (i, COLB_GA)),
                  pl.BlockSpec((tm, D), lambda i: (i, COLB_GC)), row, row, mat, mat, mat],
        out_specs=[row] * 7 + [per_b],
        out_shape=[act] * 7 + [jax.ShapeDtypeStruct((B, 1, D), F32)],
        operands=(dxo, mo, gate, proj, proj, ya, yc, w_out, w_ao, w_co), comm=comm)


def _attn_bwd(proj, sinks, o, do, lse, batch, seq, comm=None):
    T = proj.shape[0]
    nblk = seq // BLOCK
    n_steps = batch * nblk

    def body(sink_ref, q_ref, kp_ref, ko_ref, vp_ref, vo_ref, o_ref, do_ref, lse_ref,
             dq_ref, dkp_ref, dko_ref, dvp_ref, dvo_ref, dsink_ref):
        lo = lax.broadcasted_iota(jnp.int32, (1, 2 * HEAD_DIM), 1) < HEAD_DIM
        sink_lane = lax.broadcasted_iota(jnp.int32, (1, 2 * HEAD_DIM), 1)
        valid = _band_valid(pl.program_id(1) > 0)
        k_ops = _band_operands(kp_ref, ko_ref, lo)
        v_ops = _band_operands(vp_ref, vo_ref, lo)
        dsink = jnp.zeros((1, 2 * HEAD_DIM), F32)
        col = jnp.zeros((BLOCK, 1), F32)

        def fold(both):
            return (jnp.where(lo, both[:BAND], 0.0)
                    + pltpu.roll(jnp.where(lo, 0.0, both[BAND:]), HEAD_DIM, 1))

        dk_heads, dv_heads = [], []
        for kvh in range(N_KV_HEADS):
            q4 = _stack_pairs(q_ref, kvh)
            do4 = _stack_pairs(do_ref, kvh)
            dd = do4.astype(F32) * _stack_pairs(o_ref, kvh).astype(F32)
            s_all = _dot_nt(q4, k_ops[kvh]) * ATTN_SCALE
            dp_all = _dot_nt(do4, v_ops[kvh])
            ds_sides, p_sides = [], []
            for side in range(2):
                heads = [2 * (kvh * PAIRS_PER_KV + pp) + side for pp in range(PAIRS_PER_KV)]
                mine = lo if side == 0 else jnp.logical_not(lo)
                cols = slice(side * BAND, (side + 1) * BAND)
                sink = _rows_per_pair([col + sink_ref[0, h] for h in heads])
                lse = _rows_per_pair([lse_ref[:, h:h + 1] for h in heads])
                delta = jnp.sum(jnp.where(mine, dd, 0.0), axis=-1, keepdims=True)
                p = jnp.where(valid, jnp.exp(jnp.where(valid, s_all[:, cols], MASK_VALUE) - lse), 0.0)
                ds_sides.append((p * (dp_all[:, cols] - delta) * ATTN_SCALE).astype(BF16))
                p_sides.append(p.astype(BF16))
                sink_part = jnp.exp(sink - lse) * delta
                for pp, h in enumerate(heads):
                    dsink = dsink + jnp.where(sink_lane == h, -jnp.sum(sink_part[pp * BLOCK:(pp + 1) * BLOCK]), 0.0)
            ds_all = jnp.concatenate(ds_sides, axis=1)
            dq4 = _dot_nn(ds_all, k_ops[kvh])
            for pp in range(PAIRS_PER_KV):
                dq_ref[:, _pair_lanes(kvh, pp)] = dq4[pp * BLOCK:(pp + 1) * BLOCK].astype(BF16)
            dk_heads.append(fold(_dot_tn(ds_all, q4)))
            dv_heads.append(fold(_dot_tn(jnp.concatenate(p_sides, axis=1), do4)))
        dk = dk_heads[0] + pltpu.roll(dk_heads[1], HEAD_DIM, 1)
        dv = dv_heads[0] + pltpu.roll(dv_heads[1], HEAD_DIM, 1)
        dkp_ref[...] = dk[:BLOCK]
        dko_ref[...] = dk[BLOCK:]
        dvp_ref[...] = dv[:BLOCK]
        dvo_ref[...] = dv[BLOCK:]
        dsink_ref[...] = dsink

    def own(b, n):
        return (b * nblk + n, 0)

    row = pl.BlockSpec((BLOCK, D_MODEL), own)
    kv = pl.BlockSpec((BLOCK, 2 * HEAD_DIM), own)
    kv_shape = jax.ShapeDtypeStruct((T, 2 * HEAD_DIM), F32)
    return _call(
        body, name="attn_bwd", grid=(batch, nblk),
        in_specs=[SMEM_SPEC] + _attn_specs(nblk) + [row, row, pl.BlockSpec((BLOCK, N_Q_HEADS), own)],
        out_specs=[row, kv, kv, kv, kv, pl.BlockSpec((None, 1, 2 * HEAD_DIM), lambda b, n: (b * nblk + n, 0, 0))],
        out_shape=[jax.ShapeDtypeStruct((T, D_MODEL), BF16), kv_shape, kv_shape, kv_shape, kv_shape,
                   jax.ShapeDtypeStruct((n_steps, 1, 2 * HEAD_DIM), F32)],
        operands=(sinks, proj, proj, proj, proj, proj, o, do, lse), comm=comm)


def _conv_bwd(proj, dz, ydw, w_dw, ln_g, ln_b, batch, seq, comm=None):
    T = proj.shape[0]
    ts = _tile(seq, 256)
    nt = seq // ts
    per_tile = ts // CONV_PAD
    shift = CONV_PAD - (CONV_WIDTH - 1)

    def body(ca_ref, cb_ref, cah_ref, cbh_ref, dz_ref, dzn_ref, y_ref, yn_ref, w_ref, g_ref, beta_ref,
             dca_ref, dcb_ref, dw_ref, db_ref, dg_ref, dbeta_ref, upad, dypad):
        t = pl.program_id(1)
        first = (pl.program_id(0) == 0) & (t == 0)
        gv = g_ref[...]

        def ln_bwd(dzv, yv):
            lnh, rstd = _layernorm_stats(yv)
            ln = lnh * gv + beta_ref[...]
            sg = _sigmoid(ln)
            dln = dzv.astype(F32) * (sg * (1.0 + ln * (1.0 - sg)))
            dyh = dln * gv
            dy = rstd * (dyh - jnp.mean(dyh, axis=-1, keepdims=True)
                         - lnh * jnp.mean(dyh * lnh, axis=-1, keepdims=True))
            return dy, dln, lnh

        dy, dln, lnh = ln_bwd(dz_ref[...], y_ref[...])
        dy_next, _, _ = ln_bwd(dzn_ref[...], yn_ref[...])
        dypad[0, 0:ts, :] = dy
        dypad[0, ts:, :] = jnp.where(t < nt - 1, dy_next, jnp.zeros_like(dy_next))
        _fill_shifted(dypad)
        _fill_upad(upad, ca_ref, cb_ref, cah_ref, cbh_ref, t)
        _fill_shifted(upad)

        _accumulate(dg_ref, first, jnp.sum(dln * lnh, axis=0, keepdims=True))
        _accumulate(dbeta_ref, first, jnp.sum(dln, axis=0, keepdims=True))
        _accumulate(db_ref, first, jnp.sum(dy, axis=0, keepdims=True))

        @pl.when(first)
        def _():
            dw_ref[...] = jnp.zeros_like(dw_ref)

        du = jnp.zeros((ts, D_MODEL), F32)
        for k in range(CONV_WIDTH):
            du = du + w_ref[k:k + 1, :] * _shifted_rows(dypad, CONV_WIDTH - 1 - k, ts)
            dw_ref[k:k + 1, :] += jnp.sum(dy * _shifted_rows(upad, shift + k, ts), axis=0, keepdims=True)
        cav = ca_ref[...].astype(F32)
        sb = _sigmoid(cb_ref[...].astype(F32))
        dca_ref[...] = (du * sb).astype(BF16)
        dcb_ref[...] = (du * cav * (sb * (1.0 - sb))).astype(BF16)

    def tile(b, t):
        return (b * nt + t, 0)

    def after(b, t):
        return (jnp.minimum((b * nt + t + 1) * per_tile, T // CONV_PAD - 1), 0)

    row = pl.BlockSpec((ts, D_MODEL), tile)
    halo = pl.BlockSpec((CONV_PAD, D_MODEL), after)
    vec = pl.BlockSpec((1, D_MODEL), lambda b, t: (0, 0))
    wspec = pl.BlockSpec((CONV_PAD, D_MODEL), lambda b, t: (0, 0))
    act = jax.ShapeDtypeStruct((T, D_MODEL), BF16)
    vec_shape = jax.ShapeDtypeStruct((1, D_MODEL), F32)
    return _call(
        body, name="conv_bwd", grid=(batch, nt),
        in_specs=_conv_specs(ts, nt) + [row, halo, row, halo, wspec, vec, vec],
        out_specs=[row, row, wspec, vec, vec, vec],
        out_shape=[act, act, jax.ShapeDtypeStruct((CONV_PAD, D_MODEL), F32), vec_shape, vec_shape, vec_shape],
        scratch_shapes=[pltpu.VMEM((SUBLANES, ts + CONV_PAD, D_MODEL), F32)] * 2,
        operands=(proj, proj, proj, proj, dz, dz, ydw, ydw, w_dw, ln_g, ln_b), comm=comm)


def _in_proj_bwd(dproj, w_in_g, x, gn, sc, dxo, seq, comm=None):
    T, D = x.shape
    J, W, _ = w_in_g.shape
    B = T // seq
    tm = _tile(seq, 512)
    nb = seq // tm

    def body(dp_ref, w_ref, x_ref, gn_ref, sc_ref, dxo_ref, dx_ref, dsc_ref, dsh_ref, dgn_ref, acc):
        i = pl.program_id(0)
        j = pl.program_id(1)

        @pl.when(j == 0)
        def _():
            acc[...] = jnp.zeros_like(acc)

        acc[...] += _dot_nn(dp_ref[...], w_ref[...])

        @pl.when(j == J - 1)
        def _():
            _norm_mod_bwd(acc[...], x_ref[...], gn_ref[...], sc_ref[...], dxo_ref[...],
                          i % nb == 0, i == 0, dx_ref, dsc_ref, dsh_ref, dgn_ref)

    row = pl.BlockSpec((tm, D), lambda i, j: (i, 0))
    vec = pl.BlockSpec((1, D), lambda i, j: (0, 0))
    per_b = pl.BlockSpec((None, 1, D), lambda i, j: (i // nb, 0, 0))
    per_b_shape = jax.ShapeDtypeStruct((B, 1, D), F32)
    return _call(
        body, name="mix_in_proj_bwd", grid=(T // tm, J),
        in_specs=[pl.BlockSpec((None, tm, W), lambda i, j: (j, i, 0)),
                  pl.BlockSpec((None, W, D), lambda i, j: (j, 0, 0)), row, vec, per_b, row],
        out_specs=[row, per_b, per_b, vec],
        out_shape=[jax.ShapeDtypeStruct((T, D), F32), per_b_shape, per_b_shape, jax.ShapeDtypeStruct((1, D), F32)],
        scratch_shapes=[pltpu.VMEM((tm, D), F32)],
        operands=(dproj, w_in_g, x, gn, sc, dxo), comm=comm)


def _ada_fwd(c_all, w_ada, b_cols):
    nbatch, D = c_all.shape
    N = w_ada.shape[1]
    tn = _tile(N, 768)

    def body(c_ref, w_ref, b_ref, o_ref):
        cv = c_ref[...]
        act = (cv * _sigmoid(cv)).astype(BF16)
        o_ref[...] = _dot_nn(act, w_ref[...].astype(BF16)) + b_ref[...]

    return pl.pallas_call(
        body, name="ada_fwd", grid=(N // tn,),
        in_specs=[pl.BlockSpec((nbatch, D), lambda j: (0, 0)), pl.BlockSpec((D, tn), lambda j: (0, j)),
                  pl.BlockSpec((1, tn), lambda j: (0, j))],
        out_specs=pl.BlockSpec((nbatch, tn), lambda j: (0, j)),
        out_shape=jax.ShapeDtypeStruct((nbatch, N), F32),
        compiler_params=_params(1),
    )(c_all, w_ada, b_cols)


def _adamw(w, g, m, v):
    m = ADAM_B1 * m + (1.0 - ADAM_B1) * g
    v = ADAM_B2 * v + (1.0 - ADAM_B2) * (g * g)
    m_hat = m / (1.0 - ADAM_B1 ** ADAM_STEP)
    v_hat = v / (1.0 - ADAM_B2 ** ADAM_STEP)
    delta = -ADAM_LR * (m_hat / (jnp.sqrt(v_hat) + ADAM_EPS) + ADAM_WD * w)
    return delta, m, v


def _adam_call(w, g, m, v, name, comm=None):
    R, C = w.shape
    tr = _row_tile(R, 512)

    def body(w_ref, g_ref, m_ref, v_ref, d_ref, mo_ref, vo_ref):
        d, mn, vn = _adamw(w_ref[...], g_ref[...], m_ref[...], v_ref[...])
        d_ref[...] = d
        mo_ref[...] = mn
        vo_ref[...] = vn

    blk = pl.BlockSpec((tr, C), lambda i: (i, 0))
    shape = jax.ShapeDtypeStruct((R, C), F32)
    return _call(body, name=name, grid=(R // tr,), in_specs=[blk] * 4, out_specs=[blk] * 3, out_shape=[shape] * 3,
                 operands=(w, g, m, v), comm=comm)


def _ada_adam(c_act_t, dmod_cols, w, m, v, comm):
    R, C = w.shape
    nbatch = c_act_t.shape[1]
    tr = _tile(R, 128)

    def body(ct_ref, dm_ref, w_ref, m_ref, v_ref, g_ref, d_ref, mo_ref, vo_ref):
        cv = ct_ref[...]
        g = _dot_nn((cv * _sigmoid(cv)).astype(BF16), dm_ref[...].astype(BF16))
        g_ref[...] = g
        d, mn, vn = _adamw(w_ref[...], g, m_ref[...], v_ref[...])
        d_ref[...] = d
        mo_ref[...] = mn
        vo_ref[...] = vn

    blk = pl.BlockSpec((tr, C), lambda i: (i, 0))
    shape = jax.ShapeDtypeStruct((R, C), F32)
    return _call(
        body, name="ada_adam", grid=(R // tr,),
        in_specs=[pl.BlockSpec((tr, nbatch), lambda i: (i, 0)), pl.BlockSpec((nbatch, C), lambda i: (0, 0)),
                  blk, blk, blk],
        out_specs=[blk] * 4, out_shape=[shape] * 4,
        operands=(c_act_t, dmod_cols, w, m, v), comm=comm)


def _small_adam(gathered, w, m, v, rows_b0, rows_b1, rows_vec):
    _, P, D = gathered.shape
    R = w.shape[0]

    def body(ga_ref, w_ref, m_ref, v_ref, sum_ref, g_ref, d_ref, mo_ref, vo_ref):
        total = ga_ref[0]
        for dev in range(1, N_DEV):
            total = total + ga_ref[dev]
        sum_ref[...] = total
        g_ref[...] = jnp.zeros_like(g_ref)
        g_ref[0:N_MOD, :] = (sum_ref[rows_b0:rows_b0 + N_MOD, :] + sum_ref[rows_b1:rows_b1 + N_MOD, :])
        g_ref[N_MOD:N_MOD + 8, :] = sum_ref[rows_vec:rows_vec + 8, :]
        d, mn, vn = _adamw(w_ref[...], g_ref[...], m_ref[...], v_ref[...])
        d_ref[...] = d
        mo_ref[...] = mn
        vo_ref[...] = vn

    shape = jax.ShapeDtypeStruct((R, D), F32)
    return pl.pallas_call(
        body, name="small_adam",
        in_specs=[VMEM_SPEC] * 4, out_specs=[VMEM_SPEC] * 5,
        out_shape=[jax.ShapeDtypeStruct((P, D), F32), shape, shape, shape, shape],
        compiler_params=pltpu.CompilerParams(vmem_limit_bytes=VMEM_LIMIT),
    )(gathered, w, m, v)


def _gather8(v, name):
    A, W = v.shape
    flips = [(fx, fy, fc) for fx in (0, 1) for fy in (0, 1) for fc in (0, 1) if (fx, fy, fc) != (0, 0, 0)]

    def body(v_ref, out_ref, send_sems, recv_sems, local_sem):
        x, y, c = _position()
        me = 4 * x + 2 * y + c
        mine = pltpu.make_async_copy(v_ref, out_ref.at[me], local_sem)
        mine.start()

        def copy(k, block, to):
            return pltpu.make_async_remote_copy(src_ref=v_ref, dst_ref=out_ref.at[block], send_sem=send_sems.at[k],
                                                recv_sem=recv_sems.at[k], device_id=to, device_id_type=MESH)

        peers = [(_flip(x, fx), _flip(y, fy), _flip(c, fc)) for fx, fy, fc in flips]
        sends = [copy(k, me, peer) for k, peer in enumerate(peers)]
        for cp in sends:
            cp.start()
        for k, (px, py, pc) in enumerate(peers):
            copy(k, 4 * px + 2 * py + pc, (px, py, pc)).wait_recv()
        for cp in sends:
            cp.wait_send()
        mine.wait()

    return pl.pallas_call(
        body, name=name, in_specs=[VMEM_SPEC], out_specs=VMEM_SPEC,
        out_shape=jax.ShapeDtypeStruct((N_DEV, A, W), v.dtype),
        scratch_shapes=[pltpu.SemaphoreType.DMA((N_DEV - 1,)), pltpu.SemaphoreType.DMA((N_DEV - 1,)),
                        pltpu.SemaphoreType.DMA],
    )(v)


def _mod_exchange(part):
    _, A, W = part.shape

    def body(p_ref, out_ref, send_sems, recv_sems, local_sem):
        x, y, c = _position()
        me = 4 * x + 2 * y + c
        chip = 2 * x + y
        mine = pltpu.make_async_copy(p_ref.at[me], out_ref.at[chip], local_sem)
        mine.start()
        peers = [(_flip(x, fx), _flip(y, fy)) for fx, fy in CHIP_FLIPS]
        sends = []
        for k, (px, py) in enumerate(peers):
            sends.append(pltpu.make_async_remote_copy(
                src_ref=p_ref.at[4 * px + 2 * py + c], dst_ref=out_ref.at[chip], send_sem=send_sems.at[k],
                recv_sem=recv_sems.at[k], device_id=(px, py, c), device_id_type=MESH))
        for cp in sends:
            cp.start()
        for k, (px, py) in enumerate(peers):
            pltpu.make_async_remote_copy(
                src_ref=p_ref.at[me], dst_ref=out_ref.at[2 * px + py], send_sem=send_sems.at[k],
                recv_sem=recv_sems.at[k], device_id=(px, py, c), device_id_type=MESH).wait_recv()
        for cp in sends:
            cp.wait_send()
        mine.wait()

    return pl.pallas_call(
        body, name="mod_exchange", in_specs=[VMEM_SPEC], out_specs=VMEM_SPEC,
        out_shape=jax.ShapeDtypeStruct((N_CHIP, A, W), part.dtype),
        scratch_shapes=[pltpu.SemaphoreType.DMA((3,)), pltpu.SemaphoreType.DMA((3,)), pltpu.SemaphoreType.DMA],
    )(part)


def _cast_slot(w, chip_idx, name):
    R, C = w.shape
    tr = _row_tile(R, 512)

    def body(chip_ref, w_ref, o_ref):
        o_ref[...] = w_ref[...].astype(BF16)

    return pl.pallas_call(
        body, name=name,
        grid_spec=pltpu.PrefetchScalarGridSpec(
            num_scalar_prefetch=1, grid=(R // tr,),
            in_specs=[pl.BlockSpec((tr, C), lambda i, chip_ref: (i, 0))],
            out_specs=pl.BlockSpec((None, tr, C), lambda i, chip_ref: (chip_ref[0], i, 0))),
        out_shape=jax.ShapeDtypeStruct((N_CHIP, R, C), BF16),
        compiler_params=_params(1),
    )(chip_idx, w)


def _pair_sum(g32, recv, core, name):
    _, J, r, C = g32.shape

    def body(core_ref, g_ref, r_ref, o_ref):
        o_ref[...] = (g_ref[...] + r_ref[...].astype(F32)).astype(BF16)

    return pl.pallas_call(
        body, name=name,
        grid_spec=pltpu.PrefetchScalarGridSpec(
            num_scalar_prefetch=1, grid=(J,),
            in_specs=[pl.BlockSpec((None, None, r, C), lambda j, core_ref: (core_ref[0], j, 0, 0)),
                      pl.BlockSpec((None, r, C), lambda j, core_ref: (j, 0, 0))],
            out_specs=pl.BlockSpec((None, r, C), lambda j, core_ref: (j, 0, 0))),
        out_shape=jax.ShapeDtypeStruct((J, r, C), BF16),
        compiler_params=_params(1),
    )(core, g32, recv)


def _chip_sum(g32, recv_sib, recv_chips, core_chip, name):
    _, J, r, C = g32.shape

    def body(idx_ref, g_ref, s_ref, o_ref_in, o_ref):
        total = g_ref[...] + s_ref[...].astype(F32)
        for k in range(3):
            total = total + o_ref_in[k].astype(F32)
        o_ref[...] = total

    return pl.pallas_call(
        body, name=name,
        grid_spec=pltpu.PrefetchScalarGridSpec(
            num_scalar_prefetch=1, grid=(1,),
            in_specs=[pl.BlockSpec((None, None, r, C), lambda i, idx: (idx[0], idx[1], 0, 0)),
                      pl.BlockSpec((None, r, C), lambda i, idx: (idx[1], 0, 0)),
                      pl.BlockSpec((3, r, C), lambda i, idx: (0, 0, 0))],
            out_specs=pl.BlockSpec((None, r, C), lambda i, idx: (idx[0], 0, 0))),
        out_shape=jax.ShapeDtypeStruct((2, r, C), F32),
        compiler_params=_params(1),
    )(core_chip, g32, recv_sib, recv_chips)


ICI_US_PER_ELEMENT = 4.6e-5


class _Reducer:
    def __init__(self, core_idx, core_chip):
        self.core_idx, self.core_chip = core_idx, core_chip
        self.grads, self.halves, self.reduced = {}, {}, {}
        self.ready_swap, self.ready_exchange, self.ready_join = [], [], []
        self.inflight, self.current = ([], [], [], None), None
        self.flushes = 0
        self.extra, self.extra_out = None, None

    def add(self, name, grad_pair):
        self.grads[name] = grad_pair
        self.ready_swap.append(name)

    def comm(self, budget_us):
        swaps, self.ready_swap = self.ready_swap, []
        joins, self.ready_join = self.ready_join, []
        exchanges, waiting = [], []
        for item in self.ready_exchange:
            cost = ICI_US_PER_ELEMENT * 2 * item[2].shape[1] * item[2].shape[2]
            if cost <= budget_us:
                exchanges.append(item)
                budget_us -= cost
            else:
                waiting.append(item)
        self.ready_exchange = waiting
        parts = []
        if swaps:
            parts.append(_SwapComm([self.grads[n][1] for n in swaps]))
        if exchanges:
            parts.append(_ExchangeComm([pair for _, _, pair in exchanges]))
        if joins:
            parts.append(_JoinComm([self.halves[n] for n in joins]))
        extra, self.extra = self.extra, None
        if extra is not None:
            parts.append(extra)
        self.inflight = (swaps, exchanges, joins, extra)
        self.current = _CommList(parts) if parts else None
        return self.current

    def done(self, comm_outs):
        if self.current is None:
            return
        swaps, exchanges, joins, extra = self.inflight
        outs = iter(self.current.split_outputs(list(comm_outs)))
        if swaps:
            for n, recv in zip(swaps, next(outs)):
                pair = _pair_sum(self.grads[n][0], recv, self.core_idx, "pair_sum_" + n)
                self.ready_exchange.append((n, recv, pair))
        if exchanges:
            for (n, recv, _), chips in zip(exchanges, next(outs)):
                self.halves[n] = _chip_sum(self.grads[n][0], recv, chips, self.core_chip, "chip_sum_" + n)
                self.ready_join.append(n)
        if joins:
            self.reduced.update(zip(joins, next(outs)))
        if extra is not None:
            self.extra_out = next(outs)
        self.current = None

    def run(self, kernel, budget_us, *args, **kwargs):
        if budget_us is None:
            return kernel(*args, comm=None, **kwargs)[0]
        outs, comm_outs = kernel(*args, comm=self.comm(budget_us), **kwargs)
        self.done(comm_outs)
        return outs

    def step(self):
        comm = self.comm(float("inf"))
        self.flushes += 1
        self.done(_run_comm(comm, "grad_reduce_tail_%d" % self.flushes))


BIG_WEIGHTS = ("ffn1_w_gate", "ffn1_w_up", "ffn1_w_down", "w_in", "w_attn_o", "w_conv_o", "w_out",
               "ffn2_w_gate", "ffn2_w_up", "ffn2_w_down")
VECTORS = ("norm_ffn1_g", "norm_mix_g", "conv_b_dw", "conv_ln_g", "conv_ln_b", "norm_ffn2_g", "final_norm_g")
ROW_DMOD0, ROW_DMOD1, ROW_VEC, ROW_SINK, ROW_CONVW, SMALL_ROWS = 0, 16, 33, 40, 41, 72


FFN1_WEIGHTS = ("ffn1_w_gate", "ffn1_w_up", "ffn1_w_down")
FFN2_WEIGHTS = ("ffn2_w_gate", "ffn2_w_up", "ffn2_w_down")
MIX_WEIGHTS = ("w_in", "w_attn_o", "w_conv_o", "w_out")
COL_SHARDED = ("ffn1_w_gate", "ffn1_w_up", "ffn2_w_gate", "ffn2_w_up", "w_in")


def _local_grads(x, target, mod, slots, small, seq, core_idx, core_chip):
    T, D = x.shape
    B = T // seq
    mods = [mod[:, k][:, None, :] for k in range(N_MOD)]
    sh1, sc1, g1, sh2, sc2, g2, sh3, sc3, g3 = mods
    w = dict(zip(FFN1_WEIGHTS, _run_comm(_GatherComm([slots[n] for n in FFN1_WEIGHTS]), "gather_ffn1")))

    (h1, a1, u1, f1, x1), outs = _ffn_fwd(
        x, small["norm_ffn1_g"], sc1, sh1, g1, w["ffn1_w_gate"], w["ffn1_w_up"], w["ffn1_w_down"], seq, "ffn1_fwd",
        comm=_GatherComm([slots[n] for n in MIX_WEIGHTS]))
    w["w_in"] = outs[0]
    w_ao, w_co, w_o = [t.reshape(D, D) for t in outs[1:]]
    w_in_full = w["w_in"].reshape(IN_WIDTH, D)
    q_end, v_end = D, D + 4 * HEAD_DIM
    w_in_cols = jnp.concatenate([w_in_full[:q_end], w_in_full[v_end:], w_in_full[q_end:v_end]], axis=0)
    (h2, proj), _ = _in_proj(x1, small["norm_mix_g"], sc2, sh2, w_in_cols, seq)
    (o, lse), (w["ffn2_w_gate"], w["ffn2_w_up"]) = _attn_fwd(
        proj, small["attn_sinks"], B, seq, comm=_GatherComm([slots["ffn2_w_gate"], slots["ffn2_w_up"]]))
    (ydw, z), (w["ffn2_w_down"],) = _conv_fwd(
        proj, small["conv_w_dw"], small["conv_b_dw"], small["conv_ln_g"], small["conv_ln_b"], B, seq,
        comm=_GatherComm([slots["ffn2_w_down"]]))
    ya, yc, merged, mo, x2 = _merge(o, z, proj, w_ao, w_co, w_o, x1, g2, seq)
    (h3, a3, u3, f3, x3), _ = _ffn_fwd(x2, small["norm_ffn2_g"], sc3, sh3, g3, w["ffn2_w_gate"], w["ffn2_w_up"],
                                       w["ffn2_w_down"], seq, "ffn2_fwd")
    dx3, loss_parts, d_final_g = _final_loss(x3, small["final_norm_g"], target)

    red = _Reducer(core_idx, core_chip)

    def weight_grad(name, budget_us, a, a_spec, b, b_spec, rows, cols):
        red.add(name, red.run(_wgrad, budget_us, a, a_spec, b, b_spec, rows, cols, T, "dw_" + name))

    def ffn_backward(prefix, dw_budget_us, dxo, xin, h, a, u, f, gn, sc, gate, before_weight_grads=None):
        da, du, s, df, dx, dgate, dsc, dsh, dgn = red.run(
            _ffn_bwd, 170, dxo, xin, f, a, u, gn, sc, gate, w[prefix + "_w_gate"], w[prefix + "_w_up"],
            w[prefix + "_w_down"], seq, prefix + "_bwd")
        if before_weight_grads is not None:
            before_weight_grads(dgate, dsc, dsh, dgn)
        weight_grad(prefix + "_w_down", dw_budget_us, s, _spec_chip_major(FF_SHARD), df, _spec_rows(D), FF_SHARD, D)
        weight_grad(prefix + "_w_gate", dw_budget_us, da, _spec_chip_major(FF_SHARD), h, _spec_rows(D), FF_SHARD, D)
        weight_grad(prefix + "_w_up", dw_budget_us, du, _spec_chip_major(FF_SHARD), h, _spec_rows(D), FF_SHARD, D)
        return dx, dgate, dsc, dsh, dgn

    dx2, dg3, dsc3, dsh3, d_gn3 = ffn_backward("ffn2", None, dx3, x2, h3, a3, u3, f3, small["norm_ffn2_g"], sc3, g3)

    dmo, dya, dyc, dga, dgc, do, dz, dg2 = red.run(_merge_bwd, 45, dx2, mo, g2, proj, ya, yc, w_o, w_ao, w_co, seq)
    shard = D // N_CHIP
    weight_grad("w_out", None, merged, _spec_col_block(shard), dmo, _spec_rows(D), shard, D)
    weight_grad("w_attn_o", None, o, _spec_col_block(shard), dya, _spec_rows(D), shard, D)
    weight_grad("w_conv_o", None, z, _spec_col_block(shard), dyc, _spec_rows(D), shard, D)
    dq, dkp, dko, dvp, dvo, dsink_steps = red.run(_attn_bwd, 100, proj, small["attn_sinks"], o, do, lse, B, seq)
    dca, dcb, d_conv_w, d_conv_b, d_ln_g, d_ln_b = red.run(
        _conv_bwd, 165, proj, dz, ydw, small["conv_w_dw"], small["conv_ln_g"], small["conv_ln_b"], B, seq)

    def band_sum(own, prev):
        prev = prev.reshape(B, seq // BLOCK, BLOCK, 2 * HEAD_DIM)
        moved = jnp.concatenate([prev[:, 1:], jnp.zeros_like(prev[:, :1])], axis=1)
        return (own + moved.reshape(T, 2 * HEAD_DIM)).astype(BF16)

    dproj = jnp.concatenate([dq, band_sum(dko, dkp), band_sum(dvo, dvp), dca, dcb, dga, dgc], axis=1)
    dproj = dproj.reshape(T, N_CHIP, IN_SHARD).transpose(1, 0, 2)
    weight_grad("w_in", 60, dproj, _spec_chip_major(IN_SHARD), h2, _spec_rows(D), IN_SHARD, D)
    dx1, dsc2, dsh2, d_gn2 = red.run(_in_proj_bwd, 90, dproj, w["w_in"], x1, small["norm_mix_g"], sc2, dx2, seq)

    def gather_small_grads(dg1, dsc1, dsh1, d_gn1):
        dmod = jnp.concatenate([dsh1, dsc1, dg1, dsh2, dsc2, dg2, dsh3, dsc3, dg3], axis=1)
        d_sinks = jnp.sum(dsink_steps, axis=0)
        vec_grads = {"norm_ffn1_g": d_gn1, "norm_mix_g": d_gn2, "conv_b_dw": d_conv_b, "conv_ln_g": d_ln_g,
                     "conv_ln_b": d_ln_b, "norm_ffn2_g": d_gn3, "final_norm_g": d_final_g}
        block = jnp.zeros((SMALL_ROWS, D), F32)
        block = block.at[ROW_DMOD0:ROW_DMOD0 + N_MOD].set(dmod[0]).at[ROW_DMOD1:ROW_DMOD1 + N_MOD].set(dmod[1])
        block = block.at[ROW_VEC:ROW_VEC + len(VECTORS)].set(jnp.concatenate([vec_grads[n] for n in VECTORS], axis=0))
        block = block.at[ROW_SINK, :2 * HEAD_DIM].set(d_sinks[0])
        block = block.at[ROW_CONVW:ROW_CONVW + CONV_WIDTH].set(d_conv_w[:CONV_WIDTH])
        red.extra = _Gather8Comm(block)

    dx0, _, _, _, _ = ffn_backward("ffn1", 38, dx1, x, h1, a1, u1, f1, small["norm_ffn1_g"], sc1, g1,
                                   before_weight_grads=gather_small_grads)
    return loss_parts, dx0, red, red.extra_out[0]


def kernel(x, c, w_ada, b_ada, norm_ffn1_g, ffn1_w_gate, ffn1_w_up, ffn1_w_down, norm_mix_g, w_in, attn_sinks, w_attn_o, conv_w_dw, conv_b_dw, conv_ln_g, conv_ln_b, w_conv_o, w_out, norm_ffn2_g, ffn2_w_gate, ffn2_w_up, ffn2_w_down, final_norm_g, loss_target, m_w_ada, m_b_ada, m_norm_ffn1_g, m_ffn1_w_gate, m_ffn1_w_up, m_ffn1_w_down, m_norm_mix_g, m_w_in, m_attn_sinks, m_w_attn_o, m_conv_w_dw, m_conv_b_dw, m_conv_ln_g, m_conv_ln_b, m_w_conv_o, m_w_out, m_norm_ffn2_g, m_ffn2_w_gate, m_ffn2_w_up, m_ffn2_w_down, m_final_norm_g, v_w_ada, v_b_ada, v_norm_ffn1_g, v_ffn1_w_gate, v_ffn1_w_up, v_ffn1_w_down, v_norm_mix_g, v_w_in, v_attn_sinks, v_w_attn_o, v_conv_w_dw, v_conv_b_dw, v_conv_ln_g, v_conv_ln_b, v_w_conv_o, v_w_out, v_norm_ffn2_g, v_ffn2_w_gate, v_ffn2_w_up, v_ffn2_w_down, v_final_norm_g):
    args = dict(locals())
    B, seq, D = x.shape
    T = B * seq
    xi, yi, ci = _position()
    chip = 2 * xi + yi
    dev = 4 * xi + 2 * yi + ci

    def shard_2d(prefix, name):
        t = args[prefix + name][0]
        return t.T if name in COL_SHARDED else t

    big = {n: shard_2d("", n) for n in BIG_WEIGHTS}
    final_g = final_norm_g[None, :]
    vec_w = {n: (args[n] if n != "final_norm_g" else final_g) for n in VECTORS}

    conv_cols = D // N_CHIP
    conv_flat = jnp.pad(conv_w_dw[0].reshape(-1), (0, 8 * D - CONV_WIDTH * conv_cols)).reshape(8, D)
    first = _gather8(jnp.concatenate([jnp.pad(c, ((0, 8 - B), (0, 0))), conv_flat], axis=0), "gather_c")
    c_all = first[:, :B].reshape(N_DEV * B, D)
    conv_taps = first[::2, 8:].reshape(N_CHIP, 8 * D)[:, :CONV_WIDTH * conv_cols]
    conv_taps = conv_taps.reshape(N_CHIP, CONV_WIDTH, conv_cols).transpose(1, 0, 2).reshape(CONV_WIDTH, D)
    conv_taps = jnp.pad(conv_taps, ((0, CONV_PAD - CONV_WIDTH), (0, 0)))

    ada_cols = w_ada.shape[2]
    b_cols = lax.dynamic_slice(b_ada, (0, chip * ada_cols), (1, ada_cols))
    mod_part = _ada_fwd(c_all, w_ada[0], b_cols).reshape(N_DEV, B, ada_cols)
    mod = _mod_exchange(mod_part).transpose(1, 0, 2).reshape(B, N_MOD, D)

    core_idx = jnp.reshape(ci, (1,)).astype(jnp.int32)
    chip_idx = jnp.reshape(chip, (1,)).astype(jnp.int32)
    core_chip = jnp.stack([ci, chip]).astype(jnp.int32)
    slots = {n: _cast_slot(big[n], chip_idx, "cast_" + n) for n in BIG_WEIGHTS}

    small = dict(vec_w)
    small["attn_sinks"] = attn_sinks
    small["conv_w_dw"] = conv_taps

    loss_parts, dx, red, small_all = _local_grads(
        x.reshape(T, D), loss_target.reshape(T, D), mod, slots, small, seq, core_idx, core_chip)

    loss = lax.psum((0.5 / D) * jnp.sum(loss_parts), ("x", "y", "c"))
    grad_x = dx.reshape(B, seq, D)
    out = {}


    def pack_small(prefix):
        rows = [args[prefix + "b_ada"].reshape(N_MOD, D)]
        rows += [args[prefix + n].reshape(1, D) for n in VECTORS]
        rows += [jnp.pad(args[prefix + "attn_sinks"], ((0, 0), (0, D - N_Q_HEADS)))]
        return jnp.pad(jnp.concatenate(rows, axis=0), ((0, 24 - N_MOD - len(VECTORS) - 1), (0, 0)))

    small_sum, sg, sd, sm, sv = _small_adam(small_all, pack_small(""), pack_small("m_"), pack_small("v_"),
                                           ROW_DMOD0, ROW_DMOD1, ROW_VEC)

    def unpack_small(t):
        res = {"b_ada": t[:N_MOD].reshape(1, N_MOD * D)}
        for k, n in enumerate(VECTORS):
            res[n] = t[N_MOD + k].reshape(args[n].shape)
        res["attn_sinks"] = t[N_MOD + len(VECTORS), :N_Q_HEADS].reshape(1, N_Q_HEADS)
        return res

    unpacked = [unpack_small(t) for t in (sg, sd, sm, sv)]
    for n in ("b_ada", "attn_sinks") + VECTORS:
        out[n] = tuple(u[n] for u in unpacked)

    conv_g = lax.dynamic_slice(small_sum, (ROW_CONVW, chip * conv_cols), (CONV_WIDTH, conv_cols))
    d, mn, vn = red.run(_adam_call, None, conv_w_dw[0], conv_g, m_conv_w_dw[0], v_conv_w_dw[0], "adam_conv_w_dw")
    out["conv_w_dw"] = tuple(t[None] for t in (conv_g, d, mn, vn))

    dmod_rows = jnp.stack([small_all[:, ROW_DMOD0:ROW_DMOD0 + N_MOD], small_all[:, ROW_DMOD1:ROW_DMOD1 + N_MOD]], axis=1)
    dmod_all = dmod_rows.reshape(N_DEV * B, N_MOD * D)
    dmod_cols = lax.dynamic_slice(dmod_all, (0, chip * ada_cols), (N_DEV * B, ada_cols))
    ada_out = red.run(_ada_adam, 35, c_all.T, dmod_cols, w_ada[0], m_w_ada[0], v_w_ada[0])
    out["w_ada"] = tuple(t[None] for t in ada_out)

    for n in FFN2_WEIGHTS + MIX_WEIGHTS[1:] + MIX_WEIGHTS[:1] + FFN1_WEIGHTS:
        while n not in red.reduced:
            red.step()
        g = red.reduced[n].reshape(big[n].shape)
        d, mn, vn = red.run(_adam_call, None, big[n], g, shard_2d("m_", n), shard_2d("v_", n), "adam_" + n)
        out[n] = tuple((t.T if n in COL_SHARDED else t)[None] for t in (g, d, mn, vn))

    order = ("w_ada", "b_ada", "norm_ffn1_g", "ffn1_w_gate", "ffn1_w_up", "ffn1_w_down", "norm_mix_g", "w_in",
             "attn_sinks", "w_attn_o", "conv_w_dw", "conv_b_dw", "conv_ln_g", "conv_ln_b", "w_conv_o", "w_out",
             "norm_ffn2_g", "ffn2_w_gate", "ffn2_w_up", "ffn2_w_down", "final_norm_g")
    return (loss, grad_x, *[out[n][0] for n in order], *[out[n][1] for n in order],
            *[out[n][2] for n in order], *[out[n][3] for n in order])
```

```python
import functools

import jax
import jax.numpy as jnp
from jax import lax
from jax.experimental import pallas as pl
from jax.experimental.pallas import tpu as pltpu

F32 = jnp.float32
BF16 = jnp.bfloat16

D_MODEL = 1024
D_FF = 2816
N_CHIP = 4
N_DEV = 8
FF_SHARD = D_FF // N_CHIP
IN_WIDTH = 5376
IN_SHARD = IN_WIDTH // N_CHIP
HEAD_DIM = 64
N_Q_HEADS = 16
N_KV_HEADS = 2
BLOCK = 128
CONV_WIDTH = 31
CONV_PAD = 32
N_MOD = 9
EPS = 1e-6
FFN_RESIDUAL = 0.5
ATTN_SCALE = HEAD_DIM ** -0.5
MASK_VALUE = -1e30

ADAM_LR = 0.001
ADAM_B1 = 0.9
ADAM_B2 = 0.999
ADAM_EPS = 1e-08
ADAM_WD = 0.01
ADAM_STEP = 10

COLB_Q, COLB_CA, COLB_CB, COLB_GA, COLB_GC = 0, 1, 2, 3, 4
COLB_K, COLB_V = 40, 41
PROJ_TILE = 768

VMEM_LIMIT = 56 * 1024 * 1024
MESH = pl.DeviceIdType.MESH
ANY = pl.BlockSpec(memory_space=pl.ANY)
VMEM_SPEC = pl.BlockSpec(memory_space=pltpu.VMEM)
SMEM_SPEC = pl.BlockSpec(memory_space=pltpu.SMEM)


def _params(n_grid):
    return pltpu.CompilerParams(dimension_semantics=("arbitrary",) * n_grid, vmem_limit_bytes=VMEM_LIMIT)


def _tile(n, pref):
    t = min(n, pref)
    while n % t:
        t //= 2
    return t


def _row_tile(rows, cap):
    for t in range(min(rows, cap) // 16 * 16, 0, -16):
        if rows % t == 0:
            return t
    return rows


def _sigmoid(v):
    return 1.0 / (1.0 + jnp.exp(-v))


def _dot_nn(a, b):
    return lax.dot_general(a, b, (((1,), (0,)), ((), ())), preferred_element_type=F32)


def _dot_nt(a, b):
    return lax.dot_general(a, b, (((1,), (1,)), ((), ())), preferred_element_type=F32)


def _dot_tn(a, b):
    return lax.dot_general(a, b, (((0,), (0,)), ((), ())), preferred_element_type=F32)


ROW_CHUNK = 16


def _for_row_chunks(n_rows, fn):
    for r in range(0, n_rows, ROW_CHUNK):
        fn(slice(r, r + ROW_CHUNK))


def _norm_mod(xv, gn, sc, sh):
    r = lax.rsqrt(jnp.mean(xv * xv, axis=-1, keepdims=True) + EPS)
    return ((xv * r) * gn) * (1.0 + sc) + sh


def _accumulate(ref, first, value):
    @pl.when(first)
    def _():
        ref[...] = value

    @pl.when(jnp.logical_not(first))
    def _():
        ref[...] += value


def _norm_mod_bwd(dh, xv, gn, sc, dxo, first_of_batch, first, dx_ref, dsc_ref, dsh_ref, dgn_ref):
    r = lax.rsqrt(jnp.mean(xv * xv, axis=-1, keepdims=True) + EPS)
    xh = xv * r
    _accumulate(dsh_ref, first_of_batch, jnp.sum(dh, axis=0, keepdims=True))
    _accumulate(dsc_ref, first_of_batch, jnp.sum(dh * (xh * gn), axis=0, keepdims=True))
    dn = dh * (1.0 + sc)
    _accumulate(dgn_ref, first, jnp.sum(dn * xh, axis=0, keepdims=True))
    dxh = dn * gn
    dx_ref[...] = dxo + r * (dxh - xh * jnp.mean(dxh * xh, axis=-1, keepdims=True))


CHIP_FLIPS = ((1, 0), (0, 1), (1, 1))


def _position():
    return lax.axis_index("x"), lax.axis_index("y"), lax.axis_index("c")


def _flip(v, f):
    return 1 - v if f else v


class _GatherComm:
    def __init__(self, bufs):
        n = len(bufs)
        self.n = n
        self.operands = list(bufs)
        self.out_shape = [jax.ShapeDtypeStruct(b.shape, b.dtype) for b in bufs]
        self.aliases = {i: i for i in range(n)}
        self.sems = [pltpu.SemaphoreType.DMA((6 * n,)), pltpu.SemaphoreType.DMA((6 * n,))]
        self.rows = [b.shape[1] // 2 for b in bufs]

    def _half(self, ref, i, which):
        return ref.at[pl.ds(which * self.rows[i], self.rows[i]), :]

    def _ici(self, cins, couts, sems, i, k, dst_chip, to):
        x, y, c = _position()
        return pltpu.make_async_remote_copy(
            src_ref=self._half(cins[i].at[2 * x + y], i, c), dst_ref=self._half(couts[i].at[dst_chip], i, c),
            send_sem=sems[0].at[3 * i + k], recv_sem=sems[1].at[3 * i + k], device_id=to, device_id_type=MESH)

    def _d2d(self, couts, sems, i, k, src_chip, which):
        x, y, c = _position()
        place = self._half(couts[i].at[src_chip], i, which)
        return pltpu.make_async_remote_copy(
            src_ref=place, dst_ref=place, send_sem=sems[0].at[3 * self.n + 3 * i + k],
            recv_sem=sems[1].at[3 * self.n + 3 * i + k], device_id=(x, y, 1 - c), device_id_type=MESH)

    def _peers(self):
        x, y, _ = _position()
        return [(_flip(x, fx), _flip(y, fy)) for fx, fy in CHIP_FLIPS]

    def start(self, cins, couts, sems):
        x, y, c = _position()
        for i in range(self.n):
            for k, (px, py) in enumerate(self._peers()):
                self._ici(cins, couts, sems, i, k, 2 * x + y, (px, py, c)).start()

    def finish(self, cins, couts, sems):
        _, _, c = _position()
        peers = self._peers()
        for i in range(self.n):
            for k, (px, py) in enumerate(peers):
                self._ici(cins, couts, sems, i, k, 2 * px + py, (px, py, c)).wait_recv()
                self._d2d(couts, sems, i, k, 2 * px + py, c).start()
        for i in range(self.n):
            for k, (px, py) in enumerate(peers):
                self._d2d(couts, sems, i, k, 2 * px + py, 1 - c).wait_recv()
        for i in range(self.n):
            for k, (px, py) in enumerate(peers):
                self._ici(cins, couts, sems, i, k, 2 * px + py, (px, py, c)).wait_send()
                self._d2d(couts, sems, i, k, 2 * px + py, c).wait_send()


class _ExchangeComm:
    def __init__(self, pairs):
        n = len(pairs)
        self.n = n
        self.operands = list(pairs)
        self.out_shape = [jax.ShapeDtypeStruct((3,) + p.shape[1:], p.dtype) for p in pairs]
        self.aliases = {}
        self.sems = [pltpu.SemaphoreType.DMA((3 * n,)), pltpu.SemaphoreType.DMA((3 * n,))]

    def _copies(self, cins, couts, sems):
        x, y, c = _position()
        peers = [(_flip(x, fx), _flip(y, fy)) for fx, fy in CHIP_FLIPS]
        return [pltpu.make_async_remote_copy(
            src_ref=cins[i].at[2 * px + py], dst_ref=couts[i].at[k], send_sem=sems[0].at[3 * i + k],
            recv_sem=sems[1].at[3 * i + k], device_id=(px, py, c), device_id_type=MESH)
            for i in range(self.n) for k, (px, py) in enumerate(peers)]

    def start(self, cins, couts, sems):
        for cp in self._copies(cins, couts, sems):
            cp.start()

    def finish(self, cins, couts, sems):
        for cp in self._copies(cins, couts, sems):
            cp.wait()


class _SwapComm:
    def __init__(self, grads16):
        n = len(grads16)
        self.n = n
        self.operands = list(grads16)
        self.out_shape = [jax.ShapeDtypeStruct(g.shape[1:], g.dtype) for g in grads16]
        self.aliases = {}
        self.sems = [pltpu.SemaphoreType.DMA((n,)), pltpu.SemaphoreType.DMA((n,))]

    def _copies(self, cins, couts, sems):
        x, y, c = _position()
        return [pltpu.make_async_remote_copy(
            src_ref=cins[i].at[1 - c], dst_ref=couts[i], send_sem=sems[0].at[i], recv_sem=sems[1].at[i],
            device_id=(x, y, 1 - c), device_id_type=MESH) for i in range(self.n)]

    def start(self, cins, couts, sems):
        for cp in self._copies(cins, couts, sems):
            cp.start()

    def finish(self, cins, couts, sems):
        for cp in self._copies(cins, couts, sems):
            cp.wait()


class _JoinComm:
    def __init__(self, halves):
        n = len(halves)
        self.n = n
        self.operands = list(halves)
        self.out_shape = [jax.ShapeDtypeStruct(h.shape, h.dtype) for h in halves]
        self.aliases = {i: i for i in range(n)}
        self.sems = [pltpu.SemaphoreType.DMA((n,)), pltpu.SemaphoreType.DMA((n,))]

    def _copy(self, cins, couts, sems, i, which):
        x, y, c = _position()
        return pltpu.make_async_remote_copy(
            src_ref=cins[i].at[which], dst_ref=couts[i].at[which], send_sem=sems[0].at[i], recv_sem=sems[1].at[i],
            device_id=(x, y, 1 - c), device_id_type=MESH)

    def start(self, cins, couts, sems):
        _, _, c = _position()
        for i in range(self.n):
            self._copy(cins, couts, sems, i, c).start()

    def finish(self, cins, couts, sems):
        _, _, c = _position()
        for i in range(self.n):
            self._copy(cins, couts, sems, i, 1 - c).wait_recv()
        for i in range(self.n):
            self._copy(cins, couts, sems, i, c).wait_send()


class _Gather8Comm:
    def __init__(self, block):
        self.operands = [block]
        self.out_shape = [jax.ShapeDtypeStruct((N_DEV,) + block.shape, block.dtype)]
        self.aliases = {}
        self.sems = [pltpu.SemaphoreType.DMA((N_DEV - 1,)), pltpu.SemaphoreType.DMA((N_DEV - 1,)),
                     pltpu.SemaphoreType.DMA]
        self.flips = [(fx, fy, fc) for fx in (0, 1) for fy in (0, 1) for fc in (0, 1) if (fx, fy, fc) != (0, 0, 0)]

    def _peers(self):
        x, y, c = _position()
        return [(_flip(x, fx), _flip(y, fy), _flip(c, fc)) for fx, fy, fc in self.flips]

    def _copy(self, cins, couts, sems, k, block, to):
        return pltpu.make_async_remote_copy(src_ref=cins[0], dst_ref=couts[0].at[block], send_sem=sems[0].at[k],
                                            recv_sem=sems[1].at[k], device_id=to, device_id_type=MESH)

    def _mine(self, cins, couts, sems):
        x, y, c = _position()
        return pltpu.make_async_copy(cins[0], couts[0].at[4 * x + 2 * y + c], sems[2])

    def start(self, cins, couts, sems):
        x, y, c = _position()
        self._mine(cins, couts, sems).start()
        for k, peer in enumerate(self._peers()):
            self._copy(cins, couts, sems, k, 4 * x + 2 * y + c, peer).start()

    def finish(self, cins, couts, sems):
        for k, (px, py, pc) in enumerate(self._peers()):
            self._copy(cins, couts, sems, k, 4 * px + 2 * py + pc, (px, py, pc)).wait_recv()
        for k, peer in enumerate(self._peers()):
            self._copy(cins, couts, sems, k, 0, peer).wait_send()
        self._mine(cins, couts, sems).wait()


class _CommList:
    def __init__(self, parts):
        self.parts = list(parts)
        self.operands = [t for p in self.parts for t in p.operands]
        self.out_shape = [t for p in self.parts for t in p.out_shape]
        self.sems = [t for p in self.parts for t in p.sems]
        self.aliases = {}
        n_in = n_out = 0
        for p in self.parts:
            self.aliases.update({n_in + i: n_out + j for i, j in p.aliases.items()})
            n_in += len(p.operands)
            n_out += len(p.out_shape)

    def _split(self, cins, couts, sems):
        pos = [0, 0, 0]
        for p in self.parts:
            sizes = (len(p.operands), len(p.out_shape), len(p.sems))
            yield p, tuple(seq[a:a + k] for seq, a, k in zip((cins, couts, sems), pos, sizes))
            pos = [a + k for a, k in zip(pos, sizes)]

    def start(self, cins, couts, sems):
        for p, refs in self._split(cins, couts, sems):
            p.start(*refs)

    def finish(self, cins, couts, sems):
        for p, refs in self._split(cins, couts, sems):
            p.finish(*refs)

    def split_outputs(self, outs):
        res, pos = [], 0
        for p in self.parts:
            res.append(outs[pos:pos + len(p.out_shape)])
            pos += len(p.out_shape)
        return res


def _call(body, *, name, grid, in_specs, out_specs, out_shape, operands, scratch_shapes=(), comm=None):
    n_grid = len(grid)
    if comm is None:
        return pl.pallas_call(
            body, name=name, grid=grid, in_specs=list(in_specs), out_specs=list(out_specs), out_shape=list(out_shape),
            scratch_shapes=list(scratch_shapes), compiler_params=_params(n_grid))(*operands), ()
    counts = (len(in_specs), len(comm.operands), len(out_specs), len(comm.out_shape), len(scratch_shapes),
              len(comm.sems))

    def fused(*refs):
        parts, pos = [], 0
        for k in counts:
            parts.append(refs[pos:pos + k])
            pos += k
        ins, cins, outs, couts, scr, sems = parts
        first = functools.reduce(jnp.logical_and, [pl.program_id(d) == 0 for d in range(n_grid)])
        last = functools.reduce(jnp.logical_and, [pl.program_id(d) == grid[d] - 1 for d in range(n_grid)])

        @pl.when(first)
        def _():
            comm.start(cins, couts, sems)

        body(*ins, *outs, *scr)

        @pl.when(last)
        def _():
            comm.finish(cins, couts, sems)

    res = pl.pallas_call(
        fused, name=name, grid=grid, in_specs=list(in_specs) + [ANY] * counts[1],
        out_specs=list(out_specs) + [ANY] * counts[3], out_shape=list(out_shape) + list(comm.out_shape),
        scratch_shapes=list(scratch_shapes) + list(comm.sems),
        input_output_aliases={counts[0] + i: counts[2] + j for i, j in comm.aliases.items()},
        compiler_params=_params(n_grid))(*operands, *comm.operands)
    return res[:counts[2]], res[counts[2]:]


def _run_comm(comm, name):
    k_in, k_out = len(comm.operands), len(comm.out_shape)

    def body(*refs):
        cins, couts, sems = refs[:k_in], refs[k_in:k_in + k_out], refs[k_in + k_out:]
        comm.start(cins, couts, sems)
        comm.finish(cins, couts, sems)

    return pl.pallas_call(
        body, name=name, in_specs=[ANY] * k_in, out_specs=[ANY] * k_out, out_shape=list(comm.out_shape),
        scratch_shapes=list(comm.sems), input_output_aliases=dict(comm.aliases))(*comm.operands)


def _ffn_fwd(x, gn, sc, sh, gate, wg, wu, wd, seq, name, comm=None):
    T, D = x.shape
    J, Fs, _ = wg.shape
    tm = _tile(seq, 1024)
    nb = seq // tm

    def body(x_ref, gn_ref, sc_ref, sh_ref, gate_ref, wg_ref, wu_ref, wd_ref,
             h_ref, a_ref, u_ref, f_ref, xo_ref, hs, acc, s16):
        j = pl.program_id(1)

        @pl.when(j == 0)
        def _():
            hb = _norm_mod(x_ref[...], gn_ref[...], sc_ref[...], sh_ref[...]).astype(BF16)
            hs[...] = hb
            h_ref[...] = hb
            acc[...] = jnp.zeros_like(acc)

        hb = hs[...]
        a_all = _dot_nt(hb, wg_ref[...])
        u_all = _dot_nt(hb, wu_ref[...])

        def swiglu_rows(rows):
            a = a_all[rows, :]
            u = u_all[rows, :]
            a_ref[rows, :] = a.astype(BF16)
            u_ref[rows, :] = u.astype(BF16)
            s16[rows, :] = ((a * _sigmoid(a)) * u).astype(BF16)

        _for_row_chunks(tm, swiglu_rows)
        acc[...] += _dot_nn(s16[...], wd_ref[...])

        @pl.when(j == J - 1)
        def _():
            f = acc[...]
            f_ref[...] = f.astype(BF16)
            xo_ref[...] = x_ref[...] + (FFN_RESIDUAL * gate_ref[...]) * f

    row = pl.BlockSpec((tm, D), lambda i, j: (i, 0))
    vec = pl.BlockSpec((1, D), lambda i, j: (0, 0))
    per_b = pl.BlockSpec((None, 1, D), lambda i, j: (i // nb, 0, 0))
    hid = pl.BlockSpec((None, tm, Fs), lambda i, j: (j, i, 0))
    return _call(
        body, name=name, grid=(T // tm, J),
        in_specs=[row, vec, per_b, per_b, per_b] + [pl.BlockSpec((None, Fs, D), lambda i, j: (j, 0, 0))] * 3,
        out_specs=[row, hid, hid, row, row],
        out_shape=[jax.ShapeDtypeStruct((T, D), BF16), jax.ShapeDtypeStruct((J, T, Fs), BF16),
                   jax.ShapeDtypeStruct((J, T, Fs), BF16), jax.ShapeDtypeStruct((T, D), BF16),
                   jax.ShapeDtypeStruct((T, D), F32)],
        scratch_shapes=[pltpu.VMEM((tm, D), BF16), pltpu.VMEM((tm, D), F32), pltpu.VMEM((tm, Fs), BF16)],
        operands=(x, gn, sc, sh, gate, wg, wu, wd), comm=comm)


def _ffn_bwd(dxo, x, f, a, u, gn, sc, gate, wg, wu, wd, seq, name, comm=None):
    T, D = x.shape
    J, Fs, _ = wg.shape
    B = T // seq
    tm = _tile(seq, 512)
    nb = seq // tm

    def body(dxo_ref, x_ref, f_ref, a_ref, u_ref, gn_ref, sc_ref, gate_ref, wg_ref, wu_ref, wd_ref,
             da_ref, du_ref, s_ref, df_ref, dx_ref, dgate_ref, dsc_ref, dsh_ref, dgn_ref, dfs, acc):
        i = pl.program_id(0)
        j = pl.program_id(1)
        first_of_batch = i % nb == 0

        @pl.when(j == 0)
        def _():
            dxo_v = dxo_ref[...]
            dfb = ((FFN_RESIDUAL * gate_ref[...]) * dxo_v).astype(BF16)
            dfs[...] = dfb
            df_ref[...] = dfb
            part = jnp.sum((FFN_RESIDUAL * f_ref[...].astype(F32)) * dxo_v, axis=0, keepdims=True)
            _accumulate(dgate_ref, first_of_batch, part)
            acc[...] = jnp.zeros_like(acc)

        ds_all = _dot_nt(dfs[...], wd_ref[...])

        def swiglu_bwd_rows(rows):
            ds = ds_all[rows, :]
            av = a_ref[rows, :].astype(F32)
            uv = u_ref[rows, :].astype(F32)
            sig = _sigmoid(av)
            sil = av * sig
            s_ref[rows, :] = (sil * uv).astype(BF16)
            da_ref[rows, :] = (ds * uv * (sig * (1.0 + av * (1.0 - sig)))).astype(BF16)
            du_ref[rows, :] = (ds * sil).astype(BF16)

        _for_row_chunks(tm, swiglu_bwd_rows)
        acc[...] += _dot_nn(da_ref[...], wg_ref[...]) + _dot_nn(du_ref[...], wu_ref[...])

        @pl.when(j == J - 1)
        def _():
            _norm_mod_bwd(acc[...], x_ref[...], gn_ref[...], sc_ref[...], dxo_ref[...],
                          first_of_batch, i == 0, dx_ref, dsc_ref, dsh_ref, dgn_ref)

    row = pl.BlockSpec((tm, D), lambda i, j: (i, 0))
    vec = pl.BlockSpec((1, D), lambda i, j: (0, 0))
    per_b = pl.BlockSpec((None, 1, D), lambda i, j: (i // nb, 0, 0))
    hid = pl.BlockSpec((None, tm, Fs), lambda i, j: (j, i, 0))
    hid_shape = jax.ShapeDtypeStruct((J, T, Fs), BF16)
    per_b_shape = jax.ShapeDtypeStruct((B, 1, D), F32)
    return _call(
        body, name=name, grid=(T // tm, J),
        in_specs=[row, row, row, hid, hid, vec, per_b, per_b]
        + [pl.BlockSpec((None, Fs, D), lambda i, j: (j, 0, 0))] * 3,
        out_specs=[hid, hid, hid, row, row, per_b, per_b, per_b, vec],
        out_shape=[hid_shape, hid_shape, hid_shape, jax.ShapeDtypeStruct((T, D), BF16),
                   jax.ShapeDtypeStruct((T, D), F32), per_b_shape, per_b_shape, per_b_shape,
                   jax.ShapeDtypeStruct((1, D), F32)],
        scratch_shapes=[pltpu.VMEM((tm, D), BF16), pltpu.VMEM((tm, D), F32)],
        operands=(dxo, x, f, a, u, gn, sc, gate, wg, wu, wd), comm=comm)


def _wgrad(a, a_spec, b, b_spec, rows, cols, n_tok, name, comm=None):
    tk = _tile(n_tok, 1024)
    nk = n_tok // tk
    half = rows // 2

    def body(a_ref, b_ref, o32_ref, o16_ref, acc):
        k = pl.program_id(1)

        @pl.when(k == 0)
        def _():
            acc[...] = jnp.zeros_like(acc)

        acc[...] += _dot_tn(a_ref[...], b_ref[...])

        @pl.when(k == nk - 1)
        def _():
            for h in range(2):
                v = acc[h * half:(h + 1) * half, :]
                o32_ref[h] = v
                o16_ref[h] = v.astype(BF16)

    out_spec = pl.BlockSpec((2, None, half, cols), lambda j, k: (0, j, 0, 0))
    return _call(
        body, name=name, grid=(N_CHIP, nk),
        in_specs=[a_spec(tk), b_spec(tk)],
        out_specs=[out_spec, out_spec],
        out_shape=[jax.ShapeDtypeStruct((2, N_CHIP, half, cols), F32),
                   jax.ShapeDtypeStruct((2, N_CHIP, half, cols), BF16)],
        scratch_shapes=[pltpu.VMEM((rows, cols), F32)],
        operands=(a, b), comm=comm)


def _spec_rows(width):
    return lambda tk: pl.BlockSpec((tk, width), lambda j, k: (k, 0))


def _spec_chip_major(width):
    return lambda tk: pl.BlockSpec((None, tk, width), lambda j, k: (j, k, 0))


def _spec_col_block(width):
    return lambda tk: pl.BlockSpec((tk, width), lambda j, k: (k, j))


def _in_proj(x, gn, sc, sh, w_in, seq, comm=None):
    T, D = x.shape
    N = w_in.shape[0]
    tm = _tile(seq, 1024)
    nb = seq // tm

    def body(x_ref, gn_ref, sc_ref, sh_ref, w_ref, h_ref, p_ref, hs):
        @pl.when(pl.program_id(1) == 0)
        def _():
            hb = _norm_mod(x_ref[...], gn_ref[...], sc_ref[...], sh_ref[...]).astype(BF16)
            hs[...] = hb
            h_ref[...] = hb

        p_ref[...] = _dot_nt(hs[...], w_ref[...]).astype(BF16)

    row = pl.BlockSpec((tm, D), lambda i, j: (i, 0))
    per_b = pl.BlockSpec((None, 1, D), lambda i, j: (i // nb, 0, 0))
    return _call(
        body, name="mix_in_proj", grid=(T // tm, N // PROJ_TILE),
        in_specs=[row, pl.BlockSpec((1, D), lambda i, j: (0, 0)), per_b, per_b,
                  pl.BlockSpec((PROJ_TILE, D), lambda i, j: (j, 0))],
        out_specs=[row, pl.BlockSpec((tm, PROJ_TILE), lambda i, j: (i, j))],
        out_shape=[jax.ShapeDtypeStruct((T, D), BF16), jax.ShapeDtypeStruct((T, N), BF16)],
        scratch_shapes=[pltpu.VMEM((tm, D), BF16)],
        operands=(x, gn, sc, sh, w_in), comm=comm)


def _attn_specs(nblk):
    def own(col):
        return lambda b, n: (b * nblk + n, col)

    def prev(col):
        return lambda b, n: (b * nblk + jnp.maximum(n - 1, 0), col)

    kv = (BLOCK, 2 * HEAD_DIM)
    return [pl.BlockSpec((BLOCK, D_MODEL), own(COLB_Q)),
            pl.BlockSpec(kv, prev(COLB_K)), pl.BlockSpec(kv, own(COLB_K)),
            pl.BlockSpec(kv, prev(COLB_V)), pl.BlockSpec(kv, own(COLB_V))]


def _band_operands(prev_ref, own_ref, lo):
    band = jnp.concatenate([prev_ref[...], own_ref[...]], axis=0).astype(F32)
    rolled = pltpu.roll(band, HEAD_DIM, 1)
    zero = jnp.zeros_like(band)
    head0 = jnp.concatenate([jnp.where(lo, band, zero), jnp.where(lo, zero, rolled)], axis=0).astype(BF16)
    head1 = jnp.concatenate([jnp.where(lo, rolled, zero), jnp.where(lo, zero, band)], axis=0).astype(BF16)
    return head0, head1


PAIRS_PER_KV = N_Q_HEADS // 2 // N_KV_HEADS
BAND = 2 * BLOCK


def _band_valid(has_prev):
    qi = lax.broadcasted_iota(jnp.int32, (PAIRS_PER_KV * BLOCK, BAND), 0) & (BLOCK - 1)
    sj = lax.broadcasted_iota(jnp.int32, (PAIRS_PER_KV * BLOCK, BAND), 1)
    rel = qi + BLOCK - sj
    return (rel >= 0) & (rel < BLOCK) & ((sj >= BLOCK) | has_prev)


def _pair_lanes(kvh, pp):
    pair = kvh * PAIRS_PER_KV + pp
    return slice(pair * 2 * HEAD_DIM, (pair + 1) * 2 * HEAD_DIM)


def _stack_pairs(ref, kvh):
    return jnp.concatenate([ref[:, _pair_lanes(kvh, pp)] for pp in range(PAIRS_PER_KV)], axis=0)


def _rows_per_pair(columns):
    return jnp.concatenate(columns, axis=0)


def _attn_fwd(proj, sinks, batch, seq, comm=None):
    T = proj.shape[0]
    nblk = seq // BLOCK

    def body(sink_ref, q_ref, kp_ref, ko_ref, vp_ref, vo_ref, o_ref, lse_ref):
        lo = lax.broadcasted_iota(jnp.int32, (1, 2 * HEAD_DIM), 1) < HEAD_DIM
        head_lane = lax.broadcasted_iota(jnp.int32, (1, N_Q_HEADS), 1)
        valid = _band_valid(pl.program_id(1) > 0)
        k_ops = _band_operands(kp_ref, ko_ref, lo)
        v_ops = _band_operands(vp_ref, vo_ref, lo)
        lse_all = jnp.zeros((BLOCK, N_Q_HEADS), F32)
        col = jnp.zeros((BLOCK, 1), F32)
        side0_row = lax.broadcasted_iota(jnp.int32, (2 * BAND, 2 * HEAD_DIM), 0) < BAND
        low_lane = lax.broadcasted_iota(jnp.int32, (2 * BAND, 2 * HEAD_DIM), 1) < HEAD_DIM
        side_ones = jnp.where(side0_row == low_lane, 1.0, 0.0).astype(BF16)
        for kvh in range(N_KV_HEADS):
            s_all = _dot_nt(_stack_pairs(q_ref, kvh), k_ops[kvh]) * ATTN_SCALE
            weights, maxes, sink_terms = [], [], []
            for side in range(2):
                heads = [2 * (kvh * PAIRS_PER_KV + pp) + side for pp in range(PAIRS_PER_KV)]
                sink = _rows_per_pair([col + sink_ref[0, h] for h in heads])
                s = jnp.where(valid, s_all[:, side * BAND:(side + 1) * BAND], MASK_VALUE)
                m = jnp.maximum(jnp.max(s, axis=-1, keepdims=True), sink)
                weights.append(jnp.where(valid, jnp.exp(s - m), 0.0).astype(BF16))
                maxes.append(m)
                sink_terms.append(jnp.exp(sink - m))
            p_all = jnp.concatenate(weights, axis=1)
            den = _dot_nn(p_all, side_ones) + jnp.where(lo, sink_terms[0], sink_terms[1])
            out = _dot_nn(p_all, v_ops[kvh]) / den
            for pp in range(PAIRS_PER_KV):
                o_ref[:, _pair_lanes(kvh, pp)] = out[pp * BLOCK:(pp + 1) * BLOCK].astype(BF16)
            for side in range(2):
                lse = maxes[side] + jnp.log(den[:, side * HEAD_DIM:side * HEAD_DIM + 1])
                for pp in range(PAIRS_PER_KV):
                    h = 2 * (kvh * PAIRS_PER_KV + pp) + side
                    lse_all = jnp.where(head_lane == h, lse[pp * BLOCK:(pp + 1) * BLOCK], lse_all)
        lse_ref[...] = lse_all

    return _call(
        body, name="attn_fwd", grid=(batch, nblk),
        in_specs=[SMEM_SPEC] + _attn_specs(nblk),
        out_specs=[pl.BlockSpec((BLOCK, D_MODEL), lambda b, n: (b * nblk + n, 0)),
                   pl.BlockSpec((BLOCK, N_Q_HEADS), lambda b, n: (b * nblk + n, 0))],
        out_shape=[jax.ShapeDtypeStruct((T, D_MODEL), BF16), jax.ShapeDtypeStruct((T, N_Q_HEADS), F32)],
        operands=(sinks, proj, proj, proj, proj, proj), comm=comm)


def _conv_u(ca, cb):
    return ca.astype(F32) * _sigmoid(cb.astype(F32))


def _conv_specs(ts, tiles_per_seq):
    per_tile = ts // CONV_PAD

    def tile(col):
        return lambda b, t: (b * tiles_per_seq + t, col)

    def before(col):
        return lambda b, t: (jnp.maximum((b * tiles_per_seq + t) * per_tile - 1, 0), col)

    return [pl.BlockSpec((ts, D_MODEL), tile(COLB_CA)), pl.BlockSpec((ts, D_MODEL), tile(COLB_CB)),
            pl.BlockSpec((CONV_PAD, D_MODEL), before(COLB_CA)), pl.BlockSpec((CONV_PAD, D_MODEL), before(COLB_CB))]


SUBLANES = 8


def _fill_upad(upad, ca_ref, cb_ref, cah_ref, cbh_ref, t):
    halo = _conv_u(cah_ref[...], cbh_ref[...])
    upad[0, 0:CONV_PAD, :] = jnp.where(t > 0, halo, jnp.zeros_like(halo))
    upad[0, CONV_PAD:, :] = _conv_u(ca_ref[...], cb_ref[...])


def _fill_shifted(pad):
    rows = pad.shape[1] - SUBLANES
    for b in range(1, SUBLANES):
        pad[b, 0:rows, :] = pad[0, b:b + rows, :]


def _shifted_rows(pad, offset, rows):
    b = offset % SUBLANES
    return pad[b, offset - b:offset - b + rows, :]


def _layernorm_stats(y):
    mu = jnp.mean(y, axis=-1, keepdims=True)
    yc = y - mu
    rstd = lax.rsqrt(jnp.mean(yc * yc, axis=-1, keepdims=True) + EPS)
    return yc * rstd, rstd


def _conv_fwd(proj, w_dw, b_dw, ln_g, ln_b, batch, seq, comm=None):
    T = proj.shape[0]
    ts = _tile(seq, 256)
    nt = seq // ts
    shift = CONV_PAD - (CONV_WIDTH - 1)

    def body(ca_ref, cb_ref, cah_ref, cbh_ref, w_ref, b_ref, g_ref, beta_ref, y_ref, z_ref, upad):
        _fill_upad(upad, ca_ref, cb_ref, cah_ref, cbh_ref, pl.program_id(1))
        _fill_shifted(upad)
        y = jnp.zeros((ts, D_MODEL), F32) + b_ref[...]
        for k in range(CONV_WIDTH):
            y = y + w_ref[k:k + 1, :] * _shifted_rows(upad, shift + k, ts)
        y_ref[...] = y
        lnh, _ = _layernorm_stats(y)
        ln = lnh * g_ref[...] + beta_ref[...]
        z_ref[...] = (ln * _sigmoid(ln)).astype(BF16)

    vec = pl.BlockSpec((1, D_MODEL), lambda b, t: (0, 0))
    row = pl.BlockSpec((ts, D_MODEL), lambda b, t: (b * nt + t, 0))
    return _call(
        body, name="conv_fwd", grid=(batch, nt),
        in_specs=_conv_specs(ts, nt) + [pl.BlockSpec((CONV_PAD, D_MODEL), lambda b, t: (0, 0)), vec, vec, vec],
        out_specs=[row, row],
        out_shape=[jax.ShapeDtypeStruct((T, D_MODEL), F32), jax.ShapeDtypeStruct((T, D_MODEL), BF16)],
        scratch_shapes=[pltpu.VMEM((SUBLANES, ts + CONV_PAD, D_MODEL), F32)],
        operands=(proj, proj, proj, proj, w_dw, b_dw, ln_g, ln_b), comm=comm)


def _merge(o, z, proj, w_ao, w_co, w_out, x, gate, seq):
    T, D = x.shape
    tm = _tile(seq, 512)
    nb = seq // tm

    def body(o_ref, z_ref, ga_ref, gc_ref, wao_ref, wco_ref, wout_ref, x_ref, gate_ref,
             ya_ref, yc_ref, mg_ref, mo_ref, xo_ref):
        ya = _dot_nn(o_ref[...], wao_ref[...])
        yc = _dot_nn(z_ref[...], wco_ref[...])
        ya_ref[...] = ya.astype(BF16)
        yc_ref[...] = yc.astype(BF16)
        merged = (_sigmoid(ga_ref[...].astype(F32)) * ya + _sigmoid(gc_ref[...].astype(F32)) * yc).astype(BF16)
        mg_ref[...] = merged
        mo = _dot_nn(merged, wout_ref[...])
        mo_ref[...] = mo.astype(BF16)
        xo_ref[...] = x_ref[...] + gate_ref[...] * mo

    row = pl.BlockSpec((tm, D), lambda i: (i, 0))
    mat = pl.BlockSpec((D, D), lambda i: (0, 0))
    act = jax.ShapeDtypeStruct((T, D), BF16)
    return pl.pallas_call(
        body, name="mix_merge", grid=(T // tm,),
        in_specs=[row, row, pl.BlockSpec((tm, D), lambda i: (i, COLB_GA)), pl.BlockSpec((tm, D), lambda i: (i, COLB_GC)),
                  mat, mat, mat, row, pl.BlockSpec((None, 1, D), lambda i: (i // nb, 0, 0))],
        out_specs=[row, row, row, row, row],
        out_shape=[act, act, act, act, jax.ShapeDtypeStruct((T, D), F32)],
        compiler_params=_params(1),
    )(o, z, proj, proj, w_ao, w_co, w_out, x, gate)


def _final_loss(x, gf, target):
    T, D = x.shape
    tm = _tile(T, 512)

    def body(x_ref, gf_ref, t_ref, dx_ref, lp_ref, dgf_ref):
        first = pl.program_id(0) == 0
        xv = x_ref[...]
        gfv = gf_ref[...]
        r = lax.rsqrt(jnp.mean(xv * xv, axis=-1, keepdims=True) + EPS)
        xh = xv * r
        err = xh * gfv - t_ref[...]
        _accumulate(lp_ref, first, jnp.sum(err * err, axis=0, keepdims=True))
        dy = err * (1.0 / D)
        _accumulate(dgf_ref, first, jnp.sum(dy * xh, axis=0, keepdims=True))
        dxh = dy * gfv
        dx_ref[...] = r * (dxh - xh * jnp.mean(dxh * xh, axis=-1, keepdims=True))

    row = pl.BlockSpec((tm, D), lambda i: (i, 0))
    vec = pl.BlockSpec((1, D), lambda i: (0, 0))
    return pl.pallas_call(
        body, name="final_loss", grid=(T // tm,),
        in_specs=[row, vec, row], out_specs=[row, vec, vec],
        out_shape=[jax.ShapeDtypeStruct((T, D), F32), jax.ShapeDtypeStruct((1, D), F32),
                   jax.ShapeDtypeStruct((1, D), F32)],
        compiler_params=_params(1),
    )(x, gf, target)


def _merge_bwd(dxo, mo, gate, proj, ya, yc, w_out, w_ao, w_co, seq, comm=None):
    T, D = dxo.shape
    B = T // seq
    tm = _tile(seq, 512)
    nb = seq // tm

    def body(dxo_ref, mo_ref, gate_ref, ga_ref, gc_ref, ya_ref, yc_ref, wout_ref, wao_ref, wco_ref,
             dmo_ref, dya_ref, dyc_ref, dga_ref, dgc_ref, do_ref, dz_ref, dgate_ref):
        dxo_v = dxo_ref[...]
        dmo = (gate_ref[...] * dxo_v).astype(BF16)
        dmo_ref[...] = dmo
        _accumulate(dgate_ref, pl.program_id(0) % nb == 0,
                    jnp.sum(mo_ref[...].astype(F32) * dxo_v, axis=0, keepdims=True))
        dm = _dot_nt(dmo, wout_ref[...])
        sa = _sigmoid(ga_ref[...].astype(F32))
        sc = _sigmoid(gc_ref[...].astype(F32))
        dya = (sa * dm).astype(BF16)
        dyc = (sc * dm).astype(BF16)
        dya_ref[...] = dya
        dyc_ref[...] = dyc
        dga_ref[...] = (dm * ya_ref[...].astype(F32) * (sa * (1.0 - sa))).astype(BF16)
        dgc_ref[...] = (dm * yc_ref[...].astype(F32) * (sc * (1.0 - sc))).astype(BF16)
        do_ref[...] = _dot_nt(dya, wao_ref[...]).astype(BF16)
        dz_ref[...] = _dot_nt(dyc, wco_ref[...]).astype(BF16)

    row = pl.BlockSpec((tm, D), lambda i: (i, 0))
    mat = pl.BlockSpec((D, D), lambda i: (0, 0))
    per_b = pl.BlockSpec((None, 1, D), lambda i: (i // nb, 0, 0))
    act = jax.ShapeDtypeStruct((T, D), BF16)
    return _call(
        body, name="mix_merge_bwd", grid=(T // tm,),
        in_specs=[row, row, per_b, pl.BlockSpec((tm, D), lambda i: (i, COLB_GA)),
                  pl.BlockSpec((tm, D), lambda i: (i, COLB_GC)), row, row, mat, mat, mat],
        out_specs=[row] * 7 + [per_b],
        out_shape=[act] * 7 + [jax.ShapeDtypeStruct((B, 1, D), F32)],
        operands=(dxo, mo, gate, proj, proj, ya, yc, w_out, w_ao, w_co), comm=comm)


def _attn_bwd(proj, sinks, o, do, lse, batch, seq, comm=None):
    T = proj.shape[0]
    nblk = seq // BLOCK
    n_steps = batch * nblk

    def body(sink_ref, q_ref, kp_ref, ko_ref, vp_ref, vo_ref, o_ref, do_ref, lse_ref,
             dq_ref, dkp_ref, dko_ref, dvp_ref, dvo_ref, dsink_ref):
        lo = lax.broadcasted_iota(jnp.int32, (1, 2 * HEAD_DIM), 1) < HEAD_DIM
        sink_lane = lax.broadcasted_iota(jnp.int32, (1, 2 * HEAD_DIM), 1)
        valid = _band_valid(pl.program_id(1) > 0)
        k_ops = _band_operands(kp_ref, ko_ref, lo)
        v_ops = _band_operands(vp_ref, vo_ref, lo)
        dsink = jnp.zeros((1, 2 * HEAD_DIM), F32)
        col = jnp.zeros((BLOCK, 1), F32)

        def fold(both):
            return (jnp.where(lo, both[:BAND], 0.0)
                    + pltpu.roll(jnp.where(lo, 0.0, both[BAND:]), HEAD_DIM, 1))

        dk_heads, dv_heads = [], []
        for kvh in range(N_KV_HEADS):
            q4 = _stack_pairs(q_ref, kvh)
            do4 = _stack_pairs(do_ref, kvh)
            dd = do4.astype(F32) * _stack_pairs(o_ref, kvh).astype(F32)
            s_all = _dot_nt(q4, k_ops[kvh]) * ATTN_SCALE
            dp_all = _dot_nt(do4, v_ops[kvh])
            ds_sides, p_sides = [], []
            for side in range(2):
                heads = [2 * (kvh * PAIRS_PER_KV + pp) + side for pp in range(PAIRS_PER_KV)]
                mine = lo if side == 0 else jnp.logical_not(lo)
                cols = slice(side * BAND, (side + 1) * BAND)
                sink = _rows_per_pair([col + sink_ref[0, h] for h in heads])
                lse = _rows_per_pair([lse_ref[:, h:h + 1] for h in heads])
                delta = jnp.sum(jnp.where(mine, dd, 0.0), axis=-1, keepdims=True)
                p = jnp.where(valid, jnp.exp(jnp.where(valid, s_all[:, cols], MASK_VALUE) - lse), 0.0)
                ds_sides.append((p * (dp_all[:, cols] - delta) * ATTN_SCALE).astype(BF16))
                p_sides.append(p.astype(BF16))
                sink_part = jnp.exp(sink - lse) * delta
                for pp, h in enumerate(heads):
                    dsink = dsink + jnp.where(sink_lane == h, -jnp.sum(sink_part[pp * BLOCK:(pp + 1) * BLOCK]), 0.0)
            ds_all = jnp.concatenate(ds_sides, axis=1)
            dq4 = _dot_nn(ds_all, k_ops[kvh])
            for pp in range(PAIRS_PER_KV):
                dq_ref[:, _pair_lanes(kvh, pp)] = dq4[pp * BLOCK:(pp + 1) * BLOCK].astype(BF16)
            dk_heads.append(fold(_dot_tn(ds_all, q4)))
            dv_heads.append(fold(_dot_tn(jnp.concatenate(p_sides, axis=1), do4)))
        dk = dk_heads[0] + pltpu.roll(dk_heads[1], HEAD_DIM, 1)
        dv = dv_heads[0] + pltpu.roll(dv_heads[1], HEAD_DIM, 1)
        dkp_ref[...] = dk[:BLOCK]
        dko_ref[...] = dk[BLOCK:]
        dvp_ref[...] = dv[:BLOCK]
        dvo_ref[...] = dv[BLOCK:]
        dsink_ref[...] = dsink

    def own(b, n):
        return (b * nblk + n, 0)

    row = pl.BlockSpec((BLOCK, D_MODEL), own)
    kv = pl.BlockSpec((BLOCK, 2 * HEAD_DIM), own)
    kv_shape = jax.ShapeDtypeStruct((T, 2 * HEAD_DIM), F32)
    return _call(
        body, name="attn_bwd", grid=(batch, nblk),
        in_specs=[SMEM_SPEC] + _attn_specs(nblk) + [row, row, pl.BlockSpec((BLOCK, N_Q_HEADS), own)],
        out_specs=[row, kv, kv, kv, kv, pl.BlockSpec((None, 1, 2 * HEAD_DIM), lambda b, n: (b * nblk + n, 0, 0))],
        out_shape=[jax.ShapeDtypeStruct((T, D_MODEL), BF16), kv_shape, kv_shape, kv_shape, kv_shape,
                   jax.ShapeDtypeStruct((n_steps, 1, 2 * HEAD_DIM), F32)],
        operands=(sinks, proj, proj, proj, proj, proj, o, do, lse), comm=comm)


def _conv_bwd(proj, dz, ydw, w_dw, ln_g, ln_b, batch, seq, comm=None):
    T = proj.shape[0]
    ts = _tile(seq, 256)
    nt = seq // ts
    per_tile = ts // CONV_PAD
    shift = CONV_PAD - (CONV_WIDTH - 1)

    def body(ca_ref, cb_ref, cah_ref, cbh_ref, dz_ref, dzn_ref, y_ref, yn_ref, w_ref, g_ref, beta_ref,
             dca_ref, dcb_ref, dw_ref, db_ref, dg_ref, dbeta_ref, upad, dypad):
        t = pl.program_id(1)
        first = (pl.program_id(0) == 0) & (t == 0)
        gv = g_ref[...]

        def ln_bwd(dzv, yv):
            lnh, rstd = _layernorm_stats(yv)
            ln = lnh * gv + beta_ref[...]
            sg = _sigmoid(ln)
            dln = dzv.astype(F32) * (sg * (1.0 + ln * (1.0 - sg)))
            dyh = dln * gv
            dy = rstd * (dyh - jnp.mean(dyh, axis=-1, keepdims=True)
                         - lnh * jnp.mean(dyh * lnh, axis=-1, keepdims=True))
            return dy, dln, lnh

        dy, dln, lnh = ln_bwd(dz_ref[...], y_ref[...])
        dy_next, _, _ = ln_bwd(dzn_ref[...], yn_ref[...])
        dypad[0, 0:ts, :] = dy
        dypad[0, ts:, :] = jnp.where(t < nt - 1, dy_next, jnp.zeros_like(dy_next))
        _fill_shifted(dypad)
        _fill_upad(upad, ca_ref, cb_ref, cah_ref, cbh_ref, t)
        _fill_shifted(upad)

        _accumulate(dg_ref, first, jnp.sum(dln * lnh, axis=0, keepdims=True))
        _accumulate(dbeta_ref, first, jnp.sum(dln, axis=0, keepdims=True))
        _accumulate(db_ref, first, jnp.sum(dy, axis=0, keepdims=True))

        @pl.when(first)
        def _():
            dw_ref[...] = jnp.zeros_like(dw_ref)

        du = jnp.zeros((ts, D_MODEL), F32)
        for k in range(CONV_WIDTH):
            du = du + w_ref[k:k + 1, :] * _shifted_rows(dypad, CONV_WIDTH - 1 - k, ts)
            dw_ref[k:k + 1, :] += jnp.sum(dy * _shifted_rows(upad, shift + k, ts), axis=0, keepdims=True)
        cav = ca_ref[...].astype(F32)
        sb = _sigmoid(cb_ref[...].astype(F32))
        dca_ref[...] = (du * sb).astype(BF16)
        dcb_ref[...] = (du * cav * (sb * (1.0 - sb))).astype(BF16)

    def tile(b, t):
        return (b * nt + t, 0)

    def after(b, t):
        return (jnp.minimum((b * nt + t + 1) * per_tile, T // CONV_PAD - 1), 0)

    row = pl.BlockSpec((ts, D_MODEL), tile)
    halo = pl.BlockSpec((CONV_PAD, D_MODEL), after)
    vec = pl.BlockSpec((1, D_MODEL), lambda b, t: (0, 0))
    wspec = pl.BlockSpec((CONV_PAD, D_MODEL), lambda b, t: (0, 0))
    act = jax.ShapeDtypeStruct((T, D_MODEL), BF16)
    vec_shape = jax.ShapeDtypeStruct((1, D_MODEL), F32)
    return _call(
        body, name="conv_bwd", grid=(batch, nt),
        in_specs=_conv_specs(ts, nt) + [row, halo, row, halo, wspec, vec, vec],
        out_specs=[row, row, wspec, vec, vec, vec],
        out_shape=[act, act, jax.ShapeDtypeStruct((CONV_PAD, D_MODEL), F32), vec_shape, vec_shape, vec_shape],
        scratch_shapes=[pltpu.VMEM((SUBLANES, ts + CONV_PAD, D_MODEL), F32)] * 2,
        operands=(proj, proj, proj, proj, dz, dz, ydw, ydw, w_dw, ln_g, ln_b), comm=comm)


def _in_proj_bwd(dproj, w_in_g, x, gn, sc, dxo, seq, comm=None):
    T, D = x.shape
    J, W, _ = w_in_g.shape
    B = T // seq
    tm = _tile(seq, 512)
    nb = seq // tm

    def body(dp_ref, w_ref, x_ref, gn_ref, sc_ref, dxo_ref, dx_ref, dsc_ref, dsh_ref, dgn_ref, acc):
        i = pl.program_id(0)
        j = pl.program_id(1)

        @pl.when(j == 0)
        def _():
            acc[...] = jnp.zeros_like(acc)

        acc[...] += _dot_nn(dp_ref[...], w_ref[...])

        @pl.when(j == J - 1)
        def _():
            _norm_mod_bwd(acc[...], x_ref[...], gn_ref[...], sc_ref[...], dxo_ref[...],
                          i % nb == 0, i == 0, dx_ref, dsc_ref, dsh_ref, dgn_ref)

    row = pl.BlockSpec((tm, D), lambda i, j: (i, 0))
    vec = pl.BlockSpec((1, D), lambda i, j: (0, 0))
    per_b = pl.BlockSpec((None, 1, D), lambda i, j: (i // nb, 0, 0))
    per_b_shape = jax.ShapeDtypeStruct((B, 1, D), F32)
    return _call(
        body, name="mix_in_proj_bwd", grid=(T // tm, J),
        in_specs=[pl.BlockSpec((None, tm, W), lambda i, j: (j, i, 0)),
                  pl.BlockSpec((None, W, D), lambda i, j: (j, 0, 0)), row, vec, per_b, row],
        out_specs=[row, per_b, per_b, vec],
        out_shape=[jax.ShapeDtypeStruct((T, D), F32), per_b_shape, per_b_shape, jax.ShapeDtypeStruct((1, D), F32)],
        scratch_shapes=[pltpu.VMEM((tm, D), F32)],
        operands=(dproj, w_in_g, x, gn, sc, dxo), comm=comm)


def _ada_fwd(c_all, w_ada, b_cols):
    nbatch, D = c_all.shape
    N = w_ada.shape[1]
    tn = _tile(N, 768)

    def body(c_ref, w_ref, b_ref, o_ref):
        cv = c_ref[...]
        act = (cv * _sigmoid(cv)).astype(BF16)
        o_ref[...] = _dot_nn(act, w_ref[...].astype(BF16)) + b_ref[...]

    return pl.pallas_call(
        body, name="ada_fwd", grid=(N // tn,),
        in_specs=[pl.BlockSpec((nbatch, D), lambda j: (0, 0)), pl.BlockSpec((D, tn), lambda j: (0, j)),
                  pl.BlockSpec((1, tn), lambda j: (0, j))],
        out_specs=pl.BlockSpec((nbatch, tn), lambda j: (0, j)),
        out_shape=jax.ShapeDtypeStruct((nbatch, N), F32),
        compiler_params=_params(1),
    )(c_all, w_ada, b_cols)


def _adamw(w, g, m, v):
    m = ADAM_B1 * m + (1.0 - ADAM_B1) * g
    v = ADAM_B2 * v + (1.0 - ADAM_B2) * (g * g)
    m_hat = m / (1.0 - ADAM_B1 ** ADAM_STEP)
    v_hat = v / (1.0 - ADAM_B2 ** ADAM_STEP)
    delta = -ADAM_LR * (m_hat / (jnp.sqrt(v_hat) + ADAM_EPS) + ADAM_WD * w)
    return delta, m, v


def _adam_call(w, g, m, v, name, comm=None):
    R, C = w.shape
    tr = _row_tile(R, 512)

    def body(w_ref, g_ref, m_ref, v_ref, d_ref, mo_ref, vo_ref):
        d, mn, vn = _adamw(w_ref[...], g_ref[...], m_ref[...], v_ref[...])
        d_ref[...] = d
        mo_ref[...] = mn
        vo_ref[...] = vn

    blk = pl.BlockSpec((tr, C), lambda i: (i, 0))
    shape = jax.ShapeDtypeStruct((R, C), F32)
    return _call(body, name=name, grid=(R // tr,), in_specs=[blk] * 4, out_specs=[blk] * 3, out_shape=[shape] * 3,
                 operands=(w, g, m, v), comm=comm)


ADAM_GROUP_STEPS = 8


def _adam_group(ws, gs, ms, vs, name, comm=None):
    n = len(ws)

    def body(*refs):
        ins, outs = refs[:4 * n], refs[4 * n:]
        for i in range(n):
            d, mn, vn = _adamw(*(r[...] for r in ins[4 * i:4 * i + 4]))
            outs[3 * i][...] = d
            outs[3 * i + 1][...] = mn
            outs[3 * i + 2][...] = vn

    operands, in_specs, out_specs, out_shape = [], [], [], []
    for w, g, m, v in zip(ws, gs, ms, vs):
        R, C = w.shape
        blk = pl.BlockSpec((R // ADAM_GROUP_STEPS, C), lambda i: (i, 0))
        operands += [w, g, m, v]
        in_specs += [blk] * 4
        out_specs += [blk] * 3
        out_shape += [jax.ShapeDtypeStruct((R, C), F32)] * 3
    outs, comm_outs = _call(body, name=name, grid=(ADAM_GROUP_STEPS,), in_specs=in_specs, out_specs=out_specs,
                            out_shape=out_shape, operands=operands, comm=comm)
    return [tuple(outs[3 * i:3 * i + 3]) for i in range(n)], comm_outs


def _ada_adam(c_act_t, dmod_cols, w, m, v, comm):
    R, C = w.shape
    nbatch = c_act_t.shape[1]
    tr = _tile(R, 128)

    def body(ct_ref, dm_ref, w_ref, m_ref, v_ref, g_ref, d_ref, mo_ref, vo_ref):
        cv = ct_ref[...]
        g = _dot_nn((cv * _sigmoid(cv)).astype(BF16), dm_ref[...].astype(BF16))
        g_ref[...] = g
        d, mn, vn = _adamw(w_ref[...], g, m_ref[...], v_ref[...])
        d_ref[...] = d
        mo_ref[...] = mn
        vo_ref[...] = vn

    blk = pl.BlockSpec((tr, C), lambda i: (i, 0))
    shape = jax.ShapeDtypeStruct((R, C), F32)
    return _call(
        body, name="ada_adam", grid=(R // tr,),
        in_specs=[pl.BlockSpec((tr, nbatch), lambda i: (i, 0)), pl.BlockSpec((nbatch, C), lambda i: (0, 0)),
                  blk, blk, blk],
        out_specs=[blk] * 4, out_shape=[shape] * 4,
        operands=(c_act_t, dmod_cols, w, m, v), comm=comm)


def _small_adam(gathered, w, m, v, rows_b0, rows_b1, rows_vec):
    _, P, D = gathered.shape
    R = w.shape[0]

    def body(ga_ref, w_ref, m_ref, v_ref, sum_ref, g_ref, d_ref, mo_ref, vo_ref):
        total = ga_ref[0]
        for dev in range(1, N_DEV):
            total = total + ga_ref[dev]
        sum_ref[...] = total
        g_ref[...] = jnp.zeros_like(g_ref)
        g_ref[0:N_MOD, :] = (sum_ref[rows_b0:rows_b0 + N_MOD, :] + sum_ref[rows_b1:rows_b1 + N_MOD, :])
        g_ref[N_MOD:N_MOD + 8, :] = sum_ref[rows_vec:rows_vec + 8, :]
        d, mn, vn = _adamw(w_ref[...], g_ref[...], m_ref[...], v_ref[...])
        d_ref[...] = d
        mo_ref[...] = mn
        vo_ref[...] = vn

    shape = jax.ShapeDtypeStruct((R, D), F32)
    return pl.pallas_call(
        body, name="small_adam",
        in_specs=[VMEM_SPEC] * 4, out_specs=[VMEM_SPEC] * 5,
        out_shape=[jax.ShapeDtypeStruct((P, D), F32), shape, shape, shape, shape],
        compiler_params=pltpu.CompilerParams(vmem_limit_bytes=VMEM_LIMIT),
    )(gathered, w, m, v)


def _gather8(v, name):
    A, W = v.shape
    flips = [(fx, fy, fc) for fx in (0, 1) for fy in (0, 1) for fc in (0, 1) if (fx, fy, fc) != (0, 0, 0)]

    def body(v_ref, out_ref, send_sems, recv_sems, local_sem):
        x, y, c = _position()
        me = 4 * x + 2 * y + c
        mine = pltpu.make_async_copy(v_ref, out_ref.at[me], local_sem)
        mine.start()

        def copy(k, block, to):
            return pltpu.make_async_remote_copy(src_ref=v_ref, dst_ref=out_ref.at[block], send_sem=send_sems.at[k],
                                                recv_sem=recv_sems.at[k], device_id=to, device_id_type=MESH)

        peers = [(_flip(x, fx), _flip(y, fy), _flip(c, fc)) for fx, fy, fc in flips]
        sends = [copy(k, me, peer) for k, peer in enumerate(peers)]
        for cp in sends:
            cp.start()
        for k, (px, py, pc) in enumerate(peers):
            copy(k, 4 * px + 2 * py + pc, (px, py, pc)).wait_recv()
        for cp in sends:
            cp.wait_send()
        mine.wait()

    return pl.pallas_call(
        body, name=name, in_specs=[VMEM_SPEC], out_specs=VMEM_SPEC,
        out_shape=jax.ShapeDtypeStruct((N_DEV, A, W), v.dtype),
        scratch_shapes=[pltpu.SemaphoreType.DMA((N_DEV - 1,)), pltpu.SemaphoreType.DMA((N_DEV - 1,)),
                        pltpu.SemaphoreType.DMA],
    )(v)


def _mod_exchange(part):
    _, A, W = part.shape

    def body(p_ref, out_ref, send_sems, recv_sems, local_sem):
        x, y, c = _position()
        me = 4 * x + 2 * y + c
        chip = 2 * x + y
        mine = pltpu.make_async_copy(p_ref.at[me], out_ref.at[chip], local_sem)
        mine.start()
        peers = [(_flip(x, fx), _flip(y, fy)) for fx, fy in CHIP_FLIPS]
        sends = []
        for k, (px, py) in enumerate(peers):
            sends.append(pltpu.make_async_remote_copy(
                src_ref=p_ref.at[4 * px + 2 * py + c], dst_ref=out_ref.at[chip], send_sem=send_sems.at[k],
                recv_sem=recv_sems.at[k], device_id=(px, py, c), device_id_type=MESH))
        for cp in sends:
            cp.start()
        for k, (px, py) in enumerate(peers):
            pltpu.make_async_remote_copy(
                src_ref=p_ref.at[me], dst_ref=out_ref.at[2 * px + py], send_sem=send_sems.at[k],
                recv_sem=recv_sems.at[k], device_id=(px, py, c), device_id_type=MESH).wait_recv()
        for cp in sends:
            cp.wait_send()
        mine.wait()

    return pl.pallas_call(
        body, name="mod_exchange", in_specs=[VMEM_SPEC], out_specs=VMEM_SPEC,
        out_shape=jax.ShapeDtypeStruct((N_CHIP, A, W), part.dtype),
        scratch_shapes=[pltpu.SemaphoreType.DMA((3,)), pltpu.SemaphoreType.DMA((3,)), pltpu.SemaphoreType.DMA],
    )(part)


def _cast_slot(w, chip_idx, name):
    R, C = w.shape
    tr = _row_tile(R, 512)

    def body(chip_ref, w_ref, o_ref):
        o_ref[...] = w_ref[...].astype(BF16)

    return pl.pallas_call(
        body, name=name,
        grid_spec=pltpu.PrefetchScalarGridSpec(
            num_scalar_prefetch=1, grid=(R // tr,),
            in_specs=[pl.BlockSpec((tr, C), lambda i, chip_ref: (i, 0))],
            out_specs=pl.BlockSpec((None, tr, C), lambda i, chip_ref: (chip_ref[0], i, 0))),
        out_shape=jax.ShapeDtypeStruct((N_CHIP, R, C), BF16),
        compiler_params=_params(1),
    )(chip_idx, w)


def _pair_sum(g32, recv, core, name):
    _, J, r, C = g32.shape

    def body(core_ref, g_ref, r_ref, o_ref):
        o_ref[...] = (g_ref[...] + r_ref[...].astype(F32)).astype(BF16)

    return pl.pallas_call(
        body, name=name,
        grid_spec=pltpu.PrefetchScalarGridSpec(
            num_scalar_prefetch=1, grid=(J,),
            in_specs=[pl.BlockSpec((None, None, r, C), lambda j, core_ref: (core_ref[0], j, 0, 0)),
                      pl.BlockSpec((None, r, C), lambda j, core_ref: (j, 0, 0))],
            out_specs=pl.BlockSpec((None, r, C), lambda j, core_ref: (j, 0, 0))),
        out_shape=jax.ShapeDtypeStruct((J, r, C), BF16),
        compiler_params=_params(1),
    )(core, g32, recv)


def _chip_sum(g32, recv_sib, recv_chips, core_chip, name):
    _, J, r, C = g32.shape

    def body(idx_ref, g_ref, s_ref, o_ref_in, o_ref):
        total = g_ref[...] + s_ref[...].astype(F32)
        for k in range(3):
            total = total + o_ref_in[k].astype(F32)
        o_ref[...] = total

    return pl.pallas_call(
        body, name=name,
        grid_spec=pltpu.PrefetchScalarGridSpec(
            num_scalar_prefetch=1, grid=(1,),
            in_specs=[pl.BlockSpec((None, None, r, C), lambda i, idx: (idx[0], idx[1], 0, 0)),
                      pl.BlockSpec((None, r, C), lambda i, idx: (idx[1], 0, 0)),
                      pl.BlockSpec((3, r, C), lambda i, idx: (0, 0, 0))],
            out_specs=pl.BlockSpec((None, r, C), lambda i, idx: (idx[0], 0, 0))),
        out_shape=jax.ShapeDtypeStruct((2, r, C), F32),
        compiler_params=_params(1),
    )(core_chip, g32, recv_sib, recv_chips)


ICI_US_PER_ELEMENT = 4.6e-5


class _Reducer:
    def __init__(self, core_idx, core_chip):
        self.core_idx, self.core_chip = core_idx, core_chip
        self.grads, self.halves, self.reduced = {}, {}, {}
        self.ready_swap, self.ready_exchange, self.ready_join = [], [], []
        self.inflight, self.current = ([], [], [], None), None
        self.flushes = 0
        self.extra, self.extra_out = None, None

    def add(self, name, grad_pair):
        self.grads[name] = grad_pair
        self.ready_swap.append(name)

    def comm(self, budget_us):
        swaps, self.ready_swap = self.ready_swap, []
        joins, self.ready_join = self.ready_join, []
        exchanges, waiting = [], []
        for item in self.ready_exchange:
            cost = ICI_US_PER_ELEMENT * 2 * item[2].shape[1] * item[2].shape[2]
            if cost <= budget_us:
                exchanges.append(item)
                budget_us -= cost
            else:
                waiting.append(item)
        self.ready_exchange = waiting
        parts = []
        if swaps:
            parts.append(_SwapComm([self.grads[n][1] for n in swaps]))
        if exchanges:
            parts.append(_ExchangeComm([pair for _, _, pair in exchanges]))
        if joins:
            parts.append(_JoinComm([self.halves[n] for n in joins]))
        extra, self.extra = self.extra, None
        if extra is not None:
            parts.append(extra)
        self.inflight = (swaps, exchanges, joins, extra)
        self.current = _CommList(parts) if parts else None
        return self.current

    def done(self, comm_outs):
        if self.current is None:
            return
        swaps, exchanges, joins, extra = self.inflight
        outs = iter(self.current.split_outputs(list(comm_outs)))
        if swaps:
            for n, recv in zip(swaps, next(outs)):
                pair = _pair_sum(self.grads[n][0], recv, self.core_idx, "pair_sum_" + n)
                self.ready_exchange.append((n, recv, pair))
        if exchanges:
            for (n, recv, _), chips in zip(exchanges, next(outs)):
                self.halves[n] = _chip_sum(self.grads[n][0], recv, chips, self.core_chip, "chip_sum_" + n)
                self.ready_join.append(n)
        if joins:
            self.reduced.update(zip(joins, next(outs)))
        if extra is not None:
            self.extra_out = next(outs)
        self.current = None

    def run(self, kernel, budget_us, *args, **kwargs):
        if budget_us is None:
            return kernel(*args, comm=None, **kwargs)[0]
        outs, comm_outs = kernel(*args, comm=self.comm(budget_us), **kwargs)
        self.done(comm_outs)
        return outs

    def step(self):
        comm = self.comm(float("inf"))
        self.flushes += 1
        self.done(_run_comm(comm, "grad_reduce_tail_%d" % self.flushes))


BIG_WEIGHTS = ("ffn1_w_gate", "ffn1_w_up", "ffn1_w_down", "w_in", "w_attn_o", "w_conv_o", "w_out",
               "ffn2_w_gate", "ffn2_w_up", "ffn2_w_down")
VECTORS = ("norm_ffn1_g", "norm_mix_g", "conv_b_dw", "conv_ln_g", "conv_ln_b", "norm_ffn2_g", "final_norm_g")
ROW_DMOD0, ROW_DMOD1, ROW_VEC, ROW_SINK, ROW_CONVW, SMALL_ROWS = 0, 16, 33, 40, 41, 72


FFN1_WEIGHTS = ("ffn1_w_gate", "ffn1_w_up", "ffn1_w_down")
FFN2_WEIGHTS = ("ffn2_w_gate", "ffn2_w_up", "ffn2_w_down")
MIX_WEIGHTS = ("w_in", "w_attn_o", "w_conv_o", "w_out")
COL_SHARDED = ("ffn1_w_gate", "ffn1_w_up", "ffn2_w_gate", "ffn2_w_up", "w_in")


def _local_grads(x, target, mod, slots, small, seq, core_idx, core_chip):
    T, D = x.shape
    B = T // seq
    mods = [mod[:, k][:, None, :] for k in range(N_MOD)]
    sh1, sc1, g1, sh2, sc2, g2, sh3, sc3, g3 = mods
    w = dict(zip(FFN1_WEIGHTS, _run_comm(_GatherComm([slots[n] for n in FFN1_WEIGHTS]), "gather_ffn1")))

    (h1, a1, u1, f1, x1), outs = _ffn_fwd(
        x, small["norm_ffn1_g"], sc1, sh1, g1, w["ffn1_w_gate"], w["ffn1_w_up"], w["ffn1_w_down"], seq, "ffn1_fwd",
        comm=_GatherComm([slots[n] for n in MIX_WEIGHTS]))
    w["w_in"] = outs[0]
    w_ao, w_co, w_o = [t.reshape(D, D) for t in outs[1:]]
    w_in_full = w["w_in"].reshape(IN_WIDTH, D)
    q_end, v_end = D, D + 4 * HEAD_DIM
    w_in_cols = jnp.concatenate([w_in_full[:q_end], w_in_full[v_end:], w_in_full[q_end:v_end]], axis=0)
    (h2, proj), _ = _in_proj(x1, small["norm_mix_g"], sc2, sh2, w_in_cols, seq)
    (o, lse), (w["ffn2_w_gate"], w["ffn2_w_up"]) = _attn_fwd(
        proj, small["attn_sinks"], B, seq, comm=_GatherComm([slots["ffn2_w_gate"], slots["ffn2_w_up"]]))
    (ydw, z), (w["ffn2_w_down"],) = _conv_fwd(
        proj, small["conv_w_dw"], small["conv_b_dw"], small["conv_ln_g"], small["conv_ln_b"], B, seq,
        comm=_GatherComm([slots["ffn2_w_down"]]))
    ya, yc, merged, mo, x2 = _merge(o, z, proj, w_ao, w_co, w_o, x1, g2, seq)
    (h3, a3, u3, f3, x3), _ = _ffn_fwd(x2, small["norm_ffn2_g"], sc3, sh3, g3, w["ffn2_w_gate"], w["ffn2_w_up"],
                                       w["ffn2_w_down"], seq, "ffn2_fwd")
    dx3, loss_parts, d_final_g = _final_loss(x3, small["final_norm_g"], target)

    red = _Reducer(core_idx, core_chip)

    def weight_grad(name, budget_us, a, a_spec, b, b_spec, rows, cols):
        red.add(name, red.run(_wgrad, budget_us, a, a_spec, b, b_spec, rows, cols, T, "dw_" + name))

    def ffn_backward(prefix, dw_budget_us, dxo, xin, h, a, u, f, gn, sc, gate, before_weight_grads=None):
        da, du, s, df, dx, dgate, dsc, dsh, dgn = red.run(
            _ffn_bwd, 170, dxo, xin, f, a, u, gn, sc, gate, w[prefix + "_w_gate"], w[prefix + "_w_up"],
            w[prefix + "_w_down"], seq, prefix + "_bwd")
        if before_weight_grads is not None:
            before_weight_grads(dgate, dsc, dsh, dgn)
        weight_grad(prefix + "_w_down", dw_budget_us, s, _spec_chip_major(FF_SHARD), df, _spec_rows(D), FF_SHARD, D)
        weight_grad(prefix + "_w_gate", dw_budget_us, da, _spec_chip_major(FF_SHARD), h, _spec_rows(D), FF_SHARD, D)
        weight_grad(prefix + "_w_up", dw_budget_us, du, _spec_chip_major(FF_SHARD), h, _spec_rows(D), FF_SHARD, D)
        return dx, dgate, dsc, dsh, dgn

    dx2, dg3, dsc3, dsh3, d_gn3 = ffn_backward("ffn2", None, dx3, x2, h3, a3, u3, f3, small["norm_ffn2_g"], sc3, g3)

    dmo, dya, dyc, dga, dgc, do, dz, dg2 = red.run(_merge_bwd, 45, dx2, mo, g2, proj, ya, yc, w_o, w_ao, w_co, seq)
    shard = D // N_CHIP
    weight_grad("w_out", None, merged, _spec_col_block(shard), dmo, _spec_rows(D), shard, D)
    weight_grad("w_attn_o", None, o, _spec_col_block(shard), dya, _spec_rows(D), shard, D)
    weight_grad("w_conv_o", None, z, _spec_col_block(shard), dyc, _spec_rows(D), shard, D)
    dq, dkp, dko, dvp, dvo, dsink_steps = red.run(_attn_bwd, 100, proj, small["attn_sinks"], o, do, lse, B, seq)
    dca, dcb, d_conv_w, d_conv_b, d_ln_g, d_ln_b = red.run(
        _conv_bwd, 165, proj, dz, ydw, small["conv_w_dw"], small["conv_ln_g"], small["conv_ln_b"], B, seq)

    def band_sum(own, prev):
        prev = prev.reshape(B, seq // BLOCK, BLOCK, 2 * HEAD_DIM)
        moved = jnp.concatenate([prev[:, 1:], jnp.zeros_like(prev[:, :1])], axis=1)
        return (own + moved.reshape(T, 2 * HEAD_DIM)).astype(BF16)

    dproj = jnp.concatenate([dq, band_sum(dko, dkp), band_sum(dvo, dvp), dca, dcb, dga, dgc], axis=1)
    dproj = dproj.reshape(T, N_CHIP, IN_SHARD).transpose(1, 0, 2)
    weight_grad("w_in", 60, dproj, _spec_chip_major(IN_SHARD), h2, _spec_rows(D), IN_SHARD, D)
    dx1, dsc2, dsh2, d_gn2 = red.run(_in_proj_bwd, 90, dproj, w["w_in"], x1, small["norm_mix_g"], sc2, dx2, seq)

    def gather_small_grads(dg1, dsc1, dsh1, d_gn1):
        dmod = jnp.concatenate([dsh1, dsc1, dg1, dsh2, dsc2, dg2, dsh3, dsc3, dg3], axis=1)
        d_sinks = jnp.sum(dsink_steps, axis=0)
        vec_grads = {"norm_ffn1_g": d_gn1, "norm_mix_g": d_gn2, "conv_b_dw": d_conv_b, "conv_ln_g": d_ln_g,
                     "conv_ln_b": d_ln_b, "norm_ffn2_g": d_gn3, "final_norm_g": d_final_g}
        block = jnp.zeros((SMALL_ROWS, D), F32)
        block = block.at[ROW_DMOD0:ROW_DMOD0 + N_MOD].set(dmod[0]).at[ROW_DMOD1:ROW_DMOD1 + N_MOD].set(dmod[1])
        block = block.at[ROW_VEC:ROW_VEC + len(VECTORS)].set(jnp.concatenate([vec_grads[n] for n in VECTORS], axis=0))
        block = block.at[ROW_SINK, :2 * HEAD_DIM].set(d_sinks[0])
        block = block.at[ROW_CONVW:ROW_CONVW + CONV_WIDTH].set(d_conv_w[:CONV_WIDTH])
        red.extra = _Gather8Comm(block)

    dx0, _, _, _, _ = ffn_backward("ffn1", 38, dx1, x, h1, a1, u1, f1, small["norm_ffn1_g"], sc1, g1,
                                   before_weight_grads=gather_small_grads)
    return loss_parts, dx0, red, red.extra_out[0]


def kernel(x, c, w_ada, b_ada, norm_ffn1_g, ffn1_w_gate, ffn1_w_up, ffn1_w_down, norm_mix_g, w_in, attn_sinks, w_attn_o, conv_w_dw, conv_b_dw, conv_ln_g, conv_ln_b, w_conv_o, w_out, norm_ffn2_g, ffn2_w_gate, ffn2_w_up, ffn2_w_down, final_norm_g, loss_target, m_w_ada, m_b_ada, m_norm_ffn1_g, m_ffn1_w_gate, m_ffn1_w_up, m_ffn1_w_down, m_norm_mix_g, m_w_in, m_attn_sinks, m_w_attn_o, m_conv_w_dw, m_conv_b_dw, m_conv_ln_g, m_conv_ln_b, m_w_conv_o, m_w_out, m_norm_ffn2_g, m_ffn2_w_gate, m_ffn2_w_up, m_ffn2_w_down, m_final_norm_g, v_w_ada, v_b_ada, v_norm_ffn1_g, v_ffn1_w_gate, v_ffn1_w_up, v_ffn1_w_down, v_norm_mix_g, v_w_in, v_attn_sinks, v_w_attn_o, v_conv_w_dw, v_conv_b_dw, v_conv_ln_g, v_conv_ln_b, v_w_conv_o, v_w_out, v_norm_ffn2_g, v_ffn2_w_gate, v_ffn2_w_up, v_ffn2_w_down, v_final_norm_g):
    args = dict(locals())
    B, seq, D = x.shape
    T = B * seq
    xi, yi, ci = _position()
    chip = 2 * xi + yi
    dev = 4 * xi + 2 * yi + ci

    def shard_2d(prefix, name):
        t = args[prefix + name][0]
        return t.T if name in COL_SHARDED else t

    big = {n: shard_2d("", n) for n in BIG_WEIGHTS}
    final_g = final_norm_g[None, :]
    vec_w = {n: (args[n] if n != "final_norm_g" else final_g) for n in VECTORS}

    conv_cols = D // N_CHIP
    conv_flat = jnp.pad(conv_w_dw[0].reshape(-1), (0, 8 * D - CONV_WIDTH * conv_cols)).reshape(8, D)
    first = _gather8(jnp.concatenate([jnp.pad(c, ((0, 8 - B), (0, 0))), conv_flat], axis=0), "gather_c")
    c_all = first[:, :B].reshape(N_DEV * B, D)
    conv_taps = first[::2, 8:].reshape(N_CHIP, 8 * D)[:, :CONV_WIDTH * conv_cols]
    conv_taps = conv_taps.reshape(N_CHIP, CONV_WIDTH, conv_cols).transpose(1, 0, 2).reshape(CONV_WIDTH, D)
    conv_taps = jnp.pad(conv_taps, ((0, CONV_PAD - CONV_WIDTH), (0, 0)))

    ada_cols = w_ada.shape[2]
    b_cols = lax.dynamic_slice(b_ada, (0, chip * ada_cols), (1, ada_cols))
    mod_part = _ada_fwd(c_all, w_ada[0], b_cols).reshape(N_DEV, B, ada_cols)
    mod = _mod_exchange(mod_part).transpose(1, 0, 2).reshape(B, N_MOD, D)

    core_idx = jnp.reshape(ci, (1,)).astype(jnp.int32)
    chip_idx = jnp.reshape(chip, (1,)).astype(jnp.int32)
    core_chip = jnp.stack([ci, chip]).astype(jnp.int32)
    slots = {n: _cast_slot(big[n], chip_idx, "cast_" + n) for n in BIG_WEIGHTS}

    small = dict(vec_w)
    small["attn_sinks"] = attn_sinks
    small["conv_w_dw"] = conv_taps

    loss_parts, dx, red, small_all = _local_grads(
        x.reshape(T, D), loss_target.reshape(T, D), mod, slots, small, seq, core_idx, core_chip)

    loss = lax.psum((0.5 / D) * jnp.sum(loss_parts), ("x", "y", "c"))
    grad_x = dx.reshape(B, seq, D)
    out = {}


    def pack_small(prefix):
        rows = [args[prefix + "b_ada"].reshape(N_MOD, D)]
        rows += [args[prefix + n].reshape(1, D) for n in VECTORS]
        rows += [jnp.pad(args[prefix + "attn_sinks"], ((0, 0), (0, D - N_Q_HEADS)))]
        return jnp.pad(jnp.concatenate(rows, axis=0), ((0, 24 - N_MOD - len(VECTORS) - 1), (0, 0)))

    small_sum, sg, sd, sm, sv = _small_adam(small_all, pack_small(""), pack_small("m_"), pack_small("v_"),
                                           ROW_DMOD0, ROW_DMOD1, ROW_VEC)

    def unpack_small(t):
        res = {"b_ada": t[:N_MOD].reshape(1, N_MOD * D)}
        for k, n in enumerate(VECTORS):
            res[n] = t[N_MOD + k].reshape(args[n].shape)
        res["attn_sinks"] = t[N_MOD + len(VECTORS), :N_Q_HEADS].reshape(1, N_Q_HEADS)
        return res

    unpacked = [unpack_small(t) for t in (sg, sd, sm, sv)]
    for n in ("b_ada", "attn_sinks") + VECTORS:
        out[n] = tuple(u[n] for u in unpacked)

    conv_g = lax.dynamic_slice(small_sum, (ROW_CONVW, chip * conv_cols), (CONV_WIDTH, conv_cols))
    d, mn, vn = red.run(_adam_call, None, conv_w_dw[0], conv_g, m_conv_w_dw[0], v_conv_w_dw[0], "adam_conv_w_dw")
    out["conv_w_dw"] = tuple(t[None] for t in (conv_g, d, mn, vn))

    dmod_rows = jnp.stack([small_all[:, ROW_DMOD0:ROW_DMOD0 + N_MOD], small_all[:, ROW_DMOD1:ROW_DMOD1 + N_MOD]], axis=1)
    dmod_all = dmod_rows.reshape(N_DEV * B, N_MOD * D)
    dmod_cols = lax.dynamic_slice(dmod_all, (0, chip * ada_cols), (N_DEV * B, ada_cols))
    ada_out = red.run(_ada_adam, 35, c_all.T, dmod_cols, w_ada[0], m_w_ada[0], v_w_ada[0])
    out["w_ada"] = tuple(t[None] for t in ada_out)

    def finished(n):
        while n not in red.reduced:
            red.step()
        return red.reduced[n].reshape(big[n].shape)

    def emit(n, g, d, mn, vn):
        out[n] = tuple((t.T if n in COL_SHARDED else t)[None] for t in (g, d, mn, vn))

    early = FFN2_WEIGHTS + MIX_WEIGHTS
    early_g = [finished(n) for n in early]
    early_out = red.run(_adam_group, 45, [big[n] for n in early], early_g, [shard_2d("m_", n) for n in early],
                        [shard_2d("v_", n) for n in early], "adam_early")
    for n, g, (d, mn, vn) in zip(early, early_g, early_out):
        emit(n, g, d, mn, vn)
    for n in ("ffn1_w_down", "ffn1_w_gate", "ffn1_w_up"):
        g = finished(n)
        emit(n, g, *red.run(_adam_call, None, big[n], g, shard_2d("m_", n), shard_2d("v_", n), "adam_" + n))

    order = ("w_ada", "b_ada", "norm_ffn1_g", "ffn1_w_gate", "ffn1_w_up", "ffn1_w_down", "norm_mix_g", "w_in",
             "attn_sinks", "w_attn_o", "conv_w_dw", "conv_b_dw", "conv_ln_g", "conv_ln_b", "w_conv_o", "w_out",
             "norm_ffn2_g", "ffn2_w_gate", "ffn2_w_up", "ffn2_w_down", "final_norm_g")
    return (loss, grad_x, *[out[n][0] for n in order], *[out[n][1] for n in order],
            *[out[n][2] for n in order], *[out[n][3] for n in order])
```

```python
import functools

import jax
import jax.numpy as jnp
from jax import lax
from jax.experimental import pallas as pl
from jax.experimental.pallas import tpu as pltpu

F32 = jnp.float32
BF16 = jnp.bfloat16

D_MODEL = 1024
D_FF = 2816
N_CHIP = 4
N_DEV = 8
FF_SHARD = D_FF // N_CHIP
IN_WIDTH = 5376
IN_SHARD = IN_WIDTH // N_CHIP
HEAD_DIM = 64
N_Q_HEADS = 16
N_KV_HEADS = 2
BLOCK = 128
CONV_WIDTH = 31
CONV_PAD = 32
N_MOD = 9
EPS = 1e-6
FFN_RESIDUAL = 0.5
ATTN_SCALE = HEAD_DIM ** -0.5
MASK_VALUE = -1e30

ADAM_LR = 0.001
ADAM_B1 = 0.9
ADAM_B2 = 0.999
ADAM_EPS = 1e-08
ADAM_WD = 0.01
ADAM_STEP = 10

COLB_Q, COLB_CA, COLB_CB, COLB_GA, COLB_GC = 0, 1, 2, 3, 4
COLB_K, COLB_V = 40, 41
PROJ_TILE = 768

VMEM_LIMIT = 56 * 1024 * 1024
MESH = pl.DeviceIdType.MESH
ANY = pl.BlockSpec(memory_space=pl.ANY)
VMEM_SPEC = pl.BlockSpec(memory_space=pltpu.VMEM)
SMEM_SPEC = pl.BlockSpec(memory_space=pltpu.SMEM)


def _params(n_grid):
    return pltpu.CompilerParams(dimension_semantics=("arbitrary",) * n_grid, vmem_limit_bytes=VMEM_LIMIT)


def _tile(n, pref):
    t = min(n, pref)
    while n % t:
        t //= 2
    return t


def _row_tile(rows, cap):
    for t in range(min(rows, cap) // 16 * 16, 0, -16):
        if rows % t == 0:
            return t
    return rows


def _sigmoid(v):
    return 1.0 / (1.0 + jnp.exp(-v))


def _dot_nn(a, b):
    return lax.dot_general(a, b, (((1,), (0,)), ((), ())), preferred_element_type=F32)


def _dot_nt(a, b):
    return lax.dot_general(a, b, (((1,), (1,)), ((), ())), preferred_element_type=F32)


def _dot_tn(a, b):
    return lax.dot_general(a, b, (((0,), (0,)), ((), ())), preferred_element_type=F32)


ROW_CHUNK = 16


def _for_row_chunks(n_rows, fn):
    for r in range(0, n_rows, ROW_CHUNK):
        fn(slice(r, r + ROW_CHUNK))


def _norm_mod(xv, gn, sc, sh):
    r = lax.rsqrt(jnp.mean(xv * xv, axis=-1, keepdims=True) + EPS)
    return ((xv * r) * gn) * (1.0 + sc) + sh


def _accumulate(ref, first, value):
    @pl.when(first)
    def _():
        ref[...] = value

    @pl.when(jnp.logical_not(first))
    def _():
        ref[...] += value


def _norm_mod_bwd(dh, xv, gn, sc, dxo, first_of_batch, first, dx_ref, dsc_ref, dsh_ref, dgn_ref):
    r = lax.rsqrt(jnp.mean(xv * xv, axis=-1, keepdims=True) + EPS)
    xh = xv * r
    _accumulate(dsh_ref, first_of_batch, jnp.sum(dh, axis=0, keepdims=True))
    _accumulate(dsc_ref, first_of_batch, jnp.sum(dh * (xh * gn), axis=0, keepdims=True))
    dn = dh * (1.0 + sc)
    _accumulate(dgn_ref, first, jnp.sum(dn * xh, axis=0, keepdims=True))
    dxh = dn * gn
    dx_ref[...] = dxo + r * (dxh - xh * jnp.mean(dxh * xh, axis=-1, keepdims=True))


CHIP_FLIPS = ((1, 0), (0, 1), (1, 1))


def _position():
    return lax.axis_index("x"), lax.axis_index("y"), lax.axis_index("c")


def _flip(v, f):
    return 1 - v if f else v


class _GatherComm:
    def __init__(self, bufs):
        n = len(bufs)
        self.n = n
        self.operands = list(bufs)
        self.out_shape = [jax.ShapeDtypeStruct(b.shape, b.dtype) for b in bufs]
        self.aliases = {i: i for i in range(n)}
        self.sems = [pltpu.SemaphoreType.DMA((6 * n,)), pltpu.SemaphoreType.DMA((6 * n,))]
        self.rows = [b.shape[1] // 2 for b in bufs]

    def _half(self, ref, i, which):
        return ref.at[pl.ds(which * self.rows[i], self.rows[i]), :]

    def _ici(self, cins, couts, sems, i, k, dst_chip, to):
        x, y, c = _position()
        return pltpu.make_async_remote_copy(
            src_ref=self._half(cins[i].at[2 * x + y], i, c), dst_ref=self._half(couts[i].at[dst_chip], i, c),
            send_sem=sems[0].at[3 * i + k], recv_sem=sems[1].at[3 * i + k], device_id=to, device_id_type=MESH)

    def _d2d(self, couts, sems, i, k, src_chip, which):
        x, y, c = _position()
        place = self._half(couts[i].at[src_chip], i, which)
        return pltpu.make_async_remote_copy(
            src_ref=place, dst_ref=place, send_sem=sems[0].at[3 * self.n + 3 * i + k],
            recv_sem=sems[1].at[3 * self.n + 3 * i + k], device_id=(x, y, 1 - c), device_id_type=MESH)

    def _peers(self):
        x, y, _ = _position()
        return [(_flip(x, fx), _flip(y, fy)) for fx, fy in CHIP_FLIPS]

    def start(self, cins, couts, sems):
        x, y, c = _position()
        for i in range(self.n):
            for k, (px, py) in enumerate(self._peers()):
                self._ici(cins, couts, sems, i, k, 2 * x + y, (px, py, c)).start()

    def finish(self, cins, couts, sems):
        _, _, c = _position()
        peers = self._peers()
        for i in range(self.n):
            for k, (px, py) in enumerate(peers):
                self._ici(cins, couts, sems, i, k, 2 * px + py, (px, py, c)).wait_recv()
                self._d2d(couts, sems, i, k, 2 * px + py, c).start()
        for i in range(self.n):
            for k, (px, py) in enumerate(peers):
                self._d2d(couts, sems, i, k, 2 * px + py, 1 - c).wait_recv()
        for i in range(self.n):
            for k, (px, py) in enumerate(peers):
                self._ici(cins, couts, sems, i, k, 2 * px + py, (px, py, c)).wait_send()
                self._d2d(couts, sems, i, k, 2 * px + py, c).wait_send()


class _ExchangeComm:
    def __init__(self, pairs):
        n = len(pairs)
        self.n = n
        self.operands = list(pairs)
        self.out_shape = [jax.ShapeDtypeStruct((3,) + p.shape[1:], p.dtype) for p in pairs]
        self.aliases = {}
        self.sems = [pltpu.SemaphoreType.DMA((3 * n,)), pltpu.SemaphoreType.DMA((3 * n,))]

    def _copies(self, cins, couts, sems):
        x, y, c = _position()
        peers = [(_flip(x, fx), _flip(y, fy)) for fx, fy in CHIP_FLIPS]
        return [pltpu.make_async_remote_copy(
            src_ref=cins[i].at[2 * px + py], dst_ref=couts[i].at[k], send_sem=sems[0].at[3 * i + k],
            recv_sem=sems[1].at[3 * i + k], device_id=(px, py, c), device_id_type=MESH)
            for i in range(self.n) for k, (px, py) in enumerate(peers)]

    def start(self, cins, couts, sems):
        for cp in self._copies(cins, couts, sems):
            cp.start()

    def finish(self, cins, couts, sems):
        for cp in self._copies(cins, couts, sems):
            cp.wait()


class _SwapComm:
    def __init__(self, grads16):
        n = len(grads16)
        self.n = n
        self.operands = list(grads16)
        self.out_shape = [jax.ShapeDtypeStruct(g.shape[:1] + g.shape[2:], g.dtype) for g in grads16]
        self.aliases = {}
        self.sems = [pltpu.SemaphoreType.DMA((n,)), pltpu.SemaphoreType.DMA((n,))]

    def _copies(self, cins, couts, sems):
        x, y, c = _position()
        return [pltpu.make_async_remote_copy(
            src_ref=cins[i].at[:, 1 - c], dst_ref=couts[i], send_sem=sems[0].at[i], recv_sem=sems[1].at[i],
            device_id=(x, y, 1 - c), device_id_type=MESH) for i in range(self.n)]

    def start(self, cins, couts, sems):
        for cp in self._copies(cins, couts, sems):
            cp.start()

    def finish(self, cins, couts, sems):
        for cp in self._copies(cins, couts, sems):
            cp.wait()


class _JoinComm:
    def __init__(self, halves):
        n = len(halves)
        self.n = n
        self.operands = list(halves)
        self.out_shape = [jax.ShapeDtypeStruct(h.shape, h.dtype) for h in halves]
        self.aliases = {i: i for i in range(n)}
        self.sems = [pltpu.SemaphoreType.DMA((n,)), pltpu.SemaphoreType.DMA((n,))]

    def _copy(self, cins, couts, sems, i, which):
        x, y, c = _position()
        return pltpu.make_async_remote_copy(
            src_ref=cins[i].at[which], dst_ref=couts[i].at[which], send_sem=sems[0].at[i], recv_sem=sems[1].at[i],
            device_id=(x, y, 1 - c), device_id_type=MESH)

    def start(self, cins, couts, sems):
        _, _, c = _position()
        for i in range(self.n):
            self._copy(cins, couts, sems, i, c).start()

    def finish(self, cins, couts, sems):
        _, _, c = _position()
        for i in range(self.n):
            self._copy(cins, couts, sems, i, 1 - c).wait_recv()
        for i in range(self.n):
            self._copy(cins, couts, sems, i, c).wait_send()


class _Gather8Comm:
    def __init__(self, block):
        self.operands = [block]
        self.out_shape = [jax.ShapeDtypeStruct((N_DEV,) + block.shape, block.dtype)]
        self.aliases = {}
        self.sems = [pltpu.SemaphoreType.DMA((N_DEV - 1,)), pltpu.SemaphoreType.DMA((N_DEV - 1,)),
                     pltpu.SemaphoreType.DMA]
        self.flips = [(fx, fy, fc) for fx in (0, 1) for fy in (0, 1) for fc in (0, 1) if (fx, fy, fc) != (0, 0, 0)]

    def _peers(self):
        x, y, c = _position()
        return [(_flip(x, fx), _flip(y, fy), _flip(c, fc)) for fx, fy, fc in self.flips]

    def _copy(self, cins, couts, sems, k, block, to):
        return pltpu.make_async_remote_copy(src_ref=cins[0], dst_ref=couts[0].at[block], send_sem=sems[0].at[k],
                                            recv_sem=sems[1].at[k], device_id=to, device_id_type=MESH)

    def _mine(self, cins, couts, sems):
        x, y, c = _position()
        return pltpu.make_async_copy(cins[0], couts[0].at[4 * x + 2 * y + c], sems[2])

    def start(self, cins, couts, sems):
        x, y, c = _position()
        self._mine(cins, couts, sems).start()
        for k, peer in enumerate(self._peers()):
            self._copy(cins, couts, sems, k, 4 * x + 2 * y + c, peer).start()

    def finish(self, cins, couts, sems):
        for k, (px, py, pc) in enumerate(self._peers()):
            self._copy(cins, couts, sems, k, 4 * px + 2 * py + pc, (px, py, pc)).wait_recv()
        for k, peer in enumerate(self._peers()):
            self._copy(cins, couts, sems, k, 0, peer).wait_send()
        self._mine(cins, couts, sems).wait()


class _CommList:
    def __init__(self, parts):
        self.parts = list(parts)
        self.operands = [t for p in self.parts for t in p.operands]
        self.out_shape = [t for p in self.parts for t in p.out_shape]
        self.sems = [t for p in self.parts for t in p.sems]
        self.aliases = {}
        n_in = n_out = 0
        for p in self.parts:
            self.aliases.update({n_in + i: n_out + j for i, j in p.aliases.items()})
            n_in += len(p.operands)
            n_out += len(p.out_shape)

    def _split(self, cins, couts, sems):
        pos = [0, 0, 0]
        for p in self.parts:
            sizes = (len(p.operands), len(p.out_shape), len(p.sems))
            yield p, tuple(seq[a:a + k] for seq, a, k in zip((cins, couts, sems), pos, sizes))
            pos = [a + k for a, k in zip(pos, sizes)]

    def start(self, cins, couts, sems):
        for p, refs in self._split(cins, couts, sems):
            p.start(*refs)

    def finish(self, cins, couts, sems):
        for p, refs in self._split(cins, couts, sems):
            p.finish(*refs)

    def split_outputs(self, outs):
        res, pos = [], 0
        for p in self.parts:
            res.append(outs[pos:pos + len(p.out_shape)])
            pos += len(p.out_shape)
        return res


def _call(body, *, name, grid, in_specs, out_specs, out_shape, operands, scratch_shapes=(), comm=None):
    n_grid = len(grid)
    if comm is None:
        return pl.pallas_call(
            body, name=name, grid=grid, in_specs=list(in_specs), out_specs=list(out_specs), out_shape=list(out_shape),
            scratch_shapes=list(scratch_shapes), compiler_params=_params(n_grid))(*operands), ()
    counts = (len(in_specs), len(comm.operands), len(out_specs), len(comm.out_shape), len(scratch_shapes),
              len(comm.sems))

    def fused(*refs):
        parts, pos = [], 0
        for k in counts:
            parts.append(refs[pos:pos + k])
            pos += k
        ins, cins, outs, couts, scr, sems = parts
        first = functools.reduce(jnp.logical_and, [pl.program_id(d) == 0 for d in range(n_grid)])
        last = functools.reduce(jnp.logical_and, [pl.program_id(d) == grid[d] - 1 for d in range(n_grid)])

        @pl.when(first)
        def _():
            comm.start(cins, couts, sems)

        body(*ins, *outs, *scr)

        @pl.when(last)
        def _():
            comm.finish(cins, couts, sems)

    res = pl.pallas_call(
        fused, name=name, grid=grid, in_specs=list(in_specs) + [ANY] * counts[1],
        out_specs=list(out_specs) + [ANY] * counts[3], out_shape=list(out_shape) + list(comm.out_shape),
        scratch_shapes=list(scratch_shapes) + list(comm.sems),
        input_output_aliases={counts[0] + i: counts[2] + j for i, j in comm.aliases.items()},
        compiler_params=_params(n_grid))(*operands, *comm.operands)
    return res[:counts[2]], res[counts[2]:]


def _run_comm(comm, name):
    k_in, k_out = len(comm.operands), len(comm.out_shape)

    def body(*refs):
        cins, couts, sems = refs[:k_in], refs[k_in:k_in + k_out], refs[k_in + k_out:]
        comm.start(cins, couts, sems)
        comm.finish(cins, couts, sems)

    return pl.pallas_call(
        body, name=name, in_specs=[ANY] * k_in, out_specs=[ANY] * k_out, out_shape=list(comm.out_shape),
        scratch_shapes=list(comm.sems), input_output_aliases=dict(comm.aliases))(*comm.operands)


def _ffn_fwd(x, gn, sc, sh, gate, wg, wu, wd, seq, name, comm=None):
    T, D = x.shape
    J, Fs, _ = wg.shape
    tm = _tile(seq, 1024)
    nb = seq // tm

    def body(x_ref, gn_ref, sc_ref, sh_ref, gate_ref, wg_ref, wu_ref, wd_ref,
             h_ref, a_ref, u_ref, f_ref, xo_ref, hs, acc, s16):
        j = pl.program_id(1)

        @pl.when(j == 0)
        def _():
            hb = _norm_mod(x_ref[...], gn_ref[...], sc_ref[...], sh_ref[...]).astype(BF16)
            hs[...] = hb
            h_ref[...] = hb
            acc[...] = jnp.zeros_like(acc)

        hb = hs[...]
        a_all = _dot_nt(hb, wg_ref[...])
        u_all = _dot_nt(hb, wu_ref[...])

        def swiglu_rows(rows):
            a = a_all[rows, :]
            u = u_all[rows, :]
            a_ref[rows, :] = a.astype(BF16)
            u_ref[rows, :] = u.astype(BF16)
            s16[rows, :] = ((a * _sigmoid(a)) * u).astype(BF16)

        _for_row_chunks(tm, swiglu_rows)
        acc[...] += _dot_nn(s16[...], wd_ref[...])

        @pl.when(j == J - 1)
        def _():
            f = acc[...]
            f_ref[...] = f.astype(BF16)
            xo_ref[...] = x_ref[...] + (FFN_RESIDUAL * gate_ref[...]) * f

    row = pl.BlockSpec((tm, D), lambda i, j: (i, 0))
    vec = pl.BlockSpec((1, D), lambda i, j: (0, 0))
    per_b = pl.BlockSpec((None, 1, D), lambda i, j: (i // nb, 0, 0))
    hid = pl.BlockSpec((None, tm, Fs), lambda i, j: (j, i, 0))
    return _call(
        body, name=name, grid=(T // tm, J),
        in_specs=[row, vec, per_b, per_b, per_b] + [pl.BlockSpec((None, Fs, D), lambda i, j: (j, 0, 0))] * 3,
        out_specs=[row, hid, hid, row, row],
        out_shape=[jax.ShapeDtypeStruct((T, D), BF16), jax.ShapeDtypeStruct((J, T, Fs), BF16),
                   jax.ShapeDtypeStruct((J, T, Fs), BF16), jax.ShapeDtypeStruct((T, D), BF16),
                   jax.ShapeDtypeStruct((T, D), F32)],
        scratch_shapes=[pltpu.VMEM((tm, D), BF16), pltpu.VMEM((tm, D), F32), pltpu.VMEM((tm, Fs), BF16)],
        operands=(x, gn, sc, sh, gate, wg, wu, wd), comm=comm)


def _ffn_bwd(dxo, x, f, a, u, gn, sc, gate, wg, wu, wd, seq, name, comm=None):
    T, D = x.shape
    J, Fs, _ = wg.shape
    B = T // seq
    tm = _tile(seq, 512)
    nb = seq // tm

    def body(dxo_ref, x_ref, f_ref, a_ref, u_ref, gn_ref, sc_ref, gate_ref, wg_ref, wu_ref, wd_ref,
             da_ref, du_ref, s_ref, df_ref, dx_ref, dgate_ref, dsc_ref, dsh_ref, dgn_ref, dfs, acc):
        i = pl.program_id(0)
        j = pl.program_id(1)
        first_of_batch = i % nb == 0

        @pl.when(j == 0)
        def _():
            dxo_v = dxo_ref[...]
            dfb = ((FFN_RESIDUAL * gate_ref[...]) * dxo_v).astype(BF16)
            dfs[...] = dfb
            df_ref[...] = dfb
            part = jnp.sum((FFN_RESIDUAL * f_ref[...].astype(F32)) * dxo_v, axis=0, keepdims=True)
            _accumulate(dgate_ref, first_of_batch, part)
            acc[...] = jnp.zeros_like(acc)

        ds_all = _dot_nt(dfs[...], wd_ref[...])

        def swiglu_bwd_rows(rows):
            ds = ds_all[rows, :]
            av = a_ref[rows, :].astype(F32)
            uv = u_ref[rows, :].astype(F32)
            sig = _sigmoid(av)
            sil = av * sig
            s_ref[rows, :] = (sil * uv).astype(BF16)
            da_ref[rows, :] = (ds * uv * (sig * (1.0 + av * (1.0 - sig)))).astype(BF16)
            du_ref[rows, :] = (ds * sil).astype(BF16)

        _for_row_chunks(tm, swiglu_bwd_rows)
        acc[...] += _dot_nn(da_ref[...], wg_ref[...]) + _dot_nn(du_ref[...], wu_ref[...])

        @pl.when(j == J - 1)
        def _():
            _norm_mod_bwd(acc[...], x_ref[...], gn_ref[...], sc_ref[...], dxo_ref[...],
                          first_of_batch, i == 0, dx_ref, dsc_ref, dsh_ref, dgn_ref)

    row = pl.BlockSpec((tm, D), lambda i, j: (i, 0))
    vec = pl.BlockSpec((1, D), lambda i, j: (0, 0))
    per_b = pl.BlockSpec((None, 1, D), lambda i, j: (i // nb, 0, 0))
    hid = pl.BlockSpec((None, tm, Fs), lambda i, j: (j, i, 0))
    hid_shape = jax.ShapeDtypeStruct((J, T, Fs), BF16)
    per_b_shape = jax.ShapeDtypeStruct((B, 1, D), F32)
    return _call(
        body, name=name, grid=(T // tm, J),
        in_specs=[row, row, row, hid, hid, vec, per_b, per_b]
        + [pl.BlockSpec((None, Fs, D), lambda i, j: (j, 0, 0))] * 3,
        out_specs=[hid, hid, hid, row, row, per_b, per_b, per_b, vec],
        out_shape=[hid_shape, hid_shape, hid_shape, jax.ShapeDtypeStruct((T, D), BF16),
                   jax.ShapeDtypeStruct((T, D), F32), per_b_shape, per_b_shape, per_b_shape,
                   jax.ShapeDtypeStruct((1, D), F32)],
        scratch_shapes=[pltpu.VMEM((tm, D), BF16), pltpu.VMEM((tm, D), F32)],
        operands=(dxo, x, f, a, u, gn, sc, gate, wg, wu, wd), comm=comm)


def _wgrad(a, a_spec, b, b_spec, rows, cols, n_tok, name, comm=None):
    tk = _tile(n_tok, 1024)
    nk = n_tok // tk
    half = rows // 2

    def body(a_ref, b_ref, o32_ref, o16_ref, acc):
        k = pl.program_id(1)

        @pl.when(k == 0)
        def _():
            acc[...] = jnp.zeros_like(acc)

        acc[...] += _dot_tn(a_ref[...], b_ref[...])

        @pl.when(k == nk - 1)
        def _():
            for h in range(2):
                v = acc[h * half:(h + 1) * half, :]
                o32_ref[h] = v
                o16_ref[h] = v.astype(BF16)

    out_spec = pl.BlockSpec((None, 2, half, cols), lambda j, k: (j, 0, 0, 0))
    return _call(
        body, name=name, grid=(N_CHIP, nk),
        in_specs=[a_spec(tk), b_spec(tk)],
        out_specs=[out_spec, out_spec],
        out_shape=[jax.ShapeDtypeStruct((N_CHIP, 2, half, cols), F32),
                   jax.ShapeDtypeStruct((N_CHIP, 2, half, cols), BF16)],
        scratch_shapes=[pltpu.VMEM((rows, cols), F32)],
        operands=(a, b), comm=comm)


def _spec_rows(width):
    return lambda tk: pl.BlockSpec((tk, width), lambda j, k: (k, 0))


def _spec_chip_major(width):
    return lambda tk: pl.BlockSpec((None, tk, width), lambda j, k: (j, k, 0))


def _spec_col_block(width):
    return lambda tk: pl.BlockSpec((tk, width), lambda j, k: (k, j))


def _in_proj(x, gn, sc, sh, w_in, seq, comm=None):
    T, D = x.shape
    N = w_in.shape[0]
    tm = _tile(seq, 1024)
    nb = seq // tm

    def body(x_ref, gn_ref, sc_ref, sh_ref, w_ref, h_ref, p_ref, hs):
        @pl.when(pl.program_id(1) == 0)
        def _():
            hb = _norm_mod(x_ref[...], gn_ref[...], sc_ref[...], sh_ref[...]).astype(BF16)
            hs[...] = hb
            h_ref[...] = hb

        p_ref[...] = _dot_nt(hs[...], w_ref[...]).astype(BF16)

    row = pl.BlockSpec((tm, D), lambda i, j: (i, 0))
    per_b = pl.BlockSpec((None, 1, D), lambda i, j: (i // nb, 0, 0))
    return _call(
        body, name="mix_in_proj", grid=(T // tm, N // PROJ_TILE),
        in_specs=[row, pl.BlockSpec((1, D), lambda i, j: (0, 0)), per_b, per_b,
                  pl.BlockSpec((PROJ_TILE, D), lambda i, j: (j, 0))],
        out_specs=[row, pl.BlockSpec((tm, PROJ_TILE), lambda i, j: (i, j))],
        out_shape=[jax.ShapeDtypeStruct((T, D), BF16), jax.ShapeDtypeStruct((T, N), BF16)],
        scratch_shapes=[pltpu.VMEM((tm, D), BF16)],
        operands=(x, gn, sc, sh, w_in), comm=comm)


def _attn_specs(nblk):
    def own(col):
        return lambda b, n: (b * nblk + n, col)

    def prev(col):
        return lambda b, n: (b * nblk + jnp.maximum(n - 1, 0), col)

    kv = (BLOCK, 2 * HEAD_DIM)
    return [pl.BlockSpec((BLOCK, D_MODEL), own(COLB_Q)),
            pl.BlockSpec(kv, prev(COLB_K)), pl.BlockSpec(kv, own(COLB_K)),
            pl.BlockSpec(kv, prev(COLB_V)), pl.BlockSpec(kv, own(COLB_V))]


def _band_operands(prev_ref, own_ref, lo):
    band = jnp.concatenate([prev_ref[...], own_ref[...]], axis=0).astype(F32)
    rolled = pltpu.roll(band, HEAD_DIM, 1)
    zero = jnp.zeros_like(band)
    head0 = jnp.concatenate([jnp.where(lo, band, zero), jnp.where(lo, zero, rolled)], axis=0).astype(BF16)
    head1 = jnp.concatenate([jnp.where(lo, rolled, zero), jnp.where(lo, zero, band)], axis=0).astype(BF16)
    return head0, head1


PAIRS_PER_KV = N_Q_HEADS // 2 // N_KV_HEADS
BAND = 2 * BLOCK


def _band_valid(has_prev):
    qi = lax.broadcasted_iota(jnp.int32, (PAIRS_PER_KV * BLOCK, BAND), 0) & (BLOCK - 1)
    sj = lax.broadcasted_iota(jnp.int32, (PAIRS_PER_KV * BLOCK, BAND), 1)
    rel = qi + BLOCK - sj
    return (rel >= 0) & (rel < BLOCK) & ((sj >= BLOCK) | has_prev)


def _pair_lanes(kvh, pp):
    pair = kvh * PAIRS_PER_KV + pp
    return slice(pair * 2 * HEAD_DIM, (pair + 1) * 2 * HEAD_DIM)


def _stack_pairs(ref, kvh):
    return jnp.concatenate([ref[:, _pair_lanes(kvh, pp)] for pp in range(PAIRS_PER_KV)], axis=0)


def _rows_per_pair(columns):
    return jnp.concatenate(columns, axis=0)


def _attn_fwd(proj, sinks, batch, seq, comm=None):
    T = proj.shape[0]
    nblk = seq // BLOCK

    def body(sink_ref, q_ref, kp_ref, ko_ref, vp_ref, vo_ref, o_ref, lse_ref):
        lo = lax.broadcasted_iota(jnp.int32, (1, 2 * HEAD_DIM), 1) < HEAD_DIM
        head_lane = lax.broadcasted_iota(jnp.int32, (1, N_Q_HEADS), 1)
        valid = _band_valid(pl.program_id(1) > 0)
        k_ops = _band_operands(kp_ref, ko_ref, lo)
        v_ops = _band_operands(vp_ref, vo_ref, lo)
        lse_all = jnp.zeros((BLOCK, N_Q_HEADS), F32)
        col = jnp.zeros((BLOCK, 1), F32)
        side0_row = lax.broadcasted_iota(jnp.int32, (2 * BAND, 2 * HEAD_DIM), 0) < BAND
        low_lane = lax.broadcasted_iota(jnp.int32, (2 * BAND, 2 * HEAD_DIM), 1) < HEAD_DIM
        side_ones = jnp.where(side0_row == low_lane, 1.0, 0.0).astype(BF16)
        for kvh in range(N_KV_HEADS):
            s_all = _dot_nt(_stack_pairs(q_ref, kvh), k_ops[kvh]) * ATTN_SCALE
            weights, maxes, sink_terms = [], [], []
            for side in range(2):
                heads = [2 * (kvh * PAIRS_PER_KV + pp) + side for pp in range(PAIRS_PER_KV)]
                sink = _rows_per_pair([col + sink_ref[0, h] for h in heads])
                s = jnp.where(valid, s_all[:, side * BAND:(side + 1) * BAND], MASK_VALUE)
                m = jnp.maximum(jnp.max(s, axis=-1, keepdims=True), sink)
                weights.append(jnp.where(valid, jnp.exp(s - m), 0.0).astype(BF16))
                maxes.append(m)
                sink_terms.append(jnp.exp(sink - m))
            p_all = jnp.concatenate(weights, axis=1)
            den = _dot_nn(p_all, side_ones) + jnp.where(lo, sink_terms[0], sink_terms[1])
            out = _dot_nn(p_all, v_ops[kvh]) / den
            for pp in range(PAIRS_PER_KV):
                o_ref[:, _pair_lanes(kvh, pp)] = out[pp * BLOCK:(pp + 1) * BLOCK].astype(BF16)
            for side in range(2):
                lse = maxes[side] + jnp.log(den[:, side * HEAD_DIM:side * HEAD_DIM + 1])
                for pp in range(PAIRS_PER_KV):
                    h = 2 * (kvh * PAIRS_PER_KV + pp) + side
                    lse_all = jnp.where(head_lane == h, lse[pp * BLOCK:(pp + 1) * BLOCK], lse_all)
        lse_ref[...] = lse_all

    return _call(
        body, name="attn_fwd", grid=(batch, nblk),
        in_specs=[SMEM_SPEC] + _attn_specs(nblk),
        out_specs=[pl.BlockSpec((BLOCK, D_MODEL), lambda b, n: (b * nblk + n, 0)),
                   pl.BlockSpec((BLOCK, N_Q_HEADS), lambda b, n: (b * nblk + n, 0))],
        out_shape=[jax.ShapeDtypeStruct((T, D_MODEL), BF16), jax.ShapeDtypeStruct((T, N_Q_HEADS), F32)],
        operands=(sinks, proj, proj, proj, proj, proj), comm=comm)


def _conv_u(ca, cb):
    return ca.astype(F32) * _sigmoid(cb.astype(F32))


def _conv_specs(ts, tiles_per_seq):
    per_tile = ts // CONV_PAD

    def tile(col):
        return lambda b, t: (b * tiles_per_seq + t, col)

    def before(col):
        return lambda b, t: (jnp.maximum((b * tiles_per_seq + t) * per_tile - 1, 0), col)

    return [pl.BlockSpec((ts, D_MODEL), tile(COLB_CA)), pl.BlockSpec((ts, D_MODEL), tile(COLB_CB)),
            pl.BlockSpec((CONV_PAD, D_MODEL), before(COLB_CA)), pl.BlockSpec((CONV_PAD, D_MODEL), before(COLB_CB))]


SUBLANES = 8


def _fill_upad(upad, ca_ref, cb_ref, cah_ref, cbh_ref, t):
    halo = _conv_u(cah_ref[...], cbh_ref[...])
    upad[0, 0:CONV_PAD, :] = jnp.where(t > 0, halo, jnp.zeros_like(halo))
    upad[0, CONV_PAD:, :] = _conv_u(ca_ref[...], cb_ref[...])


def _fill_shifted(pad):
    rows = pad.shape[1] - SUBLANES
    for b in range(1, SUBLANES):
        pad[b, 0:rows, :] = pad[0, b:b + rows, :]


def _shifted_rows(pad, offset, rows):
    b = offset % SUBLANES
    return pad[b, offset - b:offset - b + rows, :]


def _layernorm_stats(y):
    mu = jnp.mean(y, axis=-1, keepdims=True)
    yc = y - mu
    rstd = lax.rsqrt(jnp.mean(yc * yc, axis=-1, keepdims=True) + EPS)
    return yc * rstd, rstd


def _conv_fwd(proj, w_dw, b_dw, ln_g, ln_b, batch, seq, comm=None):
    T = proj.shape[0]
    ts = _tile(seq, 256)
    nt = seq // ts
    shift = CONV_PAD - (CONV_WIDTH - 1)

    def body(ca_ref, cb_ref, cah_ref, cbh_ref, w_ref, b_ref, g_ref, beta_ref, y_ref, z_ref, upad):
        _fill_upad(upad, ca_ref, cb_ref, cah_ref, cbh_ref, pl.program_id(1))
        _fill_shifted(upad)
        y = jnp.zeros((ts, D_MODEL), F32) + b_ref[...]
        for k in range(CONV_WIDTH):
            y = y + w_ref[k:k + 1, :] * _shifted_rows(upad, shift + k, ts)
        y_ref[...] = y
        lnh, _ = _layernorm_stats(y)
        ln = lnh * g_ref[...] + beta_ref[...]
        z_ref[...] = (ln * _sigmoid(ln)).astype(BF16)

    vec = pl.BlockSpec((1, D_MODEL), lambda b, t: (0, 0))
    row = pl.BlockSpec((ts, D_MODEL), lambda b, t: (b * nt + t, 0))
    return _call(
        body, name="conv_fwd", grid=(batch, nt),
        in_specs=_conv_specs(ts, nt) + [pl.BlockSpec((CONV_PAD, D_MODEL), lambda b, t: (0, 0)), vec, vec, vec],
        out_specs=[row, row],
        out_shape=[jax.ShapeDtypeStruct((T, D_MODEL), F32), jax.ShapeDtypeStruct((T, D_MODEL), BF16)],
        scratch_shapes=[pltpu.VMEM((SUBLANES, ts + CONV_PAD, D_MODEL), F32)],
        operands=(proj, proj, proj, proj, w_dw, b_dw, ln_g, ln_b), comm=comm)


def _merge(o, z, proj, w_ao, w_co, w_out, x, gate, seq):
    T, D = x.shape
    tm = _tile(seq, 512)
    nb = seq // tm

    def body(o_ref, z_ref, ga_ref, gc_ref, wao_ref, wco_ref, wout_ref, x_ref, gate_ref,
             ya_ref, yc_ref, mg_ref, mo_ref, xo_ref):
        ya = _dot_nn(o_ref[...], wao_ref[...])
        yc = _dot_nn(z_ref[...], wco_ref[...])
        ya_ref[...] = ya.astype(BF16)
        yc_ref[...] = yc.astype(BF16)
        merged = (_sigmoid(ga_ref[...].astype(F32)) * ya + _sigmoid(gc_ref[...].astype(F32)) * yc).astype(BF16)
        mg_ref[...] = merged
        mo = _dot_nn(merged, wout_ref[...])
        mo_ref[...] = mo.astype(BF16)
        xo_ref[...] = x_ref[...] + gate_ref[...] * mo

    row = pl.BlockSpec((tm, D), lambda i: (i, 0))
    mat = pl.BlockSpec((D, D), lambda i: (0, 0))
    act = jax.ShapeDtypeStruct((T, D), BF16)
    return pl.pallas_call(
        body, name="mix_merge", grid=(T // tm,),
        in_specs=[row, row, pl.BlockSpec((tm, D), lambda i: (i, COLB_GA)), pl.BlockSpec((tm, D), lambda i: (i, COLB_GC)),
                  mat, mat, mat, row, pl.BlockSpec((None, 1, D), lambda i: (i // nb, 0, 0))],
        out_specs=[row, row, row, row, row],
        out_shape=[act, act, act, act, jax.ShapeDtypeStruct((T, D), F32)],
        compiler_params=_params(1),
    )(o, z, proj, proj, w_ao, w_co, w_out, x, gate)


def _final_loss(x, gf, target):
    T, D = x.shape
    tm = _tile(T, 512)

    def body(x_ref, gf_ref, t_ref, dx_ref, lp_ref, dgf_ref):
        first = pl.program_id(0) == 0
        xv = x_ref[...]
        gfv = gf_ref[...]
        r = lax.rsqrt(jnp.mean(xv * xv, axis=-1, keepdims=True) + EPS)
        xh = xv * r
        err = xh * gfv - t_ref[...]
        _accumulate(lp_ref, first, jnp.sum(err * err, axis=0, keepdims=True))
        dy = err * (1.0 / D)
        _accumulate(dgf_ref, first, jnp.sum(dy * xh, axis=0, keepdims=True))
        dxh = dy * gfv
        dx_ref[...] = r * (dxh - xh * jnp.mean(dxh * xh, axis=-1, keepdims=True))

    row = pl.BlockSpec((tm, D), lambda i: (i, 0))
    vec = pl.BlockSpec((1, D), lambda i: (0, 0))
    return pl.pallas_call(
        body, name="final_loss", grid=(T // tm,),
        in_specs=[row, vec, row], out_specs=[row, vec, vec],
        out_shape=[jax.ShapeDtypeStruct((T, D), F32), jax.ShapeDtypeStruct((1, D), F32),
                   jax.ShapeDtypeStruct((1, D), F32)],
        compiler_params=_params(1),
    )(x, gf, target)


def _merge_bwd(dxo, mo, gate, proj, ya, yc, w_out, w_ao, w_co, seq, comm=None):
    T, D = dxo.shape
    B = T // seq
    tm = _tile(seq, 512)
    nb = seq // tm

    def body(dxo_ref, mo_ref, gate_ref, ga_ref, gc_ref, ya_ref, yc_ref, wout_ref, wao_ref, wco_ref,
             dmo_ref, dya_ref, dyc_ref, dga_ref, dgc_ref, do_ref, dz_ref, dgate_ref):
        dxo_v = dxo_ref[...]
        dmo = (gate_ref[...] * dxo_v).astype(BF16)
        dmo_ref[...] = dmo
        _accumulate(dgate_ref, pl.program_id(0) % nb == 0,
                    jnp.sum(mo_ref[...].astype(F32) * dxo_v, axis=0, keepdims=True))
        dm = _dot_nt(dmo, wout_ref[...])
        sa = _sigmoid(ga_ref[...].astype(F32))
        sc = _sigmoid(gc_ref[...].astype(F32))
        dya = (sa * dm).astype(BF16)
        dyc = (sc * dm).astype(BF16)
        dya_ref[...] = dya
        dyc_ref[...] = dyc
        dga_ref[...] = (dm * ya_ref[...].astype(F32) * (sa * (1.0 - sa))).astype(BF16)
        dgc_ref[...] = (dm * yc_ref[...].astype(F32) * (sc * (1.0 - sc))).astype(BF16)
        do_ref[...] = _dot_nt(dya, wao_ref[...]).astype(BF16)
        dz_ref[...] = _dot_nt(dyc, wco_ref[...]).astype(BF16)

    row = pl.BlockSpec((tm, D), lambda i: (i, 0))
    mat = pl.BlockSpec((D, D), lambda i: (0, 0))
    per_b = pl.BlockSpec((None, 1, D), lambda i: (i // nb, 0, 0))
    act = jax.ShapeDtypeStruct((T, D), BF16)
    return _call(
        body, name="mix_merge_bwd", grid=(T // tm,),
        in_specs=[row, row, per_b, pl.BlockSpec((tm, D), lambda i: (i, COLB_GA)),
                  pl.BlockSpec((tm, D), lambda i: (i, COLB_GC)), row, row, mat, mat, mat],
        out_specs=[row] * 7 + [per_b],
        out_shape=[act] * 7 + [jax.ShapeDtypeStruct((B, 1, D), F32)],
        operands=(dxo, mo, gate, proj, proj, ya, yc, w_out, w_ao, w_co), comm=comm)


def _attn_bwd(proj, sinks, o, do, lse, batch, seq, comm=None):
    T = proj.shape[0]
    nblk = seq // BLOCK
    n_steps = batch * nblk

    def body(sink_ref, q_ref, kp_ref, ko_ref, vp_ref, vo_ref, o_ref, do_ref, lse_ref,
             dq_ref, dkp_ref, dko_ref, dvp_ref, dvo_ref, dsink_ref):
        lo = lax.broadcasted_iota(jnp.int32, (1, 2 * HEAD_DIM), 1) < HEAD_DIM
        sink_lane = lax.broadcasted_iota(jnp.int32, (1, 2 * HEAD_DIM), 1)
        valid = _band_valid(pl.program_id(1) > 0)
        k_ops = _band_operands(kp_ref, ko_ref, lo)
        v_ops = _band_operands(vp_ref, vo_ref, lo)
        dsink = jnp.zeros((1, 2 * HEAD_DIM), F32)
        col = jnp.zeros((BLOCK, 1), F32)

        def fold(both):
            return (jnp.where(lo, both[:BAND], 0.0)
                    + pltpu.roll(jnp.where(lo, 0.0, both[BAND:]), HEAD_DIM, 1))

        dk_heads, dv_heads = [], []
        for kvh in range(N_KV_HEADS):
            q4 = _stack_pairs(q_ref, kvh)
            do4 = _stack_pairs(do_ref, kvh)
            dd = do4.astype(F32) * _stack_pairs(o_ref, kvh).astype(F32)
            s_all = _dot_nt(q4, k_ops[kvh]) * ATTN_SCALE
            dp_all = _dot_nt(do4, v_ops[kvh])
            ds_sides, p_sides = [], []
            for side in range(2):
                heads = [2 * (kvh * PAIRS_PER_KV + pp) + side for pp in range(PAIRS_PER_KV)]
                mine = lo if side == 0 else jnp.logical_not(lo)
                cols = slice(side * BAND, (side + 1) * BAND)
                sink = _rows_per_pair([col + sink_ref[0, h] for h in heads])
                lse = _rows_per_pair([lse_ref[:, h:h + 1] for h in heads])
                delta = jnp.sum(jnp.where(mine, dd, 0.0), axis=-1, keepdims=True)
                p = jnp.where(valid, jnp.exp(jnp.where(valid, s_all[:, cols], MASK_VALUE) - lse), 0.0)
                ds_sides.append((p * (dp_all[:, cols] - delta) * ATTN_SCALE).astype(BF16))
                p_sides.append(p.astype(BF16))
                sink_part = jnp.exp(sink - lse) * delta
                for pp, h in enumerate(heads):
                    dsink = dsink + jnp.where(sink_lane == h, -jnp.sum(sink_part[pp * BLOCK:(pp + 1) * BLOCK]), 0.0)
            ds_all = jnp.concatenate(ds_sides, axis=1)
            dq4 = _dot_nn(ds_all, k_ops[kvh])
            for pp in range(PAIRS_PER_KV):
                dq_ref[:, _pair_lanes(kvh, pp)] = dq4[pp * BLOCK:(pp + 1) * BLOCK].astype(BF16)
            dk_heads.append(fold(_dot_tn(ds_all, q4)))
            dv_heads.append(fold(_dot_tn(jnp.concatenate(p_sides, axis=1), do4)))
        dk = dk_heads[0] + pltpu.roll(dk_heads[1], HEAD_DIM, 1)
        dv = dv_heads[0] + pltpu.roll(dv_heads[1], HEAD_DIM, 1)
        dkp_ref[...] = dk[:BLOCK]
        dko_ref[...] = dk[BLOCK:]
        dvp_ref[...] = dv[:BLOCK]
        dvo_ref[...] = dv[BLOCK:]
        dsink_ref[...] = dsink

    def own(b, n):
        return (b * nblk + n, 0)

    row = pl.BlockSpec((BLOCK, D_MODEL), own)
    kv = pl.BlockSpec((BLOCK, 2 * HEAD_DIM), own)
    kv_shape = jax.ShapeDtypeStruct((T, 2 * HEAD_DIM), F32)
    return _call(
        body, name="attn_bwd", grid=(batch, nblk),
        in_specs=[SMEM_SPEC] + _attn_specs(nblk) + [row, row, pl.BlockSpec((BLOCK, N_Q_HEADS), own)],
        out_specs=[row, kv, kv, kv, kv, pl.BlockSpec((None, 1, 2 * HEAD_DIM), lambda b, n: (b * nblk + n, 0, 0))],
        out_shape=[jax.ShapeDtypeStruct((T, D_MODEL), BF16), kv_shape, kv_shape, kv_shape, kv_shape,
                   jax.ShapeDtypeStruct((n_steps, 1, 2 * HEAD_DIM), F32)],
        operands=(sinks, proj, proj, proj, proj, proj, o, do, lse), comm=comm)


def _conv_bwd(proj, dz, ydw, w_dw, ln_g, ln_b, batch, seq, comm=None):
    T = proj.shape[0]
    ts = _tile(seq, 256)
    nt = seq // ts
    per_tile = ts // CONV_PAD
    shift = CONV_PAD - (CONV_WIDTH - 1)

    def body(ca_ref, cb_ref, cah_ref, cbh_ref, dz_ref, dzn_ref, y_ref, yn_ref, w_ref, g_ref, beta_ref,
             dca_ref, dcb_ref, dw_ref, db_ref, dg_ref, dbeta_ref, upad, dypad):
        t = pl.program_id(1)
        first = (pl.program_id(0) == 0) & (t == 0)
        gv = g_ref[...]

        def ln_bwd(dzv, yv):
            lnh, rstd = _layernorm_stats(yv)
            ln = lnh * gv + beta_ref[...]
            sg = _sigmoid(ln)
            dln = dzv.astype(F32) * (sg * (1.0 + ln * (1.0 - sg)))
            dyh = dln * gv
            dy = rstd * (dyh - jnp.mean(dyh, axis=-1, keepdims=True)
                         - lnh * jnp.mean(dyh * lnh, axis=-1, keepdims=True))
            return dy, dln, lnh

        dy, dln, lnh = ln_bwd(dz_ref[...], y_ref[...])
        dy_next, _, _ = ln_bwd(dzn_ref[...], yn_ref[...])
        dypad[0, 0:ts, :] = dy
        dypad[0, ts:, :] = jnp.where(t < nt - 1, dy_next, jnp.zeros_like(dy_next))
        _fill_shifted(dypad)
        _fill_upad(upad, ca_ref, cb_ref, cah_ref, cbh_ref, t)
        _fill_shifted(upad)

        _accumulate(dg_ref, first, jnp.sum(dln * lnh, axis=0, keepdims=True))
        _accumulate(dbeta_ref, first, jnp.sum(dln, axis=0, keepdims=True))
        _accumulate(db_ref, first, jnp.sum(dy, axis=0, keepdims=True))

        @pl.when(first)
        def _():
            dw_ref[...] = jnp.zeros_like(dw_ref)

        du = jnp.zeros((ts, D_MODEL), F32)
        for k in range(CONV_WIDTH):
            du = du + w_ref[k:k + 1, :] * _shifted_rows(dypad, CONV_WIDTH - 1 - k, ts)
            dw_ref[k:k + 1, :] += jnp.sum(dy * _shifted_rows(upad, shift + k, ts), axis=0, keepdims=True)
        cav = ca_ref[...].astype(F32)
        sb = _sigmoid(cb_ref[...].astype(F32))
        dca_ref[...] = (du * sb).astype(BF16)
        dcb_ref[...] = (du * cav * (sb * (1.0 - sb))).astype(BF16)

    def tile(b, t):
        return (b * nt + t, 0)

    def after(b, t):
        return (jnp.minimum((b * nt + t + 1) * per_tile, T // CONV_PAD - 1), 0)

    row = pl.BlockSpec((ts, D_MODEL), tile)
    halo = pl.BlockSpec((CONV_PAD, D_MODEL), after)
    vec = pl.BlockSpec((1, D_MODEL), lambda b, t: (0, 0))
    wspec = pl.BlockSpec((CONV_PAD, D_MODEL), lambda b, t: (0, 0))
    act = jax.ShapeDtypeStruct((T, D_MODEL), BF16)
    vec_shape = jax.ShapeDtypeStruct((1, D_MODEL), F32)
    return _call(
        body, name="conv_bwd", grid=(batch, nt),
        in_specs=_conv_specs(ts, nt) + [row, halo, row, halo, wspec, vec, vec],
        out_specs=[row, row, wspec, vec, vec, vec],
        out_shape=[act, act, jax.ShapeDtypeStruct((CONV_PAD, D_MODEL), F32), vec_shape, vec_shape, vec_shape],
        scratch_shapes=[pltpu.VMEM((SUBLANES, ts + CONV_PAD, D_MODEL), F32)] * 2,
        operands=(proj, proj, proj, proj, dz, dz, ydw, ydw, w_dw, ln_g, ln_b), comm=comm)


def _in_proj_bwd(pieces, w_cols, x, gn, sc, dxo, seq, comm=None):
    T, D = x.shape
    B = T // seq
    wide, narrow = list(pieces[:-1]), pieces[-1]
    P = len(wide)
    nw = narrow.shape[1]
    tm = _tile(seq, 512)
    nb = seq // tm

    def body(*refs):
        wide_refs = refs[:P]
        kv_ref, w_ref, wkv_ref, x_ref, gn_ref, sc_ref, dxo_ref, dx_ref, dsc_ref, dsh_ref, dgn_ref, acc = refs[P:]
        i = pl.program_id(0)
        j = pl.program_id(1)

        @pl.when(j == 0)
        def _():
            acc[...] = _dot_nn(kv_ref[...], wkv_ref[...])

        for p in range(P):
            @pl.when(j == p)
            def _(p=p):
                acc[...] += _dot_nn(wide_refs[p][...], w_ref[...])

        @pl.when(j == P - 1)
        def _():
            _norm_mod_bwd(acc[...], x_ref[...], gn_ref[...], sc_ref[...], dxo_ref[...],
                          i % nb == 0, i == 0, dx_ref, dsc_ref, dsh_ref, dgn_ref)

    row = pl.BlockSpec((tm, D), lambda i, j: (i, 0))
    vec = pl.BlockSpec((1, D), lambda i, j: (0, 0))
    per_b = pl.BlockSpec((None, 1, D), lambda i, j: (i // nb, 0, 0))
    per_b_shape = jax.ShapeDtypeStruct((B, 1, D), F32)
    return _call(
        body, name="mix_in_proj_bwd", grid=(T // tm, P),
        in_specs=[row] * P + [pl.BlockSpec((tm, nw), lambda i, j: (i, 0)),
                              pl.BlockSpec((D, D), lambda i, j: (j, 0)),
                              pl.BlockSpec((nw, D), lambda i, j: (P * D // nw, 0)), row, vec, per_b, row],
        out_specs=[row, per_b, per_b, vec],
        out_shape=[jax.ShapeDtypeStruct((T, D), F32), per_b_shape, per_b_shape, jax.ShapeDtypeStruct((1, D), F32)],
        scratch_shapes=[pltpu.VMEM((tm, D), F32)],
        operands=(*wide, narrow, w_cols, w_cols, x, gn, sc, dxo), comm=comm)


def _wgrad_rows(piece, h, out32, out16, row_offset, name):
    T, n = piece.shape
    C = h.shape[1]
    tk = _tile(T, 1024)
    nk = T // tk

    def body(a_ref, b_ref, in32, in16, o32_ref, o16_ref, acc, stage16, sems):
        k = pl.program_id(0)

        @pl.when(k == 0)
        def _():
            acc[...] = jnp.zeros_like(acc)

        acc[...] += _dot_tn(a_ref[...], b_ref[...])

        @pl.when(k == nk - 1)
        def _():
            stage16[...] = acc[...].astype(BF16)
            rows = pl.ds(row_offset, n)
            copies = [pltpu.make_async_copy(acc, o32_ref.at[rows, :], sems.at[0]),
                      pltpu.make_async_copy(stage16, o16_ref.at[rows, :], sems.at[1])]
            for cp in copies:
                cp.start()
            for cp in copies:
                cp.wait()

    return pl.pallas_call(
        body, name=name, grid=(nk,),
        in_specs=[pl.BlockSpec((tk, n), lambda k: (k, 0)), pl.BlockSpec((tk, C), lambda k: (k, 0)), ANY, ANY],
        out_specs=[ANY, ANY], out_shape=[jax.ShapeDtypeStruct(out32.shape, F32), jax.ShapeDtypeStruct(out16.shape, BF16)],
        scratch_shapes=[pltpu.VMEM((n, C), F32), pltpu.VMEM((n, C), BF16), pltpu.SemaphoreType.DMA((2,))],
        input_output_aliases={2: 0, 3: 1}, compiler_params=_params(1),
    )(piece, h, out32, out16)


def _ada_fwd(c_all, w_ada, b_cols):
    nbatch, D = c_all.shape
    N = w_ada.shape[1]
    tn = _tile(N, 768)

    def body(c_ref, w_ref, b_ref, o_ref):
        cv = c_ref[...]
        act = (cv * _sigmoid(cv)).astype(BF16)
        o_ref[...] = _dot_nn(act, w_ref[...].astype(BF16)) + b_ref[...]

    return pl.pallas_call(
        body, name="ada_fwd", grid=(N // tn,),
        in_specs=[pl.BlockSpec((nbatch, D), lambda j: (0, 0)), pl.BlockSpec((D, tn), lambda j: (0, j)),
                  pl.BlockSpec((1, tn), lambda j: (0, j))],
        out_specs=pl.BlockSpec((nbatch, tn), lambda j: (0, j)),
        out_shape=jax.ShapeDtypeStruct((nbatch, N), F32),
        compiler_params=_params(1),
    )(c_all, w_ada, b_cols)


def _adamw(w, g, m, v):
    m = ADAM_B1 * m + (1.0 - ADAM_B1) * g
    v = ADAM_B2 * v + (1.0 - ADAM_B2) * (g * g)
    m_hat = m / (1.0 - ADAM_B1 ** ADAM_STEP)
    v_hat = v / (1.0 - ADAM_B2 ** ADAM_STEP)
    delta = -ADAM_LR * (m_hat / (jnp.sqrt(v_hat) + ADAM_EPS) + ADAM_WD * w)
    return delta, m, v


def _adam_call(w, g, m, v, name, comm=None):
    R, C = w.shape
    tr = _row_tile(R, 512)

    def body(w_ref, g_ref, m_ref, v_ref, d_ref, mo_ref, vo_ref):
        d, mn, vn = _adamw(w_ref[...], g_ref[...], m_ref[...], v_ref[...])
        d_ref[...] = d
        mo_ref[...] = mn
        vo_ref[...] = vn

    blk = pl.BlockSpec((tr, C), lambda i: (i, 0))
    shape = jax.ShapeDtypeStruct((R, C), F32)
    return _call(body, name=name, grid=(R // tr,), in_specs=[blk] * 4, out_specs=[blk] * 3, out_shape=[shape] * 3,
                 operands=(w, g, m, v), comm=comm)


ADAM_GROUP_STEPS = 8


def _adam_group(ws, gs, ms, vs, name, comm=None):
    n = len(ws)

    def body(*refs):
        ins, outs = refs[:4 * n], refs[4 * n:]
        for i in range(n):
            d, mn, vn = _adamw(*(r[...] for r in ins[4 * i:4 * i + 4]))
            outs[3 * i][...] = d
            outs[3 * i + 1][...] = mn
            outs[3 * i + 2][...] = vn

    operands, in_specs, out_specs, out_shape = [], [], [], []
    for w, g, m, v in zip(ws, gs, ms, vs):
        R, C = w.shape
        blk = pl.BlockSpec((R // ADAM_GROUP_STEPS, C), lambda i: (i, 0))
        operands += [w, g, m, v]
        in_specs += [blk] * 4
        out_specs += [blk] * 3
        out_shape += [jax.ShapeDtypeStruct((R, C), F32)] * 3
    outs, comm_outs = _call(body, name=name, grid=(ADAM_GROUP_STEPS,), in_specs=in_specs, out_specs=out_specs,
                            out_shape=out_shape, operands=operands, comm=comm)
    return [tuple(outs[3 * i:3 * i + 3]) for i in range(n)], comm_outs


def _ada_adam(c_act_t, dmod_cols, w, m, v, comm):
    R, C = w.shape
    nbatch = c_act_t.shape[1]
    tr = _tile(R, 128)

    def body(ct_ref, dm_ref, w_ref, m_ref, v_ref, g_ref, d_ref, mo_ref, vo_ref):
        cv = ct_ref[...]
        g = _dot_nn((cv * _sigmoid(cv)).astype(BF16), dm_ref[...].astype(BF16))
        g_ref[...] = g
        d, mn, vn = _adamw(w_ref[...], g, m_ref[...], v_ref[...])
        d_ref[...] = d
        mo_ref[...] = mn
        vo_ref[...] = vn

    blk = pl.BlockSpec((tr, C), lambda i: (i, 0))
    shape = jax.ShapeDtypeStruct((R, C), F32)
    return _call(
        body, name="ada_adam", grid=(R // tr,),
        in_specs=[pl.BlockSpec((tr, nbatch), lambda i: (i, 0)), pl.BlockSpec((nbatch, C), lambda i: (0, 0)),
                  blk, blk, blk],
        out_specs=[blk] * 4, out_shape=[shape] * 4,
        operands=(c_act_t, dmod_cols, w, m, v), comm=comm)


def _small_adam(gathered, w, m, v, rows_b0, rows_b1, rows_vec):
    _, P, D = gathered.shape
    R = w.shape[0]

    def body(ga_ref, w_ref, m_ref, v_ref, sum_ref, g_ref, d_ref, mo_ref, vo_ref):
        total = ga_ref[0]
        for dev in range(1, N_DEV):
            total = total + ga_ref[dev]
        sum_ref[...] = total
        g_ref[...] = jnp.zeros_like(g_ref)
        g_ref[0:N_MOD, :] = (sum_ref[rows_b0:rows_b0 + N_MOD, :] + sum_ref[rows_b1:rows_b1 + N_MOD, :])
        g_ref[N_MOD:N_MOD + 8, :] = sum_ref[rows_vec:rows_vec + 8, :]
        d, mn, vn = _adamw(w_ref[...], g_ref[...], m_ref[...], v_ref[...])
        d_ref[...] = d
        mo_ref[...] = mn
        vo_ref[...] = vn

    shape = jax.ShapeDtypeStruct((R, D), F32)
    return pl.pallas_call(
        body, name="small_adam",
        in_specs=[VMEM_SPEC] * 4, out_specs=[VMEM_SPEC] * 5,
        out_shape=[jax.ShapeDtypeStruct((P, D), F32), shape, shape, shape, shape],
        compiler_params=pltpu.CompilerParams(vmem_limit_bytes=VMEM_LIMIT),
    )(gathered, w, m, v)


def _gather8(v, name):
    A, W = v.shape
    flips = [(fx, fy, fc) for fx in (0, 1) for fy in (0, 1) for fc in (0, 1) if (fx, fy, fc) != (0, 0, 0)]

    def body(v_ref, out_ref, send_sems, recv_sems, local_sem):
        x, y, c = _position()
        me = 4 * x + 2 * y + c
        mine = pltpu.make_async_copy(v_ref, out_ref.at[me], local_sem)
        mine.start()

        def copy(k, block, to):
            return pltpu.make_async_remote_copy(src_ref=v_ref, dst_ref=out_ref.at[block], send_sem=send_sems.at[k],
                                                recv_sem=recv_sems.at[k], device_id=to, device_id_type=MESH)

        peers = [(_flip(x, fx), _flip(y, fy), _flip(c, fc)) for fx, fy, fc in flips]
        sends = [copy(k, me, peer) for k, peer in enumerate(peers)]
        for cp in sends:
            cp.start()
        for k, (px, py, pc) in enumerate(peers):
            copy(k, 4 * px + 2 * py + pc, (px, py, pc)).wait_recv()
        for cp in sends:
            cp.wait_send()
        mine.wait()

    return pl.pallas_call(
        body, name=name, in_specs=[VMEM_SPEC], out_specs=VMEM_SPEC,
        out_shape=jax.ShapeDtypeStruct((N_DEV, A, W), v.dtype),
        scratch_shapes=[pltpu.SemaphoreType.DMA((N_DEV - 1,)), pltpu.SemaphoreType.DMA((N_DEV - 1,)),
                        pltpu.SemaphoreType.DMA],
    )(v)


def _mod_exchange(part):
    _, A, W = part.shape

    def body(p_ref, out_ref, send_sems, recv_sems, local_sem):
        x, y, c = _position()
        me = 4 * x + 2 * y + c
        chip = 2 * x + y
        mine = pltpu.make_async_copy(p_ref.at[me], out_ref.at[chip], local_sem)
        mine.start()
        peers = [(_flip(x, fx), _flip(y, fy)) for fx, fy in CHIP_FLIPS]
        sends = []
        for k, (px, py) in enumerate(peers):
            sends.append(pltpu.make_async_remote_copy(
                src_ref=p_ref.at[4 * px + 2 * py + c], dst_ref=out_ref.at[chip], send_sem=send_sems.at[k],
                recv_sem=recv_sems.at[k], device_id=(px, py, c), device_id_type=MESH))
        for cp in sends:
            cp.start()
        for k, (px, py) in enumerate(peers):
            pltpu.make_async_remote_copy(
                src_ref=p_ref.at[me], dst_ref=out_ref.at[2 * px + py], send_sem=send_sems.at[k],
                recv_sem=recv_sems.at[k], device_id=(px, py, c), device_id_type=MESH).wait_recv()
        for cp in sends:
            cp.wait_send()
        mine.wait()

    return pl.pallas_call(
        body, name="mod_exchange", in_specs=[VMEM_SPEC], out_specs=VMEM_SPEC,
        out_shape=jax.ShapeDtypeStruct((N_CHIP, A, W), part.dtype),
        scratch_shapes=[pltpu.SemaphoreType.DMA((3,)), pltpu.SemaphoreType.DMA((3,)), pltpu.SemaphoreType.DMA],
    )(part)


def _cast_slot(w, chip_idx, name):
    R, C = w.shape
    tr = _row_tile(R, 512)

    def body(chip_ref, w_ref, o_ref):
        o_ref[...] = w_ref[...].astype(BF16)

    return pl.pallas_call(
        body, name=name,
        grid_spec=pltpu.PrefetchScalarGridSpec(
            num_scalar_prefetch=1, grid=(R // tr,),
            in_specs=[pl.BlockSpec((tr, C), lambda i, chip_ref: (i, 0))],
            out_specs=pl.BlockSpec((None, tr, C), lambda i, chip_ref: (chip_ref[0], i, 0))),
        out_shape=jax.ShapeDtypeStruct((N_CHIP, R, C), BF16),
        compiler_params=_params(1),
    )(chip_idx, w)


def _pair_sum(g32, recv, core, name):
    J, _, r, C = g32.shape

    def body(core_ref, g_ref, r_ref, o_ref):
        o_ref[...] = (g_ref[...] + r_ref[...].astype(F32)).astype(BF16)

    return pl.pallas_call(
        body, name=name,
        grid_spec=pltpu.PrefetchScalarGridSpec(
            num_scalar_prefetch=1, grid=(J,),
            in_specs=[pl.BlockSpec((None, None, r, C), lambda j, core_ref: (j, core_ref[0], 0, 0)),
                      pl.BlockSpec((None, r, C), lambda j, core_ref: (j, 0, 0))],
            out_specs=pl.BlockSpec((None, r, C), lambda j, core_ref: (j, 0, 0))),
        out_shape=jax.ShapeDtypeStruct((J, r, C), BF16),
        compiler_params=_params(1),
    )(core, g32, recv)


def _chip_sum(g32, recv_sib, recv_chips, core_chip, name):
    J, _, r, C = g32.shape

    def body(idx_ref, g_ref, s_ref, o_ref_in, o_ref):
        total = g_ref[...] + s_ref[...].astype(F32)
        for k in range(3):
            total = total + o_ref_in[k].astype(F32)
        o_ref[...] = total

    return pl.pallas_call(
        body, name=name,
        grid_spec=pltpu.PrefetchScalarGridSpec(
            num_scalar_prefetch=1, grid=(1,),
            in_specs=[pl.BlockSpec((None, None, r, C), lambda i, idx: (idx[1], idx[0], 0, 0)),
                      pl.BlockSpec((None, r, C), lambda i, idx: (idx[1], 0, 0)),
                      pl.BlockSpec((3, r, C), lambda i, idx: (0, 0, 0))],
            out_specs=pl.BlockSpec((None, r, C), lambda i, idx: (idx[0], 0, 0))),
        out_shape=jax.ShapeDtypeStruct((2, r, C), F32),
        compiler_params=_params(1),
    )(core_chip, g32, recv_sib, recv_chips)


ICI_US_PER_ELEMENT = 4.6e-5


class _Reducer:
    def __init__(self, core_idx, core_chip):
        self.core_idx, self.core_chip = core_idx, core_chip
        self.grads, self.halves, self.reduced = {}, {}, {}
        self.ready_swap, self.ready_exchange, self.ready_join = [], [], []
        self.inflight, self.current = ([], [], [], None), None
        self.flushes = 0
        self.extra, self.extra_out = None, None

    def add(self, name, grad_pair):
        self.grads[name] = grad_pair
        self.ready_swap.append(name)

    def comm(self, budget_us):
        swaps, self.ready_swap = self.ready_swap, []
        joins, self.ready_join = self.ready_join, []
        exchanges, waiting = [], []
        for item in self.ready_exchange:
            cost = ICI_US_PER_ELEMENT * 2 * item[2].shape[1] * item[2].shape[2]
            if cost <= budget_us:
                exchanges.append(item)
                budget_us -= cost
            else:
                waiting.append(item)
        self.ready_exchange = waiting
        parts = []
        if swaps:
            parts.append(_SwapComm([self.grads[n][1] for n in swaps]))
        if exchanges:
            parts.append(_ExchangeComm([pair for _, _, pair in exchanges]))
        if joins:
            parts.append(_JoinComm([self.halves[n] for n in joins]))
        extra, self.extra = self.extra, None
        if extra is not None:
            parts.append(extra)
        self.inflight = (swaps, exchanges, joins, extra)
        self.current = _CommList(parts) if parts else None
        return self.current

    def done(self, comm_outs):
        if self.current is None:
            return
        swaps, exchanges, joins, extra = self.inflight
        outs = iter(self.current.split_outputs(list(comm_outs)))
        if swaps:
            for n, recv in zip(swaps, next(outs)):
                pair = _pair_sum(self.grads[n][0], recv, self.core_idx, "pair_sum_" + n)
                self.ready_exchange.append((n, recv, pair))
        if exchanges:
            for (n, recv, _), chips in zip(exchanges, next(outs)):
                self.halves[n] = _chip_sum(self.grads[n][0], recv, chips, self.core_chip, "chip_sum_" + n)
                self.ready_join.append(n)
        if joins:
            self.reduced.update(zip(joins, next(outs)))
        if extra is not None:
            self.extra_out = next(outs)
        self.current = None

    def run(self, kernel, budget_us, *args, **kwargs):
        if budget_us is None:
            return kernel(*args, comm=None, **kwargs)[0]
        outs, comm_outs = kernel(*args, comm=self.comm(budget_us), **kwargs)
        self.done(comm_outs)
        return outs

    def step(self):
        comm = self.comm(float("inf"))
        self.flushes += 1
        self.done(_run_comm(comm, "grad_reduce_tail_%d" % self.flushes))


BIG_WEIGHTS = ("ffn1_w_gate", "ffn1_w_up", "ffn1_w_down", "w_in", "w_attn_o", "w_conv_o", "w_out",
               "ffn2_w_gate", "ffn2_w_up", "ffn2_w_down")
VECTORS = ("norm_ffn1_g", "norm_mix_g", "conv_b_dw", "conv_ln_g", "conv_ln_b", "norm_ffn2_g", "final_norm_g")
ROW_DMOD0, ROW_DMOD1, ROW_VEC, ROW_SINK, ROW_CONVW, SMALL_ROWS = 0, 16, 33, 40, 41, 72


FFN1_WEIGHTS = ("ffn1_w_gate", "ffn1_w_up", "ffn1_w_down")
FFN2_WEIGHTS = ("ffn2_w_gate", "ffn2_w_up", "ffn2_w_down")
MIX_WEIGHTS = ("w_in", "w_attn_o", "w_conv_o", "w_out")
COL_SHARDED = ("ffn1_w_gate", "ffn1_w_up", "ffn2_w_gate", "ffn2_w_up", "w_in")


def _local_grads(x, target, mod, slots, small, seq, core_idx, core_chip):
    T, D = x.shape
    B = T // seq
    mods = [mod[:, k][:, None, :] for k in range(N_MOD)]
    sh1, sc1, g1, sh2, sc2, g2, sh3, sc3, g3 = mods
    w = dict(zip(FFN1_WEIGHTS, _run_comm(_GatherComm([slots[n] for n in FFN1_WEIGHTS]), "gather_ffn1")))

    (h1, a1, u1, f1, x1), outs = _ffn_fwd(
        x, small["norm_ffn1_g"], sc1, sh1, g1, w["ffn1_w_gate"], w["ffn1_w_up"], w["ffn1_w_down"], seq, "ffn1_fwd",
        comm=_GatherComm([slots[n] for n in MIX_WEIGHTS]))
    w["w_in"] = outs[0]
    w_ao, w_co, w_o = [t.reshape(D, D) for t in outs[1:]]
    w_in_full = w["w_in"].reshape(IN_WIDTH, D)
    q_end, v_end = D, D + 4 * HEAD_DIM
    w_in_cols = jnp.concatenate([w_in_full[:q_end], w_in_full[v_end:], w_in_full[q_end:v_end]], axis=0)
    (h2, proj), _ = _in_proj(x1, small["norm_mix_g"], sc2, sh2, w_in_cols, seq)
    (o, lse), (w["ffn2_w_gate"], w["ffn2_w_up"]) = _attn_fwd(
        proj, small["attn_sinks"], B, seq, comm=_GatherComm([slots["ffn2_w_gate"], slots["ffn2_w_up"]]))
    (ydw, z), (w["ffn2_w_down"],) = _conv_fwd(
        proj, small["conv_w_dw"], small["conv_b_dw"], small["conv_ln_g"], small["conv_ln_b"], B, seq,
        comm=_GatherComm([slots["ffn2_w_down"]]))
    ya, yc, merged, mo, x2 = _merge(o, z, proj, w_ao, w_co, w_o, x1, g2, seq)
    (h3, a3, u3, f3, x3), _ = _ffn_fwd(x2, small["norm_ffn2_g"], sc3, sh3, g3, w["ffn2_w_gate"], w["ffn2_w_up"],
                                       w["ffn2_w_down"], seq, "ffn2_fwd")
    dx3, loss_parts, d_final_g = _final_loss(x3, small["final_norm_g"], target)

    red = _Reducer(core_idx, core_chip)

    def weight_grad(name, budget_us, a, a_spec, b, b_spec, rows, cols):
        red.add(name, red.run(_wgrad, budget_us, a, a_spec, b, b_spec, rows, cols, T, "dw_" + name))

    def ffn_backward(prefix, dw_budget_us, dxo, xin, h, a, u, f, gn, sc, gate, before_weight_grads=None):
        da, du, s, df, dx, dgate, dsc, dsh, dgn = red.run(
            _ffn_bwd, 170, dxo, xin, f, a, u, gn, sc, gate, w[prefix + "_w_gate"], w[prefix + "_w_up"],
            w[prefix + "_w_down"], seq, prefix + "_bwd")
        if before_weight_grads is not None:
            before_weight_grads(dgate, dsc, dsh, dgn)
        weight_grad(prefix + "_w_down", dw_budget_us, s, _spec_chip_major(FF_SHARD), df, _spec_rows(D), FF_SHARD, D)
        weight_grad(prefix + "_w_gate", dw_budget_us, da, _spec_chip_major(FF_SHARD), h, _spec_rows(D), FF_SHARD, D)
        weight_grad(prefix + "_w_up", dw_budget_us, du, _spec_chip_major(FF_SHARD), h, _spec_rows(D), FF_SHARD, D)
        return dx, dgate, dsc, dsh, dgn

    dx2, dg3, dsc3, dsh3, d_gn3 = ffn_backward("ffn2", None, dx3, x2, h3, a3, u3, f3, small["norm_ffn2_g"], sc3, g3)

    dmo, dya, dyc, dga, dgc, do, dz, dg2 = red.run(_merge_bwd, 45, dx2, mo, g2, proj, ya, yc, w_o, w_ao, w_co, seq)
    shard = D // N_CHIP
    weight_grad("w_out", None, merged, _spec_col_block(shard), dmo, _spec_rows(D), shard, D)
    weight_grad("w_attn_o", None, o, _spec_col_block(shard), dya, _spec_rows(D), shard, D)
    weight_grad("w_conv_o", None, z, _spec_col_block(shard), dyc, _spec_rows(D), shard, D)
    dq, dkp, dko, dvp, dvo, dsink_steps = red.run(_attn_bwd, 100, proj, small["attn_sinks"], o, do, lse, B, seq)
    dca, dcb, d_conv_w, d_conv_b, d_ln_g, d_ln_b = red.run(
        _conv_bwd, 165, proj, dz, ydw, small["conv_w_dw"], small["conv_ln_g"], small["conv_ln_b"], B, seq)

    def band_sum(own, prev):
        prev = prev.reshape(B, seq // BLOCK, BLOCK, 2 * HEAD_DIM)
        moved = jnp.concatenate([prev[:, 1:], jnp.zeros_like(prev[:, :1])], axis=1)
        return (own + moved.reshape(T, 2 * HEAD_DIM)).astype(BF16)

    dkv = jnp.concatenate([band_sum(dko, dkp), band_sum(dvo, dvp)], axis=1)
    g32, g16 = lax.empty((IN_WIDTH, D), F32), lax.empty((IN_WIDTH, D), BF16)
    row_of = {"q": 0, "kv": D, "conv_a": D + 4 * HEAD_DIM, "conv_b": 2 * D + 4 * HEAD_DIM,
              "gate_a": 3 * D + 4 * HEAD_DIM, "gate_c": 4 * D + 4 * HEAD_DIM}
    for tag, piece in (("q", dq), ("kv", dkv), ("conv_a", dca), ("conv_b", dcb), ("gate_a", dga), ("gate_c", dgc)):
        g32, g16 = _wgrad_rows(piece, h2, g32, g16, row_of[tag], "dw_w_in_" + tag)
    red.add("w_in", tuple(g.reshape(N_CHIP, 2, IN_SHARD // 2, D) for g in (g32, g16)))
    dx1, dsc2, dsh2, d_gn2 = red.run(_in_proj_bwd, 90, (dq, dca, dcb, dga, dgc, dkv), w_in_cols, x1,
                                     small["norm_mix_g"], sc2, dx2, seq)

    def gather_small_grads(dg1, dsc1, dsh1, d_gn1):
        dmod = jnp.concatenate([dsh1, dsc1, dg1, dsh2, dsc2, dg2, dsh3, dsc3, dg3], axis=1)
        d_sinks = jnp.sum(dsink_steps, axis=0)
        vec_grads = {"norm_ffn1_g": d_gn1, "norm_mix_g": d_gn2, "conv_b_dw": d_conv_b, "conv_ln_g": d_ln_g,
                     "conv_ln_b": d_ln_b, "norm_ffn2_g": d_gn3, "final_norm_g": d_final_g}
        block = jnp.zeros((SMALL_ROWS, D), F32)
        block = block.at[ROW_DMOD0:ROW_DMOD0 + N_MOD].set(dmod[0]).at[ROW_DMOD1:ROW_DMOD1 + N_MOD].set(dmod[1])
        block = block.at[ROW_VEC:ROW_VEC + len(VECTORS)].set(jnp.concatenate([vec_grads[n] for n in VECTORS], axis=0))
        block = block.at[ROW_SINK, :2 * HEAD_DIM].set(d_sinks[0])
        block = block.at[ROW_CONVW:ROW_CONVW + CONV_WIDTH].set(d_conv_w[:CONV_WIDTH])
        red.extra = _Gather8Comm(block)

    dx0, _, _, _, _ = ffn_backward("ffn1", 38, dx1, x, h1, a1, u1, f1, small["norm_ffn1_g"], sc1, g1,
                                   before_weight_grads=gather_small_grads)
    return loss_parts, dx0, red, red.extra_out[0]


def kernel(x, c, w_ada, b_ada, norm_ffn1_g, ffn1_w_gate, ffn1_w_up, ffn1_w_down, norm_mix_g, w_in, attn_sinks, w_attn_o, conv_w_dw, conv_b_dw, conv_ln_g, conv_ln_b, w_conv_o, w_out, norm_ffn2_g, ffn2_w_gate, ffn2_w_up, ffn2_w_down, final_norm_g, loss_target, m_w_ada, m_b_ada, m_norm_ffn1_g, m_ffn1_w_gate, m_ffn1_w_up, m_ffn1_w_down, m_norm_mix_g, m_w_in, m_attn_sinks, m_w_attn_o, m_conv_w_dw, m_conv_b_dw, m_conv_ln_g, m_conv_ln_b, m_w_conv_o, m_w_out, m_norm_ffn2_g, m_ffn2_w_gate, m_ffn2_w_up, m_ffn2_w_down, m_final_norm_g, v_w_ada, v_b_ada, v_norm_ffn1_g, v_ffn1_w_gate, v_ffn1_w_up, v_ffn1_w_down, v_norm_mix_g, v_w_in, v_attn_sinks, v_w_attn_o, v_conv_w_dw, v_conv_b_dw, v_conv_ln_g, v_conv_ln_b, v_w_conv_o, v_w_out, v_norm_ffn2_g, v_ffn2_w_gate, v_ffn2_w_up, v_ffn2_w_down, v_final_norm_g):
    args = dict(locals())
    B, seq, D = x.shape
    T = B * seq
    xi, yi, ci = _position()
    chip = 2 * xi + yi
    dev = 4 * xi + 2 * yi + ci

    def shard_2d(prefix, name):
        t = args[prefix + name][0]
        return t.T if name in COL_SHARDED else t

    big = {n: shard_2d("", n) for n in BIG_WEIGHTS}
    final_g = final_norm_g[None, :]
    vec_w = {n: (args[n] if n != "final_norm_g" else final_g) for n in VECTORS}

    conv_cols = D // N_CHIP
    conv_flat = jnp.pad(conv_w_dw[0].reshape(-1), (0, 8 * D - CONV_WIDTH * conv_cols)).reshape(8, D)
    first = _gather8(jnp.concatenate([jnp.pad(c, ((0, 8 - B), (0, 0))), conv_flat], axis=0), "gather_c")
    c_all = first[:, :B].reshape(N_DEV * B, D)
    conv_taps = first[::2, 8:].reshape(N_CHIP, 8 * D)[:, :CONV_WIDTH * conv_cols]
    conv_taps = conv_taps.reshape(N_CHIP, CONV_WIDTH, conv_cols).transpose(1, 0, 2).reshape(CONV_WIDTH, D)
    conv_taps = jnp.pad(conv_taps, ((0, CONV_PAD - CONV_WIDTH), (0, 0)))

    ada_cols = w_ada.shape[2]
    b_cols = lax.dynamic_slice(b_ada, (0, chip * ada_cols), (1, ada_cols))
    mod_part = _ada_fwd(c_all, w_ada[0], b_cols).reshape(N_DEV, B, ada_cols)
    mod = _mod_exchange(mod_part).transpose(1, 0, 2).reshape(B, N_MOD, D)

    core_idx = jnp.reshape(ci, (1,)).astype(jnp.int32)
    chip_idx = jnp.reshape(chip, (1,)).astype(jnp.int32)
    core_chip = jnp.stack([ci, chip]).astype(jnp.int32)
    slots = {n: _cast_slot(big[n], chip_idx, "cast_" + n) for n in BIG_WEIGHTS}

    small = dict(vec_w)
    small["attn_sinks"] = attn_sinks
    small["conv_w_dw"] = conv_taps

    loss_parts, dx, red, small_all = _local_grads(
        x.reshape(T, D), loss_target.reshape(T, D), mod, slots, small, seq, core_idx, core_chip)

    loss = lax.psum((0.5 / D) * jnp.sum(loss_parts), ("x", "y", "c"))
    grad_x = dx.reshape(B, seq, D)
    out = {}


    def pack_small(prefix):
        rows = [args[prefix + "b_ada"].reshape(N_MOD, D)]
        rows += [args[prefix + n].reshape(1, D) for n in VECTORS]
        rows += [jnp.pad(args[prefix + "attn_sinks"], ((0, 0), (0, D - N_Q_HEADS)))]
        return jnp.pad(jnp.concatenate(rows, axis=0), ((0, 24 - N_MOD - len(VECTORS) - 1), (0, 0)))

    small_sum, sg, sd, sm, sv = _small_adam(small_all, pack_small(""), pack_small("m_"), pack_small("v_"),
                                           ROW_DMOD0, ROW_DMOD1, ROW_VEC)

    def unpack_small(t):
        res = {"b_ada": t[:N_MOD].reshape(1, N_MOD * D)}
        for k, n in enumerate(VECTORS):
            res[n] = t[N_MOD + k].reshape(args[n].shape)
        res["attn_sinks"] = t[N_MOD + len(VECTORS), :N_Q_HEADS].reshape(1, N_Q_HEADS)
        return res

    unpacked = [unpack_small(t) for t in (sg, sd, sm, sv)]
    for n in ("b_ada", "attn_sinks") + VECTORS:
        out[n] = tuple(u[n] for u in unpacked)

    conv_g = lax.dynamic_slice(small_sum, (ROW_CONVW, chip * conv_cols), (CONV_WIDTH, conv_cols))
    d, mn, vn = red.run(_adam_call, None, conv_w_dw[0], conv_g, m_conv_w_dw[0], v_conv_w_dw[0], "adam_conv_w_dw")
    out["conv_w_dw"] = tuple(t[None] for t in (conv_g, d, mn, vn))

    dmod_rows = jnp.stack([small_all[:, ROW_DMOD0:ROW_DMOD0 + N_MOD], small_all[:, ROW_DMOD1:ROW_DMOD1 + N_MOD]], axis=1)
    dmod_all = dmod_rows.reshape(N_DEV * B, N_MOD * D)
    dmod_cols = lax.dynamic_slice(dmod_all, (0, chip * ada_cols), (N_DEV * B, ada_cols))
    ada_out = red.run(_ada_adam, 35, c_all.T, dmod_cols, w_ada[0], m_w_ada[0], v_w_ada[0])
    out["w_ada"] = tuple(t[None] for t in ada_out)

    def finished(n):
        while n not in red.reduced:
            red.step()
        return red.reduced[n].reshape(big[n].shape)

    def emit(n, g, d, mn, vn):
        out[n] = tuple((t.T if n in COL_SHARDED else t)[None] for t in (g, d, mn, vn))

    early = FFN2_WEIGHTS + MIX_WEIGHTS
    early_g = [finished(n) for n in early]
    early_out = red.run(_adam_group, 45, [big[n] for n in early], early_g, [shard_2d("m_", n) for n in early],
                        [shard_2d("v_", n) for n in early], "adam_early")
    for n, g, (d, mn, vn) in zip(early, early_g, early_out):
        emit(n, g, d, mn, vn)
    for n in ("ffn1_w_down", "ffn1_w_gate", "ffn1_w_up"):
        g = finished(n)
        emit(n, g, *red.run(_adam_call, None, big[n], g, shard_2d("m_", n), shard_2d("v_", n), "adam_" + n))

    order = ("w_ada", "b_ada", "norm_ffn1_g", "ffn1_w_gate", "ffn1_w_up", "ffn1_w_down", "norm_mix_g", "w_in",
             "attn_sinks", "w_attn_o", "conv_w_dw", "conv_b_dw", "conv_ln_g", "conv_ln_b", "w_conv_o", "w_out",
             "norm_ffn2_g", "ffn2_w_gate", "ffn2_w_up", "ffn2_w_down", "final_norm_g")
    return (loss, grad_x, *[out[n][0] for n in order], *[out[n][1] for n in order],
            *[out[n][2] for n in order], *[out[n][3] for n in order])
```

```python
import functools

import jax
import jax.numpy as jnp
from jax import lax
from jax.experimental import pallas as pl
from jax.experimental.pallas import tpu as pltpu

F32 = jnp.float32
BF16 = jnp.bfloat16

D_MODEL = 1024
D_FF = 2816
N_CHIP = 4
N_DEV = 8
FF_SHARD = D_FF // N_CHIP
IN_WIDTH = 5376
IN_SHARD = IN_WIDTH // N_CHIP
HEAD_DIM = 64
N_Q_HEADS = 16
N_KV_HEADS = 2
BLOCK = 128
CONV_WIDTH = 31
CONV_PAD = 32
N_MOD = 9
EPS = 1e-6
FFN_RESIDUAL = 0.5
ATTN_SCALE = HEAD_DIM ** -0.5
MASK_VALUE = -1e30

ADAM_LR = 0.001
ADAM_B1 = 0.9
ADAM_B2 = 0.999
ADAM_EPS = 1e-08
ADAM_WD = 0.01
ADAM_STEP = 10

COLB_Q, COLB_CA, COLB_CB, COLB_GA, COLB_GC = 0, 1, 2, 3, 4
COLB_K, COLB_V = 40, 41
PROJ_TILE = 768

VMEM_LIMIT = 56 * 1024 * 1024
MESH = pl.DeviceIdType.MESH
ANY = pl.BlockSpec(memory_space=pl.ANY)
VMEM_SPEC = pl.BlockSpec(memory_space=pltpu.VMEM)
SMEM_SPEC = pl.BlockSpec(memory_space=pltpu.SMEM)


def _params(n_grid):
    return pltpu.CompilerParams(dimension_semantics=("arbitrary",) * n_grid, vmem_limit_bytes=VMEM_LIMIT)


def _tile(n, pref):
    t = min(n, pref)
    while n % t:
        t //= 2
    return t


def _row_tile(rows, cap):
    for t in range(min(rows, cap) // 16 * 16, 0, -16):
        if rows % t == 0:
            return t
    return rows


def _sigmoid(v):
    return 1.0 / (1.0 + jnp.exp(-v))


def _dot_nn(a, b):
    return lax.dot_general(a, b, (((1,), (0,)), ((), ())), preferred_element_type=F32)


def _dot_nt(a, b):
    return lax.dot_general(a, b, (((1,), (1,)), ((), ())), preferred_element_type=F32)


def _dot_tn(a, b):
    return lax.dot_general(a, b, (((0,), (0,)), ((), ())), preferred_element_type=F32)


ROW_CHUNK = 16


def _for_row_chunks(n_rows, fn):
    for r in range(0, n_rows, ROW_CHUNK):
        fn(slice(r, r + ROW_CHUNK))


def _norm_mod(xv, gn, sc, sh):
    r = lax.rsqrt(jnp.mean(xv * xv, axis=-1, keepdims=True) + EPS)
    return ((xv * r) * gn) * (1.0 + sc) + sh


def _accumulate(ref, first, value):
    @pl.when(first)
    def _():
        ref[...] = value

    @pl.when(jnp.logical_not(first))
    def _():
        ref[...] += value


def _norm_mod_bwd(dh, xv, gn, sc, dxo, first_of_batch, first, dx_ref, dsc_ref, dsh_ref, dgn_ref):
    r = lax.rsqrt(jnp.mean(xv * xv, axis=-1, keepdims=True) + EPS)
    xh = xv * r
    _accumulate(dsh_ref, first_of_batch, jnp.sum(dh, axis=0, keepdims=True))
    _accumulate(dsc_ref, first_of_batch, jnp.sum(dh * (xh * gn), axis=0, keepdims=True))
    dn = dh * (1.0 + sc)
    _accumulate(dgn_ref, first, jnp.sum(dn * xh, axis=0, keepdims=True))
    dxh = dn * gn
    dx_ref[...] = dxo + r * (dxh - xh * jnp.mean(dxh * xh, axis=-1, keepdims=True))


CHIP_FLIPS = ((1, 0), (0, 1), (1, 1))


def _position():
    return lax.axis_index("x"), lax.axis_index("y"), lax.axis_index("c")


def _flip(v, f):
    return 1 - v if f else v


class _GatherComm:
    def __init__(self, bufs):
        n = len(bufs)
        self.n = n
        self.operands = list(bufs)
        self.out_shape = [jax.ShapeDtypeStruct(b.shape, b.dtype) for b in bufs]
        self.aliases = {i: i for i in range(n)}
        self.sems = [pltpu.SemaphoreType.DMA((6 * n,)), pltpu.SemaphoreType.DMA((6 * n,))]
        self.rows = [b.shape[1] // 2 for b in bufs]

    def _half(self, ref, i, which):
        return ref.at[pl.ds(which * self.rows[i], self.rows[i]), :]

    def _ici(self, cins, couts, sems, i, k, dst_chip, to):
        x, y, c = _position()
        return pltpu.make_async_remote_copy(
            src_ref=self._half(cins[i].at[2 * x + y], i, c), dst_ref=self._half(couts[i].at[dst_chip], i, c),
            send_sem=sems[0].at[3 * i + k], recv_sem=sems[1].at[3 * i + k], device_id=to, device_id_type=MESH)

    def _d2d(self, couts, sems, i, k, src_chip, which):
        x, y, c = _position()
        place = self._half(couts[i].at[src_chip], i, which)
        return pltpu.make_async_remote_copy(
            src_ref=place, dst_ref=place, send_sem=sems[0].at[3 * self.n + 3 * i + k],
            recv_sem=sems[1].at[3 * self.n + 3 * i + k], device_id=(x, y, 1 - c), device_id_type=MESH)

    def _peers(self):
        x, y, _ = _position()
        return [(_flip(x, fx), _flip(y, fy)) for fx, fy in CHIP_FLIPS]

    def start(self, cins, couts, sems):
        x, y, c = _position()
        for i in range(self.n):
            for k, (px, py) in enumerate(self._peers()):
                self._ici(cins, couts, sems, i, k, 2 * x + y, (px, py, c)).start()

    def finish(self, cins, couts, sems):
        _, _, c = _position()
        peers = self._peers()
        for i in range(self.n):
            for k, (px, py) in enumerate(peers):
                self._ici(cins, couts, sems, i, k, 2 * px + py, (px, py, c)).wait_recv()
                self._d2d(couts, sems, i, k, 2 * px + py, c).start()
        for i in range(self.n):
            for k, (px, py) in enumerate(peers):
                self._d2d(couts, sems, i, k, 2 * px + py, 1 - c).wait_recv()
        for i in range(self.n):
            for k, (px, py) in enumerate(peers):
                self._ici(cins, couts, sems, i, k, 2 * px + py, (px, py, c)).wait_send()
                self._d2d(couts, sems, i, k, 2 * px + py, c).wait_send()


class _ExchangeComm:
    def __init__(self, pairs):
        n = len(pairs)
        self.n = n
        self.operands = list(pairs)
        self.out_shape = [jax.ShapeDtypeStruct((3,) + p.shape[1:], p.dtype) for p in pairs]
        self.aliases = {}
        self.sems = [pltpu.SemaphoreType.DMA((3 * n,)), pltpu.SemaphoreType.DMA((3 * n,))]

    def _copies(self, cins, couts, sems):
        x, y, c = _position()
        peers = [(_flip(x, fx), _flip(y, fy)) for fx, fy in CHIP_FLIPS]
        return [pltpu.make_async_remote_copy(
            src_ref=cins[i].at[2 * px + py], dst_ref=couts[i].at[k], send_sem=sems[0].at[3 * i + k],
            recv_sem=sems[1].at[3 * i + k], device_id=(px, py, c), device_id_type=MESH)
            for i in range(self.n) for k, (px, py) in enumerate(peers)]

    def start(self, cins, couts, sems):
        for cp in self._copies(cins, couts, sems):
            cp.start()

    def finish(self, cins, couts, sems):
        for cp in self._copies(cins, couts, sems):
            cp.wait()


class _SwapComm:
    def __init__(self, grads16):
        n = len(grads16)
        self.n = n
        self.operands = list(grads16)
        self.out_shape = [jax.ShapeDtypeStruct(g.shape[:1] + g.shape[2:], g.dtype) for g in grads16]
        self.aliases = {}
        self.sems = [pltpu.SemaphoreType.DMA((n,)), pltpu.SemaphoreType.DMA((n,))]

    def _copies(self, cins, couts, sems):
        x, y, c = _position()
        return [pltpu.make_async_remote_copy(
            src_ref=cins[i].at[:, 1 - c], dst_ref=couts[i], send_sem=sems[0].at[i], recv_sem=sems[1].at[i],
            device_id=(x, y, 1 - c), device_id_type=MESH) for i in range(self.n)]

    def start(self, cins, couts, sems):
        for cp in self._copies(cins, couts, sems):
            cp.start()

    def finish(self, cins, couts, sems):
        for cp in self._copies(cins, couts, sems):
            cp.wait()


class _JoinComm:
    def __init__(self, halves):
        n = len(halves)
        self.n = n
        self.operands = list(halves)
        self.out_shape = [jax.ShapeDtypeStruct(h.shape, h.dtype) for h in halves]
        self.aliases = {i: i for i in range(n)}
        self.sems = [pltpu.SemaphoreType.DMA((n,)), pltpu.SemaphoreType.DMA((n,))]

    def _copy(self, cins, couts, sems, i, which):
        x, y, c = _position()
        return pltpu.make_async_remote_copy(
            src_ref=cins[i].at[which], dst_ref=couts[i].at[which], send_sem=sems[0].at[i], recv_sem=sems[1].at[i],
            device_id=(x, y, 1 - c), device_id_type=MESH)

    def start(self, cins, couts, sems):
        _, _, c = _position()
        for i in range(self.n):
            self._copy(cins, couts, sems, i, c).start()

    def finish(self, cins, couts, sems):
        _, _, c = _position()
        for i in range(self.n):
            self._copy(cins, couts, sems, i, 1 - c).wait_recv()
        for i in range(self.n):
            self._copy(cins, couts, sems, i, c).wait_send()


class _Gather8Comm:
    def __init__(self, block):
        self.operands = [block]
        self.out_shape = [jax.ShapeDtypeStruct((N_DEV,) + block.shape, block.dtype)]
        self.aliases = {}
        self.sems = [pltpu.SemaphoreType.DMA((N_DEV - 1,)), pltpu.SemaphoreType.DMA((N_DEV - 1,)),
                     pltpu.SemaphoreType.DMA]
        self.flips = [(fx, fy, fc) for fx in (0, 1) for fy in (0, 1) for fc in (0, 1) if (fx, fy, fc) != (0, 0, 0)]

    def _peers(self):
        x, y, c = _position()
        return [(_flip(x, fx), _flip(y, fy), _flip(c, fc)) for fx, fy, fc in self.flips]

    def _copy(self, cins, couts, sems, k, block, to):
        return pltpu.make_async_remote_copy(src_ref=cins[0], dst_ref=couts[0].at[block], send_sem=sems[0].at[k],
                                            recv_sem=sems[1].at[k], device_id=to, device_id_type=MESH)

    def _mine(self, cins, couts, sems):
        x, y, c = _position()
        return pltpu.make_async_copy(cins[0], couts[0].at[4 * x + 2 * y + c], sems[2])

    def start(self, cins, couts, sems):
        x, y, c = _position()
        self._mine(cins, couts, sems).start()
        for k, peer in enumerate(self._peers()):
            self._copy(cins, couts, sems, k, 4 * x + 2 * y + c, peer).start()

    def finish(self, cins, couts, sems):
        for k, (px, py, pc) in enumerate(self._peers()):
            self._copy(cins, couts, sems, k, 4 * px + 2 * py + pc, (px, py, pc)).wait_recv()
        for k, peer in enumerate(self._peers()):
            self._copy(cins, couts, sems, k, 0, peer).wait_send()
        self._mine(cins, couts, sems).wait()


class _CommList:
    def __init__(self, parts):
        self.parts = list(parts)
        self.operands = [t for p in self.parts for t in p.operands]
        self.out_shape = [t for p in self.parts for t in p.out_shape]
        self.sems = [t for p in self.parts for t in p.sems]
        self.aliases = {}
        n_in = n_out = 0
        for p in self.parts:
            self.aliases.update({n_in + i: n_out + j for i, j in p.aliases.items()})
            n_in += len(p.operands)
            n_out += len(p.out_shape)

    def _split(self, cins, couts, sems):
        pos = [0, 0, 0]
        for p in self.parts:
            sizes = (len(p.operands), len(p.out_shape), len(p.sems))
            yield p, tuple(seq[a:a + k] for seq, a, k in zip((cins, couts, sems), pos, sizes))
            pos = [a + k for a, k in zip(pos, sizes)]

    def start(self, cins, couts, sems):
        for p, refs in self._split(cins, couts, sems):
            p.start(*refs)

    def finish(self, cins, couts, sems):
        for p, refs in self._split(cins, couts, sems):
            p.finish(*refs)

    def split_outputs(self, outs):
        res, pos = [], 0
        for p in self.parts:
            res.append(outs[pos:pos + len(p.out_shape)])
            pos += len(p.out_shape)
        return res


def _call(body, *, name, grid, in_specs, out_specs, out_shape, operands, scratch_shapes=(), comm=None):
    n_grid = len(grid)
    if comm is None:
        return pl.pallas_call(
            body, name=name, grid=grid, in_specs=list(in_specs), out_specs=list(out_specs), out_shape=list(out_shape),
            scratch_shapes=list(scratch_shapes), compiler_params=_params(n_grid))(*operands), ()
    counts = (len(in_specs), len(comm.operands), len(out_specs), len(comm.out_shape), len(scratch_shapes),
              len(comm.sems))

    def fused(*refs):
        parts, pos = [], 0
        for k in counts:
            parts.append(refs[pos:pos + k])
            pos += k
        ins, cins, outs, couts, scr, sems = parts
        first = functools.reduce(jnp.logical_and, [pl.program_id(d) == 0 for d in range(n_grid)])
        last = functools.reduce(jnp.logical_and, [pl.program_id(d) == grid[d] - 1 for d in range(n_grid)])

        @pl.when(first)
        def _():
            comm.start(cins, couts, sems)

        body(*ins, *outs, *scr)

        @pl.when(last)
        def _():
            comm.finish(cins, couts, sems)

    res = pl.pallas_call(
        fused, name=name, grid=grid, in_specs=list(in_specs) + [ANY] * counts[1],
        out_specs=list(out_specs) + [ANY] * counts[3], out_shape=list(out_shape) + list(comm.out_shape),
        scratch_shapes=list(scratch_shapes) + list(comm.sems),
        input_output_aliases={counts[0] + i: counts[2] + j for i, j in comm.aliases.items()},
        compiler_params=_params(n_grid))(*operands, *comm.operands)
    return res[:counts[2]], res[counts[2]:]


def _run_comm(comm, name):
    k_in, k_out = len(comm.operands), len(comm.out_shape)

    def body(*refs):
        cins, couts, sems = refs[:k_in], refs[k_in:k_in + k_out], refs[k_in + k_out:]
        comm.start(cins, couts, sems)
        comm.finish(cins, couts, sems)

    return pl.pallas_call(
        body, name=name, in_specs=[ANY] * k_in, out_specs=[ANY] * k_out, out_shape=list(comm.out_shape),
        scratch_shapes=list(comm.sems), input_output_aliases=dict(comm.aliases))(*comm.operands)


def _ffn_fwd(x, gn, sc, sh, gate, wg, wu, wd, seq, name, comm=None):
    T, D = x.shape
    J, Fs, _ = wg.shape
    tm = _tile(seq, 1024)
    nb = seq // tm

    def body(x_ref, gn_ref, sc_ref, sh_ref, gate_ref, wg_ref, wu_ref, wd_ref,
             h_ref, a_ref, u_ref, f_ref, xo_ref, hs, acc, s16):
        j = pl.program_id(1)

        @pl.when(j == 0)
        def _():
            hb = _norm_mod(x_ref[...], gn_ref[...], sc_ref[...], sh_ref[...]).astype(BF16)
            hs[...] = hb
            h_ref[...] = hb
            acc[...] = jnp.zeros_like(acc)

        hb = hs[...]
        a_all = _dot_nt(hb, wg_ref[...])
        u_all = _dot_nt(hb, wu_ref[...])

        def swiglu_rows(rows):
            a = a_all[rows, :]
            u = u_all[rows, :]
            a_ref[rows, :] = a.astype(BF16)
            u_ref[rows, :] = u.astype(BF16)
            s16[rows, :] = ((a * _sigmoid(a)) * u).astype(BF16)

        _for_row_chunks(tm, swiglu_rows)
        acc[...] += _dot_nn(s16[...], wd_ref[...])

        @pl.when(j == J - 1)
        def _():
            f = acc[...]
            f_ref[...] = f.astype(BF16)
            xo_ref[...] = x_ref[...] + (FFN_RESIDUAL * gate_ref[...]) * f

    row = pl.BlockSpec((tm, D), lambda i, j: (i, 0))
    vec = pl.BlockSpec((1, D), lambda i, j: (0, 0))
    per_b = pl.BlockSpec((None, 1, D), lambda i, j: (i // nb, 0, 0))
    hid = pl.BlockSpec((None, tm, Fs), lambda i, j: (j, i, 0))
    return _call(
        body, name=name, grid=(T // tm, J),
        in_specs=[row, vec, per_b, per_b, per_b] + [pl.BlockSpec((None, Fs, D), lambda i, j: (j, 0, 0))] * 3,
        out_specs=[row, hid, hid, row, row],
        out_shape=[jax.ShapeDtypeStruct((T, D), BF16), jax.ShapeDtypeStruct((J, T, Fs), BF16),
                   jax.ShapeDtypeStruct((J, T, Fs), BF16), jax.ShapeDtypeStruct((T, D), BF16),
                   jax.ShapeDtypeStruct((T, D), F32)],
        scratch_shapes=[pltpu.VMEM((tm, D), BF16), pltpu.VMEM((tm, D), F32), pltpu.VMEM((tm, Fs), BF16)],
        operands=(x, gn, sc, sh, gate, wg, wu, wd), comm=comm)


def _ffn_bwd(dxo, x, f, a, u, gn, sc, gate, wg, wu, wd, seq, name, comm=None):
    T, D = x.shape
    J, Fs, _ = wg.shape
    B = T // seq
    tm = _tile(seq, 512)
    nb = seq // tm

    def body(dxo_ref, x_ref, f_ref, a_ref, u_ref, gn_ref, sc_ref, gate_ref, wg_ref, wu_ref, wd_ref,
             da_ref, du_ref, s_ref, df_ref, dx_ref, dgate_ref, dsc_ref, dsh_ref, dgn_ref, dfs, acc):
        i = pl.program_id(0)
        j = pl.program_id(1)
        first_of_batch = i % nb == 0

        @pl.when(j == 0)
        def _():
            dxo_v = dxo_ref[...]
            dfb = ((FFN_RESIDUAL * gate_ref[...]) * dxo_v).astype(BF16)
            dfs[...] = dfb
            df_ref[...] = dfb
            part = jnp.sum((FFN_RESIDUAL * f_ref[...].astype(F32)) * dxo_v, axis=0, keepdims=True)
            _accumulate(dgate_ref, first_of_batch, part)
            acc[...] = jnp.zeros_like(acc)

        ds_all = _dot_nt(dfs[...], wd_ref[...])

        def swiglu_bwd_rows(rows):
            ds = ds_all[rows, :]
            av = a_ref[rows, :].astype(F32)
            uv = u_ref[rows, :].astype(F32)
            sig = _sigmoid(av)
            sil = av * sig
            s_ref[rows, :] = (sil * uv).astype(BF16)
            da_ref[rows, :] = (ds * uv * (sig * (1.0 + av * (1.0 - sig)))).astype(BF16)
            du_ref[rows, :] = (ds * sil).astype(BF16)

        _for_row_chunks(tm, swiglu_bwd_rows)
        acc[...] += _dot_nn(da_ref[...], wg_ref[...]) + _dot_nn(du_ref[...], wu_ref[...])

        @pl.when(j == J - 1)
        def _():
            _norm_mod_bwd(acc[...], x_ref[...], gn_ref[...], sc_ref[...], dxo_ref[...],
                          first_of_batch, i == 0, dx_ref, dsc_ref, dsh_ref, dgn_ref)

    row = pl.BlockSpec((tm, D), lambda i, j: (i, 0))
    vec = pl.BlockSpec((1, D), lambda i, j: (0, 0))
    per_b = pl.BlockSpec((None, 1, D), lambda i, j: (i // nb, 0, 0))
    hid = pl.BlockSpec((None, tm, Fs), lambda i, j: (j, i, 0))
    hid_shape = jax.ShapeDtypeStruct((J, T, Fs), BF16)
    per_b_shape = jax.ShapeDtypeStruct((B, 1, D), F32)
    return _call(
        body, name=name, grid=(T // tm, J),
        in_specs=[row, row, row, hid, hid, vec, per_b, per_b]
        + [pl.BlockSpec((None, Fs, D), lambda i, j: (j, 0, 0))] * 3,
        out_specs=[hid, hid, hid, row, row, per_b, per_b, per_b, vec],
        out_shape=[hid_shape, hid_shape, hid_shape, jax.ShapeDtypeStruct((T, D), BF16),
                   jax.ShapeDtypeStruct((T, D), F32), per_b_shape, per_b_shape, per_b_shape,
                   jax.ShapeDtypeStruct((1, D), F32)],
        scratch_shapes=[pltpu.VMEM((tm, D), BF16), pltpu.VMEM((tm, D), F32)],
        operands=(dxo, x, f, a, u, gn, sc, gate, wg, wu, wd), comm=comm)


def _wgrad(a, a_spec, b, b_spec, rows, cols, n_tok, name, comm=None):
    tk = _tile(n_tok, 1024)
    nk = n_tok // tk
    half = rows // 2

    def body(a_ref, b_ref, o32_ref, o16_ref, acc):
        k = pl.program_id(1)

        @pl.when(k == 0)
        def _():
            acc[...] = jnp.zeros_like(acc)

        acc[...] += _dot_tn(a_ref[...], b_ref[...])

        @pl.when(k == nk - 1)
        def _():
            for h in range(2):
                v = acc[h * half:(h + 1) * half, :]
                o32_ref[h] = v
                o16_ref[h] = v.astype(BF16)

    out_spec = pl.BlockSpec((None, 2, half, cols), lambda j, k: (j, 0, 0, 0))
    return _call(
        body, name=name, grid=(N_CHIP, nk),
        in_specs=[a_spec(tk), b_spec(tk)],
        out_specs=[out_spec, out_spec],
        out_shape=[jax.ShapeDtypeStruct((N_CHIP, 2, half, cols), F32),
                   jax.ShapeDtypeStruct((N_CHIP, 2, half, cols), BF16)],
        scratch_shapes=[pltpu.VMEM((rows, cols), F32)],
        operands=(a, b), comm=comm)


def _spec_rows(width):
    return lambda tk: pl.BlockSpec((tk, width), lambda j, k: (k, 0))


def _spec_chip_major(width):
    return lambda tk: pl.BlockSpec((None, tk, width), lambda j, k: (j, k, 0))


def _spec_col_block(width):
    return lambda tk: pl.BlockSpec((tk, width), lambda j, k: (k, j))


def _in_proj(x, gn, sc, sh, w_in, seq, comm=None):
    T, D = x.shape
    N = w_in.shape[0]
    tm = _tile(seq, 1024)
    nb = seq // tm

    def body(x_ref, gn_ref, sc_ref, sh_ref, w_ref, h_ref, p_ref, hs):
        @pl.when(pl.program_id(1) == 0)
        def _():
            hb = _norm_mod(x_ref[...], gn_ref[...], sc_ref[...], sh_ref[...]).astype(BF16)
            hs[...] = hb
            h_ref[...] = hb

        p_ref[...] = _dot_nt(hs[...], w_ref[...]).astype(BF16)

    row = pl.BlockSpec((tm, D), lambda i, j: (i, 0))
    per_b = pl.BlockSpec((None, 1, D), lambda i, j: (i // nb, 0, 0))
    return _call(
        body, name="mix_in_proj", grid=(T // tm, N // PROJ_TILE),
        in_specs=[row, pl.BlockSpec((1, D), lambda i, j: (0, 0)), per_b, per_b,
                  pl.BlockSpec((PROJ_TILE, D), lambda i, j: (j, 0))],
        out_specs=[row, pl.BlockSpec((tm, PROJ_TILE), lambda i, j: (i, j))],
        out_shape=[jax.ShapeDtypeStruct((T, D), BF16), jax.ShapeDtypeStruct((T, N), BF16)],
        scratch_shapes=[pltpu.VMEM((tm, D), BF16)],
        operands=(x, gn, sc, sh, w_in), comm=comm)


def _attn_specs(nblk):
    def own(col):
        return lambda b, n: (b * nblk + n, col)

    def prev(col):
        return lambda b, n: (b * nblk + jnp.maximum(n - 1, 0), col)

    kv = (BLOCK, 2 * HEAD_DIM)
    return [pl.BlockSpec((BLOCK, D_MODEL), own(COLB_Q)),
            pl.BlockSpec(kv, prev(COLB_K)), pl.BlockSpec(kv, own(COLB_K)),
            pl.BlockSpec(kv, prev(COLB_V)), pl.BlockSpec(kv, own(COLB_V))]


def _band_operands(prev_ref, own_ref, lo):
    band = jnp.concatenate([prev_ref[...], own_ref[...]], axis=0).astype(F32)
    rolled = pltpu.roll(band, HEAD_DIM, 1)
    zero = jnp.zeros_like(band)
    head0 = jnp.concatenate([jnp.where(lo, band, zero), jnp.where(lo, zero, rolled)], axis=0).astype(BF16)
    head1 = jnp.concatenate([jnp.where(lo, rolled, zero), jnp.where(lo, zero, band)], axis=0).astype(BF16)
    return head0, head1


PAIRS_PER_KV = N_Q_HEADS // 2 // N_KV_HEADS
BAND = 2 * BLOCK


def _band_valid(has_prev):
    qi = lax.broadcasted_iota(jnp.int32, (PAIRS_PER_KV * BLOCK, BAND), 0) & (BLOCK - 1)
    sj = lax.broadcasted_iota(jnp.int32, (PAIRS_PER_KV * BLOCK, BAND), 1)
    rel = qi + BLOCK - sj
    return (rel >= 0) & (rel < BLOCK) & ((sj >= BLOCK) | has_prev)


def _pair_lanes(kvh, pp):
    pair = kvh * PAIRS_PER_KV + pp
    return slice(pair * 2 * HEAD_DIM, (pair + 1) * 2 * HEAD_DIM)


def _stack_pairs(ref, kvh):
    return jnp.concatenate([ref[:, _pair_lanes(kvh, pp)] for pp in range(PAIRS_PER_KV)], axis=0)


def _rows_per_pair(columns):
    return jnp.concatenate(columns, axis=0)


def _attn_fwd(proj, sinks, batch, seq, comm=None):
    T = proj.shape[0]
    nblk = seq // BLOCK

    def body(sink_ref, q_ref, kp_ref, ko_ref, vp_ref, vo_ref, o_ref, lse_ref):
        lo = lax.broadcasted_iota(jnp.int32, (1, 2 * HEAD_DIM), 1) < HEAD_DIM
        head_lane = lax.broadcasted_iota(jnp.int32, (1, N_Q_HEADS), 1)
        valid = _band_valid(pl.program_id(1) > 0)
        k_ops = _band_operands(kp_ref, ko_ref, lo)
        v_ops = _band_operands(vp_ref, vo_ref, lo)
        lse_all = jnp.zeros((BLOCK, N_Q_HEADS), F32)
        col = jnp.zeros((BLOCK, 1), F32)
        side0_row = lax.broadcasted_iota(jnp.int32, (2 * BAND, 2 * HEAD_DIM), 0) < BAND
        low_lane = lax.broadcasted_iota(jnp.int32, (2 * BAND, 2 * HEAD_DIM), 1) < HEAD_DIM
        side_ones = jnp.where(side0_row == low_lane, 1.0, 0.0).astype(BF16)
        for kvh in range(N_KV_HEADS):
            s_all = _dot_nt(_stack_pairs(q_ref, kvh), k_ops[kvh]) * ATTN_SCALE
            weights, maxes, sink_terms = [], [], []
            for side in range(2):
                heads = [2 * (kvh * PAIRS_PER_KV + pp) + side for pp in range(PAIRS_PER_KV)]
                sink = _rows_per_pair([col + sink_ref[0, h] for h in heads])
                s = jnp.where(valid, s_all[:, side * BAND:(side + 1) * BAND], MASK_VALUE)
                m = jnp.maximum(jnp.max(s, axis=-1, keepdims=True), sink)
                weights.append(jnp.where(valid, jnp.exp(s - m), 0.0).astype(BF16))
                maxes.append(m)
                sink_terms.append(jnp.exp(sink - m))
            p_all = jnp.concatenate(weights, axis=1)
            den = _dot_nn(p_all, side_ones) + jnp.where(lo, sink_terms[0], sink_terms[1])
            out = _dot_nn(p_all, v_ops[kvh]) / den
            for pp in range(PAIRS_PER_KV):
                o_ref[:, _pair_lanes(kvh, pp)] = out[pp * BLOCK:(pp + 1) * BLOCK].astype(BF16)
            for side in range(2):
                lse = maxes[side] + jnp.log(den[:, side * HEAD_DIM:side * HEAD_DIM + 1])
                for pp in range(PAIRS_PER_KV):
                    h = 2 * (kvh * PAIRS_PER_KV + pp) + side
                    lse_all = jnp.where(head_lane == h, lse[pp * BLOCK:(pp + 1) * BLOCK], lse_all)
        lse_ref[...] = lse_all

    return _call(
        body, name="attn_fwd", grid=(batch, nblk),
        in_specs=[SMEM_SPEC] + _attn_specs(nblk),
        out_specs=[pl.BlockSpec((BLOCK, D_MODEL), lambda b, n: (b * nblk + n, 0)),
                   pl.BlockSpec((BLOCK, N_Q_HEADS), lambda b, n: (b * nblk + n, 0))],
        out_shape=[jax.ShapeDtypeStruct((T, D_MODEL), BF16), jax.ShapeDtypeStruct((T, N_Q_HEADS), F32)],
        operands=(sinks, proj, proj, proj, proj, proj), comm=comm)


def _conv_u(ca, cb):
    return ca.astype(F32) * _sigmoid(cb.astype(F32))


def _conv_specs(ts, tiles_per_seq):
    per_tile = ts // CONV_PAD

    def tile(col):
        return lambda b, t: (b * tiles_per_seq + t, col)

    def before(col):
        return lambda b, t: (jnp.maximum((b * tiles_per_seq + t) * per_tile - 1, 0), col)

    return [pl.BlockSpec((ts, D_MODEL), tile(COLB_CA)), pl.BlockSpec((ts, D_MODEL), tile(COLB_CB)),
            pl.BlockSpec((CONV_PAD, D_MODEL), before(COLB_CA)), pl.BlockSpec((CONV_PAD, D_MODEL), before(COLB_CB))]


SUBLANES = 8


def _fill_upad(upad, ca_ref, cb_ref, cah_ref, cbh_ref, t):
    halo = _conv_u(cah_ref[...], cbh_ref[...])
    upad[0, 0:CONV_PAD, :] = jnp.where(t > 0, halo, jnp.zeros_like(halo))
    upad[0, CONV_PAD:, :] = _conv_u(ca_ref[...], cb_ref[...])


def _fill_shifted(pad):
    rows = pad.shape[1] - SUBLANES
    for b in range(1, SUBLANES):
        pad[b, 0:rows, :] = pad[0, b:b + rows, :]


def _shifted_rows(pad, offset, rows):
    b = offset % SUBLANES
    return pad[b, offset - b:offset - b + rows, :]


def _layernorm_stats(y):
    mu = jnp.mean(y, axis=-1, keepdims=True)
    yc = y - mu
    rstd = lax.rsqrt(jnp.mean(yc * yc, axis=-1, keepdims=True) + EPS)
    return yc * rstd, rstd


def _conv_fwd(proj, w_dw, b_dw, ln_g, ln_b, batch, seq, comm=None):
    T = proj.shape[0]
    ts = _tile(seq, 256)
    nt = seq // ts
    shift = CONV_PAD - (CONV_WIDTH - 1)

    def body(ca_ref, cb_ref, cah_ref, cbh_ref, w_ref, b_ref, g_ref, beta_ref, y_ref, z_ref, upad):
        _fill_upad(upad, ca_ref, cb_ref, cah_ref, cbh_ref, pl.program_id(1))
        _fill_shifted(upad)
        y = jnp.zeros((ts, D_MODEL), F32) + b_ref[...]
        for k in range(CONV_WIDTH):
            y = y + w_ref[k:k + 1, :] * _shifted_rows(upad, shift + k, ts)
        y_ref[...] = y
        lnh, _ = _layernorm_stats(y)
        ln = lnh * g_ref[...] + beta_ref[...]
        z_ref[...] = (ln * _sigmoid(ln)).astype(BF16)

    vec = pl.BlockSpec((1, D_MODEL), lambda b, t: (0, 0))
    row = pl.BlockSpec((ts, D_MODEL), lambda b, t: (b * nt + t, 0))
    return _call(
        body, name="conv_fwd", grid=(batch, nt),
        in_specs=_conv_specs(ts, nt) + [pl.BlockSpec((CONV_PAD, D_MODEL), lambda b, t: (0, 0)), vec, vec, vec],
        out_specs=[row, row],
        out_shape=[jax.ShapeDtypeStruct((T, D_MODEL), F32), jax.ShapeDtypeStruct((T, D_MODEL), BF16)],
        scratch_shapes=[pltpu.VMEM((SUBLANES, ts + CONV_PAD, D_MODEL), F32)],
        operands=(proj, proj, proj, proj, w_dw, b_dw, ln_g, ln_b), comm=comm)


def _merge(o, z, proj, w_ao, w_co, w_out, x, gate, seq):
    T, D = x.shape
    tm = _tile(seq, 512)
    nb = seq // tm

    def body(o_ref, z_ref, ga_ref, gc_ref, wao_ref, wco_ref, wout_ref, x_ref, gate_ref,
             ya_ref, yc_ref, mg_ref, mo_ref, xo_ref):
        ya = _dot_nn(o_ref[...], wao_ref[...])
        yc = _dot_nn(z_ref[...], wco_ref[...])
        ya_ref[...] = ya.astype(BF16)
        yc_ref[...] = yc.astype(BF16)
        merged = (_sigmoid(ga_ref[...].astype(F32)) * ya + _sigmoid(gc_ref[...].astype(F32)) * yc).astype(BF16)
        mg_ref[...] = merged
        mo = _dot_nn(merged, wout_ref[...])
        mo_ref[...] = mo.astype(BF16)
        xo_ref[...] = x_ref[...] + gate_ref[...] * mo

    row = pl.BlockSpec((tm, D), lambda i: (i, 0))
    mat = pl.BlockSpec((D, D), lambda i: (0, 0))
    act = jax.ShapeDtypeStruct((T, D), BF16)
    return pl.pallas_call(
        body, name="mix_merge", grid=(T // tm,),
        in_specs=[row, row, pl.BlockSpec((tm, D), lambda i: (i, COLB_GA)), pl.BlockSpec((tm, D), lambda i: (i, COLB_GC)),
                  mat, mat, mat, row, pl.BlockSpec((None, 1, D), lambda i: (i // nb, 0, 0))],
        out_specs=[row, row, row, row, row],
        out_shape=[act, act, act, act, jax.ShapeDtypeStruct((T, D), F32)],
        compiler_params=_params(1),
    )(o, z, proj, proj, w_ao, w_co, w_out, x, gate)


def _final_loss(x, gf, target):
    T, D = x.shape
    tm = _tile(T, 512)

    def body(x_ref, gf_ref, t_ref, dx_ref, lp_ref, dgf_ref):
        first = pl.program_id(0) == 0
        xv = x_ref[...]
        gfv = gf_ref[...]
        r = lax.rsqrt(jnp.mean(xv * xv, axis=-1, keepdims=True) + EPS)
        xh = xv * r
        err = xh * gfv - t_ref[...]
        _accumulate(lp_ref, first, jnp.sum(err * err, axis=0, keepdims=True))
        dy = err * (1.0 / D)
        _accumulate(dgf_ref, first, jnp.sum(dy * xh, axis=0, keepdims=True))
        dxh = dy * gfv
        dx_ref[...] = r * (dxh - xh * jnp.mean(dxh * xh, axis=-1, keepdims=True))

    row = pl.BlockSpec((tm, D), lambda i: (i, 0))
    vec = pl.BlockSpec((1, D), lambda i: (0, 0))
    return pl.pallas_call(
        body, name="final_loss", grid=(T // tm,),
        in_specs=[row, vec, row], out_specs=[row, vec, vec],
        out_shape=[jax.ShapeDtypeStruct((T, D), F32), jax.ShapeDtypeStruct((1, D), F32),
                   jax.ShapeDtypeStruct((1, D), F32)],
        compiler_params=_params(1),
    )(x, gf, target)


def _merge_bwd(dxo, mo, gate, proj, ya, yc, w_out, w_ao, w_co, seq, comm=None):
    T, D = dxo.shape
    B = T // seq
    tm = _tile(seq, 512)
    nb = seq // tm

    def body(dxo_ref, mo_ref, gate_ref, ga_ref, gc_ref, ya_ref, yc_ref, wout_ref, wao_ref, wco_ref,
             dmo_ref, dya_ref, dyc_ref, dga_ref, dgc_ref, do_ref, dz_ref, dgate_ref):
        dxo_v = dxo_ref[...]
        dmo = (gate_ref[...] * dxo_v).astype(BF16)
        dmo_ref[...] = dmo
        _accumulate(dgate_ref, pl.program_id(0) % nb == 0,
                    jnp.sum(mo_ref[...].astype(F32) * dxo_v, axis=0, keepdims=True))
        dm = _dot_nt(dmo, wout_ref[...])
        sa = _sigmoid(ga_ref[...].astype(F32))
        sc = _sigmoid(gc_ref[...].astype(F32))
        dya = (sa * dm).astype(BF16)
        dyc = (sc * dm).astype(BF16)
        dya_ref[...] = dya
        dyc_ref[...] = dyc
        dga_ref[...] = (dm * ya_ref[...].astype(F32) * (sa * (1.0 - sa))).astype(BF16)
        dgc_ref[...] = (dm * yc_ref[...].astype(F32) * (sc * (1.0 - sc))).astype(BF16)
        do_ref[...] = _dot_nt(dya, wao_ref[...]).astype(BF16)
        dz_ref[...] = _dot_nt(dyc, wco_ref[...]).astype(BF16)

    row = pl.BlockSpec((tm, D), lambda i: (i, 0))
    mat = pl.BlockSpec((D, D), lambda i: (0, 0))
    per_b = pl.BlockSpec((None, 1, D), lambda i: (i // nb, 0, 0))
    act = jax.ShapeDtypeStruct((T, D), BF16)
    return _call(
        body, name="mix_merge_bwd", grid=(T // tm,),
        in_specs=[row, row, per_b, pl.BlockSpec((tm, D), lambda i: (i, COLB_GA)),
                  pl.BlockSpec((tm, D), lambda i: (i, COLB_GC)), row, row, mat, mat, mat],
        out_specs=[row] * 7 + [per_b],
        out_shape=[act] * 7 + [jax.ShapeDtypeStruct((B, 1, D), F32)],
        operands=(dxo, mo, gate, proj, proj, ya, yc, w_out, w_ao, w_co), comm=comm)


def _attn_bwd(proj, sinks, o, do, lse, batch, seq, comm=None):
    T = proj.shape[0]
    nblk = seq // BLOCK
    n_steps = batch * nblk

    def body(sink_ref, q_ref, kp_ref, ko_ref, vp_ref, vo_ref, o_ref, do_ref, lse_ref,
             dq_ref, dkp_ref, dko_ref, dvp_ref, dvo_ref, dsink_ref):
        lo = lax.broadcasted_iota(jnp.int32, (1, 2 * HEAD_DIM), 1) < HEAD_DIM
        sink_lane = lax.broadcasted_iota(jnp.int32, (1, 2 * HEAD_DIM), 1)
        valid = _band_valid(pl.program_id(1) > 0)
        k_ops = _band_operands(kp_ref, ko_ref, lo)
        v_ops = _band_operands(vp_ref, vo_ref, lo)
        dsink = jnp.zeros((1, 2 * HEAD_DIM), F32)
        col = jnp.zeros((BLOCK, 1), F32)

        def fold(both):
            return (jnp.where(lo, both[:BAND], 0.0)
                    + pltpu.roll(jnp.where(lo, 0.0, both[BAND:]), HEAD_DIM, 1))

        dk_heads, dv_heads = [], []
        for kvh in range(N_KV_HEADS):
            q4 = _stack_pairs(q_ref, kvh)
            do4 = _stack_pairs(do_ref, kvh)
            dd = do4.astype(F32) * _stack_pairs(o_ref, kvh).astype(F32)
            s_all = _dot_nt(q4, k_ops[kvh]) * ATTN_SCALE
            dp_all = _dot_nt(do4, v_ops[kvh])
            ds_sides, p_sides = [], []
            for side in range(2):
                heads = [2 * (kvh * PAIRS_PER_KV + pp) + side for pp in range(PAIRS_PER_KV)]
                mine = lo if side == 0 else jnp.logical_not(lo)
                cols = slice(side * BAND, (side + 1) * BAND)
                sink = _rows_per_pair([col + sink_ref[0, h] for h in heads])
                lse = _rows_per_pair([lse_ref[:, h:h + 1] for h in heads])
                delta = jnp.sum(jnp.where(mine, dd, 0.0), axis=-1, keepdims=True)
                p = jnp.where(valid, jnp.exp(jnp.where(valid, s_all[:, cols], MASK_VALUE) - lse), 0.0)
                ds_sides.append((p * (dp_all[:, cols] - delta) * ATTN_SCALE).astype(BF16))
                p_sides.append(p.astype(BF16))
                sink_part = jnp.exp(sink - lse) * delta
                for pp, h in enumerate(heads):
                    dsink = dsink + jnp.where(sink_lane == h, -jnp.sum(sink_part[pp * BLOCK:(pp + 1) * BLOCK]), 0.0)
            ds_all = jnp.concatenate(ds_sides, axis=1)
            dq4 = _dot_nn(ds_all, k_ops[kvh])
            for pp in range(PAIRS_PER_KV):
                dq_ref[:, _pair_lanes(kvh, pp)] = dq4[pp * BLOCK:(pp + 1) * BLOCK].astype(BF16)
            dk_heads.append(fold(_dot_tn(ds_all, q4)))
            dv_heads.append(fold(_dot_tn(jnp.concatenate(p_sides, axis=1), do4)))
        dk = dk_heads[0] + pltpu.roll(dk_heads[1], HEAD_DIM, 1)
        dv = dv_heads[0] + pltpu.roll(dv_heads[1], HEAD_DIM, 1)
        dkp_ref[...] = dk[:BLOCK]
        dko_ref[...] = dk[BLOCK:]
        dvp_ref[...] = dv[:BLOCK]
        dvo_ref[...] = dv[BLOCK:]
        dsink_ref[...] = dsink

    def own(b, n):
        return (b * nblk + n, 0)

    row = pl.BlockSpec((BLOCK, D_MODEL), own)
    kv = pl.BlockSpec((BLOCK, 2 * HEAD_DIM), own)
    kv_shape = jax.ShapeDtypeStruct((T, 2 * HEAD_DIM), F32)
    return _call(
        body, name="attn_bwd", grid=(batch, nblk),
        in_specs=[SMEM_SPEC] + _attn_specs(nblk) + [row, row, pl.BlockSpec((BLOCK, N_Q_HEADS), own)],
        out_specs=[row, kv, kv, kv, kv, pl.BlockSpec((None, 1, 2 * HEAD_DIM), lambda b, n: (b * nblk + n, 0, 0))],
        out_shape=[jax.ShapeDtypeStruct((T, D_MODEL), BF16), kv_shape, kv_shape, kv_shape, kv_shape,
                   jax.ShapeDtypeStruct((n_steps, 1, 2 * HEAD_DIM), F32)],
        operands=(sinks, proj, proj, proj, proj, proj, o, do, lse), comm=comm)


def _conv_bwd(proj, dz, ydw, w_dw, ln_g, ln_b, batch, seq, comm=None):
    T = proj.shape[0]
    ts = _tile(seq, 256)
    nt = seq // ts
    per_tile = ts // CONV_PAD
    shift = CONV_PAD - (CONV_WIDTH - 1)

    def body(ca_ref, cb_ref, cah_ref, cbh_ref, dz_ref, dzn_ref, y_ref, yn_ref, w_ref, g_ref, beta_ref,
             dca_ref, dcb_ref, dw_ref, db_ref, dg_ref, dbeta_ref, upad, dypad):
        t = pl.program_id(1)
        first = (pl.program_id(0) == 0) & (t == 0)
        gv = g_ref[...]

        def ln_bwd(dzv, yv):
            lnh, rstd = _layernorm_stats(yv)
            ln = lnh * gv + beta_ref[...]
            sg = _sigmoid(ln)
            dln = dzv.astype(F32) * (sg * (1.0 + ln * (1.0 - sg)))
            dyh = dln * gv
            dy = rstd * (dyh - jnp.mean(dyh, axis=-1, keepdims=True)
                         - lnh * jnp.mean(dyh * lnh, axis=-1, keepdims=True))
            return dy, dln, lnh

        dy, dln, lnh = ln_bwd(dz_ref[...], y_ref[...])
        dy_next, _, _ = ln_bwd(dzn_ref[...], yn_ref[...])
        dypad[0, 0:ts, :] = dy
        dypad[0, ts:, :] = jnp.where(t < nt - 1, dy_next, jnp.zeros_like(dy_next))
        _fill_shifted(dypad)
        _fill_upad(upad, ca_ref, cb_ref, cah_ref, cbh_ref, t)
        _fill_shifted(upad)

        _accumulate(dg_ref, first, jnp.sum(dln * lnh, axis=0, keepdims=True))
        _accumulate(dbeta_ref, first, jnp.sum(dln, axis=0, keepdims=True))
        _accumulate(db_ref, first, jnp.sum(dy, axis=0, keepdims=True))

        @pl.when(first)
        def _():
            dw_ref[...] = jnp.zeros_like(dw_ref)

        du = jnp.zeros((ts, D_MODEL), F32)
        for k in range(CONV_WIDTH):
            du = du + w_ref[k:k + 1, :] * _shifted_rows(dypad, CONV_WIDTH - 1 - k, ts)
            dw_ref[k:k + 1, :] += jnp.sum(dy * _shifted_rows(upad, shift + k, ts), axis=0, keepdims=True)
        cav = ca_ref[...].astype(F32)
        sb = _sigmoid(cb_ref[...].astype(F32))
        dca_ref[...] = (du * sb).astype(BF16)
        dcb_ref[...] = (du * cav * (sb * (1.0 - sb))).astype(BF16)

    def tile(b, t):
        return (b * nt + t, 0)

    def after(b, t):
        return (jnp.minimum((b * nt + t + 1) * per_tile, T // CONV_PAD - 1), 0)

    row = pl.BlockSpec((ts, D_MODEL), tile)
    halo = pl.BlockSpec((CONV_PAD, D_MODEL), after)
    vec = pl.BlockSpec((1, D_MODEL), lambda b, t: (0, 0))
    wspec = pl.BlockSpec((CONV_PAD, D_MODEL), lambda b, t: (0, 0))
    act = jax.ShapeDtypeStruct((T, D_MODEL), BF16)
    vec_shape = jax.ShapeDtypeStruct((1, D_MODEL), F32)
    return _call(
        body, name="conv_bwd", grid=(batch, nt),
        in_specs=_conv_specs(ts, nt) + [row, halo, row, halo, wspec, vec, vec],
        out_specs=[row, row, wspec, vec, vec, vec],
        out_shape=[act, act, jax.ShapeDtypeStruct((CONV_PAD, D_MODEL), F32), vec_shape, vec_shape, vec_shape],
        scratch_shapes=[pltpu.VMEM((SUBLANES, ts + CONV_PAD, D_MODEL), F32)] * 2,
        operands=(proj, proj, proj, proj, dz, dz, ydw, ydw, w_dw, ln_g, ln_b), comm=comm)


def _in_proj_bwd(pieces, w_cols, x, gn, sc, dxo, seq, comm=None):
    T, D = x.shape
    B = T // seq
    wide, narrow = list(pieces[:-1]), pieces[-1]
    P = len(wide)
    nw = narrow.shape[1]
    tm = _tile(seq, 512)
    nb = seq // tm

    def body(*refs):
        wide_refs = refs[:P]
        kv_ref, w_ref, wkv_ref, x_ref, gn_ref, sc_ref, dxo_ref, dx_ref, dsc_ref, dsh_ref, dgn_ref, acc = refs[P:]
        i = pl.program_id(0)
        j = pl.program_id(1)

        @pl.when(j == 0)
        def _():
            acc[...] = _dot_nn(kv_ref[...], wkv_ref[...])

        for p in range(P):
            @pl.when(j == p)
            def _(p=p):
                acc[...] += _dot_nn(wide_refs[p][...], w_ref[...])

        @pl.when(j == P - 1)
        def _():
            _norm_mod_bwd(acc[...], x_ref[...], gn_ref[...], sc_ref[...], dxo_ref[...],
                          i % nb == 0, i == 0, dx_ref, dsc_ref, dsh_ref, dgn_ref)

    row = pl.BlockSpec((tm, D), lambda i, j: (i, 0))
    vec = pl.BlockSpec((1, D), lambda i, j: (0, 0))
    per_b = pl.BlockSpec((None, 1, D), lambda i, j: (i // nb, 0, 0))
    per_b_shape = jax.ShapeDtypeStruct((B, 1, D), F32)
    return _call(
        body, name="mix_in_proj_bwd", grid=(T // tm, P),
        in_specs=[row] * P + [pl.BlockSpec((tm, nw), lambda i, j: (i, 0)),
                              pl.BlockSpec((D, D), lambda i, j: (j, 0)),
                              pl.BlockSpec((nw, D), lambda i, j: (P * D // nw, 0)), row, vec, per_b, row],
        out_specs=[row, per_b, per_b, vec],
        out_shape=[jax.ShapeDtypeStruct((T, D), F32), per_b_shape, per_b_shape, jax.ShapeDtypeStruct((1, D), F32)],
        scratch_shapes=[pltpu.VMEM((tm, D), F32)],
        operands=(*wide, narrow, w_cols, w_cols, x, gn, sc, dxo), comm=comm)


def _wgrad_rows(piece, h, out32, out16, row_offset, name):
    T, n = piece.shape
    C = h.shape[1]
    tk = _tile(T, 1024)
    nk = T // tk

    def body(a_ref, b_ref, in32, in16, o32_ref, o16_ref, acc, stage16, sems):
        k = pl.program_id(0)

        @pl.when(k == 0)
        def _():
            acc[...] = jnp.zeros_like(acc)

        acc[...] += _dot_tn(a_ref[...], b_ref[...])

        @pl.when(k == nk - 1)
        def _():
            stage16[...] = acc[...].astype(BF16)
            rows = pl.ds(row_offset, n)
            copies = [pltpu.make_async_copy(acc, o32_ref.at[rows, :], sems.at[0]),
                      pltpu.make_async_copy(stage16, o16_ref.at[rows, :], sems.at[1])]
            for cp in copies:
                cp.start()
            for cp in copies:
                cp.wait()

    return pl.pallas_call(
        body, name=name, grid=(nk,),
        in_specs=[pl.BlockSpec((tk, n), lambda k: (k, 0)), pl.BlockSpec((tk, C), lambda k: (k, 0)), ANY, ANY],
        out_specs=[ANY, ANY], out_shape=[jax.ShapeDtypeStruct(out32.shape, F32), jax.ShapeDtypeStruct(out16.shape, BF16)],
        scratch_shapes=[pltpu.VMEM((n, C), F32), pltpu.VMEM((n, C), BF16), pltpu.SemaphoreType.DMA((2,))],
        input_output_aliases={2: 0, 3: 1}, compiler_params=_params(1),
    )(piece, h, out32, out16)


def _ada_fwd(c_all, w_ada, b_cols):
    nbatch, D = c_all.shape
    N = w_ada.shape[1]
    tn = _tile(N, 768)

    def body(c_ref, w_ref, b_ref, o_ref):
        cv = c_ref[...]
        act = (cv * _sigmoid(cv)).astype(BF16)
        o_ref[...] = _dot_nn(act, w_ref[...].astype(BF16)) + b_ref[...]

    return pl.pallas_call(
        body, name="ada_fwd", grid=(N // tn,),
        in_specs=[pl.BlockSpec((nbatch, D), lambda j: (0, 0)), pl.BlockSpec((D, tn), lambda j: (0, j)),
                  pl.BlockSpec((1, tn), lambda j: (0, j))],
        out_specs=pl.BlockSpec((nbatch, tn), lambda j: (0, j)),
        out_shape=jax.ShapeDtypeStruct((nbatch, N), F32),
        compiler_params=_params(1),
    )(c_all, w_ada, b_cols)


def _adamw(w, g, m, v):
    m = ADAM_B1 * m + (1.0 - ADAM_B1) * g
    v = ADAM_B2 * v + (1.0 - ADAM_B2) * (g * g)
    m_hat = m / (1.0 - ADAM_B1 ** ADAM_STEP)
    v_hat = v / (1.0 - ADAM_B2 ** ADAM_STEP)
    delta = -ADAM_LR * (m_hat / (jnp.sqrt(v_hat) + ADAM_EPS) + ADAM_WD * w)
    return delta, m, v


def _adam_call(w, g, m, v, name, comm=None):
    R, C = w.shape
    tr = _row_tile(R, 512)

    def body(w_ref, g_ref, m_ref, v_ref, d_ref, mo_ref, vo_ref):
        d, mn, vn = _adamw(w_ref[...], g_ref[...], m_ref[...], v_ref[...])
        d_ref[...] = d
        mo_ref[...] = mn
        vo_ref[...] = vn

    blk = pl.BlockSpec((tr, C), lambda i: (i, 0))
    shape = jax.ShapeDtypeStruct((R, C), F32)
    return _call(body, name=name, grid=(R // tr,), in_specs=[blk] * 4, out_specs=[blk] * 3, out_shape=[shape] * 3,
                 operands=(w, g, m, v), comm=comm)


ADAM_GROUP_STEPS = 8


def _adam_group(ws, gs, ms, vs, name, comm=None):
    n = len(ws)

    def body(*refs):
        ins, outs = refs[:4 * n], refs[4 * n:]
        for i in range(n):
            d, mn, vn = _adamw(*(r[...] for r in ins[4 * i:4 * i + 4]))
            outs[3 * i][...] = d
            outs[3 * i + 1][...] = mn
            outs[3 * i + 2][...] = vn

    operands, in_specs, out_specs, out_shape = [], [], [], []
    for w, g, m, v in zip(ws, gs, ms, vs):
        R, C = w.shape
        blk = pl.BlockSpec((R // ADAM_GROUP_STEPS, C), lambda i: (i, 0))
        operands += [w, g, m, v]
        in_specs += [blk] * 4
        out_specs += [blk] * 3
        out_shape += [jax.ShapeDtypeStruct((R, C), F32)] * 3
    outs, comm_outs = _call(body, name=name, grid=(ADAM_GROUP_STEPS,), in_specs=in_specs, out_specs=out_specs,
                            out_shape=out_shape, operands=operands, comm=comm)
    return [tuple(outs[3 * i:3 * i + 3]) for i in range(n)], comm_outs


def _ada_adam(c_act_t, dmod_cols, w, m, v, comm):
    R, C = w.shape
    nbatch = c_act_t.shape[1]
    tr = _tile(R, 128)

    def body(ct_ref, dm_ref, w_ref, m_ref, v_ref, g_ref, d_ref, mo_ref, vo_ref):
        cv = ct_ref[...]
        g = _dot_nn((cv * _sigmoid(cv)).astype(BF16), dm_ref[...].astype(BF16))
        g_ref[...] = g
        d, mn, vn = _adamw(w_ref[...], g, m_ref[...], v_ref[...])
        d_ref[...] = d
        mo_ref[...] = mn
        vo_ref[...] = vn

    blk = pl.BlockSpec((tr, C), lambda i: (i, 0))
    shape = jax.ShapeDtypeStruct((R, C), F32)
    return _call(
        body, name="ada_adam", grid=(R // tr,),
        in_specs=[pl.BlockSpec((tr, nbatch), lambda i: (i, 0)), pl.BlockSpec((nbatch, C), lambda i: (0, 0)),
                  blk, blk, blk],
        out_specs=[blk] * 4, out_shape=[shape] * 4,
        operands=(c_act_t, dmod_cols, w, m, v), comm=comm)


def _small_adam(gathered, w, m, v, rows_b0, rows_b1, rows_vec):
    _, P, D = gathered.shape
    R = w.shape[0]

    def body(ga_ref, w_ref, m_ref, v_ref, sum_ref, g_ref, d_ref, mo_ref, vo_ref):
        total = ga_ref[0]
        for dev in range(1, N_DEV):
            total = total + ga_ref[dev]
        sum_ref[...] = total
        g_ref[...] = jnp.zeros_like(g_ref)
        g_ref[0:N_MOD, :] = (sum_ref[rows_b0:rows_b0 + N_MOD, :] + sum_ref[rows_b1:rows_b1 + N_MOD, :])
        g_ref[N_MOD:N_MOD + 8, :] = sum_ref[rows_vec:rows_vec + 8, :]
        d, mn, vn = _adamw(w_ref[...], g_ref[...], m_ref[...], v_ref[...])
        d_ref[...] = d
        mo_ref[...] = mn
        vo_ref[...] = vn

    shape = jax.ShapeDtypeStruct((R, D), F32)
    return pl.pallas_call(
        body, name="small_adam",
        in_specs=[VMEM_SPEC] * 4, out_specs=[VMEM_SPEC] * 5,
        out_shape=[jax.ShapeDtypeStruct((P, D), F32), shape, shape, shape, shape],
        compiler_params=pltpu.CompilerParams(vmem_limit_bytes=VMEM_LIMIT),
    )(gathered, w, m, v)


def _gather8(v, name):
    A, W = v.shape
    flips = [(fx, fy, fc) for fx in (0, 1) for fy in (0, 1) for fc in (0, 1) if (fx, fy, fc) != (0, 0, 0)]

    def body(v_ref, out_ref, send_sems, recv_sems, local_sem):
        x, y, c = _position()
        me = 4 * x + 2 * y + c
        mine = pltpu.make_async_copy(v_ref, out_ref.at[me], local_sem)
        mine.start()

        def copy(k, block, to):
            return pltpu.make_async_remote_copy(src_ref=v_ref, dst_ref=out_ref.at[block], send_sem=send_sems.at[k],
                                                recv_sem=recv_sems.at[k], device_id=to, device_id_type=MESH)

        peers = [(_flip(x, fx), _flip(y, fy), _flip(c, fc)) for fx, fy, fc in flips]
        sends = [copy(k, me, peer) for k, peer in enumerate(peers)]
        for cp in sends:
            cp.start()
        for k, (px, py, pc) in enumerate(peers):
            copy(k, 4 * px + 2 * py + pc, (px, py, pc)).wait_recv()
        for cp in sends:
            cp.wait_send()
        mine.wait()

    return pl.pallas_call(
        body, name=name, in_specs=[VMEM_SPEC], out_specs=VMEM_SPEC,
        out_shape=jax.ShapeDtypeStruct((N_DEV, A, W), v.dtype),
        scratch_shapes=[pltpu.SemaphoreType.DMA((N_DEV - 1,)), pltpu.SemaphoreType.DMA((N_DEV - 1,)),
                        pltpu.SemaphoreType.DMA],
    )(v)


def _mod_exchange(part):
    _, A, W = part.shape

    def body(p_ref, out_ref, send_sems, recv_sems, local_sem):
        x, y, c = _position()
        me = 4 * x + 2 * y + c
        chip = 2 * x + y
        mine = pltpu.make_async_copy(p_ref.at[me], out_ref.at[chip], local_sem)
        mine.start()
        peers = [(_flip(x, fx), _flip(y, fy)) for fx, fy in CHIP_FLIPS]
        sends = []
        for k, (px, py) in enumerate(peers):
            sends.append(pltpu.make_async_remote_copy(
                src_ref=p_ref.at[4 * px + 2 * py + c], dst_ref=out_ref.at[chip], send_sem=send_sems.at[k],
                recv_sem=recv_sems.at[k], device_id=(px, py, c), device_id_type=MESH))
        for cp in sends:
            cp.start()
        for k, (px, py) in enumerate(peers):
            pltpu.make_async_remote_copy(
                src_ref=p_ref.at[me], dst_ref=out_ref.at[2 * px + py], send_sem=send_sems.at[k],
                recv_sem=recv_sems.at[k], device_id=(px, py, c), device_id_type=MESH).wait_recv()
        for cp in sends:
            cp.wait_send()
        mine.wait()

    return pl.pallas_call(
        body, name="mod_exchange", in_specs=[VMEM_SPEC], out_specs=VMEM_SPEC,
        out_shape=jax.ShapeDtypeStruct((N_CHIP, A, W), part.dtype),
        scratch_shapes=[pltpu.SemaphoreType.DMA((3,)), pltpu.SemaphoreType.DMA((3,)), pltpu.SemaphoreType.DMA],
    )(part)


KV_ROWS = 4 * HEAD_DIM


def _kernel_row_order(w_in_t):
    R, C = w_in_t.shape
    n_blocks = R // KV_ROWS
    q_blocks = D_MODEL // KV_ROWS

    def source(t):
        return jnp.where(t < q_blocks, t, jnp.where(t < n_blocks - 1, t + 1, q_blocks))

    def body(w_ref, o_ref):
        o_ref[...] = w_ref[...]

    return pl.pallas_call(
        body, name="w_in_row_order", grid=(n_blocks,),
        in_specs=[pl.BlockSpec((KV_ROWS, C), lambda t: (source(t), 0))],
        out_specs=pl.BlockSpec((KV_ROWS, C), lambda t: (t, 0)),
        out_shape=jax.ShapeDtypeStruct((R, C), w_in_t.dtype), compiler_params=_params(1),
    )(w_in_t)


def _cast_group(ws, names, comm):
    n = len(ws)
    steps = 4

    def body(*refs):
        w_refs, out_refs, stage, sem = refs[:n], refs[n:2 * n], refs[2 * n:3 * n], refs[3 * n]
        x, y, _ = _position()
        step = pl.program_id(0)
        copies = []
        for i in range(n):
            rows = ws[i].shape[0] // steps
            stage[i][...] = w_refs[i][...].astype(BF16)
            copies.append(pltpu.make_async_copy(
                stage[i], out_refs[i].at[2 * x + y, pl.ds(step * rows, rows), :], sem.at[i]))
        for cp in copies:
            cp.start()
        for cp in copies:
            cp.wait()

    outs, comm_outs = _call(
        body, name="cast_" + "_".join(names), grid=(steps,),
        in_specs=[pl.BlockSpec((w.shape[0] // steps, w.shape[1]), lambda i: (i, 0)) for w in ws],
        out_specs=[ANY] * n, out_shape=[jax.ShapeDtypeStruct((N_CHIP,) + w.shape, BF16) for w in ws],
        scratch_shapes=[pltpu.VMEM((w.shape[0] // steps, w.shape[1]), BF16) for w in ws]
        + [pltpu.SemaphoreType.DMA((n,))],
        operands=ws, comm=comm)
    return outs, comm_outs


def _cast_slot(w, chip_idx, name):
    R, C = w.shape
    tr = _row_tile(R, 512)

    def body(chip_ref, w_ref, o_ref):
        o_ref[...] = w_ref[...].astype(BF16)

    return pl.pallas_call(
        body, name=name,
        grid_spec=pltpu.PrefetchScalarGridSpec(
            num_scalar_prefetch=1, grid=(R // tr,),
            in_specs=[pl.BlockSpec((tr, C), lambda i, chip_ref: (i, 0))],
            out_specs=pl.BlockSpec((None, tr, C), lambda i, chip_ref: (chip_ref[0], i, 0))),
        out_shape=jax.ShapeDtypeStruct((N_CHIP, R, C), BF16),
        compiler_params=_params(1),
    )(chip_idx, w)


def _pair_sum(g32, recv, core, name):
    J, _, r, C = g32.shape

    def body(core_ref, g_ref, r_ref, o_ref):
        o_ref[...] = (g_ref[...] + r_ref[...].astype(F32)).astype(BF16)

    return pl.pallas_call(
        body, name=name,
        grid_spec=pltpu.PrefetchScalarGridSpec(
            num_scalar_prefetch=1, grid=(J,),
            in_specs=[pl.BlockSpec((None, None, r, C), lambda j, core_ref: (j, core_ref[0], 0, 0)),
                      pl.BlockSpec((None, r, C), lambda j, core_ref: (j, 0, 0))],
            out_specs=pl.BlockSpec((None, r, C), lambda j, core_ref: (j, 0, 0))),
        out_shape=jax.ShapeDtypeStruct((J, r, C), BF16),
        compiler_params=_params(1),
    )(core, g32, recv)


def _chip_sum(g32, recv_sib, recv_chips, core_chip, name):
    J, _, r, C = g32.shape

    def body(idx_ref, g_ref, s_ref, o_ref_in, o_ref):
        total = g_ref[...] + s_ref[...].astype(F32)
        for k in range(3):
            total = total + o_ref_in[k].astype(F32)
        o_ref[...] = total

    return pl.pallas_call(
        body, name=name,
        grid_spec=pltpu.PrefetchScalarGridSpec(
            num_scalar_prefetch=1, grid=(1,),
            in_specs=[pl.BlockSpec((None, None, r, C), lambda i, idx: (idx[1], idx[0], 0, 0)),
                      pl.BlockSpec((None, r, C), lambda i, idx: (idx[1], 0, 0)),
                      pl.BlockSpec((3, r, C), lambda i, idx: (0, 0, 0))],
            out_specs=pl.BlockSpec((None, r, C), lambda i, idx: (idx[0], 0, 0))),
        out_shape=jax.ShapeDtypeStruct((2, r, C), F32),
        compiler_params=_params(1),
    )(core_chip, g32, recv_sib, recv_chips)


ICI_US_PER_ELEMENT = 4.6e-5


class _Reducer:
    def __init__(self, core_idx, core_chip):
        self.core_idx, self.core_chip = core_idx, core_chip
        self.grads, self.halves, self.reduced = {}, {}, {}
        self.ready_swap, self.ready_exchange, self.ready_join = [], [], []
        self.inflight, self.current = ([], [], [], None), None
        self.flushes = 0
        self.extra, self.extra_out = None, None

    def add(self, name, grad_pair):
        self.grads[name] = grad_pair
        self.ready_swap.append(name)

    def comm(self, budget_us):
        swaps, self.ready_swap = self.ready_swap, []
        joins, self.ready_join = self.ready_join, []
        exchanges, waiting = [], []
        for item in self.ready_exchange:
            cost = ICI_US_PER_ELEMENT * 2 * item[2].shape[1] * item[2].shape[2]
            if cost <= budget_us:
                exchanges.append(item)
                budget_us -= cost
            else:
                waiting.append(item)
        self.ready_exchange = waiting
        parts = []
        if swaps:
            parts.append(_SwapComm([self.grads[n][1] for n in swaps]))
        if exchanges:
            parts.append(_ExchangeComm([pair for _, _, pair in exchanges]))
        if joins:
            parts.append(_JoinComm([self.halves[n] for n in joins]))
        extra, self.extra = self.extra, None
        if extra is not None:
            parts.append(extra)
        self.inflight = (swaps, exchanges, joins, extra)
        self.current = _CommList(parts) if parts else None
        return self.current

    def done(self, comm_outs):
        if self.current is None:
            return
        swaps, exchanges, joins, extra = self.inflight
        outs = iter(self.current.split_outputs(list(comm_outs)))
        if swaps:
            for n, recv in zip(swaps, next(outs)):
                pair = _pair_sum(self.grads[n][0], recv, self.core_idx, "pair_sum_" + n)
                self.ready_exchange.append((n, recv, pair))
        if exchanges:
            for (n, recv, _), chips in zip(exchanges, next(outs)):
                self.halves[n] = _chip_sum(self.grads[n][0], recv, chips, self.core_chip, "chip_sum_" + n)
                self.ready_join.append(n)
        if joins:
            self.reduced.update(zip(joins, next(outs)))
        if extra is not None:
            self.extra_out = next(outs)
        self.current = None

    def run(self, kernel, budget_us, *args, **kwargs):
        if budget_us is None:
            return kernel(*args, comm=None, **kwargs)[0]
        outs, comm_outs = kernel(*args, comm=self.comm(budget_us), **kwargs)
        self.done(comm_outs)
        return outs

    def step(self):
        comm = self.comm(float("inf"))
        self.flushes += 1
        self.done(_run_comm(comm, "grad_reduce_tail_%d" % self.flushes))


BIG_WEIGHTS = ("ffn1_w_gate", "ffn1_w_up", "ffn1_w_down", "w_in", "w_attn_o", "w_conv_o", "w_out",
               "ffn2_w_gate", "ffn2_w_up", "ffn2_w_down")
VECTORS = ("norm_ffn1_g", "norm_mix_g", "conv_b_dw", "conv_ln_g", "conv_ln_b", "norm_ffn2_g", "final_norm_g")
ROW_DMOD0, ROW_DMOD1, ROW_VEC, ROW_SINK, ROW_CONVW, SMALL_ROWS = 0, 16, 33, 40, 41, 72


FFN1_WEIGHTS = ("ffn1_w_gate", "ffn1_w_up", "ffn1_w_down")
FFN2_WEIGHTS = ("ffn2_w_gate", "ffn2_w_up", "ffn2_w_down")
MIX_WEIGHTS = ("w_in", "w_attn_o", "w_conv_o", "w_out")
COL_SHARDED = ("ffn1_w_gate", "ffn1_w_up", "ffn2_w_gate", "ffn2_w_up", "w_in")


def _local_grads(x, target, mod, slots, ffn1_gathered, small, seq, core_idx, core_chip):
    T, D = x.shape
    B = T // seq
    mods = [mod[:, k][:, None, :] for k in range(N_MOD)]
    sh1, sc1, g1, sh2, sc2, g2, sh3, sc3, g3 = mods
    w = dict(zip(FFN1_WEIGHTS, ffn1_gathered))

    (h1, a1, u1, f1, x1), outs = _ffn_fwd(
        x, small["norm_ffn1_g"], sc1, sh1, g1, w["ffn1_w_gate"], w["ffn1_w_up"], w["ffn1_w_down"], seq, "ffn1_fwd",
        comm=_GatherComm([slots[n] for n in MIX_WEIGHTS]))
    w["w_in"] = outs[0]
    w_ao, w_co, w_o = [t.reshape(D, D) for t in outs[1:]]
    w_in_cols = _kernel_row_order(w["w_in"].reshape(IN_WIDTH, D))
    (h2, proj), _ = _in_proj(x1, small["norm_mix_g"], sc2, sh2, w_in_cols, seq)
    (o, lse), (w["ffn2_w_gate"], w["ffn2_w_up"]) = _attn_fwd(
        proj, small["attn_sinks"], B, seq, comm=_GatherComm([slots["ffn2_w_gate"], slots["ffn2_w_up"]]))
    (ydw, z), (w["ffn2_w_down"],) = _conv_fwd(
        proj, small["conv_w_dw"], small["conv_b_dw"], small["conv_ln_g"], small["conv_ln_b"], B, seq,
        comm=_GatherComm([slots["ffn2_w_down"]]))
    ya, yc, merged, mo, x2 = _merge(o, z, proj, w_ao, w_co, w_o, x1, g2, seq)
    (h3, a3, u3, f3, x3), _ = _ffn_fwd(x2, small["norm_ffn2_g"], sc3, sh3, g3, w["ffn2_w_gate"], w["ffn2_w_up"],
                                       w["ffn2_w_down"], seq, "ffn2_fwd")
    dx3, loss_parts, d_final_g = _final_loss(x3, small["final_norm_g"], target)

    red = _Reducer(core_idx, core_chip)

    def weight_grad(name, budget_us, a, a_spec, b, b_spec, rows, cols):
        red.add(name, red.run(_wgrad, budget_us, a, a_spec, b, b_spec, rows, cols, T, "dw_" + name))

    def ffn_backward(prefix, dw_budget_us, dxo, xin, h, a, u, f, gn, sc, gate, before_weight_grads=None):
        da, du, s, df, dx, dgate, dsc, dsh, dgn = red.run(
            _ffn_bwd, 170, dxo, xin, f, a, u, gn, sc, gate, w[prefix + "_w_gate"], w[prefix + "_w_up"],
            w[prefix + "_w_down"], seq, prefix + "_bwd")
        if before_weight_grads is not None:
            before_weight_grads(dgate, dsc, dsh, dgn)
        weight_grad(prefix + "_w_down", dw_budget_us, s, _spec_chip_major(FF_SHARD), df, _spec_rows(D), FF_SHARD, D)
        weight_grad(prefix + "_w_gate", dw_budget_us, da, _spec_chip_major(FF_SHARD), h, _spec_rows(D), FF_SHARD, D)
        weight_grad(prefix + "_w_up", dw_budget_us, du, _spec_chip_major(FF_SHARD), h, _spec_rows(D), FF_SHARD, D)
        return dx, dgate, dsc, dsh, dgn

    dx2, dg3, dsc3, dsh3, d_gn3 = ffn_backward("ffn2", None, dx3, x2, h3, a3, u3, f3, small["norm_ffn2_g"], sc3, g3)

    dmo, dya, dyc, dga, dgc, do, dz, dg2 = red.run(_merge_bwd, 45, dx2, mo, g2, proj, ya, yc, w_o, w_ao, w_co, seq)
    shard = D // N_CHIP
    weight_grad("w_out", None, merged, _spec_col_block(shard), dmo, _spec_rows(D), shard, D)
    weight_grad("w_attn_o", None, o, _spec_col_block(shard), dya, _spec_rows(D), shard, D)
    weight_grad("w_conv_o", None, z, _spec_col_block(shard), dyc, _spec_rows(D), shard, D)
    dq, dkp, dko, dvp, dvo, dsink_steps = red.run(_attn_bwd, 100, proj, small["attn_sinks"], o, do, lse, B, seq)
    dca, dcb, d_conv_w, d_conv_b, d_ln_g, d_ln_b = red.run(
        _conv_bwd, 165, proj, dz, ydw, small["conv_w_dw"], small["conv_ln_g"], small["conv_ln_b"], B, seq)

    def band_sum(own, prev):
        prev = prev.reshape(B, seq // BLOCK, BLOCK, 2 * HEAD_DIM)
        moved = jnp.concatenate([prev[:, 1:], jnp.zeros_like(prev[:, :1])], axis=1)
        return (own + moved.reshape(T, 2 * HEAD_DIM)).astype(BF16)

    dkv = jnp.concatenate([band_sum(dko, dkp), band_sum(dvo, dvp)], axis=1)
    g32, g16 = lax.empty((IN_WIDTH, D), F32), lax.empty((IN_WIDTH, D), BF16)
    row_of = {"q": 0, "kv": D, "conv_a": D + 4 * HEAD_DIM, "conv_b": 2 * D + 4 * HEAD_DIM,
              "gate_a": 3 * D + 4 * HEAD_DIM, "gate_c": 4 * D + 4 * HEAD_DIM}
    for tag, piece in (("q", dq), ("kv", dkv), ("conv_a", dca), ("conv_b", dcb), ("gate_a", dga), ("gate_c", dgc)):
        g32, g16 = _wgrad_rows(piece, h2, g32, g16, row_of[tag], "dw_w_in_" + tag)
    red.add("w_in", tuple(g.reshape(N_CHIP, 2, IN_SHARD // 2, D) for g in (g32, g16)))
    dx1, dsc2, dsh2, d_gn2 = red.run(_in_proj_bwd, 90, (dq, dca, dcb, dga, dgc, dkv), w_in_cols, x1,
                                     small["norm_mix_g"], sc2, dx2, seq)

    def gather_small_grads(dg1, dsc1, dsh1, d_gn1):
        dmod = jnp.concatenate([dsh1, dsc1, dg1, dsh2, dsc2, dg2, dsh3, dsc3, dg3], axis=1)
        d_sinks = jnp.sum(dsink_steps, axis=0)
        vec_grads = {"norm_ffn1_g": d_gn1, "norm_mix_g": d_gn2, "conv_b_dw": d_conv_b, "conv_ln_g": d_ln_g,
                     "conv_ln_b": d_ln_b, "norm_ffn2_g": d_gn3, "final_norm_g": d_final_g}
        block = jnp.zeros((SMALL_ROWS, D), F32)
        block = block.at[ROW_DMOD0:ROW_DMOD0 + N_MOD].set(dmod[0]).at[ROW_DMOD1:ROW_DMOD1 + N_MOD].set(dmod[1])
        block = block.at[ROW_VEC:ROW_VEC + len(VECTORS)].set(jnp.concatenate([vec_grads[n] for n in VECTORS], axis=0))
        block = block.at[ROW_SINK, :2 * HEAD_DIM].set(d_sinks[0])
        block = block.at[ROW_CONVW:ROW_CONVW + CONV_WIDTH].set(d_conv_w[:CONV_WIDTH])
        red.extra = _Gather8Comm(block)

    dx0, _, _, _, _ = ffn_backward("ffn1", 38, dx1, x, h1, a1, u1, f1, small["norm_ffn1_g"], sc1, g1,
                                   before_weight_grads=gather_small_grads)
    return loss_parts, dx0, red, red.extra_out[0]


def kernel(x, c, w_ada, b_ada, norm_ffn1_g, ffn1_w_gate, ffn1_w_up, ffn1_w_down, norm_mix_g, w_in, attn_sinks, w_attn_o, conv_w_dw, conv_b_dw, conv_ln_g, conv_ln_b, w_conv_o, w_out, norm_ffn2_g, ffn2_w_gate, ffn2_w_up, ffn2_w_down, final_norm_g, loss_target, m_w_ada, m_b_ada, m_norm_ffn1_g, m_ffn1_w_gate, m_ffn1_w_up, m_ffn1_w_down, m_norm_mix_g, m_w_in, m_attn_sinks, m_w_attn_o, m_conv_w_dw, m_conv_b_dw, m_conv_ln_g, m_conv_ln_b, m_w_conv_o, m_w_out, m_norm_ffn2_g, m_ffn2_w_gate, m_ffn2_w_up, m_ffn2_w_down, m_final_norm_g, v_w_ada, v_b_ada, v_norm_ffn1_g, v_ffn1_w_gate, v_ffn1_w_up, v_ffn1_w_down, v_norm_mix_g, v_w_in, v_attn_sinks, v_w_attn_o, v_conv_w_dw, v_conv_b_dw, v_conv_ln_g, v_conv_ln_b, v_w_conv_o, v_w_out, v_norm_ffn2_g, v_ffn2_w_gate, v_ffn2_w_up, v_ffn2_w_down, v_final_norm_g):
    args = dict(locals())
    B, seq, D = x.shape
    T = B * seq
    xi, yi, ci = _position()
    chip = 2 * xi + yi
    dev = 4 * xi + 2 * yi + ci

    def shard_2d(prefix, name):
        t = args[prefix + name][0]
        return t.T if name in COL_SHARDED else t

    big = {n: shard_2d("", n) for n in BIG_WEIGHTS}
    final_g = final_norm_g[None, :]
    vec_w = {n: (args[n] if n != "final_norm_g" else final_g) for n in VECTORS}

    conv_cols = D // N_CHIP
    conv_flat = jnp.pad(conv_w_dw[0].reshape(-1), (0, 8 * D - CONV_WIDTH * conv_cols)).reshape(8, D)
    first = _gather8(jnp.concatenate([jnp.pad(c, ((0, 8 - B), (0, 0))), conv_flat], axis=0), "gather_c")
    c_all = first[:, :B].reshape(N_DEV * B, D)
    conv_taps = first[::2, 8:].reshape(N_CHIP, 8 * D)[:, :CONV_WIDTH * conv_cols]
    conv_taps = conv_taps.reshape(N_CHIP, CONV_WIDTH, conv_cols).transpose(1, 0, 2).reshape(CONV_WIDTH, D)
    conv_taps = jnp.pad(conv_taps, ((0, CONV_PAD - CONV_WIDTH), (0, 0)))

    ada_cols = w_ada.shape[2]
    b_cols = lax.dynamic_slice(b_ada, (0, chip * ada_cols), (1, ada_cols))
    mod_part = _ada_fwd(c_all, w_ada[0], b_cols).reshape(N_DEV, B, ada_cols)
    mod = _mod_exchange(mod_part).transpose(1, 0, 2).reshape(B, N_MOD, D)

    core_idx = jnp.reshape(ci, (1,)).astype(jnp.int32)
    chip_idx = jnp.reshape(chip, (1,)).astype(jnp.int32)
    core_chip = jnp.stack([ci, chip]).astype(jnp.int32)
    slots = {n: _cast_slot(big[n], chip_idx, "cast_" + n) for n in FFN1_WEIGHTS}
    later = [n for n in BIG_WEIGHTS if n not in FFN1_WEIGHTS]
    later_slots, ffn1_gathered = _cast_group([big[n] for n in later], ["later_weights"],
                                             _GatherComm([slots[n] for n in FFN1_WEIGHTS]))
    slots.update(zip(later, later_slots))

    small = dict(vec_w)
    small["attn_sinks"] = attn_sinks
    small["conv_w_dw"] = conv_taps

    loss_parts, dx, red, small_all = _local_grads(
        x.reshape(T, D), loss_target.reshape(T, D), mod, slots, ffn1_gathered, small, seq, core_idx, core_chip)

    loss = lax.psum((0.5 / D) * jnp.sum(loss_parts), ("x", "y", "c"))
    grad_x = dx.reshape(B, seq, D)
    out = {}


    def pack_small(prefix):
        rows = [args[prefix + "b_ada"].reshape(N_MOD, D)]
        rows += [args[prefix + n].reshape(1, D) for n in VECTORS]
        rows += [jnp.pad(args[prefix + "attn_sinks"], ((0, 0), (0, D - N_Q_HEADS)))]
        return jnp.pad(jnp.concatenate(rows, axis=0), ((0, 24 - N_MOD - len(VECTORS) - 1), (0, 0)))

    small_sum, sg, sd, sm, sv = _small_adam(small_all, pack_small(""), pack_small("m_"), pack_small("v_"),
                                           ROW_DMOD0, ROW_DMOD1, ROW_VEC)

    def unpack_small(t):
        res = {"b_ada": t[:N_MOD].reshape(1, N_MOD * D)}
        for k, n in enumerate(VECTORS):
            res[n] = t[N_MOD + k].reshape(args[n].shape)
        res["attn_sinks"] = t[N_MOD + len(VECTORS), :N_Q_HEADS].reshape(1, N_Q_HEADS)
        return res

    unpacked = [unpack_small(t) for t in (sg, sd, sm, sv)]
    for n in ("b_ada", "attn_sinks") + VECTORS:
        out[n] = tuple(u[n] for u in unpacked)

    conv_g = lax.dynamic_slice(small_sum, (ROW_CONVW, chip * conv_cols), (CONV_WIDTH, conv_cols))
    d, mn, vn = red.run(_adam_call, None, conv_w_dw[0], conv_g, m_conv_w_dw[0], v_conv_w_dw[0], "adam_conv_w_dw")
    out["conv_w_dw"] = tuple(t[None] for t in (conv_g, d, mn, vn))

    dmod_rows = jnp.stack([small_all[:, ROW_DMOD0:ROW_DMOD0 + N_MOD], small_all[:, ROW_DMOD1:ROW_DMOD1 + N_MOD]], axis=1)
    dmod_all = dmod_rows.reshape(N_DEV * B, N_MOD * D)
    dmod_cols = lax.dynamic_slice(dmod_all, (0, chip * ada_cols), (N_DEV * B, ada_cols))
    ada_out = red.run(_ada_adam, 35, c_all.T, dmod_cols, w_ada[0], m_w_ada[0], v_w_ada[0])
    out["w_ada"] = tuple(t[None] for t in ada_out)

    def finished(n):
        while n not in red.reduced:
            red.step()
        return red.reduced[n].reshape(big[n].shape)

    def emit(n, g, d, mn, vn):
        out[n] = tuple((t.T if n in COL_SHARDED else t)[None] for t in (g, d, mn, vn))

    early = FFN2_WEIGHTS + MIX_WEIGHTS
    early_g = [finished(n) for n in early]
    early_out = red.run(_adam_group, 45, [big[n] for n in early], early_g, [shard_2d("m_", n) for n in early],
                        [shard_2d("v_", n) for n in early], "adam_early")
    for n, g, (d, mn, vn) in zip(early, early_g, early_out):
        emit(n, g, d, mn, vn)
    for n in ("ffn1_w_down", "ffn1_w_gate", "ffn1_w_up"):
        g = finished(n)
        emit(n, g, *red.run(_adam_call, None, big[n], g, shard_2d("m_", n), shard_2d("v_", n), "adam_" + n))

    order = ("w_ada", "b_ada", "norm_ffn1_g", "ffn1_w_gate", "ffn1_w_up", "ffn1_w_down", "norm_mix_g", "w_in",
             "attn_sinks", "w_attn_o", "conv_w_dw", "conv_b_dw", "conv_ln_g", "conv_ln_b", "w_conv_o", "w_out",
             "norm_ffn2_g", "ffn2_w_gate", "ffn2_w_up", "ffn2_w_down", "final_norm_g")
    return (loss, grad_x, *[out[n][0] for n in order], *[out[n][1] for n in order],
            *[out[n][2] for n in order], *[out[n][3] for n in order])
```

```python
import functools

import jax
import jax.numpy as jnp
from jax import lax
from jax.experimental import pallas as pl
from jax.experimental.pallas import tpu as pltpu

F32 = jnp.float32
BF16 = jnp.bfloat16

D_MODEL = 1024
D_FF = 2816
N_CHIP = 4
N_DEV = 8
FF_SHARD = D_FF // N_CHIP
IN_WIDTH = 5376
IN_SHARD = IN_WIDTH // N_CHIP
HEAD_DIM = 64
N_Q_HEADS = 16
N_KV_HEADS = 2
BLOCK = 128
CONV_WIDTH = 31
CONV_PAD = 32
N_MOD = 9
EPS = 1e-6
FFN_RESIDUAL = 0.5
ATTN_SCALE = HEAD_DIM ** -0.5
MASK_VALUE = -1e30

ADAM_LR = 0.001
ADAM_B1 = 0.9
ADAM_B2 = 0.999
ADAM_EPS = 1e-08
ADAM_WD = 0.01
ADAM_STEP = 10

COLB_Q, COLB_CA, COLB_CB, COLB_GA, COLB_GC = 0, 1, 2, 3, 4
COLB_K, COLB_V = 40, 41
PROJ_TILE = 768

VMEM_LIMIT = 56 * 1024 * 1024
MESH = pl.DeviceIdType.MESH
ANY = pl.BlockSpec(memory_space=pl.ANY)
VMEM_SPEC = pl.BlockSpec(memory_space=pltpu.VMEM)
SMEM_SPEC = pl.BlockSpec(memory_space=pltpu.SMEM)


def _params(n_grid):
    return pltpu.CompilerParams(dimension_semantics=("arbitrary",) * n_grid, vmem_limit_bytes=VMEM_LIMIT)


def _tile(n, pref):
    t = min(n, pref)
    while n % t:
        t //= 2
    return t


def _row_tile(rows, cap):
    for t in range(min(rows, cap) // 16 * 16, 0, -16):
        if rows % t == 0:
            return t
    return rows


def _sigmoid(v):
    return 1.0 / (1.0 + jnp.exp(-v))


def _dot_nn(a, b):
    return lax.dot_general(a, b, (((1,), (0,)), ((), ())), preferred_element_type=F32)


def _dot_nt(a, b):
    return lax.dot_general(a, b, (((1,), (1,)), ((), ())), preferred_element_type=F32)


def _dot_tn(a, b):
    return lax.dot_general(a, b, (((0,), (0,)), ((), ())), preferred_element_type=F32)


ROW_CHUNK = 16


def _for_row_chunks(n_rows, fn):
    for r in range(0, n_rows, ROW_CHUNK):
        fn(slice(r, r + ROW_CHUNK))


def _norm_mod(xv, gn, sc, sh):
    r = lax.rsqrt(jnp.mean(xv * xv, axis=-1, keepdims=True) + EPS)
    return ((xv * r) * gn) * (1.0 + sc) + sh


def _accumulate(ref, first, value):
    @pl.when(first)
    def _():
        ref[...] = value

    @pl.when(jnp.logical_not(first))
    def _():
        ref[...] += value


def _norm_mod_bwd(dh, xv, gn, sc, dxo, first_of_batch, first, dx_ref, dsc_ref, dsh_ref, dgn_ref):
    r = lax.rsqrt(jnp.mean(xv * xv, axis=-1, keepdims=True) + EPS)
    xh = xv * r
    _accumulate(dsh_ref, first_of_batch, jnp.sum(dh, axis=0, keepdims=True))
    _accumulate(dsc_ref, first_of_batch, jnp.sum(dh * (xh * gn), axis=0, keepdims=True))
    dn = dh * (1.0 + sc)
    _accumulate(dgn_ref, first, jnp.sum(dn * xh, axis=0, keepdims=True))
    dxh = dn * gn
    dx_ref[...] = dxo + r * (dxh - xh * jnp.mean(dxh * xh, axis=-1, keepdims=True))


CHIP_FLIPS = ((1, 0), (0, 1), (1, 1))


def _position():
    return lax.axis_index("x"), lax.axis_index("y"), lax.axis_index("c")


def _flip(v, f):
    return 1 - v if f else v


class _GatherComm:
    def __init__(self, bufs):
        n = len(bufs)
        self.n = n
        self.operands = list(bufs)
        self.out_shape = [jax.ShapeDtypeStruct(b.shape, b.dtype) for b in bufs]
        self.aliases = {i: i for i in range(n)}
        self.sems = [pltpu.SemaphoreType.DMA((6 * n,)), pltpu.SemaphoreType.DMA((6 * n,))]
        self.rows = [b.shape[1] // 2 for b in bufs]

    def _half(self, ref, i, which):
        return ref.at[pl.ds(which * self.rows[i], self.rows[i]), :]

    def _ici(self, cins, couts, sems, i, k, dst_chip, to):
        x, y, c = _position()
        return pltpu.make_async_remote_copy(
            src_ref=self._half(cins[i].at[2 * x + y], i, c), dst_ref=self._half(couts[i].at[dst_chip], i, c),
            send_sem=sems[0].at[3 * i + k], recv_sem=sems[1].at[3 * i + k], device_id=to, device_id_type=MESH)

    def _d2d(self, couts, sems, i, k, src_chip, which):
        x, y, c = _position()
        place = self._half(couts[i].at[src_chip], i, which)
        return pltpu.make_async_remote_copy(
            src_ref=place, dst_ref=place, send_sem=sems[0].at[3 * self.n + 3 * i + k],
            recv_sem=sems[1].at[3 * self.n + 3 * i + k], device_id=(x, y, 1 - c), device_id_type=MESH)

    def _peers(self):
        x, y, _ = _position()
        return [(_flip(x, fx), _flip(y, fy)) for fx, fy in CHIP_FLIPS]

    def start(self, cins, couts, sems):
        x, y, c = _position()
        for i in range(self.n):
            for k, (px, py) in enumerate(self._peers()):
                self._ici(cins, couts, sems, i, k, 2 * x + y, (px, py, c)).start()

    def finish(self, cins, couts, sems):
        _, _, c = _position()
        peers = self._peers()
        for i in range(self.n):
            for k, (px, py) in enumerate(peers):
                self._ici(cins, couts, sems, i, k, 2 * px + py, (px, py, c)).wait_recv()
                self._d2d(couts, sems, i, k, 2 * px + py, c).start()
        for i in range(self.n):
            for k, (px, py) in enumerate(peers):
                self._d2d(couts, sems, i, k, 2 * px + py, 1 - c).wait_recv()
        for i in range(self.n):
            for k, (px, py) in enumerate(peers):
                self._ici(cins, couts, sems, i, k, 2 * px + py, (px, py, c)).wait_send()
                self._d2d(couts, sems, i, k, 2 * px + py, c).wait_send()


class _ExchangeComm:
    def __init__(self, pairs):
        n = len(pairs)
        self.n = n
        self.operands = list(pairs)
        self.out_shape = [jax.ShapeDtypeStruct((3,) + p.shape[1:], p.dtype) for p in pairs]
        self.aliases = {}
        self.sems = [pltpu.SemaphoreType.DMA((3 * n,)), pltpu.SemaphoreType.DMA((3 * n,))]

    def _copies(self, cins, couts, sems):
        x, y, c = _position()
        peers = [(_flip(x, fx), _flip(y, fy)) for fx, fy in CHIP_FLIPS]
        return [pltpu.make_async_remote_copy(
            src_ref=cins[i].at[2 * px + py], dst_ref=couts[i].at[k], send_sem=sems[0].at[3 * i + k],
            recv_sem=sems[1].at[3 * i + k], device_id=(px, py, c), device_id_type=MESH)
            for i in range(self.n) for k, (px, py) in enumerate(peers)]

    def start(self, cins, couts, sems):
        for cp in self._copies(cins, couts, sems):
            cp.start()

    def finish(self, cins, couts, sems):
        for cp in self._copies(cins, couts, sems):
            cp.wait()


class _SwapComm:
    def __init__(self, grads16):
        n = len(grads16)
        self.n = n
        self.operands = list(grads16)
        self.out_shape = [jax.ShapeDtypeStruct(g.shape[:1] + g.shape[2:], g.dtype) for g in grads16]
        self.aliases = {}
        self.sems = [pltpu.SemaphoreType.DMA((n,)), pltpu.SemaphoreType.DMA((n,))]

    def _copies(self, cins, couts, sems):
        x, y, c = _position()
        return [pltpu.make_async_remote_copy(
            src_ref=cins[i].at[:, 1 - c], dst_ref=couts[i], send_sem=sems[0].at[i], recv_sem=sems[1].at[i],
            device_id=(x, y, 1 - c), device_id_type=MESH) for i in range(self.n)]

    def start(self, cins, couts, sems):
        for cp in self._copies(cins, couts, sems):
            cp.start()

    def finish(self, cins, couts, sems):
        for cp in self._copies(cins, couts, sems):
            cp.wait()


class _JoinComm:
    def __init__(self, halves):
        n = len(halves)
        self.n = n
        self.operands = list(halves)
        self.out_shape = [jax.ShapeDtypeStruct(h.shape, h.dtype) for h in halves]
        self.aliases = {i: i for i in range(n)}
        self.sems = [pltpu.SemaphoreType.DMA((n,)), pltpu.SemaphoreType.DMA((n,))]

    def _copy(self, cins, couts, sems, i, which):
        x, y, c = _position()
        return pltpu.make_async_remote_copy(
            src_ref=cins[i].at[which], dst_ref=couts[i].at[which], send_sem=sems[0].at[i], recv_sem=sems[1].at[i],
            device_id=(x, y, 1 - c), device_id_type=MESH)

    def start(self, cins, couts, sems):
        _, _, c = _position()
        for i in range(self.n):
            self._copy(cins, couts, sems, i, c).start()

    def finish(self, cins, couts, sems):
        _, _, c = _position()
        for i in range(self.n):
            self._copy(cins, couts, sems, i, 1 - c).wait_recv()
        for i in range(self.n):
            self._copy(cins, couts, sems, i, c).wait_send()


class _Gather8Comm:
    def __init__(self, block):
        self.operands = [block]
        self.out_shape = [jax.ShapeDtypeStruct((N_DEV,) + block.shape, block.dtype)]
        self.aliases = {}
        self.sems = [pltpu.SemaphoreType.DMA((N_DEV - 1,)), pltpu.SemaphoreType.DMA((N_DEV - 1,)),
                     pltpu.SemaphoreType.DMA]
        self.flips = [(fx, fy, fc) for fx in (0, 1) for fy in (0, 1) for fc in (0, 1) if (fx, fy, fc) != (0, 0, 0)]

    def _peers(self):
        x, y, c = _position()
        return [(_flip(x, fx), _flip(y, fy), _flip(c, fc)) for fx, fy, fc in self.flips]

    def _copy(self, cins, couts, sems, k, block, to):
        return pltpu.make_async_remote_copy(src_ref=cins[0], dst_ref=couts[0].at[block], send_sem=sems[0].at[k],
                                            recv_sem=sems[1].at[k], device_id=to, device_id_type=MESH)

    def _mine(self, cins, couts, sems):
        x, y, c = _position()
        return pltpu.make_async_copy(cins[0], couts[0].at[4 * x + 2 * y + c], sems[2])

    def start(self, cins, couts, sems):
        x, y, c = _position()
        self._mine(cins, couts, sems).start()
        for k, peer in enumerate(self._peers()):
            self._copy(cins, couts, sems, k, 4 * x + 2 * y + c, peer).start()

    def finish(self, cins, couts, sems):
        for k, (px, py, pc) in enumerate(self._peers()):
            self._copy(cins, couts, sems, k, 4 * px + 2 * py + pc, (px, py, pc)).wait_recv()
        for k, peer in enumerate(self._peers()):
            self._copy(cins, couts, sems, k, 0, peer).wait_send()
        self._mine(cins, couts, sems).wait()


class _CommList:
    def __init__(self, parts):
        self.parts = list(parts)
        self.operands = [t for p in self.parts for t in p.operands]
        self.out_shape = [t for p in self.parts for t in p.out_shape]
        self.sems = [t for p in self.parts for t in p.sems]
        self.aliases = {}
        n_in = n_out = 0
        for p in self.parts:
            self.aliases.update({n_in + i: n_out + j for i, j in p.aliases.items()})
            n_in += len(p.operands)
            n_out += len(p.out_shape)

    def _split(self, cins, couts, sems):
        pos = [0, 0, 0]
        for p in self.parts:
            sizes = (len(p.operands), len(p.out_shape), len(p.sems))
            yield p, tuple(seq[a:a + k] for seq, a, k in zip((cins, couts, sems), pos, sizes))
            pos = [a + k for a, k in zip(pos, sizes)]

    def start(self, cins, couts, sems):
        for p, refs in self._split(cins, couts, sems):
            p.start(*refs)

    def finish(self, cins, couts, sems):
        for p, refs in self._split(cins, couts, sems):
            p.finish(*refs)

    def split_outputs(self, outs):
        res, pos = [], 0
        for p in self.parts:
            res.append(outs[pos:pos + len(p.out_shape)])
            pos += len(p.out_shape)
        return res


def _call(body, *, name, grid, in_specs, out_specs, out_shape, operands, scratch_shapes=(), comm=None):
    n_grid = len(grid)
    if comm is None:
        return pl.pallas_call(
            body, name=name, grid=grid, in_specs=list(in_specs), out_specs=list(out_specs), out_shape=list(out_shape),
            scratch_shapes=list(scratch_shapes), compiler_params=_params(n_grid))(*operands), ()
    counts = (len(in_specs), len(comm.operands), len(out_specs), len(comm.out_shape), len(scratch_shapes),
              len(comm.sems))

    def fused(*refs):
        parts, pos = [], 0
        for k in counts:
            parts.append(refs[pos:pos + k])
            pos += k
        ins, cins, outs, couts, scr, sems = parts
        first = functools.reduce(jnp.logical_and, [pl.program_id(d) == 0 for d in range(n_grid)])
        last = functools.reduce(jnp.logical_and, [pl.program_id(d) == grid[d] - 1 for d in range(n_grid)])

        @pl.when(first)
        def _():
            comm.start(cins, couts, sems)

        body(*ins, *outs, *scr)

        @pl.when(last)
        def _():
            comm.finish(cins, couts, sems)

    res = pl.pallas_call(
        fused, name=name, grid=grid, in_specs=list(in_specs) + [ANY] * counts[1],
        out_specs=list(out_specs) + [ANY] * counts[3], out_shape=list(out_shape) + list(comm.out_shape),
        scratch_shapes=list(scratch_shapes) + list(comm.sems),
        input_output_aliases={counts[0] + i: counts[2] + j for i, j in comm.aliases.items()},
        compiler_params=_params(n_grid))(*operands, *comm.operands)
    return res[:counts[2]], res[counts[2]:]


def _run_comm(comm, name):
    k_in, k_out = len(comm.operands), len(comm.out_shape)

    def body(*refs):
        cins, couts, sems = refs[:k_in], refs[k_in:k_in + k_out], refs[k_in + k_out:]
        comm.start(cins, couts, sems)
        comm.finish(cins, couts, sems)

    return pl.pallas_call(
        body, name=name, in_specs=[ANY] * k_in, out_specs=[ANY] * k_out, out_shape=list(comm.out_shape),
        scratch_shapes=list(comm.sems), input_output_aliases=dict(comm.aliases))(*comm.operands)


def _ffn_fwd(x, gn, sc, sh, gate, wg, wu, wd, seq, name, comm=None):
    T, D = x.shape
    J, Fs, _ = wg.shape
    tm = _tile(seq, 1024)
    nb = seq // tm

    def body(x_ref, gn_ref, sc_ref, sh_ref, gate_ref, wg_ref, wu_ref, wd_ref,
             h_ref, a_ref, u_ref, f_ref, xo_ref, hs, acc, s16):
        j = pl.program_id(1)

        @pl.when(j == 0)
        def _():
            hb = _norm_mod(x_ref[...], gn_ref[...], sc_ref[...], sh_ref[...]).astype(BF16)
            hs[...] = hb
            h_ref[...] = hb
            acc[...] = jnp.zeros_like(acc)

        hb = hs[...]
        a_all = _dot_nt(hb, wg_ref[...])
        u_all = _dot_nt(hb, wu_ref[...])

        def swiglu_rows(rows):
            a = a_all[rows, :]
            u = u_all[rows, :]
            a_ref[rows, :] = a.astype(BF16)
            u_ref[rows, :] = u.astype(BF16)
            s16[rows, :] = ((a * _sigmoid(a)) * u).astype(BF16)

        _for_row_chunks(tm, swiglu_rows)
        acc[...] += _dot_nn(s16[...], wd_ref[...])

        @pl.when(j == J - 1)
        def _():
            f = acc[...]
            f_ref[...] = f.astype(BF16)
            xo_ref[...] = x_ref[...] + (FFN_RESIDUAL * gate_ref[...]) * f

    row = pl.BlockSpec((tm, D), lambda i, j: (i, 0))
    vec = pl.BlockSpec((1, D), lambda i, j: (0, 0))
    per_b = pl.BlockSpec((None, 1, D), lambda i, j: (i // nb, 0, 0))
    hid = pl.BlockSpec((None, tm, Fs), lambda i, j: (j, i, 0))
    return _call(
        body, name=name, grid=(T // tm, J),
        in_specs=[row, vec, per_b, per_b, per_b] + [pl.BlockSpec((None, Fs, D), lambda i, j: (j, 0, 0))] * 3,
        out_specs=[row, hid, hid, row, row],
        out_shape=[jax.ShapeDtypeStruct((T, D), BF16), jax.ShapeDtypeStruct((J, T, Fs), BF16),
                   jax.ShapeDtypeStruct((J, T, Fs), BF16), jax.ShapeDtypeStruct((T, D), BF16),
                   jax.ShapeDtypeStruct((T, D), F32)],
        scratch_shapes=[pltpu.VMEM((tm, D), BF16), pltpu.VMEM((tm, D), F32), pltpu.VMEM((tm, Fs), BF16)],
        operands=(x, gn, sc, sh, gate, wg, wu, wd), comm=comm)


def _ffn_bwd(dxo, x, f, a, u, gn, sc, gate, wg, wu, wd, seq, name, comm=None):
    T, D = x.shape
    J, Fs, _ = wg.shape
    B = T // seq
    tm = _tile(seq, 512)
    nb = seq // tm

    def body(dxo_ref, x_ref, f_ref, a_ref, u_ref, gn_ref, sc_ref, gate_ref, wg_ref, wu_ref, wd_ref,
             da_ref, du_ref, s_ref, df_ref, dx_ref, dgate_ref, dsc_ref, dsh_ref, dgn_ref, dfs, acc):
        i = pl.program_id(0)
        j = pl.program_id(1)
        first_of_batch = i % nb == 0

        @pl.when(j == 0)
        def _():
            dxo_v = dxo_ref[...]
            dfb = ((FFN_RESIDUAL * gate_ref[...]) * dxo_v).astype(BF16)
            dfs[...] = dfb
            df_ref[...] = dfb
            part = jnp.sum((FFN_RESIDUAL * f_ref[...].astype(F32)) * dxo_v, axis=0, keepdims=True)
            _accumulate(dgate_ref, first_of_batch, part)
            acc[...] = jnp.zeros_like(acc)

        ds_all = _dot_nt(dfs[...], wd_ref[...])

        def swiglu_bwd_rows(rows):
            ds = ds_all[rows, :]
            av = a_ref[rows, :].astype(F32)
            uv = u_ref[rows, :].astype(F32)
            sig = _sigmoid(av)
            sil = av * sig
            s_ref[rows, :] = (sil * uv).astype(BF16)
            da_ref[rows, :] = (ds * uv * (sig * (1.0 + av * (1.0 - sig)))).astype(BF16)
            du_ref[rows, :] = (ds * sil).astype(BF16)

        _for_row_chunks(tm, swiglu_bwd_rows)
        acc[...] += _dot_nn(da_ref[...], wg_ref[...]) + _dot_nn(du_ref[...], wu_ref[...])

        @pl.when(j == J - 1)
        def _():
            _norm_mod_bwd(acc[...], x_ref[...], gn_ref[...], sc_ref[...], dxo_ref[...],
                          first_of_batch, i == 0, dx_ref, dsc_ref, dsh_ref, dgn_ref)

    row = pl.BlockSpec((tm, D), lambda i, j: (i, 0))
    vec = pl.BlockSpec((1, D), lambda i, j: (0, 0))
    per_b = pl.BlockSpec((None, 1, D), lambda i, j: (i // nb, 0, 0))
    hid = pl.BlockSpec((None, tm, Fs), lambda i, j: (j, i, 0))
    hid_shape = jax.ShapeDtypeStruct((J, T, Fs), BF16)
    per_b_shape = jax.ShapeDtypeStruct((B, 1, D), F32)
    return _call(
        body, name=name, grid=(T // tm, J),
        in_specs=[row, row, row, hid, hid, vec, per_b, per_b]
        + [pl.BlockSpec((None, Fs, D), lambda i, j: (j, 0, 0))] * 3,
        out_specs=[hid, hid, hid, row, row, per_b, per_b, per_b, vec],
        out_shape=[hid_shape, hid_shape, hid_shape, jax.ShapeDtypeStruct((T, D), BF16),
                   jax.ShapeDtypeStruct((T, D), F32), per_b_shape, per_b_shape, per_b_shape,
                   jax.ShapeDtypeStruct((1, D), F32)],
        scratch_shapes=[pltpu.VMEM((tm, D), BF16), pltpu.VMEM((tm, D), F32)],
        operands=(dxo, x, f, a, u, gn, sc, gate, wg, wu, wd), comm=comm)


def _wgrad(a, a_spec, b, b_spec, rows, cols, n_tok, name, comm=None):
    tk = _tile(n_tok, 2048)
    nk = n_tok // tk
    half = rows // 2

    def body(a_ref, b_ref, o32_ref, o16_ref, acc):
        k = pl.program_id(1)

        @pl.when(k == 0)
        def _():
            acc[...] = jnp.zeros_like(acc)

        acc[...] += _dot_tn(a_ref[...], b_ref[...])

        @pl.when(k == nk - 1)
        def _():
            for h in range(2):
                v = acc[h * half:(h + 1) * half, :]
                o32_ref[h] = v
                o16_ref[h] = v.astype(BF16)

    out_spec = pl.BlockSpec((None, 2, half, cols), lambda j, k: (j, 0, 0, 0))
    return _call(
        body, name=name, grid=(N_CHIP, nk),
        in_specs=[a_spec(tk), b_spec(tk)],
        out_specs=[out_spec, out_spec],
        out_shape=[jax.ShapeDtypeStruct((N_CHIP, 2, half, cols), F32),
                   jax.ShapeDtypeStruct((N_CHIP, 2, half, cols), BF16)],
        scratch_shapes=[pltpu.VMEM((rows, cols), F32)],
        operands=(a, b), comm=comm)


def _spec_rows(width):
    return lambda tk: pl.BlockSpec((tk, width), lambda j, k: (k, 0))


def _spec_chip_major(width):
    return lambda tk: pl.BlockSpec((None, tk, width), lambda j, k: (j, k, 0))


def _spec_col_block(width):
    return lambda tk: pl.BlockSpec((tk, width), lambda j, k: (k, j))


def _in_proj(x, gn, sc, sh, w_in, seq, comm=None):
    T, D = x.shape
    N = w_in.shape[0]
    tm = _tile(seq, 2048)
    nb = seq // tm

    def body(x_ref, gn_ref, sc_ref, sh_ref, w_ref, h_ref, p_ref, hs):
        @pl.when(pl.program_id(1) == 0)
        def _():
            hb = _norm_mod(x_ref[...], gn_ref[...], sc_ref[...], sh_ref[...]).astype(BF16)
            hs[...] = hb
            h_ref[...] = hb

        p_ref[...] = _dot_nt(hs[...], w_ref[...]).astype(BF16)

    row = pl.BlockSpec((tm, D), lambda i, j: (i, 0))
    per_b = pl.BlockSpec((None, 1, D), lambda i, j: (i // nb, 0, 0))
    return _call(
        body, name="mix_in_proj", grid=(T // tm, N // PROJ_TILE),
        in_specs=[row, pl.BlockSpec((1, D), lambda i, j: (0, 0)), per_b, per_b,
                  pl.BlockSpec((PROJ_TILE, D), lambda i, j: (j, 0))],
        out_specs=[row, pl.BlockSpec((tm, PROJ_TILE), lambda i, j: (i, j))],
        out_shape=[jax.ShapeDtypeStruct((T, D), BF16), jax.ShapeDtypeStruct((T, N), BF16)],
        scratch_shapes=[pltpu.VMEM((tm, D), BF16)],
        operands=(x, gn, sc, sh, w_in), comm=comm)


def _attn_specs(nblk):
    def own(col):
        return lambda b, n: (b * nblk + n, col)

    def prev(col):
        return lambda b, n: (b * nblk + jnp.maximum(n - 1, 0), col)

    kv = (BLOCK, 2 * HEAD_DIM)
    return [pl.BlockSpec((BLOCK, D_MODEL), own(COLB_Q)),
            pl.BlockSpec(kv, prev(COLB_K)), pl.BlockSpec(kv, own(COLB_K)),
            pl.BlockSpec(kv, prev(COLB_V)), pl.BlockSpec(kv, own(COLB_V))]


def _band_operands(prev_ref, own_ref, lo):
    band = jnp.concatenate([prev_ref[...], own_ref[...]], axis=0).astype(F32)
    rolled = pltpu.roll(band, HEAD_DIM, 1)
    zero = jnp.zeros_like(band)
    head0 = jnp.concatenate([jnp.where(lo, band, zero), jnp.where(lo, zero, rolled)], axis=0).astype(BF16)
    head1 = jnp.concatenate([jnp.where(lo, rolled, zero), jnp.where(lo, zero, band)], axis=0).astype(BF16)
    return head0, head1


PAIRS_PER_KV = N_Q_HEADS // 2 // N_KV_HEADS
BAND = 2 * BLOCK


def _band_valid(has_prev):
    qi = lax.broadcasted_iota(jnp.int32, (PAIRS_PER_KV * BLOCK, BAND), 0) & (BLOCK - 1)
    sj = lax.broadcasted_iota(jnp.int32, (PAIRS_PER_KV * BLOCK, BAND), 1)
    rel = qi + BLOCK - sj
    return (rel >= 0) & (rel < BLOCK) & ((sj >= BLOCK) | has_prev)


def _pair_lanes(kvh, pp):
    pair = kvh * PAIRS_PER_KV + pp
    return slice(pair * 2 * HEAD_DIM, (pair + 1) * 2 * HEAD_DIM)


def _stack_pairs(ref, kvh):
    return jnp.concatenate([ref[:, _pair_lanes(kvh, pp)] for pp in range(PAIRS_PER_KV)], axis=0)


def _rows_per_pair(columns):
    return jnp.concatenate(columns, axis=0)


def _attn_fwd(proj, sinks, batch, seq, comm=None):
    T = proj.shape[0]
    nblk = seq // BLOCK

    def body(sink_ref, q_ref, kp_ref, ko_ref, vp_ref, vo_ref, o_ref, lse_ref):
        lo = lax.broadcasted_iota(jnp.int32, (1, 2 * HEAD_DIM), 1) < HEAD_DIM
        head_lane = lax.broadcasted_iota(jnp.int32, (1, N_Q_HEADS), 1)
        valid = _band_valid(pl.program_id(1) > 0)
        k_ops = _band_operands(kp_ref, ko_ref, lo)
        v_ops = _band_operands(vp_ref, vo_ref, lo)
        lse_all = jnp.zeros((BLOCK, N_Q_HEADS), F32)
        col = jnp.zeros((BLOCK, 1), F32)
        side0_row = lax.broadcasted_iota(jnp.int32, (2 * BAND, 2 * HEAD_DIM), 0) < BAND
        low_lane = lax.broadcasted_iota(jnp.int32, (2 * BAND, 2 * HEAD_DIM), 1) < HEAD_DIM
        side_ones = jnp.where(side0_row == low_lane, 1.0, 0.0).astype(BF16)
        for kvh in range(N_KV_HEADS):
            s_all = _dot_nt(_stack_pairs(q_ref, kvh), k_ops[kvh]) * ATTN_SCALE
            weights, maxes, sink_terms = [], [], []
            for side in range(2):
                heads = [2 * (kvh * PAIRS_PER_KV + pp) + side for pp in range(PAIRS_PER_KV)]
                sink = _rows_per_pair([col + sink_ref[0, h] for h in heads])
                s = jnp.where(valid, s_all[:, side * BAND:(side + 1) * BAND], MASK_VALUE)
                m = jnp.maximum(jnp.max(s, axis=-1, keepdims=True), sink)
                weights.append(jnp.where(valid, jnp.exp(s - m), 0.0).astype(BF16))
                maxes.append(m)
                sink_terms.append(jnp.exp(sink - m))
            p_all = jnp.concatenate(weights, axis=1)
            den = _dot_nn(p_all, side_ones) + jnp.where(lo, sink_terms[0], sink_terms[1])
            out = _dot_nn(p_all, v_ops[kvh]) / den
            for pp in range(PAIRS_PER_KV):
                o_ref[:, _pair_lanes(kvh, pp)] = out[pp * BLOCK:(pp + 1) * BLOCK].astype(BF16)
            for side in range(2):
                lse = maxes[side] + jnp.log(den[:, side * HEAD_DIM:side * HEAD_DIM + 1])
                for pp in range(PAIRS_PER_KV):
                    h = 2 * (kvh * PAIRS_PER_KV + pp) + side
                    lse_all = jnp.where(head_lane == h, lse[pp * BLOCK:(pp + 1) * BLOCK], lse_all)
        lse_ref[...] = lse_all

    return _call(
        body, name="attn_fwd", grid=(batch, nblk),
        in_specs=[SMEM_SPEC] + _attn_specs(nblk),
        out_specs=[pl.BlockSpec((BLOCK, D_MODEL), lambda b, n: (b * nblk + n, 0)),
                   pl.BlockSpec((BLOCK, N_Q_HEADS), lambda b, n: (b * nblk + n, 0))],
        out_shape=[jax.ShapeDtypeStruct((T, D_MODEL), BF16), jax.ShapeDtypeStruct((T, N_Q_HEADS), F32)],
        operands=(sinks, proj, proj, proj, proj, proj), comm=comm)


def _conv_u(ca, cb):
    return ca.astype(F32) * _sigmoid(cb.astype(F32))


def _conv_specs(ts, tiles_per_seq):
    per_tile = ts // CONV_PAD

    def tile(col):
        return lambda b, t: (b * tiles_per_seq + t, col)

    def before(col):
        return lambda b, t: (jnp.maximum((b * tiles_per_seq + t) * per_tile - 1, 0), col)

    return [pl.BlockSpec((ts, D_MODEL), tile(COLB_CA)), pl.BlockSpec((ts, D_MODEL), tile(COLB_CB)),
            pl.BlockSpec((CONV_PAD, D_MODEL), before(COLB_CA)), pl.BlockSpec((CONV_PAD, D_MODEL), before(COLB_CB))]


SUBLANES = 8


def _fill_upad(upad, ca_ref, cb_ref, cah_ref, cbh_ref, t):
    halo = _conv_u(cah_ref[...], cbh_ref[...])
    upad[0, 0:CONV_PAD, :] = jnp.where(t > 0, halo, jnp.zeros_like(halo))
    upad[0, CONV_PAD:, :] = _conv_u(ca_ref[...], cb_ref[...])


def _fill_shifted(pad):
    rows = pad.shape[1] - SUBLANES
    for b in range(1, SUBLANES):
        pad[b, 0:rows, :] = pad[0, b:b + rows, :]


def _shifted_rows(pad, offset, rows):
    b = offset % SUBLANES
    return pad[b, offset - b:offset - b + rows, :]


def _layernorm_stats(y):
    mu = jnp.mean(y, axis=-1, keepdims=True)
    yc = y - mu
    rstd = lax.rsqrt(jnp.mean(yc * yc, axis=-1, keepdims=True) + EPS)
    return yc * rstd, rstd


def _conv_fwd(proj, w_dw, b_dw, ln_g, ln_b, batch, seq, comm=None):
    T = proj.shape[0]
    ts = _tile(seq, 256)
    nt = seq // ts
    shift = CONV_PAD - (CONV_WIDTH - 1)

    def body(ca_ref, cb_ref, cah_ref, cbh_ref, w_ref, b_ref, g_ref, beta_ref, y_ref, z_ref, upad):
        _fill_upad(upad, ca_ref, cb_ref, cah_ref, cbh_ref, pl.program_id(1))
        _fill_shifted(upad)
        y = jnp.zeros((ts, D_MODEL), F32) + b_ref[...]
        for k in range(CONV_WIDTH):
            y = y + w_ref[k:k + 1, :] * _shifted_rows(upad, shift + k, ts)
        y_ref[...] = y
        lnh, _ = _layernorm_stats(y)
        ln = lnh * g_ref[...] + beta_ref[...]
        z_ref[...] = (ln * _sigmoid(ln)).astype(BF16)

    vec = pl.BlockSpec((1, D_MODEL), lambda b, t: (0, 0))
    row = pl.BlockSpec((ts, D_MODEL), lambda b, t: (b * nt + t, 0))
    return _call(
        body, name="conv_fwd", grid=(batch, nt),
        in_specs=_conv_specs(ts, nt) + [pl.BlockSpec((CONV_PAD, D_MODEL), lambda b, t: (0, 0)), vec, vec, vec],
        out_specs=[row, row],
        out_shape=[jax.ShapeDtypeStruct((T, D_MODEL), F32), jax.ShapeDtypeStruct((T, D_MODEL), BF16)],
        scratch_shapes=[pltpu.VMEM((SUBLANES, ts + CONV_PAD, D_MODEL), F32)],
        operands=(proj, proj, proj, proj, w_dw, b_dw, ln_g, ln_b), comm=comm)


def _merge(o, z, proj, w_ao, w_co, w_out, x, gate, seq):
    T, D = x.shape
    tm = _tile(seq, 512)
    nb = seq // tm

    def body(o_ref, z_ref, ga_ref, gc_ref, wao_ref, wco_ref, wout_ref, x_ref, gate_ref,
             ya_ref, yc_ref, mg_ref, mo_ref, xo_ref):
        ya = _dot_nn(o_ref[...], wao_ref[...])
        yc = _dot_nn(z_ref[...], wco_ref[...])
        ya_ref[...] = ya.astype(BF16)
        yc_ref[...] = yc.astype(BF16)
        merged = (_sigmoid(ga_ref[...].astype(F32)) * ya + _sigmoid(gc_ref[...].astype(F32)) * yc).astype(BF16)
        mg_ref[...] = merged
        mo = _dot_nn(merged, wout_ref[...])
        mo_ref[...] = mo.astype(BF16)
        xo_ref[...] = x_ref[...] + gate_ref[...] * mo

    row = pl.BlockSpec((tm, D), lambda i: (i, 0))
    mat = pl.BlockSpec((D, D), lambda i: (0, 0))
    act = jax.ShapeDtypeStruct((T, D), BF16)
    return pl.pallas_call(
        body, name="mix_merge", grid=(T // tm,),
        in_specs=[row, row, pl.BlockSpec((tm, D), lambda i: (i, COLB_GA)), pl.BlockSpec((tm, D), lambda i: (i, COLB_GC)),
                  mat, mat, mat, row, pl.BlockSpec((None, 1, D), lambda i: (i // nb, 0, 0))],
        out_specs=[row, row, row, row, row],
        out_shape=[act, act, act, act, jax.ShapeDtypeStruct((T, D), F32)],
        compiler_params=_params(1),
    )(o, z, proj, proj, w_ao, w_co, w_out, x, gate)


def _final_loss(x, gf, target):
    T, D = x.shape
    tm = _tile(T, 512)

    def body(x_ref, gf_ref, t_ref, dx_ref, lp_ref, dgf_ref):
        first = pl.program_id(0) == 0
        xv = x_ref[...]
        gfv = gf_ref[...]
        r = lax.rsqrt(jnp.mean(xv * xv, axis=-1, keepdims=True) + EPS)
        xh = xv * r
        err = xh * gfv - t_ref[...]
        _accumulate(lp_ref, first, jnp.sum(err * err, axis=0, keepdims=True))
        dy = err * (1.0 / D)
        _accumulate(dgf_ref, first, jnp.sum(dy * xh, axis=0, keepdims=True))
        dxh = dy * gfv
        dx_ref[...] = r * (dxh - xh * jnp.mean(dxh * xh, axis=-1, keepdims=True))

    row = pl.BlockSpec((tm, D), lambda i: (i, 0))
    vec = pl.BlockSpec((1, D), lambda i: (0, 0))
    return pl.pallas_call(
        body, name="final_loss", grid=(T // tm,),
        in_specs=[row, vec, row], out_specs=[row, vec, vec],
        out_shape=[jax.ShapeDtypeStruct((T, D), F32), jax.ShapeDtypeStruct((1, D), F32),
                   jax.ShapeDtypeStruct((1, D), F32)],
        compiler_params=_params(1),
    )(x, gf, target)


def _merge_bwd(dxo, mo, gate, proj, ya, yc, w_out, w_ao, w_co, seq, comm=None):
    T, D = dxo.shape
    B = T // seq
    tm = _tile(seq, 512)
    nb = seq // tm

    def body(dxo_ref, mo_ref, gate_ref, ga_ref, gc_ref, ya_ref, yc_ref, wout_ref, wao_ref, wco_ref,
             dmo_ref, dya_ref, dyc_ref, dga_ref, dgc_ref, do_ref, dz_ref, dgate_ref):
        dxo_v = dxo_ref[...]
        dmo = (gate_ref[...] * dxo_v).astype(BF16)
        dmo_ref[...] = dmo
        _accumulate(dgate_ref, pl.program_id(0) % nb == 0,
                    jnp.sum(mo_ref[...].astype(F32) * dxo_v, axis=0, keepdims=True))
        dm = _dot_nt(dmo, wout_ref[...])
        sa = _sigmoid(ga_ref[...].astype(F32))
        sc = _sigmoid(gc_ref[...].astype(F32))
        dya = (sa * dm).astype(BF16)
        dyc = (sc * dm).astype(BF16)
        dya_ref[...] = dya
        dyc_ref[...] = dyc
        dga_ref[...] = (dm * ya_ref[...].astype(F32) * (sa * (1.0 - sa))).astype(BF16)
        dgc_ref[...] = (dm * yc_ref[...].astype(F32) * (sc * (1.0 - sc))).astype(BF16)
        do_ref[...] = _dot_nt(dya, wao_ref[...]).astype(BF16)
        dz_ref[...] = _dot_nt(dyc, wco_ref[...]).astype(BF16)

    row = pl.BlockSpec((tm, D), lambda i: (i, 0))
    mat = pl.BlockSpec((D, D), lambda i: (0, 0))
    per_b = pl.BlockSpec((None, 1, D), lambda i: (i // nb, 0, 0))
    act = jax.ShapeDtypeStruct((T, D), BF16)
    return _call(
        body, name="mix_merge_bwd", grid=(T // tm,),
        in_specs=[row, row, per_b, pl.BlockSpec((tm, D), lambda i: (i, COLB_GA)),
                  pl.BlockSpec((tm, D), lambda i: (i, COLB_GC)), row, row, mat, mat, mat],
        out_specs=[row] * 7 + [per_b],
        out_shape=[act] * 7 + [jax.ShapeDtypeStruct((B, 1, D), F32)],
        operands=(dxo, mo, gate, proj, proj, ya, yc, w_out, w_ao, w_co), comm=comm)


def _attn_bwd(proj, sinks, o, do, lse, batch, seq, comm=None):
    T = proj.shape[0]
    nblk = seq // BLOCK
    n_steps = batch * nblk

    def body(sink_ref, q_ref, kp_ref, ko_ref, vp_ref, vo_ref, o_ref, do_ref, lse_ref,
             dq_ref, dkp_ref, dko_ref, dvp_ref, dvo_ref, dsink_ref):
        lo = lax.broadcasted_iota(jnp.int32, (1, 2 * HEAD_DIM), 1) < HEAD_DIM
        sink_lane = lax.broadcasted_iota(jnp.int32, (1, 2 * HEAD_DIM), 1)
        valid = _band_valid(pl.program_id(1) > 0)
        k_ops = _band_operands(kp_ref, ko_ref, lo)
        v_ops = _band_operands(vp_ref, vo_ref, lo)
        dsink = jnp.zeros((1, 2 * HEAD_DIM), F32)
        col = jnp.zeros((BLOCK, 1), F32)

        def fold(both):
            return (jnp.where(lo, both[:BAND], 0.0)
                    + pltpu.roll(jnp.where(lo, 0.0, both[BAND:]), HEAD_DIM, 1))

        dk_heads, dv_heads = [], []
        for kvh in range(N_KV_HEADS):
            q4 = _stack_pairs(q_ref, kvh)
            do4 = _stack_pairs(do_ref, kvh)
            dd = do4.astype(F32) * _stack_pairs(o_ref, kvh).astype(F32)
            s_all = _dot_nt(q4, k_ops[kvh]) * ATTN_SCALE
            dp_all = _dot_nt(do4, v_ops[kvh])
            ds_sides, p_sides = [], []
            for side in range(2):
                heads = [2 * (kvh * PAIRS_PER_KV + pp) + side for pp in range(PAIRS_PER_KV)]
                mine = lo if side == 0 else jnp.logical_not(lo)
                cols = slice(side * BAND, (side + 1) * BAND)
                sink = _rows_per_pair([col + sink_ref[0, h] for h in heads])
                lse = _rows_per_pair([lse_ref[:, h:h + 1] for h in heads])
                delta = jnp.sum(jnp.where(mine, dd, 0.0), axis=-1, keepdims=True)
                p = jnp.where(valid, jnp.exp(jnp.where(valid, s_all[:, cols], MASK_VALUE) - lse), 0.0)
                ds_sides.append((p * (dp_all[:, cols] - delta) * ATTN_SCALE).astype(BF16))
                p_sides.append(p.astype(BF16))
                sink_part = jnp.exp(sink - lse) * delta
                for pp, h in enumerate(heads):
                    dsink = dsink + jnp.where(sink_lane == h, -jnp.sum(sink_part[pp * BLOCK:(pp + 1) * BLOCK]), 0.0)
            ds_all = jnp.concatenate(ds_sides, axis=1)
            dq4 = _dot_nn(ds_all, k_ops[kvh])
            for pp in range(PAIRS_PER_KV):
                dq_ref[:, _pair_lanes(kvh, pp)] = dq4[pp * BLOCK:(pp + 1) * BLOCK].astype(BF16)
            dk_heads.append(fold(_dot_tn(ds_all, q4)))
            dv_heads.append(fold(_dot_tn(jnp.concatenate(p_sides, axis=1), do4)))
        dk = dk_heads[0] + pltpu.roll(dk_heads[1], HEAD_DIM, 1)
        dv = dv_heads[0] + pltpu.roll(dv_heads[1], HEAD_DIM, 1)
        dkp_ref[...] = dk[:BLOCK]
        dko_ref[...] = dk[BLOCK:]
        dvp_ref[...] = dv[:BLOCK]
        dvo_ref[...] = dv[BLOCK:]
        dsink_ref[...] = dsink

    def own(b, n):
        return (b * nblk + n, 0)

    row = pl.BlockSpec((BLOCK, D_MODEL), own)
    kv = pl.BlockSpec((BLOCK, 2 * HEAD_DIM), own)
    kv_shape = jax.ShapeDtypeStruct((T, 2 * HEAD_DIM), F32)
    return _call(
        body, name="attn_bwd", grid=(batch, nblk),
        in_specs=[SMEM_SPEC] + _attn_specs(nblk) + [row, row, pl.BlockSpec((BLOCK, N_Q_HEADS), own)],
        out_specs=[row, kv, kv, kv, kv, pl.BlockSpec((None, 1, 2 * HEAD_DIM), lambda b, n: (b * nblk + n, 0, 0))],
        out_shape=[jax.ShapeDtypeStruct((T, D_MODEL), BF16), kv_shape, kv_shape, kv_shape, kv_shape,
                   jax.ShapeDtypeStruct((n_steps, 1, 2 * HEAD_DIM), F32)],
        operands=(sinks, proj, proj, proj, proj, proj, o, do, lse), comm=comm)


def _conv_bwd(proj, dz, ydw, w_dw, ln_g, ln_b, batch, seq, comm=None):
    T = proj.shape[0]
    ts = _tile(seq, 256)
    nt = seq // ts
    per_tile = ts // CONV_PAD
    shift = CONV_PAD - (CONV_WIDTH - 1)

    def body(ca_ref, cb_ref, cah_ref, cbh_ref, dz_ref, dzn_ref, y_ref, yn_ref, w_ref, g_ref, beta_ref,
             dca_ref, dcb_ref, dw_ref, db_ref, dg_ref, dbeta_ref, upad, dypad):
        t = pl.program_id(1)
        first = (pl.program_id(0) == 0) & (t == 0)
        gv = g_ref[...]

        def ln_bwd(dzv, yv):
            lnh, rstd = _layernorm_stats(yv)
            ln = lnh * gv + beta_ref[...]
            sg = _sigmoid(ln)
            dln = dzv.astype(F32) * (sg * (1.0 + ln * (1.0 - sg)))
            dyh = dln * gv
            dy = rstd * (dyh - jnp.mean(dyh, axis=-1, keepdims=True)
                         - lnh * jnp.mean(dyh * lnh, axis=-1, keepdims=True))
            return dy, dln, lnh

        dy, dln, lnh = ln_bwd(dz_ref[...], y_ref[...])
        dy_next, _, _ = ln_bwd(dzn_ref[...], yn_ref[...])
        dypad[0, 0:ts, :] = dy
        dypad[0, ts:, :] = jnp.where(t < nt - 1, dy_next, jnp.zeros_like(dy_next))
        _fill_shifted(dypad)
        _fill_upad(upad, ca_ref, cb_ref, cah_ref, cbh_ref, t)
        _fill_shifted(upad)

        _accumulate(dg_ref, first, jnp.sum(dln * lnh, axis=0, keepdims=True))
        _accumulate(dbeta_ref, first, jnp.sum(dln, axis=0, keepdims=True))
        _accumulate(db_ref, first, jnp.sum(dy, axis=0, keepdims=True))

        @pl.when(first)
        def _():
            dw_ref[...] = jnp.zeros_like(dw_ref)

        du = jnp.zeros((ts, D_MODEL), F32)
        for k in range(CONV_WIDTH):
            du = du + w_ref[k:k + 1, :] * _shifted_rows(dypad, CONV_WIDTH - 1 - k, ts)
            dw_ref[k:k + 1, :] += jnp.sum(dy * _shifted_rows(upad, shift + k, ts), axis=0, keepdims=True)
        cav = ca_ref[...].astype(F32)
        sb = _sigmoid(cb_ref[...].astype(F32))
        dca_ref[...] = (du * sb).astype(BF16)
        dcb_ref[...] = (du * cav * (sb * (1.0 - sb))).astype(BF16)

    def tile(b, t):
        return (b * nt + t, 0)

    def after(b, t):
        return (jnp.minimum((b * nt + t + 1) * per_tile, T // CONV_PAD - 1), 0)

    row = pl.BlockSpec((ts, D_MODEL), tile)
    halo = pl.BlockSpec((CONV_PAD, D_MODEL), after)
    vec = pl.BlockSpec((1, D_MODEL), lambda b, t: (0, 0))
    wspec = pl.BlockSpec((CONV_PAD, D_MODEL), lambda b, t: (0, 0))
    act = jax.ShapeDtypeStruct((T, D_MODEL), BF16)
    vec_shape = jax.ShapeDtypeStruct((1, D_MODEL), F32)
    return _call(
        body, name="conv_bwd", grid=(batch, nt),
        in_specs=_conv_specs(ts, nt) + [row, halo, row, halo, wspec, vec, vec],
        out_specs=[row, row, wspec, vec, vec, vec],
        out_shape=[act, act, jax.ShapeDtypeStruct((CONV_PAD, D_MODEL), F32), vec_shape, vec_shape, vec_shape],
        scratch_shapes=[pltpu.VMEM((SUBLANES, ts + CONV_PAD, D_MODEL), F32)] * 2,
        operands=(proj, proj, proj, proj, dz, dz, ydw, ydw, w_dw, ln_g, ln_b), comm=comm)


def _in_proj_bwd(pieces, w_cols, x, gn, sc, dxo, seq, comm=None):
    T, D = x.shape
    B = T // seq
    wide, narrow = list(pieces[:-1]), pieces[-1]
    P = len(wide)
    nw = narrow.shape[1]
    tm = _tile(seq, 512)
    nb = seq // tm

    def body(*refs):
        wide_refs = refs[:P]
        kv_ref, w_ref, wkv_ref, x_ref, gn_ref, sc_ref, dxo_ref, dx_ref, dsc_ref, dsh_ref, dgn_ref, acc = refs[P:]
        i = pl.program_id(0)
        j = pl.program_id(1)

        @pl.when(j == 0)
        def _():
            acc[...] = _dot_nn(kv_ref[...], wkv_ref[...])

        for p in range(P):
            @pl.when(j == p)
            def _(p=p):
                acc[...] += _dot_nn(wide_refs[p][...], w_ref[...])

        @pl.when(j == P - 1)
        def _():
            _norm_mod_bwd(acc[...], x_ref[...], gn_ref[...], sc_ref[...], dxo_ref[...],
                          i % nb == 0, i == 0, dx_ref, dsc_ref, dsh_ref, dgn_ref)

    row = pl.BlockSpec((tm, D), lambda i, j: (i, 0))
    vec = pl.BlockSpec((1, D), lambda i, j: (0, 0))
    per_b = pl.BlockSpec((None, 1, D), lambda i, j: (i // nb, 0, 0))
    per_b_shape = jax.ShapeDtypeStruct((B, 1, D), F32)
    return _call(
        body, name="mix_in_proj_bwd", grid=(T // tm, P),
        in_specs=[row] * P + [pl.BlockSpec((tm, nw), lambda i, j: (i, 0)),
                              pl.BlockSpec((D, D), lambda i, j: (j, 0)),
                              pl.BlockSpec((nw, D), lambda i, j: (P * D // nw, 0)), row, vec, per_b, row],
        out_specs=[row, per_b, per_b, vec],
        out_shape=[jax.ShapeDtypeStruct((T, D), F32), per_b_shape, per_b_shape, jax.ShapeDtypeStruct((1, D), F32)],
        scratch_shapes=[pltpu.VMEM((tm, D), F32)],
        operands=(*wide, narrow, w_cols, w_cols, x, gn, sc, dxo), comm=comm)


def _wgrad_rows(piece, h, out32, out16, row_offset, name):
    T, n = piece.shape
    C = h.shape[1]
    tk = _tile(T, 2048)
    nk = T // tk

    def body(a_ref, b_ref, in32, in16, o32_ref, o16_ref, acc, stage16, sems):
        k = pl.program_id(0)

        @pl.when(k == 0)
        def _():
            acc[...] = jnp.zeros_like(acc)

        acc[...] += _dot_tn(a_ref[...], b_ref[...])

        @pl.when(k == nk - 1)
        def _():
            stage16[...] = acc[...].astype(BF16)
            rows = pl.ds(row_offset, n)
            copies = [pltpu.make_async_copy(acc, o32_ref.at[rows, :], sems.at[0]),
                      pltpu.make_async_copy(stage16, o16_ref.at[rows, :], sems.at[1])]
            for cp in copies:
                cp.start()
            for cp in copies:
                cp.wait()

    return pl.pallas_call(
        body, name=name, grid=(nk,),
        in_specs=[pl.BlockSpec((tk, n), lambda k: (k, 0)), pl.BlockSpec((tk, C), lambda k: (k, 0)), ANY, ANY],
        out_specs=[ANY, ANY], out_shape=[jax.ShapeDtypeStruct(out32.shape, F32), jax.ShapeDtypeStruct(out16.shape, BF16)],
        scratch_shapes=[pltpu.VMEM((n, C), F32), pltpu.VMEM((n, C), BF16), pltpu.SemaphoreType.DMA((2,))],
        input_output_aliases={2: 0, 3: 1}, compiler_params=_params(1),
    )(piece, h, out32, out16)


def _ada_fwd(c_all, w_ada, b_cols):
    nbatch, D = c_all.shape
    N = w_ada.shape[1]
    tn = _tile(N, 768)

    def body(c_ref, w_ref, b_ref, o_ref):
        cv = c_ref[...]
        act = (cv * _sigmoid(cv)).astype(BF16)
        o_ref[...] = _dot_nn(act, w_ref[...].astype(BF16)) + b_ref[...]

    return pl.pallas_call(
        body, name="ada_fwd", grid=(N // tn,),
        in_specs=[pl.BlockSpec((nbatch, D), lambda j: (0, 0)), pl.BlockSpec((D, tn), lambda j: (0, j)),
                  pl.BlockSpec((1, tn), lambda j: (0, j))],
        out_specs=pl.BlockSpec((nbatch, tn), lambda j: (0, j)),
        out_shape=jax.ShapeDtypeStruct((nbatch, N), F32),
        compiler_params=_params(1),
    )(c_all, w_ada, b_cols)


def _adamw(w, g, m, v):
    m = ADAM_B1 * m + (1.0 - ADAM_B1) * g
    v = ADAM_B2 * v + (1.0 - ADAM_B2) * (g * g)
    m_hat = m / (1.0 - ADAM_B1 ** ADAM_STEP)
    v_hat = v / (1.0 - ADAM_B2 ** ADAM_STEP)
    delta = -ADAM_LR * (m_hat / (jnp.sqrt(v_hat) + ADAM_EPS) + ADAM_WD * w)
    return delta, m, v


def _adam_call(w, g, m, v, name, comm=None):
    R, C = w.shape
    tr = _row_tile(R, 512)

    def body(w_ref, g_ref, m_ref, v_ref, d_ref, mo_ref, vo_ref):
        d, mn, vn = _adamw(w_ref[...], g_ref[...], m_ref[...], v_ref[...])
        d_ref[...] = d
        mo_ref[...] = mn
        vo_ref[...] = vn

    blk = pl.BlockSpec((tr, C), lambda i: (i, 0))
    shape = jax.ShapeDtypeStruct((R, C), F32)
    return _call(body, name=name, grid=(R // tr,), in_specs=[blk] * 4, out_specs=[blk] * 3, out_shape=[shape] * 3,
                 operands=(w, g, m, v), comm=comm)


ADAM_GROUP_STEPS = 8


def _adam_group(ws, gs, ms, vs, name, comm=None):
    n = len(ws)

    def body(*refs):
        ins, outs = refs[:4 * n], refs[4 * n:]
        for i in range(n):
            d, mn, vn = _adamw(*(r[...] for r in ins[4 * i:4 * i + 4]))
            outs[3 * i][...] = d
            outs[3 * i + 1][...] = mn
            outs[3 * i + 2][...] = vn

    operands, in_specs, out_specs, out_shape = [], [], [], []
    for w, g, m, v in zip(ws, gs, ms, vs):
        R, C = w.shape
        blk = pl.BlockSpec((R // ADAM_GROUP_STEPS, C), lambda i: (i, 0))
        operands += [w, g, m, v]
        in_specs += [blk] * 4
        out_specs += [blk] * 3
        out_shape += [jax.ShapeDtypeStruct((R, C), F32)] * 3
    outs, comm_outs = _call(body, name=name, grid=(ADAM_GROUP_STEPS,), in_specs=in_specs, out_specs=out_specs,
                            out_shape=out_shape, operands=operands, comm=comm)
    return [tuple(outs[3 * i:3 * i + 3]) for i in range(n)], comm_outs


def _ada_adam(c_act_t, dmod_cols, w, m, v, comm):
    R, C = w.shape
    nbatch = c_act_t.shape[1]
    tr = _tile(R, 128)

    def body(ct_ref, dm_ref, w_ref, m_ref, v_ref, g_ref, d_ref, mo_ref, vo_ref):
        cv = ct_ref[...]
        g = _dot_nn((cv * _sigmoid(cv)).astype(BF16), dm_ref[...].astype(BF16))
        g_ref[...] = g
        d, mn, vn = _adamw(w_ref[...], g, m_ref[...], v_ref[...])
        d_ref[...] = d
        mo_ref[...] = mn
        vo_ref[...] = vn

    blk = pl.BlockSpec((tr, C), lambda i: (i, 0))
    shape = jax.ShapeDtypeStruct((R, C), F32)
    return _call(
        body, name="ada_adam", grid=(R // tr,),
        in_specs=[pl.BlockSpec((tr, nbatch), lambda i: (i, 0)), pl.BlockSpec((nbatch, C), lambda i: (0, 0)),
                  blk, blk, blk],
        out_specs=[blk] * 4, out_shape=[shape] * 4,
        operands=(c_act_t, dmod_cols, w, m, v), comm=comm)


def _small_adam(gathered, w, m, v, rows_b0, rows_b1, rows_vec):
    _, P, D = gathered.shape
    R = w.shape[0]

    def body(ga_ref, w_ref, m_ref, v_ref, sum_ref, g_ref, d_ref, mo_ref, vo_ref):
        total = ga_ref[0]
        for dev in range(1, N_DEV):
            total = total + ga_ref[dev]
        sum_ref[...] = total
        g_ref[...] = jnp.zeros_like(g_ref)
        g_ref[0:N_MOD, :] = (sum_ref[rows_b0:rows_b0 + N_MOD, :] + sum_ref[rows_b1:rows_b1 + N_MOD, :])
        g_ref[N_MOD:N_MOD + 8, :] = sum_ref[rows_vec:rows_vec + 8, :]
        d, mn, vn = _adamw(w_ref[...], g_ref[...], m_ref[...], v_ref[...])
        d_ref[...] = d
        mo_ref[...] = mn
        vo_ref[...] = vn

    shape = jax.ShapeDtypeStruct((R, D), F32)
    return pl.pallas_call(
        body, name="small_adam",
        in_specs=[VMEM_SPEC] * 4, out_specs=[VMEM_SPEC] * 5,
        out_shape=[jax.ShapeDtypeStruct((P, D), F32), shape, shape, shape, shape],
        compiler_params=pltpu.CompilerParams(vmem_limit_bytes=VMEM_LIMIT),
    )(gathered, w, m, v)


def _gather8(v, name):
    A, W = v.shape
    flips = [(fx, fy, fc) for fx in (0, 1) for fy in (0, 1) for fc in (0, 1) if (fx, fy, fc) != (0, 0, 0)]

    def body(v_ref, out_ref, send_sems, recv_sems, local_sem):
        x, y, c = _position()
        me = 4 * x + 2 * y + c
        mine = pltpu.make_async_copy(v_ref, out_ref.at[me], local_sem)
        mine.start()

        def copy(k, block, to):
            return pltpu.make_async_remote_copy(src_ref=v_ref, dst_ref=out_ref.at[block], send_sem=send_sems.at[k],
                                                recv_sem=recv_sems.at[k], device_id=to, device_id_type=MESH)

        peers = [(_flip(x, fx), _flip(y, fy), _flip(c, fc)) for fx, fy, fc in flips]
        sends = [copy(k, me, peer) for k, peer in enumerate(peers)]
        for cp in sends:
            cp.start()
        for k, (px, py, pc) in enumerate(peers):
            copy(k, 4 * px + 2 * py + pc, (px, py, pc)).wait_recv()
        for cp in sends:
            cp.wait_send()
        mine.wait()

    return pl.pallas_call(
        body, name=name, in_specs=[VMEM_SPEC], out_specs=VMEM_SPEC,
        out_shape=jax.ShapeDtypeStruct((N_DEV, A, W), v.dtype),
        scratch_shapes=[pltpu.SemaphoreType.DMA((N_DEV - 1,)), pltpu.SemaphoreType.DMA((N_DEV - 1,)),
                        pltpu.SemaphoreType.DMA],
    )(v)


def _mod_exchange(part):
    _, A, W = part.shape

    def body(p_ref, out_ref, send_sems, recv_sems, local_sem):
        x, y, c = _position()
        me = 4 * x + 2 * y + c
        chip = 2 * x + y
        mine = pltpu.make_async_copy(p_ref.at[me], out_ref.at[chip], local_sem)
        mine.start()
        peers = [(_flip(x, fx), _flip(y, fy)) for fx, fy in CHIP_FLIPS]
        sends = []
        for k, (px, py) in enumerate(peers):
            sends.append(pltpu.make_async_remote_copy(
                src_ref=p_ref.at[4 * px + 2 * py + c], dst_ref=out_ref.at[chip], send_sem=send_sems.at[k],
                recv_sem=recv_sems.at[k], device_id=(px, py, c), device_id_type=MESH))
        for cp in sends:
            cp.start()
        for k, (px, py) in enumerate(peers):
            pltpu.make_async_remote_copy(
                src_ref=p_ref.at[me], dst_ref=out_ref.at[2 * px + py], send_sem=send_sems.at[k],
                recv_sem=recv_sems.at[k], device_id=(px, py, c), device_id_type=MESH).wait_recv()
        for cp in sends:
            cp.wait_send()
        mine.wait()

    return pl.pallas_call(
        body, name="mod_exchange", in_specs=[VMEM_SPEC], out_specs=VMEM_SPEC,
        out_shape=jax.ShapeDtypeStruct((N_CHIP, A, W), part.dtype),
        scratch_shapes=[pltpu.SemaphoreType.DMA((3,)), pltpu.SemaphoreType.DMA((3,)), pltpu.SemaphoreType.DMA],
    )(part)


KV_ROWS = 4 * HEAD_DIM


def _kernel_row_order(w_in_t):
    R, C = w_in_t.shape
    n_blocks = R // KV_ROWS
    q_blocks = D_MODEL // KV_ROWS

    def source(t):
        return jnp.where(t < q_blocks, t, jnp.where(t < n_blocks - 1, t + 1, q_blocks))

    def body(w_ref, o_ref):
        o_ref[...] = w_ref[...]

    return pl.pallas_call(
        body, name="w_in_row_order", grid=(n_blocks,),
        in_specs=[pl.BlockSpec((KV_ROWS, C), lambda t: (source(t), 0))],
        out_specs=pl.BlockSpec((KV_ROWS, C), lambda t: (t, 0)),
        out_shape=jax.ShapeDtypeStruct((R, C), w_in_t.dtype), compiler_params=_params(1),
    )(w_in_t)


def _cast_group(ws, names, comm):
    n = len(ws)
    steps = 4

    def body(*refs):
        w_refs, out_refs, stage, sem = refs[:n], refs[n:2 * n], refs[2 * n:3 * n], refs[3 * n]
        x, y, _ = _position()
        step = pl.program_id(0)
        copies = []
        for i in range(n):
            rows = ws[i].shape[0] // steps
            stage[i][...] = w_refs[i][...].astype(BF16)
            copies.append(pltpu.make_async_copy(
                stage[i], out_refs[i].at[2 * x + y, pl.ds(step * rows, rows), :], sem.at[i]))
        for cp in copies:
            cp.start()
        for cp in copies:
            cp.wait()

    outs, comm_outs = _call(
        body, name="cast_" + "_".join(names), grid=(steps,),
        in_specs=[pl.BlockSpec((w.shape[0] // steps, w.shape[1]), lambda i: (i, 0)) for w in ws],
        out_specs=[ANY] * n, out_shape=[jax.ShapeDtypeStruct((N_CHIP,) + w.shape, BF16) for w in ws],
        scratch_shapes=[pltpu.VMEM((w.shape[0] // steps, w.shape[1]), BF16) for w in ws]
        + [pltpu.SemaphoreType.DMA((n,))],
        operands=ws, comm=comm)
    return outs, comm_outs


def _cast_slot(w, chip_idx, name):
    R, C = w.shape
    tr = _row_tile(R, 512)

    def body(chip_ref, w_ref, o_ref):
        o_ref[...] = w_ref[...].astype(BF16)

    return pl.pallas_call(
        body, name=name,
        grid_spec=pltpu.PrefetchScalarGridSpec(
            num_scalar_prefetch=1, grid=(R // tr,),
            in_specs=[pl.BlockSpec((tr, C), lambda i, chip_ref: (i, 0))],
            out_specs=pl.BlockSpec((None, tr, C), lambda i, chip_ref: (chip_ref[0], i, 0))),
        out_shape=jax.ShapeDtypeStruct((N_CHIP, R, C), BF16),
        compiler_params=_params(1),
    )(chip_idx, w)


def _pair_sum(g32, recv, core, name):
    J, _, r, C = g32.shape

    def body(core_ref, g_ref, r_ref, o_ref):
        o_ref[...] = (g_ref[...] + r_ref[...].astype(F32)).astype(BF16)

    return pl.pallas_call(
        body, name=name,
        grid_spec=pltpu.PrefetchScalarGridSpec(
            num_scalar_prefetch=1, grid=(J,),
            in_specs=[pl.BlockSpec((None, None, r, C), lambda j, core_ref: (j, core_ref[0], 0, 0)),
                      pl.BlockSpec((None, r, C), lambda j, core_ref: (j, 0, 0))],
            out_specs=pl.BlockSpec((None, r, C), lambda j, core_ref: (j, 0, 0))),
        out_shape=jax.ShapeDtypeStruct((J, r, C), BF16),
        compiler_params=_params(1),
    )(core, g32, recv)


def _chip_sum(g32, recv_sib, recv_chips, core_chip, name):
    J, _, r, C = g32.shape

    def body(idx_ref, g_ref, s_ref, o_ref_in, o_ref):
        total = g_ref[...] + s_ref[...].astype(F32)
        for k in range(3):
            total = total + o_ref_in[k].astype(F32)
        o_ref[...] = total

    return pl.pallas_call(
        body, name=name,
        grid_spec=pltpu.PrefetchScalarGridSpec(
            num_scalar_prefetch=1, grid=(1,),
            in_specs=[pl.BlockSpec((None, None, r, C), lambda i, idx: (idx[1], idx[0], 0, 0)),
                      pl.BlockSpec((None, r, C), lambda i, idx: (idx[1], 0, 0)),
                      pl.BlockSpec((3, r, C), lambda i, idx: (0, 0, 0))],
            out_specs=pl.BlockSpec((None, r, C), lambda i, idx: (idx[0], 0, 0))),
        out_shape=jax.ShapeDtypeStruct((2, r, C), F32),
        compiler_params=_params(1),
    )(core_chip, g32, recv_sib, recv_chips)


ICI_US_PER_ELEMENT = 4.6e-5


class _Reducer:
    def __init__(self, core_idx, core_chip):
        self.core_idx, self.core_chip = core_idx, core_chip
        self.grads, self.halves, self.reduced = {}, {}, {}
        self.ready_swap, self.ready_exchange, self.ready_join = [], [], []
        self.inflight, self.current = ([], [], [], None), None
        self.flushes = 0
        self.extra, self.extra_out = None, None

    def add(self, name, grad_pair):
        self.grads[name] = grad_pair
        self.ready_swap.append(name)

    def comm(self, budget_us):
        swaps, self.ready_swap = self.ready_swap, []
        joins, self.ready_join = self.ready_join, []
        exchanges, waiting = [], []
        for item in self.ready_exchange:
            cost = ICI_US_PER_ELEMENT * 2 * item[2].shape[1] * item[2].shape[2]
            if cost <= budget_us:
                exchanges.append(item)
                budget_us -= cost
            else:
                waiting.append(item)
        self.ready_exchange = waiting
        parts = []
        if swaps:
            parts.append(_SwapComm([self.grads[n][1] for n in swaps]))
        if exchanges:
            parts.append(_ExchangeComm([pair for _, _, pair in exchanges]))
        if joins:
            parts.append(_JoinComm([self.halves[n] for n in joins]))
        extra, self.extra = self.extra, None
        if extra is not None:
            parts.append(extra)
        self.inflight = (swaps, exchanges, joins, extra)
        self.current = _CommList(parts) if parts else None
        return self.current

    def done(self, comm_outs):
        if self.current is None:
            return
        swaps, exchanges, joins, extra = self.inflight
        outs = iter(self.current.split_outputs(list(comm_outs)))
        if swaps:
            for n, recv in zip(swaps, next(outs)):
                pair = _pair_sum(self.grads[n][0], recv, self.core_idx, "pair_sum_" + n)
                self.ready_exchange.append((n, recv, pair))
        if exchanges:
            for (n, recv, _), chips in zip(exchanges, next(outs)):
                self.halves[n] = _chip_sum(self.grads[n][0], recv, chips, self.core_chip, "chip_sum_" + n)
                self.ready_join.append(n)
        if joins:
            self.reduced.update(zip(joins, next(outs)))
        if extra is not None:
            self.extra_out = next(outs)
        self.current = None

    def run(self, kernel, budget_us, *args, **kwargs):
        if budget_us is None:
            return kernel(*args, comm=None, **kwargs)[0]
        outs, comm_outs = kernel(*args, comm=self.comm(budget_us), **kwargs)
        self.done(comm_outs)
        return outs

    def step(self):
        comm = self.comm(float("inf"))
        self.flushes += 1
        self.done(_run_comm(comm, "grad_reduce_tail_%d" % self.flushes))


BIG_WEIGHTS = ("ffn1_w_gate", "ffn1_w_up", "ffn1_w_down", "w_in", "w_attn_o", "w_conv_o", "w_out",
               "ffn2_w_gate", "ffn2_w_up", "ffn2_w_down")
VECTORS = ("norm_ffn1_g", "norm_mix_g", "conv_b_dw", "conv_ln_g", "conv_ln_b", "norm_ffn2_g", "final_norm_g")
ROW_DMOD0, ROW_DMOD1, ROW_VEC, ROW_SINK, ROW_CONVW, SMALL_ROWS = 0, 16, 33, 40, 41, 72


FFN1_WEIGHTS = ("ffn1_w_gate", "ffn1_w_up", "ffn1_w_down")
FFN2_WEIGHTS = ("ffn2_w_gate", "ffn2_w_up", "ffn2_w_down")
MIX_WEIGHTS = ("w_in", "w_attn_o", "w_conv_o", "w_out")
COL_SHARDED = ("ffn1_w_gate", "ffn1_w_up", "ffn2_w_gate", "ffn2_w_up", "w_in")


def _local_grads(x, target, mod, slots, ffn1_gathered, small, seq, core_idx, core_chip):
    T, D = x.shape
    B = T // seq
    mods = [mod[:, k][:, None, :] for k in range(N_MOD)]
    sh1, sc1, g1, sh2, sc2, g2, sh3, sc3, g3 = mods
    w = dict(zip(FFN1_WEIGHTS, ffn1_gathered))

    (h1, a1, u1, f1, x1), outs = _ffn_fwd(
        x, small["norm_ffn1_g"], sc1, sh1, g1, w["ffn1_w_gate"], w["ffn1_w_up"], w["ffn1_w_down"], seq, "ffn1_fwd",
        comm=_GatherComm([slots[n] for n in MIX_WEIGHTS]))
    w["w_in"] = outs[0]
    w_ao, w_co, w_o = [t.reshape(D, D) for t in outs[1:]]
    w_in_cols = _kernel_row_order(w["w_in"].reshape(IN_WIDTH, D))
    (h2, proj), _ = _in_proj(x1, small["norm_mix_g"], sc2, sh2, w_in_cols, seq)
    (o, lse), (w["ffn2_w_gate"], w["ffn2_w_up"]) = _attn_fwd(
        proj, small["attn_sinks"], B, seq, comm=_GatherComm([slots["ffn2_w_gate"], slots["ffn2_w_up"]]))
    (ydw, z), (w["ffn2_w_down"],) = _conv_fwd(
        proj, small["conv_w_dw"], small["conv_b_dw"], small["conv_ln_g"], small["conv_ln_b"], B, seq,
        comm=_GatherComm([slots["ffn2_w_down"]]))
    ya, yc, merged, mo, x2 = _merge(o, z, proj, w_ao, w_co, w_o, x1, g2, seq)
    (h3, a3, u3, f3, x3), _ = _ffn_fwd(x2, small["norm_ffn2_g"], sc3, sh3, g3, w["ffn2_w_gate"], w["ffn2_w_up"],
                                       w["ffn2_w_down"], seq, "ffn2_fwd")
    dx3, loss_parts, d_final_g = _final_loss(x3, small["final_norm_g"], target)

    red = _Reducer(core_idx, core_chip)

    def weight_grad(name, budget_us, a, a_spec, b, b_spec, rows, cols):
        red.add(name, red.run(_wgrad, budget_us, a, a_spec, b, b_spec, rows, cols, T, "dw_" + name))

    def ffn_backward(prefix, dw_budget_us, dxo, xin, h, a, u, f, gn, sc, gate, before_weight_grads=None):
        da, du, s, df, dx, dgate, dsc, dsh, dgn = red.run(
            _ffn_bwd, 170, dxo, xin, f, a, u, gn, sc, gate, w[prefix + "_w_gate"], w[prefix + "_w_up"],
            w[prefix + "_w_down"], seq, prefix + "_bwd")
        if before_weight_grads is not None:
            before_weight_grads(dgate, dsc, dsh, dgn)
        weight_grad(prefix + "_w_down", dw_budget_us, s, _spec_chip_major(FF_SHARD), df, _spec_rows(D), FF_SHARD, D)
        weight_grad(prefix + "_w_gate", dw_budget_us, da, _spec_chip_major(FF_SHARD), h, _spec_rows(D), FF_SHARD, D)
        weight_grad(prefix + "_w_up", dw_budget_us, du, _spec_chip_major(FF_SHARD), h, _spec_rows(D), FF_SHARD, D)
        return dx, dgate, dsc, dsh, dgn

    dx2, dg3, dsc3, dsh3, d_gn3 = ffn_backward("ffn2", None, dx3, x2, h3, a3, u3, f3, small["norm_ffn2_g"], sc3, g3)

    dmo, dya, dyc, dga, dgc, do, dz, dg2 = red.run(_merge_bwd, 45, dx2, mo, g2, proj, ya, yc, w_o, w_ao, w_co, seq)
    shard = D // N_CHIP
    weight_grad("w_out", None, merged, _spec_col_block(shard), dmo, _spec_rows(D), shard, D)
    weight_grad("w_attn_o", None, o, _spec_col_block(shard), dya, _spec_rows(D), shard, D)
    weight_grad("w_conv_o", None, z, _spec_col_block(shard), dyc, _spec_rows(D), shard, D)
    dq, dkp, dko, dvp, dvo, dsink_steps = red.run(_attn_bwd, 100, proj, small["attn_sinks"], o, do, lse, B, seq)
    dca, dcb, d_conv_w, d_conv_b, d_ln_g, d_ln_b = red.run(
        _conv_bwd, 165, proj, dz, ydw, small["conv_w_dw"], small["conv_ln_g"], small["conv_ln_b"], B, seq)

    def band_sum(own, prev):
        prev = prev.reshape(B, seq // BLOCK, BLOCK, 2 * HEAD_DIM)
        moved = jnp.concatenate([prev[:, 1:], jnp.zeros_like(prev[:, :1])], axis=1)
        return (own + moved.reshape(T, 2 * HEAD_DIM)).astype(BF16)

    dkv = jnp.concatenate([band_sum(dko, dkp), band_sum(dvo, dvp)], axis=1)
    g32, g16 = lax.empty((IN_WIDTH, D), F32), lax.empty((IN_WIDTH, D), BF16)
    row_of = {"q": 0, "kv": D, "conv_a": D + 4 * HEAD_DIM, "conv_b": 2 * D + 4 * HEAD_DIM,
              "gate_a": 3 * D + 4 * HEAD_DIM, "gate_c": 4 * D + 4 * HEAD_DIM}
    for tag, piece in (("q", dq), ("kv", dkv), ("conv_a", dca), ("conv_b", dcb), ("gate_a", dga), ("gate_c", dgc)):
        g32, g16 = _wgrad_rows(piece, h2, g32, g16, row_of[tag], "dw_w_in_" + tag)
    red.add("w_in", tuple(g.reshape(N_CHIP, 2, IN_SHARD // 2, D) for g in (g32, g16)))
    dx1, dsc2, dsh2, d_gn2 = red.run(_in_proj_bwd, 90, (dq, dca, dcb, dga, dgc, dkv), w_in_cols, x1,
                                     small["norm_mix_g"], sc2, dx2, seq)

    def gather_small_grads(dg1, dsc1, dsh1, d_gn1):
        dmod = jnp.concatenate([dsh1, dsc1, dg1, dsh2, dsc2, dg2, dsh3, dsc3, dg3], axis=1)
        d_sinks = jnp.sum(dsink_steps, axis=0)
        vec_grads = {"norm_ffn1_g": d_gn1, "norm_mix_g": d_gn2, "conv_b_dw": d_conv_b, "conv_ln_g": d_ln_g,
                     "conv_ln_b": d_ln_b, "norm_ffn2_g": d_gn3, "final_norm_g": d_final_g}
        block = jnp.zeros((SMALL_ROWS, D), F32)
        block = block.at[ROW_DMOD0:ROW_DMOD0 + N_MOD].set(dmod[0]).at[ROW_DMOD1:ROW_DMOD1 + N_MOD].set(dmod[1])
        block = block.at[ROW_VEC:ROW_VEC + len(VECTORS)].set(jnp.concatenate([vec_grads[n] for n in VECTORS], axis=0))
        block = block.at[ROW_SINK, :2 * HEAD_DIM].set(d_sinks[0])
        block = block.at[ROW_CONVW:ROW_CONVW + CONV_WIDTH].set(d_conv_w[:CONV_WIDTH])
        red.extra = _Gather8Comm(block)

    dx0, _, _, _, _ = ffn_backward("ffn1", 38, dx1, x, h1, a1, u1, f1, small["norm_ffn1_g"], sc1, g1,
                                   before_weight_grads=gather_small_grads)
    return loss_parts, dx0, red, red.extra_out[0]


def kernel(x, c, w_ada, b_ada, norm_ffn1_g, ffn1_w_gate, ffn1_w_up, ffn1_w_down, norm_mix_g, w_in, attn_sinks, w_attn_o, conv_w_dw, conv_b_dw, conv_ln_g, conv_ln_b, w_conv_o, w_out, norm_ffn2_g, ffn2_w_gate, ffn2_w_up, ffn2_w_down, final_norm_g, loss_target, m_w_ada, m_b_ada, m_norm_ffn1_g, m_ffn1_w_gate, m_ffn1_w_up, m_ffn1_w_down, m_norm_mix_g, m_w_in, m_attn_sinks, m_w_attn_o, m_conv_w_dw, m_conv_b_dw, m_conv_ln_g, m_conv_ln_b, m_w_conv_o, m_w_out, m_norm_ffn2_g, m_ffn2_w_gate, m_ffn2_w_up, m_ffn2_w_down, m_final_norm_g, v_w_ada, v_b_ada, v_norm_ffn1_g, v_ffn1_w_gate, v_ffn1_w_up, v_ffn1_w_down, v_norm_mix_g, v_w_in, v_attn_sinks, v_w_attn_o, v_conv_w_dw, v_conv_b_dw, v_conv_ln_g, v_conv_ln_b, v_w_conv_o, v_w_out, v_norm_ffn2_g, v_ffn2_w_gate, v_ffn2_w_up, v_ffn2_w_down, v_final_norm_g):
    args = dict(locals())
    B, seq, D = x.shape
    T = B * seq
    xi, yi, ci = _position()
    chip = 2 * xi + yi
    dev = 4 * xi + 2 * yi + ci

    def shard_2d(prefix, name):
        t = args[prefix + name][0]
        return t.T if name in COL_SHARDED else t

    big = {n: shard_2d("", n) for n in BIG_WEIGHTS}
    final_g = final_norm_g[None, :]
    vec_w = {n: (args[n] if n != "final_norm_g" else final_g) for n in VECTORS}

    conv_cols = D // N_CHIP
    conv_flat = jnp.pad(conv_w_dw[0].reshape(-1), (0, 8 * D - CONV_WIDTH * conv_cols)).reshape(8, D)
    first = _gather8(jnp.concatenate([jnp.pad(c, ((0, 8 - B), (0, 0))), conv_flat], axis=0), "gather_c")
    c_all = first[:, :B].reshape(N_DEV * B, D)
    conv_taps = first[::2, 8:].reshape(N_CHIP, 8 * D)[:, :CONV_WIDTH * conv_cols]
    conv_taps = conv_taps.reshape(N_CHIP, CONV_WIDTH, conv_cols).transpose(1, 0, 2).reshape(CONV_WIDTH, D)
    conv_taps = jnp.pad(conv_taps, ((0, CONV_PAD - CONV_WIDTH), (0, 0)))

    ada_cols = w_ada.shape[2]
    b_cols = lax.dynamic_slice(b_ada, (0, chip * ada_cols), (1, ada_cols))
    mod_part = _ada_fwd(c_all, w_ada[0], b_cols).reshape(N_DEV, B, ada_cols)
    mod = _mod_exchange(mod_part).transpose(1, 0, 2).reshape(B, N_MOD, D)

    core_idx = jnp.reshape(ci, (1,)).astype(jnp.int32)
    chip_idx = jnp.reshape(chip, (1,)).astype(jnp.int32)
    core_chip = jnp.stack([ci, chip]).astype(jnp.int32)
    slots = {n: _cast_slot(big[n], chip_idx, "cast_" + n) for n in FFN1_WEIGHTS}
    later = [n for n in BIG_WEIGHTS if n not in FFN1_WEIGHTS]
    later_slots, ffn1_gathered = _cast_group([big[n] for n in later], ["later_weights"],
                                             _GatherComm([slots[n] for n in FFN1_WEIGHTS]))
    slots.update(zip(later, later_slots))

    small = dict(vec_w)
    small["attn_sinks"] = attn_sinks
    small["conv_w_dw"] = conv_taps

    loss_parts, dx, red, small_all = _local_grads(
        x.reshape(T, D), loss_target.reshape(T, D), mod, slots, ffn1_gathered, small, seq, core_idx, core_chip)

    loss = lax.psum((0.5 / D) * jnp.sum(loss_parts), ("x", "y", "c"))
    grad_x = dx.reshape(B, seq, D)
    out = {}


    def pack_small(prefix):
        rows = [args[prefix + "b_ada"].reshape(N_MOD, D)]
        rows += [args[prefix + n].reshape(1, D) for n in VECTORS]
        rows += [jnp.pad(args[prefix + "attn_sinks"], ((0, 0), (0, D - N_Q_HEADS)))]
        return jnp.pad(jnp.concatenate(rows, axis=0), ((0, 24 - N_MOD - len(VECTORS) - 1), (0, 0)))

    small_sum, sg, sd, sm, sv = _small_adam(small_all, pack_small(""), pack_small("m_"), pack_small("v_"),
                                           ROW_DMOD0, ROW_DMOD1, ROW_VEC)

    def unpack_small(t):
        res = {"b_ada": t[:N_MOD].reshape(1, N_MOD * D)}
        for k, n in enumerate(VECTORS):
            res[n] = t[N_MOD + k].reshape(args[n].shape)
        res["attn_sinks"] = t[N_MOD + len(VECTORS), :N_Q_HEADS].reshape(1, N_Q_HEADS)
        return res

    unpacked = [unpack_small(t) for t in (sg, sd, sm, sv)]
    for n in ("b_ada", "attn_sinks") + VECTORS:
        out[n] = tuple(u[n] for u in unpacked)

    conv_g = lax.dynamic_slice(small_sum, (ROW_CONVW, chip * conv_cols), (CONV_WIDTH, conv_cols))
    d, mn, vn = red.run(_adam_call, None, conv_w_dw[0], conv_g, m_conv_w_dw[0], v_conv_w_dw[0], "adam_conv_w_dw")
    out["conv_w_dw"] = tuple(t[None] for t in (conv_g, d, mn, vn))

    dmod_rows = jnp.stack([small_all[:, ROW_DMOD0:ROW_DMOD0 + N_MOD], small_all[:, ROW_DMOD1:ROW_DMOD1 + N_MOD]], axis=1)
    dmod_all = dmod_rows.reshape(N_DEV * B, N_MOD * D)
    dmod_cols = lax.dynamic_slice(dmod_all, (0, chip * ada_cols), (N_DEV * B, ada_cols))
    ada_out = red.run(_ada_adam, 35, c_all.T, dmod_cols, w_ada[0], m_w_ada[0], v_w_ada[0])
    out["w_ada"] = tuple(t[None] for t in ada_out)

    def finished(n):
        while n not in red.reduced:
            red.step()
        return red.reduced[n].reshape(big[n].shape)

    def emit(n, g, d, mn, vn):
        out[n] = tuple((t.T if n in COL_SHARDED else t)[None] for t in (g, d, mn, vn))

    early = FFN2_WEIGHTS + MIX_WEIGHTS
    early_g = [finished(n) for n in early]
    early_out = red.run(_adam_group, 45, [big[n] for n in early], early_g, [shard_2d("m_", n) for n in early],
                        [shard_2d("v_", n) for n in early], "adam_early")
    for n, g, (d, mn, vn) in zip(early, early_g, early_out):
        emit(n, g, d, mn, vn)
    for n in ("ffn1_w_down", "ffn1_w_gate", "ffn1_w_up"):
        g = finished(n)
        emit(n, g, *red.run(_adam_call, None, big[n], g, shard_2d("m_", n), shard_2d("v_", n), "adam_" + n))

    order = ("w_ada", "b_ada", "norm_ffn1_g", "ffn1_w_gate", "ffn1_w_up", "ffn1_w_down", "norm_mix_g", "w_in",
             "attn_sinks", "w_attn_o", "conv_w_dw", "conv_b_dw", "conv_ln_g", "conv_ln_b", "w_conv_o", "w_out",
             "norm_ffn2_g", "ffn2_w_gate", "ffn2_w_up", "ffn2_w_down", "final_norm_g")
    return (loss, grad_x, *[out[n][0] for n in order], *[out[n][1] for n in order],
            *[out[n][2] for n in order], *[out[n][3] for n in order])
```

```python
import functools

import jax
import jax.numpy as jnp
from jax import lax
from jax.experimental import pallas as pl
from jax.experimental.pallas import tpu as pltpu

F32 = jnp.float32
BF16 = jnp.bfloat16

D_MODEL = 1024
D_FF = 2816
N_CHIP = 4
N_DEV = 8
FF_SHARD = D_FF // N_CHIP
IN_WIDTH = 5376
IN_SHARD = IN_WIDTH // N_CHIP
HEAD_DIM = 64
N_Q_HEADS = 16
N_KV_HEADS = 2
BLOCK = 128
CONV_WIDTH = 31
CONV_PAD = 32
N_MOD = 9
EPS = 1e-6
FFN_RESIDUAL = 0.5
ATTN_SCALE = HEAD_DIM ** -0.5
MASK_VALUE = -1e30

ADAM_LR = 0.001
ADAM_B1 = 0.9
ADAM_B2 = 0.999
ADAM_EPS = 1e-08
ADAM_WD = 0.01
ADAM_STEP = 10

COLB_Q, COLB_CA, COLB_CB, COLB_GA, COLB_GC = 0, 1, 2, 3, 4
COLB_K, COLB_V = 40, 41
PROJ_TILE = 768

VMEM_LIMIT = 56 * 1024 * 1024
MESH = pl.DeviceIdType.MESH
ANY = pl.BlockSpec(memory_space=pl.ANY)
VMEM_SPEC = pl.BlockSpec(memory_space=pltpu.VMEM)
SMEM_SPEC = pl.BlockSpec(memory_space=pltpu.SMEM)


def _params(n_grid):
    return pltpu.CompilerParams(dimension_semantics=("arbitrary",) * n_grid, vmem_limit_bytes=VMEM_LIMIT)


def _tile(n, pref):
    t = min(n, pref)
    while n % t:
        t //= 2
    return t


def _row_tile(rows, cap):
    for t in range(min(rows, cap) // 16 * 16, 0, -16):
        if rows % t == 0:
            return t
    return rows


def _sigmoid(v):
    return 1.0 / (1.0 + jnp.exp(-v))


def _dot_nn(a, b):
    return lax.dot_general(a, b, (((1,), (0,)), ((), ())), preferred_element_type=F32)


def _dot_nt(a, b):
    return lax.dot_general(a, b, (((1,), (1,)), ((), ())), preferred_element_type=F32)


def _dot_tn(a, b):
    return lax.dot_general(a, b, (((0,), (0,)), ((), ())), preferred_element_type=F32)


ROW_CHUNK = 16


def _for_row_chunks(n_rows, fn):
    for r in range(0, n_rows, ROW_CHUNK):
        fn(slice(r, r + ROW_CHUNK))


def _norm_mod(xv, gn, sc, sh):
    r = lax.rsqrt(jnp.mean(xv * xv, axis=-1, keepdims=True) + EPS)
    return ((xv * r) * gn) * (1.0 + sc) + sh


def _accumulate(ref, first, value):
    @pl.when(first)
    def _():
        ref[...] = value

    @pl.when(jnp.logical_not(first))
    def _():
        ref[...] += value


def _norm_mod_bwd(dh, xv, gn, sc, dxo, first_of_batch, first, dx_ref, dsc_ref, dsh_ref, dgn_ref):
    r = lax.rsqrt(jnp.mean(xv * xv, axis=-1, keepdims=True) + EPS)
    xh = xv * r
    _accumulate(dsh_ref, first_of_batch, jnp.sum(dh, axis=0, keepdims=True))
    _accumulate(dsc_ref, first_of_batch, jnp.sum(dh * (xh * gn), axis=0, keepdims=True))
    dn = dh * (1.0 + sc)
    _accumulate(dgn_ref, first, jnp.sum(dn * xh, axis=0, keepdims=True))
    dxh = dn * gn
    dx_ref[...] = dxo + r * (dxh - xh * jnp.mean(dxh * xh, axis=-1, keepdims=True))


CHIP_FLIPS = ((1, 0), (0, 1), (1, 1))


def _position():
    return lax.axis_index("x"), lax.axis_index("y"), lax.axis_index("c")


def _flip(v, f):
    return 1 - v if f else v


class _GatherComm:
    def __init__(self, bufs):
        n = len(bufs)
        self.n = n
        self.operands = list(bufs)
        self.out_shape = [jax.ShapeDtypeStruct(b.shape, b.dtype) for b in bufs]
        self.aliases = {i: i for i in range(n)}
        self.sems = [pltpu.SemaphoreType.DMA((6 * n,)), pltpu.SemaphoreType.DMA((6 * n,))]
        self.rows = [b.shape[1] // 2 for b in bufs]

    def _half(self, ref, i, which):
        return ref.at[pl.ds(which * self.rows[i], self.rows[i]), :]

    def _ici(self, cins, couts, sems, i, k, dst_chip, to):
        x, y, c = _position()
        return pltpu.make_async_remote_copy(
            src_ref=self._half(cins[i].at[2 * x + y], i, c), dst_ref=self._half(couts[i].at[dst_chip], i, c),
            send_sem=sems[0].at[3 * i + k], recv_sem=sems[1].at[3 * i + k], device_id=to, device_id_type=MESH)

    def _d2d(self, couts, sems, i, k, src_chip, which):
        x, y, c = _position()
        place = self._half(couts[i].at[src_chip], i, which)
        return pltpu.make_async_remote_copy(
            src_ref=place, dst_ref=place, send_sem=sems[0].at[3 * self.n + 3 * i + k],
            recv_sem=sems[1].at[3 * self.n + 3 * i + k], device_id=(x, y, 1 - c), device_id_type=MESH)

    def _peers(self):
        x, y, _ = _position()
        return [(_flip(x, fx), _flip(y, fy)) for fx, fy in CHIP_FLIPS]

    def start(self, cins, couts, sems):
        x, y, c = _position()
        for i in range(self.n):
            for k, (px, py) in enumerate(self._peers()):
                self._ici(cins, couts, sems, i, k, 2 * x + y, (px, py, c)).start()

    def finish(self, cins, couts, sems):
        _, _, c = _position()
        peers = self._peers()
        for i in range(self.n):
            for k, (px, py) in enumerate(peers):
                self._ici(cins, couts, sems, i, k, 2 * px + py, (px, py, c)).wait_recv()
                self._d2d(couts, sems, i, k, 2 * px + py, c).start()
        for i in range(self.n):
            for k, (px, py) in enumerate(peers):
                self._d2d(couts, sems, i, k, 2 * px + py, 1 - c).wait_recv()
        for i in range(self.n):
            for k, (px, py) in enumerate(peers):
                self._ici(cins, couts, sems, i, k, 2 * px + py, (px, py, c)).wait_send()
                self._d2d(couts, sems, i, k, 2 * px + py, c).wait_send()


class _ExchangeComm:
    def __init__(self, pairs):
        n = len(pairs)
        self.n = n
        self.operands = list(pairs)
        self.out_shape = [jax.ShapeDtypeStruct((3,) + p.shape[1:], p.dtype) for p in pairs]
        self.aliases = {}
        self.sems = [pltpu.SemaphoreType.DMA((3 * n,)), pltpu.SemaphoreType.DMA((3 * n,))]

    def _copies(self, cins, couts, sems):
        x, y, c = _position()
        peers = [(_flip(x, fx), _flip(y, fy)) for fx, fy in CHIP_FLIPS]
        return [pltpu.make_async_remote_copy(
            src_ref=cins[i].at[2 * px + py], dst_ref=couts[i].at[k], send_sem=sems[0].at[3 * i + k],
            recv_sem=sems[1].at[3 * i + k], device_id=(px, py, c), device_id_type=MESH)
            for i in range(self.n) for k, (px, py) in enumerate(peers)]

    def start(self, cins, couts, sems):
        for cp in self._copies(cins, couts, sems):
            cp.start()

    def finish(self, cins, couts, sems):
        for cp in self._copies(cins, couts, sems):
            cp.wait()


class _SwapComm:
    def __init__(self, grads16):
        n = len(grads16)
        self.n = n
        self.operands = list(grads16)
        self.out_shape = [jax.ShapeDtypeStruct(g.shape[:1] + g.shape[2:], g.dtype) for g in grads16]
        self.aliases = {}
        self.sems = [pltpu.SemaphoreType.DMA((n,)), pltpu.SemaphoreType.DMA((n,))]

    def _copies(self, cins, couts, sems):
        x, y, c = _position()
        return [pltpu.make_async_remote_copy(
            src_ref=cins[i].at[:, 1 - c], dst_ref=couts[i], send_sem=sems[0].at[i], recv_sem=sems[1].at[i],
            device_id=(x, y, 1 - c), device_id_type=MESH) for i in range(self.n)]

    def start(self, cins, couts, sems):
        for cp in self._copies(cins, couts, sems):
            cp.start()

    def finish(self, cins, couts, sems):
        for cp in self._copies(cins, couts, sems):
            cp.wait()


class _JoinComm:
    def __init__(self, halves):
        n = len(halves)
        self.n = n
        self.operands = list(halves)
        self.out_shape = [jax.ShapeDtypeStruct(h.shape, h.dtype) for h in halves]
        self.aliases = {i: i for i in range(n)}
        self.sems = [pltpu.SemaphoreType.DMA((n,)), pltpu.SemaphoreType.DMA((n,))]

    def _copy(self, cins, couts, sems, i, which):
        x, y, c = _position()
        return pltpu.make_async_remote_copy(
            src_ref=cins[i].at[which], dst_ref=couts[i].at[which], send_sem=sems[0].at[i], recv_sem=sems[1].at[i],
            device_id=(x, y, 1 - c), device_id_type=MESH)

    def start(self, cins, couts, sems):
        _, _, c = _position()
        for i in range(self.n):
            self._copy(cins, couts, sems, i, c).start()

    def finish(self, cins, couts, sems):
        _, _, c = _position()
        for i in range(self.n):
            self._copy(cins, couts, sems, i, 1 - c).wait_recv()
        for i in range(self.n):
            self._copy(cins, couts, sems, i, c).wait_send()


class _Gather8Comm:
    def __init__(self, block):
        self.operands = [block]
        self.out_shape = [jax.ShapeDtypeStruct((N_DEV,) + block.shape, block.dtype)]
        self.aliases = {}
        self.sems = [pltpu.SemaphoreType.DMA((N_DEV - 1,)), pltpu.SemaphoreType.DMA((N_DEV - 1,)),
                     pltpu.SemaphoreType.DMA]
        self.flips = [(fx, fy, fc) for fx in (0, 1) for fy in (0, 1) for fc in (0, 1) if (fx, fy, fc) != (0, 0, 0)]

    def _peers(self):
        x, y, c = _position()
        return [(_flip(x, fx), _flip(y, fy), _flip(c, fc)) for fx, fy, fc in self.flips]

    def _copy(self, cins, couts, sems, k, block, to):
        return pltpu.make_async_remote_copy(src_ref=cins[0], dst_ref=couts[0].at[block], send_sem=sems[0].at[k],
                                            recv_sem=sems[1].at[k], device_id=to, device_id_type=MESH)

    def _mine(self, cins, couts, sems):
        x, y, c = _position()
        return pltpu.make_async_copy(cins[0], couts[0].at[4 * x + 2 * y + c], sems[2])

    def start(self, cins, couts, sems):
        x, y, c = _position()
        self._mine(cins, couts, sems).start()
        for k, peer in enumerate(self._peers()):
            self._copy(cins, couts, sems, k, 4 * x + 2 * y + c, peer).start()

    def finish(self, cins, couts, sems):
        for k, (px, py, pc) in enumerate(self._peers()):
            self._copy(cins, couts, sems, k, 4 * px + 2 * py + pc, (px, py, pc)).wait_recv()
        for k, peer in enumerate(self._peers()):
            self._copy(cins, couts, sems, k, 0, peer).wait_send()
        self._mine(cins, couts, sems).wait()


class _CommList:
    def __init__(self, parts):
        self.parts = list(parts)
        self.operands = [t for p in self.parts for t in p.operands]
        self.out_shape = [t for p in self.parts for t in p.out_shape]
        self.sems = [t for p in self.parts for t in p.sems]
        self.aliases = {}
        n_in = n_out = 0
        for p in self.parts:
            self.aliases.update({n_in + i: n_out + j for i, j in p.aliases.items()})
            n_in += len(p.operands)
            n_out += len(p.out_shape)

    def _split(self, cins, couts, sems):
        pos = [0, 0, 0]
        for p in self.parts:
            sizes = (len(p.operands), len(p.out_shape), len(p.sems))
            yield p, tuple(seq[a:a + k] for seq, a, k in zip((cins, couts, sems), pos, sizes))
            pos = [a + k for a, k in zip(pos, sizes)]

    def start(self, cins, couts, sems):
        for p, refs in self._split(cins, couts, sems):
            p.start(*refs)

    def finish(self, cins, couts, sems):
        for p, refs in self._split(cins, couts, sems):
            p.finish(*refs)

    def split_outputs(self, outs):
        res, pos = [], 0
        for p in self.parts:
            res.append(outs[pos:pos + len(p.out_shape)])
            pos += len(p.out_shape)
        return res


def _call(body, *, name, grid, in_specs, out_specs, out_shape, operands, scratch_shapes=(), comm=None):
    n_grid = len(grid)
    if comm is None:
        return pl.pallas_call(
            body, name=name, grid=grid, in_specs=list(in_specs), out_specs=list(out_specs), out_shape=list(out_shape),
            scratch_shapes=list(scratch_shapes), compiler_params=_params(n_grid))(*operands), ()
    counts = (len(in_specs), len(comm.operands), len(out_specs), len(comm.out_shape), len(scratch_shapes),
              len(comm.sems))

    def fused(*refs):
        parts, pos = [], 0
        for k in counts:
            parts.append(refs[pos:pos + k])
            pos += k
        ins, cins, outs, couts, scr, sems = parts
        first = functools.reduce(jnp.logical_and, [pl.program_id(d) == 0 for d in range(n_grid)])
        last = functools.reduce(jnp.logical_and, [pl.program_id(d) == grid[d] - 1 for d in range(n_grid)])

        @pl.when(first)
        def _():
            comm.start(cins, couts, sems)

        body(*ins, *outs, *scr)

        @pl.when(last)
        def _():
            comm.finish(cins, couts, sems)

    res = pl.pallas_call(
        fused, name=name, grid=grid, in_specs=list(in_specs) + [ANY] * counts[1],
        out_specs=list(out_specs) + [ANY] * counts[3], out_shape=list(out_shape) + list(comm.out_shape),
        scratch_shapes=list(scratch_shapes) + list(comm.sems),
        input_output_aliases={counts[0] + i: counts[2] + j for i, j in comm.aliases.items()},
        compiler_params=_params(n_grid))(*operands, *comm.operands)
    return res[:counts[2]], res[counts[2]:]


def _run_comm(comm, name):
    k_in, k_out = len(comm.operands), len(comm.out_shape)

    def body(*refs):
        cins, couts, sems = refs[:k_in], refs[k_in:k_in + k_out], refs[k_in + k_out:]
        comm.start(cins, couts, sems)
        comm.finish(cins, couts, sems)

    return pl.pallas_call(
        body, name=name, in_specs=[ANY] * k_in, out_specs=[ANY] * k_out, out_shape=list(comm.out_shape),
        scratch_shapes=list(comm.sems), input_output_aliases=dict(comm.aliases))(*comm.operands)


def _ffn_fwd(x, gn, sc, sh, gate, wg, wu, wd, seq, name, comm=None):
    T, D = x.shape
    J, Fs, _ = wg.shape
    tm = _tile(seq, 1024)
    nb = seq // tm

    def body(x_ref, gn_ref, sc_ref, sh_ref, gate_ref, wg_ref, wu_ref, wd_ref,
             h_ref, a_ref, u_ref, f_ref, xo_ref, hs, acc, s16):
        j = pl.program_id(1)

        @pl.when(j == 0)
        def _():
            hb = _norm_mod(x_ref[...], gn_ref[...], sc_ref[...], sh_ref[...]).astype(BF16)
            hs[...] = hb
            h_ref[...] = hb
            acc[...] = jnp.zeros_like(acc)

        hb = hs[...]
        a_all = _dot_nt(hb, wg_ref[...])
        u_all = _dot_nt(hb, wu_ref[...])

        def swiglu_rows(rows):
            a = a_all[rows, :]
            u = u_all[rows, :]
            a_ref[rows, :] = a.astype(BF16)
            u_ref[rows, :] = u.astype(BF16)
            s16[rows, :] = ((a * _sigmoid(a)) * u).astype(BF16)

        _for_row_chunks(tm, swiglu_rows)
        acc[...] += _dot_nn(s16[...], wd_ref[...])

        @pl.when(j == J - 1)
        def _():
            f = acc[...]
            f_ref[...] = f.astype(BF16)
            xo_ref[...] = x_ref[...] + (FFN_RESIDUAL * gate_ref[...]) * f

    row = pl.BlockSpec((tm, D), lambda i, j: (i, 0))
    vec = pl.BlockSpec((1, D), lambda i, j: (0, 0))
    per_b = pl.BlockSpec((None, 1, D), lambda i, j: (i // nb, 0, 0))
    hid = pl.BlockSpec((None, tm, Fs), lambda i, j: (j, i, 0))
    return _call(
        body, name=name, grid=(T // tm, J),
        in_specs=[row, vec, per_b, per_b, per_b] + [pl.BlockSpec((None, Fs, D), lambda i, j: (j, 0, 0))] * 3,
        out_specs=[row, hid, hid, row, row],
        out_shape=[jax.ShapeDtypeStruct((T, D), BF16), jax.ShapeDtypeStruct((J, T, Fs), BF16),
                   jax.ShapeDtypeStruct((J, T, Fs), BF16), jax.ShapeDtypeStruct((T, D), BF16),
                   jax.ShapeDtypeStruct((T, D), F32)],
        scratch_shapes=[pltpu.VMEM((tm, D), BF16), pltpu.VMEM((tm, D), F32), pltpu.VMEM((tm, Fs), BF16)],
        operands=(x, gn, sc, sh, gate, wg, wu, wd), comm=comm)


def _ffn_bwd(dxo, x, f, a, u, gn, sc, gate, wg, wu, wd, seq, name, comm=None):
    T, D = x.shape
    J, Fs, _ = wg.shape
    B = T // seq
    tm = _tile(seq, 512)
    nb = seq // tm

    def body(dxo_ref, x_ref, f_ref, a_ref, u_ref, gn_ref, sc_ref, gate_ref, wg_ref, wu_ref, wd_ref,
             da_ref, du_ref, s_ref, df_ref, dx_ref, dgate_ref, dsc_ref, dsh_ref, dgn_ref, dfs, acc):
        i = pl.program_id(0)
        j = pl.program_id(1)
        first_of_batch = i % nb == 0

        @pl.when(j == 0)
        def _():
            dxo_v = dxo_ref[...]
            dfb = ((FFN_RESIDUAL * gate_ref[...]) * dxo_v).astype(BF16)
            dfs[...] = dfb
            df_ref[...] = dfb
            part = jnp.sum((FFN_RESIDUAL * f_ref[...].astype(F32)) * dxo_v, axis=0, keepdims=True)
            _accumulate(dgate_ref, first_of_batch, part)
            acc[...] = jnp.zeros_like(acc)

        ds_all = _dot_nt(dfs[...], wd_ref[...])

        def swiglu_bwd_rows(rows):
            ds = ds_all[rows, :]
            av = a_ref[rows, :].astype(F32)
            uv = u_ref[rows, :].astype(F32)
            sig = _sigmoid(av)
            sil = av * sig
            s_ref[rows, :] = (sil * uv).astype(BF16)
            da_ref[rows, :] = (ds * uv * (sig * (1.0 + av * (1.0 - sig)))).astype(BF16)
            du_ref[rows, :] = (ds * sil).astype(BF16)

        _for_row_chunks(tm, swiglu_bwd_rows)
        acc[...] += _dot_nn(da_ref[...], wg_ref[...]) + _dot_nn(du_ref[...], wu_ref[...])

        @pl.when(j == J - 1)
        def _():
            _norm_mod_bwd(acc[...], x_ref[...], gn_ref[...], sc_ref[...], dxo_ref[...],
                          first_of_batch, i == 0, dx_ref, dsc_ref, dsh_ref, dgn_ref)

    row = pl.BlockSpec((tm, D), lambda i, j: (i, 0))
    vec = pl.BlockSpec((1, D), lambda i, j: (0, 0))
    per_b = pl.BlockSpec((None, 1, D), lambda i, j: (i // nb, 0, 0))
    hid = pl.BlockSpec((None, tm, Fs), lambda i, j: (j, i, 0))
    hid_shape = jax.ShapeDtypeStruct((J, T, Fs), BF16)
    per_b_shape = jax.ShapeDtypeStruct((B, 1, D), F32)
    return _call(
        body, name=name, grid=(T // tm, J),
        in_specs=[row, row, row, hid, hid, vec, per_b, per_b]
        + [pl.BlockSpec((None, Fs, D), lambda i, j: (j, 0, 0))] * 3,
        out_specs=[hid, hid, hid, row, row, per_b, per_b, per_b, vec],
        out_shape=[hid_shape, hid_shape, hid_shape, jax.ShapeDtypeStruct((T, D), BF16),
                   jax.ShapeDtypeStruct((T, D), F32), per_b_shape, per_b_shape, per_b_shape,
                   jax.ShapeDtypeStruct((1, D), F32)],
        scratch_shapes=[pltpu.VMEM((tm, D), BF16), pltpu.VMEM((tm, D), F32)],
        operands=(dxo, x, f, a, u, gn, sc, gate, wg, wu, wd), comm=comm)


def _wgrad(a, a_spec, b, b_spec, rows, cols, n_tok, name, comm=None):
    tk = _tile(n_tok, 4096)
    nk = n_tok // tk
    half = rows // 2

    def body(a_ref, b_ref, o32_ref, o16_ref, acc):
        k = pl.program_id(1)

        @pl.when(k == 0)
        def _():
            acc[...] = jnp.zeros_like(acc)

        acc[...] += _dot_tn(a_ref[...], b_ref[...])

        @pl.when(k == nk - 1)
        def _():
            for h in range(2):
                v = acc[h * half:(h + 1) * half, :]
                o32_ref[h] = v
                o16_ref[h] = v.astype(BF16)

    out_spec = pl.BlockSpec((None, 2, half, cols), lambda j, k: (j, 0, 0, 0))
    return _call(
        body, name=name, grid=(N_CHIP, nk),
        in_specs=[a_spec(tk), b_spec(tk)],
        out_specs=[out_spec, out_spec],
        out_shape=[jax.ShapeDtypeStruct((N_CHIP, 2, half, cols), F32),
                   jax.ShapeDtypeStruct((N_CHIP, 2, half, cols), BF16)],
        scratch_shapes=[pltpu.VMEM((rows, cols), F32)],
        operands=(a, b), comm=comm)


def _spec_rows(width):
    return lambda tk: pl.BlockSpec((tk, width), lambda j, k: (k, 0))


def _spec_chip_major(width):
    return lambda tk: pl.BlockSpec((None, tk, width), lambda j, k: (j, k, 0))


def _spec_col_block(width):
    return lambda tk: pl.BlockSpec((tk, width), lambda j, k: (k, j))


def _in_proj(x, gn, sc, sh, w_in, seq, comm=None):
    T, D = x.shape
    N = w_in.shape[0]
    tm = _tile(seq, 2048)
    nb = seq // tm

    def body(x_ref, gn_ref, sc_ref, sh_ref, w_ref, h_ref, p_ref, hs):
        @pl.when(pl.program_id(1) == 0)
        def _():
            hb = _norm_mod(x_ref[...], gn_ref[...], sc_ref[...], sh_ref[...]).astype(BF16)
            hs[...] = hb
            h_ref[...] = hb

        p_ref[...] = _dot_nt(hs[...], w_ref[...]).astype(BF16)

    row = pl.BlockSpec((tm, D), lambda i, j: (i, 0))
    per_b = pl.BlockSpec((None, 1, D), lambda i, j: (i // nb, 0, 0))
    return _call(
        body, name="mix_in_proj", grid=(T // tm, N // PROJ_TILE),
        in_specs=[row, pl.BlockSpec((1, D), lambda i, j: (0, 0)), per_b, per_b,
                  pl.BlockSpec((PROJ_TILE, D), lambda i, j: (j, 0))],
        out_specs=[row, pl.BlockSpec((tm, PROJ_TILE), lambda i, j: (i, j))],
        out_shape=[jax.ShapeDtypeStruct((T, D), BF16), jax.ShapeDtypeStruct((T, N), BF16)],
        scratch_shapes=[pltpu.VMEM((tm, D), BF16)],
        operands=(x, gn, sc, sh, w_in), comm=comm)


def _attn_specs(nblk):
    def own(col):
        return lambda b, n: (b * nblk + n, col)

    def prev(col):
        return lambda b, n: (b * nblk + jnp.maximum(n - 1, 0), col)

    kv = (BLOCK, 2 * HEAD_DIM)
    return [pl.BlockSpec((BLOCK, D_MODEL), own(COLB_Q)),
            pl.BlockSpec(kv, prev(COLB_K)), pl.BlockSpec(kv, own(COLB_K)),
            pl.BlockSpec(kv, prev(COLB_V)), pl.BlockSpec(kv, own(COLB_V))]


def _band_operands(prev_ref, own_ref, lo):
    band = jnp.concatenate([prev_ref[...], own_ref[...]], axis=0).astype(F32)
    rolled = pltpu.roll(band, HEAD_DIM, 1)
    zero = jnp.zeros_like(band)
    head0 = jnp.concatenate([jnp.where(lo, band, zero), jnp.where(lo, zero, rolled)], axis=0).astype(BF16)
    head1 = jnp.concatenate([jnp.where(lo, rolled, zero), jnp.where(lo, zero, band)], axis=0).astype(BF16)
    return head0, head1


PAIRS_PER_KV = N_Q_HEADS // 2 // N_KV_HEADS
BAND = 2 * BLOCK


def _band_valid(has_prev):
    qi = lax.broadcasted_iota(jnp.int32, (PAIRS_PER_KV * BLOCK, BAND), 0) & (BLOCK - 1)
    sj = lax.broadcasted_iota(jnp.int32, (PAIRS_PER_KV * BLOCK, BAND), 1)
    rel = qi + BLOCK - sj
    return (rel >= 0) & (rel < BLOCK) & ((sj >= BLOCK) | has_prev)


def _pair_lanes(kvh, pp):
    pair = kvh * PAIRS_PER_KV + pp
    return slice(pair * 2 * HEAD_DIM, (pair + 1) * 2 * HEAD_DIM)


def _stack_pairs(ref, kvh):
    return jnp.concatenate([ref[:, _pair_lanes(kvh, pp)] for pp in range(PAIRS_PER_KV)], axis=0)


def _rows_per_pair(columns):
    return jnp.concatenate(columns, axis=0)


def _attn_fwd(proj, sinks, batch, seq, comm=None):
    T = proj.shape[0]
    nblk = seq // BLOCK

    def body(sink_ref, q_ref, kp_ref, ko_ref, vp_ref, vo_ref, o_ref, lse_ref):
        lo = lax.broadcasted_iota(jnp.int32, (1, 2 * HEAD_DIM), 1) < HEAD_DIM
        head_lane = lax.broadcasted_iota(jnp.int32, (1, N_Q_HEADS), 1)
        valid = _band_valid(pl.program_id(1) > 0)
        k_ops = _band_operands(kp_ref, ko_ref, lo)
        v_ops = _band_operands(vp_ref, vo_ref, lo)
        lse_all = jnp.zeros((BLOCK, N_Q_HEADS), F32)
        col = jnp.zeros((BLOCK, 1), F32)
        side0_row = lax.broadcasted_iota(jnp.int32, (2 * BAND, 2 * HEAD_DIM), 0) < BAND
        low_lane = lax.broadcasted_iota(jnp.int32, (2 * BAND, 2 * HEAD_DIM), 1) < HEAD_DIM
        side_ones = jnp.where(side0_row == low_lane, 1.0, 0.0).astype(BF16)
        for kvh in range(N_KV_HEADS):
            s_all = _dot_nt(_stack_pairs(q_ref, kvh), k_ops[kvh]) * ATTN_SCALE
            weights, maxes, sink_terms = [], [], []
            for side in range(2):
                heads = [2 * (kvh * PAIRS_PER_KV + pp) + side for pp in range(PAIRS_PER_KV)]
                sink = _rows_per_pair([col + sink_ref[0, h] for h in heads])
                s = jnp.where(valid, s_all[:, side * BAND:(side + 1) * BAND], MASK_VALUE)
                m = jnp.maximum(jnp.max(s, axis=-1, keepdims=True), sink)
                weights.append(jnp.where(valid, jnp.exp(s - m), 0.0).astype(BF16))
                maxes.append(m)
                sink_terms.append(jnp.exp(sink - m))
            p_all = jnp.concatenate(weights, axis=1)
            den = _dot_nn(p_all, side_ones) + jnp.where(lo, sink_terms[0], sink_terms[1])
            out = _dot_nn(p_all, v_ops[kvh]) / den
            for pp in range(PAIRS_PER_KV):
                o_ref[:, _pair_lanes(kvh, pp)] = out[pp * BLOCK:(pp + 1) * BLOCK].astype(BF16)
            for side in range(2):
                lse = maxes[side] + jnp.log(den[:, side * HEAD_DIM:side * HEAD_DIM + 1])
                for pp in range(PAIRS_PER_KV):
                    h = 2 * (kvh * PAIRS_PER_KV + pp) + side
                    lse_all = jnp.where(head_lane == h, lse[pp * BLOCK:(pp + 1) * BLOCK], lse_all)
        lse_ref[...] = lse_all

    return _call(
        body, name="attn_fwd", grid=(batch, nblk),
        in_specs=[SMEM_SPEC] + _attn_specs(nblk),
        out_specs=[pl.BlockSpec((BLOCK, D_MODEL), lambda b, n: (b * nblk + n, 0)),
                   pl.BlockSpec((BLOCK, N_Q_HEADS), lambda b, n: (b * nblk + n, 0))],
        out_shape=[jax.ShapeDtypeStruct((T, D_MODEL), BF16), jax.ShapeDtypeStruct((T, N_Q_HEADS), F32)],
        operands=(sinks, proj, proj, proj, proj, proj), comm=comm)


def _conv_u(ca, cb):
    return ca.astype(F32) * _sigmoid(cb.astype(F32))


def _conv_specs(ts, tiles_per_seq):
    per_tile = ts // CONV_PAD

    def tile(col):
        return lambda b, t: (b * tiles_per_seq + t, col)

    def before(col):
        return lambda b, t: (jnp.maximum((b * tiles_per_seq + t) * per_tile - 1, 0), col)

    return [pl.BlockSpec((ts, D_MODEL), tile(COLB_CA)), pl.BlockSpec((ts, D_MODEL), tile(COLB_CB)),
            pl.BlockSpec((CONV_PAD, D_MODEL), before(COLB_CA)), pl.BlockSpec((CONV_PAD, D_MODEL), before(COLB_CB))]


SUBLANES = 8


def _fill_upad(upad, ca_ref, cb_ref, cah_ref, cbh_ref, t):
    halo = _conv_u(cah_ref[...], cbh_ref[...])
    upad[0, 0:CONV_PAD, :] = jnp.where(t > 0, halo, jnp.zeros_like(halo))
    upad[0, CONV_PAD:, :] = _conv_u(ca_ref[...], cb_ref[...])


def _fill_shifted(pad):
    rows = pad.shape[1] - SUBLANES
    for b in range(1, SUBLANES):
        pad[b, 0:rows, :] = pad[0, b:b + rows, :]


def _shifted_rows(pad, offset, rows):
    b = offset % SUBLANES
    return pad[b, offset - b:offset - b + rows, :]


def _layernorm_stats(y):
    mu = jnp.mean(y, axis=-1, keepdims=True)
    yc = y - mu
    rstd = lax.rsqrt(jnp.mean(yc * yc, axis=-1, keepdims=True) + EPS)
    return yc * rstd, rstd


def _conv_fwd(proj, w_dw, b_dw, ln_g, ln_b, batch, seq, comm=None):
    T = proj.shape[0]
    ts = _tile(seq, 256)
    nt = seq // ts
    shift = CONV_PAD - (CONV_WIDTH - 1)

    def body(ca_ref, cb_ref, cah_ref, cbh_ref, w_ref, b_ref, g_ref, beta_ref, y_ref, z_ref, upad):
        _fill_upad(upad, ca_ref, cb_ref, cah_ref, cbh_ref, pl.program_id(1))
        _fill_shifted(upad)
        y = jnp.zeros((ts, D_MODEL), F32) + b_ref[...]
        for k in range(CONV_WIDTH):
            y = y + w_ref[k:k + 1, :] * _shifted_rows(upad, shift + k, ts)
        y_ref[...] = y
        lnh, _ = _layernorm_stats(y)
        ln = lnh * g_ref[...] + beta_ref[...]
        z_ref[...] = (ln * _sigmoid(ln)).astype(BF16)

    vec = pl.BlockSpec((1, D_MODEL), lambda b, t: (0, 0))
    row = pl.BlockSpec((ts, D_MODEL), lambda b, t: (b * nt + t, 0))
    return _call(
        body, name="conv_fwd", grid=(batch, nt),
        in_specs=_conv_specs(ts, nt) + [pl.BlockSpec((CONV_PAD, D_MODEL), lambda b, t: (0, 0)), vec, vec, vec],
        out_specs=[row, row],
        out_shape=[jax.ShapeDtypeStruct((T, D_MODEL), F32), jax.ShapeDtypeStruct((T, D_MODEL), BF16)],
        scratch_shapes=[pltpu.VMEM((SUBLANES, ts + CONV_PAD, D_MODEL), F32)],
        operands=(proj, proj, proj, proj, w_dw, b_dw, ln_g, ln_b), comm=comm)


def _merge(o, z, proj, w_ao, w_co, w_out, x, gate, seq):
    T, D = x.shape
    tm = _tile(seq, 512)
    nb = seq // tm

    def body(o_ref, z_ref, ga_ref, gc_ref, wao_ref, wco_ref, wout_ref, x_ref, gate_ref,
             ya_ref, yc_ref, mg_ref, mo_ref, xo_ref):
        ya = _dot_nn(o_ref[...], wao_ref[...])
        yc = _dot_nn(z_ref[...], wco_ref[...])
        ya_ref[...] = ya.astype(BF16)
        yc_ref[...] = yc.astype(BF16)
        merged = (_sigmoid(ga_ref[...].astype(F32)) * ya + _sigmoid(gc_ref[...].astype(F32)) * yc).astype(BF16)
        mg_ref[...] = merged
        mo = _dot_nn(merged, wout_ref[...])
        mo_ref[...] = mo.astype(BF16)
        xo_ref[...] = x_ref[...] + gate_ref[...] * mo

    row = pl.BlockSpec((tm, D), lambda i: (i, 0))
    mat = pl.BlockSpec((D, D), lambda i: (0, 0))
    act = jax.ShapeDtypeStruct((T, D), BF16)
    return pl.pallas_call(
        body, name="mix_merge", grid=(T // tm,),
        in_specs=[row, row, pl.BlockSpec((tm, D), lambda i: (i, COLB_GA)), pl.BlockSpec((tm, D), lambda i: (i, COLB_GC)),
                  mat, mat, mat, row, pl.BlockSpec((None, 1, D), lambda i: (i // nb, 0, 0))],
        out_specs=[row, row, row, row, row],
        out_shape=[act, act, act, act, jax.ShapeDtypeStruct((T, D), F32)],
        compiler_params=_params(1),
    )(o, z, proj, proj, w_ao, w_co, w_out, x, gate)


def _final_loss(x, gf, target):
    T, D = x.shape
    tm = _tile(T, 512)

    def body(x_ref, gf_ref, t_ref, dx_ref, lp_ref, dgf_ref):
        first = pl.program_id(0) == 0
        xv = x_ref[...]
        gfv = gf_ref[...]
        r = lax.rsqrt(jnp.mean(xv * xv, axis=-1, keepdims=True) + EPS)
        xh = xv * r
        err = xh * gfv - t_ref[...]
        _accumulate(lp_ref, first, jnp.sum(err * err, axis=0, keepdims=True))
        dy = err * (1.0 / D)
        _accumulate(dgf_ref, first, jnp.sum(dy * xh, axis=0, keepdims=True))
        dxh = dy * gfv
        dx_ref[...] = r * (dxh - xh * jnp.mean(dxh * xh, axis=-1, keepdims=True))

    row = pl.BlockSpec((tm, D), lambda i: (i, 0))
    vec = pl.BlockSpec((1, D), lambda i: (0, 0))
    return pl.pallas_call(
        body, name="final_loss", grid=(T // tm,),
        in_specs=[row, vec, row], out_specs=[row, vec, vec],
        out_shape=[jax.ShapeDtypeStruct((T, D), F32), jax.ShapeDtypeStruct((1, D), F32),
                   jax.ShapeDtypeStruct((1, D), F32)],
        compiler_params=_params(1),
    )(x, gf, target)


def _merge_bwd(dxo, mo, gate, proj, ya, yc, w_out, w_ao, w_co, seq, comm=None):
    T, D = dxo.shape
    B = T // seq
    tm = _tile(seq, 512)
    nb = seq // tm

    def body(dxo_ref, mo_ref, gate_ref, ga_ref, gc_ref, ya_ref, yc_ref, wout_ref, wao_ref, wco_ref,
             dmo_ref, dya_ref, dyc_ref, dga_ref, dgc_ref, do_ref, dz_ref, dgate_ref):
        dxo_v = dxo_ref[...]
        dmo = (gate_ref[...] * dxo_v).astype(BF16)
        dmo_ref[...] = dmo
        _accumulate(dgate_ref, pl.program_id(0) % nb == 0,
                    jnp.sum(mo_ref[...].astype(F32) * dxo_v, axis=0, keepdims=True))
        dm = _dot_nt(dmo, wout_ref[...])
        sa = _sigmoid(ga_ref[...].astype(F32))
        sc = _sigmoid(gc_ref[...].astype(F32))
        dya = (sa * dm).astype(BF16)
        dyc = (sc * dm).astype(BF16)
        dya_ref[...] = dya
        dyc_ref[...] = dyc
        dga_ref[...] = (dm * ya_ref[...].astype(F32) * (sa * (1.0 - sa))).astype(BF16)
        dgc_ref[...] = (dm * yc_ref[...].astype(F32) * (sc * (1.0 - sc))).astype(BF16)
        do_ref[...] = _dot_nt(dya, wao_ref[...]).astype(BF16)
        dz_ref[...] = _dot_nt(dyc, wco_ref[...]).astype(BF16)

    row = pl.BlockSpec((tm, D), lambda i: (i, 0))
    mat = pl.BlockSpec((D, D), lambda i: (0, 0))
    per_b = pl.BlockSpec((None, 1, D), lambda i: (i // nb, 0, 0))
    act = jax.ShapeDtypeStruct((T, D), BF16)
    return _call(
        body, name="mix_merge_bwd", grid=(T // tm,),
        in_specs=[row, row, per_b, pl.BlockSpec((tm, D), lambda i: (i, COLB_GA)),
                  pl.BlockSpec((tm, D), lambda i: (i, COLB_GC)), row, row, mat, mat, mat],
        out_specs=[row] * 7 + [per_b],
        out_shape=[act] * 7 + [jax.ShapeDtypeStruct((B, 1, D), F32)],
        operands=(dxo, mo, gate, proj, proj, ya, yc, w_out, w_ao, w_co), comm=comm)


def _attn_bwd(proj, sinks, o, do, lse, batch, seq, comm=None):
    T = proj.shape[0]
    nblk = seq // BLOCK
    n_steps = batch * nblk

    def body(sink_ref, q_ref, kp_ref, ko_ref, vp_ref, vo_ref, o_ref, do_ref, lse_ref,
             dq_ref, dkp_ref, dko_ref, dvp_ref, dvo_ref, dsink_ref):
        lo = lax.broadcasted_iota(jnp.int32, (1, 2 * HEAD_DIM), 1) < HEAD_DIM
        sink_lane = lax.broadcasted_iota(jnp.int32, (1, 2 * HEAD_DIM), 1)
        valid = _band_valid(pl.program_id(1) > 0)
        k_ops = _band_operands(kp_ref, ko_ref, lo)
        v_ops = _band_operands(vp_ref, vo_ref, lo)
        dsink = jnp.zeros((1, 2 * HEAD_DIM), F32)
        col = jnp.zeros((BLOCK, 1), F32)

        def fold(both):
            return (jnp.where(lo, both[:BAND], 0.0)
                    + pltpu.roll(jnp.where(lo, 0.0, both[BAND:]), HEAD_DIM, 1))

        dk_heads, dv_heads = [], []
        for kvh in range(N_KV_HEADS):
            q4 = _stack_pairs(q_ref, kvh)
            do4 = _stack_pairs(do_ref, kvh)
            dd = do4.astype(F32) * _stack_pairs(o_ref, kvh).astype(F32)
            s_all = _dot_nt(q4, k_ops[kvh]) * ATTN_SCALE
            dp_all = _dot_nt(do4, v_ops[kvh])
            ds_sides, p_sides = [], []
            for side in range(2):
                heads = [2 * (kvh * PAIRS_PER_KV + pp) + side for pp in range(PAIRS_PER_KV)]
                mine = lo if side == 0 else jnp.logical_not(lo)
                cols = slice(side * BAND, (side + 1) * BAND)
                sink = _rows_per_pair([col + sink_ref[0, h] for h in heads])
                lse = _rows_per_pair([lse_ref[:, h:h + 1] for h in heads])
                delta = jnp.sum(jnp.where(mine, dd, 0.0), axis=-1, keepdims=True)
                p = jnp.where(valid, jnp.exp(jnp.where(valid, s_all[:, cols], MASK_VALUE) - lse), 0.0)
                ds_sides.append((p * (dp_all[:, cols] - delta) * ATTN_SCALE).astype(BF16))
                p_sides.append(p.astype(BF16))
                sink_part = jnp.exp(sink - lse) * delta
                for pp, h in enumerate(heads):
                    dsink = dsink + jnp.where(sink_lane == h, -jnp.sum(sink_part[pp * BLOCK:(pp + 1) * BLOCK]), 0.0)
            ds_all = jnp.concatenate(ds_sides, axis=1)
            dq4 = _dot_nn(ds_all, k_ops[kvh])
            for pp in range(PAIRS_PER_KV):
                dq_ref[:, _pair_lanes(kvh, pp)] = dq4[pp * BLOCK:(pp + 1) * BLOCK].astype(BF16)
            dk_heads.append(fold(_dot_tn(ds_all, q4)))
            dv_heads.append(fold(_dot_tn(jnp.concatenate(p_sides, axis=1), do4)))
        dk = dk_heads[0] + pltpu.roll(dk_heads[1], HEAD_DIM, 1)
        dv = dv_heads[0] + pltpu.roll(dv_heads[1], HEAD_DIM, 1)
        dkp_ref[...] = dk[:BLOCK]
        dko_ref[...] = dk[BLOCK:]
        dvp_ref[...] = dv[:BLOCK]
        dvo_ref[...] = dv[BLOCK:]
        dsink_ref[...] = dsink

    def own(b, n):
        return (b * nblk + n, 0)

    row = pl.BlockSpec((BLOCK, D_MODEL), own)
    kv = pl.BlockSpec((BLOCK, 2 * HEAD_DIM), own)
    kv_shape = jax.ShapeDtypeStruct((T, 2 * HEAD_DIM), F32)
    return _call(
        body, name="attn_bwd", grid=(batch, nblk),
        in_specs=[SMEM_SPEC] + _attn_specs(nblk) + [row, row, pl.BlockSpec((BLOCK, N_Q_HEADS), own)],
        out_specs=[row, kv, kv, kv, kv, pl.BlockSpec((None, 1, 2 * HEAD_DIM), lambda b, n: (b * nblk + n, 0, 0))],
        out_shape=[jax.ShapeDtypeStruct((T, D_MODEL), BF16), kv_shape, kv_shape, kv_shape, kv_shape,
                   jax.ShapeDtypeStruct((n_steps, 1, 2 * HEAD_DIM), F32)],
        operands=(sinks, proj, proj, proj, proj, proj, o, do, lse), comm=comm)


def _conv_bwd(proj, dz, ydw, w_dw, ln_g, ln_b, batch, seq, comm=None):
    T = proj.shape[0]
    ts = _tile(seq, 256)
    nt = seq // ts
    per_tile = ts // CONV_PAD
    shift = CONV_PAD - (CONV_WIDTH - 1)

    def body(ca_ref, cb_ref, cah_ref, cbh_ref, dz_ref, dzn_ref, y_ref, yn_ref, w_ref, g_ref, beta_ref,
             dca_ref, dcb_ref, dw_ref, db_ref, dg_ref, dbeta_ref, upad, dypad):
        t = pl.program_id(1)
        first = (pl.program_id(0) == 0) & (t == 0)
        gv = g_ref[...]

        def ln_bwd(dzv, yv):
            lnh, rstd = _layernorm_stats(yv)
            ln = lnh * gv + beta_ref[...]
            sg = _sigmoid(ln)
            dln = dzv.astype(F32) * (sg * (1.0 + ln * (1.0 - sg)))
            dyh = dln * gv
            dy = rstd * (dyh - jnp.mean(dyh, axis=-1, keepdims=True)
                         - lnh * jnp.mean(dyh * lnh, axis=-1, keepdims=True))
            return dy, dln, lnh

        dy, dln, lnh = ln_bwd(dz_ref[...], y_ref[...])
        dy_next, _, _ = ln_bwd(dzn_ref[...], yn_ref[...])
        dypad[0, 0:ts, :] = dy
        dypad[0, ts:, :] = jnp.where(t < nt - 1, dy_next, jnp.zeros_like(dy_next))
        _fill_shifted(dypad)
        _fill_upad(upad, ca_ref, cb_ref, cah_ref, cbh_ref, t)
        _fill_shifted(upad)

        _accumulate(dg_ref, first, jnp.sum(dln * lnh, axis=0, keepdims=True))
        _accumulate(dbeta_ref, first, jnp.sum(dln, axis=0, keepdims=True))
        _accumulate(db_ref, first, jnp.sum(dy, axis=0, keepdims=True))

        @pl.when(first)
        def _():
            dw_ref[...] = jnp.zeros_like(dw_ref)

        du = jnp.zeros((ts, D_MODEL), F32)
        for k in range(CONV_WIDTH):
            du = du + w_ref[k:k + 1, :] * _shifted_rows(dypad, CONV_WIDTH - 1 - k, ts)
            dw_ref[k:k + 1, :] += jnp.sum(dy * _shifted_rows(upad, shift + k, ts), axis=0, keepdims=True)
        cav = ca_ref[...].astype(F32)
        sb = _sigmoid(cb_ref[...].astype(F32))
        dca_ref[...] = (du * sb).astype(BF16)
        dcb_ref[...] = (du * cav * (sb * (1.0 - sb))).astype(BF16)

    def tile(b, t):
        return (b * nt + t, 0)

    def after(b, t):
        return (jnp.minimum((b * nt + t + 1) * per_tile, T // CONV_PAD - 1), 0)

    row = pl.BlockSpec((ts, D_MODEL), tile)
    halo = pl.BlockSpec((CONV_PAD, D_MODEL), after)
    vec = pl.BlockSpec((1, D_MODEL), lambda b, t: (0, 0))
    wspec = pl.BlockSpec((CONV_PAD, D_MODEL), lambda b, t: (0, 0))
    act = jax.ShapeDtypeStruct((T, D_MODEL), BF16)
    vec_shape = jax.ShapeDtypeStruct((1, D_MODEL), F32)
    return _call(
        body, name="conv_bwd", grid=(batch, nt),
        in_specs=_conv_specs(ts, nt) + [row, halo, row, halo, wspec, vec, vec],
        out_specs=[row, row, wspec, vec, vec, vec],
        out_shape=[act, act, jax.ShapeDtypeStruct((CONV_PAD, D_MODEL), F32), vec_shape, vec_shape, vec_shape],
        scratch_shapes=[pltpu.VMEM((SUBLANES, ts + CONV_PAD, D_MODEL), F32)] * 2,
        operands=(proj, proj, proj, proj, dz, dz, ydw, ydw, w_dw, ln_g, ln_b), comm=comm)


def _in_proj_bwd(pieces, w_cols, x, gn, sc, dxo, seq, comm=None):
    T, D = x.shape
    B = T // seq
    wide, narrow = list(pieces[:-1]), pieces[-1]
    P = len(wide)
    nw = narrow.shape[1]
    tm = _tile(seq, 512)
    nb = seq // tm

    def body(*refs):
        wide_refs = refs[:P]
        kv_ref, w_ref, wkv_ref, x_ref, gn_ref, sc_ref, dxo_ref, dx_ref, dsc_ref, dsh_ref, dgn_ref, acc = refs[P:]
        i = pl.program_id(0)
        j = pl.program_id(1)

        @pl.when(j == 0)
        def _():
            acc[...] = _dot_nn(kv_ref[...], wkv_ref[...])

        for p in range(P):
            @pl.when(j == p)
            def _(p=p):
                acc[...] += _dot_nn(wide_refs[p][...], w_ref[...])

        @pl.when(j == P - 1)
        def _():
            _norm_mod_bwd(acc[...], x_ref[...], gn_ref[...], sc_ref[...], dxo_ref[...],
                          i % nb == 0, i == 0, dx_ref, dsc_ref, dsh_ref, dgn_ref)

    row = pl.BlockSpec((tm, D), lambda i, j: (i, 0))
    vec = pl.BlockSpec((1, D), lambda i, j: (0, 0))
    per_b = pl.BlockSpec((None, 1, D), lambda i, j: (i // nb, 0, 0))
    per_b_shape = jax.ShapeDtypeStruct((B, 1, D), F32)
    return _call(
        body, name="mix_in_proj_bwd", grid=(T // tm, P),
        in_specs=[row] * P + [pl.BlockSpec((tm, nw), lambda i, j: (i, 0)),
                              pl.BlockSpec((D, D), lambda i, j: (j, 0)),
                              pl.BlockSpec((nw, D), lambda i, j: (P * D // nw, 0)), row, vec, per_b, row],
        out_specs=[row, per_b, per_b, vec],
        out_shape=[jax.ShapeDtypeStruct((T, D), F32), per_b_shape, per_b_shape, jax.ShapeDtypeStruct((1, D), F32)],
        scratch_shapes=[pltpu.VMEM((tm, D), F32)],
        operands=(*wide, narrow, w_cols, w_cols, x, gn, sc, dxo), comm=comm)


def _wgrad_rows(piece, h, out32, out16, row_offset, name):
    T, n = piece.shape
    C = h.shape[1]
    tk = _tile(T, 1024)
    nk = T // tk

    def body(a_ref, b_ref, in32, in16, o32_ref, o16_ref, acc, stage16, sems):
        k = pl.program_id(0)

        @pl.when(k == 0)
        def _():
            acc[...] = jnp.zeros_like(acc)

        acc[...] += _dot_tn(a_ref[...], b_ref[...])

        @pl.when(k == nk - 1)
        def _():
            stage16[...] = acc[...].astype(BF16)
            rows = pl.ds(row_offset, n)
            copies = [pltpu.make_async_copy(acc, o32_ref.at[rows, :], sems.at[0]),
                      pltpu.make_async_copy(stage16, o16_ref.at[rows, :], sems.at[1])]
            for cp in copies:
                cp.start()
            for cp in copies:
                cp.wait()

    return pl.pallas_call(
        body, name=name, grid=(nk,),
        in_specs=[pl.BlockSpec((tk, n), lambda k: (k, 0)), pl.BlockSpec((tk, C), lambda k: (k, 0)), ANY, ANY],
        out_specs=[ANY, ANY], out_shape=[jax.ShapeDtypeStruct(out32.shape, F32), jax.ShapeDtypeStruct(out16.shape, BF16)],
        scratch_shapes=[pltpu.VMEM((n, C), F32), pltpu.VMEM((n, C), BF16), pltpu.SemaphoreType.DMA((2,))],
        input_output_aliases={2: 0, 3: 1}, compiler_params=_params(1),
    )(piece, h, out32, out16)


def _ada_fwd(c_all, w_ada, b_cols):
    nbatch, D = c_all.shape
    N = w_ada.shape[1]
    tn = _tile(N, 768)

    def body(c_ref, w_ref, b_ref, o_ref):
        cv = c_ref[...]
        act = (cv * _sigmoid(cv)).astype(BF16)
        o_ref[...] = _dot_nn(act, w_ref[...].astype(BF16)) + b_ref[...]

    return pl.pallas_call(
        body, name="ada_fwd", grid=(N // tn,),
        in_specs=[pl.BlockSpec((nbatch, D), lambda j: (0, 0)), pl.BlockSpec((D, tn), lambda j: (0, j)),
                  pl.BlockSpec((1, tn), lambda j: (0, j))],
        out_specs=pl.BlockSpec((nbatch, tn), lambda j: (0, j)),
        out_shape=jax.ShapeDtypeStruct((nbatch, N), F32),
        compiler_params=_params(1),
    )(c_all, w_ada, b_cols)


def _adamw(w, g, m, v):
    m = ADAM_B1 * m + (1.0 - ADAM_B1) * g
    v = ADAM_B2 * v + (1.0 - ADAM_B2) * (g * g)
    m_hat = m / (1.0 - ADAM_B1 ** ADAM_STEP)
    v_hat = v / (1.0 - ADAM_B2 ** ADAM_STEP)
    delta = -ADAM_LR * (m_hat / (jnp.sqrt(v_hat) + ADAM_EPS) + ADAM_WD * w)
    return delta, m, v


def _adam_call(w, g, m, v, name, comm=None):
    R, C = w.shape
    tr = _row_tile(R, 512)

    def body(w_ref, g_ref, m_ref, v_ref, d_ref, mo_ref, vo_ref):
        d, mn, vn = _adamw(w_ref[...], g_ref[...], m_ref[...], v_ref[...])
        d_ref[...] = d
        mo_ref[...] = mn
        vo_ref[...] = vn

    blk = pl.BlockSpec((tr, C), lambda i: (i, 0))
    shape = jax.ShapeDtypeStruct((R, C), F32)
    return _call(body, name=name, grid=(R // tr,), in_specs=[blk] * 4, out_specs=[blk] * 3, out_shape=[shape] * 3,
                 operands=(w, g, m, v), comm=comm)


ADAM_GROUP_STEPS = 8


def _adam_group(ws, gs, ms, vs, name, comm=None):
    n = len(ws)

    def body(*refs):
        ins, outs = refs[:4 * n], refs[4 * n:]
        for i in range(n):
            d, mn, vn = _adamw(*(r[...] for r in ins[4 * i:4 * i + 4]))
            outs[3 * i][...] = d
            outs[3 * i + 1][...] = mn
            outs[3 * i + 2][...] = vn

    operands, in_specs, out_specs, out_shape = [], [], [], []
    for w, g, m, v in zip(ws, gs, ms, vs):
        R, C = w.shape
        blk = pl.BlockSpec((R // ADAM_GROUP_STEPS, C), lambda i: (i, 0))
        operands += [w, g, m, v]
        in_specs += [blk] * 4
        out_specs += [blk] * 3
        out_shape += [jax.ShapeDtypeStruct((R, C), F32)] * 3
    outs, comm_outs = _call(body, name=name, grid=(ADAM_GROUP_STEPS,), in_specs=in_specs, out_specs=out_specs,
                            out_shape=out_shape, operands=operands, comm=comm)
    return [tuple(outs[3 * i:3 * i + 3]) for i in range(n)], comm_outs


def _ada_adam(c_act_t, dmod_cols, w, m, v, comm):
    R, C = w.shape
    nbatch = c_act_t.shape[1]
    tr = _tile(R, 128)

    def body(ct_ref, dm_ref, w_ref, m_ref, v_ref, g_ref, d_ref, mo_ref, vo_ref):
        cv = ct_ref[...]
        g = _dot_nn((cv * _sigmoid(cv)).astype(BF16), dm_ref[...].astype(BF16))
        g_ref[...] = g
        d, mn, vn = _adamw(w_ref[...], g, m_ref[...], v_ref[...])
        d_ref[...] = d
        mo_ref[...] = mn
        vo_ref[...] = vn

    blk = pl.BlockSpec((tr, C), lambda i: (i, 0))
    shape = jax.ShapeDtypeStruct((R, C), F32)
    return _call(
        body, name="ada_adam", grid=(R // tr,),
        in_specs=[pl.BlockSpec((tr, nbatch), lambda i: (i, 0)), pl.BlockSpec((nbatch, C), lambda i: (0, 0)),
                  blk, blk, blk],
        out_specs=[blk] * 4, out_shape=[shape] * 4,
        operands=(c_act_t, dmod_cols, w, m, v), comm=comm)


def _small_adam(gathered, w, m, v, rows_b0, rows_b1, rows_vec):
    _, P, D = gathered.shape
    R = w.shape[0]

    def body(ga_ref, w_ref, m_ref, v_ref, sum_ref, g_ref, d_ref, mo_ref, vo_ref):
        total = ga_ref[0]
        for dev in range(1, N_DEV):
            total = total + ga_ref[dev]
        sum_ref[...] = total
        g_ref[...] = jnp.zeros_like(g_ref)
        g_ref[0:N_MOD, :] = (sum_ref[rows_b0:rows_b0 + N_MOD, :] + sum_ref[rows_b1:rows_b1 + N_MOD, :])
        g_ref[N_MOD:N_MOD + 8, :] = sum_ref[rows_vec:rows_vec + 8, :]
        d, mn, vn = _adamw(w_ref[...], g_ref[...], m_ref[...], v_ref[...])
        d_ref[...] = d
        mo_ref[...] = mn
        vo_ref[...] = vn

    shape = jax.ShapeDtypeStruct((R, D), F32)
    return pl.pallas_call(
        body, name="small_adam",
        in_specs=[VMEM_SPEC] * 4, out_specs=[VMEM_SPEC] * 5,
        out_shape=[jax.ShapeDtypeStruct((P, D), F32), shape, shape, shape, shape],
        compiler_params=pltpu.CompilerParams(vmem_limit_bytes=VMEM_LIMIT),
    )(gathered, w, m, v)


def _mod_exchange(part):
    _, A, W = part.shape

    def body(p_ref, out_ref, send_sems, recv_sems, local_sem):
        x, y, c = _position()
        me = 4 * x + 2 * y + c
        chip = 2 * x + y
        mine = pltpu.make_async_copy(p_ref.at[me], out_ref.at[chip], local_sem)
        mine.start()
        peers = [(_flip(x, fx), _flip(y, fy)) for fx, fy in CHIP_FLIPS]
        sends = []
        for k, (px, py) in enumerate(peers):
            sends.append(pltpu.make_async_remote_copy(
                src_ref=p_ref.at[4 * px + 2 * py + c], dst_ref=out_ref.at[chip], send_sem=send_sems.at[k],
                recv_sem=recv_sems.at[k], device_id=(px, py, c), device_id_type=MESH))
        for cp in sends:
            cp.start()
        for k, (px, py) in enumerate(peers):
            pltpu.make_async_remote_copy(
                src_ref=p_ref.at[me], dst_ref=out_ref.at[2 * px + py], send_sem=send_sems.at[k],
                recv_sem=recv_sems.at[k], device_id=(px, py, c), device_id_type=MESH).wait_recv()
        for cp in sends:
            cp.wait_send()
        mine.wait()

    return pl.pallas_call(
        body, name="mod_exchange", in_specs=[VMEM_SPEC], out_specs=VMEM_SPEC,
        out_shape=jax.ShapeDtypeStruct((N_CHIP, A, W), part.dtype),
        scratch_shapes=[pltpu.SemaphoreType.DMA((3,)), pltpu.SemaphoreType.DMA((3,)), pltpu.SemaphoreType.DMA],
    )(part)


KV_ROWS = 4 * HEAD_DIM


def _kernel_row_order(w_in_t):
    R, C = w_in_t.shape
    n_blocks = R // KV_ROWS
    q_blocks = D_MODEL // KV_ROWS

    def source(t):
        return jnp.where(t < q_blocks, t, jnp.where(t < n_blocks - 1, t + 1, q_blocks))

    def body(w_ref, o_ref):
        o_ref[...] = w_ref[...]

    return pl.pallas_call(
        body, name="w_in_row_order", grid=(n_blocks,),
        in_specs=[pl.BlockSpec((KV_ROWS, C), lambda t: (source(t), 0))],
        out_specs=pl.BlockSpec((KV_ROWS, C), lambda t: (t, 0)),
        out_shape=jax.ShapeDtypeStruct((R, C), w_in_t.dtype), compiler_params=_params(1),
    )(w_in_t)


def _cast_group(ws, names, comm):
    n = len(ws)
    steps = 4

    def body(*refs):
        w_refs, out_refs, stage, sem = refs[:n], refs[n:2 * n], refs[2 * n:3 * n], refs[3 * n]
        x, y, _ = _position()
        step = pl.program_id(0)
        copies = []
        for i in range(n):
            rows = ws[i].shape[0] // steps
            stage[i][...] = w_refs[i][...].astype(BF16)
            copies.append(pltpu.make_async_copy(
                stage[i], out_refs[i].at[2 * x + y, pl.ds(step * rows, rows), :], sem.at[i]))
        for cp in copies:
            cp.start()
        for cp in copies:
            cp.wait()

    outs, comm_outs = _call(
        body, name="cast_" + "_".join(names), grid=(steps,),
        in_specs=[pl.BlockSpec((w.shape[0] // steps, w.shape[1]), lambda i: (i, 0)) for w in ws],
        out_specs=[ANY] * n, out_shape=[jax.ShapeDtypeStruct((N_CHIP,) + w.shape, BF16) for w in ws],
        scratch_shapes=[pltpu.VMEM((w.shape[0] // steps, w.shape[1]), BF16) for w in ws]
        + [pltpu.SemaphoreType.DMA((n,))],
        operands=ws, comm=comm)
    return outs, comm_outs


def _cast_slot(w, chip_idx, name):
    R, C = w.shape
    tr = _row_tile(R, 512)

    def body(chip_ref, w_ref, o_ref):
        o_ref[...] = w_ref[...].astype(BF16)

    return pl.pallas_call(
        body, name=name,
        grid_spec=pltpu.PrefetchScalarGridSpec(
            num_scalar_prefetch=1, grid=(R // tr,),
            in_specs=[pl.BlockSpec((tr, C), lambda i, chip_ref: (i, 0))],
            out_specs=pl.BlockSpec((None, tr, C), lambda i, chip_ref: (chip_ref[0], i, 0))),
        out_shape=jax.ShapeDtypeStruct((N_CHIP, R, C), BF16),
        compiler_params=_params(1),
    )(chip_idx, w)


def _pair_sum(g32, recv, core, name):
    J, _, r, C = g32.shape

    def body(core_ref, g_ref, r_ref, o_ref):
        o_ref[...] = (g_ref[...] + r_ref[...].astype(F32)).astype(BF16)

    return pl.pallas_call(
        body, name=name,
        grid_spec=pltpu.PrefetchScalarGridSpec(
            num_scalar_prefetch=1, grid=(J,),
            in_specs=[pl.BlockSpec((None, None, r, C), lambda j, core_ref: (j, core_ref[0], 0, 0)),
                      pl.BlockSpec((None, r, C), lambda j, core_ref: (j, 0, 0))],
            out_specs=pl.BlockSpec((None, r, C), lambda j, core_ref: (j, 0, 0))),
        out_shape=jax.ShapeDtypeStruct((J, r, C), BF16),
        compiler_params=_params(1),
    )(core, g32, recv)


def _chip_sum(g32, recv_sib, recv_chips, core_chip, name):
    J, _, r, C = g32.shape

    def body(idx_ref, g_ref, s_ref, o_ref_in, o_ref):
        total = g_ref[...] + s_ref[...].astype(F32)
        for k in range(3):
            total = total + o_ref_in[k].astype(F32)
        o_ref[...] = total

    return pl.pallas_call(
        body, name=name,
        grid_spec=pltpu.PrefetchScalarGridSpec(
            num_scalar_prefetch=1, grid=(1,),
            in_specs=[pl.BlockSpec((None, None, r, C), lambda i, idx: (idx[1], idx[0], 0, 0)),
                      pl.BlockSpec((None, r, C), lambda i, idx: (idx[1], 0, 0)),
                      pl.BlockSpec((3, r, C), lambda i, idx: (0, 0, 0))],
            out_specs=pl.BlockSpec((None, r, C), lambda i, idx: (idx[0], 0, 0))),
        out_shape=jax.ShapeDtypeStruct((2, r, C), F32),
        compiler_params=_params(1),
    )(core_chip, g32, recv_sib, recv_chips)


ICI_US_PER_ELEMENT = 4.6e-5


class _Reducer:
    def __init__(self, core_idx, core_chip):
        self.core_idx, self.core_chip = core_idx, core_chip
        self.grads, self.halves, self.reduced = {}, {}, {}
        self.ready_swap, self.ready_exchange, self.ready_join = [], [], []
        self.inflight, self.current = ([], [], [], None), None
        self.flushes = 0
        self.extra, self.extra_out = None, None

    def add(self, name, grad_pair):
        self.grads[name] = grad_pair
        self.ready_swap.append(name)

    def comm(self, budget_us):
        swaps, self.ready_swap = self.ready_swap, []
        joins, self.ready_join = self.ready_join, []
        exchanges, waiting = [], []
        for item in self.ready_exchange:
            cost = ICI_US_PER_ELEMENT * 2 * item[2].shape[1] * item[2].shape[2]
            if cost <= budget_us:
                exchanges.append(item)
                budget_us -= cost
            else:
                waiting.append(item)
        self.ready_exchange = waiting
        parts = []
        if swaps:
            parts.append(_SwapComm([self.grads[n][1] for n in swaps]))
        if exchanges:
            parts.append(_ExchangeComm([pair for _, _, pair in exchanges]))
        if joins:
            parts.append(_JoinComm([self.halves[n] for n in joins]))
        extra, self.extra = self.extra, None
        if extra is not None:
            parts.append(extra)
        self.inflight = (swaps, exchanges, joins, extra)
        self.current = _CommList(parts) if parts else None
        return self.current

    def done(self, comm_outs):
        if self.current is None:
            return
        swaps, exchanges, joins, extra = self.inflight
        outs = iter(self.current.split_outputs(list(comm_outs)))
        if swaps:
            for n, recv in zip(swaps, next(outs)):
                pair = _pair_sum(self.grads[n][0], recv, self.core_idx, "pair_sum_" + n)
                self.ready_exchange.append((n, recv, pair))
        if exchanges:
            for (n, recv, _), chips in zip(exchanges, next(outs)):
                self.halves[n] = _chip_sum(self.grads[n][0], recv, chips, self.core_chip, "chip_sum_" + n)
                self.ready_join.append(n)
        if joins:
            self.reduced.update(zip(joins, next(outs)))
        if extra is not None:
            self.extra_out = next(outs)
        self.current = None

    def run(self, kernel, budget_us, *args, **kwargs):
        if budget_us is None:
            return kernel(*args, comm=None, **kwargs)[0]
        outs, comm_outs = kernel(*args, comm=self.comm(budget_us), **kwargs)
        self.done(comm_outs)
        return outs

    def step(self):
        comm = self.comm(float("inf"))
        self.flushes += 1
        self.done(_run_comm(comm, "grad_reduce_tail_%d" % self.flushes))


BIG_WEIGHTS = ("ffn1_w_gate", "ffn1_w_up", "ffn1_w_down", "w_in", "w_attn_o", "w_conv_o", "w_out",
               "ffn2_w_gate", "ffn2_w_up", "ffn2_w_down")
VECTORS = ("norm_ffn1_g", "norm_mix_g", "conv_b_dw", "conv_ln_g", "conv_ln_b", "norm_ffn2_g", "final_norm_g")
ROW_DMOD0, ROW_DMOD1, ROW_VEC, ROW_SINK, ROW_CONVW, SMALL_ROWS = 0, 16, 33, 40, 41, 72


FFN1_WEIGHTS = ("ffn1_w_gate", "ffn1_w_up", "ffn1_w_down")
FFN2_WEIGHTS = ("ffn2_w_gate", "ffn2_w_up", "ffn2_w_down")
MIX_WEIGHTS = ("w_in", "w_attn_o", "w_conv_o", "w_out")
COL_SHARDED = ("ffn1_w_gate", "ffn1_w_up", "ffn2_w_gate", "ffn2_w_up", "w_in")


def _local_grads(x, target, mod, slots, ffn1_gathered, small, seq, core_idx, core_chip):
    T, D = x.shape
    B = T // seq
    mods = [mod[:, k][:, None, :] for k in range(N_MOD)]
    sh1, sc1, g1, sh2, sc2, g2, sh3, sc3, g3 = mods
    w = dict(zip(FFN1_WEIGHTS, ffn1_gathered))

    (h1, a1, u1, f1, x1), outs = _ffn_fwd(
        x, small["norm_ffn1_g"], sc1, sh1, g1, w["ffn1_w_gate"], w["ffn1_w_up"], w["ffn1_w_down"], seq, "ffn1_fwd",
        comm=_GatherComm([slots[n] for n in MIX_WEIGHTS]))
    w["w_in"] = outs[0]
    w_ao, w_co, w_o = [t.reshape(D, D) for t in outs[1:]]
    w_in_cols = _kernel_row_order(w["w_in"].reshape(IN_WIDTH, D))
    (h2, proj), _ = _in_proj(x1, small["norm_mix_g"], sc2, sh2, w_in_cols, seq)
    (o, lse), (w["ffn2_w_gate"], w["ffn2_w_up"]) = _attn_fwd(
        proj, small["attn_sinks"], B, seq, comm=_GatherComm([slots["ffn2_w_gate"], slots["ffn2_w_up"]]))
    (ydw, z), (w["ffn2_w_down"],) = _conv_fwd(
        proj, small["conv_w_dw"], small["conv_b_dw"], small["conv_ln_g"], small["conv_ln_b"], B, seq,
        comm=_GatherComm([slots["ffn2_w_down"]]))
    ya, yc, merged, mo, x2 = _merge(o, z, proj, w_ao, w_co, w_o, x1, g2, seq)
    (h3, a3, u3, f3, x3), _ = _ffn_fwd(x2, small["norm_ffn2_g"], sc3, sh3, g3, w["ffn2_w_gate"], w["ffn2_w_up"],
                                       w["ffn2_w_down"], seq, "ffn2_fwd")
    dx3, loss_parts, d_final_g = _final_loss(x3, small["final_norm_g"], target)

    red = _Reducer(core_idx, core_chip)

    def weight_grad(name, budget_us, a, a_spec, b, b_spec, rows, cols):
        red.add(name, red.run(_wgrad, budget_us, a, a_spec, b, b_spec, rows, cols, T, "dw_" + name))

    def ffn_backward(prefix, dw_budget_us, dxo, xin, h, a, u, f, gn, sc, gate, before_weight_grads=None):
        da, du, s, df, dx, dgate, dsc, dsh, dgn = red.run(
            _ffn_bwd, 170, dxo, xin, f, a, u, gn, sc, gate, w[prefix + "_w_gate"], w[prefix + "_w_up"],
            w[prefix + "_w_down"], seq, prefix + "_bwd")
        if before_weight_grads is not None:
            before_weight_grads(dgate, dsc, dsh, dgn)
        weight_grad(prefix + "_w_down", dw_budget_us, s, _spec_chip_major(FF_SHARD), df, _spec_rows(D), FF_SHARD, D)
        weight_grad(prefix + "_w_gate", dw_budget_us, da, _spec_chip_major(FF_SHARD), h, _spec_rows(D), FF_SHARD, D)
        weight_grad(prefix + "_w_up", dw_budget_us, du, _spec_chip_major(FF_SHARD), h, _spec_rows(D), FF_SHARD, D)
        return dx, dgate, dsc, dsh, dgn

    dx2, dg3, dsc3, dsh3, d_gn3 = ffn_backward("ffn2", None, dx3, x2, h3, a3, u3, f3, small["norm_ffn2_g"], sc3, g3)

    dmo, dya, dyc, dga, dgc, do, dz, dg2 = red.run(_merge_bwd, 45, dx2, mo, g2, proj, ya, yc, w_o, w_ao, w_co, seq)
    shard = D // N_CHIP
    weight_grad("w_out", None, merged, _spec_col_block(shard), dmo, _spec_rows(D), shard, D)
    weight_grad("w_attn_o", None, o, _spec_col_block(shard), dya, _spec_rows(D), shard, D)
    weight_grad("w_conv_o", None, z, _spec_col_block(shard), dyc, _spec_rows(D), shard, D)
    dq, dkp, dko, dvp, dvo, dsink_steps = red.run(_attn_bwd, 100, proj, small["attn_sinks"], o, do, lse, B, seq)
    dca, dcb, d_conv_w, d_conv_b, d_ln_g, d_ln_b = red.run(
        _conv_bwd, 165, proj, dz, ydw, small["conv_w_dw"], small["conv_ln_g"], small["conv_ln_b"], B, seq)

    def band_sum(own, prev):
        prev = prev.reshape(B, seq // BLOCK, BLOCK, 2 * HEAD_DIM)
        moved = jnp.concatenate([prev[:, 1:], jnp.zeros_like(prev[:, :1])], axis=1)
        return (own + moved.reshape(T, 2 * HEAD_DIM)).astype(BF16)

    dkv = jnp.concatenate([band_sum(dko, dkp), band_sum(dvo, dvp)], axis=1)
    g32, g16 = lax.empty((IN_WIDTH, D), F32), lax.empty((IN_WIDTH, D), BF16)
    row_of = {"q": 0, "kv": D, "conv_a": D + 4 * HEAD_DIM, "conv_b": 2 * D + 4 * HEAD_DIM,
              "gate_a": 3 * D + 4 * HEAD_DIM, "gate_c": 4 * D + 4 * HEAD_DIM}
    for tag, piece in (("q", dq), ("kv", dkv), ("conv_a", dca), ("conv_b", dcb), ("gate_a", dga), ("gate_c", dgc)):
        g32, g16 = _wgrad_rows(piece, h2, g32, g16, row_of[tag], "dw_w_in_" + tag)
    red.add("w_in", tuple(g.reshape(N_CHIP, 2, IN_SHARD // 2, D) for g in (g32, g16)))
    dx1, dsc2, dsh2, d_gn2 = red.run(_in_proj_bwd, 90, (dq, dca, dcb, dga, dgc, dkv), w_in_cols, x1,
                                     small["norm_mix_g"], sc2, dx2, seq)

    def gather_small_grads(dg1, dsc1, dsh1, d_gn1):
        dmod = jnp.concatenate([dsh1, dsc1, dg1, dsh2, dsc2, dg2, dsh3, dsc3, dg3], axis=1)
        d_sinks = jnp.sum(dsink_steps, axis=0)
        vec_grads = {"norm_ffn1_g": d_gn1, "norm_mix_g": d_gn2, "conv_b_dw": d_conv_b, "conv_ln_g": d_ln_g,
                     "conv_ln_b": d_ln_b, "norm_ffn2_g": d_gn3, "final_norm_g": d_final_g}
        block = jnp.zeros((SMALL_ROWS, D), F32)
        block = block.at[ROW_DMOD0:ROW_DMOD0 + N_MOD].set(dmod[0]).at[ROW_DMOD1:ROW_DMOD1 + N_MOD].set(dmod[1])
        block = block.at[ROW_VEC:ROW_VEC + len(VECTORS)].set(jnp.concatenate([vec_grads[n] for n in VECTORS], axis=0))
        block = block.at[ROW_SINK, :2 * HEAD_DIM].set(d_sinks[0])
        block = block.at[ROW_CONVW:ROW_CONVW + CONV_WIDTH].set(d_conv_w[:CONV_WIDTH])
        red.extra = _Gather8Comm(block)

    dx0, _, _, _, _ = ffn_backward("ffn1", 38, dx1, x, h1, a1, u1, f1, small["norm_ffn1_g"], sc1, g1,
                                   before_weight_grads=gather_small_grads)
    return loss_parts, dx0, red, red.extra_out[0]


def kernel(x, c, w_ada, b_ada, norm_ffn1_g, ffn1_w_gate, ffn1_w_up, ffn1_w_down, norm_mix_g, w_in, attn_sinks, w_attn_o, conv_w_dw, conv_b_dw, conv_ln_g, conv_ln_b, w_conv_o, w_out, norm_ffn2_g, ffn2_w_gate, ffn2_w_up, ffn2_w_down, final_norm_g, loss_target, m_w_ada, m_b_ada, m_norm_ffn1_g, m_ffn1_w_gate, m_ffn1_w_up, m_ffn1_w_down, m_norm_mix_g, m_w_in, m_attn_sinks, m_w_attn_o, m_conv_w_dw, m_conv_b_dw, m_conv_ln_g, m_conv_ln_b, m_w_conv_o, m_w_out, m_norm_ffn2_g, m_ffn2_w_gate, m_ffn2_w_up, m_ffn2_w_down, m_final_norm_g, v_w_ada, v_b_ada, v_norm_ffn1_g, v_ffn1_w_gate, v_ffn1_w_up, v_ffn1_w_down, v_norm_mix_g, v_w_in, v_attn_sinks, v_w_attn_o, v_conv_w_dw, v_conv_b_dw, v_conv_ln_g, v_conv_ln_b, v_w_conv_o, v_w_out, v_norm_ffn2_g, v_ffn2_w_gate, v_ffn2_w_up, v_ffn2_w_down, v_final_norm_g):
    args = dict(locals())
    B, seq, D = x.shape
    T = B * seq
    xi, yi, ci = _position()
    chip = 2 * xi + yi
    dev = 4 * xi + 2 * yi + ci

    def shard_2d(prefix, name):
        t = args[prefix + name][0]
        return t.T if name in COL_SHARDED else t

    big = {n: shard_2d("", n) for n in BIG_WEIGHTS}
    final_g = final_norm_g[None, :]
    vec_w = {n: (args[n] if n != "final_norm_g" else final_g) for n in VECTORS}

    core_idx = jnp.reshape(ci, (1,)).astype(jnp.int32)
    chip_idx = jnp.reshape(chip, (1,)).astype(jnp.int32)
    core_chip = jnp.stack([ci, chip]).astype(jnp.int32)
    conv_cols = D // N_CHIP
    conv_flat = jnp.pad(conv_w_dw[0].reshape(-1), (0, 8 * D - CONV_WIDTH * conv_cols)).reshape(8, D)
    first_block = jnp.concatenate([jnp.pad(c, ((0, 8 - B), (0, 0))), conv_flat], axis=0)
    slots = {n: _cast_slot(big[n], chip_idx, "cast_" + n) for n in FFN1_WEIGHTS}
    later = [n for n in BIG_WEIGHTS if n not in FFN1_WEIGHTS]
    carried = _CommList([_GatherComm([slots[n] for n in FFN1_WEIGHTS]), _Gather8Comm(first_block)])
    later_slots, carried_outs = _cast_group([big[n] for n in later], ["later_weights"], carried)
    ffn1_gathered, (first,) = carried.split_outputs(carried_outs)
    slots.update(zip(later, later_slots))
    c_all = first[:, :B].reshape(N_DEV * B, D)
    conv_taps = first[::2, 8:].reshape(N_CHIP, 8 * D)[:, :CONV_WIDTH * conv_cols]
    conv_taps = conv_taps.reshape(N_CHIP, CONV_WIDTH, conv_cols).transpose(1, 0, 2).reshape(CONV_WIDTH, D)
    conv_taps = jnp.pad(conv_taps, ((0, CONV_PAD - CONV_WIDTH), (0, 0)))

    ada_cols = w_ada.shape[2]
    b_cols = lax.dynamic_slice(b_ada, (0, chip * ada_cols), (1, ada_cols))
    mod_part = _ada_fwd(c_all, w_ada[0], b_cols).reshape(N_DEV, B, ada_cols)
    mod = _mod_exchange(mod_part).transpose(1, 0, 2).reshape(B, N_MOD, D)

    small = dict(vec_w)
    small["attn_sinks"] = attn_sinks
    small["conv_w_dw"] = conv_taps

    loss_parts, dx, red, small_all = _local_grads(
        x.reshape(T, D), loss_target.reshape(T, D), mod, slots, ffn1_gathered, small, seq, core_idx, core_chip)

    loss = lax.psum((0.5 / D) * jnp.sum(loss_parts), ("x", "y", "c"))
    grad_x = dx.reshape(B, seq, D)
    out = {}


    def pack_small(prefix):
        rows = [args[prefix + "b_ada"].reshape(N_MOD, D)]
        rows += [args[prefix + n].reshape(1, D) for n in VECTORS]
        rows += [jnp.pad(args[prefix + "attn_sinks"], ((0, 0), (0, D - N_Q_HEADS)))]
        return jnp.pad(jnp.concatenate(rows, axis=0), ((0, 24 - N_MOD - len(VECTORS) - 1), (0, 0)))

    small_sum, sg, sd, sm, sv = _small_adam(small_all, pack_small(""), pack_small("m_"), pack_small("v_"),
                                           ROW_DMOD0, ROW_DMOD1, ROW_VEC)

    def unpack_small(t):
        res = {"b_ada": t[:N_MOD].reshape(1, N_MOD * D)}
        for k, n in enumerate(VECTORS):
            res[n] = t[N_MOD + k].reshape(args[n].shape)
        res["attn_sinks"] = t[N_MOD + len(VECTORS), :N_Q_HEADS].reshape(1, N_Q_HEADS)
        return res

    unpacked = [unpack_small(t) for t in (sg, sd, sm, sv)]
    for n in ("b_ada", "attn_sinks") + VECTORS:
        out[n] = tuple(u[n] for u in unpacked)

    conv_g = lax.dynamic_slice(small_sum, (ROW_CONVW, chip * conv_cols), (CONV_WIDTH, conv_cols))
    d, mn, vn = red.run(_adam_call, None, conv_w_dw[0], conv_g, m_conv_w_dw[0], v_conv_w_dw[0], "adam_conv_w_dw")
    out["conv_w_dw"] = tuple(t[None] for t in (conv_g, d, mn, vn))

    dmod_rows = jnp.stack([small_all[:, ROW_DMOD0:ROW_DMOD0 + N_MOD], small_all[:, ROW_DMOD1:ROW_DMOD1 + N_MOD]], axis=1)
    dmod_all = dmod_rows.reshape(N_DEV * B, N_MOD * D)
    dmod_cols = lax.dynamic_slice(dmod_all, (0, chip * ada_cols), (N_DEV * B, ada_cols))
    ada_out = red.run(_ada_adam, 35, c_all.T, dmod_cols, w_ada[0], m_w_ada[0], v_w_ada[0])
    out["w_ada"] = tuple(t[None] for t in ada_out)

    def finished(n):
        while n not in red.reduced:
            red.step()
        return red.reduced[n].reshape(big[n].shape)

    def emit(n, g, d, mn, vn):
        out[n] = tuple((t.T if n in COL_SHARDED else t)[None] for t in (g, d, mn, vn))

    early = FFN2_WEIGHTS + MIX_WEIGHTS
    early_g = [finished(n) for n in early]
    early_out = red.run(_adam_group, 45, [big[n] for n in early], early_g, [shard_2d("m_", n) for n in early],
                        [shard_2d("v_", n) for n in early], "adam_early")
    for n, g, (d, mn, vn) in zip(early, early_g, early_out):
        emit(n, g, d, mn, vn)
    for n in ("ffn1_w_down", "ffn1_w_gate", "ffn1_w_up"):
        g = finished(n)
        emit(n, g, *red.run(_adam_call, None, big[n], g, shard_2d("m_", n), shard_2d("v_", n), "adam_" + n))

    order = ("w_ada", "b_ada", "norm_ffn1_g", "ffn1_w_gate", "ffn1_w_up", "ffn1_w_down", "norm_mix_g", "w_in",
             "attn_sinks", "w_attn_o", "conv_w_dw", "conv_b_dw", "conv_ln_g", "conv_ln_b", "w_conv_o", "w_out",
             "norm_ffn2_g", "ffn2_w_gate", "ffn2_w_up", "ffn2_w_down", "final_norm_g")
    return (loss, grad_x, *[out[n][0] for n in order], *[out[n][1] for n in order],
            *[out[n][2] for n in order], *[out[n][3] for n in order])
```

```python
import functools

import jax
import jax.numpy as jnp
from jax import lax
from jax.experimental import pallas as pl
from jax.experimental.pallas import tpu as pltpu

F32 = jnp.float32
BF16 = jnp.bfloat16

D_MODEL = 1024
D_FF = 2816
N_CHIP = 4
N_DEV = 8
FF_SHARD = D_FF // N_CHIP
IN_WIDTH = 5376
IN_SHARD = IN_WIDTH // N_CHIP
HEAD_DIM = 64
N_Q_HEADS = 16
N_KV_HEADS = 2
BLOCK = 128
CONV_WIDTH = 31
CONV_PAD = 32
N_MOD = 9
EPS = 1e-6
FFN_RESIDUAL = 0.5
ATTN_SCALE = HEAD_DIM ** -0.5
MASK_VALUE = -1e30

ADAM_LR = 0.001
ADAM_B1 = 0.9
ADAM_B2 = 0.999
ADAM_EPS = 1e-08
ADAM_WD = 0.01
ADAM_STEP = 10

COLB_Q, COLB_CA, COLB_CB, COLB_GA, COLB_GC = 0, 1, 2, 3, 4
COLB_K, COLB_V = 40, 41
PROJ_TILE = 768

VMEM_LIMIT = 56 * 1024 * 1024
MESH = pl.DeviceIdType.MESH
ANY = pl.BlockSpec(memory_space=pl.ANY)
VMEM_SPEC = pl.BlockSpec(memory_space=pltpu.VMEM)
SMEM_SPEC = pl.BlockSpec(memory_space=pltpu.SMEM)


def _params(n_grid):
    return pltpu.CompilerParams(dimension_semantics=("arbitrary",) * n_grid, vmem_limit_bytes=VMEM_LIMIT)


def _tile(n, pref):
    t = min(n, pref)
    while n % t:
        t //= 2
    return t


def _row_tile(rows, cap):
    for t in range(min(rows, cap) // 16 * 16, 0, -16):
        if rows % t == 0:
            return t
    return rows


def _sigmoid(v):
    return 1.0 / (1.0 + jnp.exp(-v))


def _dot_nn(a, b):
    return lax.dot_general(a, b, (((1,), (0,)), ((), ())), preferred_element_type=F32)


def _dot_nt(a, b):
    return lax.dot_general(a, b, (((1,), (1,)), ((), ())), preferred_element_type=F32)


def _dot_tn(a, b):
    return lax.dot_general(a, b, (((0,), (0,)), ((), ())), preferred_element_type=F32)


ROW_CHUNK = 16


def _for_row_chunks(n_rows, fn):
    for r in range(0, n_rows, ROW_CHUNK):
        fn(slice(r, r + ROW_CHUNK))


def _norm_mod(xv, gn, sc, sh):
    r = lax.rsqrt(jnp.mean(xv * xv, axis=-1, keepdims=True) + EPS)
    return ((xv * r) * gn) * (1.0 + sc) + sh


def _accumulate(ref, first, value):
    @pl.when(first)
    def _():
        ref[...] = value

    @pl.when(jnp.logical_not(first))
    def _():
        ref[...] += value


def _norm_mod_bwd(dh, xv, gn, sc, dxo, first_of_batch, first, dx_ref, dsc_ref, dsh_ref, dgn_ref):
    r = lax.rsqrt(jnp.mean(xv * xv, axis=-1, keepdims=True) + EPS)
    xh = xv * r
    _accumulate(dsh_ref, first_of_batch, jnp.sum(dh, axis=0, keepdims=True))
    _accumulate(dsc_ref, first_of_batch, jnp.sum(dh * (xh * gn), axis=0, keepdims=True))
    dn = dh * (1.0 + sc)
    _accumulate(dgn_ref, first, jnp.sum(dn * xh, axis=0, keepdims=True))
    dxh = dn * gn
    dx_ref[...] = dxo + r * (dxh - xh * jnp.mean(dxh * xh, axis=-1, keepdims=True))


CHIP_FLIPS = ((1, 0), (0, 1), (1, 1))


def _position():
    return lax.axis_index("x"), lax.axis_index("y"), lax.axis_index("c")


def _flip(v, f):
    return 1 - v if f else v


class _GatherComm:
    def __init__(self, bufs):
        n = len(bufs)
        self.n = n
        self.operands = list(bufs)
        self.out_shape = [jax.ShapeDtypeStruct(b.shape, b.dtype) for b in bufs]
        self.aliases = {i: i for i in range(n)}
        self.sems = [pltpu.SemaphoreType.DMA((6 * n,)), pltpu.SemaphoreType.DMA((6 * n,))]
        self.rows = [b.shape[1] // 2 for b in bufs]

    def _half(self, ref, i, which):
        return ref.at[pl.ds(which * self.rows[i], self.rows[i]), :]

    def _ici(self, cins, couts, sems, i, k, dst_chip, to):
        x, y, c = _position()
        return pltpu.make_async_remote_copy(
            src_ref=self._half(cins[i].at[2 * x + y], i, c), dst_ref=self._half(couts[i].at[dst_chip], i, c),
            send_sem=sems[0].at[3 * i + k], recv_sem=sems[1].at[3 * i + k], device_id=to, device_id_type=MESH)

    def _d2d(self, couts, sems, i, k, src_chip, which):
        x, y, c = _position()
        place = self._half(couts[i].at[src_chip], i, which)
        return pltpu.make_async_remote_copy(
            src_ref=place, dst_ref=place, send_sem=sems[0].at[3 * self.n + 3 * i + k],
            recv_sem=sems[1].at[3 * self.n + 3 * i + k], device_id=(x, y, 1 - c), device_id_type=MESH)

    def _peers(self):
        x, y, _ = _position()
        return [(_flip(x, fx), _flip(y, fy)) for fx, fy in CHIP_FLIPS]

    def start(self, cins, couts, sems):
        x, y, c = _position()
        for i in range(self.n):
            for k, (px, py) in enumerate(self._peers()):
                self._ici(cins, couts, sems, i, k, 2 * x + y, (px, py, c)).start()

    def finish(self, cins, couts, sems):
        _, _, c = _position()
        peers = self._peers()
        for i in range(self.n):
            for k, (px, py) in enumerate(peers):
                self._ici(cins, couts, sems, i, k, 2 * px + py, (px, py, c)).wait_recv()
                self._d2d(couts, sems, i, k, 2 * px + py, c).start()
        for i in range(self.n):
            for k, (px, py) in enumerate(peers):
                self._d2d(couts, sems, i, k, 2 * px + py, 1 - c).wait_recv()
        for i in range(self.n):
            for k, (px, py) in enumerate(peers):
                self._ici(cins, couts, sems, i, k, 2 * px + py, (px, py, c)).wait_send()
                self._d2d(couts, sems, i, k, 2 * px + py, c).wait_send()


class _ExchangeComm:
    def __init__(self, pairs):
        n = len(pairs)
        self.n = n
        self.operands = list(pairs)
        self.out_shape = [jax.ShapeDtypeStruct((3,) + p.shape[1:], p.dtype) for p in pairs]
        self.aliases = {}
        self.sems = [pltpu.SemaphoreType.DMA((3 * n,)), pltpu.SemaphoreType.DMA((3 * n,))]

    def _copies(self, cins, couts, sems):
        x, y, c = _position()
        peers = [(_flip(x, fx), _flip(y, fy)) for fx, fy in CHIP_FLIPS]
        return [pltpu.make_async_remote_copy(
            src_ref=cins[i].at[2 * px + py], dst_ref=couts[i].at[k], send_sem=sems[0].at[3 * i + k],
            recv_sem=sems[1].at[3 * i + k], device_id=(px, py, c), device_id_type=MESH)
            for i in range(self.n) for k, (px, py) in enumerate(peers)]

    def start(self, cins, couts, sems):
        for cp in self._copies(cins, couts, sems):
            cp.start()

    def finish(self, cins, couts, sems):
        for cp in self._copies(cins, couts, sems):
            cp.wait()


class _SwapComm:
    def __init__(self, grads16):
        n = len(grads16)
        self.n = n
        self.operands = list(grads16)
        self.out_shape = [jax.ShapeDtypeStruct(g.shape[:1] + g.shape[2:], g.dtype) for g in grads16]
        self.aliases = {}
        self.sems = [pltpu.SemaphoreType.DMA((n,)), pltpu.SemaphoreType.DMA((n,))]

    def _copies(self, cins, couts, sems):
        x, y, c = _position()
        return [pltpu.make_async_remote_copy(
            src_ref=cins[i].at[:, 1 - c], dst_ref=couts[i], send_sem=sems[0].at[i], recv_sem=sems[1].at[i],
            device_id=(x, y, 1 - c), device_id_type=MESH) for i in range(self.n)]

    def start(self, cins, couts, sems):
        for cp in self._copies(cins, couts, sems):
            cp.start()

    def finish(self, cins, couts, sems):
        for cp in self._copies(cins, couts, sems):
            cp.wait()


class _JoinComm:
    def __init__(self, halves):
        n = len(halves)
        self.n = n
        self.operands = list(halves)
        self.out_shape = [jax.ShapeDtypeStruct(h.shape, h.dtype) for h in halves]
        self.aliases = {i: i for i in range(n)}
        self.sems = [pltpu.SemaphoreType.DMA((n,)), pltpu.SemaphoreType.DMA((n,))]

    def _copy(self, cins, couts, sems, i, which):
        x, y, c = _position()
        return pltpu.make_async_remote_copy(
            src_ref=cins[i].at[which], dst_ref=couts[i].at[which], send_sem=sems[0].at[i], recv_sem=sems[1].at[i],
            device_id=(x, y, 1 - c), device_id_type=MESH)

    def start(self, cins, couts, sems):
        _, _, c = _position()
        for i in range(self.n):
            self._copy(cins, couts, sems, i, c).start()

    def finish(self, cins, couts, sems):
        _, _, c = _position()
        for i in range(self.n):
            self._copy(cins, couts, sems, i, 1 - c).wait_recv()
        for i in range(self.n):
            self._copy(cins, couts, sems, i, c).wait_send()


class _Gather8Comm:
    def __init__(self, block):
        self.operands = [block]
        self.out_shape = [jax.ShapeDtypeStruct((N_DEV,) + block.shape, block.dtype)]
        self.aliases = {}
        self.sems = [pltpu.SemaphoreType.DMA((N_DEV - 1,)), pltpu.SemaphoreType.DMA((N_DEV - 1,)),
                     pltpu.SemaphoreType.DMA]
        self.flips = [(fx, fy, fc) for fx in (0, 1) for fy in (0, 1) for fc in (0, 1) if (fx, fy, fc) != (0, 0, 0)]

    def _peers(self):
        x, y, c = _position()
        return [(_flip(x, fx), _flip(y, fy), _flip(c, fc)) for fx, fy, fc in self.flips]

    def _copy(self, cins, couts, sems, k, block, to):
        return pltpu.make_async_remote_copy(src_ref=cins[0], dst_ref=couts[0].at[block], send_sem=sems[0].at[k],
                                            recv_sem=sems[1].at[k], device_id=to, device_id_type=MESH)

    def _mine(self, cins, couts, sems):
        x, y, c = _position()
        return pltpu.make_async_copy(cins[0], couts[0].at[4 * x + 2 * y + c], sems[2])

    def start(self, cins, couts, sems):
        x, y, c = _position()
        self._mine(cins, couts, sems).start()
        for k, peer in enumerate(self._peers()):
            self._copy(cins, couts, sems, k, 4 * x + 2 * y + c, peer).start()

    def finish(self, cins, couts, sems):
        for k, (px, py, pc) in enumerate(self._peers()):
            self._copy(cins, couts, sems, k, 4 * px + 2 * py + pc, (px, py, pc)).wait_recv()
        for k, peer in enumerate(self._peers()):
            self._copy(cins, couts, sems, k, 0, peer).wait_send()
        self._mine(cins, couts, sems).wait()


class _CommList:
    def __init__(self, parts):
        self.parts = list(parts)
        self.operands = [t for p in self.parts for t in p.operands]
        self.out_shape = [t for p in self.parts for t in p.out_shape]
        self.sems = [t for p in self.parts for t in p.sems]
        self.aliases = {}
        n_in = n_out = 0
        for p in self.parts:
            self.aliases.update({n_in + i: n_out + j for i, j in p.aliases.items()})
            n_in += len(p.operands)
            n_out += len(p.out_shape)

    def _split(self, cins, couts, sems):
        pos = [0, 0, 0]
        for p in self.parts:
            sizes = (len(p.operands), len(p.out_shape), len(p.sems))
            yield p, tuple(seq[a:a + k] for seq, a, k in zip((cins, couts, sems), pos, sizes))
            pos = [a + k for a, k in zip(pos, sizes)]

    def start(self, cins, couts, sems):
        for p, refs in self._split(cins, couts, sems):
            p.start(*refs)

    def finish(self, cins, couts, sems):
        for p, refs in self._split(cins, couts, sems):
            p.finish(*refs)

    def split_outputs(self, outs):
        res, pos = [], 0
        for p in self.parts:
            res.append(outs[pos:pos + len(p.out_shape)])
            pos += len(p.out_shape)
        return res


def _call(body, *, name, grid, in_specs, out_specs, out_shape, operands, scratch_shapes=(), comm=None):
    n_grid = len(grid)
    if comm is None:
        return pl.pallas_call(
            body, name=name, grid=grid, in_specs=list(in_specs), out_specs=list(out_specs), out_shape=list(out_shape),
            scratch_shapes=list(scratch_shapes), compiler_params=_params(n_grid))(*operands), ()
    counts = (len(in_specs), len(comm.operands), len(out_specs), len(comm.out_shape), len(scratch_shapes),
              len(comm.sems))

    def fused(*refs):
        parts, pos = [], 0
        for k in counts:
            parts.append(refs[pos:pos + k])
            pos += k
        ins, cins, outs, couts, scr, sems = parts
        first = functools.reduce(jnp.logical_and, [pl.program_id(d) == 0 for d in range(n_grid)])
        last = functools.reduce(jnp.logical_and, [pl.program_id(d) == grid[d] - 1 for d in range(n_grid)])

        @pl.when(first)
        def _():
            comm.start(cins, couts, sems)

        body(*ins, *outs, *scr)

        @pl.when(last)
        def _():
            comm.finish(cins, couts, sems)

    res = pl.pallas_call(
        fused, name=name, grid=grid, in_specs=list(in_specs) + [ANY] * counts[1],
        out_specs=list(out_specs) + [ANY] * counts[3], out_shape=list(out_shape) + list(comm.out_shape),
        scratch_shapes=list(scratch_shapes) + list(comm.sems),
        input_output_aliases={counts[0] + i: counts[2] + j for i, j in comm.aliases.items()},
        compiler_params=_params(n_grid))(*operands, *comm.operands)
    return res[:counts[2]], res[counts[2]:]


def _run_comm(comm, name):
    k_in, k_out = len(comm.operands), len(comm.out_shape)

    def body(*refs):
        cins, couts, sems = refs[:k_in], refs[k_in:k_in + k_out], refs[k_in + k_out:]
        comm.start(cins, couts, sems)
        comm.finish(cins, couts, sems)

    return pl.pallas_call(
        body, name=name, in_specs=[ANY] * k_in, out_specs=[ANY] * k_out, out_shape=list(comm.out_shape),
        scratch_shapes=list(comm.sems), input_output_aliases=dict(comm.aliases))(*comm.operands)


def _ffn_fwd(x, gn, sc, sh, gate, wg, wu, wd, seq, name, comm=None):
    T, D = x.shape
    J, Fs, _ = wg.shape
    tm = _tile(seq, 1024)
    nb = seq // tm

    def body(x_ref, gn_ref, sc_ref, sh_ref, gate_ref, wg_ref, wu_ref, wd_ref,
             h_ref, a_ref, u_ref, f_ref, xo_ref, hs, acc, s16):
        j = pl.program_id(1)

        @pl.when(j == 0)
        def _():
            hb = _norm_mod(x_ref[...], gn_ref[...], sc_ref[...], sh_ref[...]).astype(BF16)
            hs[...] = hb
            h_ref[...] = hb
            acc[...] = jnp.zeros_like(acc)

        hb = hs[...]
        a_all = _dot_nt(hb, wg_ref[...])
        u_all = _dot_nt(hb, wu_ref[...])

        def swiglu_rows(rows):
            a = a_all[rows, :]
            u = u_all[rows, :]
            a_ref[rows, :] = a.astype(BF16)
            u_ref[rows, :] = u.astype(BF16)
            s16[rows, :] = ((a * _sigmoid(a)) * u).astype(BF16)

        _for_row_chunks(tm, swiglu_rows)
        acc[...] += _dot_nn(s16[...], wd_ref[...])

        @pl.when(j == J - 1)
        def _():
            f = acc[...]
            f_ref[...] = f.astype(BF16)
            xo_ref[...] = x_ref[...] + (FFN_RESIDUAL * gate_ref[...]) * f

    row = pl.BlockSpec((tm, D), lambda i, j: (i, 0))
    vec = pl.BlockSpec((1, D), lambda i, j: (0, 0))
    per_b = pl.BlockSpec((None, 1, D), lambda i, j: (i // nb, 0, 0))
    hid = pl.BlockSpec((None, tm, Fs), lambda i, j: (j, i, 0))
    return _call(
        body, name=name, grid=(T // tm, J),
        in_specs=[row, vec, per_b, per_b, per_b] + [pl.BlockSpec((None, Fs, D), lambda i, j: (j, 0, 0))] * 3,
        out_specs=[row, hid, hid, row, row],
        out_shape=[jax.ShapeDtypeStruct((T, D), BF16), jax.ShapeDtypeStruct((J, T, Fs), BF16),
                   jax.ShapeDtypeStruct((J, T, Fs), BF16), jax.ShapeDtypeStruct((T, D), BF16),
                   jax.ShapeDtypeStruct((T, D), F32)],
        scratch_shapes=[pltpu.VMEM((tm, D), BF16), pltpu.VMEM((tm, D), F32), pltpu.VMEM((tm, Fs), BF16)],
        operands=(x, gn, sc, sh, gate, wg, wu, wd), comm=comm)


def _ffn_bwd(dxo, x, f, a, u, gn, sc, gate, wg, wu, wd, seq, name, comm=None):
    T, D = x.shape
    J, Fs, _ = wg.shape
    B = T // seq
    tm = _tile(seq, 512)
    nb = seq // tm

    def body(dxo_ref, x_ref, f_ref, a_ref, u_ref, gn_ref, sc_ref, gate_ref, wg_ref, wu_ref, wd_ref,
             da_ref, du_ref, s_ref, df_ref, dx_ref, dgate_ref, dsc_ref, dsh_ref, dgn_ref, dfs, acc):
        i = pl.program_id(0)
        j = pl.program_id(1)
        first_of_batch = i % nb == 0

        @pl.when(j == 0)
        def _():
            dxo_v = dxo_ref[...]
            dfb = ((FFN_RESIDUAL * gate_ref[...]) * dxo_v).astype(BF16)
            dfs[...] = dfb
            df_ref[...] = dfb
            part = jnp.sum((FFN_RESIDUAL * f_ref[...].astype(F32)) * dxo_v, axis=0, keepdims=True)
            _accumulate(dgate_ref, first_of_batch, part)
            acc[...] = jnp.zeros_like(acc)

        ds_all = _dot_nt(dfs[...], wd_ref[...])

        def swiglu_bwd_rows(rows):
            ds = ds_all[rows, :]
            av = a_ref[rows, :].astype(F32)
            uv = u_ref[rows, :].astype(F32)
            sig = _sigmoid(av)
            sil = av * sig
            s_ref[rows, :] = (sil * uv).astype(BF16)
            da_ref[rows, :] = (ds * uv * (sig * (1.0 + av * (1.0 - sig)))).astype(BF16)
            du_ref[rows, :] = (ds * sil).astype(BF16)

        _for_row_chunks(tm, swiglu_bwd_rows)
        acc[...] += _dot_nn(da_ref[...], wg_ref[...]) + _dot_nn(du_ref[...], wu_ref[...])

        @pl.when(j == J - 1)
        def _():
            _norm_mod_bwd(acc[...], x_ref[...], gn_ref[...], sc_ref[...], dxo_ref[...],
                          first_of_batch, i == 0, dx_ref, dsc_ref, dsh_ref, dgn_ref)

    row = pl.BlockSpec((tm, D), lambda i, j: (i, 0))
    vec = pl.BlockSpec((1, D), lambda i, j: (0, 0))
    per_b = pl.BlockSpec((None, 1, D), lambda i, j: (i // nb, 0, 0))
    hid = pl.BlockSpec((None, tm, Fs), lambda i, j: (j, i, 0))
    hid_shape = jax.ShapeDtypeStruct((J, T, Fs), BF16)
    per_b_shape = jax.ShapeDtypeStruct((B, 1, D), F32)
    return _call(
        body, name=name, grid=(T // tm, J),
        in_specs=[row, row, row, hid, hid, vec, per_b, per_b]
        + [pl.BlockSpec((None, Fs, D), lambda i, j: (j, 0, 0))] * 3,
        out_specs=[hid, hid, hid, row, row, per_b, per_b, per_b, vec],
        out_shape=[hid_shape, hid_shape, hid_shape, jax.ShapeDtypeStruct((T, D), BF16),
                   jax.ShapeDtypeStruct((T, D), F32), per_b_shape, per_b_shape, per_b_shape,
                   jax.ShapeDtypeStruct((1, D), F32)],
        scratch_shapes=[pltpu.VMEM((tm, D), BF16), pltpu.VMEM((tm, D), F32)],
        operands=(dxo, x, f, a, u, gn, sc, gate, wg, wu, wd), comm=comm)


def _wgrad(a, a_spec, b, b_spec, rows, cols, n_tok, name, comm=None):
    tk = _tile(n_tok, 4096)
    nk = n_tok // tk
    half = rows // 2

    def body(a_ref, b_ref, o32_ref, o16_ref, acc):
        k = pl.program_id(1)

        @pl.when(k == 0)
        def _():
            acc[...] = jnp.zeros_like(acc)

        acc[...] += _dot_tn(a_ref[...], b_ref[...])

        @pl.when(k == nk - 1)
        def _():
            for h in range(2):
                v = acc[h * half:(h + 1) * half, :]
                o32_ref[h] = v
                o16_ref[h] = v.astype(BF16)

    out_spec = pl.BlockSpec((None, 2, half, cols), lambda j, k: (j, 0, 0, 0))
    return _call(
        body, name=name, grid=(N_CHIP, nk),
        in_specs=[a_spec(tk), b_spec(tk)],
        out_specs=[out_spec, out_spec],
        out_shape=[jax.ShapeDtypeStruct((N_CHIP, 2, half, cols), F32),
                   jax.ShapeDtypeStruct((N_CHIP, 2, half, cols), BF16)],
        scratch_shapes=[pltpu.VMEM((rows, cols), F32)],
        operands=(a, b), comm=comm)


def _spec_rows(width):
    return lambda tk: pl.BlockSpec((tk, width), lambda j, k: (k, 0))


def _spec_chip_major(width):
    return lambda tk: pl.BlockSpec((None, tk, width), lambda j, k: (j, k, 0))


def _spec_col_block(width):
    return lambda tk: pl.BlockSpec((tk, width), lambda j, k: (k, j))


def _in_proj(x, gn, sc, sh, w_in, seq, comm=None):
    T, D = x.shape
    N = w_in.shape[0]
    tm = _tile(seq, 2048)
    nb = seq // tm

    def body(x_ref, gn_ref, sc_ref, sh_ref, w_ref, h_ref, p_ref, hs):
        @pl.when(pl.program_id(1) == 0)
        def _():
            hb = _norm_mod(x_ref[...], gn_ref[...], sc_ref[...], sh_ref[...]).astype(BF16)
            hs[...] = hb
            h_ref[...] = hb

        p_ref[...] = _dot_nt(hs[...], w_ref[...]).astype(BF16)

    row = pl.BlockSpec((tm, D), lambda i, j: (i, 0))
    per_b = pl.BlockSpec((None, 1, D), lambda i, j: (i // nb, 0, 0))
    return _call(
        body, name="mix_in_proj", grid=(T // tm, N // PROJ_TILE),
        in_specs=[row, pl.BlockSpec((1, D), lambda i, j: (0, 0)), per_b, per_b,
                  pl.BlockSpec((PROJ_TILE, D), lambda i, j: (j, 0))],
        out_specs=[row, pl.BlockSpec((tm, PROJ_TILE), lambda i, j: (i, j))],
        out_shape=[jax.ShapeDtypeStruct((T, D), BF16), jax.ShapeDtypeStruct((T, N), BF16)],
        scratch_shapes=[pltpu.VMEM((tm, D), BF16)],
        operands=(x, gn, sc, sh, w_in), comm=comm)


def _attn_specs(nblk):
    def own(col):
        return lambda b, n: (b * nblk + n, col)

    def prev(col):
        return lambda b, n: (b * nblk + jnp.maximum(n - 1, 0), col)

    kv = (BLOCK, 2 * HEAD_DIM)
    return [pl.BlockSpec((BLOCK, D_MODEL), own(COLB_Q)),
            pl.BlockSpec(kv, prev(COLB_K)), pl.BlockSpec(kv, own(COLB_K)),
            pl.BlockSpec(kv, prev(COLB_V)), pl.BlockSpec(kv, own(COLB_V))]


def _band_operands(prev_ref, own_ref, lo):
    band = jnp.concatenate([prev_ref[...], own_ref[...]], axis=0).astype(F32)
    rolled = pltpu.roll(band, HEAD_DIM, 1)
    zero = jnp.zeros_like(band)
    head0 = jnp.concatenate([jnp.where(lo, band, zero), jnp.where(lo, zero, rolled)], axis=0).astype(BF16)
    head1 = jnp.concatenate([jnp.where(lo, rolled, zero), jnp.where(lo, zero, band)], axis=0).astype(BF16)
    return head0, head1


PAIRS_PER_KV = N_Q_HEADS // 2 // N_KV_HEADS
BAND = 2 * BLOCK


def _band_valid(has_prev):
    qi = lax.broadcasted_iota(jnp.int32, (PAIRS_PER_KV * BLOCK, BAND), 0) & (BLOCK - 1)
    sj = lax.broadcasted_iota(jnp.int32, (PAIRS_PER_KV * BLOCK, BAND), 1)
    rel = qi + BLOCK - sj
    return (rel >= 0) & (rel < BLOCK) & ((sj >= BLOCK) | has_prev)


def _pair_lanes(kvh, pp):
    pair = kvh * PAIRS_PER_KV + pp
    return slice(pair * 2 * HEAD_DIM, (pair + 1) * 2 * HEAD_DIM)


def _stack_pairs(ref, kvh):
    return jnp.concatenate([ref[:, _pair_lanes(kvh, pp)] for pp in range(PAIRS_PER_KV)], axis=0)


def _rows_per_pair(columns):
    return jnp.concatenate(columns, axis=0)


def _attn_fwd(proj, sinks, batch, seq, comm=None):
    T = proj.shape[0]
    nblk = seq // BLOCK

    def body(sink_ref, q_ref, kp_ref, ko_ref, vp_ref, vo_ref, o_ref, lse_ref):
        lo = lax.broadcasted_iota(jnp.int32, (1, 2 * HEAD_DIM), 1) < HEAD_DIM
        head_lane = lax.broadcasted_iota(jnp.int32, (1, N_Q_HEADS), 1)
        valid = _band_valid(pl.program_id(1) > 0)
        k_ops = _band_operands(kp_ref, ko_ref, lo)
        v_ops = _band_operands(vp_ref, vo_ref, lo)
        lse_all = jnp.zeros((BLOCK, N_Q_HEADS), F32)
        col = jnp.zeros((BLOCK, 1), F32)
        side0_row = lax.broadcasted_iota(jnp.int32, (2 * BAND, 2 * HEAD_DIM), 0) < BAND
        low_lane = lax.broadcasted_iota(jnp.int32, (2 * BAND, 2 * HEAD_DIM), 1) < HEAD_DIM
        side_ones = jnp.where(side0_row == low_lane, 1.0, 0.0).astype(BF16)
        for kvh in range(N_KV_HEADS):
            s_all = _dot_nt(_stack_pairs(q_ref, kvh), k_ops[kvh]) * ATTN_SCALE
            weights, maxes, sink_terms = [], [], []
            for side in range(2):
                heads = [2 * (kvh * PAIRS_PER_KV + pp) + side for pp in range(PAIRS_PER_KV)]
                sink = _rows_per_pair([col + sink_ref[0, h] for h in heads])
                s = jnp.where(valid, s_all[:, side * BAND:(side + 1) * BAND], MASK_VALUE)
                m = jnp.maximum(jnp.max(s, axis=-1, keepdims=True), sink)
                weights.append(jnp.where(valid, jnp.exp(s - m), 0.0).astype(BF16))
                maxes.append(m)
                sink_terms.append(jnp.exp(sink - m))
            p_all = jnp.concatenate(weights, axis=1)
            den = _dot_nn(p_all, side_ones) + jnp.where(lo, sink_terms[0], sink_terms[1])
            out = _dot_nn(p_all, v_ops[kvh]) / den
            for pp in range(PAIRS_PER_KV):
                o_ref[:, _pair_lanes(kvh, pp)] = out[pp * BLOCK:(pp + 1) * BLOCK].astype(BF16)
            for side in range(2):
                lse = maxes[side] + jnp.log(den[:, side * HEAD_DIM:side * HEAD_DIM + 1])
                for pp in range(PAIRS_PER_KV):
                    h = 2 * (kvh * PAIRS_PER_KV + pp) + side
                    lse_all = jnp.where(head_lane == h, lse[pp * BLOCK:(pp + 1) * BLOCK], lse_all)
        lse_ref[...] = lse_all

    return _call(
        body, name="attn_fwd", grid=(batch, nblk),
        in_specs=[SMEM_SPEC] + _attn_specs(nblk),
        out_specs=[pl.BlockSpec((BLOCK, D_MODEL), lambda b, n: (b * nblk + n, 0)),
                   pl.BlockSpec((BLOCK, N_Q_HEADS), lambda b, n: (b * nblk + n, 0))],
        out_shape=[jax.ShapeDtypeStruct((T, D_MODEL), BF16), jax.ShapeDtypeStruct((T, N_Q_HEADS), F32)],
        operands=(sinks, proj, proj, proj, proj, proj), comm=comm)


def _conv_u(ca, cb):
    return ca.astype(F32) * _sigmoid(cb.astype(F32))


def _conv_specs(ts, tiles_per_seq):
    per_tile = ts // CONV_PAD

    def tile(col):
        return lambda b, t: (b * tiles_per_seq + t, col)

    def before(col):
        return lambda b, t: (jnp.maximum((b * tiles_per_seq + t) * per_tile - 1, 0), col)

    return [pl.BlockSpec((ts, D_MODEL), tile(COLB_CA)), pl.BlockSpec((ts, D_MODEL), tile(COLB_CB)),
            pl.BlockSpec((CONV_PAD, D_MODEL), before(COLB_CA)), pl.BlockSpec((CONV_PAD, D_MODEL), before(COLB_CB))]


SUBLANES = 8


def _fill_upad(upad, ca_ref, cb_ref, cah_ref, cbh_ref, t):
    halo = _conv_u(cah_ref[...], cbh_ref[...])
    upad[0, 0:CONV_PAD, :] = jnp.where(t > 0, halo, jnp.zeros_like(halo))
    upad[0, CONV_PAD:, :] = _conv_u(ca_ref[...], cb_ref[...])


def _fill_shifted(pad):
    rows = pad.shape[1] - SUBLANES
    for b in range(1, SUBLANES):
        pad[b, 0:rows, :] = pad[0, b:b + rows, :]


def _shifted_rows(pad, offset, rows):
    b = offset % SUBLANES
    return pad[b, offset - b:offset - b + rows, :]


def _layernorm_stats(y):
    mu = jnp.mean(y, axis=-1, keepdims=True)
    yc = y - mu
    rstd = lax.rsqrt(jnp.mean(yc * yc, axis=-1, keepdims=True) + EPS)
    return yc * rstd, rstd


def _conv_fwd(proj, w_dw, b_dw, ln_g, ln_b, batch, seq, comm=None):
    T = proj.shape[0]
    ts = _tile(seq, 512)
    nt = seq // ts
    shift = CONV_PAD - (CONV_WIDTH - 1)

    def body(ca_ref, cb_ref, cah_ref, cbh_ref, w_ref, b_ref, g_ref, beta_ref, y_ref, z_ref, upad):
        _fill_upad(upad, ca_ref, cb_ref, cah_ref, cbh_ref, pl.program_id(1))
        _fill_shifted(upad)
        y = jnp.zeros((ts, D_MODEL), F32) + b_ref[...]
        for k in range(CONV_WIDTH):
            y = y + w_ref[k:k + 1, :] * _shifted_rows(upad, shift + k, ts)
        y_ref[...] = y
        lnh, _ = _layernorm_stats(y)
        ln = lnh * g_ref[...] + beta_ref[...]
        z_ref[...] = (ln * _sigmoid(ln)).astype(BF16)

    vec = pl.BlockSpec((1, D_MODEL), lambda b, t: (0, 0))
    row = pl.BlockSpec((ts, D_MODEL), lambda b, t: (b * nt + t, 0))
    return _call(
        body, name="conv_fwd", grid=(batch, nt),
        in_specs=_conv_specs(ts, nt) + [pl.BlockSpec((CONV_PAD, D_MODEL), lambda b, t: (0, 0)), vec, vec, vec],
        out_specs=[row, row],
        out_shape=[jax.ShapeDtypeStruct((T, D_MODEL), F32), jax.ShapeDtypeStruct((T, D_MODEL), BF16)],
        scratch_shapes=[pltpu.VMEM((SUBLANES, ts + CONV_PAD, D_MODEL), F32)],
        operands=(proj, proj, proj, proj, w_dw, b_dw, ln_g, ln_b), comm=comm)


def _merge(o, z, proj, w_ao, w_co, w_out, x, gate, seq):
    T, D = x.shape
    tm = _tile(seq, 512)
    nb = seq // tm

    def body(o_ref, z_ref, ga_ref, gc_ref, wao_ref, wco_ref, wout_ref, x_ref, gate_ref,
             ya_ref, yc_ref, mg_ref, mo_ref, xo_ref):
        ya = _dot_nn(o_ref[...], wao_ref[...])
        yc = _dot_nn(z_ref[...], wco_ref[...])
        ya_ref[...] = ya.astype(BF16)
        yc_ref[...] = yc.astype(BF16)
        merged = (_sigmoid(ga_ref[...].astype(F32)) * ya + _sigmoid(gc_ref[...].astype(F32)) * yc).astype(BF16)
        mg_ref[...] = merged
        mo = _dot_nn(merged, wout_ref[...])
        mo_ref[...] = mo.astype(BF16)
        xo_ref[...] = x_ref[...] + gate_ref[...] * mo

    row = pl.BlockSpec((tm, D), lambda i: (i, 0))
    mat = pl.BlockSpec((D, D), lambda i: (0, 0))
    act = jax.ShapeDtypeStruct((T, D), BF16)
    return pl.pallas_call(
        body, name="mix_merge", grid=(T // tm,),
        in_specs=[row, row, pl.BlockSpec((tm, D), lambda i: (i, COLB_GA)), pl.BlockSpec((tm, D), lambda i: (i, COLB_GC)),
                  mat, mat, mat, row, pl.BlockSpec((None, 1, D), lambda i: (i // nb, 0, 0))],
        out_specs=[row, row, row, row, row],
        out_shape=[act, act, act, act, jax.ShapeDtypeStruct((T, D), F32)],
        compiler_params=_params(1),
    )(o, z, proj, proj, w_ao, w_co, w_out, x, gate)


def _final_loss(x, gf, target):
    T, D = x.shape
    tm = _tile(T, 512)

    def body(x_ref, gf_ref, t_ref, dx_ref, lp_ref, dgf_ref):
        first = pl.program_id(0) == 0
        xv = x_ref[...]
        gfv = gf_ref[...]
        r = lax.rsqrt(jnp.mean(xv * xv, axis=-1, keepdims=True) + EPS)
        xh = xv * r
        err = xh * gfv - t_ref[...]
        _accumulate(lp_ref, first, jnp.sum(err * err, axis=0, keepdims=True))
        dy = err * (1.0 / D)
        _accumulate(dgf_ref, first, jnp.sum(dy * xh, axis=0, keepdims=True))
        dxh = dy * gfv
        dx_ref[...] = r * (dxh - xh * jnp.mean(dxh * xh, axis=-1, keepdims=True))

    row = pl.BlockSpec((tm, D), lambda i: (i, 0))
    vec = pl.BlockSpec((1, D), lambda i: (0, 0))
    return pl.pallas_call(
        body, name="final_loss", grid=(T // tm,),
        in_specs=[row, vec, row], out_specs=[row, vec, vec],
        out_shape=[jax.ShapeDtypeStruct((T, D), F32), jax.ShapeDtypeStruct((1, D), F32),
                   jax.ShapeDtypeStruct((1, D), F32)],
        compiler_params=_params(1),
    )(x, gf, target)


def _merge_bwd(dxo, mo, gate, proj, ya, yc, w_out, w_ao, w_co, seq, comm=None):
    T, D = dxo.shape
    B = T // seq
    tm = _tile(seq, 512)
    nb = seq // tm

    def body(dxo_ref, mo_ref, gate_ref, ga_ref, gc_ref, ya_ref, yc_ref, wout_ref, wao_ref, wco_ref,
             dmo_ref, dya_ref, dyc_ref, dga_ref, dgc_ref, do_ref, dz_ref, dgate_ref):
        dxo_v = dxo_ref[...]
        dmo = (gate_ref[...] * dxo_v).astype(BF16)
        dmo_ref[...] = dmo
        _accumulate(dgate_ref, pl.program_id(0) % nb == 0,
                    jnp.sum(mo_ref[...].astype(F32) * dxo_v, axis=0, keepdims=True))
        dm = _dot_nt(dmo, wout_ref[...])
        sa = _sigmoid(ga_ref[...].astype(F32))
        sc = _sigmoid(gc_ref[...].astype(F32))
        dya = (sa * dm).astype(BF16)
        dyc = (sc * dm).astype(BF16)
        dya_ref[...] = dya
        dyc_ref[...] = dyc
        dga_ref[...] = (dm * ya_ref[...].astype(F32) * (sa * (1.0 - sa))).astype(BF16)
        dgc_ref[...] = (dm * yc_ref[...].astype(F32) * (sc * (1.0 - sc))).astype(BF16)
        do_ref[...] = _dot_nt(dya, wao_ref[...]).astype(BF16)
        dz_ref[...] = _dot_nt(dyc, wco_ref[...]).astype(BF16)

    row = pl.BlockSpec((tm, D), lambda i: (i, 0))
    mat = pl.BlockSpec((D, D), lambda i: (0, 0))
    per_b = pl.BlockSpec((None, 1, D), lambda i: (i // nb, 0, 0))
    act = jax.ShapeDtypeStruct((T, D), BF16)
    return _call(
        body, name="mix_merge_bwd", grid=(T // tm,),
        in_specs=[row, row, per_b, pl.BlockSpec((tm, D), lambda i: (i, COLB_GA)),
                  pl.BlockSpec((tm, D), lambda i: (i, COLB_GC)), row, row, mat, mat, mat],
        out_specs=[row] * 7 + [per_b],
        out_shape=[act] * 7 + [jax.ShapeDtypeStruct((B, 1, D), F32)],
        operands=(dxo, mo, gate, proj, proj, ya, yc, w_out, w_ao, w_co), comm=comm)


def _attn_bwd(proj, sinks, o, do, lse, batch, seq, comm=None):
    T = proj.shape[0]
    nblk = seq // BLOCK
    n_steps = batch * nblk

    def body(sink_ref, q_ref, kp_ref, ko_ref, vp_ref, vo_ref, o_ref, do_ref, lse_ref,
             dq_ref, dkp_ref, dko_ref, dvp_ref, dvo_ref, dsink_ref):
        lo = lax.broadcasted_iota(jnp.int32, (1, 2 * HEAD_DIM), 1) < HEAD_DIM
        sink_lane = lax.broadcasted_iota(jnp.int32, (1, 2 * HEAD_DIM), 1)
        valid = _band_valid(pl.program_id(1) > 0)
        k_ops = _band_operands(kp_ref, ko_ref, lo)
        v_ops = _band_operands(vp_ref, vo_ref, lo)
        dsink = jnp.zeros((1, 2 * HEAD_DIM), F32)
        col = jnp.zeros((BLOCK, 1), F32)

        def fold(both):
            return (jnp.where(lo, both[:BAND], 0.0)
                    + pltpu.roll(jnp.where(lo, 0.0, both[BAND:]), HEAD_DIM, 1))

        dk_heads, dv_heads = [], []
        for kvh in range(N_KV_HEADS):
            q4 = _stack_pairs(q_ref, kvh)
            do4 = _stack_pairs(do_ref, kvh)
            dd = do4.astype(F32) * _stack_pairs(o_ref, kvh).astype(F32)
            s_all = _dot_nt(q4, k_ops[kvh]) * ATTN_SCALE
            dp_all = _dot_nt(do4, v_ops[kvh])
            ds_sides, p_sides = [], []
            for side in range(2):
                heads = [2 * (kvh * PAIRS_PER_KV + pp) + side for pp in range(PAIRS_PER_KV)]
                mine = lo if side == 0 else jnp.logical_not(lo)
                cols = slice(side * BAND, (side + 1) * BAND)
                sink = _rows_per_pair([col + sink_ref[0, h] for h in heads])
                lse = _rows_per_pair([lse_ref[:, h:h + 1] for h in heads])
                delta = jnp.sum(jnp.where(mine, dd, 0.0), axis=-1, keepdims=True)
                p = jnp.where(valid, jnp.exp(jnp.where(valid, s_all[:, cols], MASK_VALUE) - lse), 0.0)
                ds_sides.append((p * (dp_all[:, cols] - delta) * ATTN_SCALE).astype(BF16))
                p_sides.append(p.astype(BF16))
                sink_part = jnp.exp(sink - lse) * delta
                for pp, h in enumerate(heads):
                    dsink = dsink + jnp.where(sink_lane == h, -jnp.sum(sink_part[pp * BLOCK:(pp + 1) * BLOCK]), 0.0)
            ds_all = jnp.concatenate(ds_sides, axis=1)
            dq4 = _dot_nn(ds_all, k_ops[kvh])
            for pp in range(PAIRS_PER_KV):
                dq_ref[:, _pair_lanes(kvh, pp)] = dq4[pp * BLOCK:(pp + 1) * BLOCK].astype(BF16)
            dk_heads.append(fold(_dot_tn(ds_all, q4)))
            dv_heads.append(fold(_dot_tn(jnp.concatenate(p_sides, axis=1), do4)))
        dk = dk_heads[0] + pltpu.roll(dk_heads[1], HEAD_DIM, 1)
        dv = dv_heads[0] + pltpu.roll(dv_heads[1], HEAD_DIM, 1)
        dkp_ref[...] = dk[:BLOCK]
        dko_ref[...] = dk[BLOCK:]
        dvp_ref[...] = dv[:BLOCK]
        dvo_ref[...] = dv[BLOCK:]
        dsink_ref[...] = dsink

    def own(b, n):
        return (b * nblk + n, 0)

    row = pl.BlockSpec((BLOCK, D_MODEL), own)
    kv = pl.BlockSpec((BLOCK, 2 * HEAD_DIM), own)
    kv_shape = jax.ShapeDtypeStruct((T, 2 * HEAD_DIM), F32)
    return _call(
        body, name="attn_bwd", grid=(batch, nblk),
        in_specs=[SMEM_SPEC] + _attn_specs(nblk) + [row, row, pl.BlockSpec((BLOCK, N_Q_HEADS), own)],
        out_specs=[row, kv, kv, kv, kv, pl.BlockSpec((None, 1, 2 * HEAD_DIM), lambda b, n: (b * nblk + n, 0, 0))],
        out_shape=[jax.ShapeDtypeStruct((T, D_MODEL), BF16), kv_shape, kv_shape, kv_shape, kv_shape,
                   jax.ShapeDtypeStruct((n_steps, 1, 2 * HEAD_DIM), F32)],
        operands=(sinks, proj, proj, proj, proj, proj, o, do, lse), comm=comm)


def _conv_bwd(proj, dz, ydw, w_dw, ln_g, ln_b, batch, seq, comm=None):
    T = proj.shape[0]
    ts = _tile(seq, 256)
    nt = seq // ts
    per_tile = ts // CONV_PAD
    shift = CONV_PAD - (CONV_WIDTH - 1)

    def body(ca_ref, cb_ref, cah_ref, cbh_ref, dz_ref, dzn_ref, y_ref, yn_ref, w_ref, g_ref, beta_ref,
             dca_ref, dcb_ref, dw_ref, db_ref, dg_ref, dbeta_ref, upad, dypad):
        t = pl.program_id(1)
        first = (pl.program_id(0) == 0) & (t == 0)
        gv = g_ref[...]

        def ln_bwd(dzv, yv):
            lnh, rstd = _layernorm_stats(yv)
            ln = lnh * gv + beta_ref[...]
            sg = _sigmoid(ln)
            dln = dzv.astype(F32) * (sg * (1.0 + ln * (1.0 - sg)))
            dyh = dln * gv
            dy = rstd * (dyh - jnp.mean(dyh, axis=-1, keepdims=True)
                         - lnh * jnp.mean(dyh * lnh, axis=-1, keepdims=True))
            return dy, dln, lnh

        dy, dln, lnh = ln_bwd(dz_ref[...], y_ref[...])
        dy_next, _, _ = ln_bwd(dzn_ref[...], yn_ref[...])
        dypad[0, 0:ts, :] = dy
        dypad[0, ts:, :] = jnp.where(t < nt - 1, dy_next, jnp.zeros_like(dy_next))
        _fill_shifted(dypad)
        _fill_upad(upad, ca_ref, cb_ref, cah_ref, cbh_ref, t)
        _fill_shifted(upad)

        _accumulate(dg_ref, first, jnp.sum(dln * lnh, axis=0, keepdims=True))
        _accumulate(dbeta_ref, first, jnp.sum(dln, axis=0, keepdims=True))
        _accumulate(db_ref, first, jnp.sum(dy, axis=0, keepdims=True))

        @pl.when(first)
        def _():
            dw_ref[...] = jnp.zeros_like(dw_ref)

        du = jnp.zeros((ts, D_MODEL), F32)
        for k in range(CONV_WIDTH):
            du = du + w_ref[k:k + 1, :] * _shifted_rows(dypad, CONV_WIDTH - 1 - k, ts)
            dw_ref[k:k + 1, :] += jnp.sum(dy * _shifted_rows(upad, shift + k, ts), axis=0, keepdims=True)
        cav = ca_ref[...].astype(F32)
        sb = _sigmoid(cb_ref[...].astype(F32))
        dca_ref[...] = (du * sb).astype(BF16)
        dcb_ref[...] = (du * cav * (sb * (1.0 - sb))).astype(BF16)

    def tile(b, t):
        return (b * nt + t, 0)

    def after(b, t):
        return (jnp.minimum((b * nt + t + 1) * per_tile, T // CONV_PAD - 1), 0)

    row = pl.BlockSpec((ts, D_MODEL), tile)
    halo = pl.BlockSpec((CONV_PAD, D_MODEL), after)
    vec = pl.BlockSpec((1, D_MODEL), lambda b, t: (0, 0))
    wspec = pl.BlockSpec((CONV_PAD, D_MODEL), lambda b, t: (0, 0))
    act = jax.ShapeDtypeStruct((T, D_MODEL), BF16)
    vec_shape = jax.ShapeDtypeStruct((1, D_MODEL), F32)
    return _call(
        body, name="conv_bwd", grid=(batch, nt),
        in_specs=_conv_specs(ts, nt) + [row, halo, row, halo, wspec, vec, vec],
        out_specs=[row, row, wspec, vec, vec, vec],
        out_shape=[act, act, jax.ShapeDtypeStruct((CONV_PAD, D_MODEL), F32), vec_shape, vec_shape, vec_shape],
        scratch_shapes=[pltpu.VMEM((SUBLANES, ts + CONV_PAD, D_MODEL), F32)] * 2,
        operands=(proj, proj, proj, proj, dz, dz, ydw, ydw, w_dw, ln_g, ln_b), comm=comm)


def _in_proj_bwd(pieces, w_cols, x, gn, sc, dxo, seq, comm=None):
    T, D = x.shape
    B = T // seq
    wide, narrow = list(pieces[:-1]), pieces[-1]
    P = len(wide)
    nw = narrow.shape[1]
    tm = _tile(seq, 512)
    nb = seq // tm

    def body(*refs):
        wide_refs = refs[:P]
        kv_ref, w_ref, wkv_ref, x_ref, gn_ref, sc_ref, dxo_ref, dx_ref, dsc_ref, dsh_ref, dgn_ref, acc = refs[P:]
        i = pl.program_id(0)
        j = pl.program_id(1)

        @pl.when(j == 0)
        def _():
            acc[...] = _dot_nn(kv_ref[...], wkv_ref[...])

        for p in range(P):
            @pl.when(j == p)
            def _(p=p):
                acc[...] += _dot_nn(wide_refs[p][...], w_ref[...])

        @pl.when(j == P - 1)
        def _():
            _norm_mod_bwd(acc[...], x_ref[...], gn_ref[...], sc_ref[...], dxo_ref[...],
                          i % nb == 0, i == 0, dx_ref, dsc_ref, dsh_ref, dgn_ref)

    row = pl.BlockSpec((tm, D), lambda i, j: (i, 0))
    vec = pl.BlockSpec((1, D), lambda i, j: (0, 0))
    per_b = pl.BlockSpec((None, 1, D), lambda i, j: (i // nb, 0, 0))
    per_b_shape = jax.ShapeDtypeStruct((B, 1, D), F32)
    return _call(
        body, name="mix_in_proj_bwd", grid=(T // tm, P),
        in_specs=[row] * P + [pl.BlockSpec((tm, nw), lambda i, j: (i, 0)),
                              pl.BlockSpec((D, D), lambda i, j: (j, 0)),
                              pl.BlockSpec((nw, D), lambda i, j: (P * D // nw, 0)), row, vec, per_b, row],
        out_specs=[row, per_b, per_b, vec],
        out_shape=[jax.ShapeDtypeStruct((T, D), F32), per_b_shape, per_b_shape, jax.ShapeDtypeStruct((1, D), F32)],
        scratch_shapes=[pltpu.VMEM((tm, D), F32)],
        operands=(*wide, narrow, w_cols, w_cols, x, gn, sc, dxo), comm=comm)


def _wgrad_rows(piece, h, out32, out16, row_offset, name):
    T, n = piece.shape
    C = h.shape[1]
    tk = _tile(T, 1024)
    nk = T // tk

    def body(a_ref, b_ref, in32, in16, o32_ref, o16_ref, acc, stage16, sems):
        k = pl.program_id(0)

        @pl.when(k == 0)
        def _():
            acc[...] = jnp.zeros_like(acc)

        acc[...] += _dot_tn(a_ref[...], b_ref[...])

        @pl.when(k == nk - 1)
        def _():
            stage16[...] = acc[...].astype(BF16)
            rows = pl.ds(row_offset, n)
            copies = [pltpu.make_async_copy(acc, o32_ref.at[rows, :], sems.at[0]),
                      pltpu.make_async_copy(stage16, o16_ref.at[rows, :], sems.at[1])]
            for cp in copies:
                cp.start()
            for cp in copies:
                cp.wait()

    return pl.pallas_call(
        body, name=name, grid=(nk,),
        in_specs=[pl.BlockSpec((tk, n), lambda k: (k, 0)), pl.BlockSpec((tk, C), lambda k: (k, 0)), ANY, ANY],
        out_specs=[ANY, ANY], out_shape=[jax.ShapeDtypeStruct(out32.shape, F32), jax.ShapeDtypeStruct(out16.shape, BF16)],
        scratch_shapes=[pltpu.VMEM((n, C), F32), pltpu.VMEM((n, C), BF16), pltpu.SemaphoreType.DMA((2,))],
        input_output_aliases={2: 0, 3: 1}, compiler_params=_params(1),
    )(piece, h, out32, out16)


def _ada_fwd(c_all, w_ada, b_cols):
    nbatch, D = c_all.shape
    N = w_ada.shape[1]
    tn = _tile(N, 768)

    def body(c_ref, w_ref, b_ref, o_ref):
        cv = c_ref[...]
        act = (cv * _sigmoid(cv)).astype(BF16)
        o_ref[...] = _dot_nn(act, w_ref[...].astype(BF16)) + b_ref[...]

    return pl.pallas_call(
        body, name="ada_fwd", grid=(N // tn,),
        in_specs=[pl.BlockSpec((nbatch, D), lambda j: (0, 0)), pl.BlockSpec((D, tn), lambda j: (0, j)),
                  pl.BlockSpec((1, tn), lambda j: (0, j))],
        out_specs=pl.BlockSpec((nbatch, tn), lambda j: (0, j)),
        out_shape=jax.ShapeDtypeStruct((nbatch, N), F32),
        compiler_params=_params(1),
    )(c_all, w_ada, b_cols)


def _adamw(w, g, m, v):
    m = ADAM_B1 * m + (1.0 - ADAM_B1) * g
    v = ADAM_B2 * v + (1.0 - ADAM_B2) * (g * g)
    m_hat = m / (1.0 - ADAM_B1 ** ADAM_STEP)
    v_hat = v / (1.0 - ADAM_B2 ** ADAM_STEP)
    delta = -ADAM_LR * (m_hat / (jnp.sqrt(v_hat) + ADAM_EPS) + ADAM_WD * w)
    return delta, m, v


def _adam_call(w, g, m, v, name, comm=None):
    R, C = w.shape
    tr = _row_tile(R, 512)

    def body(w_ref, g_ref, m_ref, v_ref, d_ref, mo_ref, vo_ref):
        d, mn, vn = _adamw(w_ref[...], g_ref[...], m_ref[...], v_ref[...])
        d_ref[...] = d
        mo_ref[...] = mn
        vo_ref[...] = vn

    blk = pl.BlockSpec((tr, C), lambda i: (i, 0))
    shape = jax.ShapeDtypeStruct((R, C), F32)
    return _call(body, name=name, grid=(R // tr,), in_specs=[blk] * 4, out_specs=[blk] * 3, out_shape=[shape] * 3,
                 operands=(w, g, m, v), comm=comm)


ADAM_GROUP_STEPS = 8


def _adam_group(ws, gs, ms, vs, name, comm=None):
    n = len(ws)

    def body(*refs):
        ins, outs = refs[:4 * n], refs[4 * n:]
        for i in range(n):
            d, mn, vn = _adamw(*(r[...] for r in ins[4 * i:4 * i + 4]))
            outs[3 * i][...] = d
            outs[3 * i + 1][...] = mn
            outs[3 * i + 2][...] = vn

    operands, in_specs, out_specs, out_shape = [], [], [], []
    for w, g, m, v in zip(ws, gs, ms, vs):
        R, C = w.shape
        blk = pl.BlockSpec((R // ADAM_GROUP_STEPS, C), lambda i: (i, 0))
        operands += [w, g, m, v]
        in_specs += [blk] * 4
        out_specs += [blk] * 3
        out_shape += [jax.ShapeDtypeStruct((R, C), F32)] * 3
    outs, comm_outs = _call(body, name=name, grid=(ADAM_GROUP_STEPS,), in_specs=in_specs, out_specs=out_specs,
                            out_shape=out_shape, operands=operands, comm=comm)
    return [tuple(outs[3 * i:3 * i + 3]) for i in range(n)], comm_outs


def _ada_adam(c_act_t, dmod_cols, w, m, v, comm):
    R, C = w.shape
    nbatch = c_act_t.shape[1]
    tr = _tile(R, 128)

    def body(ct_ref, dm_ref, w_ref, m_ref, v_ref, g_ref, d_ref, mo_ref, vo_ref):
        cv = ct_ref[...]
        g = _dot_nn((cv * _sigmoid(cv)).astype(BF16), dm_ref[...].astype(BF16))
        g_ref[...] = g
        d, mn, vn = _adamw(w_ref[...], g, m_ref[...], v_ref[...])
        d_ref[...] = d
        mo_ref[...] = mn
        vo_ref[...] = vn

    blk = pl.BlockSpec((tr, C), lambda i: (i, 0))
    shape = jax.ShapeDtypeStruct((R, C), F32)
    return _call(
        body, name="ada_adam", grid=(R // tr,),
        in_specs=[pl.BlockSpec((tr, nbatch), lambda i: (i, 0)), pl.BlockSpec((nbatch, C), lambda i: (0, 0)),
                  blk, blk, blk],
        out_specs=[blk] * 4, out_shape=[shape] * 4,
        operands=(c_act_t, dmod_cols, w, m, v), comm=comm)


def _small_adam(gathered, w, m, v, rows_b0, rows_b1, rows_vec):
    _, P, D = gathered.shape
    R = w.shape[0]

    def body(ga_ref, w_ref, m_ref, v_ref, sum_ref, g_ref, d_ref, mo_ref, vo_ref):
        total = ga_ref[0]
        for dev in range(1, N_DEV):
            total = total + ga_ref[dev]
        sum_ref[...] = total
        g_ref[...] = jnp.zeros_like(g_ref)
        g_ref[0:N_MOD, :] = (sum_ref[rows_b0:rows_b0 + N_MOD, :] + sum_ref[rows_b1:rows_b1 + N_MOD, :])
        g_ref[N_MOD:N_MOD + 8, :] = sum_ref[rows_vec:rows_vec + 8, :]
        d, mn, vn = _adamw(w_ref[...], g_ref[...], m_ref[...], v_ref[...])
        d_ref[...] = d
        mo_ref[...] = mn
        vo_ref[...] = vn

    shape = jax.ShapeDtypeStruct((R, D), F32)
    return pl.pallas_call(
        body, name="small_adam",
        in_specs=[VMEM_SPEC] * 4, out_specs=[VMEM_SPEC] * 5,
        out_shape=[jax.ShapeDtypeStruct((P, D), F32), shape, shape, shape, shape],
        compiler_params=pltpu.CompilerParams(vmem_limit_bytes=VMEM_LIMIT),
    )(gathered, w, m, v)


def _mod_exchange(part):
    _, A, W = part.shape

    def body(p_ref, out_ref, send_sems, recv_sems, local_sem):
        x, y, c = _position()
        me = 4 * x + 2 * y + c
        chip = 2 * x + y
        mine = pltpu.make_async_copy(p_ref.at[me], out_ref.at[chip], local_sem)
        mine.start()
        peers = [(_flip(x, fx), _flip(y, fy)) for fx, fy in CHIP_FLIPS]
        sends = []
        for k, (px, py) in enumerate(peers):
            sends.append(pltpu.make_async_remote_copy(
                src_ref=p_ref.at[4 * px + 2 * py + c], dst_ref=out_ref.at[chip], send_sem=send_sems.at[k],
                recv_sem=recv_sems.at[k], device_id=(px, py, c), device_id_type=MESH))
        for cp in sends:
            cp.start()
        for k, (px, py) in enumerate(peers):
            pltpu.make_async_remote_copy(
                src_ref=p_ref.at[me], dst_ref=out_ref.at[2 * px + py], send_sem=send_sems.at[k],
                recv_sem=recv_sems.at[k], device_id=(px, py, c), device_id_type=MESH).wait_recv()
        for cp in sends:
            cp.wait_send()
        mine.wait()

    return pl.pallas_call(
        body, name="mod_exchange", in_specs=[VMEM_SPEC], out_specs=VMEM_SPEC,
        out_shape=jax.ShapeDtypeStruct((N_CHIP, A, W), part.dtype),
        scratch_shapes=[pltpu.SemaphoreType.DMA((3,)), pltpu.SemaphoreType.DMA((3,)), pltpu.SemaphoreType.DMA],
    )(part)


KV_ROWS = 4 * HEAD_DIM


def _kernel_row_order(w_in_t):
    R, C = w_in_t.shape
    n_blocks = R // KV_ROWS
    q_blocks = D_MODEL // KV_ROWS

    def source(t):
        return jnp.where(t < q_blocks, t, jnp.where(t < n_blocks - 1, t + 1, q_blocks))

    def body(w_ref, o_ref):
        o_ref[...] = w_ref[...]

    return pl.pallas_call(
        body, name="w_in_row_order", grid=(n_blocks,),
        in_specs=[pl.BlockSpec((KV_ROWS, C), lambda t: (source(t), 0))],
        out_specs=pl.BlockSpec((KV_ROWS, C), lambda t: (t, 0)),
        out_shape=jax.ShapeDtypeStruct((R, C), w_in_t.dtype), compiler_params=_params(1),
    )(w_in_t)


def _cast_group(ws, names, comm):
    n = len(ws)
    steps = 4

    def body(*refs):
        w_refs, out_refs, stage, sem = refs[:n], refs[n:2 * n], refs[2 * n:3 * n], refs[3 * n]
        x, y, _ = _position()
        step = pl.program_id(0)
        copies = []
        for i in range(n):
            rows = ws[i].shape[0] // steps
            stage[i][...] = w_refs[i][...].astype(BF16)
            copies.append(pltpu.make_async_copy(
                stage[i], out_refs[i].at[2 * x + y, pl.ds(step * rows, rows), :], sem.at[i]))
        for cp in copies:
            cp.start()
        for cp in copies:
            cp.wait()

    outs, comm_outs = _call(
        body, name="cast_" + "_".join(names), grid=(steps,),
        in_specs=[pl.BlockSpec((w.shape[0] // steps, w.shape[1]), lambda i: (i, 0)) for w in ws],
        out_specs=[ANY] * n, out_shape=[jax.ShapeDtypeStruct((N_CHIP,) + w.shape, BF16) for w in ws],
        scratch_shapes=[pltpu.VMEM((w.shape[0] // steps, w.shape[1]), BF16) for w in ws]
        + [pltpu.SemaphoreType.DMA((n,))],
        operands=ws, comm=comm)
    return outs, comm_outs


def _cast_slot(w, chip_idx, name):
    R, C = w.shape
    tr = _row_tile(R, 512)

    def body(chip_ref, w_ref, o_ref):
        o_ref[...] = w_ref[...].astype(BF16)

    return pl.pallas_call(
        body, name=name,
        grid_spec=pltpu.PrefetchScalarGridSpec(
            num_scalar_prefetch=1, grid=(R // tr,),
            in_specs=[pl.BlockSpec((tr, C), lambda i, chip_ref: (i, 0))],
            out_specs=pl.BlockSpec((None, tr, C), lambda i, chip_ref: (chip_ref[0], i, 0))),
        out_shape=jax.ShapeDtypeStruct((N_CHIP, R, C), BF16),
        compiler_params=_params(1),
    )(chip_idx, w)


def _pair_sum(g32, recv, core, name):
    J, _, r, C = g32.shape

    def body(core_ref, g_ref, r_ref, o_ref):
        o_ref[...] = (g_ref[...] + r_ref[...].astype(F32)).astype(BF16)

    return pl.pallas_call(
        body, name=name,
        grid_spec=pltpu.PrefetchScalarGridSpec(
            num_scalar_prefetch=1, grid=(J,),
            in_specs=[pl.BlockSpec((None, None, r, C), lambda j, core_ref: (j, core_ref[0], 0, 0)),
                      pl.BlockSpec((None, r, C), lambda j, core_ref: (j, 0, 0))],
            out_specs=pl.BlockSpec((None, r, C), lambda j, core_ref: (j, 0, 0))),
        out_shape=jax.ShapeDtypeStruct((J, r, C), BF16),
        compiler_params=_params(1),
    )(core, g32, recv)


def _chip_sum(g32, recv_sib, recv_chips, core_chip, name):
    J, _, r, C = g32.shape

    def body(idx_ref, g_ref, s_ref, o_ref_in, o_ref):
        total = g_ref[...] + s_ref[...].astype(F32)
        for k in range(3):
            total = total + o_ref_in[k].astype(F32)
        o_ref[...] = total

    return pl.pallas_call(
        body, name=name,
        grid_spec=pltpu.PrefetchScalarGridSpec(
            num_scalar_prefetch=1, grid=(1,),
            in_specs=[pl.BlockSpec((None, None, r, C), lambda i, idx: (idx[1], idx[0], 0, 0)),
                      pl.BlockSpec((None, r, C), lambda i, idx: (idx[1], 0, 0)),
                      pl.BlockSpec((3, r, C), lambda i, idx: (0, 0, 0))],
            out_specs=pl.BlockSpec((None, r, C), lambda i, idx: (idx[0], 0, 0))),
        out_shape=jax.ShapeDtypeStruct((2, r, C), F32),
        compiler_params=_params(1),
    )(core_chip, g32, recv_sib, recv_chips)


ICI_US_PER_ELEMENT = 4.6e-5


class _Reducer:
    def __init__(self, core_idx, core_chip):
        self.core_idx, self.core_chip = core_idx, core_chip
        self.grads, self.halves, self.reduced = {}, {}, {}
        self.ready_swap, self.ready_exchange, self.ready_join = [], [], []
        self.inflight, self.current = ([], [], [], None), None
        self.flushes = 0
        self.extra, self.extra_out = None, None

    def add(self, name, grad_pair):
        self.grads[name] = grad_pair
        self.ready_swap.append(name)

    def comm(self, budget_us):
        swaps, self.ready_swap = self.ready_swap, []
        joins, self.ready_join = self.ready_join, []
        exchanges, waiting = [], []
        for item in self.ready_exchange:
            cost = ICI_US_PER_ELEMENT * 2 * item[2].shape[1] * item[2].shape[2]
            if cost <= budget_us:
                exchanges.append(item)
                budget_us -= cost
            else:
                waiting.append(item)
        self.ready_exchange = waiting
        parts = []
        if swaps:
            parts.append(_SwapComm([self.grads[n][1] for n in swaps]))
        if exchanges:
            parts.append(_ExchangeComm([pair for _, _, pair in exchanges]))
        if joins:
            parts.append(_JoinComm([self.halves[n] for n in joins]))
        extra, self.extra = self.extra, None
        if extra is not None:
            parts.append(extra)
        self.inflight = (swaps, exchanges, joins, extra)
        self.current = _CommList(parts) if parts else None
        return self.current

    def done(self, comm_outs):
        if self.current is None:
            return
        swaps, exchanges, joins, extra = self.inflight
        outs = iter(self.current.split_outputs(list(comm_outs)))
        if swaps:
            for n, recv in zip(swaps, next(outs)):
                pair = _pair_sum(self.grads[n][0], recv, self.core_idx, "pair_sum_" + n)
                self.ready_exchange.append((n, recv, pair))
        if exchanges:
            for (n, recv, _), chips in zip(exchanges, next(outs)):
                self.halves[n] = _chip_sum(self.grads[n][0], recv, chips, self.core_chip, "chip_sum_" + n)
                self.ready_join.append(n)
        if joins:
            self.reduced.update(zip(joins, next(outs)))
        if extra is not None:
            self.extra_out = next(outs)
        self.current = None

    def run(self, kernel, budget_us, *args, **kwargs):
        if budget_us is None:
            return kernel(*args, comm=None, **kwargs)[0]
        outs, comm_outs = kernel(*args, comm=self.comm(budget_us), **kwargs)
        self.done(comm_outs)
        return outs

    def step(self):
        comm = self.comm(float("inf"))
        self.flushes += 1
        self.done(_run_comm(comm, "grad_reduce_tail_%d" % self.flushes))


BIG_WEIGHTS = ("ffn1_w_gate", "ffn1_w_up", "ffn1_w_down", "w_in", "w_attn_o", "w_conv_o", "w_out",
               "ffn2_w_gate", "ffn2_w_up", "ffn2_w_down")
VECTORS = ("norm_ffn1_g", "norm_mix_g", "conv_b_dw", "conv_ln_g", "conv_ln_b", "norm_ffn2_g", "final_norm_g")
ROW_DMOD0, ROW_DMOD1, ROW_LOSS, ROW_VEC, ROW_SINK, ROW_CONVW, SMALL_ROWS = 0, 16, 32, 33, 40, 41, 72


FFN1_WEIGHTS = ("ffn1_w_gate", "ffn1_w_up", "ffn1_w_down")
FFN2_WEIGHTS = ("ffn2_w_gate", "ffn2_w_up", "ffn2_w_down")
MIX_WEIGHTS = ("w_in", "w_attn_o", "w_conv_o", "w_out")
COL_SHARDED = ("ffn1_w_gate", "ffn1_w_up", "ffn2_w_gate", "ffn2_w_up", "w_in")


def _local_grads(x, target, mod, slots, ffn1_gathered, small, seq, core_idx, core_chip):
    T, D = x.shape
    B = T // seq
    mods = [mod[:, k][:, None, :] for k in range(N_MOD)]
    sh1, sc1, g1, sh2, sc2, g2, sh3, sc3, g3 = mods
    w = dict(zip(FFN1_WEIGHTS, ffn1_gathered))

    (h1, a1, u1, f1, x1), outs = _ffn_fwd(
        x, small["norm_ffn1_g"], sc1, sh1, g1, w["ffn1_w_gate"], w["ffn1_w_up"], w["ffn1_w_down"], seq, "ffn1_fwd",
        comm=_GatherComm([slots[n] for n in MIX_WEIGHTS]))
    w["w_in"] = outs[0]
    w_ao, w_co, w_o = [t.reshape(D, D) for t in outs[1:]]
    w_in_cols = _kernel_row_order(w["w_in"].reshape(IN_WIDTH, D))
    (h2, proj), _ = _in_proj(x1, small["norm_mix_g"], sc2, sh2, w_in_cols, seq)
    (o, lse), (w["ffn2_w_gate"], w["ffn2_w_up"]) = _attn_fwd(
        proj, small["attn_sinks"], B, seq, comm=_GatherComm([slots["ffn2_w_gate"], slots["ffn2_w_up"]]))
    (ydw, z), (w["ffn2_w_down"],) = _conv_fwd(
        proj, small["conv_w_dw"], small["conv_b_dw"], small["conv_ln_g"], small["conv_ln_b"], B, seq,
        comm=_GatherComm([slots["ffn2_w_down"]]))
    ya, yc, merged, mo, x2 = _merge(o, z, proj, w_ao, w_co, w_o, x1, g2, seq)
    (h3, a3, u3, f3, x3), _ = _ffn_fwd(x2, small["norm_ffn2_g"], sc3, sh3, g3, w["ffn2_w_gate"], w["ffn2_w_up"],
                                       w["ffn2_w_down"], seq, "ffn2_fwd")
    dx3, loss_parts, d_final_g = _final_loss(x3, small["final_norm_g"], target)

    red = _Reducer(core_idx, core_chip)

    def weight_grad(name, budget_us, a, a_spec, b, b_spec, rows, cols):
        red.add(name, red.run(_wgrad, budget_us, a, a_spec, b, b_spec, rows, cols, T, "dw_" + name))

    def ffn_backward(prefix, dw_budget_us, dxo, xin, h, a, u, f, gn, sc, gate, before_weight_grads=None):
        da, du, s, df, dx, dgate, dsc, dsh, dgn = red.run(
            _ffn_bwd, 170, dxo, xin, f, a, u, gn, sc, gate, w[prefix + "_w_gate"], w[prefix + "_w_up"],
            w[prefix + "_w_down"], seq, prefix + "_bwd")
        if before_weight_grads is not None:
            before_weight_grads(dgate, dsc, dsh, dgn)
        weight_grad(prefix + "_w_down", dw_budget_us, s, _spec_chip_major(FF_SHARD), df, _spec_rows(D), FF_SHARD, D)
        weight_grad(prefix + "_w_gate", dw_budget_us, da, _spec_chip_major(FF_SHARD), h, _spec_rows(D), FF_SHARD, D)
        weight_grad(prefix + "_w_up", dw_budget_us, du, _spec_chip_major(FF_SHARD), h, _spec_rows(D), FF_SHARD, D)
        return dx, dgate, dsc, dsh, dgn

    dx2, dg3, dsc3, dsh3, d_gn3 = ffn_backward("ffn2", None, dx3, x2, h3, a3, u3, f3, small["norm_ffn2_g"], sc3, g3)

    dmo, dya, dyc, dga, dgc, do, dz, dg2 = red.run(_merge_bwd, 45, dx2, mo, g2, proj, ya, yc, w_o, w_ao, w_co, seq)
    shard = D // N_CHIP
    weight_grad("w_out", None, merged, _spec_col_block(shard), dmo, _spec_rows(D), shard, D)
    weight_grad("w_attn_o", None, o, _spec_col_block(shard), dya, _spec_rows(D), shard, D)
    weight_grad("w_conv_o", None, z, _spec_col_block(shard), dyc, _spec_rows(D), shard, D)
    dq, dkp, dko, dvp, dvo, dsink_steps = red.run(_attn_bwd, 100, proj, small["attn_sinks"], o, do, lse, B, seq)
    dca, dcb, d_conv_w, d_conv_b, d_ln_g, d_ln_b = red.run(
        _conv_bwd, 165, proj, dz, ydw, small["conv_w_dw"], small["conv_ln_g"], small["conv_ln_b"], B, seq)

    def band_sum(own, prev):
        prev = prev.reshape(B, seq // BLOCK, BLOCK, 2 * HEAD_DIM)
        moved = jnp.concatenate([prev[:, 1:], jnp.zeros_like(prev[:, :1])], axis=1)
        return (own + moved.reshape(T, 2 * HEAD_DIM)).astype(BF16)

    dkv = jnp.concatenate([band_sum(dko, dkp), band_sum(dvo, dvp)], axis=1)
    g32, g16 = lax.empty((IN_WIDTH, D), F32), lax.empty((IN_WIDTH, D), BF16)
    row_of = {"q": 0, "kv": D, "conv_a": D + 4 * HEAD_DIM, "conv_b": 2 * D + 4 * HEAD_DIM,
              "gate_a": 3 * D + 4 * HEAD_DIM, "gate_c": 4 * D + 4 * HEAD_DIM}
    for tag, piece in (("q", dq), ("kv", dkv), ("conv_a", dca), ("conv_b", dcb), ("gate_a", dga), ("gate_c", dgc)):
        g32, g16 = _wgrad_rows(piece, h2, g32, g16, row_of[tag], "dw_w_in_" + tag)
    red.add("w_in", tuple(g.reshape(N_CHIP, 2, IN_SHARD // 2, D) for g in (g32, g16)))
    dx1, dsc2, dsh2, d_gn2 = red.run(_in_proj_bwd, 90, (dq, dca, dcb, dga, dgc, dkv), w_in_cols, x1,
                                     small["norm_mix_g"], sc2, dx2, seq)

    def gather_small_grads(dg1, dsc1, dsh1, d_gn1):
        dmod = jnp.concatenate([dsh1, dsc1, dg1, dsh2, dsc2, dg2, dsh3, dsc3, dg3], axis=1)
        d_sinks = jnp.sum(dsink_steps, axis=0)
        vec_grads = {"norm_ffn1_g": d_gn1, "norm_mix_g": d_gn2, "conv_b_dw": d_conv_b, "conv_ln_g": d_ln_g,
                     "conv_ln_b": d_ln_b, "norm_ffn2_g": d_gn3, "final_norm_g": d_final_g}
        block = jnp.zeros((SMALL_ROWS, D), F32)
        block = block.at[ROW_DMOD0:ROW_DMOD0 + N_MOD].set(dmod[0]).at[ROW_DMOD1:ROW_DMOD1 + N_MOD].set(dmod[1])
        block = block.at[ROW_VEC:ROW_VEC + len(VECTORS)].set(jnp.concatenate([vec_grads[n] for n in VECTORS], axis=0))
        block = block.at[ROW_LOSS].set(loss_parts[0])
        block = block.at[ROW_SINK, :2 * HEAD_DIM].set(d_sinks[0])
        block = block.at[ROW_CONVW:ROW_CONVW + CONV_WIDTH].set(d_conv_w[:CONV_WIDTH])
        red.extra = _Gather8Comm(block)

    dx0, _, _, _, _ = ffn_backward("ffn1", 38, dx1, x, h1, a1, u1, f1, small["norm_ffn1_g"], sc1, g1,
                                   before_weight_grads=gather_small_grads)
    return dx0, red, red.extra_out[0]


def kernel(x, c, w_ada, b_ada, norm_ffn1_g, ffn1_w_gate, ffn1_w_up, ffn1_w_down, norm_mix_g, w_in, attn_sinks, w_attn_o, conv_w_dw, conv_b_dw, conv_ln_g, conv_ln_b, w_conv_o, w_out, norm_ffn2_g, ffn2_w_gate, ffn2_w_up, ffn2_w_down, final_norm_g, loss_target, m_w_ada, m_b_ada, m_norm_ffn1_g, m_ffn1_w_gate, m_ffn1_w_up, m_ffn1_w_down, m_norm_mix_g, m_w_in, m_attn_sinks, m_w_attn_o, m_conv_w_dw, m_conv_b_dw, m_conv_ln_g, m_conv_ln_b, m_w_conv_o, m_w_out, m_norm_ffn2_g, m_ffn2_w_gate, m_ffn2_w_up, m_ffn2_w_down, m_final_norm_g, v_w_ada, v_b_ada, v_norm_ffn1_g, v_ffn1_w_gate, v_ffn1_w_up, v_ffn1_w_down, v_norm_mix_g, v_w_in, v_attn_sinks, v_w_attn_o, v_conv_w_dw, v_conv_b_dw, v_conv_ln_g, v_conv_ln_b, v_w_conv_o, v_w_out, v_norm_ffn2_g, v_ffn2_w_gate, v_ffn2_w_up, v_ffn2_w_down, v_final_norm_g):
    args = dict(locals())
    B, seq, D = x.shape
    T = B * seq
    xi, yi, ci = _position()
    chip = 2 * xi + yi
    dev = 4 * xi + 2 * yi + ci

    def shard_2d(prefix, name):
        t = args[prefix + name][0]
        return t.T if name in COL_SHARDED else t

    big = {n: shard_2d("", n) for n in BIG_WEIGHTS}
    final_g = final_norm_g[None, :]
    vec_w = {n: (args[n] if n != "final_norm_g" else final_g) for n in VECTORS}

    core_idx = jnp.reshape(ci, (1,)).astype(jnp.int32)
    chip_idx = jnp.reshape(chip, (1,)).astype(jnp.int32)
    core_chip = jnp.stack([ci, chip]).astype(jnp.int32)
    conv_cols = D // N_CHIP
    conv_flat = jnp.pad(conv_w_dw[0].reshape(-1), (0, 8 * D - CONV_WIDTH * conv_cols)).reshape(8, D)
    first_block = jnp.concatenate([jnp.pad(c, ((0, 8 - B), (0, 0))), conv_flat], axis=0)
    slots = {n: _cast_slot(big[n], chip_idx, "cast_" + n) for n in FFN1_WEIGHTS}
    later = [n for n in BIG_WEIGHTS if n not in FFN1_WEIGHTS]
    carried = _CommList([_GatherComm([slots[n] for n in FFN1_WEIGHTS]), _Gather8Comm(first_block)])
    later_slots, carried_outs = _cast_group([big[n] for n in later], ["later_weights"], carried)
    ffn1_gathered, (first,) = carried.split_outputs(carried_outs)
    slots.update(zip(later, later_slots))
    c_all = first[:, :B].reshape(N_DEV * B, D)
    conv_taps = first[::2, 8:].reshape(N_CHIP, 8 * D)[:, :CONV_WIDTH * conv_cols]
    conv_taps = conv_taps.reshape(N_CHIP, CONV_WIDTH, conv_cols).transpose(1, 0, 2).reshape(CONV_WIDTH, D)
    conv_taps = jnp.pad(conv_taps, ((0, CONV_PAD - CONV_WIDTH), (0, 0)))

    ada_cols = w_ada.shape[2]
    b_cols = lax.dynamic_slice(b_ada, (0, chip * ada_cols), (1, ada_cols))
    mod_part = _ada_fwd(c_all, w_ada[0], b_cols).reshape(N_DEV, B, ada_cols)
    mod = _mod_exchange(mod_part).transpose(1, 0, 2).reshape(B, N_MOD, D)

    small = dict(vec_w)
    small["attn_sinks"] = attn_sinks
    small["conv_w_dw"] = conv_taps

    dx, red, small_all = _local_grads(
        x.reshape(T, D), loss_target.reshape(T, D), mod, slots, ffn1_gathered, small, seq, core_idx, core_chip)
    grad_x = dx.reshape(B, seq, D)
    out = {}


    def pack_small(prefix):
        rows = [args[prefix + "b_ada"].reshape(N_MOD, D)]
        rows += [args[prefix + n].reshape(1, D) for n in VECTORS]
        rows += [jnp.pad(args[prefix + "attn_sinks"], ((0, 0), (0, D - N_Q_HEADS)))]
        return jnp.pad(jnp.concatenate(rows, axis=0), ((0, 24 - N_MOD - len(VECTORS) - 1), (0, 0)))

    small_sum, sg, sd, sm, sv = _small_adam(small_all, pack_small(""), pack_small("m_"), pack_small("v_"),
                                           ROW_DMOD0, ROW_DMOD1, ROW_VEC)
    loss = (0.5 / D) * jnp.sum(small_sum[ROW_LOSS])

    def unpack_small(t):
        res = {"b_ada": t[:N_MOD].reshape(1, N_MOD * D)}
        for k, n in enumerate(VECTORS):
            res[n] = t[N_MOD + k].reshape(args[n].shape)
        res["attn_sinks"] = t[N_MOD + len(VECTORS), :N_Q_HEADS].reshape(1, N_Q_HEADS)
        return res

    unpacked = [unpack_small(t) for t in (sg, sd, sm, sv)]
    for n in ("b_ada", "attn_sinks") + VECTORS:
        out[n] = tuple(u[n] for u in unpacked)

    conv_g = lax.dynamic_slice(small_sum, (ROW_CONVW, chip * conv_cols), (CONV_WIDTH, conv_cols))
    d, mn, vn = red.run(_adam_call, None, conv_w_dw[0], conv_g, m_conv_w_dw[0], v_conv_w_dw[0], "adam_conv_w_dw")
    out["conv_w_dw"] = tuple(t[None] for t in (conv_g, d, mn, vn))

    dmod_rows = jnp.stack([small_all[:, ROW_DMOD0:ROW_DMOD0 + N_MOD], small_all[:, ROW_DMOD1:ROW_DMOD1 + N_MOD]], axis=1)
    dmod_all = dmod_rows.reshape(N_DEV * B, N_MOD * D)
    dmod_cols = lax.dynamic_slice(dmod_all, (0, chip * ada_cols), (N_DEV * B, ada_cols))
    ada_out = red.run(_ada_adam, 35, c_all.T, dmod_cols, w_ada[0], m_w_ada[0], v_w_ada[0])
    out["w_ada"] = tuple(t[None] for t in ada_out)

    def finished(n):
        while n not in red.reduced:
            red.step()
        return red.reduced[n].reshape(big[n].shape)

    def emit(n, g, d, mn, vn):
        out[n] = tuple((t.T if n in COL_SHARDED else t)[None] for t in (g, d, mn, vn))

    early = FFN2_WEIGHTS + MIX_WEIGHTS
    early_g = [finished(n) for n in early]
    early_out = red.run(_adam_group, 45, [big[n] for n in early], early_g, [shard_2d("m_", n) for n in early],
                        [shard_2d("v_", n) for n in early], "adam_early")
    for n, g, (d, mn, vn) in zip(early, early_g, early_out):
        emit(n, g, d, mn, vn)
    for n in ("ffn1_w_down", "ffn1_w_gate", "ffn1_w_up"):
        g = finished(n)
        emit(n, g, *red.run(_adam_call, None, big[n], g, shard_2d("m_", n), shard_2d("v_", n), "adam_" + n))

    order = ("w_ada", "b_ada", "norm_ffn1_g", "ffn1_w_gate", "ffn1_w_up", "ffn1_w_down", "norm_mix_g", "w_in",
             "attn_sinks", "w_attn_o", "conv_w_dw", "conv_b_dw", "conv_ln_g", "conv_ln_b", "w_conv_o", "w_out",
             "norm_ffn2_g", "ffn2_w_gate", "ffn2_w_up", "ffn2_w_down", "final_norm_g")
    return (loss, grad_x, *[out[n][0] for n in order], *[out[n][1] for n in order],
            *[out[n][2] for n in order], *[out[n][3] for n in order])
```

```python
import functools

import jax
import jax.numpy as jnp
from jax import lax
from jax.experimental import pallas as pl
from jax.experimental.pallas import tpu as pltpu

F32 = jnp.float32
BF16 = jnp.bfloat16

D_MODEL = 1024
D_FF = 2816
N_CHIP = 4
N_DEV = 8
FF_SHARD = D_FF // N_CHIP
IN_WIDTH = 5376
IN_SHARD = IN_WIDTH // N_CHIP
HEAD_DIM = 64
N_Q_HEADS = 16
N_KV_HEADS = 2
BLOCK = 128
CONV_WIDTH = 31
CONV_PAD = 32
N_MOD = 9
EPS = 1e-6
FFN_RESIDUAL = 0.5
ATTN_SCALE = HEAD_DIM ** -0.5
MASK_VALUE = -1e30

ADAM_LR = 0.001
ADAM_B1 = 0.9
ADAM_B2 = 0.999
ADAM_EPS = 1e-08
ADAM_WD = 0.01
ADAM_STEP = 10

COLB_Q, COLB_CA, COLB_CB, COLB_GA, COLB_GC = 0, 1, 2, 3, 4
COLB_K, COLB_V = 40, 41
PROJ_TILE = 768

VMEM_LIMIT = 56 * 1024 * 1024
MESH = pl.DeviceIdType.MESH
ANY = pl.BlockSpec(memory_space=pl.ANY)
VMEM_SPEC = pl.BlockSpec(memory_space=pltpu.VMEM)
SMEM_SPEC = pl.BlockSpec(memory_space=pltpu.SMEM)


def _params(n_grid):
    return pltpu.CompilerParams(dimension_semantics=("arbitrary",) * n_grid, vmem_limit_bytes=VMEM_LIMIT)


def _tile(n, pref):
    t = min(n, pref)
    while n % t:
        t //= 2
    return t


def _row_tile(rows, cap):
    for t in range(min(rows, cap) // 16 * 16, 0, -16):
        if rows % t == 0:
            return t
    return rows


def _sigmoid(v):
    return 1.0 / (1.0 + jnp.exp(-v))


def _dot_nn(a, b):
    return lax.dot_general(a, b, (((1,), (0,)), ((), ())), preferred_element_type=F32)


def _dot_nt(a, b):
    return lax.dot_general(a, b, (((1,), (1,)), ((), ())), preferred_element_type=F32)


def _dot_tn(a, b):
    return lax.dot_general(a, b, (((0,), (0,)), ((), ())), preferred_element_type=F32)


ROW_CHUNK = 16


def _for_row_chunks(n_rows, fn):
    for r in range(0, n_rows, ROW_CHUNK):
        fn(slice(r, r + ROW_CHUNK))


def _norm_mod(xv, gn, sc, sh):
    r = lax.rsqrt(jnp.mean(xv * xv, axis=-1, keepdims=True) + EPS)
    return ((xv * r) * gn) * (1.0 + sc) + sh


def _accumulate(ref, first, value):
    @pl.when(first)
    def _():
        ref[...] = value

    @pl.when(jnp.logical_not(first))
    def _():
        ref[...] += value


def _norm_mod_bwd(dh, xv, gn, sc, dxo, first_of_batch, first, dx_ref, dsc_ref, dsh_ref, dgn_ref):
    r = lax.rsqrt(jnp.mean(xv * xv, axis=-1, keepdims=True) + EPS)
    xh = xv * r
    _accumulate(dsh_ref, first_of_batch, jnp.sum(dh, axis=0, keepdims=True))
    _accumulate(dsc_ref, first_of_batch, jnp.sum(dh * (xh * gn), axis=0, keepdims=True))
    dn = dh * (1.0 + sc)
    _accumulate(dgn_ref, first, jnp.sum(dn * xh, axis=0, keepdims=True))
    dxh = dn * gn
    dx_ref[...] = dxo + r * (dxh - xh * jnp.mean(dxh * xh, axis=-1, keepdims=True))


CHIP_FLIPS = ((1, 0), (0, 1), (1, 1))


def _position():
    return lax.axis_index("x"), lax.axis_index("y"), lax.axis_index("c")


def _flip(v, f):
    return 1 - v if f else v


class _GatherComm:
    def __init__(self, bufs):
        n = len(bufs)
        self.n = n
        self.operands = list(bufs)
        self.out_shape = [jax.ShapeDtypeStruct(b.shape, b.dtype) for b in bufs]
        self.aliases = {i: i for i in range(n)}
        self.sems = [pltpu.SemaphoreType.DMA((6 * n,)), pltpu.SemaphoreType.DMA((6 * n,))]
        self.rows = [b.shape[1] // 2 for b in bufs]

    def _half(self, ref, i, which):
        return ref.at[pl.ds(which * self.rows[i], self.rows[i]), :]

    def _ici(self, cins, couts, sems, i, k, dst_chip, to):
        x, y, c = _position()
        return pltpu.make_async_remote_copy(
            src_ref=self._half(cins[i].at[2 * x + y], i, c), dst_ref=self._half(couts[i].at[dst_chip], i, c),
            send_sem=sems[0].at[3 * i + k], recv_sem=sems[1].at[3 * i + k], device_id=to, device_id_type=MESH)

    def _d2d(self, couts, sems, i, k, src_chip, which):
        x, y, c = _position()
        place = self._half(couts[i].at[src_chip], i, which)
        return pltpu.make_async_remote_copy(
            src_ref=place, dst_ref=place, send_sem=sems[0].at[3 * self.n + 3 * i + k],
            recv_sem=sems[1].at[3 * self.n + 3 * i + k], device_id=(x, y, 1 - c), device_id_type=MESH)

    def _peers(self):
        x, y, _ = _position()
        return [(_flip(x, fx), _flip(y, fy)) for fx, fy in CHIP_FLIPS]

    def start(self, cins, couts, sems):
        x, y, c = _position()
        for i in range(self.n):
            for k, (px, py) in enumerate(self._peers()):
                self._ici(cins, couts, sems, i, k, 2 * x + y, (px, py, c)).start()

    def finish(self, cins, couts, sems):
        _, _, c = _position()
        peers = self._peers()
        for i in range(self.n):
            for k, (px, py) in enumerate(peers):
                self._ici(cins, couts, sems, i, k, 2 * px + py, (px, py, c)).wait_recv()
                self._d2d(couts, sems, i, k, 2 * px + py, c).start()
        for i in range(self.n):
            for k, (px, py) in enumerate(peers):
                self._d2d(couts, sems, i, k, 2 * px + py, 1 - c).wait_recv()
        for i in range(self.n):
            for k, (px, py) in enumerate(peers):
                self._ici(cins, couts, sems, i, k, 2 * px + py, (px, py, c)).wait_send()
                self._d2d(couts, sems, i, k, 2 * px + py, c).wait_send()


class _ExchangeComm:
    def __init__(self, pairs):
        n = len(pairs)
        self.n = n
        self.operands = list(pairs)
        self.out_shape = [jax.ShapeDtypeStruct((3,) + p.shape[1:], p.dtype) for p in pairs]
        self.aliases = {}
        self.sems = [pltpu.SemaphoreType.DMA((3 * n,)), pltpu.SemaphoreType.DMA((3 * n,))]

    def _copies(self, cins, couts, sems):
        x, y, c = _position()
        peers = [(_flip(x, fx), _flip(y, fy)) for fx, fy in CHIP_FLIPS]
        return [pltpu.make_async_remote_copy(
            src_ref=cins[i].at[2 * px + py], dst_ref=couts[i].at[k], send_sem=sems[0].at[3 * i + k],
            recv_sem=sems[1].at[3 * i + k], device_id=(px, py, c), device_id_type=MESH)
            for i in range(self.n) for k, (px, py) in enumerate(peers)]

    def start(self, cins, couts, sems):
        for cp in self._copies(cins, couts, sems):
            cp.start()

    def finish(self, cins, couts, sems):
        for cp in self._copies(cins, couts, sems):
            cp.wait()


class _SwapComm:
    def __init__(self, grads16):
        n = len(grads16)
        self.n = n
        self.operands = list(grads16)
        self.out_shape = [jax.ShapeDtypeStruct(g.shape[:1] + g.shape[2:], g.dtype) for g in grads16]
        self.aliases = {}
        self.sems = [pltpu.SemaphoreType.DMA((n,)), pltpu.SemaphoreType.DMA((n,))]

    def _copies(self, cins, couts, sems):
        x, y, c = _position()
        return [pltpu.make_async_remote_copy(
            src_ref=cins[i].at[:, 1 - c], dst_ref=couts[i], send_sem=sems[0].at[i], recv_sem=sems[1].at[i],
            device_id=(x, y, 1 - c), device_id_type=MESH) for i in range(self.n)]

    def start(self, cins, couts, sems):
        for cp in self._copies(cins, couts, sems):
            cp.start()

    def finish(self, cins, couts, sems):
        for cp in self._copies(cins, couts, sems):
            cp.wait()


class _JoinComm:
    def __init__(self, halves):
        n = len(halves)
        self.n = n
        self.operands = list(halves)
        self.out_shape = [jax.ShapeDtypeStruct(h.shape, h.dtype) for h in halves]
        self.aliases = {i: i for i in range(n)}
        self.sems = [pltpu.SemaphoreType.DMA((n,)), pltpu.SemaphoreType.DMA((n,))]

    def _copy(self, cins, couts, sems, i, which):
        x, y, c = _position()
        return pltpu.make_async_remote_copy(
            src_ref=cins[i].at[which], dst_ref=couts[i].at[which], send_sem=sems[0].at[i], recv_sem=sems[1].at[i],
            device_id=(x, y, 1 - c), device_id_type=MESH)

    def start(self, cins, couts, sems):
        _, _, c = _position()
        for i in range(self.n):
            self._copy(cins, couts, sems, i, c).start()

    def finish(self, cins, couts, sems):
        _, _, c = _position()
        for i in range(self.n):
            self._copy(cins, couts, sems, i, 1 - c).wait_recv()
        for i in range(self.n):
            self._copy(cins, couts, sems, i, c).wait_send()


class _Gather8Comm:
    def __init__(self, block):
        self.operands = [block]
        self.out_shape = [jax.ShapeDtypeStruct((N_DEV,) + block.shape, block.dtype)]
        self.aliases = {}
        self.sems = [pltpu.SemaphoreType.DMA((N_DEV - 1,)), pltpu.SemaphoreType.DMA((N_DEV - 1,)),
                     pltpu.SemaphoreType.DMA]
        self.flips = [(fx, fy, fc) for fx in (0, 1) for fy in (0, 1) for fc in (0, 1) if (fx, fy, fc) != (0, 0, 0)]

    def _peers(self):
        x, y, c = _position()
        return [(_flip(x, fx), _flip(y, fy), _flip(c, fc)) for fx, fy, fc in self.flips]

    def _copy(self, cins, couts, sems, k, block, to):
        return pltpu.make_async_remote_copy(src_ref=cins[0], dst_ref=couts[0].at[block], send_sem=sems[0].at[k],
                                            recv_sem=sems[1].at[k], device_id=to, device_id_type=MESH)

    def _mine(self, cins, couts, sems):
        x, y, c = _position()
        return pltpu.make_async_copy(cins[0], couts[0].at[4 * x + 2 * y + c], sems[2])

    def start(self, cins, couts, sems):
        x, y, c = _position()
        self._mine(cins, couts, sems).start()
        for k, peer in enumerate(self._peers()):
            self._copy(cins, couts, sems, k, 4 * x + 2 * y + c, peer).start()

    def finish(self, cins, couts, sems):
        for k, (px, py, pc) in enumerate(self._peers()):
            self._copy(cins, couts, sems, k, 4 * px + 2 * py + pc, (px, py, pc)).wait_recv()
        for k, peer in enumerate(self._peers()):
            self._copy(cins, couts, sems, k, 0, peer).wait_send()
        self._mine(cins, couts, sems).wait()


class _CommList:
    def __init__(self, parts):
        self.parts = list(parts)
        self.operands = [t for p in self.parts for t in p.operands]
        self.out_shape = [t for p in self.parts for t in p.out_shape]
        self.sems = [t for p in self.parts for t in p.sems]
        self.aliases = {}
        n_in = n_out = 0
        for p in self.parts:
            self.aliases.update({n_in + i: n_out + j for i, j in p.aliases.items()})
            n_in += len(p.operands)
            n_out += len(p.out_shape)

    def _split(self, cins, couts, sems):
        pos = [0, 0, 0]
        for p in self.parts:
            sizes = (len(p.operands), len(p.out_shape), len(p.sems))
            yield p, tuple(seq[a:a + k] for seq, a, k in zip((cins, couts, sems), pos, sizes))
            pos = [a + k for a, k in zip(pos, sizes)]

    def start(self, cins, couts, sems):
        for p, refs in self._split(cins, couts, sems):
            p.start(*refs)

    def finish(self, cins, couts, sems):
        for p, refs in self._split(cins, couts, sems):
            p.finish(*refs)

    def split_outputs(self, outs):
        res, pos = [], 0
        for p in self.parts:
            res.append(outs[pos:pos + len(p.out_shape)])
            pos += len(p.out_shape)
        return res


def _call(body, *, name, grid, in_specs, out_specs, out_shape, operands, scratch_shapes=(), comm=None):
    n_grid = len(grid)
    if comm is None:
        return pl.pallas_call(
            body, name=name, grid=grid, in_specs=list(in_specs), out_specs=list(out_specs), out_shape=list(out_shape),
            scratch_shapes=list(scratch_shapes), compiler_params=_params(n_grid))(*operands), ()
    counts = (len(in_specs), len(comm.operands), len(out_specs), len(comm.out_shape), len(scratch_shapes),
              len(comm.sems))

    def fused(*refs):
        parts, pos = [], 0
        for k in counts:
            parts.append(refs[pos:pos + k])
            pos += k
        ins, cins, outs, couts, scr, sems = parts
        first = functools.reduce(jnp.logical_and, [pl.program_id(d) == 0 for d in range(n_grid)])
        last = functools.reduce(jnp.logical_and, [pl.program_id(d) == grid[d] - 1 for d in range(n_grid)])

        @pl.when(first)
        def _():
            comm.start(cins, couts, sems)

        body(*ins, *outs, *scr)

        @pl.when(last)
        def _():
            comm.finish(cins, couts, sems)

    res = pl.pallas_call(
        fused, name=name, grid=grid, in_specs=list(in_specs) + [ANY] * counts[1],
        out_specs=list(out_specs) + [ANY] * counts[3], out_shape=list(out_shape) + list(comm.out_shape),
        scratch_shapes=list(scratch_shapes) + list(comm.sems),
        input_output_aliases={counts[0] + i: counts[2] + j for i, j in comm.aliases.items()},
        compiler_params=_params(n_grid))(*operands, *comm.operands)
    return res[:counts[2]], res[counts[2]:]


def _run_comm(comm, name):
    k_in, k_out = len(comm.operands), len(comm.out_shape)

    def body(*refs):
        cins, couts, sems = refs[:k_in], refs[k_in:k_in + k_out], refs[k_in + k_out:]
        comm.start(cins, couts, sems)
        comm.finish(cins, couts, sems)

    return pl.pallas_call(
        body, name=name, in_specs=[ANY] * k_in, out_specs=[ANY] * k_out, out_shape=list(comm.out_shape),
        scratch_shapes=list(comm.sems), input_output_aliases=dict(comm.aliases))(*comm.operands)


def _ffn_fwd(x, gn, sc, sh, gate, wg, wu, wd, seq, name, comm=None):
    T, D = x.shape
    J, Fs, _ = wg.shape
    tm = _tile(seq, 1024)
    nb = seq // tm

    def body(x_ref, gn_ref, sc_ref, sh_ref, gate_ref, wg_ref, wu_ref, wd_ref,
             h_ref, a_ref, u_ref, f_ref, xo_ref, hs, acc, s16):
        j = pl.program_id(1)

        @pl.when(j == 0)
        def _():
            hb = _norm_mod(x_ref[...], gn_ref[...], sc_ref[...], sh_ref[...]).astype(BF16)
            hs[...] = hb
            h_ref[...] = hb
            acc[...] = jnp.zeros_like(acc)

        hb = hs[...]
        a_all = _dot_nt(hb, wg_ref[...])
        u_all = _dot_nt(hb, wu_ref[...])

        def swiglu_rows(rows):
            a = a_all[rows, :]
            u = u_all[rows, :]
            a_ref[rows, :] = a.astype(BF16)
            u_ref[rows, :] = u.astype(BF16)
            s16[rows, :] = ((a * _sigmoid(a)) * u).astype(BF16)

        _for_row_chunks(tm, swiglu_rows)
        acc[...] += _dot_nn(s16[...], wd_ref[...])

        @pl.when(j == J - 1)
        def _():
            f = acc[...]
            f_ref[...] = f.astype(BF16)
            xo_ref[...] = x_ref[...] + (FFN_RESIDUAL * gate_ref[...]) * f

    row = pl.BlockSpec((tm, D), lambda i, j: (i, 0))
    vec = pl.BlockSpec((1, D), lambda i, j: (0, 0))
    per_b = pl.BlockSpec((None, 1, D), lambda i, j: (i // nb, 0, 0))
    hid = pl.BlockSpec((None, tm, Fs), lambda i, j: (j, i, 0))
    return _call(
        body, name=name, grid=(T // tm, J),
        in_specs=[row, vec, per_b, per_b, per_b] + [pl.BlockSpec((None, Fs, D), lambda i, j: (j, 0, 0))] * 3,
        out_specs=[row, hid, hid, row, row],
        out_shape=[jax.ShapeDtypeStruct((T, D), BF16), jax.ShapeDtypeStruct((J, T, Fs), BF16),
                   jax.ShapeDtypeStruct((J, T, Fs), BF16), jax.ShapeDtypeStruct((T, D), BF16),
                   jax.ShapeDtypeStruct((T, D), F32)],
        scratch_shapes=[pltpu.VMEM((tm, D), BF16), pltpu.VMEM((tm, D), F32), pltpu.VMEM((tm, Fs), BF16)],
        operands=(x, gn, sc, sh, gate, wg, wu, wd), comm=comm)


def _ffn_bwd(dxo, x, f, a, u, gn, sc, gate, wg, wu, wd, seq, name, comm=None):
    T, D = x.shape
    J, Fs, _ = wg.shape
    B = T // seq
    tm = _tile(seq, 512)
    nb = seq // tm

    def body(dxo_ref, x_ref, f_ref, a_ref, u_ref, gn_ref, sc_ref, gate_ref, wg_ref, wu_ref, wd_ref,
             da_ref, du_ref, s_ref, df_ref, dx_ref, dgate_ref, dsc_ref, dsh_ref, dgn_ref, dfs, acc):
        i = pl.program_id(0)
        j = pl.program_id(1)
        first_of_batch = i % nb == 0

        @pl.when(j == 0)
        def _():
            dxo_v = dxo_ref[...]
            dfb = ((FFN_RESIDUAL * gate_ref[...]) * dxo_v).astype(BF16)
            dfs[...] = dfb
            df_ref[...] = dfb
            part = jnp.sum((FFN_RESIDUAL * f_ref[...].astype(F32)) * dxo_v, axis=0, keepdims=True)
            _accumulate(dgate_ref, first_of_batch, part)
            acc[...] = jnp.zeros_like(acc)

        ds_all = _dot_nt(dfs[...], wd_ref[...])

        def swiglu_bwd_rows(rows):
            ds = ds_all[rows, :]
            av = a_ref[rows, :].astype(F32)
            uv = u_ref[rows, :].astype(F32)
            sig = _sigmoid(av)
            sil = av * sig
            s_ref[rows, :] = (sil * uv).astype(BF16)
            da_ref[rows, :] = (ds * uv * (sig * (1.0 + av * (1.0 - sig)))).astype(BF16)
            du_ref[rows, :] = (ds * sil).astype(BF16)

        _for_row_chunks(tm, swiglu_bwd_rows)
        acc[...] += _dot_nn(da_ref[...], wg_ref[...]) + _dot_nn(du_ref[...], wu_ref[...])

        @pl.when(j == J - 1)
        def _():
            _norm_mod_bwd(acc[...], x_ref[...], gn_ref[...], sc_ref[...], dxo_ref[...],
                          first_of_batch, i == 0, dx_ref, dsc_ref, dsh_ref, dgn_ref)

    row = pl.BlockSpec((tm, D), lambda i, j: (i, 0))
    vec = pl.BlockSpec((1, D), lambda i, j: (0, 0))
    per_b = pl.BlockSpec((None, 1, D), lambda i, j: (i // nb, 0, 0))
    hid = pl.BlockSpec((None, tm, Fs), lambda i, j: (j, i, 0))
    hid_shape = jax.ShapeDtypeStruct((J, T, Fs), BF16)
    per_b_shape = jax.ShapeDtypeStruct((B, 1, D), F32)
    return _call(
        body, name=name, grid=(T // tm, J),
        in_specs=[row, row, row, hid, hid, vec, per_b, per_b]
        + [pl.BlockSpec((None, Fs, D), lambda i, j: (j, 0, 0))] * 3,
        out_specs=[hid, hid, hid, row, row, per_b, per_b, per_b, vec],
        out_shape=[hid_shape, hid_shape, hid_shape, jax.ShapeDtypeStruct((T, D), BF16),
                   jax.ShapeDtypeStruct((T, D), F32), per_b_shape, per_b_shape, per_b_shape,
                   jax.ShapeDtypeStruct((1, D), F32)],
        scratch_shapes=[pltpu.VMEM((tm, D), BF16), pltpu.VMEM((tm, D), F32)],
        operands=(dxo, x, f, a, u, gn, sc, gate, wg, wu, wd), comm=comm)


def _wgrad(a, a_spec, b, b_spec, rows, cols, n_tok, name, comm=None):
    tk = _tile(n_tok, 4096)
    nk = n_tok // tk
    half = rows // 2

    def body(a_ref, b_ref, o32_ref, o16_ref, acc):
        k = pl.program_id(1)

        @pl.when(k == 0)
        def _():
            acc[...] = jnp.zeros_like(acc)

        acc[...] += _dot_tn(a_ref[...], b_ref[...])

        @pl.when(k == nk - 1)
        def _():
            for h in range(2):
                v = acc[h * half:(h + 1) * half, :]
                o32_ref[h] = v
                o16_ref[h] = v.astype(BF16)

    out_spec = pl.BlockSpec((None, 2, half, cols), lambda j, k: (j, 0, 0, 0))
    return _call(
        body, name=name, grid=(N_CHIP, nk),
        in_specs=[a_spec(tk), b_spec(tk)],
        out_specs=[out_spec, out_spec],
        out_shape=[jax.ShapeDtypeStruct((N_CHIP, 2, half, cols), F32),
                   jax.ShapeDtypeStruct((N_CHIP, 2, half, cols), BF16)],
        scratch_shapes=[pltpu.VMEM((rows, cols), F32)],
        operands=(a, b), comm=comm)


def _spec_rows(width):
    return lambda tk: pl.BlockSpec((tk, width), lambda j, k: (k, 0))


def _spec_chip_major(width):
    return lambda tk: pl.BlockSpec((None, tk, width), lambda j, k: (j, k, 0))


def _spec_col_block(width):
    return lambda tk: pl.BlockSpec((tk, width), lambda j, k: (k, j))


def _in_proj(x, gn, sc, sh, w_in, seq, comm=None):
    T, D = x.shape
    N = w_in.shape[0]
    tm = _tile(seq, 2048)
    nb = seq // tm

    def body(x_ref, gn_ref, sc_ref, sh_ref, w_ref, h_ref, p_ref, hs):
        @pl.when(pl.program_id(1) == 0)
        def _():
            hb = _norm_mod(x_ref[...], gn_ref[...], sc_ref[...], sh_ref[...]).astype(BF16)
            hs[...] = hb
            h_ref[...] = hb

        p_ref[...] = _dot_nt(hs[...], w_ref[...]).astype(BF16)

    row = pl.BlockSpec((tm, D), lambda i, j: (i, 0))
    per_b = pl.BlockSpec((None, 1, D), lambda i, j: (i // nb, 0, 0))
    return _call(
        body, name="mix_in_proj", grid=(T // tm, N // PROJ_TILE),
        in_specs=[row, pl.BlockSpec((1, D), lambda i, j: (0, 0)), per_b, per_b,
                  pl.BlockSpec((PROJ_TILE, D), lambda i, j: (j, 0))],
        out_specs=[row, pl.BlockSpec((tm, PROJ_TILE), lambda i, j: (i, j))],
        out_shape=[jax.ShapeDtypeStruct((T, D), BF16), jax.ShapeDtypeStruct((T, N), BF16)],
        scratch_shapes=[pltpu.VMEM((tm, D), BF16)],
        operands=(x, gn, sc, sh, w_in), comm=comm)


def _attn_specs(nblk):
    def own(col):
        return lambda b, n: (b * nblk + n, col)

    def prev(col):
        return lambda b, n: (b * nblk + jnp.maximum(n - 1, 0), col)

    kv = (BLOCK, 2 * HEAD_DIM)
    return [pl.BlockSpec((BLOCK, D_MODEL), own(COLB_Q)),
            pl.BlockSpec(kv, prev(COLB_K)), pl.BlockSpec(kv, own(COLB_K)),
            pl.BlockSpec(kv, prev(COLB_V)), pl.BlockSpec(kv, own(COLB_V))]


def _band_operands(prev_ref, own_ref, lo):
    band = jnp.concatenate([prev_ref[...], own_ref[...]], axis=0).astype(F32)
    rolled = pltpu.roll(band, HEAD_DIM, 1)
    zero = jnp.zeros_like(band)
    head0 = jnp.concatenate([jnp.where(lo, band, zero), jnp.where(lo, zero, rolled)], axis=0).astype(BF16)
    head1 = jnp.concatenate([jnp.where(lo, rolled, zero), jnp.where(lo, zero, band)], axis=0).astype(BF16)
    return head0, head1


PAIRS_PER_KV = N_Q_HEADS // 2 // N_KV_HEADS
BAND = 2 * BLOCK


def _band_valid(has_prev):
    qi = lax.broadcasted_iota(jnp.int32, (PAIRS_PER_KV * BLOCK, BAND), 0) & (BLOCK - 1)
    sj = lax.broadcasted_iota(jnp.int32, (PAIRS_PER_KV * BLOCK, BAND), 1)
    rel = qi + BLOCK - sj
    return (rel >= 0) & (rel < BLOCK) & ((sj >= BLOCK) | has_prev)


def _pair_lanes(kvh, pp):
    pair = kvh * PAIRS_PER_KV + pp
    return slice(pair * 2 * HEAD_DIM, (pair + 1) * 2 * HEAD_DIM)


def _stack_pairs(ref, kvh):
    return jnp.concatenate([ref[:, _pair_lanes(kvh, pp)] for pp in range(PAIRS_PER_KV)], axis=0)


def _rows_per_pair(columns):
    return jnp.concatenate(columns, axis=0)


def _attn_fwd(proj, sinks, batch, seq, comm=None):
    T = proj.shape[0]
    nblk = seq // BLOCK

    def body(sink_ref, q_ref, kp_ref, ko_ref, vp_ref, vo_ref, o_ref, lse_ref):
        lo = lax.broadcasted_iota(jnp.int32, (1, 2 * HEAD_DIM), 1) < HEAD_DIM
        head_lane = lax.broadcasted_iota(jnp.int32, (1, N_Q_HEADS), 1)
        valid = _band_valid(pl.program_id(1) > 0)
        k_ops = _band_operands(kp_ref, ko_ref, lo)
        v_ops = _band_operands(vp_ref, vo_ref, lo)
        lse_all = jnp.zeros((BLOCK, N_Q_HEADS), F32)
        col = jnp.zeros((BLOCK, 1), F32)
        side0_row = lax.broadcasted_iota(jnp.int32, (2 * BAND, 2 * HEAD_DIM), 0) < BAND
        low_lane = lax.broadcasted_iota(jnp.int32, (2 * BAND, 2 * HEAD_DIM), 1) < HEAD_DIM
        side_ones = jnp.where(side0_row == low_lane, 1.0, 0.0).astype(BF16)
        for kvh in range(N_KV_HEADS):
            s_all = _dot_nt(_stack_pairs(q_ref, kvh), k_ops[kvh]) * ATTN_SCALE
            weights, maxes, sink_terms = [], [], []
            for side in range(2):
                heads = [2 * (kvh * PAIRS_PER_KV + pp) + side for pp in range(PAIRS_PER_KV)]
                sink = _rows_per_pair([col + sink_ref[0, h] for h in heads])
                s = jnp.where(valid, s_all[:, side * BAND:(side + 1) * BAND], MASK_VALUE)
                m = jnp.maximum(jnp.max(s, axis=-1, keepdims=True), sink)
                weights.append(jnp.where(valid, jnp.exp(s - m), 0.0).astype(BF16))
                maxes.append(m)
                sink_terms.append(jnp.exp(sink - m))
            p_all = jnp.concatenate(weights, axis=1)
            den = _dot_nn(p_all, side_ones) + jnp.where(lo, sink_terms[0], sink_terms[1])
            out = _dot_nn(p_all, v_ops[kvh]) / den
            for pp in range(PAIRS_PER_KV):
                o_ref[:, _pair_lanes(kvh, pp)] = out[pp * BLOCK:(pp + 1) * BLOCK].astype(BF16)
            for side in range(2):
                lse = maxes[side] + jnp.log(den[:, side * HEAD_DIM:side * HEAD_DIM + 1])
                for pp in range(PAIRS_PER_KV):
                    h = 2 * (kvh * PAIRS_PER_KV + pp) + side
                    lse_all = jnp.where(head_lane == h, lse[pp * BLOCK:(pp + 1) * BLOCK], lse_all)
        lse_ref[...] = lse_all

    return _call(
        body, name="attn_fwd", grid=(batch, nblk),
        in_specs=[SMEM_SPEC] + _attn_specs(nblk),
        out_specs=[pl.BlockSpec((BLOCK, D_MODEL), lambda b, n: (b * nblk + n, 0)),
                   pl.BlockSpec((BLOCK, N_Q_HEADS), lambda b, n: (b * nblk + n, 0))],
        out_shape=[jax.ShapeDtypeStruct((T, D_MODEL), BF16), jax.ShapeDtypeStruct((T, N_Q_HEADS), F32)],
        operands=(sinks, proj, proj, proj, proj, proj), comm=comm)


def _conv_u(ca, cb):
    return ca.astype(F32) * _sigmoid(cb.astype(F32))


def _conv_specs(ts, tiles_per_seq):
    per_tile = ts // CONV_PAD

    def tile(col):
        return lambda b, t: (b * tiles_per_seq + t, col)

    def before(col):
        return lambda b, t: (jnp.maximum((b * tiles_per_seq + t) * per_tile - 1, 0), col)

    return [pl.BlockSpec((ts, D_MODEL), tile(COLB_CA)), pl.BlockSpec((ts, D_MODEL), tile(COLB_CB)),
            pl.BlockSpec((CONV_PAD, D_MODEL), before(COLB_CA)), pl.BlockSpec((CONV_PAD, D_MODEL), before(COLB_CB))]


SUBLANES = 8


def _fill_upad(upad, ca_ref, cb_ref, cah_ref, cbh_ref, t):
    halo = _conv_u(cah_ref[...], cbh_ref[...])
    upad[0, 0:CONV_PAD, :] = jnp.where(t > 0, halo, jnp.zeros_like(halo))
    upad[0, CONV_PAD:, :] = _conv_u(ca_ref[...], cb_ref[...])


def _fill_shifted(pad):
    rows = pad.shape[1] - SUBLANES
    for b in range(1, SUBLANES):
        pad[b, 0:rows, :] = pad[0, b:b + rows, :]


def _shifted_rows(pad, offset, rows):
    b = offset % SUBLANES
    return pad[b, offset - b:offset - b + rows, :]


def _layernorm_stats(y):
    mu = jnp.mean(y, axis=-1, keepdims=True)
    yc = y - mu
    rstd = lax.rsqrt(jnp.mean(yc * yc, axis=-1, keepdims=True) + EPS)
    return yc * rstd, rstd


def _conv_fwd(proj, w_dw, b_dw, ln_g, ln_b, batch, seq, comm=None):
    T = proj.shape[0]
    ts = _tile(seq, 256)
    nt = seq // ts
    shift = CONV_PAD - (CONV_WIDTH - 1)

    def body(ca_ref, cb_ref, cah_ref, cbh_ref, w_ref, b_ref, g_ref, beta_ref, y_ref, z_ref, upad):
        _fill_upad(upad, ca_ref, cb_ref, cah_ref, cbh_ref, pl.program_id(1))
        _fill_shifted(upad)
        y = jnp.zeros((ts, D_MODEL), F32) + b_ref[...]
        for k in range(CONV_WIDTH):
            y = y + w_ref[k:k + 1, :] * _shifted_rows(upad, shift + k, ts)
        y_ref[...] = y
        lnh, _ = _layernorm_stats(y)
        ln = lnh * g_ref[...] + beta_ref[...]
        z_ref[...] = (ln * _sigmoid(ln)).astype(BF16)

    vec = pl.BlockSpec((1, D_MODEL), lambda b, t: (0, 0))
    row = pl.BlockSpec((ts, D_MODEL), lambda b, t: (b * nt + t, 0))
    return _call(
        body, name="conv_fwd", grid=(batch, nt),
        in_specs=_conv_specs(ts, nt) + [pl.BlockSpec((CONV_PAD, D_MODEL), lambda b, t: (0, 0)), vec, vec, vec],
        out_specs=[row, row],
        out_shape=[jax.ShapeDtypeStruct((T, D_MODEL), F32), jax.ShapeDtypeStruct((T, D_MODEL), BF16)],
        scratch_shapes=[pltpu.VMEM((SUBLANES, ts + CONV_PAD, D_MODEL), F32)],
        operands=(proj, proj, proj, proj, w_dw, b_dw, ln_g, ln_b), comm=comm)


def _merge(o, z, proj, w_ao, w_co, w_out, x, gate, seq):
    T, D = x.shape
    tm = _tile(seq, 512)
    nb = seq // tm

    def body(o_ref, z_ref, ga_ref, gc_ref, wao_ref, wco_ref, wout_ref, x_ref, gate_ref,
             ya_ref, yc_ref, mg_ref, mo_ref, xo_ref):
        ya = _dot_nn(o_ref[...], wao_ref[...])
        yc = _dot_nn(z_ref[...], wco_ref[...])
        ya_ref[...] = ya.astype(BF16)
        yc_ref[...] = yc.astype(BF16)
        merged = (_sigmoid(ga_ref[...].astype(F32)) * ya + _sigmoid(gc_ref[...].astype(F32)) * yc).astype(BF16)
        mg_ref[...] = merged
        mo = _dot_nn(merged, wout_ref[...])
        mo_ref[...] = mo.astype(BF16)
        xo_ref[...] = x_ref[...] + gate_ref[...] * mo

    row = pl.BlockSpec((tm, D), lambda i: (i, 0))
    mat = pl.BlockSpec((D, D), lambda i: (0, 0))
    act = jax.ShapeDtypeStruct((T, D), BF16)
    return pl.pallas_call(
        body, name="mix_merge", grid=(T // tm,),
        in_specs=[row, row, pl.BlockSpec((tm, D), lambda i: (i, COLB_GA)), pl.BlockSpec((tm, D), lambda i: (i, COLB_GC)),
                  mat, mat, mat, row, pl.BlockSpec((None, 1, D), lambda i: (i // nb, 0, 0))],
        out_specs=[row, row, row, row, row],
        out_shape=[act, act, act, act, jax.ShapeDtypeStruct((T, D), F32)],
        compiler_params=_params(1),
    )(o, z, proj, proj, w_ao, w_co, w_out, x, gate)


def _final_loss(x, gf, target):
    T, D = x.shape
    tm = _tile(T, 512)

    def body(x_ref, gf_ref, t_ref, dx_ref, lp_ref, dgf_ref):
        first = pl.program_id(0) == 0
        xv = x_ref[...]
        gfv = gf_ref[...]
        r = lax.rsqrt(jnp.mean(xv * xv, axis=-1, keepdims=True) + EPS)
        xh = xv * r
        err = xh * gfv - t_ref[...]
        _accumulate(lp_ref, first, jnp.sum(err * err, axis=0, keepdims=True))
        dy = err * (1.0 / D)
        _accumulate(dgf_ref, first, jnp.sum(dy * xh, axis=0, keepdims=True))
        dxh = dy * gfv
        dx_ref[...] = r * (dxh - xh * jnp.mean(dxh * xh, axis=-1, keepdims=True))

    row = pl.BlockSpec((tm, D), lambda i: (i, 0))
    vec = pl.BlockSpec((1, D), lambda i: (0, 0))
    return pl.pallas_call(
        body, name="final_loss", grid=(T // tm,),
        in_specs=[row, vec, row], out_specs=[row, vec, vec],
        out_shape=[jax.ShapeDtypeStruct((T, D), F32), jax.ShapeDtypeStruct((1, D), F32),
                   jax.ShapeDtypeStruct((1, D), F32)],
        compiler_params=_params(1),
    )(x, gf, target)


def _merge_bwd(dxo, mo, gate, proj, ya, yc, w_out, w_ao, w_co, seq, comm=None):
    T, D = dxo.shape
    B = T // seq
    tm = _tile(seq, 512)
    nb = seq // tm

    def body(dxo_ref, mo_ref, gate_ref, ga_ref, gc_ref, ya_ref, yc_ref, wout_ref, wao_ref, wco_ref,
             dmo_ref, dya_ref, dyc_ref, dga_ref, dgc_ref, do_ref, dz_ref, dgate_ref):
        dxo_v = dxo_ref[...]
        dmo = (gate_ref[...] * dxo_v).astype(BF16)
        dmo_ref[...] = dmo
        _accumulate(dgate_ref, pl.program_id(0) % nb == 0,
                    jnp.sum(mo_ref[...].astype(F32) * dxo_v, axis=0, keepdims=True))
        dm = _dot_nt(dmo, wout_ref[...])
        sa = _sigmoid(ga_ref[...].astype(F32))
        sc = _sigmoid(gc_ref[...].astype(F32))
        dya = (sa * dm).astype(BF16)
        dyc = (sc * dm).astype(BF16)
        dya_ref[...] = dya
        dyc_ref[...] = dyc
        dga_ref[...] = (dm * ya_ref[...].astype(F32) * (sa * (1.0 - sa))).astype(BF16)
        dgc_ref[...] = (dm * yc_ref[...].astype(F32) * (sc * (1.0 - sc))).astype(BF16)
        do_ref[...] = _dot_nt(dya, wao_ref[...]).astype(BF16)
        dz_ref[...] = _dot_nt(dyc, wco_ref[...]).astype(BF16)

    row = pl.BlockSpec((tm, D), lambda i: (i, 0))
    mat = pl.BlockSpec((D, D), lambda i: (0, 0))
    per_b = pl.BlockSpec((None, 1, D), lambda i: (i // nb, 0, 0))
    act = jax.ShapeDtypeStruct((T, D), BF16)
    return _call(
        body, name="mix_merge_bwd", grid=(T // tm,),
        in_specs=[row, row, per_b, pl.BlockSpec((tm, D), lambda i: (i, COLB_GA)),
                  pl.BlockSpec((tm, D), lambda i: (i, COLB_GC)), row, row, mat, mat, mat],
        out_specs=[row] * 7 + [per_b],
        out_shape=[act] * 7 + [jax.ShapeDtypeStruct((B, 1, D), F32)],
        operands=(dxo, mo, gate, proj, proj, ya, yc, w_out, w_ao, w_co), comm=comm)


def _attn_bwd(proj, sinks, o, do, lse, batch, seq, comm=None):
    T = proj.shape[0]
    nblk = seq // BLOCK
    n_steps = batch * nblk

    def body(sink_ref, q_ref, kp_ref, ko_ref, vp_ref, vo_ref, o_ref, do_ref, lse_ref,
             dq_ref, dkp_ref, dko_ref, dvp_ref, dvo_ref, dsink_ref):
        lo = lax.broadcasted_iota(jnp.int32, (1, 2 * HEAD_DIM), 1) < HEAD_DIM
        sink_lane = lax.broadcasted_iota(jnp.int32, (1, 2 * HEAD_DIM), 1)
        valid = _band_valid(pl.program_id(1) > 0)
        k_ops = _band_operands(kp_ref, ko_ref, lo)
        v_ops = _band_operands(vp_ref, vo_ref, lo)
        dsink = jnp.zeros((1, 2 * HEAD_DIM), F32)
        col = jnp.zeros((BLOCK, 1), F32)

        def fold(both):
            return (jnp.where(lo, both[:BAND], 0.0)
                    + pltpu.roll(jnp.where(lo, 0.0, both[BAND:]), HEAD_DIM, 1))

        dk_heads, dv_heads = [], []
        for kvh in range(N_KV_HEADS):
            q4 = _stack_pairs(q_ref, kvh)
            do4 = _stack_pairs(do_ref, kvh)
            dd = do4.astype(F32) * _stack_pairs(o_ref, kvh).astype(F32)
            s_all = _dot_nt(q4, k_ops[kvh]) * ATTN_SCALE
            dp_all = _dot_nt(do4, v_ops[kvh])
            ds_sides, p_sides = [], []
            for side in range(2):
                heads = [2 * (kvh * PAIRS_PER_KV + pp) + side for pp in range(PAIRS_PER_KV)]
                mine = lo if side == 0 else jnp.logical_not(lo)
                cols = slice(side * BAND, (side + 1) * BAND)
                sink = _rows_per_pair([col + sink_ref[0, h] for h in heads])
                lse = _rows_per_pair([lse_ref[:, h:h + 1] for h in heads])
                delta = jnp.sum(jnp.where(mine, dd, 0.0), axis=-1, keepdims=True)
                p = jnp.where(valid, jnp.exp(jnp.where(valid, s_all[:, cols], MASK_VALUE) - lse), 0.0)
                ds_sides.append((p * (dp_all[:, cols] - delta) * ATTN_SCALE).astype(BF16))
                p_sides.append(p.astype(BF16))
                sink_part = jnp.exp(sink - lse) * delta
                for pp, h in enumerate(heads):
                    dsink = dsink + jnp.where(sink_lane == h, -jnp.sum(sink_part[pp * BLOCK:(pp + 1) * BLOCK]), 0.0)
            ds_all = jnp.concatenate(ds_sides, axis=1)
            dq4 = _dot_nn(ds_all, k_ops[kvh])
            for pp in range(PAIRS_PER_KV):
                dq_ref[:, _pair_lanes(kvh, pp)] = dq4[pp * BLOCK:(pp + 1) * BLOCK].astype(BF16)
            dk_heads.append(fold(_dot_tn(ds_all, q4)))
            dv_heads.append(fold(_dot_tn(jnp.concatenate(p_sides, axis=1), do4)))
        dk = dk_heads[0] + pltpu.roll(dk_heads[1], HEAD_DIM, 1)
        dv = dv_heads[0] + pltpu.roll(dv_heads[1], HEAD_DIM, 1)
        dkp_ref[...] = dk[:BLOCK]
        dko_ref[...] = dk[BLOCK:]
        dvp_ref[...] = dv[:BLOCK]
        dvo_ref[...] = dv[BLOCK:]
        dsink_ref[...] = dsink

    def own(b, n):
        return (b * nblk + n, 0)

    row = pl.BlockSpec((BLOCK, D_MODEL), own)
    kv = pl.BlockSpec((BLOCK, 2 * HEAD_DIM), own)
    kv_shape = jax.ShapeDtypeStruct((T, 2 * HEAD_DIM), F32)
    return _call(
        body, name="attn_bwd", grid=(batch, nblk),
        in_specs=[SMEM_SPEC] + _attn_specs(nblk) + [row, row, pl.BlockSpec((BLOCK, N_Q_HEADS), own)],
        out_specs=[row, kv, kv, kv, kv, pl.BlockSpec((None, 1, 2 * HEAD_DIM), lambda b, n: (b * nblk + n, 0, 0))],
        out_shape=[jax.ShapeDtypeStruct((T, D_MODEL), BF16), kv_shape, kv_shape, kv_shape, kv_shape,
                   jax.ShapeDtypeStruct((n_steps, 1, 2 * HEAD_DIM), F32)],
        operands=(sinks, proj, proj, proj, proj, proj, o, do, lse), comm=comm)


def _conv_bwd(proj, dz, ydw, w_dw, ln_g, ln_b, batch, seq, comm=None):
    T = proj.shape[0]
    ts = _tile(seq, 256)
    nt = seq // ts
    per_tile = ts // CONV_PAD
    shift = CONV_PAD - (CONV_WIDTH - 1)

    def body(ca_ref, cb_ref, cah_ref, cbh_ref, dz_ref, dzn_ref, y_ref, yn_ref, w_ref, g_ref, beta_ref,
             dca_ref, dcb_ref, dw_ref, db_ref, dg_ref, dbeta_ref, upad, dypad):
        t = pl.program_id(1)
        first = (pl.program_id(0) == 0) & (t == 0)
        gv = g_ref[...]

        def ln_bwd(dzv, yv):
            lnh, rstd = _layernorm_stats(yv)
            ln = lnh * gv + beta_ref[...]
            sg = _sigmoid(ln)
            dln = dzv.astype(F32) * (sg * (1.0 + ln * (1.0 - sg)))
            dyh = dln * gv
            dy = rstd * (dyh - jnp.mean(dyh, axis=-1, keepdims=True)
                         - lnh * jnp.mean(dyh * lnh, axis=-1, keepdims=True))
            return dy, dln, lnh

        dy, dln, lnh = ln_bwd(dz_ref[...], y_ref[...])
        dy_next, _, _ = ln_bwd(dzn_ref[...], yn_ref[...])
        dypad[0, 0:ts, :] = dy
        dypad[0, ts:, :] = jnp.where(t < nt - 1, dy_next, jnp.zeros_like(dy_next))
        _fill_shifted(dypad)
        _fill_upad(upad, ca_ref, cb_ref, cah_ref, cbh_ref, t)
        _fill_shifted(upad)

        _accumulate(dg_ref, first, jnp.sum(dln * lnh, axis=0, keepdims=True))
        _accumulate(dbeta_ref, first, jnp.sum(dln, axis=0, keepdims=True))
        _accumulate(db_ref, first, jnp.sum(dy, axis=0, keepdims=True))

        @pl.when(first)
        def _():
            dw_ref[...] = jnp.zeros_like(dw_ref)

        du = jnp.zeros((ts, D_MODEL), F32)
        for k in range(CONV_WIDTH):
            du = du + w_ref[k:k + 1, :] * _shifted_rows(dypad, CONV_WIDTH - 1 - k, ts)
            dw_ref[k:k + 1, :] += jnp.sum(dy * _shifted_rows(upad, shift + k, ts), axis=0, keepdims=True)
        cav = ca_ref[...].astype(F32)
        sb = _sigmoid(cb_ref[...].astype(F32))
        dca_ref[...] = (du * sb).astype(BF16)
        dcb_ref[...] = (du * cav * (sb * (1.0 - sb))).astype(BF16)

    def tile(b, t):
        return (b * nt + t, 0)

    def after(b, t):
        return (jnp.minimum((b * nt + t + 1) * per_tile, T // CONV_PAD - 1), 0)

    row = pl.BlockSpec((ts, D_MODEL), tile)
    halo = pl.BlockSpec((CONV_PAD, D_MODEL), after)
    vec = pl.BlockSpec((1, D_MODEL), lambda b, t: (0, 0))
    wspec = pl.BlockSpec((CONV_PAD, D_MODEL), lambda b, t: (0, 0))
    act = jax.ShapeDtypeStruct((T, D_MODEL), BF16)
    vec_shape = jax.ShapeDtypeStruct((1, D_MODEL), F32)
    return _call(
        body, name="conv_bwd", grid=(batch, nt),
        in_specs=_conv_specs(ts, nt) + [row, halo, row, halo, wspec, vec, vec],
        out_specs=[row, row, wspec, vec, vec, vec],
        out_shape=[act, act, jax.ShapeDtypeStruct((CONV_PAD, D_MODEL), F32), vec_shape, vec_shape, vec_shape],
        scratch_shapes=[pltpu.VMEM((SUBLANES, ts + CONV_PAD, D_MODEL), F32)] * 2,
        operands=(proj, proj, proj, proj, dz, dz, ydw, ydw, w_dw, ln_g, ln_b), comm=comm)


def _in_proj_bwd(pieces, w_cols, x, gn, sc, dxo, seq, comm=None):
    T, D = x.shape
    B = T // seq
    wide, narrow = list(pieces[:-1]), pieces[-1]
    P = len(wide)
    nw = narrow.shape[1]
    tm = _tile(seq, 512)
    nb = seq // tm

    def body(*refs):
        wide_refs = refs[:P]
        kv_ref, w_ref, wkv_ref, x_ref, gn_ref, sc_ref, dxo_ref, dx_ref, dsc_ref, dsh_ref, dgn_ref, acc = refs[P:]
        i = pl.program_id(0)
        j = pl.program_id(1)

        @pl.when(j == 0)
        def _():
            acc[...] = _dot_nn(kv_ref[...], wkv_ref[...])

        for p in range(P):
            @pl.when(j == p)
            def _(p=p):
                acc[...] += _dot_nn(wide_refs[p][...], w_ref[...])

        @pl.when(j == P - 1)
        def _():
            _norm_mod_bwd(acc[...], x_ref[...], gn_ref[...], sc_ref[...], dxo_ref[...],
                          i % nb == 0, i == 0, dx_ref, dsc_ref, dsh_ref, dgn_ref)

    row = pl.BlockSpec((tm, D), lambda i, j: (i, 0))
    vec = pl.BlockSpec((1, D), lambda i, j: (0, 0))
    per_b = pl.BlockSpec((None, 1, D), lambda i, j: (i // nb, 0, 0))
    per_b_shape = jax.ShapeDtypeStruct((B, 1, D), F32)
    return _call(
        body, name="mix_in_proj_bwd", grid=(T // tm, P),
        in_specs=[row] * P + [pl.BlockSpec((tm, nw), lambda i, j: (i, 0)),
                              pl.BlockSpec((D, D), lambda i, j: (j, 0)),
                              pl.BlockSpec((nw, D), lambda i, j: (P * D // nw, 0)), row, vec, per_b, row],
        out_specs=[row, per_b, per_b, vec],
        out_shape=[jax.ShapeDtypeStruct((T, D), F32), per_b_shape, per_b_shape, jax.ShapeDtypeStruct((1, D), F32)],
        scratch_shapes=[pltpu.VMEM((tm, D), F32)],
        operands=(*wide, narrow, w_cols, w_cols, x, gn, sc, dxo), comm=comm)


def _wgrad_rows(piece, h, out32, out16, row_offset, name):
    T, n = piece.shape
    C = h.shape[1]
    tk = _tile(T, 4096)
    nk = T // tk

    def body(a_ref, b_ref, in32, in16, o32_ref, o16_ref, acc, stage16, sems):
        k = pl.program_id(0)

        @pl.when(k == 0)
        def _():
            acc[...] = jnp.zeros_like(acc)

        acc[...] += _dot_tn(a_ref[...], b_ref[...])

        @pl.when(k == nk - 1)
        def _():
            stage16[...] = acc[...].astype(BF16)
            rows = pl.ds(row_offset, n)
            copies = [pltpu.make_async_copy(acc, o32_ref.at[rows, :], sems.at[0]),
                      pltpu.make_async_copy(stage16, o16_ref.at[rows, :], sems.at[1])]
            for cp in copies:
                cp.start()
            for cp in copies:
                cp.wait()

    return pl.pallas_call(
        body, name=name, grid=(nk,),
        in_specs=[pl.BlockSpec((tk, n), lambda k: (k, 0)), pl.BlockSpec((tk, C), lambda k: (k, 0)), ANY, ANY],
        out_specs=[ANY, ANY], out_shape=[jax.ShapeDtypeStruct(out32.shape, F32), jax.ShapeDtypeStruct(out16.shape, BF16)],
        scratch_shapes=[pltpu.VMEM((n, C), F32), pltpu.VMEM((n, C), BF16), pltpu.SemaphoreType.DMA((2,))],
        input_output_aliases={2: 0, 3: 1}, compiler_params=_params(1),
    )(piece, h, out32, out16)


def _ada_fwd(c_all, w_ada, b_cols):
    nbatch, D = c_all.shape
    N = w_ada.shape[1]
    tn = _tile(N, 768)

    def body(c_ref, w_ref, b_ref, o_ref):
        cv = c_ref[...]
        act = (cv * _sigmoid(cv)).astype(BF16)
        o_ref[...] = _dot_nn(act, w_ref[...].astype(BF16)) + b_ref[...]

    return pl.pallas_call(
        body, name="ada_fwd", grid=(N // tn,),
        in_specs=[pl.BlockSpec((nbatch, D), lambda j: (0, 0)), pl.BlockSpec((D, tn), lambda j: (0, j)),
                  pl.BlockSpec((1, tn), lambda j: (0, j))],
        out_specs=pl.BlockSpec((nbatch, tn), lambda j: (0, j)),
        out_shape=jax.ShapeDtypeStruct((nbatch, N), F32),
        compiler_params=_params(1),
    )(c_all, w_ada, b_cols)


def _adamw(w, g, m, v):
    m = ADAM_B1 * m + (1.0 - ADAM_B1) * g
    v = ADAM_B2 * v + (1.0 - ADAM_B2) * (g * g)
    m_hat = m / (1.0 - ADAM_B1 ** ADAM_STEP)
    v_hat = v / (1.0 - ADAM_B2 ** ADAM_STEP)
    delta = -ADAM_LR * (m_hat / (jnp.sqrt(v_hat) + ADAM_EPS) + ADAM_WD * w)
    return delta, m, v


def _adam_call(w, g, m, v, name, comm=None):
    R, C = w.shape
    tr = _row_tile(R, 512)

    def body(w_ref, g_ref, m_ref, v_ref, d_ref, mo_ref, vo_ref):
        d, mn, vn = _adamw(w_ref[...], g_ref[...], m_ref[...], v_ref[...])
        d_ref[...] = d
        mo_ref[...] = mn
        vo_ref[...] = vn

    blk = pl.BlockSpec((tr, C), lambda i: (i, 0))
    shape = jax.ShapeDtypeStruct((R, C), F32)
    return _call(body, name=name, grid=(R // tr,), in_specs=[blk] * 4, out_specs=[blk] * 3, out_shape=[shape] * 3,
                 operands=(w, g, m, v), comm=comm)


ADAM_GROUP_STEPS = 8


def _adam_group(ws, gs, ms, vs, name, comm=None):
    n = len(ws)

    def body(*refs):
        ins, outs = refs[:4 * n], refs[4 * n:]
        for i in range(n):
            d, mn, vn = _adamw(*(r[...] for r in ins[4 * i:4 * i + 4]))
            outs[3 * i][...] = d
            outs[3 * i + 1][...] = mn
            outs[3 * i + 2][...] = vn

    operands, in_specs, out_specs, out_shape = [], [], [], []
    for w, g, m, v in zip(ws, gs, ms, vs):
        R, C = w.shape
        blk = pl.BlockSpec((R // ADAM_GROUP_STEPS, C), lambda i: (i, 0))
        operands += [w, g, m, v]
        in_specs += [blk] * 4
        out_specs += [blk] * 3
        out_shape += [jax.ShapeDtypeStruct((R, C), F32)] * 3
    outs, comm_outs = _call(body, name=name, grid=(ADAM_GROUP_STEPS,), in_specs=in_specs, out_specs=out_specs,
                            out_shape=out_shape, operands=operands, comm=comm)
    return [tuple(outs[3 * i:3 * i + 3]) for i in range(n)], comm_outs


def _ada_adam(c_act_t, dmod_cols, w, m, v, comm):
    R, C = w.shape
    nbatch = c_act_t.shape[1]
    tr = _tile(R, 128)

    def body(ct_ref, dm_ref, w_ref, m_ref, v_ref, g_ref, d_ref, mo_ref, vo_ref):
        cv = ct_ref[...]
        g = _dot_nn((cv * _sigmoid(cv)).astype(BF16), dm_ref[...].astype(BF16))
        g_ref[...] = g
        d, mn, vn = _adamw(w_ref[...], g, m_ref[...], v_ref[...])
        d_ref[...] = d
        mo_ref[...] = mn
        vo_ref[...] = vn

    blk = pl.BlockSpec((tr, C), lambda i: (i, 0))
    shape = jax.ShapeDtypeStruct((R, C), F32)
    return _call(
        body, name="ada_adam", grid=(R // tr,),
        in_specs=[pl.BlockSpec((tr, nbatch), lambda i: (i, 0)), pl.BlockSpec((nbatch, C), lambda i: (0, 0)),
                  blk, blk, blk],
        out_specs=[blk] * 4, out_shape=[shape] * 4,
        operands=(c_act_t, dmod_cols, w, m, v), comm=comm)


def _small_adam(gathered, w, m, v, rows_b0, rows_b1, rows_vec):
    _, P, D = gathered.shape
    R = w.shape[0]

    def body(ga_ref, w_ref, m_ref, v_ref, sum_ref, g_ref, d_ref, mo_ref, vo_ref):
        total = ga_ref[0]
        for dev in range(1, N_DEV):
            total = total + ga_ref[dev]
        sum_ref[...] = total
        g_ref[...] = jnp.zeros_like(g_ref)
        g_ref[0:N_MOD, :] = (sum_ref[rows_b0:rows_b0 + N_MOD, :] + sum_ref[rows_b1:rows_b1 + N_MOD, :])
        g_ref[N_MOD:N_MOD + 8, :] = sum_ref[rows_vec:rows_vec + 8, :]
        d, mn, vn = _adamw(w_ref[...], g_ref[...], m_ref[...], v_ref[...])
        d_ref[...] = d
        mo_ref[...] = mn
        vo_ref[...] = vn

    shape = jax.ShapeDtypeStruct((R, D), F32)
    return pl.pallas_call(
        body, name="small_adam",
        in_specs=[VMEM_SPEC] * 4, out_specs=[VMEM_SPEC] * 5,
        out_shape=[jax.ShapeDtypeStruct((P, D), F32), shape, shape, shape, shape],
        compiler_params=pltpu.CompilerParams(vmem_limit_bytes=VMEM_LIMIT),
    )(gathered, w, m, v)


def _mod_exchange(part):
    _, A, W = part.shape

    def body(p_ref, out_ref, send_sems, recv_sems, local_sem):
        x, y, c = _position()
        me = 4 * x + 2 * y + c
        chip = 2 * x + y
        mine = pltpu.make_async_copy(p_ref.at[me], out_ref.at[chip], local_sem)
        mine.start()
        peers = [(_flip(x, fx), _flip(y, fy)) for fx, fy in CHIP_FLIPS]
        sends = []
        for k, (px, py) in enumerate(peers):
            sends.append(pltpu.make_async_remote_copy(
                src_ref=p_ref.at[4 * px + 2 * py + c], dst_ref=out_ref.at[chip], send_sem=send_sems.at[k],
                recv_sem=recv_sems.at[k], device_id=(px, py, c), device_id_type=MESH))
        for cp in sends:
            cp.start()
        for k, (px, py) in enumerate(peers):
            pltpu.make_async_remote_copy(
                src_ref=p_ref.at[me], dst_ref=out_ref.at[2 * px + py], send_sem=send_sems.at[k],
                recv_sem=recv_sems.at[k], device_id=(px, py, c), device_id_type=MESH).wait_recv()
        for cp in sends:
            cp.wait_send()
        mine.wait()

    return pl.pallas_call(
        body, name="mod_exchange", in_specs=[VMEM_SPEC], out_specs=VMEM_SPEC,
        out_shape=jax.ShapeDtypeStruct((N_CHIP, A, W), part.dtype),
        scratch_shapes=[pltpu.SemaphoreType.DMA((3,)), pltpu.SemaphoreType.DMA((3,)), pltpu.SemaphoreType.DMA],
    )(part)


KV_ROWS = 4 * HEAD_DIM


def _kernel_row_order(w_in_t):
    R, C = w_in_t.shape
    n_blocks = R // KV_ROWS
    q_blocks = D_MODEL // KV_ROWS

    def source(t):
        return jnp.where(t < q_blocks, t, jnp.where(t < n_blocks - 1, t + 1, q_blocks))

    def body(w_ref, o_ref):
        o_ref[...] = w_ref[...]

    return pl.pallas_call(
        body, name="w_in_row_order", grid=(n_blocks,),
        in_specs=[pl.BlockSpec((KV_ROWS, C), lambda t: (source(t), 0))],
        out_specs=pl.BlockSpec((KV_ROWS, C), lambda t: (t, 0)),
        out_shape=jax.ShapeDtypeStruct((R, C), w_in_t.dtype), compiler_params=_params(1),
    )(w_in_t)


def _cast_group(ws, names, comm):
    n = len(ws)
    steps = 4

    def body(*refs):
        w_refs, out_refs, stage, sem = refs[:n], refs[n:2 * n], refs[2 * n:3 * n], refs[3 * n]
        x, y, _ = _position()
        step = pl.program_id(0)
        copies = []
        for i in range(n):
            rows = ws[i].shape[0] // steps
            stage[i][...] = w_refs[i][...].astype(BF16)
            copies.append(pltpu.make_async_copy(
                stage[i], out_refs[i].at[2 * x + y, pl.ds(step * rows, rows), :], sem.at[i]))
        for cp in copies:
            cp.start()
        for cp in copies:
            cp.wait()

    outs, comm_outs = _call(
        body, name="cast_" + "_".join(names), grid=(steps,),
        in_specs=[pl.BlockSpec((w.shape[0] // steps, w.shape[1]), lambda i: (i, 0)) for w in ws],
        out_specs=[ANY] * n, out_shape=[jax.ShapeDtypeStruct((N_CHIP,) + w.shape, BF16) for w in ws],
        scratch_shapes=[pltpu.VMEM((w.shape[0] // steps, w.shape[1]), BF16) for w in ws]
        + [pltpu.SemaphoreType.DMA((n,))],
        operands=ws, comm=comm)
    return outs, comm_outs


def _cast_slot(w, chip_idx, name):
    R, C = w.shape
    tr = _row_tile(R, 512)

    def body(chip_ref, w_ref, o_ref):
        o_ref[...] = w_ref[...].astype(BF16)

    return pl.pallas_call(
        body, name=name,
        grid_spec=pltpu.PrefetchScalarGridSpec(
            num_scalar_prefetch=1, grid=(R // tr,),
            in_specs=[pl.BlockSpec((tr, C), lambda i, chip_ref: (i, 0))],
            out_specs=pl.BlockSpec((None, tr, C), lambda i, chip_ref: (chip_ref[0], i, 0))),
        out_shape=jax.ShapeDtypeStruct((N_CHIP, R, C), BF16),
        compiler_params=_params(1),
    )(chip_idx, w)


def _pair_sum(g32, recv, core, name):
    J, _, r, C = g32.shape

    def body(core_ref, g_ref, r_ref, o_ref):
        o_ref[...] = (g_ref[...] + r_ref[...].astype(F32)).astype(BF16)

    return pl.pallas_call(
        body, name=name,
        grid_spec=pltpu.PrefetchScalarGridSpec(
            num_scalar_prefetch=1, grid=(J,),
            in_specs=[pl.BlockSpec((None, None, r, C), lambda j, core_ref: (j, core_ref[0], 0, 0)),
                      pl.BlockSpec((None, r, C), lambda j, core_ref: (j, 0, 0))],
            out_specs=pl.BlockSpec((None, r, C), lambda j, core_ref: (j, 0, 0))),
        out_shape=jax.ShapeDtypeStruct((J, r, C), BF16),
        compiler_params=_params(1),
    )(core, g32, recv)


def _chip_sum(g32, recv_sib, recv_chips, core_chip, name):
    J, _, r, C = g32.shape

    def body(idx_ref, g_ref, s_ref, o_ref_in, o_ref):
        total = g_ref[...] + s_ref[...].astype(F32)
        for k in range(3):
            total = total + o_ref_in[k].astype(F32)
        o_ref[...] = total

    return pl.pallas_call(
        body, name=name,
        grid_spec=pltpu.PrefetchScalarGridSpec(
            num_scalar_prefetch=1, grid=(1,),
            in_specs=[pl.BlockSpec((None, None, r, C), lambda i, idx: (idx[1], idx[0], 0, 0)),
                      pl.BlockSpec((None, r, C), lambda i, idx: (idx[1], 0, 0)),
                      pl.BlockSpec((3, r, C), lambda i, idx: (0, 0, 0))],
            out_specs=pl.BlockSpec((None, r, C), lambda i, idx: (idx[0], 0, 0))),
        out_shape=jax.ShapeDtypeStruct((2, r, C), F32),
        compiler_params=_params(1),
    )(core_chip, g32, recv_sib, recv_chips)


ICI_US_PER_ELEMENT = 4.6e-5


class _Reducer:
    def __init__(self, core_idx, core_chip):
        self.core_idx, self.core_chip = core_idx, core_chip
        self.grads, self.halves, self.reduced = {}, {}, {}
        self.ready_swap, self.ready_exchange, self.ready_join = [], [], []
        self.inflight, self.current = ([], [], [], None), None
        self.flushes = 0
        self.extra, self.extra_out = None, None

    def add(self, name, grad_pair):
        self.grads[name] = grad_pair
        self.ready_swap.append(name)

    def comm(self, budget_us):
        swaps, self.ready_swap = self.ready_swap, []
        joins, self.ready_join = self.ready_join, []
        exchanges, waiting = [], []
        for item in self.ready_exchange:
            cost = ICI_US_PER_ELEMENT * 2 * item[2].shape[1] * item[2].shape[2]
            if cost <= budget_us:
                exchanges.append(item)
                budget_us -= cost
            else:
                waiting.append(item)
        self.ready_exchange = waiting
        parts = []
        if swaps:
            parts.append(_SwapComm([self.grads[n][1] for n in swaps]))
        if exchanges:
            parts.append(_ExchangeComm([pair for _, _, pair in exchanges]))
        if joins:
            parts.append(_JoinComm([self.halves[n] for n in joins]))
        extra, self.extra = self.extra, None
        if extra is not None:
            parts.append(extra)
        self.inflight = (swaps, exchanges, joins, extra)
        self.current = _CommList(parts) if parts else None
        return self.current

    def done(self, comm_outs):
        if self.current is None:
            return
        swaps, exchanges, joins, extra = self.inflight
        outs = iter(self.current.split_outputs(list(comm_outs)))
        if swaps:
            for n, recv in zip(swaps, next(outs)):
                pair = _pair_sum(self.grads[n][0], recv, self.core_idx, "pair_sum_" + n)
                self.ready_exchange.append((n, recv, pair))
        if exchanges:
            for (n, recv, _), chips in zip(exchanges, next(outs)):
                self.halves[n] = _chip_sum(self.grads[n][0], recv, chips, self.core_chip, "chip_sum_" + n)
                self.ready_join.append(n)
        if joins:
            self.reduced.update(zip(joins, next(outs)))
        if extra is not None:
            self.extra_out = next(outs)
        self.current = None

    def run(self, kernel, budget_us, *args, **kwargs):
        if budget_us is None:
            return kernel(*args, comm=None, **kwargs)[0]
        outs, comm_outs = kernel(*args, comm=self.comm(budget_us), **kwargs)
        self.done(comm_outs)
        return outs

    def step(self):
        comm = self.comm(float("inf"))
        self.flushes += 1
        self.done(_run_comm(comm, "grad_reduce_tail_%d" % self.flushes))


BIG_WEIGHTS = ("ffn1_w_gate", "ffn1_w_up", "ffn1_w_down", "w_in", "w_attn_o", "w_conv_o", "w_out",
               "ffn2_w_gate", "ffn2_w_up", "ffn2_w_down")
VECTORS = ("norm_ffn1_g", "norm_mix_g", "conv_b_dw", "conv_ln_g", "conv_ln_b", "norm_ffn2_g", "final_norm_g")
ROW_DMOD0, ROW_DMOD1, ROW_LOSS, ROW_VEC, ROW_SINK, ROW_CONVW, SMALL_ROWS = 0, 16, 32, 33, 40, 41, 72


FFN1_WEIGHTS = ("ffn1_w_gate", "ffn1_w_up", "ffn1_w_down")
FFN2_WEIGHTS = ("ffn2_w_gate", "ffn2_w_up", "ffn2_w_down")
MIX_WEIGHTS = ("w_in", "w_attn_o", "w_conv_o", "w_out")
COL_SHARDED = ("ffn1_w_gate", "ffn1_w_up", "ffn2_w_gate", "ffn2_w_up", "w_in")


def _local_grads(x, target, mod, slots, ffn1_gathered, small, seq, core_idx, core_chip):
    T, D = x.shape
    B = T // seq
    mods = [mod[:, k][:, None, :] for k in range(N_MOD)]
    sh1, sc1, g1, sh2, sc2, g2, sh3, sc3, g3 = mods
    w = dict(zip(FFN1_WEIGHTS, ffn1_gathered))

    (h1, a1, u1, f1, x1), outs = _ffn_fwd(
        x, small["norm_ffn1_g"], sc1, sh1, g1, w["ffn1_w_gate"], w["ffn1_w_up"], w["ffn1_w_down"], seq, "ffn1_fwd",
        comm=_GatherComm([slots[n] for n in MIX_WEIGHTS]))
    w["w_in"] = outs[0]
    w_ao, w_co, w_o = [t.reshape(D, D) for t in outs[1:]]
    w_in_cols = _kernel_row_order(w["w_in"].reshape(IN_WIDTH, D))
    (h2, proj), _ = _in_proj(x1, small["norm_mix_g"], sc2, sh2, w_in_cols, seq)
    (o, lse), (w["ffn2_w_gate"], w["ffn2_w_up"]) = _attn_fwd(
        proj, small["attn_sinks"], B, seq, comm=_GatherComm([slots["ffn2_w_gate"], slots["ffn2_w_up"]]))
    (ydw, z), (w["ffn2_w_down"],) = _conv_fwd(
        proj, small["conv_w_dw"], small["conv_b_dw"], small["conv_ln_g"], small["conv_ln_b"], B, seq,
        comm=_GatherComm([slots["ffn2_w_down"]]))
    ya, yc, merged, mo, x2 = _merge(o, z, proj, w_ao, w_co, w_o, x1, g2, seq)
    (h3, a3, u3, f3, x3), _ = _ffn_fwd(x2, small["norm_ffn2_g"], sc3, sh3, g3, w["ffn2_w_gate"], w["ffn2_w_up"],
                                       w["ffn2_w_down"], seq, "ffn2_fwd")
    dx3, loss_parts, d_final_g = _final_loss(x3, small["final_norm_g"], target)

    red = _Reducer(core_idx, core_chip)

    def weight_grad(name, budget_us, a, a_spec, b, b_spec, rows, cols):
        red.add(name, red.run(_wgrad, budget_us, a, a_spec, b, b_spec, rows, cols, T, "dw_" + name))

    def ffn_backward(prefix, dw_budget_us, dxo, xin, h, a, u, f, gn, sc, gate, before_weight_grads=None):
        da, du, s, df, dx, dgate, dsc, dsh, dgn = red.run(
            _ffn_bwd, 170, dxo, xin, f, a, u, gn, sc, gate, w[prefix + "_w_gate"], w[prefix + "_w_up"],
            w[prefix + "_w_down"], seq, prefix + "_bwd")
        if before_weight_grads is not None:
            before_weight_grads(dgate, dsc, dsh, dgn)
        weight_grad(prefix + "_w_down", dw_budget_us, s, _spec_chip_major(FF_SHARD), df, _spec_rows(D), FF_SHARD, D)
        weight_grad(prefix + "_w_gate", dw_budget_us, da, _spec_chip_major(FF_SHARD), h, _spec_rows(D), FF_SHARD, D)
        weight_grad(prefix + "_w_up", dw_budget_us, du, _spec_chip_major(FF_SHARD), h, _spec_rows(D), FF_SHARD, D)
        return dx, dgate, dsc, dsh, dgn

    dx2, dg3, dsc3, dsh3, d_gn3 = ffn_backward("ffn2", None, dx3, x2, h3, a3, u3, f3, small["norm_ffn2_g"], sc3, g3)

    dmo, dya, dyc, dga, dgc, do, dz, dg2 = red.run(_merge_bwd, None, dx2, mo, g2, proj, ya, yc, w_o, w_ao, w_co, seq)
    shard = D // N_CHIP
    weight_grad("w_out", None, merged, _spec_col_block(shard), dmo, _spec_rows(D), shard, D)
    weight_grad("w_attn_o", None, o, _spec_col_block(shard), dya, _spec_rows(D), shard, D)
    weight_grad("w_conv_o", None, z, _spec_col_block(shard), dyc, _spec_rows(D), shard, D)
    dq, dkp, dko, dvp, dvo, dsink_steps = red.run(_attn_bwd, 100, proj, small["attn_sinks"], o, do, lse, B, seq)
    dca, dcb, d_conv_w, d_conv_b, d_ln_g, d_ln_b = red.run(
        _conv_bwd, 165, proj, dz, ydw, small["conv_w_dw"], small["conv_ln_g"], small["conv_ln_b"], B, seq)

    def band_sum(own, prev):
        prev = prev.reshape(B, seq // BLOCK, BLOCK, 2 * HEAD_DIM)
        moved = jnp.concatenate([prev[:, 1:], jnp.zeros_like(prev[:, :1])], axis=1)
        return (own + moved.reshape(T, 2 * HEAD_DIM)).astype(BF16)

    dkv = jnp.concatenate([band_sum(dko, dkp), band_sum(dvo, dvp)], axis=1)
    g32, g16 = lax.empty((IN_WIDTH, D), F32), lax.empty((IN_WIDTH, D), BF16)
    row_of = {"q": 0, "kv": D, "conv_a": D + 4 * HEAD_DIM, "conv_b": 2 * D + 4 * HEAD_DIM,
              "gate_a": 3 * D + 4 * HEAD_DIM, "gate_c": 4 * D + 4 * HEAD_DIM}
    for tag, piece in (("q", dq), ("kv", dkv), ("conv_a", dca), ("conv_b", dcb), ("gate_a", dga), ("gate_c", dgc)):
        g32, g16 = _wgrad_rows(piece, h2, g32, g16, row_of[tag], "dw_w_in_" + tag)
    red.add("w_in", tuple(g.reshape(N_CHIP, 2, IN_SHARD // 2, D) for g in (g32, g16)))
    dx1, dsc2, dsh2, d_gn2 = red.run(_in_proj_bwd, 90, (dq, dca, dcb, dga, dgc, dkv), w_in_cols, x1,
                                     small["norm_mix_g"], sc2, dx2, seq)

    def gather_small_grads(dg1, dsc1, dsh1, d_gn1):
        dmod = jnp.concatenate([dsh1, dsc1, dg1, dsh2, dsc2, dg2, dsh3, dsc3, dg3], axis=1)
        d_sinks = jnp.sum(dsink_steps, axis=0)
        vec_grads = {"norm_ffn1_g": d_gn1, "norm_mix_g": d_gn2, "conv_b_dw": d_conv_b, "conv_ln_g": d_ln_g,
                     "conv_ln_b": d_ln_b, "norm_ffn2_g": d_gn3, "final_norm_g": d_final_g}
        block = jnp.zeros((SMALL_ROWS, D), F32)
        block = block.at[ROW_DMOD0:ROW_DMOD0 + N_MOD].set(dmod[0]).at[ROW_DMOD1:ROW_DMOD1 + N_MOD].set(dmod[1])
        block = block.at[ROW_VEC:ROW_VEC + len(VECTORS)].set(jnp.concatenate([vec_grads[n] for n in VECTORS], axis=0))
        block = block.at[ROW_LOSS].set(loss_parts[0])
        block = block.at[ROW_SINK, :2 * HEAD_DIM].set(d_sinks[0])
        block = block.at[ROW_CONVW:ROW_CONVW + CONV_WIDTH].set(d_conv_w[:CONV_WIDTH])
        red.extra = _Gather8Comm(block)

    dx0, _, _, _, _ = ffn_backward("ffn1", 38, dx1, x, h1, a1, u1, f1, small["norm_ffn1_g"], sc1, g1,
                                   before_weight_grads=gather_small_grads)
    return dx0, red, red.extra_out[0]


def kernel(x, c, w_ada, b_ada, norm_ffn1_g, ffn1_w_gate, ffn1_w_up, ffn1_w_down, norm_mix_g, w_in, attn_sinks, w_attn_o, conv_w_dw, conv_b_dw, conv_ln_g, conv_ln_b, w_conv_o, w_out, norm_ffn2_g, ffn2_w_gate, ffn2_w_up, ffn2_w_down, final_norm_g, loss_target, m_w_ada, m_b_ada, m_norm_ffn1_g, m_ffn1_w_gate, m_ffn1_w_up, m_ffn1_w_down, m_norm_mix_g, m_w_in, m_attn_sinks, m_w_attn_o, m_conv_w_dw, m_conv_b_dw, m_conv_ln_g, m_conv_ln_b, m_w_conv_o, m_w_out, m_norm_ffn2_g, m_ffn2_w_gate, m_ffn2_w_up, m_ffn2_w_down, m_final_norm_g, v_w_ada, v_b_ada, v_norm_ffn1_g, v_ffn1_w_gate, v_ffn1_w_up, v_ffn1_w_down, v_norm_mix_g, v_w_in, v_attn_sinks, v_w_attn_o, v_conv_w_dw, v_conv_b_dw, v_conv_ln_g, v_conv_ln_b, v_w_conv_o, v_w_out, v_norm_ffn2_g, v_ffn2_w_gate, v_ffn2_w_up, v_ffn2_w_down, v_final_norm_g):
    args = dict(locals())
    B, seq, D = x.shape
    T = B * seq
    xi, yi, ci = _position()
    chip = 2 * xi + yi
    dev = 4 * xi + 2 * yi + ci

    def shard_2d(prefix, name):
        t = args[prefix + name][0]
        return t.T if name in COL_SHARDED else t

    big = {n: shard_2d("", n) for n in BIG_WEIGHTS}
    final_g = final_norm_g[None, :]
    vec_w = {n: (args[n] if n != "final_norm_g" else final_g) for n in VECTORS}

    core_idx = jnp.reshape(ci, (1,)).astype(jnp.int32)
    chip_idx = jnp.reshape(chip, (1,)).astype(jnp.int32)
    core_chip = jnp.stack([ci, chip]).astype(jnp.int32)
    conv_cols = D // N_CHIP
    conv_flat = jnp.pad(conv_w_dw[0].reshape(-1), (0, 8 * D - CONV_WIDTH * conv_cols)).reshape(8, D)
    first_block = jnp.concatenate([jnp.pad(c, ((0, 8 - B), (0, 0))), conv_flat], axis=0)
    slots = {n: _cast_slot(big[n], chip_idx, "cast_" + n) for n in FFN1_WEIGHTS}
    later = [n for n in BIG_WEIGHTS if n not in FFN1_WEIGHTS]
    carried = _CommList([_GatherComm([slots[n] for n in FFN1_WEIGHTS]), _Gather8Comm(first_block)])
    later_slots, carried_outs = _cast_group([big[n] for n in later], ["later_weights"], carried)
    ffn1_gathered, (first,) = carried.split_outputs(carried_outs)
    slots.update(zip(later, later_slots))
    c_all = first[:, :B].reshape(N_DEV * B, D)
    conv_taps = first[::2, 8:].reshape(N_CHIP, 8 * D)[:, :CONV_WIDTH * conv_cols]
    conv_taps = conv_taps.reshape(N_CHIP, CONV_WIDTH, conv_cols).transpose(1, 0, 2).reshape(CONV_WIDTH, D)
    conv_taps = jnp.pad(conv_taps, ((0, CONV_PAD - CONV_WIDTH), (0, 0)))

    ada_cols = w_ada.shape[2]
    b_cols = lax.dynamic_slice(b_ada, (0, chip * ada_cols), (1, ada_cols))
    mod_part = _ada_fwd(c_all, w_ada[0], b_cols).reshape(N_DEV, B, ada_cols)
    mod = _mod_exchange(mod_part).transpose(1, 0, 2).reshape(B, N_MOD, D)

    small = dict(vec_w)
    small["attn_sinks"] = attn_sinks
    small["conv_w_dw"] = conv_taps

    dx, red, small_all = _local_grads(
        x.reshape(T, D), loss_target.reshape(T, D), mod, slots, ffn1_gathered, small, seq, core_idx, core_chip)
    grad_x = dx.reshape(B, seq, D)
    out = {}


    def pack_small(prefix):
        rows = [args[prefix + "b_ada"].reshape(N_MOD, D)]
        rows += [args[prefix + n].reshape(1, D) for n in VECTORS]
        rows += [jnp.pad(args[prefix + "attn_sinks"], ((0, 0), (0, D - N_Q_HEADS)))]
        return jnp.pad(jnp.concatenate(rows, axis=0), ((0, 24 - N_MOD - len(VECTORS) - 1), (0, 0)))

    small_sum, sg, sd, sm, sv = _small_adam(small_all, pack_small(""), pack_small("m_"), pack_small("v_"),
                                           ROW_DMOD0, ROW_DMOD1, ROW_VEC)
    loss = (0.5 / D) * jnp.sum(small_sum[ROW_LOSS])

    def unpack_small(t):
        res = {"b_ada": t[:N_MOD].reshape(1, N_MOD * D)}
        for k, n in enumerate(VECTORS):
            res[n] = t[N_MOD + k].reshape(args[n].shape)
        res["attn_sinks"] = t[N_MOD + len(VECTORS), :N_Q_HEADS].reshape(1, N_Q_HEADS)
        return res

    unpacked = [unpack_small(t) for t in (sg, sd, sm, sv)]
    for n in ("b_ada", "attn_sinks") + VECTORS:
        out[n] = tuple(u[n] for u in unpacked)

    conv_g = lax.dynamic_slice(small_sum, (ROW_CONVW, chip * conv_cols), (CONV_WIDTH, conv_cols))
    d, mn, vn = red.run(_adam_call, None, conv_w_dw[0], conv_g, m_conv_w_dw[0], v_conv_w_dw[0], "adam_conv_w_dw")
    out["conv_w_dw"] = tuple(t[None] for t in (conv_g, d, mn, vn))

    dmod_rows = jnp.stack([small_all[:, ROW_DMOD0:ROW_DMOD0 + N_MOD], small_all[:, ROW_DMOD1:ROW_DMOD1 + N_MOD]], axis=1)
    dmod_all = dmod_rows.reshape(N_DEV * B, N_MOD * D)
    dmod_cols = lax.dynamic_slice(dmod_all, (0, chip * ada_cols), (N_DEV * B, ada_cols))
    ada_out = red.run(_ada_adam, 35, c_all.T, dmod_cols, w_ada[0], m_w_ada[0], v_w_ada[0])
    out["w_ada"] = tuple(t[None] for t in ada_out)

    def finished(n):
        while n not in red.reduced:
            red.step()
        return red.reduced[n].reshape(big[n].shape)

    def emit(n, g, d, mn, vn):
        out[n] = tuple((t.T if n in COL_SHARDED else t)[None] for t in (g, d, mn, vn))

    early = FFN2_WEIGHTS + MIX_WEIGHTS
    early_g = [finished(n) for n in early]
    early_out = red.run(_adam_group, 45, [big[n] for n in early], early_g, [shard_2d("m_", n) for n in early],
                        [shard_2d("v_", n) for n in early], "adam_early")
    for n, g, (d, mn, vn) in zip(early, early_g, early_out):
        emit(n, g, d, mn, vn)
    for n in ("ffn1_w_down", "ffn1_w_gate", "ffn1_w_up"):
        g = finished(n)
        emit(n, g, *red.run(_adam_call, None, big[n], g, shard_2d("m_", n), shard_2d("v_", n), "adam_" + n))

    order = ("w_ada", "b_ada", "norm_ffn1_g", "ffn1_w_gate", "ffn1_w_up", "ffn1_w_down", "norm_mix_g", "w_in",
             "attn_sinks", "w_attn_o", "conv_w_dw", "conv_b_dw", "conv_ln_g", "conv_ln_b", "w_conv_o", "w_out",
             "norm_ffn2_g", "ffn2_w_gate", "ffn2_w_up", "ffn2_w_down", "final_norm_g")
    return (loss, grad_x, *[out[n][0] for n in order], *[out[n][1] for n in order],
            *[out[n][2] for n in order], *[out[n][3] for n in order])
```

```python
import functools

import jax
import jax.numpy as jnp
from jax import lax
from jax.experimental import pallas as pl
from jax.experimental.pallas import tpu as pltpu

F32 = jnp.float32
BF16 = jnp.bfloat16

D_MODEL = 1024
D_FF = 2816
N_CHIP = 4
N_DEV = 8
FF_SHARD = D_FF // N_CHIP
IN_WIDTH = 5376
IN_SHARD = IN_WIDTH // N_CHIP
HEAD_DIM = 64
N_Q_HEADS = 16
N_KV_HEADS = 2
BLOCK = 128
CONV_WIDTH = 31
CONV_PAD = 32
N_MOD = 9
EPS = 1e-6
FFN_RESIDUAL = 0.5
ATTN_SCALE = HEAD_DIM ** -0.5
MASK_VALUE = -1e30

ADAM_LR = 0.001
ADAM_B1 = 0.9
ADAM_B2 = 0.999
ADAM_EPS = 1e-08
ADAM_WD = 0.01
ADAM_STEP = 10

COLB_Q, COLB_CA, COLB_CB, COLB_GA, COLB_GC = 0, 1, 2, 3, 4
COLB_K, COLB_V = 40, 41
PROJ_TILE = 768

VMEM_LIMIT = 56 * 1024 * 1024
MESH = pl.DeviceIdType.MESH
ANY = pl.BlockSpec(memory_space=pl.ANY)
VMEM_SPEC = pl.BlockSpec(memory_space=pltpu.VMEM)
SMEM_SPEC = pl.BlockSpec(memory_space=pltpu.SMEM)


def _params(n_grid):
    return pltpu.CompilerParams(dimension_semantics=("arbitrary",) * n_grid, vmem_limit_bytes=VMEM_LIMIT)


def _tile(n, pref):
    t = min(n, pref)
    while n % t:
        t //= 2
    return t


def _row_tile(rows, cap):
    for t in range(min(rows, cap) // 16 * 16, 0, -16):
        if rows % t == 0:
            return t
    return rows


def _sigmoid(v):
    return 1.0 / (1.0 + jnp.exp(-v))


def _dot_nn(a, b):
    return lax.dot_general(a, b, (((1,), (0,)), ((), ())), preferred_element_type=F32)


def _dot_nt(a, b):
    return lax.dot_general(a, b, (((1,), (1,)), ((), ())), preferred_element_type=F32)


def _dot_tn(a, b):
    return lax.dot_general(a, b, (((0,), (0,)), ((), ())), preferred_element_type=F32)


ROW_CHUNK = 16


def _for_row_chunks(n_rows, fn):
    for r in range(0, n_rows, ROW_CHUNK):
        fn(slice(r, r + ROW_CHUNK))


def _norm_mod(xv, gn, sc, sh):
    r = lax.rsqrt(jnp.mean(xv * xv, axis=-1, keepdims=True) + EPS)
    return ((xv * r) * gn) * (1.0 + sc) + sh


def _accumulate(ref, first, value):
    @pl.when(first)
    def _():
        ref[...] = value

    @pl.when(jnp.logical_not(first))
    def _():
        ref[...] += value


def _norm_mod_bwd(dh, xv, gn, sc, dxo, first_of_batch, first, dx_ref, dsc_ref, dsh_ref, dgn_ref):
    r = lax.rsqrt(jnp.mean(xv * xv, axis=-1, keepdims=True) + EPS)
    xh = xv * r
    _accumulate(dsh_ref, first_of_batch, jnp.sum(dh, axis=0, keepdims=True))
    _accumulate(dsc_ref, first_of_batch, jnp.sum(dh * (xh * gn), axis=0, keepdims=True))
    dn = dh * (1.0 + sc)
    _accumulate(dgn_ref, first, jnp.sum(dn * xh, axis=0, keepdims=True))
    dxh = dn * gn
    dx_ref[...] = dxo + r * (dxh - xh * jnp.mean(dxh * xh, axis=-1, keepdims=True))


CHIP_FLIPS = ((1, 0), (0, 1), (1, 1))


def _position():
    return lax.axis_index("x"), lax.axis_index("y"), lax.axis_index("c")


def _flip(v, f):
    return 1 - v if f else v


class _GatherComm:
    def __init__(self, bufs):
        n = len(bufs)
        self.n = n
        self.operands = list(bufs)
        self.out_shape = [jax.ShapeDtypeStruct(b.shape, b.dtype) for b in bufs]
        self.aliases = {i: i for i in range(n)}
        self.sems = [pltpu.SemaphoreType.DMA((6 * n,)), pltpu.SemaphoreType.DMA((6 * n,))]
        self.rows = [b.shape[1] // 2 for b in bufs]

    def _half(self, ref, i, which):
        return ref.at[pl.ds(which * self.rows[i], self.rows[i]), :]

    def _ici(self, cins, couts, sems, i, k, dst_chip, to):
        x, y, c = _position()
        return pltpu.make_async_remote_copy(
            src_ref=self._half(cins[i].at[2 * x + y], i, c), dst_ref=self._half(couts[i].at[dst_chip], i, c),
            send_sem=sems[0].at[3 * i + k], recv_sem=sems[1].at[3 * i + k], device_id=to, device_id_type=MESH)

    def _d2d(self, couts, sems, i, k, src_chip, which):
        x, y, c = _position()
        place = self._half(couts[i].at[src_chip], i, which)
        return pltpu.make_async_remote_copy(
            src_ref=place, dst_ref=place, send_sem=sems[0].at[3 * self.n + 3 * i + k],
            recv_sem=sems[1].at[3 * self.n + 3 * i + k], device_id=(x, y, 1 - c), device_id_type=MESH)

    def _peers(self):
        x, y, _ = _position()
        return [(_flip(x, fx), _flip(y, fy)) for fx, fy in CHIP_FLIPS]

    def start(self, cins, couts, sems):
        x, y, c = _position()
        for i in range(self.n):
            for k, (px, py) in enumerate(self._peers()):
                self._ici(cins, couts, sems, i, k, 2 * x + y, (px, py, c)).start()

    def finish(self, cins, couts, sems):
        _, _, c = _position()
        peers = self._peers()
        for i in range(self.n):
            for k, (px, py) in enumerate(peers):
                self._ici(cins, couts, sems, i, k, 2 * px + py, (px, py, c)).wait_recv()
                self._d2d(couts, sems, i, k, 2 * px + py, c).start()
        for i in range(self.n):
            for k, (px, py) in enumerate(peers):
                self._d2d(couts, sems, i, k, 2 * px + py, 1 - c).wait_recv()
        for i in range(self.n):
            for k, (px, py) in enumerate(peers):
                self._ici(cins, couts, sems, i, k, 2 * px + py, (px, py, c)).wait_send()
                self._d2d(couts, sems, i, k, 2 * px + py, c).wait_send()


class _ExchangeComm:
    def __init__(self, pairs):
        n = len(pairs)
        self.n = n
        self.operands = list(pairs)
        self.out_shape = [jax.ShapeDtypeStruct((3,) + p.shape[1:], p.dtype) for p in pairs]
        self.aliases = {}
        self.sems = [pltpu.SemaphoreType.DMA((3 * n,)), pltpu.SemaphoreType.DMA((3 * n,))]

    def _copies(self, cins, couts, sems):
        x, y, c = _position()
        peers = [(_flip(x, fx), _flip(y, fy)) for fx, fy in CHIP_FLIPS]
        return [pltpu.make_async_remote_copy(
            src_ref=cins[i].at[2 * px + py], dst_ref=couts[i].at[k], send_sem=sems[0].at[3 * i + k],
            recv_sem=sems[1].at[3 * i + k], device_id=(px, py, c), device_id_type=MESH)
            for i in range(self.n) for k, (px, py) in enumerate(peers)]

    def start(self, cins, couts, sems):
        for cp in self._copies(cins, couts, sems):
            cp.start()

    def finish(self, cins, couts, sems):
        for cp in self._copies(cins, couts, sems):
            cp.wait()


class _SwapComm:
    def __init__(self, grads16):
        n = len(grads16)
        self.n = n
        self.operands = list(grads16)
        self.out_shape = [jax.ShapeDtypeStruct(g.shape[:1] + g.shape[2:], g.dtype) for g in grads16]
        self.aliases = {}
        self.sems = [pltpu.SemaphoreType.DMA((n,)), pltpu.SemaphoreType.DMA((n,))]

    def _copies(self, cins, couts, sems):
        x, y, c = _position()
        return [pltpu.make_async_remote_copy(
            src_ref=cins[i].at[:, 1 - c], dst_ref=couts[i], send_sem=sems[0].at[i], recv_sem=sems[1].at[i],
            device_id=(x, y, 1 - c), device_id_type=MESH) for i in range(self.n)]

    def start(self, cins, couts, sems):
        for cp in self._copies(cins, couts, sems):
            cp.start()

    def finish(self, cins, couts, sems):
        for cp in self._copies(cins, couts, sems):
            cp.wait()


class _JoinComm:
    def __init__(self, halves):
        n = len(halves)
        self.n = n
        self.operands = list(halves)
        self.out_shape = [jax.ShapeDtypeStruct(h.shape, h.dtype) for h in halves]
        self.aliases = {i: i for i in range(n)}
        self.sems = [pltpu.SemaphoreType.DMA((n,)), pltpu.SemaphoreType.DMA((n,))]

    def _copy(self, cins, couts, sems, i, which):
        x, y, c = _position()
        return pltpu.make_async_remote_copy(
            src_ref=cins[i].at[which], dst_ref=couts[i].at[which], send_sem=sems[0].at[i], recv_sem=sems[1].at[i],
            device_id=(x, y, 1 - c), device_id_type=MESH)

    def start(self, cins, couts, sems):
        _, _, c = _position()
        for i in range(self.n):
            self._copy(cins, couts, sems, i, c).start()

    def finish(self, cins, couts, sems):
        _, _, c = _position()
        for i in range(self.n):
            self._copy(cins, couts, sems, i, 1 - c).wait_recv()
        for i in range(self.n):
            self._copy(cins, couts, sems, i, c).wait_send()


class _Gather8Comm:
    def __init__(self, block):
        self.operands = [block]
        self.out_shape = [jax.ShapeDtypeStruct((N_DEV,) + block.shape, block.dtype)]
        self.aliases = {}
        self.sems = [pltpu.SemaphoreType.DMA((N_DEV - 1,)), pltpu.SemaphoreType.DMA((N_DEV - 1,)),
                     pltpu.SemaphoreType.DMA]
        self.flips = [(fx, fy, fc) for fx in (0, 1) for fy in (0, 1) for fc in (0, 1) if (fx, fy, fc) != (0, 0, 0)]

    def _peers(self):
        x, y, c = _position()
        return [(_flip(x, fx), _flip(y, fy), _flip(c, fc)) for fx, fy, fc in self.flips]

    def _copy(self, cins, couts, sems, k, block, to):
        return pltpu.make_async_remote_copy(src_ref=cins[0], dst_ref=couts[0].at[block], send_sem=sems[0].at[k],
                                            recv_sem=sems[1].at[k], device_id=to, device_id_type=MESH)

    def _mine(self, cins, couts, sems):
        x, y, c = _position()
        return pltpu.make_async_copy(cins[0], couts[0].at[4 * x + 2 * y + c], sems[2])

    def start(self, cins, couts, sems):
        x, y, c = _position()
        self._mine(cins, couts, sems).start()
        for k, peer in enumerate(self._peers()):
            self._copy(cins, couts, sems, k, 4 * x + 2 * y + c, peer).start()

    def finish(self, cins, couts, sems):
        for k, (px, py, pc) in enumerate(self._peers()):
            self._copy(cins, couts, sems, k, 4 * px + 2 * py + pc, (px, py, pc)).wait_recv()
        for k, peer in enumerate(self._peers()):
            self._copy(cins, couts, sems, k, 0, peer).wait_send()
        self._mine(cins, couts, sems).wait()


class _CommList:
    def __init__(self, parts):
        self.parts = list(parts)
        self.operands = [t for p in self.parts for t in p.operands]
        self.out_shape = [t for p in self.parts for t in p.out_shape]
        self.sems = [t for p in self.parts for t in p.sems]
        self.aliases = {}
        n_in = n_out = 0
        for p in self.parts:
            self.aliases.update({n_in + i: n_out + j for i, j in p.aliases.items()})
            n_in += len(p.operands)
            n_out += len(p.out_shape)

    def _split(self, cins, couts, sems):
        pos = [0, 0, 0]
        for p in self.parts:
            sizes = (len(p.operands), len(p.out_shape), len(p.sems))
            yield p, tuple(seq[a:a + k] for seq, a, k in zip((cins, couts, sems), pos, sizes))
            pos = [a + k for a, k in zip(pos, sizes)]

    def start(self, cins, couts, sems):
        for p, refs in self._split(cins, couts, sems):
            p.start(*refs)

    def finish(self, cins, couts, sems):
        for p, refs in self._split(cins, couts, sems):
            p.finish(*refs)

    def split_outputs(self, outs):
        res, pos = [], 0
        for p in self.parts:
            res.append(outs[pos:pos + len(p.out_shape)])
            pos += len(p.out_shape)
        return res


def _call(body, *, name, grid, in_specs, out_specs, out_shape, operands, scratch_shapes=(), comm=None):
    n_grid = len(grid)
    if comm is None:
        return pl.pallas_call(
            body, name=name, grid=grid, in_specs=list(in_specs), out_specs=list(out_specs), out_shape=list(out_shape),
            scratch_shapes=list(scratch_shapes), compiler_params=_params(n_grid))(*operands), ()
    counts = (len(in_specs), len(comm.operands), len(out_specs), len(comm.out_shape), len(scratch_shapes),
              len(comm.sems))

    def fused(*refs):
        parts, pos = [], 0
        for k in counts:
            parts.append(refs[pos:pos + k])
            pos += k
        ins, cins, outs, couts, scr, sems = parts
        first = functools.reduce(jnp.logical_and, [pl.program_id(d) == 0 for d in range(n_grid)])
        last = functools.reduce(jnp.logical_and, [pl.program_id(d) == grid[d] - 1 for d in range(n_grid)])

        @pl.when(first)
        def _():
            comm.start(cins, couts, sems)

        body(*ins, *outs, *scr)

        @pl.when(last)
        def _():
            comm.finish(cins, couts, sems)

    res = pl.pallas_call(
        fused, name=name, grid=grid, in_specs=list(in_specs) + [ANY] * counts[1],
        out_specs=list(out_specs) + [ANY] * counts[3], out_shape=list(out_shape) + list(comm.out_shape),
        scratch_shapes=list(scratch_shapes) + list(comm.sems),
        input_output_aliases={counts[0] + i: counts[2] + j for i, j in comm.aliases.items()},
        compiler_params=_params(n_grid))(*operands, *comm.operands)
    return res[:counts[2]], res[counts[2]:]


def _run_comm(comm, name):
    k_in, k_out = len(comm.operands), len(comm.out_shape)

    def body(*refs):
        cins, couts, sems = refs[:k_in], refs[k_in:k_in + k_out], refs[k_in + k_out:]
        comm.start(cins, couts, sems)
        comm.finish(cins, couts, sems)

    return pl.pallas_call(
        body, name=name, in_specs=[ANY] * k_in, out_specs=[ANY] * k_out, out_shape=list(comm.out_shape),
        scratch_shapes=list(comm.sems), input_output_aliases=dict(comm.aliases))(*comm.operands)


def _ffn_fwd(x, gn, sc, sh, gate, wg, wu, wd, seq, name, comm=None):
    T, D = x.shape
    J, Fs, _ = wg.shape
    tm = _tile(seq, 1024)
    nb = seq // tm

    def body(x_ref, gn_ref, sc_ref, sh_ref, gate_ref, wg_ref, wu_ref, wd_ref,
             h_ref, a_ref, u_ref, f_ref, xo_ref, hs, acc, s16):
        j = pl.program_id(1)

        @pl.when(j == 0)
        def _():
            hb = _norm_mod(x_ref[...], gn_ref[...], sc_ref[...], sh_ref[...]).astype(BF16)
            hs[...] = hb
            h_ref[...] = hb
            acc[...] = jnp.zeros_like(acc)

        hb = hs[...]
        a_all = _dot_nt(hb, wg_ref[...])
        u_all = _dot_nt(hb, wu_ref[...])

        def swiglu_rows(rows):
            a = a_all[rows, :]
            u = u_all[rows, :]
            a_ref[rows, :] = a.astype(BF16)
            u_ref[rows, :] = u.astype(BF16)
            s16[rows, :] = ((a * _sigmoid(a)) * u).astype(BF16)

        _for_row_chunks(tm, swiglu_rows)
        acc[...] += _dot_nn(s16[...], wd_ref[...])

        @pl.when(j == J - 1)
        def _():
            f = acc[...]
            f_ref[...] = f.astype(BF16)
            xo_ref[...] = x_ref[...] + (FFN_RESIDUAL * gate_ref[...]) * f

    row = pl.BlockSpec((tm, D), lambda i, j: (i, 0))
    vec = pl.BlockSpec((1, D), lambda i, j: (0, 0))
    per_b = pl.BlockSpec((None, 1, D), lambda i, j: (i // nb, 0, 0))
    hid = pl.BlockSpec((None, tm, Fs), lambda i, j: (j, i, 0))
    return _call(
        body, name=name, grid=(T // tm, J),
        in_specs=[row, vec, per_b, per_b, per_b] + [pl.BlockSpec((None, Fs, D), lambda i, j: (j, 0, 0))] * 3,
        out_specs=[row, hid, hid, row, row],
        out_shape=[jax.ShapeDtypeStruct((T, D), BF16), jax.ShapeDtypeStruct((J, T, Fs), BF16),
                   jax.ShapeDtypeStruct((J, T, Fs), BF16), jax.ShapeDtypeStruct((T, D), BF16),
                   jax.ShapeDtypeStruct((T, D), F32)],
        scratch_shapes=[pltpu.VMEM((tm, D), BF16), pltpu.VMEM((tm, D), F32), pltpu.VMEM((tm, Fs), BF16)],
        operands=(x, gn, sc, sh, gate, wg, wu, wd), comm=comm)


def _ffn_bwd(dxo, x, f, a, u, gn, sc, gate, wg, wu, wd, seq, name, comm=None):
    T, D = x.shape
    J, Fs, _ = wg.shape
    B = T // seq
    tm = _tile(seq, 512)
    nb = seq // tm

    def body(dxo_ref, x_ref, f_ref, a_ref, u_ref, gn_ref, sc_ref, gate_ref, wg_ref, wu_ref, wd_ref,
             da_ref, du_ref, s_ref, df_ref, dx_ref, dgate_ref, dsc_ref, dsh_ref, dgn_ref, dfs, acc):
        i = pl.program_id(0)
        j = pl.program_id(1)
        first_of_batch = i % nb == 0

        @pl.when(j == 0)
        def _():
            dxo_v = dxo_ref[...]
            dfb = ((FFN_RESIDUAL * gate_ref[...]) * dxo_v).astype(BF16)
            dfs[...] = dfb
            df_ref[...] = dfb
            part = jnp.sum((FFN_RESIDUAL * f_ref[...].astype(F32)) * dxo_v, axis=0, keepdims=True)
            _accumulate(dgate_ref, first_of_batch, part)
            acc[...] = jnp.zeros_like(acc)

        ds_all = _dot_nt(dfs[...], wd_ref[...])

        def swiglu_bwd_rows(rows):
            ds = ds_all[rows, :]
            av = a_ref[rows, :].astype(F32)
            uv = u_ref[rows, :].astype(F32)
            sig = _sigmoid(av)
            sil = av * sig
            s_ref[rows, :] = (sil * uv).astype(BF16)
            da_ref[rows, :] = (ds * uv * (sig * (1.0 + av * (1.0 - sig)))).astype(BF16)
            du_ref[rows, :] = (ds * sil).astype(BF16)

        _for_row_chunks(tm, swiglu_bwd_rows)
        acc[...] += _dot_nn(da_ref[...], wg_ref[...]) + _dot_nn(du_ref[...], wu_ref[...])

        @pl.when(j == J - 1)
        def _():
            _norm_mod_bwd(acc[...], x_ref[...], gn_ref[...], sc_ref[...], dxo_ref[...],
                          first_of_batch, i == 0, dx_ref, dsc_ref, dsh_ref, dgn_ref)

    row = pl.BlockSpec((tm, D), lambda i, j: (i, 0))
    vec = pl.BlockSpec((1, D), lambda i, j: (0, 0))
    per_b = pl.BlockSpec((None, 1, D), lambda i, j: (i // nb, 0, 0))
    hid = pl.BlockSpec((None, tm, Fs), lambda i, j: (j, i, 0))
    hid_shape = jax.ShapeDtypeStruct((J, T, Fs), BF16)
    per_b_shape = jax.ShapeDtypeStruct((B, 1, D), F32)
    return _call(
        body, name=name, grid=(T // tm, J),
        in_specs=[row, row, row, hid, hid, vec, per_b, per_b]
        + [pl.BlockSpec((None, Fs, D), lambda i, j: (j, 0, 0))] * 3,
        out_specs=[hid, hid, hid, row, row, per_b, per_b, per_b, vec],
        out_shape=[hid_shape, hid_shape, hid_shape, jax.ShapeDtypeStruct((T, D), BF16),
                   jax.ShapeDtypeStruct((T, D), F32), per_b_shape, per_b_shape, per_b_shape,
                   jax.ShapeDtypeStruct((1, D), F32)],
        scratch_shapes=[pltpu.VMEM((tm, D), BF16), pltpu.VMEM((tm, D), F32)],
        operands=(dxo, x, f, a, u, gn, sc, gate, wg, wu, wd), comm=comm)


def _wgrad(a, a_spec, b, b_spec, rows, cols, n_tok, name, comm=None):
    tk = _tile(n_tok, 4096)
    nk = n_tok // tk
    half = rows // 2

    def body(a_ref, b_ref, o32_ref, o16_ref, acc):
        k = pl.program_id(1)

        @pl.when(k == 0)
        def _():
            acc[...] = jnp.zeros_like(acc)

        acc[...] += _dot_tn(a_ref[...], b_ref[...])

        @pl.when(k == nk - 1)
        def _():
            for h in range(2):
                v = acc[h * half:(h + 1) * half, :]
                o32_ref[h] = v
                o16_ref[h] = v.astype(BF16)

    out_spec = pl.BlockSpec((None, 2, half, cols), lambda j, k: (j, 0, 0, 0))
    return _call(
        body, name=name, grid=(N_CHIP, nk),
        in_specs=[a_spec(tk), b_spec(tk)],
        out_specs=[out_spec, out_spec],
        out_shape=[jax.ShapeDtypeStruct((N_CHIP, 2, half, cols), F32),
                   jax.ShapeDtypeStruct((N_CHIP, 2, half, cols), BF16)],
        scratch_shapes=[pltpu.VMEM((rows, cols), F32)],
        operands=(a, b), comm=comm)


def _spec_rows(width):
    return lambda tk: pl.BlockSpec((tk, width), lambda j, k: (k, 0))


def _spec_chip_major(width):
    return lambda tk: pl.BlockSpec((None, tk, width), lambda j, k: (j, k, 0))


def _spec_col_block(width):
    return lambda tk: pl.BlockSpec((tk, width), lambda j, k: (k, j))


def _in_proj(x, gn, sc, sh, w_in, seq, comm=None):
    T, D = x.shape
    N = w_in.shape[0]
    tm = _tile(seq, 2048)
    nb = seq // tm

    def body(x_ref, gn_ref, sc_ref, sh_ref, w_ref, h_ref, p_ref, hs):
        @pl.when(pl.program_id(1) == 0)
        def _():
            hb = _norm_mod(x_ref[...], gn_ref[...], sc_ref[...], sh_ref[...]).astype(BF16)
            hs[...] = hb
            h_ref[...] = hb

        p_ref[...] = _dot_nt(hs[...], w_ref[...]).astype(BF16)

    row = pl.BlockSpec((tm, D), lambda i, j: (i, 0))
    per_b = pl.BlockSpec((None, 1, D), lambda i, j: (i // nb, 0, 0))
    return _call(
        body, name="mix_in_proj", grid=(T // tm, N // PROJ_TILE),
        in_specs=[row, pl.BlockSpec((1, D), lambda i, j: (0, 0)), per_b, per_b,
                  pl.BlockSpec((PROJ_TILE, D), lambda i, j: (j, 0))],
        out_specs=[row, pl.BlockSpec((tm, PROJ_TILE), lambda i, j: (i, j))],
        out_shape=[jax.ShapeDtypeStruct((T, D), BF16), jax.ShapeDtypeStruct((T, N), BF16)],
        scratch_shapes=[pltpu.VMEM((tm, D), BF16)],
        operands=(x, gn, sc, sh, w_in), comm=comm)


def _attn_specs(nblk):
    def own(col):
        return lambda b, n: (b * nblk + n, col)

    def prev(col):
        return lambda b, n: (b * nblk + jnp.maximum(n - 1, 0), col)

    kv = (BLOCK, 2 * HEAD_DIM)
    return [pl.BlockSpec((BLOCK, D_MODEL), own(COLB_Q)),
            pl.BlockSpec(kv, prev(COLB_K)), pl.BlockSpec(kv, own(COLB_K)),
            pl.BlockSpec(kv, prev(COLB_V)), pl.BlockSpec(kv, own(COLB_V))]


def _band_operands(prev_ref, own_ref, lo):
    band = jnp.concatenate([prev_ref[...], own_ref[...]], axis=0).astype(F32)
    rolled = pltpu.roll(band, HEAD_DIM, 1)
    zero = jnp.zeros_like(band)
    head0 = jnp.concatenate([jnp.where(lo, band, zero), jnp.where(lo, zero, rolled)], axis=0).astype(BF16)
    head1 = jnp.concatenate([jnp.where(lo, rolled, zero), jnp.where(lo, zero, band)], axis=0).astype(BF16)
    return head0, head1


PAIRS_PER_KV = N_Q_HEADS // 2 // N_KV_HEADS
BAND = 2 * BLOCK


def _band_valid(has_prev):
    qi = lax.broadcasted_iota(jnp.int32, (PAIRS_PER_KV * BLOCK, BAND), 0) & (BLOCK - 1)
    sj = lax.broadcasted_iota(jnp.int32, (PAIRS_PER_KV * BLOCK, BAND), 1)
    rel = qi + BLOCK - sj
    return (rel >= 0) & (rel < BLOCK) & ((sj >= BLOCK) | has_prev)


def _pair_lanes(kvh, pp):
    pair = kvh * PAIRS_PER_KV + pp
    return slice(pair * 2 * HEAD_DIM, (pair + 1) * 2 * HEAD_DIM)


def _stack_pairs(ref, kvh):
    return jnp.concatenate([ref[:, _pair_lanes(kvh, pp)] for pp in range(PAIRS_PER_KV)], axis=0)


def _rows_per_pair(columns):
    return jnp.concatenate(columns, axis=0)


def _attn_fwd(proj, sinks, batch, seq, comm=None):
    T = proj.shape[0]
    nblk = seq // BLOCK

    def body(sink_ref, q_ref, kp_ref, ko_ref, vp_ref, vo_ref, o_ref, lse_ref):
        lo = lax.broadcasted_iota(jnp.int32, (1, 2 * HEAD_DIM), 1) < HEAD_DIM
        head_lane = lax.broadcasted_iota(jnp.int32, (1, N_Q_HEADS), 1)
        valid = _band_valid(pl.program_id(1) > 0)
        k_ops = _band_operands(kp_ref, ko_ref, lo)
        v_ops = _band_operands(vp_ref, vo_ref, lo)
        lse_all = jnp.zeros((BLOCK, N_Q_HEADS), F32)
        col = jnp.zeros((BLOCK, 1), F32)
        side0_row = lax.broadcasted_iota(jnp.int32, (2 * BAND, 2 * HEAD_DIM), 0) < BAND
        low_lane = lax.broadcasted_iota(jnp.int32, (2 * BAND, 2 * HEAD_DIM), 1) < HEAD_DIM
        side_ones = jnp.where(side0_row == low_lane, 1.0, 0.0).astype(BF16)
        for kvh in range(N_KV_HEADS):
            s_all = _dot_nt(_stack_pairs(q_ref, kvh), k_ops[kvh]) * ATTN_SCALE
            weights, maxes, sink_terms = [], [], []
            for side in range(2):
                heads = [2 * (kvh * PAIRS_PER_KV + pp) + side for pp in range(PAIRS_PER_KV)]
                sink = _rows_per_pair([col + sink_ref[0, h] for h in heads])
                s = jnp.where(valid, s_all[:, side * BAND:(side + 1) * BAND], MASK_VALUE)
                m = jnp.maximum(jnp.max(s, axis=-1, keepdims=True), sink)
                weights.append(jnp.where(valid, jnp.exp(s - m), 0.0).astype(BF16))
                maxes.append(m)
                sink_terms.append(jnp.exp(sink - m))
            p_all = jnp.concatenate(weights, axis=1)
            den = _dot_nn(p_all, side_ones) + jnp.where(lo, sink_terms[0], sink_terms[1])
            out = _dot_nn(p_all, v_ops[kvh]) / den
            for pp in range(PAIRS_PER_KV):
                o_ref[:, _pair_lanes(kvh, pp)] = out[pp * BLOCK:(pp + 1) * BLOCK].astype(BF16)
            for side in range(2):
                lse = maxes[side] + jnp.log(den[:, side * HEAD_DIM:side * HEAD_DIM + 1])
                for pp in range(PAIRS_PER_KV):
                    h = 2 * (kvh * PAIRS_PER_KV + pp) + side
                    lse_all = jnp.where(head_lane == h, lse[pp * BLOCK:(pp + 1) * BLOCK], lse_all)
        lse_ref[...] = lse_all

    return _call(
        body, name="attn_fwd", grid=(batch, nblk),
        in_specs=[SMEM_SPEC] + _attn_specs(nblk),
        out_specs=[pl.BlockSpec((BLOCK, D_MODEL), lambda b, n: (b * nblk + n, 0)),
                   pl.BlockSpec((BLOCK, N_Q_HEADS), lambda b, n: (b * nblk + n, 0))],
        out_shape=[jax.ShapeDtypeStruct((T, D_MODEL), BF16), jax.ShapeDtypeStruct((T, N_Q_HEADS), F32)],
        operands=(sinks, proj, proj, proj, proj, proj), comm=comm)


def _conv_u(ca, cb):
    return ca.astype(F32) * _sigmoid(cb.astype(F32))


def _conv_specs(ts, tiles_per_seq):
    per_tile = ts // CONV_PAD

    def tile(col):
        return lambda b, t: (b * tiles_per_seq + t, col)

    def before(col):
        return lambda b, t: (jnp.maximum((b * tiles_per_seq + t) * per_tile - 1, 0), col)

    return [pl.BlockSpec((ts, D_MODEL), tile(COLB_CA)), pl.BlockSpec((ts, D_MODEL), tile(COLB_CB)),
            pl.BlockSpec((CONV_PAD, D_MODEL), before(COLB_CA)), pl.BlockSpec((CONV_PAD, D_MODEL), before(COLB_CB))]


SUBLANES = 8


def _fill_upad(upad, ca_ref, cb_ref, cah_ref, cbh_ref, t):
    halo = _conv_u(cah_ref[...], cbh_ref[...])
    upad[0, 0:CONV_PAD, :] = jnp.where(t > 0, halo, jnp.zeros_like(halo))
    upad[0, CONV_PAD:, :] = _conv_u(ca_ref[...], cb_ref[...])


def _fill_shifted(pad):
    rows = pad.shape[1] - SUBLANES
    for b in range(1, SUBLANES):
        pad[b, 0:rows, :] = pad[0, b:b + rows, :]


def _shifted_rows(pad, offset, rows):
    b = offset % SUBLANES
    return pad[b, offset - b:offset - b + rows, :]


def _layernorm_stats(y):
    mu = jnp.mean(y, axis=-1, keepdims=True)
    yc = y - mu
    rstd = lax.rsqrt(jnp.mean(yc * yc, axis=-1, keepdims=True) + EPS)
    return yc * rstd, rstd


def _conv_fwd(proj, w_dw, b_dw, ln_g, ln_b, batch, seq, comm=None):
    T = proj.shape[0]
    ts = _tile(seq, 256)
    nt = seq // ts
    shift = CONV_PAD - (CONV_WIDTH - 1)

    def body(ca_ref, cb_ref, cah_ref, cbh_ref, w_ref, b_ref, g_ref, beta_ref, y_ref, z_ref, upad):
        _fill_upad(upad, ca_ref, cb_ref, cah_ref, cbh_ref, pl.program_id(1))
        _fill_shifted(upad)
        y = jnp.zeros((ts, D_MODEL), F32) + b_ref[...]
        for k in range(CONV_WIDTH):
            y = y + w_ref[k:k + 1, :] * _shifted_rows(upad, shift + k, ts)
        y_ref[...] = y
        lnh, _ = _layernorm_stats(y)
        ln = lnh * g_ref[...] + beta_ref[...]
        z_ref[...] = (ln * _sigmoid(ln)).astype(BF16)

    vec = pl.BlockSpec((1, D_MODEL), lambda b, t: (0, 0))
    row = pl.BlockSpec((ts, D_MODEL), lambda b, t: (b * nt + t, 0))
    return _call(
        body, name="conv_fwd", grid=(batch, nt),
        in_specs=_conv_specs(ts, nt) + [pl.BlockSpec((CONV_PAD, D_MODEL), lambda b, t: (0, 0)), vec, vec, vec],
        out_specs=[row, row],
        out_shape=[jax.ShapeDtypeStruct((T, D_MODEL), F32), jax.ShapeDtypeStruct((T, D_MODEL), BF16)],
        scratch_shapes=[pltpu.VMEM((SUBLANES, ts + CONV_PAD, D_MODEL), F32)],
        operands=(proj, proj, proj, proj, w_dw, b_dw, ln_g, ln_b), comm=comm)


def _merge(o, z, proj, w_ao, w_co, w_out, x, gate, seq):
    T, D = x.shape
    tm = _tile(seq, 512)
    nb = seq // tm

    def body(o_ref, z_ref, ga_ref, gc_ref, wao_ref, wco_ref, wout_ref, x_ref, gate_ref,
             ya_ref, yc_ref, mg_ref, mo_ref, xo_ref):
        ya = _dot_nn(o_ref[...], wao_ref[...])
        yc = _dot_nn(z_ref[...], wco_ref[...])
        ya_ref[...] = ya.astype(BF16)
        yc_ref[...] = yc.astype(BF16)
        merged = (_sigmoid(ga_ref[...].astype(F32)) * ya + _sigmoid(gc_ref[...].astype(F32)) * yc).astype(BF16)
        mg_ref[...] = merged
        mo = _dot_nn(merged, wout_ref[...])
        mo_ref[...] = mo.astype(BF16)
        xo_ref[...] = x_ref[...] + gate_ref[...] * mo

    row = pl.BlockSpec((tm, D), lambda i: (i, 0))
    mat = pl.BlockSpec((D, D), lambda i: (0, 0))
    act = jax.ShapeDtypeStruct((T, D), BF16)
    return pl.pallas_call(
        body, name="mix_merge", grid=(T // tm,),
        in_specs=[row, row, pl.BlockSpec((tm, D), lambda i: (i, COLB_GA)), pl.BlockSpec((tm, D), lambda i: (i, COLB_GC)),
                  mat, mat, mat, row, pl.BlockSpec((None, 1, D), lambda i: (i // nb, 0, 0))],
        out_specs=[row, row, row, row, row],
        out_shape=[act, act, act, act, jax.ShapeDtypeStruct((T, D), F32)],
        compiler_params=_params(1),
    )(o, z, proj, proj, w_ao, w_co, w_out, x, gate)


def _final_loss(x, gf, target):
    T, D = x.shape
    tm = _tile(T, 512)

    def body(x_ref, gf_ref, t_ref, dx_ref, lp_ref, dgf_ref):
        first = pl.program_id(0) == 0
        xv = x_ref[...]
        gfv = gf_ref[...]
        r = lax.rsqrt(jnp.mean(xv * xv, axis=-1, keepdims=True) + EPS)
        xh = xv * r
        err = xh * gfv - t_ref[...]
        _accumulate(lp_ref, first, jnp.sum(err * err, axis=0, keepdims=True))
        dy = err * (1.0 / D)
        _accumulate(dgf_ref, first, jnp.sum(dy * xh, axis=0, keepdims=True))
        dxh = dy * gfv
        dx_ref[...] = r * (dxh - xh * jnp.mean(dxh * xh, axis=-1, keepdims=True))

    row = pl.BlockSpec((tm, D), lambda i: (i, 0))
    vec = pl.BlockSpec((1, D), lambda i: (0, 0))
    return pl.pallas_call(
        body, name="final_loss", grid=(T // tm,),
        in_specs=[row, vec, row], out_specs=[row, vec, vec],
        out_shape=[jax.ShapeDtypeStruct((T, D), F32), jax.ShapeDtypeStruct((1, D), F32),
                   jax.ShapeDtypeStruct((1, D), F32)],
        compiler_params=_params(1),
    )(x, gf, target)


def _merge_bwd(dxo, mo, gate, proj, ya, yc, w_out, w_ao, w_co, seq, comm=None):
    T, D = dxo.shape
    B = T // seq
    tm = _tile(seq, 512)
    nb = seq // tm

    def body(dxo_ref, mo_ref, gate_ref, ga_ref, gc_ref, ya_ref, yc_ref, wout_ref, wao_ref, wco_ref,
             dmo_ref, dya_ref, dyc_ref, dga_ref, dgc_ref, do_ref, dz_ref, dgate_ref):
        dxo_v = dxo_ref[...]
        dmo = (gate_ref[...] * dxo_v).astype(BF16)
        dmo_ref[...] = dmo
        _accumulate(dgate_ref, pl.program_id(0) % nb == 0,
                    jnp.sum(mo_ref[...].astype(F32) * dxo_v, axis=0, keepdims=True))
        dm = _dot_nt(dmo, wout_ref[...])
        sa = _sigmoid(ga_ref[...].astype(F32))
        sc = _sigmoid(gc_ref[...].astype(F32))
        dya = (sa * dm).astype(BF16)
        dyc = (sc * dm).astype(BF16)
        dya_ref[...] = dya
        dyc_ref[...] = dyc
        dga_ref[...] = (dm * ya_ref[...].astype(F32) * (sa * (1.0 - sa))).astype(BF16)
        dgc_ref[...] = (dm * yc_ref[...].astype(F32) * (sc * (1.0 - sc))).astype(BF16)
        do_ref[...] = _dot_nt(dya, wao_ref[...]).astype(BF16)
        dz_ref[...] = _dot_nt(dyc, wco_ref[...]).astype(BF16)

    row = pl.BlockSpec((tm, D), lambda i: (i, 0))
    mat = pl.BlockSpec((D, D), lambda i: (0, 0))
    per_b = pl.BlockSpec((None, 1, D), lambda i: (i // nb, 0, 0))
    act = jax.ShapeDtypeStruct((T, D), BF16)
    return _call(
        body, name="mix_merge_bwd", grid=(T // tm,),
        in_specs=[row, row, per_b, pl.BlockSpec((tm, D), lambda i: (i, COLB_GA)),
                  pl.BlockSpec((tm, D), lambda i: (i, COLB_GC)), row, row, mat, mat, mat],
        out_specs=[row] * 7 + [per_b],
        out_shape=[act] * 7 + [jax.ShapeDtypeStruct((B, 1, D), F32)],
        operands=(dxo, mo, gate, proj, proj, ya, yc, w_out, w_ao, w_co), comm=comm)


def _attn_bwd(proj, sinks, o, do, lse, batch, seq, comm=None):
    T = proj.shape[0]
    nblk = seq // BLOCK
    n_steps = batch * nblk

    def body(sink_ref, q_ref, kp_ref, ko_ref, vp_ref, vo_ref, o_ref, do_ref, lse_ref,
             dq_ref, dkp_ref, dko_ref, dvp_ref, dvo_ref, dsink_ref):
        lo = lax.broadcasted_iota(jnp.int32, (1, 2 * HEAD_DIM), 1) < HEAD_DIM
        sink_lane = lax.broadcasted_iota(jnp.int32, (1, 2 * HEAD_DIM), 1)
        valid = _band_valid(pl.program_id(1) > 0)
        k_ops = _band_operands(kp_ref, ko_ref, lo)
        v_ops = _band_operands(vp_ref, vo_ref, lo)
        dsink = jnp.zeros((1, 2 * HEAD_DIM), F32)
        col = jnp.zeros((BLOCK, 1), F32)

        def fold(both):
            return (jnp.where(lo, both[:BAND], 0.0)
                    + pltpu.roll(jnp.where(lo, 0.0, both[BAND:]), HEAD_DIM, 1))

        dk_heads, dv_heads = [], []
        for kvh in range(N_KV_HEADS):
            q4 = _stack_pairs(q_ref, kvh)
            do4 = _stack_pairs(do_ref, kvh)
            dd = do4.astype(F32) * _stack_pairs(o_ref, kvh).astype(F32)
            s_all = _dot_nt(q4, k_ops[kvh]) * ATTN_SCALE
            dp_all = _dot_nt(do4, v_ops[kvh])
            ds_sides, p_sides = [], []
            for side in range(2):
                heads = [2 * (kvh * PAIRS_PER_KV + pp) + side for pp in range(PAIRS_PER_KV)]
                mine = lo if side == 0 else jnp.logical_not(lo)
                cols = slice(side * BAND, (side + 1) * BAND)
                sink = _rows_per_pair([col + sink_ref[0, h] for h in heads])
                lse = _rows_per_pair([lse_ref[:, h:h + 1] for h in heads])
                delta = jnp.sum(jnp.where(mine, dd, 0.0), axis=-1, keepdims=True)
                p = jnp.where(valid, jnp.exp(jnp.where(valid, s_all[:, cols], MASK_VALUE) - lse), 0.0)
                ds_sides.append((p * (dp_all[:, cols] - delta) * ATTN_SCALE).astype(BF16))
                p_sides.append(p.astype(BF16))
                sink_part = jnp.exp(sink - lse) * delta
                for pp, h in enumerate(heads):
                    dsink = dsink + jnp.where(sink_lane == h, -jnp.sum(sink_part[pp * BLOCK:(pp + 1) * BLOCK]), 0.0)
            ds_all = jnp.concatenate(ds_sides, axis=1)
            dq4 = _dot_nn(ds_all, k_ops[kvh])
            for pp in range(PAIRS_PER_KV):
                dq_ref[:, _pair_lanes(kvh, pp)] = dq4[pp * BLOCK:(pp + 1) * BLOCK].astype(BF16)
            dk_heads.append(fold(_dot_tn(ds_all, q4)))
            dv_heads.append(fold(_dot_tn(jnp.concatenate(p_sides, axis=1), do4)))
        dk = dk_heads[0] + pltpu.roll(dk_heads[1], HEAD_DIM, 1)
        dv = dv_heads[0] + pltpu.roll(dv_heads[1], HEAD_DIM, 1)
        dkp_ref[...] = dk[:BLOCK]
        dko_ref[...] = dk[BLOCK:]
        dvp_ref[...] = dv[:BLOCK]
        dvo_ref[...] = dv[BLOCK:]
        dsink_ref[...] = dsink

    def own(b, n):
        return (b * nblk + n, 0)

    row = pl.BlockSpec((BLOCK, D_MODEL), own)
    kv = pl.BlockSpec((BLOCK, 2 * HEAD_DIM), own)
    kv_shape = jax.ShapeDtypeStruct((T, 2 * HEAD_DIM), F32)
    return _call(
        body, name="attn_bwd", grid=(batch, nblk),
        in_specs=[SMEM_SPEC] + _attn_specs(nblk) + [row, row, pl.BlockSpec((BLOCK, N_Q_HEADS), own)],
        out_specs=[row, kv, kv, kv, kv, pl.BlockSpec((None, 1, 2 * HEAD_DIM), lambda b, n: (b * nblk + n, 0, 0))],
        out_shape=[jax.ShapeDtypeStruct((T, D_MODEL), BF16), kv_shape, kv_shape, kv_shape, kv_shape,
                   jax.ShapeDtypeStruct((n_steps, 1, 2 * HEAD_DIM), F32)],
        operands=(sinks, proj, proj, proj, proj, proj, o, do, lse), comm=comm)


def _conv_bwd(proj, dz, ydw, w_dw, ln_g, ln_b, batch, seq, comm=None):
    T = proj.shape[0]
    ts = _tile(seq, 256)
    nt = seq // ts
    per_tile = ts // CONV_PAD
    shift = CONV_PAD - (CONV_WIDTH - 1)

    def body(ca_ref, cb_ref, cah_ref, cbh_ref, dz_ref, dzn_ref, y_ref, yn_ref, w_ref, g_ref, beta_ref,
             dca_ref, dcb_ref, dw_ref, db_ref, dg_ref, dbeta_ref, upad, dypad):
        t = pl.program_id(1)
        first = (pl.program_id(0) == 0) & (t == 0)
        gv = g_ref[...]

        def ln_bwd(dzv, yv):
            lnh, rstd = _layernorm_stats(yv)
            ln = lnh * gv + beta_ref[...]
            sg = _sigmoid(ln)
            dln = dzv.astype(F32) * (sg * (1.0 + ln * (1.0 - sg)))
            dyh = dln * gv
            dy = rstd * (dyh - jnp.mean(dyh, axis=-1, keepdims=True)
                         - lnh * jnp.mean(dyh * lnh, axis=-1, keepdims=True))
            return dy, dln, lnh

        dy, dln, lnh = ln_bwd(dz_ref[...], y_ref[...])
        dy_next, _, _ = ln_bwd(dzn_ref[...], yn_ref[...])
        dypad[0, 0:ts, :] = dy
        dypad[0, ts:, :] = jnp.where(t < nt - 1, dy_next, jnp.zeros_like(dy_next))
        _fill_shifted(dypad)
        _fill_upad(upad, ca_ref, cb_ref, cah_ref, cbh_ref, t)
        _fill_shifted(upad)

        _accumulate(dg_ref, first, jnp.sum(dln * lnh, axis=0, keepdims=True))
        _accumulate(dbeta_ref, first, jnp.sum(dln, axis=0, keepdims=True))
        _accumulate(db_ref, first, jnp.sum(dy, axis=0, keepdims=True))

        @pl.when(first)
        def _():
            dw_ref[...] = jnp.zeros_like(dw_ref)

        du = jnp.zeros((ts, D_MODEL), F32)
        for k in range(CONV_WIDTH):
            du = du + w_ref[k:k + 1, :] * _shifted_rows(dypad, CONV_WIDTH - 1 - k, ts)
            dw_ref[k:k + 1, :] += jnp.sum(dy * _shifted_rows(upad, shift + k, ts), axis=0, keepdims=True)
        cav = ca_ref[...].astype(F32)
        sb = _sigmoid(cb_ref[...].astype(F32))
        dca_ref[...] = (du * sb).astype(BF16)
        dcb_ref[...] = (du * cav * (sb * (1.0 - sb))).astype(BF16)

    def tile(b, t):
        return (b * nt + t, 0)

    def after(b, t):
        return (jnp.minimum((b * nt + t + 1) * per_tile, T // CONV_PAD - 1), 0)

    row = pl.BlockSpec((ts, D_MODEL), tile)
    halo = pl.BlockSpec((CONV_PAD, D_MODEL), after)
    vec = pl.BlockSpec((1, D_MODEL), lambda b, t: (0, 0))
    wspec = pl.BlockSpec((CONV_PAD, D_MODEL), lambda b, t: (0, 0))
    act = jax.ShapeDtypeStruct((T, D_MODEL), BF16)
    vec_shape = jax.ShapeDtypeStruct((1, D_MODEL), F32)
    return _call(
        body, name="conv_bwd", grid=(batch, nt),
        in_specs=_conv_specs(ts, nt) + [row, halo, row, halo, wspec, vec, vec],
        out_specs=[row, row, wspec, vec, vec, vec],
        out_shape=[act, act, jax.ShapeDtypeStruct((CONV_PAD, D_MODEL), F32), vec_shape, vec_shape, vec_shape],
        scratch_shapes=[pltpu.VMEM((SUBLANES, ts + CONV_PAD, D_MODEL), F32)] * 2,
        operands=(proj, proj, proj, proj, dz, dz, ydw, ydw, w_dw, ln_g, ln_b), comm=comm)


def _in_proj_bwd(pieces, w_cols, x, gn, sc, dxo, seq, comm=None):
    T, D = x.shape
    B = T // seq
    wide, narrow = list(pieces[:-1]), pieces[-1]
    P = len(wide)
    nw = narrow.shape[1]
    tm = _tile(seq, 512)
    nb = seq // tm

    def body(*refs):
        wide_refs = refs[:P]
        kv_ref, w_ref, wkv_ref, x_ref, gn_ref, sc_ref, dxo_ref, dx_ref, dsc_ref, dsh_ref, dgn_ref, acc = refs[P:]
        i = pl.program_id(0)
        j = pl.program_id(1)

        @pl.when(j == 0)
        def _():
            acc[...] = _dot_nn(kv_ref[...], wkv_ref[...])

        for p in range(P):
            @pl.when(j == p)
            def _(p=p):
                acc[...] += _dot_nn(wide_refs[p][...], w_ref[...])

        @pl.when(j == P - 1)
        def _():
            _norm_mod_bwd(acc[...], x_ref[...], gn_ref[...], sc_ref[...], dxo_ref[...],
                          i % nb == 0, i == 0, dx_ref, dsc_ref, dsh_ref, dgn_ref)

    row = pl.BlockSpec((tm, D), lambda i, j: (i, 0))
    vec = pl.BlockSpec((1, D), lambda i, j: (0, 0))
    per_b = pl.BlockSpec((None, 1, D), lambda i, j: (i // nb, 0, 0))
    per_b_shape = jax.ShapeDtypeStruct((B, 1, D), F32)
    return _call(
        body, name="mix_in_proj_bwd", grid=(T // tm, P),
        in_specs=[row] * P + [pl.BlockSpec((tm, nw), lambda i, j: (i, 0)),
                              pl.BlockSpec((D, D), lambda i, j: (j, 0)),
                              pl.BlockSpec((nw, D), lambda i, j: (P * D // nw, 0)), row, vec, per_b, row],
        out_specs=[row, per_b, per_b, vec],
        out_shape=[jax.ShapeDtypeStruct((T, D), F32), per_b_shape, per_b_shape, jax.ShapeDtypeStruct((1, D), F32)],
        scratch_shapes=[pltpu.VMEM((tm, D), F32)],
        operands=(*wide, narrow, w_cols, w_cols, x, gn, sc, dxo), comm=comm)


def _wgrad_rows(piece, h, out32, out16, row_offset, name):
    T, n = piece.shape
    C = h.shape[1]
    tk = _tile(T, 1024)
    nk = T // tk

    def body(a_ref, b_ref, in32, in16, o32_ref, o16_ref, acc, stage16, sems):
        k = pl.program_id(0)

        @pl.when(k == 0)
        def _():
            acc[...] = jnp.zeros_like(acc)

        acc[...] += _dot_tn(a_ref[...], b_ref[...])

        @pl.when(k == nk - 1)
        def _():
            stage16[...] = acc[...].astype(BF16)
            rows = pl.ds(row_offset, n)
            copies = [pltpu.make_async_copy(acc, o32_ref.at[rows, :], sems.at[0]),
                      pltpu.make_async_copy(stage16, o16_ref.at[rows, :], sems.at[1])]
            for cp in copies:
                cp.start()
            for cp in copies:
                cp.wait()

    return pl.pallas_call(
        body, name=name, grid=(nk,),
        in_specs=[pl.BlockSpec((tk, n), lambda k: (k, 0)), pl.BlockSpec((tk, C), lambda k: (k, 0)), ANY, ANY],
        out_specs=[ANY, ANY], out_shape=[jax.ShapeDtypeStruct(out32.shape, F32), jax.ShapeDtypeStruct(out16.shape, BF16)],
        scratch_shapes=[pltpu.VMEM((n, C), F32), pltpu.VMEM((n, C), BF16), pltpu.SemaphoreType.DMA((2,))],
        input_output_aliases={2: 0, 3: 1}, compiler_params=_params(1),
    )(piece, h, out32, out16)


def _ada_fwd(c_all, w_ada, b_cols):
    nbatch, D = c_all.shape
    N = w_ada.shape[1]
    tn = _tile(N, 768)

    def body(c_ref, w_ref, b_ref, o_ref):
        cv = c_ref[...]
        act = (cv * _sigmoid(cv)).astype(BF16)
        o_ref[...] = _dot_nn(act, w_ref[...].astype(BF16)) + b_ref[...]

    return pl.pallas_call(
        body, name="ada_fwd", grid=(N // tn,),
        in_specs=[pl.BlockSpec((nbatch, D), lambda j: (0, 0)), pl.BlockSpec((D, tn), lambda j: (0, j)),
                  pl.BlockSpec((1, tn), lambda j: (0, j))],
        out_specs=pl.BlockSpec((nbatch, tn), lambda j: (0, j)),
        out_shape=jax.ShapeDtypeStruct((nbatch, N), F32),
        compiler_params=_params(1),
    )(c_all, w_ada, b_cols)


def _adamw(w, g, m, v):
    m = ADAM_B1 * m + (1.0 - ADAM_B1) * g
    v = ADAM_B2 * v + (1.0 - ADAM_B2) * (g * g)
    m_hat = m / (1.0 - ADAM_B1 ** ADAM_STEP)
    v_hat = v / (1.0 - ADAM_B2 ** ADAM_STEP)
    delta = -ADAM_LR * (m_hat / (jnp.sqrt(v_hat) + ADAM_EPS) + ADAM_WD * w)
    return delta, m, v


def _adam_call(w, g, m, v, name, comm=None):
    R, C = w.shape
    tr = _row_tile(R, 512)

    def body(w_ref, g_ref, m_ref, v_ref, d_ref, mo_ref, vo_ref):
        d, mn, vn = _adamw(w_ref[...], g_ref[...], m_ref[...], v_ref[...])
        d_ref[...] = d
        mo_ref[...] = mn
        vo_ref[...] = vn

    blk = pl.BlockSpec((tr, C), lambda i: (i, 0))
    shape = jax.ShapeDtypeStruct((R, C), F32)
    return _call(body, name=name, grid=(R // tr,), in_specs=[blk] * 4, out_specs=[blk] * 3, out_shape=[shape] * 3,
                 operands=(w, g, m, v), comm=comm)


ADAM_GROUP_STEPS = 8


def _adam_group(ws, gs, ms, vs, name, comm=None):
    n = len(ws)

    def body(*refs):
        ins, outs = refs[:4 * n], refs[4 * n:]
        for i in range(n):
            d, mn, vn = _adamw(*(r[...] for r in ins[4 * i:4 * i + 4]))
            outs[3 * i][...] = d
            outs[3 * i + 1][...] = mn
            outs[3 * i + 2][...] = vn

    operands, in_specs, out_specs, out_shape = [], [], [], []
    for w, g, m, v in zip(ws, gs, ms, vs):
        R, C = w.shape
        blk = pl.BlockSpec((R // ADAM_GROUP_STEPS, C), lambda i: (i, 0))
        operands += [w, g, m, v]
        in_specs += [blk] * 4
        out_specs += [blk] * 3
        out_shape += [jax.ShapeDtypeStruct((R, C), F32)] * 3
    outs, comm_outs = _call(body, name=name, grid=(ADAM_GROUP_STEPS,), in_specs=in_specs, out_specs=out_specs,
                            out_shape=out_shape, operands=operands, comm=comm)
    return [tuple(outs[3 * i:3 * i + 3]) for i in range(n)], comm_outs


def _ada_adam(c_act_t, dmod_cols, w, m, v, comm):
    R, C = w.shape
    nbatch = c_act_t.shape[1]
    tr = _tile(R, 128)

    def body(ct_ref, dm_ref, w_ref, m_ref, v_ref, g_ref, d_ref, mo_ref, vo_ref):
        cv = ct_ref[...]
        g = _dot_nn((cv * _sigmoid(cv)).astype(BF16), dm_ref[...].astype(BF16))
        g_ref[...] = g
        d, mn, vn = _adamw(w_ref[...], g, m_ref[...], v_ref[...])
        d_ref[...] = d
        mo_ref[...] = mn
        vo_ref[...] = vn

    blk = pl.BlockSpec((tr, C), lambda i: (i, 0))
    shape = jax.ShapeDtypeStruct((R, C), F32)
    return _call(
        body, name="ada_adam", grid=(R // tr,),
        in_specs=[pl.BlockSpec((tr, nbatch), lambda i: (i, 0)), pl.BlockSpec((nbatch, C), lambda i: (0, 0)),
                  blk, blk, blk],
        out_specs=[blk] * 4, out_shape=[shape] * 4,
        operands=(c_act_t, dmod_cols, w, m, v), comm=comm)


def _small_adam(gathered, w, m, v, rows_b0, rows_b1, rows_vec):
    _, P, D = gathered.shape
    R = w.shape[0]

    def body(ga_ref, w_ref, m_ref, v_ref, sum_ref, g_ref, d_ref, mo_ref, vo_ref):
        total = ga_ref[0]
        for dev in range(1, N_DEV):
            total = total + ga_ref[dev]
        sum_ref[...] = total
        g_ref[...] = jnp.zeros_like(g_ref)
        g_ref[0:N_MOD, :] = (sum_ref[rows_b0:rows_b0 + N_MOD, :] + sum_ref[rows_b1:rows_b1 + N_MOD, :])
        g_ref[N_MOD:N_MOD + 8, :] = sum_ref[rows_vec:rows_vec + 8, :]
        d, mn, vn = _adamw(w_ref[...], g_ref[...], m_ref[...], v_ref[...])
        d_ref[...] = d
        mo_ref[...] = mn
        vo_ref[...] = vn

    shape = jax.ShapeDtypeStruct((R, D), F32)
    return pl.pallas_call(
        body, name="small_adam",
        in_specs=[VMEM_SPEC] * 4, out_specs=[VMEM_SPEC] * 5,
        out_shape=[jax.ShapeDtypeStruct((P, D), F32), shape, shape, shape, shape],
        compiler_params=pltpu.CompilerParams(vmem_limit_bytes=VMEM_LIMIT),
    )(gathered, w, m, v)


def _mod_exchange(part):
    _, A, W = part.shape

    def body(p_ref, out_ref, send_sems, recv_sems, local_sem):
        x, y, c = _position()
        me = 4 * x + 2 * y + c
        chip = 2 * x + y
        mine = pltpu.make_async_copy(p_ref.at[me], out_ref.at[chip], local_sem)
        mine.start()
        peers = [(_flip(x, fx), _flip(y, fy)) for fx, fy in CHIP_FLIPS]
        sends = []
        for k, (px, py) in enumerate(peers):
            sends.append(pltpu.make_async_remote_copy(
                src_ref=p_ref.at[4 * px + 2 * py + c], dst_ref=out_ref.at[chip], send_sem=send_sems.at[k],
                recv_sem=recv_sems.at[k], device_id=(px, py, c), device_id_type=MESH))
        for cp in sends:
            cp.start()
        for k, (px, py) in enumerate(peers):
            pltpu.make_async_remote_copy(
                src_ref=p_ref.at[me], dst_ref=out_ref.at[2 * px + py], send_sem=send_sems.at[k],
                recv_sem=recv_sems.at[k], device_id=(px, py, c), device_id_type=MESH).wait_recv()
        for cp in sends:
            cp.wait_send()
        mine.wait()

    return pl.pallas_call(
        body, name="mod_exchange", in_specs=[VMEM_SPEC], out_specs=VMEM_SPEC,
        out_shape=jax.ShapeDtypeStruct((N_CHIP, A, W), part.dtype),
        scratch_shapes=[pltpu.SemaphoreType.DMA((3,)), pltpu.SemaphoreType.DMA((3,)), pltpu.SemaphoreType.DMA],
    )(part)


KV_ROWS = 4 * HEAD_DIM


def _kernel_row_order(w_in_t):
    R, C = w_in_t.shape
    n_blocks = R // KV_ROWS
    q_blocks = D_MODEL // KV_ROWS

    def source(t):
        return jnp.where(t < q_blocks, t, jnp.where(t < n_blocks - 1, t + 1, q_blocks))

    def body(w_ref, o_ref):
        o_ref[...] = w_ref[...]

    return pl.pallas_call(
        body, name="w_in_row_order", grid=(n_blocks,),
        in_specs=[pl.BlockSpec((KV_ROWS, C), lambda t: (source(t), 0))],
        out_specs=pl.BlockSpec((KV_ROWS, C), lambda t: (t, 0)),
        out_shape=jax.ShapeDtypeStruct((R, C), w_in_t.dtype), compiler_params=_params(1),
    )(w_in_t)


def _cast_group(ws, names, comm):
    n = len(ws)
    steps = 4

    def body(*refs):
        w_refs, out_refs, stage, sem = refs[:n], refs[n:2 * n], refs[2 * n:3 * n], refs[3 * n]
        x, y, _ = _position()
        step = pl.program_id(0)
        copies = []
        for i in range(n):
            rows = ws[i].shape[0] // steps
            stage[i][...] = w_refs[i][...].astype(BF16)
            copies.append(pltpu.make_async_copy(
                stage[i], out_refs[i].at[2 * x + y, pl.ds(step * rows, rows), :], sem.at[i]))
        for cp in copies:
            cp.start()
        for cp in copies:
            cp.wait()

    outs, comm_outs = _call(
        body, name="cast_" + "_".join(names), grid=(steps,),
        in_specs=[pl.BlockSpec((w.shape[0] // steps, w.shape[1]), lambda i: (i, 0)) for w in ws],
        out_specs=[ANY] * n, out_shape=[jax.ShapeDtypeStruct((N_CHIP,) + w.shape, BF16) for w in ws],
        scratch_shapes=[pltpu.VMEM((w.shape[0] // steps, w.shape[1]), BF16) for w in ws]
        + [pltpu.SemaphoreType.DMA((n,))],
        operands=ws, comm=comm)
    return outs, comm_outs


def _cast_slot(w, chip_idx, name):
    R, C = w.shape
    tr = _row_tile(R, 512)

    def body(chip_ref, w_ref, o_ref):
        o_ref[...] = w_ref[...].astype(BF16)

    return pl.pallas_call(
        body, name=name,
        grid_spec=pltpu.PrefetchScalarGridSpec(
            num_scalar_prefetch=1, grid=(R // tr,),
            in_specs=[pl.BlockSpec((tr, C), lambda i, chip_ref: (i, 0))],
            out_specs=pl.BlockSpec((None, tr, C), lambda i, chip_ref: (chip_ref[0], i, 0))),
        out_shape=jax.ShapeDtypeStruct((N_CHIP, R, C), BF16),
        compiler_params=_params(1),
    )(chip_idx, w)


def _pair_sum(g32, recv, core, name):
    J, _, r, C = g32.shape

    def body(core_ref, g_ref, r_ref, o_ref):
        o_ref[...] = (g_ref[...] + r_ref[...].astype(F32)).astype(BF16)

    return pl.pallas_call(
        body, name=name,
        grid_spec=pltpu.PrefetchScalarGridSpec(
            num_scalar_prefetch=1, grid=(J,),
            in_specs=[pl.BlockSpec((None, None, r, C), lambda j, core_ref: (j, core_ref[0], 0, 0)),
                      pl.BlockSpec((None, r, C), lambda j, core_ref: (j, 0, 0))],
            out_specs=pl.BlockSpec((None, r, C), lambda j, core_ref: (j, 0, 0))),
        out_shape=jax.ShapeDtypeStruct((J, r, C), BF16),
        compiler_params=_params(1),
    )(core, g32, recv)


def _chip_sum(g32, recv_sib, recv_chips, core_chip, name):
    J, _, r, C = g32.shape

    def body(idx_ref, g_ref, s_ref, o_ref_in, o_ref):
        total = g_ref[...] + s_ref[...].astype(F32)
        for k in range(3):
            total = total + o_ref_in[k].astype(F32)
        o_ref[...] = total

    return pl.pallas_call(
        body, name=name,
        grid_spec=pltpu.PrefetchScalarGridSpec(
            num_scalar_prefetch=1, grid=(1,),
            in_specs=[pl.BlockSpec((None, None, r, C), lambda i, idx: (idx[1], idx[0], 0, 0)),
                      pl.BlockSpec((None, r, C), lambda i, idx: (idx[1], 0, 0)),
                      pl.BlockSpec((3, r, C), lambda i, idx: (0, 0, 0))],
            out_specs=pl.BlockSpec((None, r, C), lambda i, idx: (idx[0], 0, 0))),
        out_shape=jax.ShapeDtypeStruct((2, r, C), F32),
        compiler_params=_params(1),
    )(core_chip, g32, recv_sib, recv_chips)


ICI_US_PER_ELEMENT = 4.6e-5


class _Reducer:
    def __init__(self, core_idx, core_chip):
        self.core_idx, self.core_chip = core_idx, core_chip
        self.grads, self.halves, self.reduced = {}, {}, {}
        self.ready_swap, self.ready_exchange, self.ready_join = [], [], []
        self.inflight, self.current = ([], [], [], None), None
        self.flushes = 0
        self.extra, self.extra_out = None, None

    def add(self, name, grad_pair):
        self.grads[name] = grad_pair
        self.ready_swap.append(name)

    def comm(self, budget_us):
        swaps, self.ready_swap = self.ready_swap, []
        joins, self.ready_join = self.ready_join, []
        exchanges, waiting = [], []
        for item in self.ready_exchange:
            cost = ICI_US_PER_ELEMENT * 2 * item[2].shape[1] * item[2].shape[2]
            if cost <= budget_us:
                exchanges.append(item)
                budget_us -= cost
            else:
                waiting.append(item)
        self.ready_exchange = waiting
        parts = []
        if swaps:
            parts.append(_SwapComm([self.grads[n][1] for n in swaps]))
        if exchanges:
            parts.append(_ExchangeComm([pair for _, _, pair in exchanges]))
        if joins:
            parts.append(_JoinComm([self.halves[n] for n in joins]))
        extra, self.extra = self.extra, None
        if extra is not None:
            parts.append(extra)
        self.inflight = (swaps, exchanges, joins, extra)
        self.current = _CommList(parts) if parts else None
        return self.current

    def done(self, comm_outs):
        if self.current is None:
            return
        swaps, exchanges, joins, extra = self.inflight
        outs = iter(self.current.split_outputs(list(comm_outs)))
        if swaps:
            for n, recv in zip(swaps, next(outs)):
                pair = _pair_sum(self.grads[n][0], recv, self.core_idx, "pair_sum_" + n)
                self.ready_exchange.append((n, recv, pair))
        if exchanges:
            for (n, recv, _), chips in zip(exchanges, next(outs)):
                self.halves[n] = _chip_sum(self.grads[n][0], recv, chips, self.core_chip, "chip_sum_" + n)
                self.ready_join.append(n)
        if joins:
            self.reduced.update(zip(joins, next(outs)))
        if extra is not None:
            self.extra_out = next(outs)
        self.current = None

    def run(self, kernel, budget_us, *args, **kwargs):
        if budget_us is None:
            return kernel(*args, comm=None, **kwargs)[0]
        outs, comm_outs = kernel(*args, comm=self.comm(budget_us), **kwargs)
        self.done(comm_outs)
        return outs

    def step(self):
        comm = self.comm(float("inf"))
        self.flushes += 1
        self.done(_run_comm(comm, "grad_reduce_tail_%d" % self.flushes))


BIG_WEIGHTS = ("ffn1_w_gate", "ffn1_w_up", "ffn1_w_down", "w_in", "w_attn_o", "w_conv_o", "w_out",
               "ffn2_w_gate", "ffn2_w_up", "ffn2_w_down")
VECTORS = ("norm_ffn1_g", "norm_mix_g", "conv_b_dw", "conv_ln_g", "conv_ln_b", "norm_ffn2_g", "final_norm_g")
ROW_DMOD0, ROW_DMOD1, ROW_VEC, ROW_SINK, ROW_CONVW, SMALL_ROWS = 0, 16, 33, 40, 41, 72


FFN1_WEIGHTS = ("ffn1_w_gate", "ffn1_w_up", "ffn1_w_down")
FFN2_WEIGHTS = ("ffn2_w_gate", "ffn2_w_up", "ffn2_w_down")
MIX_WEIGHTS = ("w_in", "w_attn_o", "w_conv_o", "w_out")
COL_SHARDED = ("ffn1_w_gate", "ffn1_w_up", "ffn2_w_gate", "ffn2_w_up", "w_in")


def _local_grads(x, target, mod, slots, ffn1_gathered, small, seq, core_idx, core_chip):
    T, D = x.shape
    B = T // seq
    mods = [mod[:, k][:, None, :] for k in range(N_MOD)]
    sh1, sc1, g1, sh2, sc2, g2, sh3, sc3, g3 = mods
    w = dict(zip(FFN1_WEIGHTS, ffn1_gathered))

    (h1, a1, u1, f1, x1), outs = _ffn_fwd(
        x, small["norm_ffn1_g"], sc1, sh1, g1, w["ffn1_w_gate"], w["ffn1_w_up"], w["ffn1_w_down"], seq, "ffn1_fwd",
        comm=_GatherComm([slots[n] for n in MIX_WEIGHTS]))
    w["w_in"] = outs[0]
    w_ao, w_co, w_o = [t.reshape(D, D) for t in outs[1:]]
    w_in_cols = _kernel_row_order(w["w_in"].reshape(IN_WIDTH, D))
    (h2, proj), _ = _in_proj(x1, small["norm_mix_g"], sc2, sh2, w_in_cols, seq)
    (o, lse), ffn2_gathered = _attn_fwd(proj, small["attn_sinks"], B, seq,
                                        comm=_GatherComm([slots[n] for n in FFN2_WEIGHTS]))
    w.update(zip(FFN2_WEIGHTS, ffn2_gathered))
    (ydw, z), _ = _conv_fwd(proj, small["conv_w_dw"], small["conv_b_dw"], small["conv_ln_g"], small["conv_ln_b"],
                            B, seq)
    ya, yc, merged, mo, x2 = _merge(o, z, proj, w_ao, w_co, w_o, x1, g2, seq)
    (h3, a3, u3, f3, x3), _ = _ffn_fwd(x2, small["norm_ffn2_g"], sc3, sh3, g3, w["ffn2_w_gate"], w["ffn2_w_up"],
                                       w["ffn2_w_down"], seq, "ffn2_fwd")
    dx3, loss_parts, d_final_g = _final_loss(x3, small["final_norm_g"], target)

    red = _Reducer(core_idx, core_chip)

    def weight_grad(name, budget_us, a, a_spec, b, b_spec, rows, cols):
        red.add(name, red.run(_wgrad, budget_us, a, a_spec, b, b_spec, rows, cols, T, "dw_" + name))

    def ffn_backward(prefix, dw_budget_us, dxo, xin, h, a, u, f, gn, sc, gate, before_weight_grads=None):
        da, du, s, df, dx, dgate, dsc, dsh, dgn = red.run(
            _ffn_bwd, 170, dxo, xin, f, a, u, gn, sc, gate, w[prefix + "_w_gate"], w[prefix + "_w_up"],
            w[prefix + "_w_down"], seq, prefix + "_bwd")
        if before_weight_grads is not None:
            before_weight_grads(dgate, dsc, dsh, dgn)
        weight_grad(prefix + "_w_down", dw_budget_us, s, _spec_chip_major(FF_SHARD), df, _spec_rows(D), FF_SHARD, D)
        weight_grad(prefix + "_w_gate", dw_budget_us, da, _spec_chip_major(FF_SHARD), h, _spec_rows(D), FF_SHARD, D)
        weight_grad(prefix + "_w_up", dw_budget_us, du, _spec_chip_major(FF_SHARD), h, _spec_rows(D), FF_SHARD, D)
        return dx, dgate, dsc, dsh, dgn

    dx2, dg3, dsc3, dsh3, d_gn3 = ffn_backward("ffn2", None, dx3, x2, h3, a3, u3, f3, small["norm_ffn2_g"], sc3, g3)

    dmo, dya, dyc, dga, dgc, do, dz, dg2 = red.run(_merge_bwd, 45, dx2, mo, g2, proj, ya, yc, w_o, w_ao, w_co, seq)
    shard = D // N_CHIP
    weight_grad("w_out", None, merged, _spec_col_block(shard), dmo, _spec_rows(D), shard, D)
    weight_grad("w_attn_o", None, o, _spec_col_block(shard), dya, _spec_rows(D), shard, D)
    weight_grad("w_conv_o", None, z, _spec_col_block(shard), dyc, _spec_rows(D), shard, D)
    dq, dkp, dko, dvp, dvo, dsink_steps = red.run(_attn_bwd, 100, proj, small["attn_sinks"], o, do, lse, B, seq)
    dca, dcb, d_conv_w, d_conv_b, d_ln_g, d_ln_b = red.run(
        _conv_bwd, 165, proj, dz, ydw, small["conv_w_dw"], small["conv_ln_g"], small["conv_ln_b"], B, seq)

    def band_sum(own, prev):
        prev = prev.reshape(B, seq // BLOCK, BLOCK, 2 * HEAD_DIM)
        moved = jnp.concatenate([prev[:, 1:], jnp.zeros_like(prev[:, :1])], axis=1)
        return (own + moved.reshape(T, 2 * HEAD_DIM)).astype(BF16)

    dkv = jnp.concatenate([band_sum(dko, dkp), band_sum(dvo, dvp)], axis=1)
    g32, g16 = lax.empty((IN_WIDTH, D), F32), lax.empty((IN_WIDTH, D), BF16)
    row_of = {"q": 0, "kv": D, "conv_a": D + 4 * HEAD_DIM, "conv_b": 2 * D + 4 * HEAD_DIM,
              "gate_a": 3 * D + 4 * HEAD_DIM, "gate_c": 4 * D + 4 * HEAD_DIM}
    for tag, piece in (("q", dq), ("kv", dkv), ("conv_a", dca), ("conv_b", dcb), ("gate_a", dga), ("gate_c", dgc)):
        g32, g16 = _wgrad_rows(piece, h2, g32, g16, row_of[tag], "dw_w_in_" + tag)
    red.add("w_in", tuple(g.reshape(N_CHIP, 2, IN_SHARD // 2, D) for g in (g32, g16)))
    dx1, dsc2, dsh2, d_gn2 = red.run(_in_proj_bwd, 90, (dq, dca, dcb, dga, dgc, dkv), w_in_cols, x1,
                                     small["norm_mix_g"], sc2, dx2, seq)

    def gather_small_grads(dg1, dsc1, dsh1, d_gn1):
        dmod = jnp.concatenate([dsh1, dsc1, dg1, dsh2, dsc2, dg2, dsh3, dsc3, dg3], axis=1)
        d_sinks = jnp.sum(dsink_steps, axis=0)
        vec_grads = {"norm_ffn1_g": d_gn1, "norm_mix_g": d_gn2, "conv_b_dw": d_conv_b, "conv_ln_g": d_ln_g,
                     "conv_ln_b": d_ln_b, "norm_ffn2_g": d_gn3, "final_norm_g": d_final_g}
        block = jnp.zeros((SMALL_ROWS, D), F32)
        block = block.at[ROW_DMOD0:ROW_DMOD0 + N_MOD].set(dmod[0]).at[ROW_DMOD1:ROW_DMOD1 + N_MOD].set(dmod[1])
        block = block.at[ROW_VEC:ROW_VEC + len(VECTORS)].set(jnp.concatenate([vec_grads[n] for n in VECTORS], axis=0))
        block = block.at[ROW_SINK, :2 * HEAD_DIM].set(d_sinks[0])
        block = block.at[ROW_CONVW:ROW_CONVW + CONV_WIDTH].set(d_conv_w[:CONV_WIDTH])
        red.extra = _Gather8Comm(block)

    dx0, _, _, _, _ = ffn_backward("ffn1", 38, dx1, x, h1, a1, u1, f1, small["norm_ffn1_g"], sc1, g1,
                                   before_weight_grads=gather_small_grads)
    return loss_parts, dx0, red, red.extra_out[0]


def kernel(x, c, w_ada, b_ada, norm_ffn1_g, ffn1_w_gate, ffn1_w_up, ffn1_w_down, norm_mix_g, w_in, attn_sinks, w_attn_o, conv_w_dw, conv_b_dw, conv_ln_g, conv_ln_b, w_conv_o, w_out, norm_ffn2_g, ffn2_w_gate, ffn2_w_up, ffn2_w_down, final_norm_g, loss_target, m_w_ada, m_b_ada, m_norm_ffn1_g, m_ffn1_w_gate, m_ffn1_w_up, m_ffn1_w_down, m_norm_mix_g, m_w_in, m_attn_sinks, m_w_attn_o, m_conv_w_dw, m_conv_b_dw, m_conv_ln_g, m_conv_ln_b, m_w_conv_o, m_w_out, m_norm_ffn2_g, m_ffn2_w_gate, m_ffn2_w_up, m_ffn2_w_down, m_final_norm_g, v_w_ada, v_b_ada, v_norm_ffn1_g, v_ffn1_w_gate, v_ffn1_w_up, v_ffn1_w_down, v_norm_mix_g, v_w_in, v_attn_sinks, v_w_attn_o, v_conv_w_dw, v_conv_b_dw, v_conv_ln_g, v_conv_ln_b, v_w_conv_o, v_w_out, v_norm_ffn2_g, v_ffn2_w_gate, v_ffn2_w_up, v_ffn2_w_down, v_final_norm_g):
    args = dict(locals())
    B, seq, D = x.shape
    T = B * seq
    xi, yi, ci = _position()
    chip = 2 * xi + yi
    dev = 4 * xi + 2 * yi + ci

    def shard_2d(prefix, name):
        t = args[prefix + name][0]
        return t.T if name in COL_SHARDED else t

    big = {n: shard_2d("", n) for n in BIG_WEIGHTS}
    final_g = final_norm_g[None, :]
    vec_w = {n: (args[n] if n != "final_norm_g" else final_g) for n in VECTORS}

    core_idx = jnp.reshape(ci, (1,)).astype(jnp.int32)
    chip_idx = jnp.reshape(chip, (1,)).astype(jnp.int32)
    core_chip = jnp.stack([ci, chip]).astype(jnp.int32)
    conv_cols = D // N_CHIP
    conv_flat = jnp.pad(conv_w_dw[0].reshape(-1), (0, 8 * D - CONV_WIDTH * conv_cols)).reshape(8, D)
    first_block = jnp.concatenate([jnp.pad(c, ((0, 8 - B), (0, 0))), conv_flat], axis=0)
    slots = {n: _cast_slot(big[n], chip_idx, "cast_" + n) for n in FFN1_WEIGHTS}
    later = [n for n in BIG_WEIGHTS if n not in FFN1_WEIGHTS]
    carried = _CommList([_GatherComm([slots[n] for n in FFN1_WEIGHTS]), _Gather8Comm(first_block)])
    later_slots, carried_outs = _cast_group([big[n] for n in later], ["later_weights"], carried)
    ffn1_gathered, (first,) = carried.split_outputs(carried_outs)
    slots.update(zip(later, later_slots))
    c_all = first[:, :B].reshape(N_DEV * B, D)
    conv_taps = first[::2, 8:].reshape(N_CHIP, 8 * D)[:, :CONV_WIDTH * conv_cols]
    conv_taps = conv_taps.reshape(N_CHIP, CONV_WIDTH, conv_cols).transpose(1, 0, 2).reshape(CONV_WIDTH, D)
    conv_taps = jnp.pad(conv_taps, ((0, CONV_PAD - CONV_WIDTH), (0, 0)))

    ada_cols = w_ada.shape[2]
    b_cols = lax.dynamic_slice(b_ada, (0, chip * ada_cols), (1, ada_cols))
    mod_part = _ada_fwd(c_all, w_ada[0], b_cols).reshape(N_DEV, B, ada_cols)
    mod = _mod_exchange(mod_part).transpose(1, 0, 2).reshape(B, N_MOD, D)

    small = dict(vec_w)
    small["attn_sinks"] = attn_sinks
    small["conv_w_dw"] = conv_taps

    loss_parts, dx, red, small_all = _local_grads(
        x.reshape(T, D), loss_target.reshape(T, D), mod, slots, ffn1_gathered, small, seq, core_idx, core_chip)

    loss = lax.psum((0.5 / D) * jnp.sum(loss_parts), ("x", "y", "c"))
    grad_x = dx.reshape(B, seq, D)
    out = {}


    def pack_small(prefix):
        rows = [args[prefix + "b_ada"].reshape(N_MOD, D)]
        rows += [args[prefix + n].reshape(1, D) for n in VECTORS]
        rows += [jnp.pad(args[prefix + "attn_sinks"], ((0, 0), (0, D - N_Q_HEADS)))]
        return jnp.pad(jnp.concatenate(rows, axis=0), ((0, 24 - N_MOD - len(VECTORS) - 1), (0, 0)))

    small_sum, sg, sd, sm, sv = _small_adam(small_all, pack_small(""), pack_small("m_"), pack_small("v_"),
                                           ROW_DMOD0, ROW_DMOD1, ROW_VEC)

    def unpack_small(t):
        res = {"b_ada": t[:N_MOD].reshape(1, N_MOD * D)}
        for k, n in enumerate(VECTORS):
            res[n] = t[N_MOD + k].reshape(args[n].shape)
        res["attn_sinks"] = t[N_MOD + len(VECTORS), :N_Q_HEADS].reshape(1, N_Q_HEADS)
        return res

    unpacked = [unpack_small(t) for t in (sg, sd, sm, sv)]
    for n in ("b_ada", "attn_sinks") + VECTORS:
        out[n] = tuple(u[n] for u in unpacked)

    conv_g = lax.dynamic_slice(small_sum, (ROW_CONVW, chip * conv_cols), (CONV_WIDTH, conv_cols))
    d, mn, vn = red.run(_adam_call, None, conv_w_dw[0], conv_g, m_conv_w_dw[0], v_conv_w_dw[0], "adam_conv_w_dw")
    out["conv_w_dw"] = tuple(t[None] for t in (conv_g, d, mn, vn))

    dmod_rows = jnp.stack([small_all[:, ROW_DMOD0:ROW_DMOD0 + N_MOD], small_all[:, ROW_DMOD1:ROW_DMOD1 + N_MOD]], axis=1)
    dmod_all = dmod_rows.reshape(N_DEV * B, N_MOD * D)
    dmod_cols = lax.dynamic_slice(dmod_all, (0, chip * ada_cols), (N_DEV * B, ada_cols))
    ada_out = red.run(_ada_adam, 35, c_all.T, dmod_cols, w_ada[0], m_w_ada[0], v_w_ada[0])
    out["w_ada"] = tuple(t[None] for t in ada_out)

    def finished(n):
        while n not in red.reduced:
            red.step()
        return red.reduced[n].reshape(big[n].shape)

    def emit(n, g, d, mn, vn):
        out[n] = tuple((t.T if n in COL_SHARDED else t)[None] for t in (g, d, mn, vn))

    early = FFN2_WEIGHTS + MIX_WEIGHTS
    early_g = [finished(n) for n in early]
    early_out = red.run(_adam_group, 45, [big[n] for n in early], early_g, [shard_2d("m_", n) for n in early],
                        [shard_2d("v_", n) for n in early], "adam_early")
    for n, g, (d, mn, vn) in zip(early, early_g, early_out):
        emit(n, g, d, mn, vn)
    for n in ("ffn1_w_down", "ffn1_w_gate", "ffn1_w_up"):
        g = finished(n)
        emit(n, g, *red.run(_adam_call, None, big[n], g, shard_2d("m_", n), shard_2d("v_", n), "adam_" + n))

    order = ("w_ada", "b_ada", "norm_ffn1_g", "ffn1_w_gate", "ffn1_w_up", "ffn1_w_down", "norm_mix_g", "w_in",
             "attn_sinks", "w_attn_o", "conv_w_dw", "conv_b_dw", "conv_ln_g", "conv_ln_b", "w_conv_o", "w_out",
             "norm_ffn2_g", "ffn2_w_gate", "ffn2_w_up", "ffn2_w_down", "final_norm_g")
    return (loss, grad_x, *[out[n][0] for n in order], *[out[n][1] for n in order],
            *[out[n][2] for n in order], *[out[n][3] for n in order])
```

```python
import functools

import jax
import jax.numpy as jnp
from jax import lax
from jax.experimental import pallas as pl
from jax.experimental.pallas import tpu as pltpu

F32 = jnp.float32
BF16 = jnp.bfloat16

D_MODEL = 1024
D_FF = 2816
N_CHIP = 4
N_DEV = 8
FF_SHARD = D_FF // N_CHIP
IN_WIDTH = 5376
IN_SHARD = IN_WIDTH // N_CHIP
HEAD_DIM = 64
N_Q_HEADS = 16
N_KV_HEADS = 2
BLOCK = 128
CONV_WIDTH = 31
CONV_PAD = 32
N_MOD = 9
EPS = 1e-6
FFN_RESIDUAL = 0.5
ATTN_SCALE = HEAD_DIM ** -0.5
MASK_VALUE = -1e30

ADAM_LR = 0.001
ADAM_B1 = 0.9
ADAM_B2 = 0.999
ADAM_EPS = 1e-08
ADAM_WD = 0.01
ADAM_STEP = 10

COLB_Q, COLB_CA, COLB_CB, COLB_GA, COLB_GC = 0, 1, 2, 3, 4
COLB_K, COLB_V = 40, 41
PROJ_TILE = 768

VMEM_LIMIT = 56 * 1024 * 1024
MESH = pl.DeviceIdType.MESH
ANY = pl.BlockSpec(memory_space=pl.ANY)
VMEM_SPEC = pl.BlockSpec(memory_space=pltpu.VMEM)
SMEM_SPEC = pl.BlockSpec(memory_space=pltpu.SMEM)


def _params(n_grid):
    return pltpu.CompilerParams(dimension_semantics=("arbitrary",) * n_grid, vmem_limit_bytes=VMEM_LIMIT)


def _tile(n, pref):
    t = min(n, pref)
    while n % t:
        t //= 2
    return t


def _row_tile(rows, cap):
    for t in range(min(rows, cap) // 16 * 16, 0, -16):
        if rows % t == 0:
            return t
    return rows


def _sigmoid(v):
    return 1.0 / (1.0 + jnp.exp(-v))


def _dot_nn(a, b):
    return lax.dot_general(a, b, (((1,), (0,)), ((), ())), preferred_element_type=F32)


def _dot_nt(a, b):
    return lax.dot_general(a, b, (((1,), (1,)), ((), ())), preferred_element_type=F32)


def _dot_tn(a, b):
    return lax.dot_general(a, b, (((0,), (0,)), ((), ())), preferred_element_type=F32)


ROW_CHUNK = 16


def _for_row_chunks(n_rows, fn):
    for r in range(0, n_rows, ROW_CHUNK):
        fn(slice(r, r + ROW_CHUNK))


def _norm_mod(xv, gn, sc, sh):
    r = lax.rsqrt(jnp.mean(xv * xv, axis=-1, keepdims=True) + EPS)
    return ((xv * r) * gn) * (1.0 + sc) + sh


def _accumulate(ref, first, value):
    @pl.when(first)
    def _():
        ref[...] = value

    @pl.when(jnp.logical_not(first))
    def _():
        ref[...] += value


def _norm_mod_bwd(dh, xv, gn, sc, dxo, first_of_batch, first, dx_ref, dsc_ref, dsh_ref, dgn_ref):
    r = lax.rsqrt(jnp.mean(xv * xv, axis=-1, keepdims=True) + EPS)
    xh = xv * r
    _accumulate(dsh_ref, first_of_batch, jnp.sum(dh, axis=0, keepdims=True))
    _accumulate(dsc_ref, first_of_batch, jnp.sum(dh * (xh * gn), axis=0, keepdims=True))
    dn = dh * (1.0 + sc)
    _accumulate(dgn_ref, first, jnp.sum(dn * xh, axis=0, keepdims=True))
    dxh = dn * gn
    dx_ref[...] = dxo + r * (dxh - xh * jnp.mean(dxh * xh, axis=-1, keepdims=True))


CHIP_FLIPS = ((1, 0), (0, 1), (1, 1))


def _position():
    return lax.axis_index("x"), lax.axis_index("y"), lax.axis_index("c")


def _flip(v, f):
    return 1 - v if f else v


class _GatherComm:
    def __init__(self, bufs):
        n = len(bufs)
        self.n = n
        self.operands = list(bufs)
        self.out_shape = [jax.ShapeDtypeStruct(b.shape, b.dtype) for b in bufs]
        self.aliases = {i: i for i in range(n)}
        self.sems = [pltpu.SemaphoreType.DMA((6 * n,)), pltpu.SemaphoreType.DMA((6 * n,))]
        self.rows = [b.shape[1] // 2 for b in bufs]

    def _half(self, ref, i, which):
        return ref.at[pl.ds(which * self.rows[i], self.rows[i]), :]

    def _ici(self, cins, couts, sems, i, k, dst_chip, to):
        x, y, c = _position()
        return pltpu.make_async_remote_copy(
            src_ref=self._half(cins[i].at[2 * x + y], i, c), dst_ref=self._half(couts[i].at[dst_chip], i, c),
            send_sem=sems[0].at[3 * i + k], recv_sem=sems[1].at[3 * i + k], device_id=to, device_id_type=MESH)

    def _d2d(self, couts, sems, i, k, src_chip, which):
        x, y, c = _position()
        place = self._half(couts[i].at[src_chip], i, which)
        return pltpu.make_async_remote_copy(
            src_ref=place, dst_ref=place, send_sem=sems[0].at[3 * self.n + 3 * i + k],
            recv_sem=sems[1].at[3 * self.n + 3 * i + k], device_id=(x, y, 1 - c), device_id_type=MESH)

    def _peers(self):
        x, y, _ = _position()
        return [(_flip(x, fx), _flip(y, fy)) for fx, fy in CHIP_FLIPS]

    def start(self, cins, couts, sems):
        x, y, c = _position()
        for i in range(self.n):
            for k, (px, py) in enumerate(self._peers()):
                self._ici(cins, couts, sems, i, k, 2 * x + y, (px, py, c)).start()

    def finish(self, cins, couts, sems):
        _, _, c = _position()
        peers = self._peers()
        for i in range(self.n):
            for k, (px, py) in enumerate(peers):
                self._ici(cins, couts, sems, i, k, 2 * px + py, (px, py, c)).wait_recv()
                self._d2d(couts, sems, i, k, 2 * px + py, c).start()
        for i in range(self.n):
            for k, (px, py) in enumerate(peers):
                self._d2d(couts, sems, i, k, 2 * px + py, 1 - c).wait_recv()
        for i in range(self.n):
            for k, (px, py) in enumerate(peers):
                self._ici(cins, couts, sems, i, k, 2 * px + py, (px, py, c)).wait_send()
                self._d2d(couts, sems, i, k, 2 * px + py, c).wait_send()


class _ExchangeComm:
    def __init__(self, pairs):
        n = len(pairs)
        self.n = n
        self.operands = list(pairs)
        self.out_shape = [jax.ShapeDtypeStruct((3,) + p.shape[1:], p.dtype) for p in pairs]
        self.aliases = {}
        self.sems = [pltpu.SemaphoreType.DMA((3 * n,)), pltpu.SemaphoreType.DMA((3 * n,))]

    def _copies(self, cins, couts, sems):
        x, y, c = _position()
        peers = [(_flip(x, fx), _flip(y, fy)) for fx, fy in CHIP_FLIPS]
        return [pltpu.make_async_remote_copy(
            src_ref=cins[i].at[2 * px + py], dst_ref=couts[i].at[k], send_sem=sems[0].at[3 * i + k],
            recv_sem=sems[1].at[3 * i + k], device_id=(px, py, c), device_id_type=MESH)
            for i in range(self.n) for k, (px, py) in enumerate(peers)]

    def start(self, cins, couts, sems):
        for cp in self._copies(cins, couts, sems):
            cp.start()

    def finish(self, cins, couts, sems):
        for cp in self._copies(cins, couts, sems):
            cp.wait()


class _SwapComm:
    def __init__(self, grads16):
        n = len(grads16)
        self.n = n
        self.operands = list(grads16)
        self.out_shape = [jax.ShapeDtypeStruct(g.shape[:1] + g.shape[2:], g.dtype) for g in grads16]
        self.aliases = {}
        self.sems = [pltpu.SemaphoreType.DMA((n,)), pltpu.SemaphoreType.DMA((n,))]

    def _copies(self, cins, couts, sems):
        x, y, c = _position()
        return [pltpu.make_async_remote_copy(
            src_ref=cins[i].at[:, 1 - c], dst_ref=couts[i], send_sem=sems[0].at[i], recv_sem=sems[1].at[i],
            device_id=(x, y, 1 - c), device_id_type=MESH) for i in range(self.n)]

    def start(self, cins, couts, sems):
        for cp in self._copies(cins, couts, sems):
            cp.start()

    def finish(self, cins, couts, sems):
        for cp in self._copies(cins, couts, sems):
            cp.wait()


class _JoinComm:
    def __init__(self, halves):
        n = len(halves)
        self.n = n
        self.operands = list(halves)
        self.out_shape = [jax.ShapeDtypeStruct(h.shape, h.dtype) for h in halves]
        self.aliases = {i: i for i in range(n)}
        self.sems = [pltpu.SemaphoreType.DMA((n,)), pltpu.SemaphoreType.DMA((n,))]

    def _copy(self, cins, couts, sems, i, which):
        x, y, c = _position()
        return pltpu.make_async_remote_copy(
            src_ref=cins[i].at[which], dst_ref=couts[i].at[which], send_sem=sems[0].at[i], recv_sem=sems[1].at[i],
            device_id=(x, y, 1 - c), device_id_type=MESH)

    def start(self, cins, couts, sems):
        _, _, c = _position()
        for i in range(self.n):
            self._copy(cins, couts, sems, i, c).start()

    def finish(self, cins, couts, sems):
        _, _, c = _position()
        for i in range(self.n):
            self._copy(cins, couts, sems, i, 1 - c).wait_recv()
        for i in range(self.n):
            self._copy(cins, couts, sems, i, c).wait_send()


class _Gather8Comm:
    def __init__(self, block):
        self.operands = [block]
        self.out_shape = [jax.ShapeDtypeStruct((N_DEV,) + block.shape, block.dtype)]
        self.aliases = {}
        self.sems = [pltpu.SemaphoreType.DMA((N_DEV - 1,)), pltpu.SemaphoreType.DMA((N_DEV - 1,)),
                     pltpu.SemaphoreType.DMA]
        self.flips = [(fx, fy, fc) for fx in (0, 1) for fy in (0, 1) for fc in (0, 1) if (fx, fy, fc) != (0, 0, 0)]

    def _peers(self):
        x, y, c = _position()
        return [(_flip(x, fx), _flip(y, fy), _flip(c, fc)) for fx, fy, fc in self.flips]

    def _copy(self, cins, couts, sems, k, block, to):
        return pltpu.make_async_remote_copy(src_ref=cins[0], dst_ref=couts[0].at[block], send_sem=sems[0].at[k],
                                            recv_sem=sems[1].at[k], device_id=to, device_id_type=MESH)

    def _mine(self, cins, couts, sems):
        x, y, c = _position()
        return pltpu.make_async_copy(cins[0], couts[0].at[4 * x + 2 * y + c], sems[2])

    def start(self, cins, couts, sems):
        x, y, c = _position()
        self._mine(cins, couts, sems).start()
        for k, peer in enumerate(self._peers()):
            self._copy(cins, couts, sems, k, 4 * x + 2 * y + c, peer).start()

    def finish(self, cins, couts, sems):
        for k, (px, py, pc) in enumerate(self._peers()):
            self._copy(cins, couts, sems, k, 4 * px + 2 * py + pc, (px, py, pc)).wait_recv()
        for k, peer in enumerate(self._peers()):
            self._copy(cins, couts, sems, k, 0, peer).wait_send()
        self._mine(cins, couts, sems).wait()


class _CommList:
    def __init__(self, parts):
        self.parts = list(parts)
        self.operands = [t for p in self.parts for t in p.operands]
        self.out_shape = [t for p in self.parts for t in p.out_shape]
        self.sems = [t for p in self.parts for t in p.sems]
        self.aliases = {}
        n_in = n_out = 0
        for p in self.parts:
            self.aliases.update({n_in + i: n_out + j for i, j in p.aliases.items()})
            n_in += len(p.operands)
            n_out += len(p.out_shape)

    def _split(self, cins, couts, sems):
        pos = [0, 0, 0]
        for p in self.parts:
            sizes = (len(p.operands), len(p.out_shape), len(p.sems))
            yield p, tuple(seq[a:a + k] for seq, a, k in zip((cins, couts, sems), pos, sizes))
            pos = [a + k for a, k in zip(pos, sizes)]

    def start(self, cins, couts, sems):
        for p, refs in self._split(cins, couts, sems):
            p.start(*refs)

    def finish(self, cins, couts, sems):
        for p, refs in self._split(cins, couts, sems):
            p.finish(*refs)

    def split_outputs(self, outs):
        res, pos = [], 0
        for p in self.parts:
            res.append(outs[pos:pos + len(p.out_shape)])
            pos += len(p.out_shape)
        return res


def _call(body, *, name, grid, in_specs, out_specs, out_shape, operands, scratch_shapes=(), comm=None):
    n_grid = len(grid)
    if comm is None:
        return pl.pallas_call(
            body, name=name, grid=grid, in_specs=list(in_specs), out_specs=list(out_specs), out_shape=list(out_shape),
            scratch_shapes=list(scratch_shapes), compiler_params=_params(n_grid))(*operands), ()
    counts = (len(in_specs), len(comm.operands), len(out_specs), len(comm.out_shape), len(scratch_shapes),
              len(comm.sems))

    def fused(*refs):
        parts, pos = [], 0
        for k in counts:
            parts.append(refs[pos:pos + k])
            pos += k
        ins, cins, outs, couts, scr, sems = parts
        first = functools.reduce(jnp.logical_and, [pl.program_id(d) == 0 for d in range(n_grid)])
        last = functools.reduce(jnp.logical_and, [pl.program_id(d) == grid[d] - 1 for d in range(n_grid)])

        @pl.when(first)
        def _():
            comm.start(cins, couts, sems)

        body(*ins, *outs, *scr)

        @pl.when(last)
        def _():
            comm.finish(cins, couts, sems)

    res = pl.pallas_call(
        fused, name=name, grid=grid, in_specs=list(in_specs) + [ANY] * counts[1],
        out_specs=list(out_specs) + [ANY] * counts[3], out_shape=list(out_shape) + list(comm.out_shape),
        scratch_shapes=list(scratch_shapes) + list(comm.sems),
        input_output_aliases={counts[0] + i: counts[2] + j for i, j in comm.aliases.items()},
        compiler_params=_params(n_grid))(*operands, *comm.operands)
    return res[:counts[2]], res[counts[2]:]


def _run_comm(comm, name):
    k_in, k_out = len(comm.operands), len(comm.out_shape)

    def body(*refs):
        cins, couts, sems = refs[:k_in], refs[k_in:k_in + k_out], refs[k_in + k_out:]
        comm.start(cins, couts, sems)
        comm.finish(cins, couts, sems)

    return pl.pallas_call(
        body, name=name, in_specs=[ANY] * k_in, out_specs=[ANY] * k_out, out_shape=list(comm.out_shape),
        scratch_shapes=list(comm.sems), input_output_aliases=dict(comm.aliases))(*comm.operands)


def _ffn_fwd(x, gn, sc, sh, gate, wg, wu, wd, seq, name, comm=None):
    T, D = x.shape
    J, Fs, _ = wg.shape
    tm = _tile(seq, 1024)
    nb = seq // tm

    def body(x_ref, gn_ref, sc_ref, sh_ref, gate_ref, wg_ref, wu_ref, wd_ref,
             h_ref, a_ref, u_ref, f_ref, xo_ref, hs, acc, s16):
        j = pl.program_id(1)

        @pl.when(j == 0)
        def _():
            hb = _norm_mod(x_ref[...], gn_ref[...], sc_ref[...], sh_ref[...]).astype(BF16)
            hs[...] = hb
            h_ref[...] = hb
            acc[...] = jnp.zeros_like(acc)

        hb = hs[...]
        a_all = _dot_nt(hb, wg_ref[...])
        u_all = _dot_nt(hb, wu_ref[...])

        def swiglu_rows(rows):
            a = a_all[rows, :]
            u = u_all[rows, :]
            a_ref[rows, :] = a.astype(BF16)
            u_ref[rows, :] = u.astype(BF16)
            s16[rows, :] = ((a * _sigmoid(a)) * u).astype(BF16)

        _for_row_chunks(tm, swiglu_rows)
        acc[...] += _dot_nn(s16[...], wd_ref[...])

        @pl.when(j == J - 1)
        def _():
            f = acc[...]
            f_ref[...] = f.astype(BF16)
            xo_ref[...] = x_ref[...] + (FFN_RESIDUAL * gate_ref[...]) * f

    row = pl.BlockSpec((tm, D), lambda i, j: (i, 0))
    vec = pl.BlockSpec((1, D), lambda i, j: (0, 0))
    per_b = pl.BlockSpec((None, 1, D), lambda i, j: (i // nb, 0, 0))
    hid = pl.BlockSpec((None, tm, Fs), lambda i, j: (j, i, 0))
    return _call(
        body, name=name, grid=(T // tm, J),
        in_specs=[row, vec, per_b, per_b, per_b] + [pl.BlockSpec((None, Fs, D), lambda i, j: (j, 0, 0))] * 3,
        out_specs=[row, hid, hid, row, row],
        out_shape=[jax.ShapeDtypeStruct((T, D), BF16), jax.ShapeDtypeStruct((J, T, Fs), BF16),
                   jax.ShapeDtypeStruct((J, T, Fs), BF16), jax.ShapeDtypeStruct((T, D), BF16),
                   jax.ShapeDtypeStruct((T, D), F32)],
        scratch_shapes=[pltpu.VMEM((tm, D), BF16), pltpu.VMEM((tm, D), F32), pltpu.VMEM((tm, Fs), BF16)],
        operands=(x, gn, sc, sh, gate, wg, wu, wd), comm=comm)


def _ffn_bwd(dxo, x, f, a, u, gn, sc, gate, wg, wu, wd, seq, name, comm=None):
    T, D = x.shape
    J, Fs, _ = wg.shape
    B = T // seq
    tm = _tile(seq, 512)
    nb = seq // tm

    def body(dxo_ref, x_ref, f_ref, a_ref, u_ref, gn_ref, sc_ref, gate_ref, wg_ref, wu_ref, wd_ref,
             da_ref, du_ref, s_ref, df_ref, dx_ref, dgate_ref, dsc_ref, dsh_ref, dgn_ref, dfs, acc):
        i = pl.program_id(0)
        j = pl.program_id(1)
        first_of_batch = i % nb == 0

        @pl.when(j == 0)
        def _():
            dxo_v = dxo_ref[...]
            dfb = ((FFN_RESIDUAL * gate_ref[...]) * dxo_v).astype(BF16)
            dfs[...] = dfb
            df_ref[...] = dfb
            part = jnp.sum((FFN_RESIDUAL * f_ref[...].astype(F32)) * dxo_v, axis=0, keepdims=True)
            _accumulate(dgate_ref, first_of_batch, part)
            acc[...] = jnp.zeros_like(acc)

        ds_all = _dot_nt(dfs[...], wd_ref[...])

        def swiglu_bwd_rows(rows):
            ds = ds_all[rows, :]
            av = a_ref[rows, :].astype(F32)
            uv = u_ref[rows, :].astype(F32)
            sig = _sigmoid(av)
            sil = av * sig
            s_ref[rows, :] = (sil * uv).astype(BF16)
            da_ref[rows, :] = (ds * uv * (sig * (1.0 + av * (1.0 - sig)))).astype(BF16)
            du_ref[rows, :] = (ds * sil).astype(BF16)

        _for_row_chunks(tm, swiglu_bwd_rows)
        acc[...] += _dot_nn(da_ref[...], wg_ref[...]) + _dot_nn(du_ref[...], wu_ref[...])

        @pl.when(j == J - 1)
        def _():
            _norm_mod_bwd(acc[...], x_ref[...], gn_ref[...], sc_ref[...], dxo_ref[...],
                          first_of_batch, i == 0, dx_ref, dsc_ref, dsh_ref, dgn_ref)

    row = pl.BlockSpec((tm, D), lambda i, j: (i, 0))
    vec = pl.BlockSpec((1, D), lambda i, j: (0, 0))
    per_b = pl.BlockSpec((None, 1, D), lambda i, j: (i // nb, 0, 0))
    hid = pl.BlockSpec((None, tm, Fs), lambda i, j: (j, i, 0))
    hid_shape = jax.ShapeDtypeStruct((J, T, Fs), BF16)
    per_b_shape = jax.ShapeDtypeStruct((B, 1, D), F32)
    return _call(
        body, name=name, grid=(T // tm, J),
        in_specs=[row, row, row, hid, hid, vec, per_b, per_b]
        + [pl.BlockSpec((None, Fs, D), lambda i, j: (j, 0, 0))] * 3,
        out_specs=[hid, hid, hid, row, row, per_b, per_b, per_b, vec],
        out_shape=[hid_shape, hid_shape, hid_shape, jax.ShapeDtypeStruct((T, D), BF16),
                   jax.ShapeDtypeStruct((T, D), F32), per_b_shape, per_b_shape, per_b_shape,
                   jax.ShapeDtypeStruct((1, D), F32)],
        scratch_shapes=[pltpu.VMEM((tm, D), BF16), pltpu.VMEM((tm, D), F32)],
        operands=(dxo, x, f, a, u, gn, sc, gate, wg, wu, wd), comm=comm)


def _wgrad(a, a_spec, b, b_spec, rows, cols, n_tok, name, comm=None):
    tk = _tile(n_tok, 4096)
    nk = n_tok // tk
    half = rows // 2

    def body(a_ref, b_ref, o32_ref, o16_ref, acc):
        k = pl.program_id(1)

        @pl.when(k == 0)
        def _():
            acc[...] = jnp.zeros_like(acc)

        acc[...] += _dot_tn(a_ref[...], b_ref[...])

        @pl.when(k == nk - 1)
        def _():
            for h in range(2):
                v = acc[h * half:(h + 1) * half, :]
                o32_ref[h] = v
                o16_ref[h] = v.astype(BF16)

    out_spec = pl.BlockSpec((None, 2, half, cols), lambda j, k: (j, 0, 0, 0))
    return _call(
        body, name=name, grid=(N_CHIP, nk),
        in_specs=[a_spec(tk), b_spec(tk)],
        out_specs=[out_spec, out_spec],
        out_shape=[jax.ShapeDtypeStruct((N_CHIP, 2, half, cols), F32),
                   jax.ShapeDtypeStruct((N_CHIP, 2, half, cols), BF16)],
        scratch_shapes=[pltpu.VMEM((rows, cols), F32)],
        operands=(a, b), comm=comm)


def _spec_rows(width):
    return lambda tk: pl.BlockSpec((tk, width), lambda j, k: (k, 0))


def _spec_chip_major(width):
    return lambda tk: pl.BlockSpec((None, tk, width), lambda j, k: (j, k, 0))


def _spec_col_block(width):
    return lambda tk: pl.BlockSpec((tk, width), lambda j, k: (k, j))


def _in_proj(x, gn, sc, sh, w_in, seq, comm=None):
    T, D = x.shape
    N = w_in.shape[0]
    tm = _tile(seq, 2048)
    nb = seq // tm

    def body(x_ref, gn_ref, sc_ref, sh_ref, w_ref, h_ref, p_ref, hs):
        @pl.when(pl.program_id(1) == 0)
        def _():
            hb = _norm_mod(x_ref[...], gn_ref[...], sc_ref[...], sh_ref[...]).astype(BF16)
            hs[...] = hb
            h_ref[...] = hb

        p_ref[...] = _dot_nt(hs[...], w_ref[...]).astype(BF16)

    row = pl.BlockSpec((tm, D), lambda i, j: (i, 0))
    per_b = pl.BlockSpec((None, 1, D), lambda i, j: (i // nb, 0, 0))
    return _call(
        body, name="mix_in_proj", grid=(T // tm, N // PROJ_TILE),
        in_specs=[row, pl.BlockSpec((1, D), lambda i, j: (0, 0)), per_b, per_b,
                  pl.BlockSpec((PROJ_TILE, D), lambda i, j: (j, 0))],
        out_specs=[row, pl.BlockSpec((tm, PROJ_TILE), lambda i, j: (i, j))],
        out_shape=[jax.ShapeDtypeStruct((T, D), BF16), jax.ShapeDtypeStruct((T, N), BF16)],
        scratch_shapes=[pltpu.VMEM((tm, D), BF16)],
        operands=(x, gn, sc, sh, w_in), comm=comm)


def _attn_specs(nblk):
    def own(col):
        return lambda b, n: (b * nblk + n, col)

    def prev(col):
        return lambda b, n: (b * nblk + jnp.maximum(n - 1, 0), col)

    kv = (BLOCK, 2 * HEAD_DIM)
    return [pl.BlockSpec((BLOCK, D_MODEL), own(COLB_Q)),
            pl.BlockSpec(kv, prev(COLB_K)), pl.BlockSpec(kv, own(COLB_K)),
            pl.BlockSpec(kv, prev(COLB_V)), pl.BlockSpec(kv, own(COLB_V))]


def _band_operands(prev_ref, own_ref, lo):
    band = jnp.concatenate([prev_ref[...], own_ref[...]], axis=0).astype(F32)
    rolled = pltpu.roll(band, HEAD_DIM, 1)
    zero = jnp.zeros_like(band)
    head0 = jnp.concatenate([jnp.where(lo, band, zero), jnp.where(lo, zero, rolled)], axis=0).astype(BF16)
    head1 = jnp.concatenate([jnp.where(lo, rolled, zero), jnp.where(lo, zero, band)], axis=0).astype(BF16)
    return head0, head1


PAIRS_PER_KV = N_Q_HEADS // 2 // N_KV_HEADS
BAND = 2 * BLOCK


def _band_valid(has_prev):
    qi = lax.broadcasted_iota(jnp.int32, (PAIRS_PER_KV * BLOCK, BAND), 0) & (BLOCK - 1)
    sj = lax.broadcasted_iota(jnp.int32, (PAIRS_PER_KV * BLOCK, BAND), 1)
    rel = qi + BLOCK - sj
    return (rel >= 0) & (rel < BLOCK) & ((sj >= BLOCK) | has_prev)


def _pair_lanes(kvh, pp):
    pair = kvh * PAIRS_PER_KV + pp
    return slice(pair * 2 * HEAD_DIM, (pair + 1) * 2 * HEAD_DIM)


def _stack_pairs(ref, kvh):
    return jnp.concatenate([ref[:, _pair_lanes(kvh, pp)] for pp in range(PAIRS_PER_KV)], axis=0)


def _rows_per_pair(columns):
    return jnp.concatenate(columns, axis=0)


def _attn_fwd(proj, sinks, batch, seq, comm=None):
    T = proj.shape[0]
    nblk = seq // BLOCK

    def body(sink_ref, q_ref, kp_ref, ko_ref, vp_ref, vo_ref, o_ref, lse_ref):
        lo = lax.broadcasted_iota(jnp.int32, (1, 2 * HEAD_DIM), 1) < HEAD_DIM
        head_lane = lax.broadcasted_iota(jnp.int32, (1, N_Q_HEADS), 1)
        valid = _band_valid(pl.program_id(1) > 0)
        k_ops = _band_operands(kp_ref, ko_ref, lo)
        v_ops = _band_operands(vp_ref, vo_ref, lo)
        lse_all = jnp.zeros((BLOCK, N_Q_HEADS), F32)
        col = jnp.zeros((BLOCK, 1), F32)
        side0_row = lax.broadcasted_iota(jnp.int32, (2 * BAND, 2 * HEAD_DIM), 0) < BAND
        low_lane = lax.broadcasted_iota(jnp.int32, (2 * BAND, 2 * HEAD_DIM), 1) < HEAD_DIM
        side_ones = jnp.where(side0_row == low_lane, 1.0, 0.0).astype(BF16)
        for kvh in range(N_KV_HEADS):
            s_all = _dot_nt(_stack_pairs(q_ref, kvh), k_ops[kvh]) * ATTN_SCALE
            weights, maxes, sink_terms = [], [], []
            for side in range(2):
                heads = [2 * (kvh * PAIRS_PER_KV + pp) + side for pp in range(PAIRS_PER_KV)]
                sink = _rows_per_pair([col + sink_ref[0, h] for h in heads])
                s = jnp.where(valid, s_all[:, side * BAND:(side + 1) * BAND], MASK_VALUE)
                m = jnp.maximum(jnp.max(s, axis=-1, keepdims=True), sink)
                weights.append(jnp.where(valid, jnp.exp(s - m), 0.0).astype(BF16))
                maxes.append(m)
                sink_terms.append(jnp.exp(sink - m))
            p_all = jnp.concatenate(weights, axis=1)
            den = _dot_nn(p_all, side_ones) + jnp.where(lo, sink_terms[0], sink_terms[1])
            out = _dot_nn(p_all, v_ops[kvh]) / den
            for pp in range(PAIRS_PER_KV):
                o_ref[:, _pair_lanes(kvh, pp)] = out[pp * BLOCK:(pp + 1) * BLOCK].astype(BF16)
            for side in range(2):
                lse = maxes[side] + jnp.log(den[:, side * HEAD_DIM:side * HEAD_DIM + 1])
                for pp in range(PAIRS_PER_KV):
                    h = 2 * (kvh * PAIRS_PER_KV + pp) + side
                    lse_all = jnp.where(head_lane == h, lse[pp * BLOCK:(pp + 1) * BLOCK], lse_all)
        lse_ref[...] = lse_all

    return _call(
        body, name="attn_fwd", grid=(batch, nblk),
        in_specs=[SMEM_SPEC] + _attn_specs(nblk),
        out_specs=[pl.BlockSpec((BLOCK, D_MODEL), lambda b, n: (b * nblk + n, 0)),
                   pl.BlockSpec((BLOCK, N_Q_HEADS), lambda b, n: (b * nblk + n, 0))],
        out_shape=[jax.ShapeDtypeStruct((T, D_MODEL), BF16), jax.ShapeDtypeStruct((T, N_Q_HEADS), F32)],
        operands=(sinks, proj, proj, proj, proj, proj), comm=comm)


def _conv_u(ca, cb):
    return ca.astype(F32) * _sigmoid(cb.astype(F32))


def _conv_specs(ts, tiles_per_seq):
    per_tile = ts // CONV_PAD

    def tile(col):
        return lambda b, t: (b * tiles_per_seq + t, col)

    def before(col):
        return lambda b, t: (jnp.maximum((b * tiles_per_seq + t) * per_tile - 1, 0), col)

    return [pl.BlockSpec((ts, D_MODEL), tile(COLB_CA)), pl.BlockSpec((ts, D_MODEL), tile(COLB_CB)),
            pl.BlockSpec((CONV_PAD, D_MODEL), before(COLB_CA)), pl.BlockSpec((CONV_PAD, D_MODEL), before(COLB_CB))]


SUBLANES = 8


def _fill_upad(upad, ca_ref, cb_ref, cah_ref, cbh_ref, t):
    halo = _conv_u(cah_ref[...], cbh_ref[...])
    upad[0, 0:CONV_PAD, :] = jnp.where(t > 0, halo, jnp.zeros_like(halo))
    upad[0, CONV_PAD:, :] = _conv_u(ca_ref[...], cb_ref[...])


def _fill_shifted(pad):
    rows = pad.shape[1] - SUBLANES
    for b in range(1, SUBLANES):
        pad[b, 0:rows, :] = pad[0, b:b + rows, :]


def _shifted_rows(pad, offset, rows):
    b = offset % SUBLANES
    return pad[b, offset - b:offset - b + rows, :]


def _layernorm_stats(y):
    mu = jnp.mean(y, axis=-1, keepdims=True)
    yc = y - mu
    rstd = lax.rsqrt(jnp.mean(yc * yc, axis=-1, keepdims=True) + EPS)
    return yc * rstd, rstd


def _conv_fwd(proj, w_dw, b_dw, ln_g, ln_b, batch, seq, comm=None):
    T = proj.shape[0]
    ts = _tile(seq, 256)
    nt = seq // ts
    shift = CONV_PAD - (CONV_WIDTH - 1)

    def body(ca_ref, cb_ref, cah_ref, cbh_ref, w_ref, b_ref, g_ref, beta_ref, y_ref, z_ref, upad):
        _fill_upad(upad, ca_ref, cb_ref, cah_ref, cbh_ref, pl.program_id(1))
        _fill_shifted(upad)
        y = jnp.zeros((ts, D_MODEL), F32) + b_ref[...]
        for k in range(CONV_WIDTH):
            y = y + w_ref[k:k + 1, :] * _shifted_rows(upad, shift + k, ts)
        y_ref[...] = y
        lnh, _ = _layernorm_stats(y)
        ln = lnh * g_ref[...] + beta_ref[...]
        z_ref[...] = (ln * _sigmoid(ln)).astype(BF16)

    vec = pl.BlockSpec((1, D_MODEL), lambda b, t: (0, 0))
    row = pl.BlockSpec((ts, D_MODEL), lambda b, t: (b * nt + t, 0))
    return _call(
        body, name="conv_fwd", grid=(batch, nt),
        in_specs=_conv_specs(ts, nt) + [pl.BlockSpec((CONV_PAD, D_MODEL), lambda b, t: (0, 0)), vec, vec, vec],
        out_specs=[row, row],
        out_shape=[jax.ShapeDtypeStruct((T, D_MODEL), F32), jax.ShapeDtypeStruct((T, D_MODEL), BF16)],
        scratch_shapes=[pltpu.VMEM((SUBLANES, ts + CONV_PAD, D_MODEL), F32)],
        operands=(proj, proj, proj, proj, w_dw, b_dw, ln_g, ln_b), comm=comm)


def _merge(o, z, proj, w_ao, w_co, w_out, x, gate, seq):
    T, D = x.shape
    tm = _tile(seq, 512)
    nb = seq // tm

    def body(o_ref, z_ref, ga_ref, gc_ref, wao_ref, wco_ref, wout_ref, x_ref, gate_ref,
             ya_ref, yc_ref, mg_ref, mo_ref, xo_ref):
        ya = _dot_nn(o_ref[...], wao_ref[...])
        yc = _dot_nn(z_ref[...], wco_ref[...])
        ya_ref[...] = ya.astype(BF16)
        yc_ref[...] = yc.astype(BF16)
        merged = (_sigmoid(ga_ref[...].astype(F32)) * ya + _sigmoid(gc_ref[...].astype(F32)) * yc).astype(BF16)
        mg_ref[...] = merged
        mo = _dot_nn(merged, wout_ref[...])
        mo_ref[...] = mo.astype(BF16)
        xo_ref[...] = x_ref[...] + gate_ref[...] * mo

    row = pl.BlockSpec((tm, D), lambda i: (i, 0))
    mat = pl.BlockSpec((D, D), lambda i: (0, 0))
    act = jax.ShapeDtypeStruct((T, D), BF16)
    return pl.pallas_call(
        body, name="mix_merge", grid=(T // tm,),
        in_specs=[row, row, pl.BlockSpec((tm, D), lambda i: (i, COLB_GA)), pl.BlockSpec((tm, D), lambda i: (i, COLB_GC)),
                  mat, mat, mat, row, pl.BlockSpec((None, 1, D), lambda i: (i // nb, 0, 0))],
        out_specs=[row, row, row, row, row],
        out_shape=[act, act, act, act, jax.ShapeDtypeStruct((T, D), F32)],
        compiler_params=_params(1),
    )(o, z, proj, proj, w_ao, w_co, w_out, x, gate)


def _final_loss(x, gf, target):
    T, D = x.shape
    tm = _tile(T, 512)

    def body(x_ref, gf_ref, t_ref, dx_ref, lp_ref, dgf_ref):
        first = pl.program_id(0) == 0
        xv = x_ref[...]
        gfv = gf_ref[...]
        r = lax.rsqrt(jnp.mean(xv * xv, axis=-1, keepdims=True) + EPS)
        xh = xv * r
        err = xh * gfv - t_ref[...]
        _accumulate(lp_ref, first, jnp.sum(err * err, axis=0, keepdims=True))
        dy = err * (1.0 / D)
        _accumulate(dgf_ref, first, jnp.sum(dy * xh, axis=0, keepdims=True))
        dxh = dy * gfv
        dx_ref[...] = r * (dxh - xh * jnp.mean(dxh * xh, axis=-1, keepdims=True))

    row = pl.BlockSpec((tm, D), lambda i: (i, 0))
    vec = pl.BlockSpec((1, D), lambda i: (0, 0))
    return pl.pallas_call(
        body, name="final_loss", grid=(T // tm,),
        in_specs=[row, vec, row], out_specs=[row, vec, vec],
        out_shape=[jax.ShapeDtypeStruct((T, D), F32), jax.ShapeDtypeStruct((1, D), F32),
                   jax.ShapeDtypeStruct((1, D), F32)],
        compiler_params=_params(1),
    )(x, gf, target)


def _merge_bwd(dxo, mo, gate, proj, ya, yc, w_out, w_ao, w_co, seq, comm=None):
    T, D = dxo.shape
    B = T // seq
    tm = _tile(seq, 512)
    nb = seq // tm

    def body(dxo_ref, mo_ref, gate_ref, ga_ref, gc_ref, ya_ref, yc_ref, wout_ref, wao_ref, wco_ref,
             dmo_ref, dya_ref, dyc_ref, dga_ref, dgc_ref, do_ref, dz_ref, dgate_ref):
        dxo_v = dxo_ref[...]
        dmo = (gate_ref[...] * dxo_v).astype(BF16)
        dmo_ref[...] = dmo
        _accumulate(dgate_ref, pl.program_id(0) % nb == 0,
                    jnp.sum(mo_ref[...].astype(F32) * dxo_v, axis=0, keepdims=True))
        dm = _dot_nt(dmo, wout_ref[...])
        sa = _sigmoid(ga_ref[...].astype(F32))
        sc = _sigmoid(gc_ref[...].astype(F32))
        dya = (sa * dm).astype(BF16)
        dyc = (sc * dm).astype(BF16)
        dya_ref[...] = dya
        dyc_ref[...] = dyc
        dga_ref[...] = (dm * ya_ref[...].astype(F32) * (sa * (1.0 - sa))).astype(BF16)
        dgc_ref[...] = (dm * yc_ref[...].astype(F32) * (sc * (1.0 - sc))).astype(BF16)
        do_ref[...] = _dot_nt(dya, wao_ref[...]).astype(BF16)
        dz_ref[...] = _dot_nt(dyc, wco_ref[...]).astype(BF16)

    row = pl.BlockSpec((tm, D), lambda i: (i, 0))
    mat = pl.BlockSpec((D, D), lambda i: (0, 0))
    per_b = pl.BlockSpec((None, 1, D), lambda i: (i // nb, 0, 0))
    act = jax.ShapeDtypeStruct((T, D), BF16)
    return _call(
        body, name="mix_merge_bwd", grid=(T // tm,),
        in_specs=[row, row, per_b, pl.BlockSpec((tm, D), lambda i: (i, COLB_GA)),
                  pl.BlockSpec((tm, D), lambda i: (i, COLB_GC)), row, row, mat, mat, mat],
        out_specs=[row] * 7 + [per_b],
        out_shape=[act] * 7 + [jax.ShapeDtypeStruct((B, 1, D), F32)],
        operands=(dxo, mo, gate, proj, proj, ya, yc, w_out, w_ao, w_co), comm=comm)


def _attn_bwd(proj, sinks, o, do, lse, batch, seq, comm=None):
    T = proj.shape[0]
    nblk = seq // BLOCK
    n_steps = batch * nblk

    def body(sink_ref, q_ref, kp_ref, ko_ref, vp_ref, vo_ref, o_ref, do_ref, lse_ref,
             dq_ref, dkp_ref, dko_ref, dvp_ref, dvo_ref, dsink_ref):
        lo = lax.broadcasted_iota(jnp.int32, (1, 2 * HEAD_DIM), 1) < HEAD_DIM
        sink_lane = lax.broadcasted_iota(jnp.int32, (1, 2 * HEAD_DIM), 1)
        valid = _band_valid(pl.program_id(1) > 0)
        k_ops = _band_operands(kp_ref, ko_ref, lo)
        v_ops = _band_operands(vp_ref, vo_ref, lo)
        dsink = jnp.zeros((1, 2 * HEAD_DIM), F32)
        col = jnp.zeros((BLOCK, 1), F32)

        def fold(both):
            return (jnp.where(lo, both[:BAND], 0.0)
                    + pltpu.roll(jnp.where(lo, 0.0, both[BAND:]), HEAD_DIM, 1))

        dk_heads, dv_heads = [], []
        for kvh in range(N_KV_HEADS):
            q4 = _stack_pairs(q_ref, kvh)
            do4 = _stack_pairs(do_ref, kvh)
            dd = do4.astype(F32) * _stack_pairs(o_ref, kvh).astype(F32)
            s_all = _dot_nt(q4, k_ops[kvh]) * ATTN_SCALE
            dp_all = _dot_nt(do4, v_ops[kvh])
            ds_sides, p_sides = [], []
            for side in range(2):
                heads = [2 * (kvh * PAIRS_PER_KV + pp) + side for pp in range(PAIRS_PER_KV)]
                mine = lo if side == 0 else jnp.logical_not(lo)
                cols = slice(side * BAND, (side + 1) * BAND)
                sink = _rows_per_pair([col + sink_ref[0, h] for h in heads])
                lse = _rows_per_pair([lse_ref[:, h:h + 1] for h in heads])
                delta = jnp.sum(jnp.where(mine, dd, 0.0), axis=-1, keepdims=True)
                p = jnp.where(valid, jnp.exp(jnp.where(valid, s_all[:, cols], MASK_VALUE) - lse), 0.0)
                ds_sides.append((p * (dp_all[:, cols] - delta) * ATTN_SCALE).astype(BF16))
                p_sides.append(p.astype(BF16))
                sink_part = jnp.exp(sink - lse) * delta
                for pp, h in enumerate(heads):
                    dsink = dsink + jnp.where(sink_lane == h, -jnp.sum(sink_part[pp * BLOCK:(pp + 1) * BLOCK]), 0.0)
            ds_all = jnp.concatenate(ds_sides, axis=1)
            dq4 = _dot_nn(ds_all, k_ops[kvh])
            for pp in range(PAIRS_PER_KV):
                dq_ref[:, _pair_lanes(kvh, pp)] = dq4[pp * BLOCK:(pp + 1) * BLOCK].astype(BF16)
            dk_heads.append(fold(_dot_tn(ds_all, q4)))
            dv_heads.append(fold(_dot_tn(jnp.concatenate(p_sides, axis=1), do4)))
        dk = dk_heads[0] + pltpu.roll(dk_heads[1], HEAD_DIM, 1)
        dv = dv_heads[0] + pltpu.roll(dv_heads[1], HEAD_DIM, 1)
        dkp_ref[...] = dk[:BLOCK]
        dko_ref[...] = dk[BLOCK:]
        dvp_ref[...] = dv[:BLOCK]
        dvo_ref[...] = dv[BLOCK:]
        dsink_ref[...] = dsink

    def own(b, n):
        return (b * nblk + n, 0)

    row = pl.BlockSpec((BLOCK, D_MODEL), own)
    kv = pl.BlockSpec((BLOCK, 2 * HEAD_DIM), own)
    kv_shape = jax.ShapeDtypeStruct((T, 2 * HEAD_DIM), F32)
    return _call(
        body, name="attn_bwd", grid=(batch, nblk),
        in_specs=[SMEM_SPEC] + _attn_specs(nblk) + [row, row, pl.BlockSpec((BLOCK, N_Q_HEADS), own)],
        out_specs=[row, kv, kv, kv, kv, pl.BlockSpec((None, 1, 2 * HEAD_DIM), lambda b, n: (b * nblk + n, 0, 0))],
        out_shape=[jax.ShapeDtypeStruct((T, D_MODEL), BF16), kv_shape, kv_shape, kv_shape, kv_shape,
                   jax.ShapeDtypeStruct((n_steps, 1, 2 * HEAD_DIM), F32)],
        operands=(sinks, proj, proj, proj, proj, proj, o, do, lse), comm=comm)


def _conv_bwd(proj, dz, ydw, w_dw, ln_g, ln_b, batch, seq, comm=None):
    T = proj.shape[0]
    ts = _tile(seq, 256)
    nt = seq // ts
    per_tile = ts // CONV_PAD
    shift = CONV_PAD - (CONV_WIDTH - 1)

    def body(ca_ref, cb_ref, cah_ref, cbh_ref, dz_ref, dzn_ref, y_ref, yn_ref, w_ref, g_ref, beta_ref,
             dca_ref, dcb_ref, dw_ref, db_ref, dg_ref, dbeta_ref, upad, dypad):
        t = pl.program_id(1)
        first = (pl.program_id(0) == 0) & (t == 0)
        gv = g_ref[...]

        def ln_bwd(dzv, yv):
            lnh, rstd = _layernorm_stats(yv)
            ln = lnh * gv + beta_ref[...]
            sg = _sigmoid(ln)
            dln = dzv.astype(F32) * (sg * (1.0 + ln * (1.0 - sg)))
            dyh = dln * gv
            dy = rstd * (dyh - jnp.mean(dyh, axis=-1, keepdims=True)
                         - lnh * jnp.mean(dyh * lnh, axis=-1, keepdims=True))
            return dy, dln, lnh

        dy, dln, lnh = ln_bwd(dz_ref[...], y_ref[...])
        dy_next, _, _ = ln_bwd(dzn_ref[...], yn_ref[...])
        dypad[0, 0:ts, :] = dy
        dypad[0, ts:, :] = jnp.where(t < nt - 1, dy_next, jnp.zeros_like(dy_next))
        _fill_shifted(dypad)
        _fill_upad(upad, ca_ref, cb_ref, cah_ref, cbh_ref, t)
        _fill_shifted(upad)

        _accumulate(dg_ref, first, jnp.sum(dln * lnh, axis=0, keepdims=True))
        _accumulate(dbeta_ref, first, jnp.sum(dln, axis=0, keepdims=True))
        _accumulate(db_ref, first, jnp.sum(dy, axis=0, keepdims=True))

        @pl.when(first)
        def _():
            dw_ref[...] = jnp.zeros_like(dw_ref)

        du = jnp.zeros((ts, D_MODEL), F32)
        for k in range(CONV_WIDTH):
            du = du + w_ref[k:k + 1, :] * _shifted_rows(dypad, CONV_WIDTH - 1 - k, ts)
            dw_ref[k:k + 1, :] += jnp.sum(dy * _shifted_rows(upad, shift + k, ts), axis=0, keepdims=True)
        cav = ca_ref[...].astype(F32)
        sb = _sigmoid(cb_ref[...].astype(F32))
        dca_ref[...] = (du * sb).astype(BF16)
        dcb_ref[...] = (du * cav * (sb * (1.0 - sb))).astype(BF16)

    def tile(b, t):
        return (b * nt + t, 0)

    def after(b, t):
        return (jnp.minimum((b * nt + t + 1) * per_tile, T // CONV_PAD - 1), 0)

    row = pl.BlockSpec((ts, D_MODEL), tile)
    halo = pl.BlockSpec((CONV_PAD, D_MODEL), after)
    vec = pl.BlockSpec((1, D_MODEL), lambda b, t: (0, 0))
    wspec = pl.BlockSpec((CONV_PAD, D_MODEL), lambda b, t: (0, 0))
    act = jax.ShapeDtypeStruct((T, D_MODEL), BF16)
    vec_shape = jax.ShapeDtypeStruct((1, D_MODEL), F32)
    return _call(
        body, name="conv_bwd", grid=(batch, nt),
        in_specs=_conv_specs(ts, nt) + [row, halo, row, halo, wspec, vec, vec],
        out_specs=[row, row, wspec, vec, vec, vec],
        out_shape=[act, act, jax.ShapeDtypeStruct((CONV_PAD, D_MODEL), F32), vec_shape, vec_shape, vec_shape],
        scratch_shapes=[pltpu.VMEM((SUBLANES, ts + CONV_PAD, D_MODEL), F32)] * 2,
        operands=(proj, proj, proj, proj, dz, dz, ydw, ydw, w_dw, ln_g, ln_b), comm=comm)


def _in_proj_bwd(pieces, w_cols, x, gn, sc, dxo, seq, comm=None):
    T, D = x.shape
    B = T // seq
    wide, narrow = list(pieces[:-1]), pieces[-1]
    P = len(wide)
    nw = narrow.shape[1]
    tm = _tile(seq, 512)
    nb = seq // tm

    def body(*refs):
        wide_refs = refs[:P]
        kv_ref, w_ref, wkv_ref, x_ref, gn_ref, sc_ref, dxo_ref, dx_ref, dsc_ref, dsh_ref, dgn_ref, acc = refs[P:]
        i = pl.program_id(0)
        j = pl.program_id(1)

        @pl.when(j == 0)
        def _():
            acc[...] = _dot_nn(kv_ref[...], wkv_ref[...])

        for p in range(P):
            @pl.when(j == p)
            def _(p=p):
                acc[...] += _dot_nn(wide_refs[p][...], w_ref[...])

        @pl.when(j == P - 1)
        def _():
            _norm_mod_bwd(acc[...], x_ref[...], gn_ref[...], sc_ref[...], dxo_ref[...],
                          i % nb == 0, i == 0, dx_ref, dsc_ref, dsh_ref, dgn_ref)

    row = pl.BlockSpec((tm, D), lambda i, j: (i, 0))
    vec = pl.BlockSpec((1, D), lambda i, j: (0, 0))
    per_b = pl.BlockSpec((None, 1, D), lambda i, j: (i // nb, 0, 0))
    per_b_shape = jax.ShapeDtypeStruct((B, 1, D), F32)
    return _call(
        body, name="mix_in_proj_bwd", grid=(T // tm, P),
        in_specs=[row] * P + [pl.BlockSpec((tm, nw), lambda i, j: (i, 0)),
                              pl.BlockSpec((D, D), lambda i, j: (j, 0)),
                              pl.BlockSpec((nw, D), lambda i, j: (P * D // nw, 0)), row, vec, per_b, row],
        out_specs=[row, per_b, per_b, vec],
        out_shape=[jax.ShapeDtypeStruct((T, D), F32), per_b_shape, per_b_shape, jax.ShapeDtypeStruct((1, D), F32)],
        scratch_shapes=[pltpu.VMEM((tm, D), F32)],
        operands=(*wide, narrow, w_cols, w_cols, x, gn, sc, dxo), comm=comm)


def _wgrad_rows(piece, h, out32, out16, row_offset, name):
    T, n = piece.shape
    C = h.shape[1]
    tk = _tile(T, 1024)
    nk = T // tk

    def body(a_ref, b_ref, in32, in16, o32_ref, o16_ref, acc, stage16, sems):
        k = pl.program_id(0)

        @pl.when(k == 0)
        def _():
            acc[...] = jnp.zeros_like(acc)

        acc[...] += _dot_tn(a_ref[...], b_ref[...])

        @pl.when(k == nk - 1)
        def _():
            stage16[...] = acc[...].astype(BF16)
            rows = pl.ds(row_offset, n)
            copies = [pltpu.make_async_copy(acc, o32_ref.at[rows, :], sems.at[0]),
                      pltpu.make_async_copy(stage16, o16_ref.at[rows, :], sems.at[1])]
            for cp in copies:
                cp.start()
            for cp in copies:
                cp.wait()

    return pl.pallas_call(
        body, name=name, grid=(nk,),
        in_specs=[pl.BlockSpec((tk, n), lambda k: (k, 0)), pl.BlockSpec((tk, C), lambda k: (k, 0)), ANY, ANY],
        out_specs=[ANY, ANY], out_shape=[jax.ShapeDtypeStruct(out32.shape, F32), jax.ShapeDtypeStruct(out16.shape, BF16)],
        scratch_shapes=[pltpu.VMEM((n, C), F32), pltpu.VMEM((n, C), BF16), pltpu.SemaphoreType.DMA((2,))],
        input_output_aliases={2: 0, 3: 1}, compiler_params=_params(1),
    )(piece, h, out32, out16)


def _ada_fwd(c_all, w_ada, b_cols):
    nbatch, D = c_all.shape
    N = w_ada.shape[1]
    tn = _tile(N, 768)

    def body(c_ref, w_ref, b_ref, o_ref):
        cv = c_ref[...]
        act = (cv * _sigmoid(cv)).astype(BF16)
        o_ref[...] = _dot_nn(act, w_ref[...].astype(BF16)) + b_ref[...]

    return pl.pallas_call(
        body, name="ada_fwd", grid=(N // tn,),
        in_specs=[pl.BlockSpec((nbatch, D), lambda j: (0, 0)), pl.BlockSpec((D, tn), lambda j: (0, j)),
                  pl.BlockSpec((1, tn), lambda j: (0, j))],
        out_specs=pl.BlockSpec((nbatch, tn), lambda j: (0, j)),
        out_shape=jax.ShapeDtypeStruct((nbatch, N), F32),
        compiler_params=_params(1),
    )(c_all, w_ada, b_cols)


def _adamw(w, g, m, v):
    m = ADAM_B1 * m + (1.0 - ADAM_B1) * g
    v = ADAM_B2 * v + (1.0 - ADAM_B2) * (g * g)
    m_hat = m / (1.0 - ADAM_B1 ** ADAM_STEP)
    v_hat = v / (1.0 - ADAM_B2 ** ADAM_STEP)
    delta = -ADAM_LR * (m_hat / (jnp.sqrt(v_hat) + ADAM_EPS) + ADAM_WD * w)
    return delta, m, v


def _adam_call(w, g, m, v, name, comm=None):
    R, C = w.shape
    tr = _row_tile(R, 512)

    def body(w_ref, g_ref, m_ref, v_ref, d_ref, mo_ref, vo_ref):
        d, mn, vn = _adamw(w_ref[...], g_ref[...], m_ref[...], v_ref[...])
        d_ref[...] = d
        mo_ref[...] = mn
        vo_ref[...] = vn

    blk = pl.BlockSpec((tr, C), lambda i: (i, 0))
    shape = jax.ShapeDtypeStruct((R, C), F32)
    return _call(body, name=name, grid=(R // tr,), in_specs=[blk] * 4, out_specs=[blk] * 3, out_shape=[shape] * 3,
                 operands=(w, g, m, v), comm=comm)


ADAM_GROUP_STEPS = 8


def _adam_group(ws, gs, ms, vs, name, comm=None):
    n = len(ws)

    def body(*refs):
        ins, outs = refs[:4 * n], refs[4 * n:]
        for i in range(n):
            d, mn, vn = _adamw(*(r[...] for r in ins[4 * i:4 * i + 4]))
            outs[3 * i][...] = d
            outs[3 * i + 1][...] = mn
            outs[3 * i + 2][...] = vn

    operands, in_specs, out_specs, out_shape = [], [], [], []
    for w, g, m, v in zip(ws, gs, ms, vs):
        R, C = w.shape
        blk = pl.BlockSpec((R // ADAM_GROUP_STEPS, C), lambda i: (i, 0))
        operands += [w, g, m, v]
        in_specs += [blk] * 4
        out_specs += [blk] * 3
        out_shape += [jax.ShapeDtypeStruct((R, C), F32)] * 3
    outs, comm_outs = _call(body, name=name, grid=(ADAM_GROUP_STEPS,), in_specs=in_specs, out_specs=out_specs,
                            out_shape=out_shape, operands=operands, comm=comm)
    return [tuple(outs[3 * i:3 * i + 3]) for i in range(n)], comm_outs


def _ada_adam(c_act_t, dmod_cols, w, m, v, comm):
    R, C = w.shape
    nbatch = c_act_t.shape[1]
    tr = _tile(R, 128)

    def body(ct_ref, dm_ref, w_ref, m_ref, v_ref, g_ref, d_ref, mo_ref, vo_ref):
        cv = ct_ref[...]
        g = _dot_nn((cv * _sigmoid(cv)).astype(BF16), dm_ref[...].astype(BF16))
        g_ref[...] = g
        d, mn, vn = _adamw(w_ref[...], g, m_ref[...], v_ref[...])
        d_ref[...] = d
        mo_ref[...] = mn
        vo_ref[...] = vn

    blk = pl.BlockSpec((tr, C), lambda i: (i, 0))
    shape = jax.ShapeDtypeStruct((R, C), F32)
    return _call(
        body, name="ada_adam", grid=(R // tr,),
        in_specs=[pl.BlockSpec((tr, nbatch), lambda i: (i, 0)), pl.BlockSpec((nbatch, C), lambda i: (0, 0)),
                  blk, blk, blk],
        out_specs=[blk] * 4, out_shape=[shape] * 4,
        operands=(c_act_t, dmod_cols, w, m, v), comm=comm)


def _small_adam(gathered, w, m, v, rows_b0, rows_b1, rows_vec):
    _, P, D = gathered.shape
    R = w.shape[0]

    def body(ga_ref, w_ref, m_ref, v_ref, sum_ref, g_ref, d_ref, mo_ref, vo_ref):
        total = ga_ref[0]
        for dev in range(1, N_DEV):
            total = total + ga_ref[dev]
        sum_ref[...] = total
        g_ref[...] = jnp.zeros_like(g_ref)
        g_ref[0:N_MOD, :] = (sum_ref[rows_b0:rows_b0 + N_MOD, :] + sum_ref[rows_b1:rows_b1 + N_MOD, :])
        g_ref[N_MOD:N_MOD + 8, :] = sum_ref[rows_vec:rows_vec + 8, :]
        d, mn, vn = _adamw(w_ref[...], g_ref[...], m_ref[...], v_ref[...])
        d_ref[...] = d
        mo_ref[...] = mn
        vo_ref[...] = vn

    shape = jax.ShapeDtypeStruct((R, D), F32)
    return pl.pallas_call(
        body, name="small_adam",
        in_specs=[VMEM_SPEC] * 4, out_specs=[VMEM_SPEC] * 5,
        out_shape=[jax.ShapeDtypeStruct((P, D), F32), shape, shape, shape, shape],
        compiler_params=pltpu.CompilerParams(vmem_limit_bytes=VMEM_LIMIT),
    )(gathered, w, m, v)


def _mod_exchange(part):
    _, A, W = part.shape

    def body(p_ref, out_ref, send_sems, recv_sems, local_sem):
        x, y, c = _position()
        me = 4 * x + 2 * y + c
        chip = 2 * x + y
        mine = pltpu.make_async_copy(p_ref.at[me], out_ref.at[chip], local_sem)
        mine.start()
        peers = [(_flip(x, fx), _flip(y, fy)) for fx, fy in CHIP_FLIPS]
        sends = []
        for k, (px, py) in enumerate(peers):
            sends.append(pltpu.make_async_remote_copy(
                src_ref=p_ref.at[4 * px + 2 * py + c], dst_ref=out_ref.at[chip], send_sem=send_sems.at[k],
                recv_sem=recv_sems.at[k], device_id=(px, py, c), device_id_type=MESH))
        for cp in sends:
            cp.start()
        for k, (px, py) in enumerate(peers):
            pltpu.make_async_remote_copy(
                src_ref=p_ref.at[me], dst_ref=out_ref.at[2 * px + py], send_sem=send_sems.at[k],
                recv_sem=recv_sems.at[k], device_id=(px, py, c), device_id_type=MESH).wait_recv()
        for cp in sends:
            cp.wait_send()
        mine.wait()

    return pl.pallas_call(
        body, name="mod_exchange", in_specs=[VMEM_SPEC], out_specs=VMEM_SPEC,
        out_shape=jax.ShapeDtypeStruct((N_CHIP, A, W), part.dtype),
        scratch_shapes=[pltpu.SemaphoreType.DMA((3,)), pltpu.SemaphoreType.DMA((3,)), pltpu.SemaphoreType.DMA],
    )(part)


KV_ROWS = 4 * HEAD_DIM


def _kernel_row_order(w_in_t):
    R, C = w_in_t.shape
    n_blocks = R // KV_ROWS
    q_blocks = D_MODEL // KV_ROWS

    def source(t):
        return jnp.where(t < q_blocks, t, jnp.where(t < n_blocks - 1, t + 1, q_blocks))

    def body(w_ref, o_ref):
        o_ref[...] = w_ref[...]

    return pl.pallas_call(
        body, name="w_in_row_order", grid=(n_blocks,),
        in_specs=[pl.BlockSpec((KV_ROWS, C), lambda t: (source(t), 0))],
        out_specs=pl.BlockSpec((KV_ROWS, C), lambda t: (t, 0)),
        out_shape=jax.ShapeDtypeStruct((R, C), w_in_t.dtype), compiler_params=_params(1),
    )(w_in_t)


def _cast_group(ws, names, comm):
    n = len(ws)
    steps = 4

    def body(*refs):
        w_refs, out_refs, stage, sem = refs[:n], refs[n:2 * n], refs[2 * n:3 * n], refs[3 * n]
        x, y, _ = _position()
        step = pl.program_id(0)
        copies = []
        for i in range(n):
            rows = ws[i].shape[0] // steps
            stage[i][...] = w_refs[i][...].astype(BF16)
            copies.append(pltpu.make_async_copy(
                stage[i], out_refs[i].at[2 * x + y, pl.ds(step * rows, rows), :], sem.at[i]))
        for cp in copies:
            cp.start()
        for cp in copies:
            cp.wait()

    outs, comm_outs = _call(
        body, name="cast_" + "_".join(names), grid=(steps,),
        in_specs=[pl.BlockSpec((w.shape[0] // steps, w.shape[1]), lambda i: (i, 0)) for w in ws],
        out_specs=[ANY] * n, out_shape=[jax.ShapeDtypeStruct((N_CHIP,) + w.shape, BF16) for w in ws],
        scratch_shapes=[pltpu.VMEM((w.shape[0] // steps, w.shape[1]), BF16) for w in ws]
        + [pltpu.SemaphoreType.DMA((n,))],
        operands=ws, comm=comm)
    return outs, comm_outs


def _cast_slot(w, chip_idx, name):
    R, C = w.shape
    tr = _row_tile(R, 512)

    def body(chip_ref, w_ref, o_ref):
        o_ref[...] = w_ref[...].astype(BF16)

    return pl.pallas_call(
        body, name=name,
        grid_spec=pltpu.PrefetchScalarGridSpec(
            num_scalar_prefetch=1, grid=(R // tr,),
            in_specs=[pl.BlockSpec((tr, C), lambda i, chip_ref: (i, 0))],
            out_specs=pl.BlockSpec((None, tr, C), lambda i, chip_ref: (chip_ref[0], i, 0))),
        out_shape=jax.ShapeDtypeStruct((N_CHIP, R, C), BF16),
        compiler_params=_params(1),
    )(chip_idx, w)


def _pair_sum(g32, recv, core, name):
    J, _, r, C = g32.shape

    def body(core_ref, g_ref, r_ref, o_ref):
        o_ref[...] = (g_ref[...] + r_ref[...].astype(F32)).astype(BF16)

    return pl.pallas_call(
        body, name=name,
        grid_spec=pltpu.PrefetchScalarGridSpec(
            num_scalar_prefetch=1, grid=(J,),
            in_specs=[pl.BlockSpec((None, None, r, C), lambda j, core_ref: (j, core_ref[0], 0, 0)),
                      pl.BlockSpec((None, r, C), lambda j, core_ref: (j, 0, 0))],
            out_specs=pl.BlockSpec((None, r, C), lambda j, core_ref: (j, 0, 0))),
        out_shape=jax.ShapeDtypeStruct((J, r, C), BF16),
        compiler_params=_params(1),
    )(core, g32, recv)


def _chip_sum(g32, recv_sib, recv_chips, core_chip, name):
    J, _, r, C = g32.shape

    def body(idx_ref, g_ref, s_ref, o_ref_in, o_ref):
        total = g_ref[...] + s_ref[...].astype(F32)
        for k in range(3):
            total = total + o_ref_in[k].astype(F32)
        o_ref[...] = total

    return pl.pallas_call(
        body, name=name,
        grid_spec=pltpu.PrefetchScalarGridSpec(
            num_scalar_prefetch=1, grid=(1,),
            in_specs=[pl.BlockSpec((None, None, r, C), lambda i, idx: (idx[1], idx[0], 0, 0)),
                      pl.BlockSpec((None, r, C), lambda i, idx: (idx[1], 0, 0)),
                      pl.BlockSpec((3, r, C), lambda i, idx: (0, 0, 0))],
            out_specs=pl.BlockSpec((None, r, C), lambda i, idx: (idx[0], 0, 0))),
        out_shape=jax.ShapeDtypeStruct((2, r, C), F32),
        compiler_params=_params(1),
    )(core_chip, g32, recv_sib, recv_chips)


ICI_US_PER_ELEMENT = 4.6e-5


class _Reducer:
    def __init__(self, core_idx, core_chip):
        self.core_idx, self.core_chip = core_idx, core_chip
        self.grads, self.halves, self.reduced = {}, {}, {}
        self.ready_swap, self.ready_exchange, self.ready_join = [], [], []
        self.inflight, self.current = ([], [], [], None), None
        self.flushes = 0
        self.extra, self.extra_out = None, None

    def add(self, name, grad_pair):
        self.grads[name] = grad_pair
        self.ready_swap.append(name)

    def comm(self, budget_us):
        swaps, self.ready_swap = self.ready_swap, []
        joins, self.ready_join = self.ready_join, []
        exchanges, waiting = [], []
        for item in self.ready_exchange:
            cost = ICI_US_PER_ELEMENT * 2 * item[2].shape[1] * item[2].shape[2]
            if cost <= budget_us:
                exchanges.append(item)
                budget_us -= cost
            else:
                waiting.append(item)
        self.ready_exchange = waiting
        parts = []
        if swaps:
            parts.append(_SwapComm([self.grads[n][1] for n in swaps]))
        if exchanges:
            parts.append(_ExchangeComm([pair for _, _, pair in exchanges]))
        if joins:
            parts.append(_JoinComm([self.halves[n] for n in joins]))
        extra, self.extra = self.extra, None
        if extra is not None:
            parts.append(extra)
        self.inflight = (swaps, exchanges, joins, extra)
        self.current = _CommList(parts) if parts else None
        return self.current

    def done(self, comm_outs):
        if self.current is None:
            return
        swaps, exchanges, joins, extra = self.inflight
        outs = iter(self.current.split_outputs(list(comm_outs)))
        if swaps:
            for n, recv in zip(swaps, next(outs)):
                pair = _pair_sum(self.grads[n][0], recv, self.core_idx, "pair_sum_" + n)
                self.ready_exchange.append((n, recv, pair))
        if exchanges:
            for (n, recv, _), chips in zip(exchanges, next(outs)):
                self.halves[n] = _chip_sum(self.grads[n][0], recv, chips, self.core_chip, "chip_sum_" + n)
                self.ready_join.append(n)
        if joins:
            self.reduced.update(zip(joins, next(outs)))
        if extra is not None:
            self.extra_out = next(outs)
        self.current = None

    def run(self, kernel, budget_us, *args, **kwargs):
        if budget_us is None:
            return kernel(*args, comm=None, **kwargs)[0]
        outs, comm_outs = kernel(*args, comm=self.comm(budget_us), **kwargs)
        self.done(comm_outs)
        return outs

    def step(self):
        comm = self.comm(float("inf"))
        self.flushes += 1
        self.done(_run_comm(comm, "grad_reduce_tail_%d" % self.flushes))


BIG_WEIGHTS = ("ffn1_w_gate", "ffn1_w_up", "ffn1_w_down", "w_in", "w_attn_o", "w_conv_o", "w_out",
               "ffn2_w_gate", "ffn2_w_up", "ffn2_w_down")
VECTORS = ("norm_ffn1_g", "norm_mix_g", "conv_b_dw", "conv_ln_g", "conv_ln_b", "norm_ffn2_g", "final_norm_g")
ROW_DMOD0, ROW_DMOD1, ROW_VEC, ROW_SINK, ROW_CONVW, SMALL_ROWS = 0, 16, 33, 40, 41, 72


FFN1_WEIGHTS = ("ffn1_w_gate", "ffn1_w_up", "ffn1_w_down")
FFN2_WEIGHTS = ("ffn2_w_gate", "ffn2_w_up", "ffn2_w_down")
MIX_WEIGHTS = ("w_in", "w_attn_o", "w_conv_o", "w_out")
COL_SHARDED = ("ffn1_w_gate", "ffn1_w_up", "ffn2_w_gate", "ffn2_w_up", "w_in")


def _local_grads(x, target, mod, slots, ffn1_gathered, small, seq, core_idx, core_chip):
    T, D = x.shape
    B = T // seq
    mods = [mod[:, k][:, None, :] for k in range(N_MOD)]
    sh1, sc1, g1, sh2, sc2, g2, sh3, sc3, g3 = mods
    w = dict(zip(FFN1_WEIGHTS, ffn1_gathered))

    (h1, a1, u1, f1, x1), outs = _ffn_fwd(
        x, small["norm_ffn1_g"], sc1, sh1, g1, w["ffn1_w_gate"], w["ffn1_w_up"], w["ffn1_w_down"], seq, "ffn1_fwd",
        comm=_GatherComm([slots[n] for n in MIX_WEIGHTS]))
    w["w_in"] = outs[0]
    w_ao, w_co, w_o = [t.reshape(D, D) for t in outs[1:]]
    w_in_cols = _kernel_row_order(w["w_in"].reshape(IN_WIDTH, D))
    (h2, proj), _ = _in_proj(x1, small["norm_mix_g"], sc2, sh2, w_in_cols, seq)
    (o, lse), ffn2_gathered = _attn_fwd(proj, small["attn_sinks"], B, seq,
                                        comm=_GatherComm([slots[n] for n in FFN2_WEIGHTS]))
    w.update(zip(FFN2_WEIGHTS, ffn2_gathered))
    (ydw, z), _ = _conv_fwd(proj, small["conv_w_dw"], small["conv_b_dw"], small["conv_ln_g"], small["conv_ln_b"],
                            B, seq)
    ya, yc, merged, mo, x2 = _merge(o, z, proj, w_ao, w_co, w_o, x1, g2, seq)
    (h3, a3, u3, f3, x3), _ = _ffn_fwd(x2, small["norm_ffn2_g"], sc3, sh3, g3, w["ffn2_w_gate"], w["ffn2_w_up"],
                                       w["ffn2_w_down"], seq, "ffn2_fwd")
    dx3, loss_parts, d_final_g = _final_loss(x3, small["final_norm_g"], target)

    red = _Reducer(core_idx, core_chip)

    def weight_grad(name, budget_us, a, a_spec, b, b_spec, rows, cols):
        red.add(name, red.run(_wgrad, budget_us, a, a_spec, b, b_spec, rows, cols, T, "dw_" + name))

    def ffn_backward(prefix, dw_budget_us, dxo, xin, h, a, u, f, gn, sc, gate, before_weight_grads=None):
        da, du, s, df, dx, dgate, dsc, dsh, dgn = red.run(
            _ffn_bwd, 170, dxo, xin, f, a, u, gn, sc, gate, w[prefix + "_w_gate"], w[prefix + "_w_up"],
            w[prefix + "_w_down"], seq, prefix + "_bwd")
        if before_weight_grads is not None:
            before_weight_grads(dgate, dsc, dsh, dgn)
        weight_grad(prefix + "_w_down", dw_budget_us, s, _spec_chip_major(FF_SHARD), df, _spec_rows(D), FF_SHARD, D)
        weight_grad(prefix + "_w_gate", dw_budget_us, da, _spec_chip_major(FF_SHARD), h, _spec_rows(D), FF_SHARD, D)
        weight_grad(prefix + "_w_up", dw_budget_us, du, _spec_chip_major(FF_SHARD), h, _spec_rows(D), FF_SHARD, D)
        return dx, dgate, dsc, dsh, dgn

    dx2, dg3, dsc3, dsh3, d_gn3 = ffn_backward("ffn2", None, dx3, x2, h3, a3, u3, f3, small["norm_ffn2_g"], sc3, g3)

    dmo, dya, dyc, dga, dgc, do, dz, dg2 = red.run(_merge_bwd, None, dx2, mo, g2, proj, ya, yc, w_o, w_ao, w_co, seq)
    shard = D // N_CHIP
    weight_grad("w_out", None, merged, _spec_col_block(shard), dmo, _spec_rows(D), shard, D)
    weight_grad("w_attn_o", None, o, _spec_col_block(shard), dya, _spec_rows(D), shard, D)
    weight_grad("w_conv_o", None, z, _spec_col_block(shard), dyc, _spec_rows(D), shard, D)
    dq, dkp, dko, dvp, dvo, dsink_steps = red.run(_attn_bwd, 100, proj, small["attn_sinks"], o, do, lse, B, seq)
    dca, dcb, d_conv_w, d_conv_b, d_ln_g, d_ln_b = red.run(
        _conv_bwd, 165, proj, dz, ydw, small["conv_w_dw"], small["conv_ln_g"], small["conv_ln_b"], B, seq)

    def band_sum(own, prev):
        prev = prev.reshape(B, seq // BLOCK, BLOCK, 2 * HEAD_DIM)
        moved = jnp.concatenate([prev[:, 1:], jnp.zeros_like(prev[:, :1])], axis=1)
        return (own + moved.reshape(T, 2 * HEAD_DIM)).astype(BF16)

    dkv = jnp.concatenate([band_sum(dko, dkp), band_sum(dvo, dvp)], axis=1)
    g32, g16 = lax.empty((IN_WIDTH, D), F32), lax.empty((IN_WIDTH, D), BF16)
    row_of = {"q": 0, "kv": D, "conv_a": D + 4 * HEAD_DIM, "conv_b": 2 * D + 4 * HEAD_DIM,
              "gate_a": 3 * D + 4 * HEAD_DIM, "gate_c": 4 * D + 4 * HEAD_DIM}
    for tag, piece in (("q", dq), ("kv", dkv), ("conv_a", dca), ("conv_b", dcb), ("gate_a", dga), ("gate_c", dgc)):
        g32, g16 = _wgrad_rows(piece, h2, g32, g16, row_of[tag], "dw_w_in_" + tag)
    red.add("w_in", tuple(g.reshape(N_CHIP, 2, IN_SHARD // 2, D) for g in (g32, g16)))
    dx1, dsc2, dsh2, d_gn2 = red.run(_in_proj_bwd, 90, (dq, dca, dcb, dga, dgc, dkv), w_in_cols, x1,
                                     small["norm_mix_g"], sc2, dx2, seq)

    def gather_small_grads(dg1, dsc1, dsh1, d_gn1):
        dmod = jnp.concatenate([dsh1, dsc1, dg1, dsh2, dsc2, dg2, dsh3, dsc3, dg3], axis=1)
        d_sinks = jnp.sum(dsink_steps, axis=0)
        vec_grads = {"norm_ffn1_g": d_gn1, "norm_mix_g": d_gn2, "conv_b_dw": d_conv_b, "conv_ln_g": d_ln_g,
                     "conv_ln_b": d_ln_b, "norm_ffn2_g": d_gn3, "final_norm_g": d_final_g}
        block = jnp.zeros((SMALL_ROWS, D), F32)
        block = block.at[ROW_DMOD0:ROW_DMOD0 + N_MOD].set(dmod[0]).at[ROW_DMOD1:ROW_DMOD1 + N_MOD].set(dmod[1])
        block = block.at[ROW_VEC:ROW_VEC + len(VECTORS)].set(jnp.concatenate([vec_grads[n] for n in VECTORS], axis=0))
        block = block.at[ROW_SINK, :2 * HEAD_DIM].set(d_sinks[0])
        block = block.at[ROW_CONVW:ROW_CONVW + CONV_WIDTH].set(d_conv_w[:CONV_WIDTH])
        red.extra = _Gather8Comm(block)

    dx0, _, _, _, _ = ffn_backward("ffn1", 38, dx1, x, h1, a1, u1, f1, small["norm_ffn1_g"], sc1, g1,
                                   before_weight_grads=gather_small_grads)
    return loss_parts, dx0, red, red.extra_out[0]


def kernel(x, c, w_ada, b_ada, norm_ffn1_g, ffn1_w_gate, ffn1_w_up, ffn1_w_down, norm_mix_g, w_in, attn_sinks, w_attn_o, conv_w_dw, conv_b_dw, conv_ln_g, conv_ln_b, w_conv_o, w_out, norm_ffn2_g, ffn2_w_gate, ffn2_w_up, ffn2_w_down, final_norm_g, loss_target, m_w_ada, m_b_ada, m_norm_ffn1_g, m_ffn1_w_gate, m_ffn1_w_up, m_ffn1_w_down, m_norm_mix_g, m_w_in, m_attn_sinks, m_w_attn_o, m_conv_w_dw, m_conv_b_dw, m_conv_ln_g, m_conv_ln_b, m_w_conv_o, m_w_out, m_norm_ffn2_g, m_ffn2_w_gate, m_ffn2_w_up, m_ffn2_w_down, m_final_norm_g, v_w_ada, v_b_ada, v_norm_ffn1_g, v_ffn1_w_gate, v_ffn1_w_up, v_ffn1_w_down, v_norm_mix_g, v_w_in, v_attn_sinks, v_w_attn_o, v_conv_w_dw, v_conv_b_dw, v_conv_ln_g, v_conv_ln_b, v_w_conv_o, v_w_out, v_norm_ffn2_g, v_ffn2_w_gate, v_ffn2_w_up, v_ffn2_w_down, v_final_norm_g):
    args = dict(locals())
    B, seq, D = x.shape
    T = B * seq
    xi, yi, ci = _position()
    chip = 2 * xi + yi
    dev = 4 * xi + 2 * yi + ci

    def shard_2d(prefix, name):
        t = args[prefix + name][0]
        return t.T if name in COL_SHARDED else t

    big = {n: shard_2d("", n) for n in BIG_WEIGHTS}
    final_g = final_norm_g[None, :]
    vec_w = {n: (args[n] if n != "final_norm_g" else final_g) for n in VECTORS}

    core_idx = jnp.reshape(ci, (1,)).astype(jnp.int32)
    chip_idx = jnp.reshape(chip, (1,)).astype(jnp.int32)
    core_chip = jnp.stack([ci, chip]).astype(jnp.int32)
    conv_cols = D // N_CHIP
    conv_flat = jnp.pad(conv_w_dw[0].reshape(-1), (0, 8 * D - CONV_WIDTH * conv_cols)).reshape(8, D)
    first_block = jnp.concatenate([jnp.pad(c, ((0, 8 - B), (0, 0))), conv_flat], axis=0)
    slots = {n: _cast_slot(big[n], chip_idx, "cast_" + n) for n in FFN1_WEIGHTS}
    later = [n for n in BIG_WEIGHTS if n not in FFN1_WEIGHTS]
    carried = _CommList([_GatherComm([slots[n] for n in FFN1_WEIGHTS]), _Gather8Comm(first_block)])
    later_slots, carried_outs = _cast_group([big[n] for n in later], ["later_weights"], carried)
    ffn1_gathered, (first,) = carried.split_outputs(carried_outs)
    slots.update(zip(later, later_slots))
    c_all = first[:, :B].reshape(N_DEV * B, D)
    conv_taps = first[::2, 8:].reshape(N_CHIP, 8 * D)[:, :CONV_WIDTH * conv_cols]
    conv_taps = conv_taps.reshape(N_CHIP, CONV_WIDTH, conv_cols).transpose(1, 0, 2).reshape(CONV_WIDTH, D)
    conv_taps = jnp.pad(conv_taps, ((0, CONV_PAD - CONV_WIDTH), (0, 0)))

    ada_cols = w_ada.shape[2]
    b_cols = lax.dynamic_slice(b_ada, (0, chip * ada_cols), (1, ada_cols))
    mod_part = _ada_fwd(c_all, w_ada[0], b_cols).reshape(N_DEV, B, ada_cols)
    mod = _mod_exchange(mod_part).transpose(1, 0, 2).reshape(B, N_MOD, D)

    small = dict(vec_w)
    small["attn_sinks"] = attn_sinks
    small["conv_w_dw"] = conv_taps

    loss_parts, dx, red, small_all = _local_grads(
        x.reshape(T, D), loss_target.reshape(T, D), mod, slots, ffn1_gathered, small, seq, core_idx, core_chip)

    loss = lax.psum((0.5 / D) * jnp.sum(loss_parts), ("x", "y", "c"))
    grad_x = dx.reshape(B, seq, D)
    out = {}


    def pack_small(prefix):
        rows = [args[prefix + "b_ada"].reshape(N_MOD, D)]
        rows += [args[prefix + n].reshape(1, D) for n in VECTORS]
        rows += [jnp.pad(args[prefix + "attn_sinks"], ((0, 0), (0, D - N_Q_HEADS)))]
        return jnp.pad(jnp.concatenate(rows, axis=0), ((0, 24 - N_MOD - len(VECTORS) - 1), (0, 0)))

    small_sum, sg, sd, sm, sv = _small_adam(small_all, pack_small(""), pack_small("m_"), pack_small("v_"),
                                           ROW_DMOD0, ROW_DMOD1, ROW_VEC)

    def unpack_small(t):
        res = {"b_ada": t[:N_MOD].reshape(1, N_MOD * D)}
        for k, n in enumerate(VECTORS):
            res[n] = t[N_MOD + k].reshape(args[n].shape)
        res["attn_sinks"] = t[N_MOD + len(VECTORS), :N_Q_HEADS].reshape(1, N_Q_HEADS)
        return res

    unpacked = [unpack_small(t) for t in (sg, sd, sm, sv)]
    for n in ("b_ada", "attn_sinks") + VECTORS:
        out[n] = tuple(u[n] for u in unpacked)

    conv_g = lax.dynamic_slice(small_sum, (ROW_CONVW, chip * conv_cols), (CONV_WIDTH, conv_cols))
    d, mn, vn = red.run(_adam_call, None, conv_w_dw[0], conv_g, m_conv_w_dw[0], v_conv_w_dw[0], "adam_conv_w_dw")
    out["conv_w_dw"] = tuple(t[None] for t in (conv_g, d, mn, vn))

    dmod_rows = jnp.stack([small_all[:, ROW_DMOD0:ROW_DMOD0 + N_MOD], small_all[:, ROW_DMOD1:ROW_DMOD1 + N_MOD]], axis=1)
    dmod_all = dmod_rows.reshape(N_DEV * B, N_MOD * D)
    dmod_cols = lax.dynamic_slice(dmod_all, (0, chip * ada_cols), (N_DEV * B, ada_cols))
    ada_out = red.run(_ada_adam, 35, c_all.T, dmod_cols, w_ada[0], m_w_ada[0], v_w_ada[0])
    out["w_ada"] = tuple(t[None] for t in ada_out)

    def finished(n):
        while n not in red.reduced:
            red.step()
        return red.reduced[n].reshape(big[n].shape)

    def emit(n, g, d, mn, vn):
        out[n] = tuple((t.T if n in COL_SHARDED else t)[None] for t in (g, d, mn, vn))

    early = FFN2_WEIGHTS + MIX_WEIGHTS
    early_g = [finished(n) for n in early]
    early_out = red.run(_adam_group, 45, [big[n] for n in early], early_g, [shard_2d("m_", n) for n in early],
                        [shard_2d("v_", n) for n in early], "adam_early")
    for n, g, (d, mn, vn) in zip(early, early_g, early_out):
        emit(n, g, d, mn, vn)
    for n in ("ffn1_w_down", "ffn1_w_gate", "ffn1_w_up"):
        g = finished(n)
        emit(n, g, *red.run(_adam_call, None, big[n], g, shard_2d("m_", n), shard_2d("v_", n), "adam_" + n))

    order = ("w_ada", "b_ada", "norm_ffn1_g", "ffn1_w_gate", "ffn1_w_up", "ffn1_w_down", "norm_mix_g", "w_in",
             "attn_sinks", "w_attn_o", "conv_w_dw", "conv_b_dw", "conv_ln_g", "conv_ln_b", "w_conv_o", "w_out",
             "norm_ffn2_g", "ffn2_w_gate", "ffn2_w_up", "ffn2_w_down", "final_norm_g")
    return (loss, grad_x, *[out[n][0] for n in order], *[out[n][1] for n in order],
            *[out[n][2] for n in order], *[out[n][3] for n in order])
```

```python
import functools

import jax
import jax.numpy as jnp
from jax import lax
from jax.experimental import pallas as pl
from jax.experimental.pallas import tpu as pltpu

F32 = jnp.float32
BF16 = jnp.bfloat16

D_MODEL = 1024
D_FF = 2816
N_CHIP = 4
N_DEV = 8
FF_SHARD = D_FF // N_CHIP
IN_WIDTH = 5376
IN_SHARD = IN_WIDTH // N_CHIP
HEAD_DIM = 64
N_Q_HEADS = 16
N_KV_HEADS = 2
BLOCK = 128
CONV_WIDTH = 31
CONV_PAD = 32
N_MOD = 9
EPS = 1e-6
FFN_RESIDUAL = 0.5
ATTN_SCALE = HEAD_DIM ** -0.5
MASK_VALUE = -1e30

ADAM_LR = 0.001
ADAM_B1 = 0.9
ADAM_B2 = 0.999
ADAM_EPS = 1e-08
ADAM_WD = 0.01
ADAM_STEP = 10

COLB_Q, COLB_CA, COLB_CB, COLB_GA, COLB_GC = 0, 1, 2, 3, 4
COLB_K, COLB_V = 40, 41
PROJ_TILE = 768

VMEM_LIMIT = 56 * 1024 * 1024
MESH = pl.DeviceIdType.MESH
ANY = pl.BlockSpec(memory_space=pl.ANY)
VMEM_SPEC = pl.BlockSpec(memory_space=pltpu.VMEM)
SMEM_SPEC = pl.BlockSpec(memory_space=pltpu.SMEM)


def _params(n_grid):
    return pltpu.CompilerParams(dimension_semantics=("arbitrary",) * n_grid, vmem_limit_bytes=VMEM_LIMIT)


def _tile(n, pref):
    t = min(n, pref)
    while n % t:
        t //= 2
    return t


def _row_tile(rows, cap):
    for t in range(min(rows, cap) // 16 * 16, 0, -16):
        if rows % t == 0:
            return t
    return rows


def _sigmoid(v):
    return 1.0 / (1.0 + jnp.exp(-v))


def _dot_nn(a, b):
    return lax.dot_general(a, b, (((1,), (0,)), ((), ())), preferred_element_type=F32)


def _dot_nt(a, b):
    return lax.dot_general(a, b, (((1,), (1,)), ((), ())), preferred_element_type=F32)


def _dot_tn(a, b):
    return lax.dot_general(a, b, (((0,), (0,)), ((), ())), preferred_element_type=F32)


ROW_CHUNK = 16


def _for_row_chunks(n_rows, fn):
    for r in range(0, n_rows, ROW_CHUNK):
        fn(slice(r, r + ROW_CHUNK))


def _norm_mod(xv, gn, sc, sh):
    r = lax.rsqrt(jnp.mean(xv * xv, axis=-1, keepdims=True) + EPS)
    return ((xv * r) * gn) * (1.0 + sc) + sh


def _accumulate(ref, first, value):
    @pl.when(first)
    def _():
        ref[...] = value

    @pl.when(jnp.logical_not(first))
    def _():
        ref[...] += value


def _norm_mod_bwd(dh, xv, gn, sc, dxo, first_of_batch, first, dx_ref, dsc_ref, dsh_ref, dgn_ref):
    r = lax.rsqrt(jnp.mean(xv * xv, axis=-1, keepdims=True) + EPS)
    xh = xv * r
    _accumulate(dsh_ref, first_of_batch, jnp.sum(dh, axis=0, keepdims=True))
    _accumulate(dsc_ref, first_of_batch, jnp.sum(dh * (xh * gn), axis=0, keepdims=True))
    dn = dh * (1.0 + sc)
    _accumulate(dgn_ref, first, jnp.sum(dn * xh, axis=0, keepdims=True))
    dxh = dn * gn
    dx_ref[...] = dxo + r * (dxh - xh * jnp.mean(dxh * xh, axis=-1, keepdims=True))


CHIP_FLIPS = ((1, 0), (0, 1), (1, 1))


def _position():
    return lax.axis_index("x"), lax.axis_index("y"), lax.axis_index("c")


def _flip(v, f):
    return 1 - v if f else v


class _GatherComm:
    def __init__(self, bufs):
        n = len(bufs)
        self.n = n
        self.operands = list(bufs)
        self.out_shape = [jax.ShapeDtypeStruct(b.shape, b.dtype) for b in bufs]
        self.aliases = {i: i for i in range(n)}
        self.sems = [pltpu.SemaphoreType.DMA((6 * n,)), pltpu.SemaphoreType.DMA((6 * n,))]
        self.rows = [b.shape[1] // 2 for b in bufs]

    def _half(self, ref, i, which):
        return ref.at[pl.ds(which * self.rows[i], self.rows[i]), :]

    def _ici(self, cins, couts, sems, i, k, dst_chip, to):
        x, y, c = _position()
        return pltpu.make_async_remote_copy(
            src_ref=self._half(cins[i].at[2 * x + y], i, c), dst_ref=self._half(couts[i].at[dst_chip], i, c),
            send_sem=sems[0].at[3 * i + k], recv_sem=sems[1].at[3 * i + k], device_id=to, device_id_type=MESH)

    def _d2d(self, couts, sems, i, k, src_chip, which):
        x, y, c = _position()
        place = self._half(couts[i].at[src_chip], i, which)
        return pltpu.make_async_remote_copy(
            src_ref=place, dst_ref=place, send_sem=sems[0].at[3 * self.n + 3 * i + k],
            recv_sem=sems[1].at[3 * self.n + 3 * i + k], device_id=(x, y, 1 - c), device_id_type=MESH)

    def _peers(self):
        x, y, _ = _position()
        return [(_flip(x, fx), _flip(y, fy)) for fx, fy in CHIP_FLIPS]

    def start(self, cins, couts, sems):
        x, y, c = _position()
        for i in range(self.n):
            for k, (px, py) in enumerate(self._peers()):
                self._ici(cins, couts, sems, i, k, 2 * x + y, (px, py, c)).start()

    def finish(self, cins, couts, sems):
        _, _, c = _position()
        peers = self._peers()
        for i in range(self.n):
            for k, (px, py) in enumerate(peers):
                self._ici(cins, couts, sems, i, k, 2 * px + py, (px, py, c)).wait_recv()
                self._d2d(couts, sems, i, k, 2 * px + py, c).start()
        for i in range(self.n):
            for k, (px, py) in enumerate(peers):
                self._d2d(couts, sems, i, k, 2 * px + py, 1 - c).wait_recv()
        for i in range(self.n):
            for k, (px, py) in enumerate(peers):
                self._ici(cins, couts, sems, i, k, 2 * px + py, (px, py, c)).wait_send()
                self._d2d(couts, sems, i, k, 2 * px + py, c).wait_send()


class _ExchangeComm:
    def __init__(self, pairs):
        n = len(pairs)
        self.n = n
        self.operands = list(pairs)
        self.out_shape = [jax.ShapeDtypeStruct((3,) + p.shape[1:], p.dtype) for p in pairs]
        self.aliases = {}
        self.sems = [pltpu.SemaphoreType.DMA((3 * n,)), pltpu.SemaphoreType.DMA((3 * n,))]

    def _copies(self, cins, couts, sems):
        x, y, c = _position()
        peers = [(_flip(x, fx), _flip(y, fy)) for fx, fy in CHIP_FLIPS]
        return [pltpu.make_async_remote_copy(
            src_ref=cins[i].at[2 * px + py], dst_ref=couts[i].at[k], send_sem=sems[0].at[3 * i + k],
            recv_sem=sems[1].at[3 * i + k], device_id=(px, py, c), device_id_type=MESH)
            for i in range(self.n) for k, (px, py) in enumerate(peers)]

    def start(self, cins, couts, sems):
        for cp in self._copies(cins, couts, sems):
            cp.start()

    def finish(self, cins, couts, sems):
        for cp in self._copies(cins, couts, sems):
            cp.wait()


class _SwapComm:
    def __init__(self, grads16):
        n = len(grads16)
        self.n = n
        self.operands = list(grads16)
        self.out_shape = [jax.ShapeDtypeStruct(g.shape[:1] + g.shape[2:], g.dtype) for g in grads16]
        self.aliases = {}
        self.sems = [pltpu.SemaphoreType.DMA((n,)), pltpu.SemaphoreType.DMA((n,))]

    def _copies(self, cins, couts, sems):
        x, y, c = _position()
        return [pltpu.make_async_remote_copy(
            src_ref=cins[i].at[:, 1 - c], dst_ref=couts[i], send_sem=sems[0].at[i], recv_sem=sems[1].at[i],
            device_id=(x, y, 1 - c), device_id_type=MESH) for i in range(self.n)]

    def start(self, cins, couts, sems):
        for cp in self._copies(cins, couts, sems):
            cp.start()

    def finish(self, cins, couts, sems):
        for cp in self._copies(cins, couts, sems):
            cp.wait()


class _JoinComm:
    def __init__(self, halves):
        n = len(halves)
        self.n = n
        self.operands = list(halves)
        self.out_shape = [jax.ShapeDtypeStruct(h.shape, h.dtype) for h in halves]
        self.aliases = {i: i for i in range(n)}
        self.sems = [pltpu.SemaphoreType.DMA((n,)), pltpu.SemaphoreType.DMA((n,))]

    def _copy(self, cins, couts, sems, i, which):
        x, y, c = _position()
        return pltpu.make_async_remote_copy(
            src_ref=cins[i].at[which], dst_ref=couts[i].at[which], send_sem=sems[0].at[i], recv_sem=sems[1].at[i],
            device_id=(x, y, 1 - c), device_id_type=MESH)

    def start(self, cins, couts, sems):
        _, _, c = _position()
        for i in range(self.n):
            self._copy(cins, couts, sems, i, c).start()

    def finish(self, cins, couts, sems):
        _, _, c = _position()
        for i in range(self.n):
            self._copy(cins, couts, sems, i, 1 - c).wait_recv()
        for i in range(self.n):
            self._copy(cins, couts, sems, i, c).wait_send()


class _Gather8Comm:
    def __init__(self, block):
        self.operands = [block]
        self.out_shape = [jax.ShapeDtypeStruct((N_DEV,) + block.shape, block.dtype)]
        self.aliases = {}
        self.sems = [pltpu.SemaphoreType.DMA((N_DEV - 1,)), pltpu.SemaphoreType.DMA((N_DEV - 1,)),
                     pltpu.SemaphoreType.DMA]
        self.flips = [(fx, fy, fc) for fx in (0, 1) for fy in (0, 1) for fc in (0, 1) if (fx, fy, fc) != (0, 0, 0)]

    def _peers(self):
        x, y, c = _position()
        return [(_flip(x, fx), _flip(y, fy), _flip(c, fc)) for fx, fy, fc in self.flips]

    def _copy(self, cins, couts, sems, k, block, to):
        return pltpu.make_async_remote_copy(src_ref=cins[0], dst_ref=couts[0].at[block], send_sem=sems[0].at[k],
                                            recv_sem=sems[1].at[k], device_id=to, device_id_type=MESH)

    def _mine(self, cins, couts, sems):
        x, y, c = _position()
        return pltpu.make_async_copy(cins[0], couts[0].at[4 * x + 2 * y + c], sems[2])

    def start(self, cins, couts, sems):
        x, y, c = _position()
        self._mine(cins, couts, sems).start()
        for k, peer in enumerate(self._peers()):
            self._copy(cins, couts, sems, k, 4 * x + 2 * y + c, peer).start()

    def finish(self, cins, couts, sems):
        for k, (px, py, pc) in enumerate(self._peers()):
            self._copy(cins, couts, sems, k, 4 * px + 2 * py + pc, (px, py, pc)).wait_recv()
        for k, peer in enumerate(self._peers()):
            self._copy(cins, couts, sems, k, 0, peer).wait_send()
        self._mine(cins, couts, sems).wait()


class _CommList:
    def __init__(self, parts):
        self.parts = list(parts)
        self.operands = [t for p in self.parts for t in p.operands]
        self.out_shape = [t for p in self.parts for t in p.out_shape]
        self.sems = [t for p in self.parts for t in p.sems]
        self.aliases = {}
        n_in = n_out = 0
        for p in self.parts:
            self.aliases.update({n_in + i: n_out + j for i, j in p.aliases.items()})
            n_in += len(p.operands)
            n_out += len(p.out_shape)

    def _split(self, cins, couts, sems):
        pos = [0, 0, 0]
        for p in self.parts:
            sizes = (len(p.operands), len(p.out_shape), len(p.sems))
            yield p, tuple(seq[a:a + k] for seq, a, k in zip((cins, couts, sems), pos, sizes))
            pos = [a + k for a, k in zip(pos, sizes)]

    def start(self, cins, couts, sems):
        for p, refs in self._split(cins, couts, sems):
            p.start(*refs)

    def finish(self, cins, couts, sems):
        for p, refs in self._split(cins, couts, sems):
            p.finish(*refs)

    def split_outputs(self, outs):
        res, pos = [], 0
        for p in self.parts:
            res.append(outs[pos:pos + len(p.out_shape)])
            pos += len(p.out_shape)
        return res


def _call(body, *, name, grid, in_specs, out_specs, out_shape, operands, scratch_shapes=(), comm=None):
    n_grid = len(grid)
    if comm is None:
        return pl.pallas_call(
            body, name=name, grid=grid, in_specs=list(in_specs), out_specs=list(out_specs), out_shape=list(out_shape),
            scratch_shapes=list(scratch_shapes), compiler_params=_params(n_grid))(*operands), ()
    counts = (len(in_specs), len(comm.operands), len(out_specs), len(comm.out_shape), len(scratch_shapes),
              len(comm.sems))

    def fused(*refs):
        parts, pos = [], 0
        for k in counts:
            parts.append(refs[pos:pos + k])
            pos += k
        ins, cins, outs, couts, scr, sems = parts
        first = functools.reduce(jnp.logical_and, [pl.program_id(d) == 0 for d in range(n_grid)])
        last = functools.reduce(jnp.logical_and, [pl.program_id(d) == grid[d] - 1 for d in range(n_grid)])

        @pl.when(first)
        def _():
            comm.start(cins, couts, sems)

        body(*ins, *outs, *scr)

        @pl.when(last)
        def _():
            comm.finish(cins, couts, sems)

    res = pl.pallas_call(
        fused, name=name, grid=grid, in_specs=list(in_specs) + [ANY] * counts[1],
        out_specs=list(out_specs) + [ANY] * counts[3], out_shape=list(out_shape) + list(comm.out_shape),
        scratch_shapes=list(scratch_shapes) + list(comm.sems),
        input_output_aliases={counts[0] + i: counts[2] + j for i, j in comm.aliases.items()},
        compiler_params=_params(n_grid))(*operands, *comm.operands)
    return res[:counts[2]], res[counts[2]:]


def _run_comm(comm, name):
    k_in, k_out = len(comm.operands), len(comm.out_shape)

    def body(*refs):
        cins, couts, sems = refs[:k_in], refs[k_in:k_in + k_out], refs[k_in + k_out:]
        comm.start(cins, couts, sems)
        comm.finish(cins, couts, sems)

    return pl.pallas_call(
        body, name=name, in_specs=[ANY] * k_in, out_specs=[ANY] * k_out, out_shape=list(comm.out_shape),
        scratch_shapes=list(comm.sems), input_output_aliases=dict(comm.aliases))(*comm.operands)


def _ffn_fwd(x, gn, sc, sh, gate, wg, wu, wd, seq, name, comm=None):
    T, D = x.shape
    J, Fs, _ = wg.shape
    tm = _tile(seq, 1024)
    nb = seq // tm

    def body(x_ref, gn_ref, sc_ref, sh_ref, gate_ref, wg_ref, wu_ref, wd_ref,
             h_ref, a_ref, u_ref, f_ref, xo_ref, hs, acc, s16):
        j = pl.program_id(1)

        @pl.when(j == 0)
        def _():
            hb = _norm_mod(x_ref[...], gn_ref[...], sc_ref[...], sh_ref[...]).astype(BF16)
            hs[...] = hb
            h_ref[...] = hb
            acc[...] = jnp.zeros_like(acc)

        hb = hs[...]
        a_all = _dot_nt(hb, wg_ref[...])
        u_all = _dot_nt(hb, wu_ref[...])

        def swiglu_rows(rows):
            a = a_all[rows, :]
            u = u_all[rows, :]
            a_ref[rows, :] = a.astype(BF16)
            u_ref[rows, :] = u.astype(BF16)
            s16[rows, :] = ((a * _sigmoid(a)) * u).astype(BF16)

        _for_row_chunks(tm, swiglu_rows)
        acc[...] += _dot_nn(s16[...], wd_ref[...])

        @pl.when(j == J - 1)
        def _():
            f = acc[...]
            f_ref[...] = f.astype(BF16)
            xo_ref[...] = x_ref[...] + (FFN_RESIDUAL * gate_ref[...]) * f

    row = pl.BlockSpec((tm, D), lambda i, j: (i, 0))
    vec = pl.BlockSpec((1, D), lambda i, j: (0, 0))
    per_b = pl.BlockSpec((None, 1, D), lambda i, j: (i // nb, 0, 0))
    hid = pl.BlockSpec((None, tm, Fs), lambda i, j: (j, i, 0))
    return _call(
        body, name=name, grid=(T // tm, J),
        in_specs=[row, vec, per_b, per_b, per_b] + [pl.BlockSpec((None, Fs, D), lambda i, j: (j, 0, 0))] * 3,
        out_specs=[row, hid, hid, row, row],
        out_shape=[jax.ShapeDtypeStruct((T, D), BF16), jax.ShapeDtypeStruct((J, T, Fs), BF16),
                   jax.ShapeDtypeStruct((J, T, Fs), BF16), jax.ShapeDtypeStruct((T, D), BF16),
                   jax.ShapeDtypeStruct((T, D), F32)],
        scratch_shapes=[pltpu.VMEM((tm, D), BF16), pltpu.VMEM((tm, D), F32), pltpu.VMEM((tm, Fs), BF16)],
        operands=(x, gn, sc, sh, gate, wg, wu, wd), comm=comm)


def _ffn_bwd(dxo, x, f, a, u, gn, sc, gate, wg, wu, wd, seq, name, comm=None):
    T, D = x.shape
    J, Fs, _ = wg.shape
    B = T // seq
    tm = _tile(seq, 512)
    nb = seq // tm

    def body(dxo_ref, x_ref, f_ref, a_ref, u_ref, gn_ref, sc_ref, gate_ref, wg_ref, wu_ref, wd_ref,
             da_ref, du_ref, s_ref, df_ref, dx_ref, dgate_ref, dsc_ref, dsh_ref, dgn_ref, dfs, acc):
        i = pl.program_id(0)
        j = pl.program_id(1)
        first_of_batch = i % nb == 0

        @pl.when(j == 0)
        def _():
            dxo_v = dxo_ref[...]
            dfb = ((FFN_RESIDUAL * gate_ref[...]) * dxo_v).astype(BF16)
            dfs[...] = dfb
            df_ref[...] = dfb
            part = jnp.sum((FFN_RESIDUAL * f_ref[...].astype(F32)) * dxo_v, axis=0, keepdims=True)
            _accumulate(dgate_ref, first_of_batch, part)
            acc[...] = jnp.zeros_like(acc)

        ds_all = _dot_nt(dfs[...], wd_ref[...])

        def swiglu_bwd_rows(rows):
            ds = ds_all[rows, :]
            av = a_ref[rows, :].astype(F32)
            uv = u_ref[rows, :].astype(F32)
            sig = _sigmoid(av)
            sil = av * sig
            s_ref[rows, :] = (sil * uv).astype(BF16)
            da_ref[rows, :] = (ds * uv * (sig * (1.0 + av * (1.0 - sig)))).astype(BF16)
            du_ref[rows, :] = (ds * sil).astype(BF16)

        _for_row_chunks(tm, swiglu_bwd_rows)
        acc[...] += _dot_nn(da_ref[...], wg_ref[...]) + _dot_nn(du_ref[...], wu_ref[...])

        @pl.when(j == J - 1)
        def _():
            _norm_mod_bwd(acc[...], x_ref[...], gn_ref[...], sc_ref[...], dxo_ref[...],
                          first_of_batch, i == 0, dx_ref, dsc_ref, dsh_ref, dgn_ref)

    row = pl.BlockSpec((tm, D), lambda i, j: (i, 0))
    vec = pl.BlockSpec((1, D), lambda i, j: (0, 0))
    per_b = pl.BlockSpec((None, 1, D), lambda i, j: (i // nb, 0, 0))
    hid = pl.BlockSpec((None, tm, Fs), lambda i, j: (j, i, 0))
    hid_shape = jax.ShapeDtypeStruct((J, T, Fs), BF16)
    per_b_shape = jax.ShapeDtypeStruct((B, 1, D), F32)
    return _call(
        body, name=name, grid=(T // tm, J),
        in_specs=[row, row, row, hid, hid, vec, per_b, per_b]
        + [pl.BlockSpec((None, Fs, D), lambda i, j: (j, 0, 0))] * 3,
        out_specs=[hid, hid, hid, row, row, per_b, per_b, per_b, vec],
        out_shape=[hid_shape, hid_shape, hid_shape, jax.ShapeDtypeStruct((T, D), BF16),
                   jax.ShapeDtypeStruct((T, D), F32), per_b_shape, per_b_shape, per_b_shape,
                   jax.ShapeDtypeStruct((1, D), F32)],
        scratch_shapes=[pltpu.VMEM((tm, D), BF16), pltpu.VMEM((tm, D), F32)],
        operands=(dxo, x, f, a, u, gn, sc, gate, wg, wu, wd), comm=comm)


def _wgrad(a, a_spec, b, b_spec, rows, cols, n_tok, name, comm=None):
    tk = _tile(n_tok, 4096)
    nk = n_tok // tk
    half = rows // 2

    def body(a_ref, b_ref, o32_ref, o16_ref, acc):
        k = pl.program_id(1)

        @pl.when(k == 0)
        def _():
            acc[...] = jnp.zeros_like(acc)

        acc[...] += _dot_tn(a_ref[...], b_ref[...])

        @pl.when(k == nk - 1)
        def _():
            for h in range(2):
                v = acc[h * half:(h + 1) * half, :]
                o32_ref[h] = v
                o16_ref[h] = v.astype(BF16)

    out_spec = pl.BlockSpec((None, 2, half, cols), lambda j, k: (j, 0, 0, 0))
    return _call(
        body, name=name, grid=(N_CHIP, nk),
        in_specs=[a_spec(tk), b_spec(tk)],
        out_specs=[out_spec, out_spec],
        out_shape=[jax.ShapeDtypeStruct((N_CHIP, 2, half, cols), F32),
                   jax.ShapeDtypeStruct((N_CHIP, 2, half, cols), BF16)],
        scratch_shapes=[pltpu.VMEM((rows, cols), F32)],
        operands=(a, b), comm=comm)


def _spec_rows(width):
    return lambda tk: pl.BlockSpec((tk, width), lambda j, k: (k, 0))


def _spec_chip_major(width):
    return lambda tk: pl.BlockSpec((None, tk, width), lambda j, k: (j, k, 0))


def _spec_col_block(width):
    return lambda tk: pl.BlockSpec((tk, width), lambda j, k: (k, j))


def _in_proj(x, gn, sc, sh, w_in, seq, comm=None):
    T, D = x.shape
    N = w_in.shape[0]
    tm = _tile(seq, 2048)
    nb = seq // tm

    def body(x_ref, gn_ref, sc_ref, sh_ref, w_ref, h_ref, p_ref, hs):
        @pl.when(pl.program_id(1) == 0)
        def _():
            hb = _norm_mod(x_ref[...], gn_ref[...], sc_ref[...], sh_ref[...]).astype(BF16)
            hs[...] = hb
            h_ref[...] = hb

        p_ref[...] = _dot_nt(hs[...], w_ref[...]).astype(BF16)

    row = pl.BlockSpec((tm, D), lambda i, j: (i, 0))
    per_b = pl.BlockSpec((None, 1, D), lambda i, j: (i // nb, 0, 0))
    return _call(
        body, name="mix_in_proj", grid=(T // tm, N // PROJ_TILE),
        in_specs=[row, pl.BlockSpec((1, D), lambda i, j: (0, 0)), per_b, per_b,
                  pl.BlockSpec((PROJ_TILE, D), lambda i, j: (j, 0))],
        out_specs=[row, pl.BlockSpec((tm, PROJ_TILE), lambda i, j: (i, j))],
        out_shape=[jax.ShapeDtypeStruct((T, D), BF16), jax.ShapeDtypeStruct((T, N), BF16)],
        scratch_shapes=[pltpu.VMEM((tm, D), BF16)],
        operands=(x, gn, sc, sh, w_in), comm=comm)


def _attn_specs(nblk):
    def own(col):
        return lambda b, n: (b * nblk + n, col)

    def prev(col):
        return lambda b, n: (b * nblk + jnp.maximum(n - 1, 0), col)

    kv = (BLOCK, 2 * HEAD_DIM)
    return [pl.BlockSpec((BLOCK, D_MODEL), own(COLB_Q)),
            pl.BlockSpec(kv, prev(COLB_K)), pl.BlockSpec(kv, own(COLB_K)),
            pl.BlockSpec(kv, prev(COLB_V)), pl.BlockSpec(kv, own(COLB_V))]


def _band_operands(prev_ref, own_ref, lo):
    band = jnp.concatenate([prev_ref[...], own_ref[...]], axis=0).astype(F32)
    rolled = pltpu.roll(band, HEAD_DIM, 1)
    zero = jnp.zeros_like(band)
    head0 = jnp.concatenate([jnp.where(lo, band, zero), jnp.where(lo, zero, rolled)], axis=0).astype(BF16)
    head1 = jnp.concatenate([jnp.where(lo, rolled, zero), jnp.where(lo, zero, band)], axis=0).astype(BF16)
    return head0, head1


PAIRS_PER_KV = N_Q_HEADS // 2 // N_KV_HEADS
BAND = 2 * BLOCK


def _band_valid(has_prev):
    qi = lax.broadcasted_iota(jnp.int32, (PAIRS_PER_KV * BLOCK, BAND), 0) & (BLOCK - 1)
    sj = lax.broadcasted_iota(jnp.int32, (PAIRS_PER_KV * BLOCK, BAND), 1)
    rel = qi + BLOCK - sj
    return (rel >= 0) & (rel < BLOCK) & ((sj >= BLOCK) | has_prev)


def _pair_lanes(kvh, pp):
    pair = kvh * PAIRS_PER_KV + pp
    return slice(pair * 2 * HEAD_DIM, (pair + 1) * 2 * HEAD_DIM)


def _stack_pairs(ref, kvh):
    return jnp.concatenate([ref[:, _pair_lanes(kvh, pp)] for pp in range(PAIRS_PER_KV)], axis=0)


def _rows_per_pair(columns):
    return jnp.concatenate(columns, axis=0)


def _attn_fwd(proj, sinks, batch, seq, comm=None):
    T = proj.shape[0]
    nblk = seq // BLOCK

    def body(sink_ref, q_ref, kp_ref, ko_ref, vp_ref, vo_ref, o_ref, lse_ref):
        lo = lax.broadcasted_iota(jnp.int32, (1, 2 * HEAD_DIM), 1) < HEAD_DIM
        head_lane = lax.broadcasted_iota(jnp.int32, (1, N_Q_HEADS), 1)
        valid = _band_valid(pl.program_id(1) > 0)
        k_ops = _band_operands(kp_ref, ko_ref, lo)
        v_ops = _band_operands(vp_ref, vo_ref, lo)
        lse_all = jnp.zeros((BLOCK, N_Q_HEADS), F32)
        col = jnp.zeros((BLOCK, 1), F32)
        side0_row = lax.broadcasted_iota(jnp.int32, (2 * BAND, 2 * HEAD_DIM), 0) < BAND
        low_lane = lax.broadcasted_iota(jnp.int32, (2 * BAND, 2 * HEAD_DIM), 1) < HEAD_DIM
        side_ones = jnp.where(side0_row == low_lane, 1.0, 0.0).astype(BF16)
        for kvh in range(N_KV_HEADS):
            s_all = _dot_nt(_stack_pairs(q_ref, kvh), k_ops[kvh]) * ATTN_SCALE
            weights, maxes, sink_terms = [], [], []
            for side in range(2):
                heads = [2 * (kvh * PAIRS_PER_KV + pp) + side for pp in range(PAIRS_PER_KV)]
                sink = _rows_per_pair([col + sink_ref[0, h] for h in heads])
                s = jnp.where(valid, s_all[:, side * BAND:(side + 1) * BAND], MASK_VALUE)
                m = jnp.maximum(jnp.max(s, axis=-1, keepdims=True), sink)
                weights.append(jnp.where(valid, jnp.exp(s - m), 0.0).astype(BF16))
                maxes.append(m)
                sink_terms.append(jnp.exp(sink - m))
            p_all = jnp.concatenate(weights, axis=1)
            den = _dot_nn(p_all, side_ones) + jnp.where(lo, sink_terms[0], sink_terms[1])
            out = _dot_nn(p_all, v_ops[kvh]) / den
            for pp in range(PAIRS_PER_KV):
                o_ref[:, _pair_lanes(kvh, pp)] = out[pp * BLOCK:(pp + 1) * BLOCK].astype(BF16)
            for side in range(2):
                lse = maxes[side] + jnp.log(den[:, side * HEAD_DIM:side * HEAD_DIM + 1])
                for pp in range(PAIRS_PER_KV):
                    h = 2 * (kvh * PAIRS_PER_KV + pp) + side
                    lse_all = jnp.where(head_lane == h, lse[pp * BLOCK:(pp + 1) * BLOCK], lse_all)
        lse_ref[...] = lse_all

    return _call(
        body, name="attn_fwd", grid=(batch, nblk),
        in_specs=[SMEM_SPEC] + _attn_specs(nblk),
        out_specs=[pl.BlockSpec((BLOCK, D_MODEL), lambda b, n: (b * nblk + n, 0)),
                   pl.BlockSpec((BLOCK, N_Q_HEADS), lambda b, n: (b * nblk + n, 0))],
        out_shape=[jax.ShapeDtypeStruct((T, D_MODEL), BF16), jax.ShapeDtypeStruct((T, N_Q_HEADS), F32)],
        operands=(sinks, proj, proj, proj, proj, proj), comm=comm)


def _conv_u(ca, cb):
    return ca.astype(F32) * _sigmoid(cb.astype(F32))


def _conv_specs(ts, tiles_per_seq):
    per_tile = ts // CONV_PAD

    def tile(col):
        return lambda b, t: (b * tiles_per_seq + t, col)

    def before(col):
        return lambda b, t: (jnp.maximum((b * tiles_per_seq + t) * per_tile - 1, 0), col)

    return [pl.BlockSpec((ts, D_MODEL), tile(COLB_CA)), pl.BlockSpec((ts, D_MODEL), tile(COLB_CB)),
            pl.BlockSpec((CONV_PAD, D_MODEL), before(COLB_CA)), pl.BlockSpec((CONV_PAD, D_MODEL), before(COLB_CB))]


SUBLANES = 8


def _fill_upad(upad, ca_ref, cb_ref, cah_ref, cbh_ref, t):
    halo = _conv_u(cah_ref[...], cbh_ref[...])
    upad[0, 0:CONV_PAD, :] = jnp.where(t > 0, halo, jnp.zeros_like(halo))
    upad[0, CONV_PAD:, :] = _conv_u(ca_ref[...], cb_ref[...])


def _fill_shifted(pad):
    rows = pad.shape[1] - SUBLANES
    for b in range(1, SUBLANES):
        pad[b, 0:rows, :] = pad[0, b:b + rows, :]


def _shifted_rows(pad, offset, rows):
    b = offset % SUBLANES
    return pad[b, offset - b:offset - b + rows, :]


def _layernorm_stats(y):
    mu = jnp.mean(y, axis=-1, keepdims=True)
    yc = y - mu
    rstd = lax.rsqrt(jnp.mean(yc * yc, axis=-1, keepdims=True) + EPS)
    return yc * rstd, rstd


def _conv_fwd(proj, w_dw, b_dw, ln_g, ln_b, batch, seq, comm=None):
    T = proj.shape[0]
    ts = _tile(seq, 256)
    nt = seq // ts
    shift = CONV_PAD - (CONV_WIDTH - 1)

    def body(ca_ref, cb_ref, cah_ref, cbh_ref, w_ref, b_ref, g_ref, beta_ref, y_ref, z_ref, upad):
        _fill_upad(upad, ca_ref, cb_ref, cah_ref, cbh_ref, pl.program_id(1))
        _fill_shifted(upad)
        y = jnp.zeros((ts, D_MODEL), F32) + b_ref[...]
        for k in range(CONV_WIDTH):
            y = y + w_ref[k:k + 1, :] * _shifted_rows(upad, shift + k, ts)
        y_ref[...] = y
        lnh, _ = _layernorm_stats(y)
        ln = lnh * g_ref[...] + beta_ref[...]
        z_ref[...] = (ln * _sigmoid(ln)).astype(BF16)

    vec = pl.BlockSpec((1, D_MODEL), lambda b, t: (0, 0))
    row = pl.BlockSpec((ts, D_MODEL), lambda b, t: (b * nt + t, 0))
    return _call(
        body, name="conv_fwd", grid=(batch, nt),
        in_specs=_conv_specs(ts, nt) + [pl.BlockSpec((CONV_PAD, D_MODEL), lambda b, t: (0, 0)), vec, vec, vec],
        out_specs=[row, row],
        out_shape=[jax.ShapeDtypeStruct((T, D_MODEL), F32), jax.ShapeDtypeStruct((T, D_MODEL), BF16)],
        scratch_shapes=[pltpu.VMEM((SUBLANES, ts + CONV_PAD, D_MODEL), F32)],
        operands=(proj, proj, proj, proj, w_dw, b_dw, ln_g, ln_b), comm=comm)


def _merge(o, z, proj, w_ao, w_co, w_out, x, gate, seq):
    T, D = x.shape
    tm = _tile(seq, 512)
    nb = seq // tm

    def body(o_ref, z_ref, ga_ref, gc_ref, wao_ref, wco_ref, wout_ref, x_ref, gate_ref,
             ya_ref, yc_ref, mg_ref, mo_ref, xo_ref):
        ya = _dot_nn(o_ref[...], wao_ref[...])
        yc = _dot_nn(z_ref[...], wco_ref[...])
        ya_ref[...] = ya.astype(BF16)
        yc_ref[...] = yc.astype(BF16)
        merged = (_sigmoid(ga_ref[...].astype(F32)) * ya + _sigmoid(gc_ref[...].astype(F32)) * yc).astype(BF16)
        mg_ref[...] = merged
        mo = _dot_nn(merged, wout_ref[...])
        mo_ref[...] = mo.astype(BF16)
        xo_ref[...] = x_ref[...] + gate_ref[...] * mo

    row = pl.BlockSpec((tm, D), lambda i: (i, 0))
    mat = pl.BlockSpec((D, D), lambda i: (0, 0))
    act = jax.ShapeDtypeStruct((T, D), BF16)
    return pl.pallas_call(
        body, name="mix_merge", grid=(T // tm,),
        in_specs=[row, row, pl.BlockSpec((tm, D), lambda i: (i, COLB_GA)), pl.BlockSpec((tm, D), lambda i: (i, COLB_GC)),
                  mat, mat, mat, row, pl.BlockSpec((None, 1, D), lambda i: (i // nb, 0, 0))],
        out_specs=[row, row, row, row, row],
        out_shape=[act, act, act, act, jax.ShapeDtypeStruct((T, D), F32)],
        compiler_params=_params(1),
    )(o, z, proj, proj, w_ao, w_co, w_out, x, gate)


def _final_loss(x, gf, target):
    T, D = x.shape
    tm = _tile(T, 512)

    def body(x_ref, gf_ref, t_ref, dx_ref, lp_ref, dgf_ref):
        first = pl.program_id(0) == 0
        xv = x_ref[...]
        gfv = gf_ref[...]
        r = lax.rsqrt(jnp.mean(xv * xv, axis=-1, keepdims=True) + EPS)
        xh = xv * r
        err = xh * gfv - t_ref[...]
        _accumulate(lp_ref, first, jnp.sum(err * err, axis=0, keepdims=True))
        dy = err * (1.0 / D)
        _accumulate(dgf_ref, first, jnp.sum(dy * xh, axis=0, keepdims=True))
        dxh = dy * gfv
        dx_ref[...] = r * (dxh - xh * jnp.mean(dxh * xh, axis=-1, keepdims=True))

    row = pl.BlockSpec((tm, D), lambda i: (i, 0))
    vec = pl.BlockSpec((1, D), lambda i: (0, 0))
    return pl.pallas_call(
        body, name="final_loss", grid=(T // tm,),
        in_specs=[row, vec, row], out_specs=[row, vec, vec],
        out_shape=[jax.ShapeDtypeStruct((T, D), F32), jax.ShapeDtypeStruct((1, D), F32),
                   jax.ShapeDtypeStruct((1, D), F32)],
        compiler_params=_params(1),
    )(x, gf, target)


def _merge_bwd(dxo, mo, gate, proj, ya, yc, w_out, w_ao, w_co, seq, comm=None):
    T, D = dxo.shape
    B = T // seq
    tm = _tile(seq, 512)
    nb = seq // tm

    def body(dxo_ref, mo_ref, gate_ref, ga_ref, gc_ref, ya_ref, yc_ref, wout_ref, wao_ref, wco_ref,
             dmo_ref, dya_ref, dyc_ref, dga_ref, dgc_ref, do_ref, dz_ref, dgate_ref):
        dxo_v = dxo_ref[...]
        dmo = (gate_ref[...] * dxo_v).astype(BF16)
        dmo_ref[...] = dmo
        _accumulate(dgate_ref, pl.program_id(0) % nb == 0,
                    jnp.sum(mo_ref[...].astype(F32) * dxo_v, axis=0, keepdims=True))
        dm = _dot_nt(dmo, wout_ref[...])
        sa = _sigmoid(ga_ref[...].astype(F32))
        sc = _sigmoid(gc_ref[...].astype(F32))
        dya = (sa * dm).astype(BF16)
        dyc = (sc * dm).astype(BF16)
        dya_ref[...] = dya
        dyc_ref[...] = dyc
        dga_ref[...] = (dm * ya_ref[...].astype(F32) * (sa * (1.0 - sa))).astype(BF16)
        dgc_ref[...] = (dm * yc_ref[...].astype(F32) * (sc * (1.0 - sc))).astype(BF16)
        do_ref[...] = _dot_nt(dya, wao_ref[...]).astype(BF16)
        dz_ref[...] = _dot_nt(dyc, wco_ref[...]).astype(BF16)

    row = pl.BlockSpec((tm, D), lambda i: (i, 0))
    mat = pl.BlockSpec((D, D), lambda i: (0, 0))
    per_b = pl.BlockSpec((None, 1, D), lambda i: (i // nb, 0, 0))
    act = jax.ShapeDtypeStruct((T, D), BF16)
    return _call(
        body, name="mix_merge_bwd", grid=(T // tm,),
        in_specs=[row, row, per_b, pl.BlockSpec((tm, D), lambda i: (i, COLB_GA)),
                  pl.BlockSpec((tm, D), lambda i: (i, COLB_GC)), row, row, mat, mat, mat],
        out_specs=[row] * 7 + [per_b],
        out_shape=[act] * 7 + [jax.ShapeDtypeStruct((B, 1, D), F32)],
        operands=(dxo, mo, gate, proj, proj, ya, yc, w_out, w_ao, w_co), comm=comm)


def _attn_bwd(proj, sinks, o, do, lse, batch, seq, comm=None):
    T = proj.shape[0]
    nblk = seq // BLOCK
    n_steps = batch * nblk

    def body(sink_ref, q_ref, kp_ref, ko_ref, vp_ref, vo_ref, o_ref, do_ref, lse_ref,
             dq_ref, dkp_ref, dko_ref, dvp_ref, dvo_ref, dsink_ref):
        lo = lax.broadcasted_iota(jnp.int32, (1, 2 * HEAD_DIM), 1) < HEAD_DIM
        sink_lane = lax.broadcasted_iota(jnp.int32, (1, 2 * HEAD_DIM), 1)
        valid = _band_valid(pl.program_id(1) > 0)
        k_ops = _band_operands(kp_ref, ko_ref, lo)
        v_ops = _band_operands(vp_ref, vo_ref, lo)
        dsink = jnp.zeros((1, 2 * HEAD_DIM), F32)
        col = jnp.zeros((BLOCK, 1), F32)

        def fold(both):
            return (jnp.where(lo, both[:BAND], 0.0)
                    + pltpu.roll(jnp.where(lo, 0.0, both[BAND:]), HEAD_DIM, 1))

        dk_heads, dv_heads = [], []
        for kvh in range(N_KV_HEADS):
            q4 = _stack_pairs(q_ref, kvh)
            do4 = _stack_pairs(do_ref, kvh)
            dd = do4.astype(F32) * _stack_pairs(o_ref, kvh).astype(F32)
            s_all = _dot_nt(q4, k_ops[kvh]) * ATTN_SCALE
            dp_all = _dot_nt(do4, v_ops[kvh])
            ds_sides, p_sides = [], []
            for side in range(2):
                heads = [2 * (kvh * PAIRS_PER_KV + pp) + side for pp in range(PAIRS_PER_KV)]
                mine = lo if side == 0 else jnp.logical_not(lo)
                cols = slice(side * BAND, (side + 1) * BAND)
                sink = _rows_per_pair([col + sink_ref[0, h] for h in heads])
                lse = _rows_per_pair([lse_ref[:, h:h + 1] for h in heads])
                delta = jnp.sum(jnp.where(mine, dd, 0.0), axis=-1, keepdims=True)
                p = jnp.where(valid, jnp.exp(jnp.where(valid, s_all[:, cols], MASK_VALUE) - lse), 0.0)
                ds_sides.append((p * (dp_all[:, cols] - delta) * ATTN_SCALE).astype(BF16))
                p_sides.append(p.astype(BF16))
                sink_part = jnp.exp(sink - lse) * delta
                for pp, h in enumerate(heads):
                    dsink = dsink + jnp.where(sink_lane == h, -jnp.sum(sink_part[pp * BLOCK:(pp + 1) * BLOCK]), 0.0)
            ds_all = jnp.concatenate(ds_sides, axis=1)
            dq4 = _dot_nn(ds_all, k_ops[kvh])
            for pp in range(PAIRS_PER_KV):
                dq_ref[:, _pair_lanes(kvh, pp)] = dq4[pp * BLOCK:(pp + 1) * BLOCK].astype(BF16)
            dk_heads.append(fold(_dot_tn(ds_all, q4)))
            dv_heads.append(fold(_dot_tn(jnp.concatenate(p_sides, axis=1), do4)))
        dk = dk_heads[0] + pltpu.roll(dk_heads[1], HEAD_DIM, 1)
        dv = dv_heads[0] + pltpu.roll(dv_heads[1], HEAD_DIM, 1)
        dkp_ref[...] = dk[:BLOCK]
        dko_ref[...] = dk[BLOCK:]
        dvp_ref[...] = dv[:BLOCK]
        dvo_ref[...] = dv[BLOCK:]
        dsink_ref[...] = dsink

    def own(b, n):
        return (b * nblk + n, 0)

    row = pl.BlockSpec((BLOCK, D_MODEL), own)
    kv = pl.BlockSpec((BLOCK, 2 * HEAD_DIM), own)
    kv_shape = jax.ShapeDtypeStruct((T, 2 * HEAD_DIM), F32)
    return _call(
        body, name="attn_bwd", grid=(batch, nblk),
        in_specs=[SMEM_SPEC] + _attn_specs(nblk) + [row, row, pl.BlockSpec((BLOCK, N_Q_HEADS), own)],
        out_specs=[row, kv, kv, kv, kv, pl.BlockSpec((None, 1, 2 * HEAD_DIM), lambda b, n: (b * nblk + n, 0, 0))],
        out_shape=[jax.ShapeDtypeStruct((T, D_MODEL), BF16), kv_shape, kv_shape, kv_shape, kv_shape,
                   jax.ShapeDtypeStruct((n_steps, 1, 2 * HEAD_DIM), F32)],
        operands=(sinks, proj, proj, proj, proj, proj, o, do, lse), comm=comm)


def _conv_bwd(proj, dz, ydw, w_dw, ln_g, ln_b, batch, seq, comm=None):
    T = proj.shape[0]
    ts = _tile(seq, 256)
    nt = seq // ts
    per_tile = ts // CONV_PAD
    shift = CONV_PAD - (CONV_WIDTH - 1)

    def body(ca_ref, cb_ref, cah_ref, cbh_ref, dz_ref, dzn_ref, y_ref, yn_ref, w_ref, g_ref, beta_ref,
             dca_ref, dcb_ref, dw_ref, db_ref, dg_ref, dbeta_ref, upad, dypad):
        t = pl.program_id(1)
        first = (pl.program_id(0) == 0) & (t == 0)
        gv = g_ref[...]

        def ln_bwd(dzv, yv):
            lnh, rstd = _layernorm_stats(yv)
            ln = lnh * gv + beta_ref[...]
            sg = _sigmoid(ln)
            dln = dzv.astype(F32) * (sg * (1.0 + ln * (1.0 - sg)))
            dyh = dln * gv
            dy = rstd * (dyh - jnp.mean(dyh, axis=-1, keepdims=True)
                         - lnh * jnp.mean(dyh * lnh, axis=-1, keepdims=True))
            return dy, dln, lnh

        dy, dln, lnh = ln_bwd(dz_ref[...], y_ref[...])
        dy_next, _, _ = ln_bwd(dzn_ref[...], yn_ref[...])
        dypad[0, 0:ts, :] = dy
        dypad[0, ts:, :] = jnp.where(t < nt - 1, dy_next, jnp.zeros_like(dy_next))
        _fill_shifted(dypad)
        _fill_upad(upad, ca_ref, cb_ref, cah_ref, cbh_ref, t)
        _fill_shifted(upad)

        _accumulate(dg_ref, first, jnp.sum(dln * lnh, axis=0, keepdims=True))
        _accumulate(dbeta_ref, first, jnp.sum(dln, axis=0, keepdims=True))
        _accumulate(db_ref, first, jnp.sum(dy, axis=0, keepdims=True))

        @pl.when(first)
        def _():
            dw_ref[...] = jnp.zeros_like(dw_ref)

        du = jnp.zeros((ts, D_MODEL), F32)
        for k in range(CONV_WIDTH):
            du = du + w_ref[k:k + 1, :] * _shifted_rows(dypad, CONV_WIDTH - 1 - k, ts)
            dw_ref[k:k + 1, :] += jnp.sum(dy * _shifted_rows(upad, shift + k, ts), axis=0, keepdims=True)
        cav = ca_ref[...].astype(F32)
        sb = _sigmoid(cb_ref[...].astype(F32))
        dca_ref[...] = (du * sb).astype(BF16)
        dcb_ref[...] = (du * cav * (sb * (1.0 - sb))).astype(BF16)

    def tile(b, t):
        return (b * nt + t, 0)

    def after(b, t):
        return (jnp.minimum((b * nt + t + 1) * per_tile, T // CONV_PAD - 1), 0)

    row = pl.BlockSpec((ts, D_MODEL), tile)
    halo = pl.BlockSpec((CONV_PAD, D_MODEL), after)
    vec = pl.BlockSpec((1, D_MODEL), lambda b, t: (0, 0))
    wspec = pl.BlockSpec((CONV_PAD, D_MODEL), lambda b, t: (0, 0))
    act = jax.ShapeDtypeStruct((T, D_MODEL), BF16)
    vec_shape = jax.ShapeDtypeStruct((1, D_MODEL), F32)
    return _call(
        body, name="conv_bwd", grid=(batch, nt),
        in_specs=_conv_specs(ts, nt) + [row, halo, row, halo, wspec, vec, vec],
        out_specs=[row, row, wspec, vec, vec, vec],
        out_shape=[act, act, jax.ShapeDtypeStruct((CONV_PAD, D_MODEL), F32), vec_shape, vec_shape, vec_shape],
        scratch_shapes=[pltpu.VMEM((SUBLANES, ts + CONV_PAD, D_MODEL), F32)] * 2,
        operands=(proj, proj, proj, proj, dz, dz, ydw, ydw, w_dw, ln_g, ln_b), comm=comm)


def _in_proj_bwd(pieces, w_cols, x, gn, sc, dxo, seq, comm=None):
    T, D = x.shape
    B = T // seq
    wide, narrow = list(pieces[:-1]), pieces[-1]
    P = len(wide)
    nw = narrow.shape[1]
    tm = _tile(seq, 512)
    nb = seq // tm

    def body(*refs):
        wide_refs = refs[:P]
        kv_ref, w_ref, wkv_ref, x_ref, gn_ref, sc_ref, dxo_ref, dx_ref, dsc_ref, dsh_ref, dgn_ref, acc = refs[P:]
        i = pl.program_id(0)
        j = pl.program_id(1)

        @pl.when(j == 0)
        def _():
            acc[...] = _dot_nn(kv_ref[...], wkv_ref[...])

        for p in range(P):
            @pl.when(j == p)
            def _(p=p):
                acc[...] += _dot_nn(wide_refs[p][...], w_ref[...])

        @pl.when(j == P - 1)
        def _():
            _norm_mod_bwd(acc[...], x_ref[...], gn_ref[...], sc_ref[...], dxo_ref[...],
                          i % nb == 0, i == 0, dx_ref, dsc_ref, dsh_ref, dgn_ref)

    row = pl.BlockSpec((tm, D), lambda i, j: (i, 0))
    vec = pl.BlockSpec((1, D), lambda i, j: (0, 0))
    per_b = pl.BlockSpec((None, 1, D), lambda i, j: (i // nb, 0, 0))
    per_b_shape = jax.ShapeDtypeStruct((B, 1, D), F32)
    return _call(
        body, name="mix_in_proj_bwd", grid=(T // tm, P),
        in_specs=[row] * P + [pl.BlockSpec((tm, nw), lambda i, j: (i, 0)),
                              pl.BlockSpec((D, D), lambda i, j: (j, 0)),
                              pl.BlockSpec((nw, D), lambda i, j: (P * D // nw, 0)), row, vec, per_b, row],
        out_specs=[row, per_b, per_b, vec],
        out_shape=[jax.ShapeDtypeStruct((T, D), F32), per_b_shape, per_b_shape, jax.ShapeDtypeStruct((1, D), F32)],
        scratch_shapes=[pltpu.VMEM((tm, D), F32)],
        operands=(*wide, narrow, w_cols, w_cols, x, gn, sc, dxo), comm=comm)


def _wgrad_rows(piece, h, out32, out16, row_offset, name):
    T, n = piece.shape
    C = h.shape[1]
    tk = _tile(T, 1024)
    nk = T // tk

    def body(a_ref, b_ref, in32, in16, o32_ref, o16_ref, acc, stage16, sems):
        k = pl.program_id(0)

        @pl.when(k == 0)
        def _():
            acc[...] = jnp.zeros_like(acc)

        acc[...] += _dot_tn(a_ref[...], b_ref[...])

        @pl.when(k == nk - 1)
        def _():
            stage16[...] = acc[...].astype(BF16)
            rows = pl.ds(row_offset, n)
            copies = [pltpu.make_async_copy(acc, o32_ref.at[rows, :], sems.at[0]),
                      pltpu.make_async_copy(stage16, o16_ref.at[rows, :], sems.at[1])]
            for cp in copies:
                cp.start()
            for cp in copies:
                cp.wait()

    return pl.pallas_call(
        body, name=name, grid=(nk,),
        in_specs=[pl.BlockSpec((tk, n), lambda k: (k, 0)), pl.BlockSpec((tk, C), lambda k: (k, 0)), ANY, ANY],
        out_specs=[ANY, ANY], out_shape=[jax.ShapeDtypeStruct(out32.shape, F32), jax.ShapeDtypeStruct(out16.shape, BF16)],
        scratch_shapes=[pltpu.VMEM((n, C), F32), pltpu.VMEM((n, C), BF16), pltpu.SemaphoreType.DMA((2,))],
        input_output_aliases={2: 0, 3: 1}, compiler_params=_params(1),
    )(piece, h, out32, out16)


def _ada_fwd(c_all, w_ada, b_cols):
    nbatch, D = c_all.shape
    N = w_ada.shape[1]
    tn = _tile(N, 768)

    def body(c_ref, w_ref, b_ref, o_ref):
        cv = c_ref[...]
        act = (cv * _sigmoid(cv)).astype(BF16)
        o_ref[...] = _dot_nn(act, w_ref[...].astype(BF16)) + b_ref[...]

    return pl.pallas_call(
        body, name="ada_fwd", grid=(N // tn,),
        in_specs=[pl.BlockSpec((nbatch, D), lambda j: (0, 0)), pl.BlockSpec((D, tn), lambda j: (0, j)),
                  pl.BlockSpec((1, tn), lambda j: (0, j))],
        out_specs=pl.BlockSpec((nbatch, tn), lambda j: (0, j)),
        out_shape=jax.ShapeDtypeStruct((nbatch, N), F32),
        compiler_params=_params(1),
    )(c_all, w_ada, b_cols)


def _adamw(w, g, m, v):
    m = ADAM_B1 * m + (1.0 - ADAM_B1) * g
    v = ADAM_B2 * v + (1.0 - ADAM_B2) * (g * g)
    m_hat = m / (1.0 - ADAM_B1 ** ADAM_STEP)
    v_hat = v / (1.0 - ADAM_B2 ** ADAM_STEP)
    delta = -ADAM_LR * (m_hat / (jnp.sqrt(v_hat) + ADAM_EPS) + ADAM_WD * w)
    return delta, m, v


def _adam_call(w, g, m, v, name, comm=None):
    R, C = w.shape
    tr = _row_tile(R, 512)

    def body(w_ref, g_ref, m_ref, v_ref, d_ref, mo_ref, vo_ref):
        d, mn, vn = _adamw(w_ref[...], g_ref[...], m_ref[...], v_ref[...])
        d_ref[...] = d
        mo_ref[...] = mn
        vo_ref[...] = vn

    blk = pl.BlockSpec((tr, C), lambda i: (i, 0))
    shape = jax.ShapeDtypeStruct((R, C), F32)
    return _call(body, name=name, grid=(R // tr,), in_specs=[blk] * 4, out_specs=[blk] * 3, out_shape=[shape] * 3,
                 operands=(w, g, m, v), comm=comm)


ADAM_GROUP_STEPS = 8


def _adam_group(ws, gs, ms, vs, name, comm=None):
    n = len(ws)

    def body(*refs):
        ins, outs = refs[:4 * n], refs[4 * n:]
        for i in range(n):
            d, mn, vn = _adamw(*(r[...] for r in ins[4 * i:4 * i + 4]))
            outs[3 * i][...] = d
            outs[3 * i + 1][...] = mn
            outs[3 * i + 2][...] = vn

    operands, in_specs, out_specs, out_shape = [], [], [], []
    for w, g, m, v in zip(ws, gs, ms, vs):
        R, C = w.shape
        blk = pl.BlockSpec((R // ADAM_GROUP_STEPS, C), lambda i: (i, 0))
        operands += [w, g, m, v]
        in_specs += [blk] * 4
        out_specs += [blk] * 3
        out_shape += [jax.ShapeDtypeStruct((R, C), F32)] * 3
    outs, comm_outs = _call(body, name=name, grid=(ADAM_GROUP_STEPS,), in_specs=in_specs, out_specs=out_specs,
                            out_shape=out_shape, operands=operands, comm=comm)
    return [tuple(outs[3 * i:3 * i + 3]) for i in range(n)], comm_outs


def _ada_adam(c_act_t, dmod_cols, w, m, v, comm):
    R, C = w.shape
    nbatch = c_act_t.shape[1]
    tr = _tile(R, 128)

    def body(ct_ref, dm_ref, w_ref, m_ref, v_ref, g_ref, d_ref, mo_ref, vo_ref):
        cv = ct_ref[...]
        g = _dot_nn((cv * _sigmoid(cv)).astype(BF16), dm_ref[...].astype(BF16))
        g_ref[...] = g
        d, mn, vn = _adamw(w_ref[...], g, m_ref[...], v_ref[...])
        d_ref[...] = d
        mo_ref[...] = mn
        vo_ref[...] = vn

    blk = pl.BlockSpec((tr, C), lambda i: (i, 0))
    shape = jax.ShapeDtypeStruct((R, C), F32)
    return _call(
        body, name="ada_adam", grid=(R // tr,),
        in_specs=[pl.BlockSpec((tr, nbatch), lambda i: (i, 0)), pl.BlockSpec((nbatch, C), lambda i: (0, 0)),
                  blk, blk, blk],
        out_specs=[blk] * 4, out_shape=[shape] * 4,
        operands=(c_act_t, dmod_cols, w, m, v), comm=comm)


def _small_adam(gathered, w, m, v, rows_b0, rows_b1, rows_vec):
    _, P, D = gathered.shape
    R = w.shape[0]

    def body(ga_ref, w_ref, m_ref, v_ref, sum_ref, g_ref, d_ref, mo_ref, vo_ref):
        total = ga_ref[0]
        for dev in range(1, N_DEV):
            total = total + ga_ref[dev]
        sum_ref[...] = total
        g_ref[...] = jnp.zeros_like(g_ref)
        g_ref[0:N_MOD, :] = (sum_ref[rows_b0:rows_b0 + N_MOD, :] + sum_ref[rows_b1:rows_b1 + N_MOD, :])
        g_ref[N_MOD:N_MOD + 8, :] = sum_ref[rows_vec:rows_vec + 8, :]
        d, mn, vn = _adamw(w_ref[...], g_ref[...], m_ref[...], v_ref[...])
        d_ref[...] = d
        mo_ref[...] = mn
        vo_ref[...] = vn

    shape = jax.ShapeDtypeStruct((R, D), F32)
    return pl.pallas_call(
        body, name="small_adam",
        in_specs=[VMEM_SPEC] * 4, out_specs=[VMEM_SPEC] * 5,
        out_shape=[jax.ShapeDtypeStruct((P, D), F32), shape, shape, shape, shape],
        compiler_params=pltpu.CompilerParams(vmem_limit_bytes=VMEM_LIMIT),
    )(gathered, w, m, v)


def _mod_exchange(part):
    _, A, W = part.shape

    def body(p_ref, out_ref, send_sems, recv_sems, local_sem):
        x, y, c = _position()
        me = 4 * x + 2 * y + c
        chip = 2 * x + y
        mine = pltpu.make_async_copy(p_ref.at[me], out_ref.at[chip], local_sem)
        mine.start()
        peers = [(_flip(x, fx), _flip(y, fy)) for fx, fy in CHIP_FLIPS]
        sends = []
        for k, (px, py) in enumerate(peers):
            sends.append(pltpu.make_async_remote_copy(
                src_ref=p_ref.at[4 * px + 2 * py + c], dst_ref=out_ref.at[chip], send_sem=send_sems.at[k],
                recv_sem=recv_sems.at[k], device_id=(px, py, c), device_id_type=MESH))
        for cp in sends:
            cp.start()
        for k, (px, py) in enumerate(peers):
            pltpu.make_async_remote_copy(
                src_ref=p_ref.at[me], dst_ref=out_ref.at[2 * px + py], send_sem=send_sems.at[k],
                recv_sem=recv_sems.at[k], device_id=(px, py, c), device_id_type=MESH).wait_recv()
        for cp in sends:
            cp.wait_send()
        mine.wait()

    return pl.pallas_call(
        body, name="mod_exchange", in_specs=[VMEM_SPEC], out_specs=VMEM_SPEC,
        out_shape=jax.ShapeDtypeStruct((N_CHIP, A, W), part.dtype),
        scratch_shapes=[pltpu.SemaphoreType.DMA((3,)), pltpu.SemaphoreType.DMA((3,)), pltpu.SemaphoreType.DMA],
    )(part)


KV_ROWS = 4 * HEAD_DIM


def _kernel_row_order(w_in_t):
    R, C = w_in_t.shape
    n_blocks = R // KV_ROWS
    q_blocks = D_MODEL // KV_ROWS

    def source(t):
        return jnp.where(t < q_blocks, t, jnp.where(t < n_blocks - 1, t + 1, q_blocks))

    def body(w_ref, o_ref):
        o_ref[...] = w_ref[...]

    return pl.pallas_call(
        body, name="w_in_row_order", grid=(n_blocks,),
        in_specs=[pl.BlockSpec((KV_ROWS, C), lambda t: (source(t), 0))],
        out_specs=pl.BlockSpec((KV_ROWS, C), lambda t: (t, 0)),
        out_shape=jax.ShapeDtypeStruct((R, C), w_in_t.dtype), compiler_params=_params(1),
    )(w_in_t)


def _cast_group(ws, names, comm):
    n = len(ws)
    steps = 4

    def body(*refs):
        w_refs, out_refs, stage, sem = refs[:n], refs[n:2 * n], refs[2 * n:3 * n], refs[3 * n]
        x, y, _ = _position()
        step = pl.program_id(0)
        copies = []
        for i in range(n):
            rows = ws[i].shape[0] // steps
            stage[i][...] = w_refs[i][...].astype(BF16)
            copies.append(pltpu.make_async_copy(
                stage[i], out_refs[i].at[2 * x + y, pl.ds(step * rows, rows), :], sem.at[i]))
        for cp in copies:
            cp.start()
        for cp in copies:
            cp.wait()

    outs, comm_outs = _call(
        body, name="cast_" + "_".join(names), grid=(steps,),
        in_specs=[pl.BlockSpec((w.shape[0] // steps, w.shape[1]), lambda i: (i, 0)) for w in ws],
        out_specs=[ANY] * n, out_shape=[jax.ShapeDtypeStruct((N_CHIP,) + w.shape, BF16) for w in ws],
        scratch_shapes=[pltpu.VMEM((w.shape[0] // steps, w.shape[1]), BF16) for w in ws]
        + [pltpu.SemaphoreType.DMA((n,))],
        operands=ws, comm=comm)
    return outs, comm_outs


def _cast_slot(w, chip_idx, name):
    R, C = w.shape
    tr = _row_tile(R, 512)

    def body(chip_ref, w_ref, o_ref):
        o_ref[...] = w_ref[...].astype(BF16)

    return pl.pallas_call(
        body, name=name,
        grid_spec=pltpu.PrefetchScalarGridSpec(
            num_scalar_prefetch=1, grid=(R // tr,),
            in_specs=[pl.BlockSpec((tr, C), lambda i, chip_ref: (i, 0))],
            out_specs=pl.BlockSpec((None, tr, C), lambda i, chip_ref: (chip_ref[0], i, 0))),
        out_shape=jax.ShapeDtypeStruct((N_CHIP, R, C), BF16),
        compiler_params=_params(1),
    )(chip_idx, w)


def _pair_sum(g32, recv, core, name):
    J, _, r, C = g32.shape

    def body(core_ref, g_ref, r_ref, o_ref):
        o_ref[...] = (g_ref[...] + r_ref[...].astype(F32)).astype(BF16)

    return pl.pallas_call(
        body, name=name,
        grid_spec=pltpu.PrefetchScalarGridSpec(
            num_scalar_prefetch=1, grid=(J,),
            in_specs=[pl.BlockSpec((None, None, r, C), lambda j, core_ref: (j, core_ref[0], 0, 0)),
                      pl.BlockSpec((None, r, C), lambda j, core_ref: (j, 0, 0))],
            out_specs=pl.BlockSpec((None, r, C), lambda j, core_ref: (j, 0, 0))),
        out_shape=jax.ShapeDtypeStruct((J, r, C), BF16),
        compiler_params=_params(1),
    )(core, g32, recv)


def _chip_sum(g32, recv_sib, recv_chips, core_chip, name):
    J, _, r, C = g32.shape

    def body(idx_ref, g_ref, s_ref, o_ref_in, o_ref):
        total = g_ref[...] + s_ref[...].astype(F32)
        for k in range(3):
            total = total + o_ref_in[k].astype(F32)
        o_ref[...] = total

    return pl.pallas_call(
        body, name=name,
        grid_spec=pltpu.PrefetchScalarGridSpec(
            num_scalar_prefetch=1, grid=(1,),
            in_specs=[pl.BlockSpec((None, None, r, C), lambda i, idx: (idx[1], idx[0], 0, 0)),
                      pl.BlockSpec((None, r, C), lambda i, idx: (idx[1], 0, 0)),
                      pl.BlockSpec((3, r, C), lambda i, idx: (0, 0, 0))],
            out_specs=pl.BlockSpec((None, r, C), lambda i, idx: (idx[0], 0, 0))),
        out_shape=jax.ShapeDtypeStruct((2, r, C), F32),
        compiler_params=_params(1),
    )(core_chip, g32, recv_sib, recv_chips)


ICI_US_PER_ELEMENT = 4.6e-5


class _Reducer:
    def __init__(self, core_idx, core_chip):
        self.core_idx, self.core_chip = core_idx, core_chip
        self.grads, self.halves, self.reduced = {}, {}, {}
        self.ready_swap, self.ready_exchange, self.ready_join = [], [], []
        self.inflight, self.current = ([], [], [], None), None
        self.flushes = 0
        self.extra, self.extra_out = None, None

    def add(self, name, grad_pair):
        self.grads[name] = grad_pair
        self.ready_swap.append(name)

    def comm(self, budget_us):
        swaps, self.ready_swap = self.ready_swap, []
        joins, self.ready_join = self.ready_join, []
        exchanges, waiting = [], []
        for item in self.ready_exchange:
            cost = ICI_US_PER_ELEMENT * 2 * item[2].shape[1] * item[2].shape[2]
            if cost <= budget_us:
                exchanges.append(item)
                budget_us -= cost
            else:
                waiting.append(item)
        self.ready_exchange = waiting
        parts = []
        if swaps:
            parts.append(_SwapComm([self.grads[n][1] for n in swaps]))
        if exchanges:
            parts.append(_ExchangeComm([pair for _, _, pair in exchanges]))
        if joins:
            parts.append(_JoinComm([self.halves[n] for n in joins]))
        extra, self.extra = self.extra, None
        if extra is not None:
            parts.append(extra)
        self.inflight = (swaps, exchanges, joins, extra)
        self.current = _CommList(parts) if parts else None
        return self.current

    def done(self, comm_outs):
        if self.current is None:
            return
        swaps, exchanges, joins, extra = self.inflight
        outs = iter(self.current.split_outputs(list(comm_outs)))
        if swaps:
            for n, recv in zip(swaps, next(outs)):
                pair = _pair_sum(self.grads[n][0], recv, self.core_idx, "pair_sum_" + n)
                self.ready_exchange.append((n, recv, pair))
        if exchanges:
            for (n, recv, _), chips in zip(exchanges, next(outs)):
                self.halves[n] = _chip_sum(self.grads[n][0], recv, chips, self.core_chip, "chip_sum_" + n)
                self.ready_join.append(n)
        if joins:
            self.reduced.update(zip(joins, next(outs)))
        if extra is not None:
            self.extra_out = next(outs)
        self.current = None

    def run(self, kernel, budget_us, *args, **kwargs):
        if budget_us is None:
            return kernel(*args, comm=None, **kwargs)[0]
        outs, comm_outs = kernel(*args, comm=self.comm(budget_us), **kwargs)
        self.done(comm_outs)
        return outs

    def step(self):
        comm = self.comm(float("inf"))
        self.flushes += 1
        self.done(_run_comm(comm, "grad_reduce_tail_%d" % self.flushes))


BIG_WEIGHTS = ("ffn1_w_gate", "ffn1_w_up", "ffn1_w_down", "w_in", "w_attn_o", "w_conv_o", "w_out",
               "ffn2_w_gate", "ffn2_w_up", "ffn2_w_down")
VECTORS = ("norm_ffn1_g", "norm_mix_g", "conv_b_dw", "conv_ln_g", "conv_ln_b", "norm_ffn2_g", "final_norm_g")
ROW_DMOD0, ROW_DMOD1, ROW_VEC, ROW_SINK, ROW_CONVW, SMALL_ROWS = 0, 16, 33, 40, 41, 72


FFN1_WEIGHTS = ("ffn1_w_gate", "ffn1_w_up", "ffn1_w_down")
FFN2_WEIGHTS = ("ffn2_w_gate", "ffn2_w_up", "ffn2_w_down")
MIX_WEIGHTS = ("w_in", "w_attn_o", "w_conv_o", "w_out")
COL_SHARDED = ("ffn1_w_gate", "ffn1_w_up", "ffn2_w_gate", "ffn2_w_up", "w_in")


def _local_grads(x, target, mod, slots, ffn1_gathered, small, seq, core_idx, core_chip):
    T, D = x.shape
    B = T // seq
    mods = [mod[:, k][:, None, :] for k in range(N_MOD)]
    sh1, sc1, g1, sh2, sc2, g2, sh3, sc3, g3 = mods
    w = dict(zip(FFN1_WEIGHTS, ffn1_gathered))

    (h1, a1, u1, f1, x1), outs = _ffn_fwd(
        x, small["norm_ffn1_g"], sc1, sh1, g1, w["ffn1_w_gate"], w["ffn1_w_up"], w["ffn1_w_down"], seq, "ffn1_fwd",
        comm=_GatherComm([slots[n] for n in MIX_WEIGHTS]))
    w["w_in"] = outs[0]
    w_ao, w_co, w_o = [t.reshape(D, D) for t in outs[1:]]
    w_in_cols = _kernel_row_order(w["w_in"].reshape(IN_WIDTH, D))
    (h2, proj), _ = _in_proj(x1, small["norm_mix_g"], sc2, sh2, w_in_cols, seq)
    (o, lse), ffn2_gathered = _attn_fwd(proj, small["attn_sinks"], B, seq,
                                        comm=_GatherComm([slots[n] for n in FFN2_WEIGHTS]))
    w.update(zip(FFN2_WEIGHTS, ffn2_gathered))
    (ydw, z), _ = _conv_fwd(proj, small["conv_w_dw"], small["conv_b_dw"], small["conv_ln_g"], small["conv_ln_b"],
                            B, seq)
    ya, yc, merged, mo, x2 = _merge(o, z, proj, w_ao, w_co, w_o, x1, g2, seq)
    (h3, a3, u3, f3, x3), _ = _ffn_fwd(x2, small["norm_ffn2_g"], sc3, sh3, g3, w["ffn2_w_gate"], w["ffn2_w_up"],
                                       w["ffn2_w_down"], seq, "ffn2_fwd")
    dx3, loss_parts, d_final_g = _final_loss(x3, small["final_norm_g"], target)

    red = _Reducer(core_idx, core_chip)

    def weight_grad(name, budget_us, a, a_spec, b, b_spec, rows, cols):
        red.add(name, red.run(_wgrad, budget_us, a, a_spec, b, b_spec, rows, cols, T, "dw_" + name))

    def ffn_backward(prefix, dw_budget_us, dxo, xin, h, a, u, f, gn, sc, gate, before_weight_grads=None):
        da, du, s, df, dx, dgate, dsc, dsh, dgn = red.run(
            _ffn_bwd, 170, dxo, xin, f, a, u, gn, sc, gate, w[prefix + "_w_gate"], w[prefix + "_w_up"],
            w[prefix + "_w_down"], seq, prefix + "_bwd")
        if before_weight_grads is not None:
            before_weight_grads(dgate, dsc, dsh, dgn)
        weight_grad(prefix + "_w_down", dw_budget_us, s, _spec_chip_major(FF_SHARD), df, _spec_rows(D), FF_SHARD, D)
        weight_grad(prefix + "_w_gate", dw_budget_us, da, _spec_chip_major(FF_SHARD), h, _spec_rows(D), FF_SHARD, D)
        weight_grad(prefix + "_w_up", dw_budget_us, du, _spec_chip_major(FF_SHARD), h, _spec_rows(D), FF_SHARD, D)
        return dx, dgate, dsc, dsh, dgn

    dx2, dg3, dsc3, dsh3, d_gn3 = ffn_backward("ffn2", None, dx3, x2, h3, a3, u3, f3, small["norm_ffn2_g"], sc3, g3)

    dmo, dya, dyc, dga, dgc, do, dz, dg2 = red.run(_merge_bwd, None, dx2, mo, g2, proj, ya, yc, w_o, w_ao, w_co, seq)
    shard = D // N_CHIP
    weight_grad("w_out", None, merged, _spec_col_block(shard), dmo, _spec_rows(D), shard, D)
    weight_grad("w_attn_o", None, o, _spec_col_block(shard), dya, _spec_rows(D), shard, D)
    weight_grad("w_conv_o", None, z, _spec_col_block(shard), dyc, _spec_rows(D), shard, D)
    dq, dkp, dko, dvp, dvo, dsink_steps = red.run(_attn_bwd, None, proj, small["attn_sinks"], o, do, lse, B, seq)
    dca, dcb, d_conv_w, d_conv_b, d_ln_g, d_ln_b = red.run(
        _conv_bwd, 165, proj, dz, ydw, small["conv_w_dw"], small["conv_ln_g"], small["conv_ln_b"], B, seq)

    def band_sum(own, prev):
        prev = prev.reshape(B, seq // BLOCK, BLOCK, 2 * HEAD_DIM)
        moved = jnp.concatenate([prev[:, 1:], jnp.zeros_like(prev[:, :1])], axis=1)
        return (own + moved.reshape(T, 2 * HEAD_DIM)).astype(BF16)

    dkv = jnp.concatenate([band_sum(dko, dkp), band_sum(dvo, dvp)], axis=1)
    g32, g16 = lax.empty((IN_WIDTH, D), F32), lax.empty((IN_WIDTH, D), BF16)
    row_of = {"q": 0, "kv": D, "conv_a": D + 4 * HEAD_DIM, "conv_b": 2 * D + 4 * HEAD_DIM,
              "gate_a": 3 * D + 4 * HEAD_DIM, "gate_c": 4 * D + 4 * HEAD_DIM}
    for tag, piece in (("q", dq), ("kv", dkv), ("conv_a", dca), ("conv_b", dcb), ("gate_a", dga), ("gate_c", dgc)):
        g32, g16 = _wgrad_rows(piece, h2, g32, g16, row_of[tag], "dw_w_in_" + tag)
    red.add("w_in", tuple(g.reshape(N_CHIP, 2, IN_SHARD // 2, D) for g in (g32, g16)))
    dx1, dsc2, dsh2, d_gn2 = red.run(_in_proj_bwd, 90, (dq, dca, dcb, dga, dgc, dkv), w_in_cols, x1,
                                     small["norm_mix_g"], sc2, dx2, seq)

    def gather_small_grads(dg1, dsc1, dsh1, d_gn1):
        dmod = jnp.concatenate([dsh1, dsc1, dg1, dsh2, dsc2, dg2, dsh3, dsc3, dg3], axis=1)
        d_sinks = jnp.sum(dsink_steps, axis=0)
        vec_grads = {"norm_ffn1_g": d_gn1, "norm_mix_g": d_gn2, "conv_b_dw": d_conv_b, "conv_ln_g": d_ln_g,
                     "conv_ln_b": d_ln_b, "norm_ffn2_g": d_gn3, "final_norm_g": d_final_g}
        block = jnp.zeros((SMALL_ROWS, D), F32)
        block = block.at[ROW_DMOD0:ROW_DMOD0 + N_MOD].set(dmod[0]).at[ROW_DMOD1:ROW_DMOD1 + N_MOD].set(dmod[1])
        block = block.at[ROW_VEC:ROW_VEC + len(VECTORS)].set(jnp.concatenate([vec_grads[n] for n in VECTORS], axis=0))
        block = block.at[ROW_SINK, :2 * HEAD_DIM].set(d_sinks[0])
        block = block.at[ROW_CONVW:ROW_CONVW + CONV_WIDTH].set(d_conv_w[:CONV_WIDTH])
        red.extra = _Gather8Comm(block)

    dx0, _, _, _, _ = ffn_backward("ffn1", 38, dx1, x, h1, a1, u1, f1, small["norm_ffn1_g"], sc1, g1,
                                   before_weight_grads=gather_small_grads)
    return loss_parts, dx0, red, red.extra_out[0]


def kernel(x, c, w_ada, b_ada, norm_ffn1_g, ffn1_w_gate, ffn1_w_up, ffn1_w_down, norm_mix_g, w_in, attn_sinks, w_attn_o, conv_w_dw, conv_b_dw, conv_ln_g, conv_ln_b, w_conv_o, w_out, norm_ffn2_g, ffn2_w_gate, ffn2_w_up, ffn2_w_down, final_norm_g, loss_target, m_w_ada, m_b_ada, m_norm_ffn1_g, m_ffn1_w_gate, m_ffn1_w_up, m_ffn1_w_down, m_norm_mix_g, m_w_in, m_attn_sinks, m_w_attn_o, m_conv_w_dw, m_conv_b_dw, m_conv_ln_g, m_conv_ln_b, m_w_conv_o, m_w_out, m_norm_ffn2_g, m_ffn2_w_gate, m_ffn2_w_up, m_ffn2_w_down, m_final_norm_g, v_w_ada, v_b_ada, v_norm_ffn1_g, v_ffn1_w_gate, v_ffn1_w_up, v_ffn1_w_down, v_norm_mix_g, v_w_in, v_attn_sinks, v_w_attn_o, v_conv_w_dw, v_conv_b_dw, v_conv_ln_g, v_conv_ln_b, v_w_conv_o, v_w_out, v_norm_ffn2_g, v_ffn2_w_gate, v_ffn2_w_up, v_ffn2_w_down, v_final_norm_g):
    args = dict(locals())
    B, seq, D = x.shape
    T = B * seq
    xi, yi, ci = _position()
    chip = 2 * xi + yi
    dev = 4 * xi + 2 * yi + ci

    def shard_2d(prefix, name):
        t = args[prefix + name][0]
        return t.T if name in COL_SHARDED else t

    big = {n: shard_2d("", n) for n in BIG_WEIGHTS}
    final_g = final_norm_g[None, :]
    vec_w = {n: (args[n] if n != "final_norm_g" else final_g) for n in VECTORS}

    core_idx = jnp.reshape(ci, (1,)).astype(jnp.int32)
    chip_idx = jnp.reshape(chip, (1,)).astype(jnp.int32)
    core_chip = jnp.stack([ci, chip]).astype(jnp.int32)
    conv_cols = D // N_CHIP
    conv_flat = jnp.pad(conv_w_dw[0].reshape(-1), (0, 8 * D - CONV_WIDTH * conv_cols)).reshape(8, D)
    first_block = jnp.concatenate([jnp.pad(c, ((0, 8 - B), (0, 0))), conv_flat], axis=0)
    slots = {n: _cast_slot(big[n], chip_idx, "cast_" + n) for n in FFN1_WEIGHTS}
    later = [n for n in BIG_WEIGHTS if n not in FFN1_WEIGHTS]
    carried = _CommList([_GatherComm([slots[n] for n in FFN1_WEIGHTS]), _Gather8Comm(first_block)])
    later_slots, carried_outs = _cast_group([big[n] for n in later], ["later_weights"], carried)
    ffn1_gathered, (first,) = carried.split_outputs(carried_outs)
    slots.update(zip(later, later_slots))
    c_all = first[:, :B].reshape(N_DEV * B, D)
    conv_taps = first[::2, 8:].reshape(N_CHIP, 8 * D)[:, :CONV_WIDTH * conv_cols]
    conv_taps = conv_taps.reshape(N_CHIP, CONV_WIDTH, conv_cols).transpose(1, 0, 2).reshape(CONV_WIDTH, D)
    conv_taps = jnp.pad(conv_taps, ((0, CONV_PAD - CONV_WIDTH), (0, 0)))

    ada_cols = w_ada.shape[2]
    b_cols = lax.dynamic_slice(b_ada, (0, chip * ada_cols), (1, ada_cols))
    mod_part = _ada_fwd(c_all, w_ada[0], b_cols).reshape(N_DEV, B, ada_cols)
    mod = _mod_exchange(mod_part).transpose(1, 0, 2).reshape(B, N_MOD, D)

    small = dict(vec_w)
    small["attn_sinks"] = attn_sinks
    small["conv_w_dw"] = conv_taps

    loss_parts, dx, red, small_all = _local_grads(
        x.reshape(T, D), loss_target.reshape(T, D), mod, slots, ffn1_gathered, small, seq, core_idx, core_chip)

    loss = lax.psum((0.5 / D) * jnp.sum(loss_parts), ("x", "y", "c"))
    grad_x = dx.reshape(B, seq, D)
    out = {}


    def pack_small(prefix):
        rows = [args[prefix + "b_ada"].reshape(N_MOD, D)]
        rows += [args[prefix + n].reshape(1, D) for n in VECTORS]
        rows += [jnp.pad(args[prefix + "attn_sinks"], ((0, 0), (0, D - N_Q_HEADS)))]
        return jnp.pad(jnp.concatenate(rows, axis=0), ((0, 24 - N_MOD - len(VECTORS) - 1), (0, 0)))

    small_sum, sg, sd, sm, sv = _small_adam(small_all, pack_small(""), pack_small("m_"), pack_small("v_"),
                                           ROW_DMOD0, ROW_DMOD1, ROW_VEC)

    def unpack_small(t):
        res = {"b_ada": t[:N_MOD].reshape(1, N_MOD * D)}
        for k, n in enumerate(VECTORS):
            res[n] = t[N_MOD + k].reshape(args[n].shape)
        res["attn_sinks"] = t[N_MOD + len(VECTORS), :N_Q_HEADS].reshape(1, N_Q_HEADS)
        return res

    unpacked = [unpack_small(t) for t in (sg, sd, sm, sv)]
    for n in ("b_ada", "attn_sinks") + VECTORS:
        out[n] = tuple(u[n] for u in unpacked)

    conv_g = lax.dynamic_slice(small_sum, (ROW_CONVW, chip * conv_cols), (CONV_WIDTH, conv_cols))
    d, mn, vn = red.run(_adam_call, None, conv_w_dw[0], conv_g, m_conv_w_dw[0], v_conv_w_dw[0], "adam_conv_w_dw")
    out["conv_w_dw"] = tuple(t[None] for t in (conv_g, d, mn, vn))

    dmod_rows = jnp.stack([small_all[:, ROW_DMOD0:ROW_DMOD0 + N_MOD], small_all[:, ROW_DMOD1:ROW_DMOD1 + N_MOD]], axis=1)
    dmod_all = dmod_rows.reshape(N_DEV * B, N_MOD * D)
    dmod_cols = lax.dynamic_slice(dmod_all, (0, chip * ada_cols), (N_DEV * B, ada_cols))
    ada_out = red.run(_ada_adam, 35, c_all.T, dmod_cols, w_ada[0], m_w_ada[0], v_w_ada[0])
    out["w_ada"] = tuple(t[None] for t in ada_out)

    def finished(n):
        while n not in red.reduced:
            red.step()
        return red.reduced[n].reshape(big[n].shape)

    def emit(n, g, d, mn, vn):
        out[n] = tuple((t.T if n in COL_SHARDED else t)[None] for t in (g, d, mn, vn))

    early = FFN2_WEIGHTS + MIX_WEIGHTS
    early_g = [finished(n) for n in early]
    early_out = red.run(_adam_group, 45, [big[n] for n in early], early_g, [shard_2d("m_", n) for n in early],
                        [shard_2d("v_", n) for n in early], "adam_early")
    for n, g, (d, mn, vn) in zip(early, early_g, early_out):
        emit(n, g, d, mn, vn)
    for n in ("ffn1_w_down", "ffn1_w_gate", "ffn1_w_up"):
        g = finished(n)
        emit(n, g, *red.run(_adam_call, None, big[n], g, shard_2d("m_", n), shard_2d("v_", n), "adam_" + n))

    order = ("w_ada", "b_ada", "norm_ffn1_g", "ffn1_w_gate", "ffn1_w_up", "ffn1_w_down", "norm_mix_g", "w_in",
             "attn_sinks", "w_attn_o", "conv_w_dw", "conv_b_dw", "conv_ln_g", "conv_ln_b", "w_conv_o", "w_out",
             "norm_ffn2_g", "ffn2_w_gate", "ffn2_w_up", "ffn2_w_down", "final_norm_g")
    return (loss, grad_x, *[out[n][0] for n in order], *[out[n][1] for n in order],
            *[out[n][2] for n in order], *[out[n][3] for n in order])
```
